```python
import math
import jax, jax.numpy as jnp
from jax import lax
import numpy as np

D_MODEL = 2048
BATCH = 8
SEQ = 4096
DEPTH = 2

N_A_LAYERS = DEPTH // 2
N_B_LAYERS = DEPTH - N_A_LAYERS
HEAD_DIM = 128
N_HEADS = D_MODEL // HEAD_DIM
DILATED_GROUPS = ((128, 1), (512, 4), (2048, 16))
N_GROUPS = len(DILATED_GROUPS)
ATTN_BLOCK = 128
CONV_WIDTH = 31
D_FF = 4 * D_MODEL
PLE_DIM = 256
ROPE_THETA = 10000.0
LN_EPS = 1e-5
DEEPNORM_ALPHA = (2 * DEPTH) ** 0.25
DEEPNORM_BETA = (8 * DEPTH) ** -0.25

kernel_name = "yoco_conformer_dilated_hybrid"


def layer_norm(x, g, b):
    xf = x.astype(jnp.float32)
    mu = jnp.mean(xf, axis=-1, keepdims=True)
    var = jnp.mean(jnp.square(xf - mu), axis=-1, keepdims=True)
    y = (xf - mu) * lax.rsqrt(var + LN_EPS) * g.astype(jnp.float32) + b.astype(jnp.float32)
    return y.astype(x.dtype)


def rotary(x, positions):
    half = HEAD_DIM // 2
    inv_freq = ROPE_THETA ** (-jnp.arange(half, dtype=jnp.float32) * (2.0 / HEAD_DIM))
    ang = positions.astype(jnp.float32)[..., None] * inv_freq
    cos = jnp.cos(ang)[:, :, None, :]
    sin = jnp.sin(ang)[:, :, None, :]
    xf = x.astype(jnp.float32)
    x1, x2 = xf[..., :half], xf[..., half:]
    return jnp.concatenate([x1 * cos - x2 * sin, x2 * cos + x1 * sin], axis=-1).astype(x.dtype)


def conformer_conv(x, w_in, b_in, dw, dw_b, ln_g, ln_b, w_out):
    u = x @ w_in + b_in
    a, g = jnp.split(u, 2, axis=-1)
    u = a * jax.nn.sigmoid(g)
    u = lax.conv_general_dilated(
        u, dw.astype(u.dtype), window_strides=(1,), padding=[(CONV_WIDTH - 1, 0)],
        dimension_numbers=("NWC", "WIO", "NWC"), feature_group_count=D_MODEL) + dw_b
    u = jax.nn.silu(layer_norm(u, ln_g, ln_b))
    return u @ w_out


def to_strided_blocks(t, dil):
    b_, s_pad, h, hd = t.shape
    length = s_pad // dil
    t = t.reshape(b_, length, dil, h, hd).transpose(0, 2, 3, 1, 4)
    return t.reshape(b_, dil, h, length // ATTN_BLOCK, ATTN_BLOCK, hd)


def with_previous_block(t):
    prev = jnp.pad(t, ((0, 0), (0, 0), (0, 0), (1, 0), (0, 0), (0, 0)))[:, :, :, :-1]
    return jnp.concatenate([prev, t], axis=4)


def dilated_window_attention(q, k, v, window, dil):
    n_back = window // dil
    b_, s, h, hd = q.shape
    span = dil * ATTN_BLOCK
    s_pad = -(-s // span) * span
    pad = ((0, 0), (0, s_pad - s), (0, 0), (0, 0))
    qb = to_strided_blocks(jnp.pad(q, pad), dil)
    kk = with_previous_block(to_strided_blocks(jnp.pad(k, pad), dil))
    vv = with_previous_block(to_strided_blocks(jnp.pad(v, pad), dil))
    nb = qb.shape[3]
    sc = jnp.einsum("brhnqc,brhnkc->brhnqk", qb, kk,
                    preferred_element_type=jnp.float32) * (HEAD_DIM ** -0.5)
    qi = jnp.arange(ATTN_BLOCK)[:, None]
    kj = jnp.arange(2 * ATTN_BLOCK)[None, :]
    delta = qi + ATTN_BLOCK - kj
    key_sub = jnp.arange(nb)[:, None, None] * ATTN_BLOCK + kj - ATTN_BLOCK
    valid = (delta >= 0) & (delta <= n_back) & (key_sub >= 0)
    sc = jnp.where(valid, sc, -jnp.inf)
    m = jnp.max(sc, axis=-1, keepdims=True)
    pexp = jnp.exp(sc - m)
    l = jnp.sum(pexp, axis=-1, keepdims=True)
    o = jnp.einsum("brhnqk,brhnkc->brhnqc", pexp, vv.astype(jnp.float32)) / l
    lse = (m + jnp.log(l))[..., 0]
    o = o.transpose(0, 3, 4, 1, 2, 5).reshape(b_, s_pad, h, hd)[:, :s]
    lse = lse.transpose(0, 3, 4, 1, 2).reshape(b_, s_pad, h)[:, :s]
    return o, lse


def shared_kv(x, g, b, w_kv, positions):
    b_, s, _ = x.shape
    kv = (layer_norm(x, g, b) @ w_kv).reshape(b_, s, 2, N_HEADS, HEAD_DIM)
    return rotary(kv[:, :, 0], positions), kv[:, :, 1]


def dilated_mixer(x, w_q, k, v, w_o, positions):
    b_, s, _ = x.shape
    q = (x @ w_q).reshape(b_, s, N_GROUPS, N_HEADS, HEAD_DIM)
    outs, lses = [], []
    for g, (window, dil) in enumerate(DILATED_GROUPS):
        o, lse = dilated_window_attention(rotary(q[:, :, g], positions), k, v, window, dil)
        outs.append(o)
        lses.append(lse)
    wts = jax.nn.softmax(jnp.stack(lses), axis=0)
    o = jnp.sum(wts[..., None] * jnp.stack(outs), axis=0)
    return o.reshape(b_, s, D_MODEL).astype(x.dtype) @ w_o


def sq_relu_mlp(x, w_up, w_down):
    return jnp.square(jax.nn.relu(x @ w_up)) @ w_down


def per_layer_embedding(x, p_i, w_proj, w_gate):
    return (p_i @ w_proj) * jax.nn.sigmoid(x @ w_gate)


def _fwd_setup_inputs(seed: int = 0) -> dict:
    key = jax.random.key(seed)
    ks = jax.random.split(key, 24)
    f32 = jnp.float32
    D = D_MODEL

    def w(k, shape, fan_in, scale=1.0):
        return jax.random.normal(k, shape, f32) * (fan_in ** -0.5) * scale

    def gain(k, shape):
        return 1.0 + 0.02 * jax.random.normal(k, shape, f32)

    def bias(k, shape):
        return 0.02 * jax.random.normal(k, shape, f32)

    x = jax.random.normal(ks[0], (BATCH, SEQ, D), f32)
    p = jax.random.normal(ks[1], (DEPTH, BATCH, SEQ, PLE_DIM), f32)
    offsets = jax.random.randint(ks[2], (BATCH, 1), 0, 1024, dtype=jnp.int32)
    positions = (jnp.arange(SEQ, dtype=jnp.int32)[None, :] + offsets).astype(jnp.int32)

    w_k = w(ks[10], (D, D), D)
    w_v = w(ks[11], (D, D), D, DEEPNORM_BETA)
    w_kv = jnp.concatenate([w_k, w_v], axis=-1)

    return {
        "x": x,
        "p": p,
        "positions": positions,
        "conv_w_in": w(ks[3], (N_A_LAYERS, D, 2 * D), D),
        "conv_b_in": bias(ks[4], (N_A_LAYERS, 2 * D)),
        "conv_dw": w(ks[5], (N_A_LAYERS, CONV_WIDTH, 1, D), CONV_WIDTH),
        "conv_dw_b": bias(ks[6], (N_A_LAYERS, D)),
        "conv_ln_g": gain(ks[7], (N_A_LAYERS, D)),
        "conv_ln_b": bias(ks[8], (N_A_LAYERS, D)),
        "conv_w_out": w(ks[9], (N_A_LAYERS, D, D), D, DEEPNORM_BETA),
        "kv_ln_g": gain(ks[12], (D,)),
        "kv_ln_b": bias(ks[13], (D,)),
        "w_kv": w_kv,
        "attn_w_q": w(ks[14], (N_B_LAYERS, D, N_GROUPS * D), D),
        "attn_w_o": w(ks[15], (N_B_LAYERS, D, D), D, DEEPNORM_BETA),
        "ln1_g": gain(ks[16], (DEPTH, D)),
        "ln1_b": bias(ks[17], (DEPTH, D)),
        "mlp_up": w(ks[18], (DEPTH, D, D_FF), D),
        "mlp_down": w(ks[19], (DEPTH, D_FF, D), D_FF, DEEPNORM_BETA),
        "ln2_g": gain(ks[20], (DEPTH, D)),
        "ln2_b": bias(ks[21], (DEPTH, D)),
        "ple_proj": w(ks[22], (DEPTH, PLE_DIM, D), PLE_DIM),
        "ple_gate": w(ks[23], (DEPTH, D, D), D),
    }


def _fwd_reference(x, p, positions, conv_w_in, conv_b_in, conv_dw, conv_dw_b, conv_ln_g, conv_ln_b,
              conv_w_out, kv_ln_g, kv_ln_b, w_kv, attn_w_q, attn_w_o, ln1_g, ln1_b, mlp_up, mlp_down,
              ln2_g, ln2_b, ple_proj, ple_gate):
    k = v = None
    for i in range(DEPTH):
        if i < N_A_LAYERS:
            mix = conformer_conv(x, conv_w_in[i], conv_b_in[i], conv_dw[i], conv_dw_b[i],
                                 conv_ln_g[i], conv_ln_b[i], conv_w_out[i])
        else:
            j = i - N_A_LAYERS
            mix = dilated_mixer(x, attn_w_q[j], k, v, attn_w_o[j], positions)
        x = layer_norm(DEEPNORM_ALPHA * x + mix, ln1_g[i], ln1_b[i])
        x = layer_norm(DEEPNORM_ALPHA * x + sq_relu_mlp(x, mlp_up[i], mlp_down[i]), ln2_g[i], ln2_b[i])
        x = x + per_layer_embedding(x, p[i], ple_proj[i], ple_gate[i])
        if i == N_A_LAYERS - 1:
            k, v = shared_kv(x, kv_ln_g, kv_ln_b, w_kv, positions)
    return x


import jax as _jax
import jax.numpy as _jnp

TWIN_FORMAT = 'train_step'
FWD_PARAMS = ['x', 'p', 'positions', 'conv_w_in', 'conv_b_in', 'conv_dw', 'conv_dw_b', 'conv_ln_g', 'conv_ln_b', 'conv_w_out', 'kv_ln_g', 'kv_ln_b', 'w_kv', 'attn_w_q', 'attn_w_o', 'ln1_g', 'ln1_b', 'mlp_up', 'mlp_down', 'ln2_g', 'ln2_b', 'ple_proj', 'ple_gate']
TWIN_WEIGHTS = ['conv_w_in', 'conv_b_in', 'conv_dw', 'conv_dw_b', 'conv_ln_g', 'conv_ln_b', 'conv_w_out', 'kv_ln_g', 'kv_ln_b', 'w_kv', 'attn_w_q', 'attn_w_o', 'ln1_g', 'ln1_b', 'mlp_up', 'mlp_down', 'ln2_g', 'ln2_b', 'ple_proj', 'ple_gate']
TWIN_DIFF_INPUT = 'x'
TWIN_INPUTS = ['x', 'p', 'positions', 'conv_w_in', 'conv_b_in', 'conv_dw', 'conv_dw_b', 'conv_ln_g', 'conv_ln_b', 'conv_w_out', 'kv_ln_g', 'kv_ln_b', 'w_kv', 'attn_w_q', 'attn_w_o', 'ln1_g', 'ln1_b', 'mlp_up', 'mlp_down', 'ln2_g', 'ln2_b', 'ple_proj', 'ple_gate', 'loss_target', 'm_conv_w_in', 'm_conv_b_in', 'm_conv_dw', 'm_conv_dw_b', 'm_conv_ln_g', 'm_conv_ln_b', 'm_conv_w_out', 'm_kv_ln_g', 'm_kv_ln_b', 'm_w_kv', 'm_attn_w_q', 'm_attn_w_o', 'm_ln1_g', 'm_ln1_b', 'm_mlp_up', 'm_mlp_down', 'm_ln2_g', 'm_ln2_b', 'm_ple_proj', 'm_ple_gate', 'v_conv_w_in', 'v_conv_b_in', 'v_conv_dw', 'v_conv_dw_b', 'v_conv_ln_g', 'v_conv_ln_b', 'v_conv_w_out', 'v_kv_ln_g', 'v_kv_ln_b', 'v_w_kv', 'v_attn_w_q', 'v_attn_w_o', 'v_ln1_g', 'v_ln1_b', 'v_mlp_up', 'v_mlp_down', 'v_ln2_g', 'v_ln2_b', 'v_ple_proj', 'v_ple_gate']
TWIN_OUTPUTS = ['loss', 'grad_x', 'grad_conv_w_in', 'grad_conv_b_in', 'grad_conv_dw', 'grad_conv_dw_b', 'grad_conv_ln_g', 'grad_conv_ln_b', 'grad_conv_w_out', 'grad_kv_ln_g', 'grad_kv_ln_b', 'grad_w_kv', 'grad_attn_w_q', 'grad_attn_w_o', 'grad_ln1_g', 'grad_ln1_b', 'grad_mlp_up', 'grad_mlp_down', 'grad_ln2_g', 'grad_ln2_b', 'grad_ple_proj', 'grad_ple_gate', 'delta_conv_w_in', 'delta_conv_b_in', 'delta_conv_dw', 'delta_conv_dw_b', 'delta_conv_ln_g', 'delta_conv_ln_b', 'delta_conv_w_out', 'delta_kv_ln_g', 'delta_kv_ln_b', 'delta_w_kv', 'delta_attn_w_q', 'delta_attn_w_o', 'delta_ln1_g', 'delta_ln1_b', 'delta_mlp_up', 'delta_mlp_down', 'delta_ln2_g', 'delta_ln2_b', 'delta_ple_proj', 'delta_ple_gate', 'new_m_conv_w_in', 'new_m_conv_b_in', 'new_m_conv_dw', 'new_m_conv_dw_b', 'new_m_conv_ln_g', 'new_m_conv_ln_b', 'new_m_conv_w_out', 'new_m_kv_ln_g', 'new_m_kv_ln_b', 'new_m_w_kv', 'new_m_attn_w_q', 'new_m_attn_w_o', 'new_m_ln1_g', 'new_m_ln1_b', 'new_m_mlp_up', 'new_m_mlp_down', 'new_m_ln2_g', 'new_m_ln2_b', 'new_m_ple_proj', 'new_m_ple_gate', 'new_v_conv_w_in', 'new_v_conv_b_in', 'new_v_conv_dw', 'new_v_conv_dw_b', 'new_v_conv_ln_g', 'new_v_conv_ln_b', 'new_v_conv_w_out', 'new_v_kv_ln_g', 'new_v_kv_ln_b', 'new_v_w_kv', 'new_v_attn_w_q', 'new_v_attn_w_o', 'new_v_ln1_g', 'new_v_ln1_b', 'new_v_mlp_up', 'new_v_mlp_down', 'new_v_ln2_g', 'new_v_ln2_b', 'new_v_ple_proj', 'new_v_ple_gate']
TWIN_LEAF_KINDS = {'loss': 'loss', 'grad_x': 'grad_x', 'grad_conv_w_in': 'grad_w', 'grad_conv_b_in': 'grad_w', 'grad_conv_dw': 'grad_w', 'grad_conv_dw_b': 'grad_w', 'grad_conv_ln_g': 'grad_w', 'grad_conv_ln_b': 'grad_w', 'grad_conv_w_out': 'grad_w', 'grad_kv_ln_g': 'grad_w', 'grad_kv_ln_b': 'grad_w', 'grad_w_kv': 'grad_w', 'grad_attn_w_q': 'grad_w', 'grad_attn_w_o': 'grad_w', 'grad_ln1_g': 'grad_w', 'grad_ln1_b': 'grad_w', 'grad_mlp_up': 'grad_w', 'grad_mlp_down': 'grad_w', 'grad_ln2_g': 'grad_w', 'grad_ln2_b': 'grad_w', 'grad_ple_proj': 'grad_w', 'grad_ple_gate': 'grad_w', 'delta_conv_w_in': 'delta_w', 'delta_conv_b_in': 'delta_w', 'delta_conv_dw': 'delta_w', 'delta_conv_dw_b': 'delta_w', 'delta_conv_ln_g': 'delta_w', 'delta_conv_ln_b': 'delta_w', 'delta_conv_w_out': 'delta_w', 'delta_kv_ln_g': 'delta_w', 'delta_kv_ln_b': 'delta_w', 'delta_w_kv': 'delta_w', 'delta_attn_w_q': 'delta_w', 'delta_attn_w_o': 'delta_w', 'delta_ln1_g': 'delta_w', 'delta_ln1_b': 'delta_w', 'delta_mlp_up': 'delta_w', 'delta_mlp_down': 'delta_w', 'delta_ln2_g': 'delta_w', 'delta_ln2_b': 'delta_w', 'delta_ple_proj': 'delta_w', 'delta_ple_gate': 'delta_w', 'new_m_conv_w_in': 'new_m', 'new_m_conv_b_in': 'new_m', 'new_m_conv_dw': 'new_m', 'new_m_conv_dw_b': 'new_m', 'new_m_conv_ln_g': 'new_m', 'new_m_conv_ln_b': 'new_m', 'new_m_conv_w_out': 'new_m', 'new_m_kv_ln_g': 'new_m', 'new_m_kv_ln_b': 'new_m', 'new_m_w_kv': 'new_m', 'new_m_attn_w_q': 'new_m', 'new_m_attn_w_o': 'new_m', 'new_m_ln1_g': 'new_m', 'new_m_ln1_b': 'new_m', 'new_m_mlp_up': 'new_m', 'new_m_mlp_down': 'new_m', 'new_m_ln2_g': 'new_m', 'new_m_ln2_b': 'new_m', 'new_m_ple_proj': 'new_m', 'new_m_ple_gate': 'new_m', 'new_v_conv_w_in': 'new_v', 'new_v_conv_b_in': 'new_v', 'new_v_conv_dw': 'new_v', 'new_v_conv_dw_b': 'new_v', 'new_v_conv_ln_g': 'new_v', 'new_v_conv_ln_b': 'new_v', 'new_v_conv_w_out': 'new_v', 'new_v_kv_ln_g': 'new_v', 'new_v_kv_ln_b': 'new_v', 'new_v_w_kv': 'new_v', 'new_v_attn_w_q': 'new_v', 'new_v_attn_w_o': 'new_v', 'new_v_ln1_g': 'new_v', 'new_v_ln1_b': 'new_v', 'new_v_mlp_up': 'new_v', 'new_v_mlp_down': 'new_v', 'new_v_ln2_g': 'new_v', 'new_v_ln2_b': 'new_v', 'new_v_ple_proj': 'new_v', 'new_v_ple_gate': 'new_v'}


def _forward(args):
    return _fwd_reference(*[args[k] for k in FWD_PARAMS])


def _output_shape():
    def fwd():
        inp = _fwd_setup_inputs(0)
        return _fwd_reference(*[inp[k] for k in FWD_PARAMS])
    out = _jax.eval_shape(fwd)
    return out.shape, out.dtype

N_MICROBATCH = 1
ADAM_LR = 0.001
ADAM_B1 = 0.9
ADAM_B2 = 0.999
ADAM_EPS = 1e-08
ADAM_WD = 0.01
ADAM_STEP = 10
PER_EXAMPLE_BATCH_AXIS = {'x': 0, 'p': 1, 'positions': 0, 'loss_target': 0}
SHARED_INPUTS = []
_WEIGHT_DTYPES = {'conv_w_in': _jnp.float32, 'conv_b_in': _jnp.float32, 'conv_dw': _jnp.float32, 'conv_dw_b': _jnp.float32, 'conv_ln_g': _jnp.float32, 'conv_ln_b': _jnp.float32, 'conv_w_out': _jnp.float32, 'kv_ln_g': _jnp.float32, 'kv_ln_b': _jnp.float32, 'w_kv': _jnp.float32, 'attn_w_q': _jnp.float32, 'attn_w_o': _jnp.float32, 'ln1_g': _jnp.float32, 'ln1_b': _jnp.float32, 'mlp_up': _jnp.float32, 'mlp_down': _jnp.float32, 'ln2_g': _jnp.float32, 'ln2_b': _jnp.float32, 'ple_proj': _jnp.float32, 'ple_gate': _jnp.float32}
MOMENT_SCALE = {'conv_w_in': 1.642212e-02, 'conv_b_in': 1.781090e-01, 'conv_dw': 3.065479e-02, 'conv_dw_b': 4.285877e-01, 'conv_ln_g': 1.651686e-01, 'conv_ln_b': 2.449792e-01, 'conv_w_out': 1.873426e-01, 'kv_ln_g': 3.429599e-02, 'kv_ln_b': 2.490290e-01, 'w_kv': 5.327828e-02, 'attn_w_q': 1.874516e-03, 'attn_w_o': 7.450131e-02, 'ln1_g': 4.697157e-01, 'ln1_b': 1.528289e+00, 'mlp_up': 3.059698e-02, 'mlp_down': 5.007290e-01, 'ln2_g': 1.170993e+01, 'ln2_b': 2.982607e+00, 'ple_proj': 1.823444e-01, 'ple_gate': 2.468670e-01}


def _to_microbatches(a, axis):
    t = _jnp.moveaxis(a, axis, 0)
    t = t.reshape((N_MICROBATCH, t.shape[0] // N_MICROBATCH) + t.shape[1:])
    return _jnp.moveaxis(t, 1, axis + 1)


def setup_inputs(seed: int = 0) -> dict:
    inp = _fwd_setup_inputs(seed)
    key = _jax.random.fold_in(_jax.random.key(seed), 7919)
    shape, _ = _output_shape()
    out = dict(inp)
    out["loss_target"] = _jax.random.normal(_jax.random.fold_in(key, 0), shape, _jnp.float32)
    for i, name in enumerate(TWIN_WEIGHTS):
        w = inp[name].astype(_jnp.float32)
        if MOMENT_SCALE is None:
            s = _jnp.sqrt(_jnp.mean(_jnp.square(w)) + 1e-30)
        else:
            s = MOMENT_SCALE[name]
        km, kv = _jax.random.split(_jax.random.fold_in(key, i + 1))
        out[name] = w
        out["m_" + name] = s * _jax.random.normal(km, w.shape, _jnp.float32)
        out["v_" + name] = (s * s) * _jax.random.uniform(kv, w.shape, _jnp.float32, 0.5, 1.5)
    if N_MICROBATCH > 1:
        for name, axis in PER_EXAMPLE_BATCH_AXIS.items():
            out[name] = _to_microbatches(out[name], axis)
    return {'x': out['x'], 'p': out['p'], 'positions': out['positions'], 'conv_w_in': out['conv_w_in'], 'conv_b_in': out['conv_b_in'], 'conv_dw': out['conv_dw'], 'conv_dw_b': out['conv_dw_b'], 'conv_ln_g': out['conv_ln_g'], 'conv_ln_b': out['conv_ln_b'], 'conv_w_out': out['conv_w_out'], 'kv_ln_g': out['kv_ln_g'], 'kv_ln_b': out['kv_ln_b'], 'w_kv': out['w_kv'], 'attn_w_q': out['attn_w_q'], 'attn_w_o': out['attn_w_o'], 'ln1_g': out['ln1_g'], 'ln1_b': out['ln1_b'], 'mlp_up': out['mlp_up'], 'mlp_down': out['mlp_down'], 'ln2_g': out['ln2_g'], 'ln2_b': out['ln2_b'], 'ple_proj': out['ple_proj'], 'ple_gate': out['ple_gate'], 'loss_target': out['loss_target'], 'm_conv_w_in': out['m_conv_w_in'], 'm_conv_b_in': out['m_conv_b_in'], 'm_conv_dw': out['m_conv_dw'], 'm_conv_dw_b': out['m_conv_dw_b'], 'm_conv_ln_g': out['m_conv_ln_g'], 'm_conv_ln_b': out['m_conv_ln_b'], 'm_conv_w_out': out['m_conv_w_out'], 'm_kv_ln_g': out['m_kv_ln_g'], 'm_kv_ln_b': out['m_kv_ln_b'], 'm_w_kv': out['m_w_kv'], 'm_attn_w_q': out['m_attn_w_q'], 'm_attn_w_o': out['m_attn_w_o'], 'm_ln1_g': out['m_ln1_g'], 'm_ln1_b': out['m_ln1_b'], 'm_mlp_up': out['m_mlp_up'], 'm_mlp_down': out['m_mlp_down'], 'm_ln2_g': out['m_ln2_g'], 'm_ln2_b': out['m_ln2_b'], 'm_ple_proj': out['m_ple_proj'], 'm_ple_gate': out['m_ple_gate'], 'v_conv_w_in': out['v_conv_w_in'], 'v_conv_b_in': out['v_conv_b_in'], 'v_conv_dw': out['v_conv_dw'], 'v_conv_dw_b': out['v_conv_dw_b'], 'v_conv_ln_g': out['v_conv_ln_g'], 'v_conv_ln_b': out['v_conv_ln_b'], 'v_conv_w_out': out['v_conv_w_out'], 'v_kv_ln_g': out['v_kv_ln_g'], 'v_kv_ln_b': out['v_kv_ln_b'], 'v_w_kv': out['v_w_kv'], 'v_attn_w_q': out['v_attn_w_q'], 'v_attn_w_o': out['v_attn_w_o'], 'v_ln1_g': out['v_ln1_g'], 'v_ln1_b': out['v_ln1_b'], 'v_mlp_up': out['v_mlp_up'], 'v_mlp_down': out['v_mlp_down'], 'v_ln2_g': out['v_ln2_g'], 'v_ln2_b': out['v_ln2_b'], 'v_ple_proj': out['v_ple_proj'], 'v_ple_gate': out['v_ple_gate']}


def _loss(weights, diff, rest, loss_target):
    with _jax.named_scope("forward"):
        args = {**rest, TWIN_DIFF_INPUT: diff, **{k: w.astype(_WEIGHT_DTYPES[k]) for k, w in weights.items()}}
        y = _forward(args)
    with _jax.named_scope("loss_head"):
        err = _jnp.square(y.astype(_jnp.float32) - loss_target)
        return 0.5 * _jnp.sum(_jnp.mean(err, axis=-1)) if err.ndim else 0.5 * err


def _adamw(w, g, m, v):
    m = ADAM_B1 * m + (1.0 - ADAM_B1) * g
    v = ADAM_B2 * v + (1.0 - ADAM_B2) * _jnp.square(g)
    m_hat = m / (1.0 - ADAM_B1 ** ADAM_STEP)
    v_hat = v / (1.0 - ADAM_B2 ** ADAM_STEP)
    delta = -ADAM_LR * (m_hat / (_jnp.sqrt(v_hat) + ADAM_EPS) + ADAM_WD * w)
    return delta, m, v


def reference(x, p, positions, conv_w_in, conv_b_in, conv_dw, conv_dw_b, conv_ln_g, conv_ln_b, conv_w_out, kv_ln_g, kv_ln_b, w_kv, attn_w_q, attn_w_o, ln1_g, ln1_b, mlp_up, mlp_down, ln2_g, ln2_b, ple_proj, ple_gate, loss_target, m_conv_w_in, m_conv_b_in, m_conv_dw, m_conv_dw_b, m_conv_ln_g, m_conv_ln_b, m_conv_w_out, m_kv_ln_g, m_kv_ln_b, m_w_kv, m_attn_w_q, m_attn_w_o, m_ln1_g, m_ln1_b, m_mlp_up, m_mlp_down, m_ln2_g, m_ln2_b, m_ple_proj, m_ple_gate, v_conv_w_in, v_conv_b_in, v_conv_dw, v_conv_dw_b, v_conv_ln_g, v_conv_ln_b, v_conv_w_out, v_kv_ln_g, v_kv_ln_b, v_w_kv, v_attn_w_q, v_attn_w_o, v_ln1_g, v_ln1_b, v_mlp_up, v_mlp_down, v_ln2_g, v_ln2_b, v_ple_proj, v_ple_gate):
    given = dict(x=x, p=p, positions=positions, conv_w_in=conv_w_in, conv_b_in=conv_b_in, conv_dw=conv_dw, conv_dw_b=conv_dw_b, conv_ln_g=conv_ln_g, conv_ln_b=conv_ln_b, conv_w_out=conv_w_out, kv_ln_g=kv_ln_g, kv_ln_b=kv_ln_b, w_kv=w_kv, attn_w_q=attn_w_q, attn_w_o=attn_w_o, ln1_g=ln1_g, ln1_b=ln1_b, mlp_up=mlp_up, mlp_down=mlp_down, ln2_g=ln2_g, ln2_b=ln2_b, ple_proj=ple_proj, ple_gate=ple_gate, loss_target=loss_target, m_conv_w_in=m_conv_w_in, m_conv_b_in=m_conv_b_in, m_conv_dw=m_conv_dw, m_conv_dw_b=m_conv_dw_b, m_conv_ln_g=m_conv_ln_g, m_conv_ln_b=m_conv_ln_b, m_conv_w_out=m_conv_w_out, m_kv_ln_g=m_kv_ln_g, m_kv_ln_b=m_kv_ln_b, m_w_kv=m_w_kv, m_attn_w_q=m_attn_w_q, m_attn_w_o=m_attn_w_o, m_ln1_g=m_ln1_g, m_ln1_b=m_ln1_b, m_mlp_up=m_mlp_up, m_mlp_down=m_mlp_down, m_ln2_g=m_ln2_g, m_ln2_b=m_ln2_b, m_ple_proj=m_ple_proj, m_ple_gate=m_ple_gate, v_conv_w_in=v_conv_w_in, v_conv_b_in=v_conv_b_in, v_conv_dw=v_conv_dw, v_conv_dw_b=v_conv_dw_b, v_conv_ln_g=v_conv_ln_g, v_conv_ln_b=v_conv_ln_b, v_conv_w_out=v_conv_w_out, v_kv_ln_g=v_kv_ln_g, v_kv_ln_b=v_kv_ln_b, v_w_kv=v_w_kv, v_attn_w_q=v_attn_w_q, v_attn_w_o=v_attn_w_o, v_ln1_g=v_ln1_g, v_ln1_b=v_ln1_b, v_mlp_up=v_mlp_up, v_mlp_down=v_mlp_down, v_ln2_g=v_ln2_g, v_ln2_b=v_ln2_b, v_ple_proj=v_ple_proj, v_ple_gate=v_ple_gate)
    weights = {n: given[n] for n in TWIN_WEIGHTS}
    shared = {n: given[n] for n in SHARED_INPUTS}
    per_example = {n: given[n] for n in ['x', 'p', 'positions']}
    grad_fn = _jax.value_and_grad(_loss, argnums=(0, 1))

    def one_microbatch(ex, loss_target):
        ex = dict(ex)
        diff = ex.pop(TWIN_DIFF_INPUT)
        return grad_fn(weights, diff, {**shared, **ex}, loss_target)

    if N_MICROBATCH == 1:
        loss, (grad_w, grad_x) = one_microbatch(per_example, given["loss_target"])
    else:
        def body(carry, xs):
            loss_sum, grad_sum = carry
            l_k, (gw_k, gx_k) = one_microbatch(xs[0], xs[1])
            with _jax.named_scope("update"):
                return (loss_sum + l_k, _jax.tree.map(_jnp.add, grad_sum, gw_k)), gx_k

        init = (_jnp.zeros((), _jnp.float32), _jax.tree.map(_jnp.zeros_like, weights))
        (loss, grad_w), grad_x = _jax.lax.scan(body, init, (per_example, given["loss_target"]))
    with _jax.named_scope("update"):
        delta_w, new_m, new_v = {}, {}, {}
        for n in TWIN_WEIGHTS:
            delta_w[n], new_m[n], new_v[n] = _adamw(weights[n], grad_w[n], given["m_" + n], given["v_" + n])
    return (loss, grad_x, *[grad_w[n] for n in TWIN_WEIGHTS], *[delta_w[n] for n in TWIN_WEIGHTS],
            *[new_m[n] for n in TWIN_WEIGHTS], *[new_v[n] for n in TWIN_WEIGHTS])
```

```python
import functools

import jax
import jax.numpy as jnp
from jax import lax
from jax.experimental import pallas as pl
from jax.experimental.pallas import tpu as pltpu

F32 = jnp.float32
BF16 = jnp.bfloat16

HEAD_DIM = 128
ATTN_BLOCK = 128
DILATIONS = (1, 4, 16)
N_GROUPS = 3
CONV_WIDTH = 31
CONV_PAD = 32
ROPE_THETA = 10000.0
LN_EPS = 1e-5
ALPHA = 4.0 ** 0.25
ATTN_SCALE = HEAD_DIM ** -0.5
NEG = -1e30

ADAM_LR = 0.001
ADAM_B1 = 0.9
ADAM_B2 = 0.999
ADAM_EPS = 1e-08
ADAM_WD = 0.01
ADAM_STEP = 10

N_CHIPS = 4
VMEM_CAP = 60 << 20
MESH = pl.DeviceIdType.MESH
ANY = pl.BlockSpec(memory_space=pl.ANY)


def _vmem(nbytes):
    return int(min(max(2 * nbytes + (8 << 20), 24 << 20), VMEM_CAP))


def _fit(tile, n):
    if n <= tile:
        return n
    t = tile - tile % 128
    while n % t:
        t -= 128
    return t


def _nbytes(shape, dtype):
    n = 1
    for s in shape:
        n *= s
    return n * jnp.dtype(dtype).itemsize


_DIMS = {"nn": (((1,), (0,)), ((), ())), "nt": (((1,), (1,)), ((), ())), "tn": (((0,), (0,)), ((), ()))}


def _mm(a, b, *, mode, outs, name, epilogue=None, extras=(), rextras=(), vecs=(), a_sel=None, b_sel=None,
        tm=1024, tn=1024, tk=512, layout=None):
    a2, b2 = a.shape[-2:], b.shape[-2:]
    if mode == "nn":
        (M, K), (K2, N) = a2, b2
    elif mode == "nt":
        (M, K), (N, K2) = a2, b2
    else:
        (K, M), (K2, N) = a2, b2
    assert K == K2, (a.shape, b.shape, mode)
    if layout is not None:
        kind, nslots = layout
        r, c = (M // 2, N // nslots) if kind == "col" else (M // nslots, N // 2)
        tm, tn = _fit(tm, r), _fit(tn, c)
        assert r % tm == 0 and c % tn == 0
    else:
        tm, tn = _fit(tm, M), _fit(tn, N)
    tk = _fit(tk, K)
    assert M % tm == 0 and N % tn == 0 and K % tk == 0, (M, N, K, tm, tn, tk)
    nk = K // tk
    grid = (M // tm, N // tn, nk)

    def spec(arr, sel, blk, imap):
        if arr.ndim == 3:
            return pl.BlockSpec((None,) + blk, lambda i, j, k: (sel,) + imap(i, j, k))
        return pl.BlockSpec(blk, imap)

    if mode == "tn":
        a_spec = spec(a, a_sel, (tk, tm), lambda i, j, k: (k, i))
    else:
        a_spec = spec(a, a_sel, (tm, tk), lambda i, j, k: (i, k))
    if mode == "nt":
        b_spec = spec(b, b_sel, (tn, tk), lambda i, j, k: (j, k))
    else:
        b_spec = spec(b, b_sel, (tk, tn), lambda i, j, k: (k, j))
    in_specs = [a_spec, b_spec]
    in_specs += [pl.BlockSpec((tm, tn), lambda i, j, k: (i, j)) for _ in extras]
    in_specs += [pl.BlockSpec((tm, e.shape[1]), lambda i, j, k: (i, 0)) for e in rextras]
    in_specs += [pl.BlockSpec((1, tn), lambda i, j, k: (0, j)) for _ in vecs]

    if layout is None:
        out_shape = [jax.ShapeDtypeStruct((M, N), d) for d in outs]
        out_specs = [pl.BlockSpec((tm, tn), lambda i, j, k: (i, j)) for _ in outs]
    else:
        assert len(outs) == 1
        out_shape = [jax.ShapeDtypeStruct((2, nslots, r, c), outs[0])]
        rb, cb = r // tm, c // tn
        if kind == "col":
            omap = lambda i, j, k: (i // rb, j // cb, i % rb, j % cb)
        else:
            omap = lambda i, j, k: (j // cb, i // rb, i % rb, j % cb)
        out_specs = [pl.BlockSpec((None, None, tm, tn), omap)]

    ne, nr, nv, no = len(extras), len(rextras), len(vecs), len(outs)
    dims = _DIMS[mode]

    def body(*refs):
        a_ref, b_ref = refs[0], refs[1]
        rest = refs[2:2 + ne + nr + nv]
        o_refs = refs[2 + ne + nr + nv:2 + ne + nr + nv + no]
        acc = refs[-1]
        k = pl.program_id(2)

        @pl.when(k == 0)
        def _():
            acc[...] = jnp.zeros_like(acc)

        acc[...] += lax.dot_general(a_ref[...].astype(BF16), b_ref[...].astype(BF16), dims,
                                    preferred_element_type=F32)

        @pl.when(k == nk - 1)
        def _():
            if epilogue is None:
                res = (acc[...],) * no
            else:
                res = epilogue(acc[...], *[x[...] for x in rest])
            for o, val in zip(o_refs, res):
                o[...] = val.astype(o.dtype)

    blk = (_nbytes((tm, tk), a.dtype) + _nbytes((tk, tn), b.dtype) + sum(_nbytes((tm, tn), e.dtype) for e in extras)
           + sum(_nbytes((tm, tn), d) for d in outs) + 4 * tm * tn * 4)
    res = pl.pallas_call(
        body, name=name, grid=grid, in_specs=in_specs, out_specs=out_specs, out_shape=out_shape,
        scratch_shapes=[pltpu.VMEM((tm, tn), F32)],
        compiler_params=pltpu.CompilerParams(dimension_semantics=("parallel", "parallel", "arbitrary"),
                                             vmem_limit_bytes=_vmem(blk)),
    )(a, b, *extras, *rextras, *vecs)
    return res[0] if no == 1 else tuple(res)


def _rows(fn, rows, vecs, outs, sums, *, tm, name):
    S = rows[0].shape[0]
    tm = min(tm, S)
    assert S % tm == 0
    nr, nv, no, ns = len(rows), len(vecs), len(outs), len(sums)

    def body(*refs):
        vals = fn(*[r[...] for r in refs[:nr + nv]])
        o_refs = refs[nr + nv:nr + nv + no]
        s_refs = refs[nr + nv + no:]
        for o, val in zip(o_refs, vals[:no]):
            o[...] = val.astype(o.dtype)
        if ns:
            @pl.when(pl.program_id(0) == 0)
            def _():
                for s in s_refs:
                    s[...] = jnp.zeros_like(s)

            for s, val in zip(s_refs, vals[no:]):
                s[...] += jnp.sum(val.astype(F32), axis=0, keepdims=True)

    in_specs = [pl.BlockSpec((tm, r.shape[1]), lambda i: (i, 0)) for r in rows]
    in_specs += [pl.BlockSpec(v.shape, lambda i: (0, 0)) for v in vecs]
    out_specs = [pl.BlockSpec((tm, c), lambda i: (i, 0)) for c, _ in outs]
    out_specs += [pl.BlockSpec((1, c), lambda i: (0, 0)) for c in sums]
    out_shape = [jax.ShapeDtypeStruct((S, c), d) for c, d in outs]
    out_shape += [jax.ShapeDtypeStruct((1, c), F32) for c in sums]
    blk = sum(_nbytes((tm, r.shape[1]), r.dtype) for r in rows) + sum(_nbytes((tm, c), d) for c, d in outs)
    blk += 6 * tm * max(r.shape[1] for r in rows) * 4
    res = pl.pallas_call(
        body, name=name, grid=(S // tm,), in_specs=in_specs, out_specs=out_specs, out_shape=out_shape,
        compiler_params=pltpu.CompilerParams(dimension_semantics=("arbitrary",), vmem_limit_bytes=_vmem(blk)),
    )(*rows, *vecs)
    return tuple(res)


def _ln_norm(z):
    mu = jnp.mean(z, axis=-1, keepdims=True)
    d = z - mu
    var = jnp.mean(d * d, axis=-1, keepdims=True)
    rstd = lax.rsqrt(var + LN_EPS)
    return d * rstd, rstd


def _ln(z, g, b):
    return _ln_norm(z)[0] * g + b


def _ln_bwd(dy, n, rstd, g):
    dn = dy * g
    return rstd * (dn - jnp.mean(dn, axis=-1, keepdims=True) - n * jnp.mean(dn * n, axis=-1, keepdims=True))


def _sigmoid(x):
    return 1.0 / (1.0 + jnp.exp(-x))


def _per_head(x, fn):
    h = x.shape[1] // HEAD_DIM
    return jnp.concatenate([fn(x[:, i * HEAD_DIM:(i + 1) * HEAD_DIM], i) for i in range(h)], axis=1)


def _rot(x, cosf, sinf):
    return _per_head(x, lambda xh, i: xh * cosf + pltpu.roll(xh, HEAD_DIM // 2, 1) * sinf)


def _rot_t(dy, cosf, sinf):
    return _per_head(dy, lambda dh, i: dh * cosf + pltpu.roll(dh * sinf, HEAD_DIM // 2, 1))


def _conv_fwd(glu, dw, dwb, *, tm=256, tc=512):
    S, D = glu.shape
    tm, tc = min(tm, S), min(tc, D)
    ni = S // tm

    def body(cur_ref, prev_ref, dw_ref, dwb_ref, o_ref, win):
        i = pl.program_id(1)
        tail = prev_ref[tm - CONV_PAD:tm, :]
        win[0:CONV_PAD, :] = jnp.where(i > 0, tail, jnp.zeros_like(tail))
        win[CONV_PAD:CONV_PAD + tm, :] = cur_ref[...]
        acc = jnp.zeros((tm, tc), F32) + dwb_ref[...]
        for k in range(CONV_WIDTH):
            acc = acc + win[pl.ds(CONV_PAD - CONV_WIDTH + 1 + k, tm), :] * dw_ref[k:k + 1, :]
        o_ref[...] = acc

    return pl.pallas_call(
        body, name="conv_fwd", grid=(D // tc, ni),
        in_specs=[pl.BlockSpec((tm, tc), lambda j, i: (i, j)),
                  pl.BlockSpec((tm, tc), lambda j, i: (jnp.maximum(i - 1, 0), j)),
                  pl.BlockSpec((CONV_PAD, tc), lambda j, i: (0, j)),
                  pl.BlockSpec((1, tc), lambda j, i: (0, j))],
        out_specs=pl.BlockSpec((tm, tc), lambda j, i: (i, j)),
        out_shape=jax.ShapeDtypeStruct((S, D), F32),
        scratch_shapes=[pltpu.VMEM((tm + CONV_PAD, tc), F32)],
        compiler_params=pltpu.CompilerParams(dimension_semantics=("parallel", "arbitrary"),
                                             vmem_limit_bytes=_vmem(8 * tm * tc * 4)),
    )(glu, glu, dw, dwb)


def _conv_bwd(dc, glu, a_pre, g_pre, dw, ba, bg, *, tm=256, tc=512):
    S, D = dc.shape
    tm, tc = min(tm, S), min(tc, D)
    ni = S // tm

    def body(dc_ref, dcn_ref, glu_ref, glup_ref, a_ref, g_ref, dw_ref, ba_ref, bg_ref,
             da_ref, dg_ref, ddw_ref, dba_ref, dbg_ref, dwin, gwin):
        i = pl.program_id(1)

        @pl.when(i == 0)
        def _():
            ddw_ref[...] = jnp.zeros_like(ddw_ref)
            dba_ref[...] = jnp.zeros_like(dba_ref)
            dbg_ref[...] = jnp.zeros_like(dbg_ref)

        dcur = dc_ref[...]
        head = dcn_ref[0:CONV_PAD, :]
        dwin[0:tm, :] = dcur
        dwin[tm:tm + CONV_PAD, :] = jnp.where(i < ni - 1, head, jnp.zeros_like(head))
        tail = glup_ref[tm - CONV_PAD:tm, :]
        gwin[0:CONV_PAD, :] = jnp.where(i > 0, tail, jnp.zeros_like(tail))
        gwin[CONV_PAD:CONV_PAD + tm, :] = glu_ref[...]
        dglu = jnp.zeros((tm, tc), F32)
        for k in range(CONV_WIDTH):
            dglu = dglu + dwin[pl.ds(CONV_WIDTH - 1 - k, tm), :] * dw_ref[k:k + 1, :]
            shifted = gwin[pl.ds(CONV_PAD - CONV_WIDTH + 1 + k, tm), :]
            ddw_ref[k:k + 1, :] += jnp.sum(dcur * shifted, axis=0, keepdims=True)
        a = a_ref[...] + ba_ref[...]
        sg = _sigmoid(g_ref[...] + bg_ref[...])
        da = dglu * sg
        dg = dglu * a * sg * (1.0 - sg)
        da_ref[...] = da.astype(BF16)
        dg_ref[...] = dg.astype(BF16)
        dba_ref[...] += jnp.sum(da, axis=0, keepdims=True)
        dbg_ref[...] += jnp.sum(dg, axis=0, keepdims=True)

    tile = lambda f: pl.BlockSpec((tm, tc), f)
    vec = pl.BlockSpec((1, tc), lambda j, i: (0, j))
    return pl.pallas_call(
        body, name="conv_bwd", grid=(D // tc, ni),
        in_specs=[tile(lambda j, i: (i, j)), tile(lambda j, i: (jnp.minimum(i + 1, ni - 1), j)),
                  tile(lambda j, i: (i, j)), tile(lambda j, i: (jnp.maximum(i - 1, 0), j)),
                  tile(lambda j, i: (i, j)), tile(lambda j, i: (i, j)),
                  pl.BlockSpec((CONV_PAD, tc), lambda j, i: (0, j)), vec, vec],
        out_specs=[tile(lambda j, i: (i, j)), tile(lambda j, i: (i, j)),
                   pl.BlockSpec((CONV_PAD, tc), lambda j, i: (0, j)), vec, vec],
        out_shape=[jax.ShapeDtypeStruct((S, D), BF16), jax.ShapeDtypeStruct((S, D), BF16),
                   jax.ShapeDtypeStruct((CONV_PAD, D), F32), jax.ShapeDtypeStruct((1, D), F32),
                   jax.ShapeDtypeStruct((1, D), F32)],
        scratch_shapes=[pltpu.VMEM((tm + CONV_PAD, tc), F32), pltpu.VMEM((tm + CONV_PAD, tc), F32)],
        compiler_params=pltpu.CompilerParams(dimension_semantics=("parallel", "arbitrary"),
                                             vmem_limit_bytes=_vmem(16 * tm * tc * 4)),
    )(dc, dc, glu, glu, a_pre, g_pre, dw, ba, bg)


def _nt(a, b):
    return lax.dot_general(a, b, _DIMS["nt"], preferred_element_type=F32)


def _tn(a, b):
    return lax.dot_general(a, b, _DIMS["tn"], preferred_element_type=F32)


def _window_mask(qi, kj, first_key):
    B = ATTN_BLOCK
    return ((kj < B) & (kj >= qi) & (kj >= first_key)) | ((kj >= B) & (kj - B <= qi))


def _attn_fwd(q_rot, k, v, g, dil):
    S, D = k.shape
    H = D // HEAD_DIM
    L = S // dil
    nb_count = L // ATTN_BLOCK
    B = ATTN_BLOCK

    def body(q_ref, kc_ref, kp_ref, vc_ref, vp_ref, o_ref, lse_ref):
        nb = pl.program_id(1)
        qi = lax.broadcasted_iota(jnp.int32, (B, 2 * B), 0)
        kj = lax.broadcasted_iota(jnp.int32, (B, 2 * B), 1)
        valid = _window_mask(qi, kj, jnp.where(nb > 0, 0, B))
        for h in range(H):
            hs = slice(h * HEAD_DIM, (h + 1) * HEAD_DIM)
            kk = jnp.concatenate([kp_ref[:, hs], kc_ref[:, hs]], axis=0)
            vv = jnp.concatenate([vp_ref[:, hs], vc_ref[:, hs]], axis=0)
            s = jnp.where(valid, _nt(q_ref[:, hs], kk) * ATTN_SCALE, NEG)
            m = jnp.max(s, axis=1, keepdims=True)
            p = jnp.exp(s - m)
            l = jnp.sum(p, axis=1, keepdims=True)
            o = jnp.dot(p.astype(BF16), vv, preferred_element_type=F32) / l
            o_ref[:, hs] = o.astype(o_ref.dtype)
            lse_ref[:, hs] = jnp.broadcast_to(m + jnp.log(l), (B, HEAD_DIM))

    blk = lambda f: pl.BlockSpec((B, D), f)
    cur = lambda r, nb: (nb, r)
    prev = lambda r, nb: (jnp.maximum(nb - 1, 0), r)
    o, lse = pl.pallas_call(
        body, name=f"attn_fwd_g{g}", grid=(dil, nb_count),
        in_specs=[blk(lambda r, nb: (nb, r * N_GROUPS + g)), blk(cur), blk(prev), blk(cur), blk(prev)],
        out_specs=[blk(cur), blk(cur)],
        out_shape=[jax.ShapeDtypeStruct((L, dil * D), BF16), jax.ShapeDtypeStruct((L, dil * D), F32)],
        compiler_params=pltpu.CompilerParams(dimension_semantics=("parallel", "arbitrary"),
                                             vmem_limit_bytes=_vmem(12 * B * D * 4)),
    )(q_rot.reshape(L, dil * N_GROUPS * D), k.reshape(L, dil * D), k.reshape(L, dil * D),
      v.reshape(L, dil * D), v.reshape(L, dil * D))
    return o.reshape(S, D), lse.reshape(S, D)


def _attn_bwd(q_rot, k, v, do, lse, dlt, g, dil):
    S, D = k.shape
    H = D // HEAD_DIM
    L = S // dil
    nb_count = L // ATTN_BLOCK
    B = ATTN_BLOCK

    def body(q_ref, qn_ref, kc_ref, kp_ref, vc_ref, vp_ref, do_ref, don_ref, l_ref, ln_ref, d_ref, dn_ref,
             dq_ref, dk_ref, dv_ref):
        nb = pl.program_id(1)
        qi = lax.broadcasted_iota(jnp.int32, (B, 2 * B), 0)
        kj = lax.broadcasted_iota(jnp.int32, (B, 2 * B), 1)
        valid_q = _window_mask(qi, kj, jnp.where(nb > 0, 0, B))
        qr = lax.broadcasted_iota(jnp.int32, (2 * B, B), 0)
        kc_i = lax.broadcasted_iota(jnp.int32, (2 * B, B), 1)
        q_end = jnp.where(nb < nb_count - 1, 2 * B, B)
        valid_k = ((qr < B) & (kc_i <= qr)) | ((qr >= B) & (qr < q_end) & (kc_i >= qr - B))
        for h in range(H):
            hs = slice(h * HEAD_DIM, (h + 1) * HEAD_DIM)
            q, qn = q_ref[:, hs], qn_ref[:, hs]
            kc, vc = kc_ref[:, hs], vc_ref[:, hs]
            dout, doutn = do_ref[:, hs], don_ref[:, hs]
            lq, lqn = l_ref[:, hs][:, 0:1], ln_ref[:, hs][:, 0:1]
            dq_, dqn_ = d_ref[:, hs][:, 0:1], dn_ref[:, hs][:, 0:1]
            kk = jnp.concatenate([kp_ref[:, hs], kc], axis=0)
            vv = jnp.concatenate([vp_ref[:, hs], vc], axis=0)
            s = jnp.where(valid_q, _nt(q, kk) * ATTN_SCALE, NEG)
            p = jnp.exp(s - lq)
            ds = p * (_nt(dout, vv) - dq_)
            dq_ref[:, hs] = (jnp.dot(ds.astype(BF16), kk, preferred_element_type=F32) * ATTN_SCALE).astype(dq_ref.dtype)
            qq = jnp.concatenate([q, qn], axis=0)
            dd = jnp.concatenate([dout, doutn], axis=0)
            ll = jnp.concatenate([lq, lqn], axis=0)
            dl = jnp.concatenate([dq_, dqn_], axis=0)
            s2 = jnp.where(valid_k, _nt(qq, kc) * ATTN_SCALE, NEG)
            p2 = jnp.exp(s2 - ll)
            dv_ref[:, hs] = _tn(p2.astype(BF16), dd)
            ds2 = p2 * (_nt(dd, vc) - dl)
            dk_ref[:, hs] = _tn(ds2.astype(BF16), qq) * ATTN_SCALE

    blk = lambda f: pl.BlockSpec((B, D), f)
    cur = lambda r, nb: (nb, r)
    prev = lambda r, nb: (jnp.maximum(nb - 1, 0), r)
    nxt = lambda r, nb: (jnp.minimum(nb + 1, nb_count - 1), r)
    qcur = lambda r, nb: (nb, r * N_GROUPS + g)
    qnxt = lambda r, nb: (jnp.minimum(nb + 1, nb_count - 1), r * N_GROUPS + g)
    qv = q_rot.reshape(L, dil * N_GROUPS * D)
    view = lambda t: t.reshape(L, dil * D)
    dq, dk, dv = pl.pallas_call(
        body, name=f"attn_bwd_g{g}", grid=(dil, nb_count),
        in_specs=[blk(qcur), blk(qnxt), blk(cur), blk(prev), blk(cur), blk(prev), blk(cur), blk(nxt),
                  blk(cur), blk(nxt), blk(cur), blk(nxt)],
        out_specs=[blk(cur), blk(cur), blk(cur)],
        out_shape=[jax.ShapeDtypeStruct((L, dil * D), BF16), jax.ShapeDtypeStruct((L, dil * D), F32),
                   jax.ShapeDtypeStruct((L, dil * D), F32)],
        compiler_params=pltpu.CompilerParams(dimension_semantics=("parallel", "arbitrary"),
                                             vmem_limit_bytes=_vmem(24 * B * D * 4)),
    )(qv, qv, view(k), view(k), view(v), view(v), view(do), view(do), view(lse), view(lse), view(dlt), view(dlt))
    return dq.reshape(S, D), dk.reshape(S, D), dv.reshape(S, D)


def _mlp_ple_fwd(z1, h1b, p_l, target, wts, vec, l, last):
    D = z1.shape[1]
    up, down, pp_w, pg_w = wts
    g1, b1, g2, b2 = vec

    def act_ep(acc):
        return acc, jnp.square(jnp.maximum(acc, 0.0))

    t, act = _mm(h1b, up, mode="nn", outs=[BF16, BF16], epilogue=act_ep, name=f"mlp_up{l}", tm=512)
    f = _mm(act, down, mode="nn", outs=[F32], name=f"mlp_down{l}")

    def z2_fn(z1_t, f_t, g1_, b1_, g2_, b2_):
        z2 = ALPHA * _ln(z1_t, g1_, b1_) + f_t
        return z2, _ln(z2, g2_, b2_)

    z2, h2b = _rows(z2_fn, [z1, f], [g1, b1, g2, b2], [(D, F32), (D, BF16)], [], tm=256, name=f"ln2_fwd{l}")
    pp = _mm(p_l, pp_w, mode="nn", outs=[F32], name=f"ple_proj{l}")
    gpre = _mm(h2b, pg_w, mode="nn", outs=[F32], name=f"ple_gate{l}")
    return t, act, z2, h2b, pp, gpre


def _mlp_ple_bwd(dy, d_pp, d_gpre, p_l, z1, h1b, t, act, z2, h2b, wts, vec, l):
    D = z1.shape[1]
    up, down, pp_w, pg_w = wts
    g1, b1, g2, b2 = vec
    gw = {}
    gw[f"ple_proj{l}"] = _mm(p_l, d_pp, mode="tn", outs=[BF16], layout=("col", N_CHIPS), name=f"d_ple_proj{l}")
    gw[f"ple_gate{l}"] = _mm(h2b, d_gpre, mode="tn", outs=[BF16], layout=("row", N_CHIPS), name=f"d_ple_gate{l}")
    dh2 = _mm(d_gpre, pg_w, mode="nt", outs=[F32], extras=[dy], epilogue=lambda acc, e: (acc + e,),
              name=f"dh2_{l}")

    def ln2_bwd(dh2_t, z2_t, g2_):
        n, rstd = _ln_norm(z2_t)
        dz2 = _ln_bwd(dh2_t, n, rstd, g2_)
        return dz2, dz2, dh2_t * n, dh2_t

    dz2, dz2b, dg2, db2 = _rows(ln2_bwd, [dh2, z2], [g2], [(D, F32), (D, BF16)], [D, D], tm=256,
                                name=f"ln2_bwd{l}")
    gw[f"mlp_down{l}"] = _mm(act, dz2b, mode="tn", outs=[BF16], layout=("row", N_CHIPS), name=f"d_mlp_down{l}")
    dt = _mm(dz2b, down, mode="nt", outs=[BF16], extras=[t],
             epilogue=lambda acc, t_: (acc * 2.0 * jnp.maximum(t_.astype(F32), 0.0),), name=f"dt{l}", tm=512)
    gw[f"mlp_up{l}"] = _mm(h1b, dt, mode="tn", outs=[BF16], layout=("col", N_CHIPS), name=f"d_mlp_up{l}")
    dh1 = _mm(dt, up, mode="nt", outs=[F32], extras=[dz2], epilogue=lambda acc, e: (acc + ALPHA * e,),
              name=f"dh1_{l}")

    def ln1_bwd(dh1_t, z1_t, g1_):
        n, rstd = _ln_norm(z1_t)
        dz1 = _ln_bwd(dh1_t, n, rstd, g1_)
        return dz1, dz1, dh1_t * n, dh1_t

    dz1, dz1b, dg1, db1 = _rows(ln1_bwd, [dh1, z1], [g1], [(D, F32), (D, BF16)], [D, D], tm=256,
                                name=f"ln1_bwd{l}")
    return dz1, dz1b, gw, (dg1, db1, dg2, db2)


def _ple_out(z2, pp, gpre, g2, b2):
    gt = _sigmoid(gpre)
    return _ln(z2, g2, b2) + pp * gt, gt


def _local_step(x, p, cosf, sinf, target, W, V):
    S, D = x.shape
    gw, gv = {}, {}

    a_pre = _mm(x, W["conv_w_in"], b_sel=0, mode="nn", outs=[F32], name="conv_in_a")
    g_pre = _mm(x, W["conv_w_in"], b_sel=1, mode="nn", outs=[F32], name="conv_in_g")
    (glu,) = _rows(lambda a, g, ba, bg: ((a + ba) * _sigmoid(g + bg),), [a_pre, g_pre], [V["conv_b_a"], V["conv_b_g"]],
                   [(D, F32)], [], tm=256, name="glu_fwd")
    cv = _conv_fwd(glu, V["conv_dw"], V["conv_dw_b"])

    def silu_ln(c, g_, b_):
        y = _ln(c, g_, b_)
        return (y * _sigmoid(y),)

    (sb,) = _rows(silu_ln, [cv], [V["conv_ln_g"], V["conv_ln_b"]], [(D, BF16)], [], tm=256, name="conv_ln_fwd")
    mix0 = _mm(sb, W["conv_w_out"], mode="nn", outs=[F32], name="conv_out")

    def z1_fn(x_t, mix_t, g_, b_):
        z1 = ALPHA * x_t + mix_t
        return z1, _ln(z1, g_, b_)

    vec0 = (V["ln1_g0"], V["ln1_b0"], V["ln2_g0"], V["ln2_b0"])
    vec1 = (V["ln1_g1"], V["ln1_b1"], V["ln2_g1"], V["ln2_b1"])
    wts0 = (W["mlp_up0"], W["mlp_down0"], W["ple_proj0"], W["ple_gate0"])
    wts1 = (W["mlp_up1"], W["mlp_down1"], W["ple_proj1"], W["ple_gate1"])
    z1_0, h1b_0 = _rows(z1_fn, [x, mix0], [vec0[0], vec0[1]], [(D, F32), (D, BF16)], [], tm=256, name="ln1_fwd0")
    t0, act0, z2_0, h2b_0, pp0, gpre0 = _mlp_ple_fwd(z1_0, h1b_0, p[0], None, wts0, vec0, 0, False)

    def x1_fn(z2, pp, gpre, g2, b2, kg, kb):
        x1, _ = _ple_out(z2, pp, gpre, g2, b2)
        return x1, _ln(x1, kg, kb)

    x1, kvn = _rows(x1_fn, [z2_0, pp0, gpre0], [vec0[2], vec0[3], V["kv_ln_g"], V["kv_ln_b"]],
                    [(D, F32), (D, BF16)], [], tm=256, name="ple_out0")

    rot_ep = lambda acc, c_, s_: (_rot(acc, c_, s_),)
    k_rot = _mm(kvn, W["w_kv"], b_sel=0, mode="nn", outs=[BF16], rextras=[cosf, sinf], epilogue=rot_ep, name="kv_k",
                tm=512)
    v_b = _mm(kvn, W["w_kv"], b_sel=1, mode="nn", outs=[BF16], name="kv_v")
    q_rot = _mm(x1, W["attn_w_q"], mode="nn", outs=[BF16], rextras=[cosf, sinf], epilogue=rot_ep, name="attn_q",
                tm=512)
    og, lg = [], []
    for g, dil in enumerate(DILATIONS):
        o_g, l_g = _attn_fwd(q_rot, k_rot, v_b, g, dil)
        og.append(o_g)
        lg.append(l_g)

    def merge(o0, o1, o2, l0, l1, l2):
        m = jnp.maximum(jnp.maximum(l0, l1), l2)
        e0, e1, e2 = jnp.exp(l0 - m), jnp.exp(l1 - m), jnp.exp(l2 - m)
        den = e0 + e1 + e2
        o = (e0 * o0.astype(F32) + e1 * o1.astype(F32) + e2 * o2.astype(F32)) / den
        return o, m + jnp.log(den)

    ob, lse = _rows(merge, og + lg, [], [(D, BF16), (D, F32)], [], tm=256, name="attn_merge")
    mix1 = _mm(ob, W["attn_w_o"], mode="nn", outs=[F32], name="attn_out")
    z1_1, h1b_1 = _rows(z1_fn, [x1, mix1], [vec1[0], vec1[1]], [(D, F32), (D, BF16)], [], tm=256, name="ln1_fwd1")
    t1, act1, z2_1, h2b_1, pp1, gpre1 = _mlp_ple_fwd(z1_1, h1b_1, p[1], None, wts1, vec1, 1, True)

    def head(z2, pp, gpre, tgt, g2, b2):
        y, gt = _ple_out(z2, pp, gpre, g2, b2)
        err = y - tgt
        dy = err * (1.0 / D)
        return dy, dy * gt, dy * pp * gt * (1.0 - gt), 0.5 * err * err * (1.0 / D)

    dy1, d_pp1, d_gpre1, loss_cols = _rows(head, [z2_1, pp1, gpre1, target], [vec1[2], vec1[3]],
                                           [(D, F32), (D, BF16), (D, BF16)], [D], tm=256, name="loss_head")

    dz1_1, dz1b_1, gw1, (gv["ln1_g1"], gv["ln1_b1"], gv["ln2_g1"], gv["ln2_b1"]) = _mlp_ple_bwd(
        dy1, d_pp1, d_gpre1, p[1], z1_1, h1b_1, t1, act1, z2_1, h2b_1, wts1, vec1, 1)
    gw.update(gw1)
    gw["attn_w_o"] = _mm(ob, dz1b_1, mode="tn", outs=[BF16], layout=("row", N_CHIPS), name="d_attn_w_o")

    def do_ep(acc, o_t):
        prod = acc * o_t.astype(F32)
        dlt = _per_head(prod, lambda ph, i: jnp.broadcast_to(jnp.sum(ph, axis=1, keepdims=True), ph.shape))
        return acc, dlt

    do_b, dlt = _mm(dz1b_1, W["attn_w_o"], mode="nt", outs=[BF16, F32], extras=[ob], epilogue=do_ep, name="attn_do",
                    tm=512)
    dqs, dks, dvs = [], [], []
    for g, dil in enumerate(DILATIONS):
        dq_g, dk_g, dv_g = _attn_bwd(q_rot, k_rot, v_b, do_b, lse, dlt, g, dil)
        dqs.append(dq_g)
        dks.append(dk_g)
        dvs.append(dv_g)

    def unrot(q0, q1, q2, k0, k1, k2, v0, v1, v2, c_, s_):
        dq = jnp.concatenate([_rot_t(t_.astype(F32), c_, s_) for t_ in (q0, q1, q2)], axis=1)
        return dq, _rot_t(k0 + k1 + k2, c_, s_), v0 + v1 + v2

    dq, dk, dv = _rows(unrot, dqs + dks + dvs + [cosf, sinf], [], [(N_GROUPS * D, BF16), (D, BF16), (D, BF16)], [],
                       tm=128, name="attn_unrot")
    gw["attn_w_q"] = _mm(x1, dq, mode="tn", outs=[BF16], layout=("col", N_CHIPS), tn=512, name="d_attn_w_q")
    dx1_q = _mm(dq, W["attn_w_q"], mode="nt", outs=[F32], extras=[dz1_1], epilogue=lambda acc, e: (acc + ALPHA * e,),
                name="dx1_q")
    gw["w_kv"] = jnp.concatenate(
        [_mm(kvn, dk, mode="tn", outs=[BF16], layout=("col", 2), name="d_w_kv_k"),
         _mm(kvn, dv, mode="tn", outs=[BF16], layout=("col", 2), name="d_w_kv_v")], axis=1)
    dkvn_k = _mm(dk, W["w_kv"], b_sel=0, mode="nt", outs=[F32], name="dkvn_k")
    dkvn = _mm(dv, W["w_kv"], b_sel=1, mode="nt", outs=[F32], extras=[dkvn_k], epilogue=lambda acc, e: (acc + e,),
               name="dkvn_v")

    def x1_bwd(dx1q_t, dkvn_t, x1_t, pp, gpre, kg):
        n, rstd = _ln_norm(x1_t)
        dy = dx1q_t + _ln_bwd(dkvn_t, n, rstd, kg)
        gt = _sigmoid(gpre)
        return dy, dy * gt, dy * pp * gt * (1.0 - gt), dkvn_t * n, dkvn_t

    dy0, d_pp0, d_gpre0, gv["kv_ln_g"], gv["kv_ln_b"] = _rows(
        x1_bwd, [dx1_q, dkvn, x1, pp0, gpre0], [V["kv_ln_g"]], [(D, F32), (D, BF16), (D, BF16)], [D, D], tm=256,
        name="x1_bwd")

    dz1_0, dz1b_0, gw0, (gv["ln1_g0"], gv["ln1_b0"], gv["ln2_g0"], gv["ln2_b0"]) = _mlp_ple_bwd(
        dy0, d_pp0, d_gpre0, p[0], z1_0, h1b_0, t0, act0, z2_0, h2b_0, wts0, vec0, 0)
    gw.update(gw0)
    gw["conv_w_out"] = _mm(sb, dz1b_0, mode="tn", outs=[BF16], layout=("row", N_CHIPS), name="d_conv_w_out")
    ds = _mm(dz1b_0, W["conv_w_out"], mode="nt", outs=[F32], name="conv_ds")

    def conv_ln_bwd(ds_t, c_t, g_, b_):
        n, rstd = _ln_norm(c_t)
        y = n * g_ + b_
        sg = _sigmoid(y)
        dln = ds_t * sg * (1.0 + y * (1.0 - sg))
        dc = _ln_bwd(dln, n, rstd, g_)
        return dc, dln * n, dln, dc

    dc, gv["conv_ln_g"], gv["conv_ln_b"], gv["conv_dw_b"] = _rows(
        conv_ln_bwd, [ds, cv], [V["conv_ln_g"], V["conv_ln_b"]], [(D, F32)], [D, D, D], tm=256, name="conv_ln_bwd")
    da, dg, gv["conv_dw"], gv["conv_b_a"], gv["conv_b_g"] = _conv_bwd(
        dc, glu, a_pre, g_pre, V["conv_dw"], V["conv_b_a"], V["conv_b_g"])
    gw["conv_w_in"] = jnp.concatenate(
        [_mm(x, da, mode="tn", outs=[BF16], layout=("col", 2), name="d_conv_w_in_a"),
         _mm(x, dg, mode="tn", outs=[BF16], layout=("col", 2), name="d_conv_w_in_g")], axis=1)
    dx_a = _mm(da, W["conv_w_in"], b_sel=0, mode="nt", outs=[F32], extras=[dz1_0],
               epilogue=lambda acc, e: (acc + ALPHA * e,), name="dx_a")
    grad_x = _mm(dg, W["conv_w_in"], b_sel=1, mode="nt", outs=[F32], extras=[dx_a], epilogue=lambda acc, e: (acc + e,),
                 name="dx_g")
    return loss_cols, grad_x, gw, gv


def _place():
    x, y, c = lax.axis_index("x"), lax.axis_index("y"), lax.axis_index("c")
    chips = [(1 - x, y), (x, 1 - y), (1 - x, 1 - y)]
    return x, y, c, chips


def _remote(src, dst, ssem, rsem, dev):
    return pltpu.make_async_remote_copy(src_ref=src, dst_ref=dst, send_sem=ssem, recv_sem=rsem, device_id=dev,
                                        device_id_type=MESH)


def _allgather8(block, name):
    R, C = block.shape

    def body(x_ref, out_ref, send_sems, recv_sems, local_sem):
        x, y, c, chips = _place()
        me, sibling = (x, y, c), (x, y, 1 - c)

        def slot(px, py, pc):
            return out_ref.at[4 * px + 2 * py + pc]

        def copy(k, blockpos, to, src=None):
            return _remote(slot(*blockpos) if src is None else src, slot(*blockpos), send_sems.at[k], recv_sems.at[k], to)

        mine = pltpu.make_async_copy(x_ref, slot(*me), local_sem)
        mine.start()
        first = [copy(0, me, sibling, src=x_ref)]
        first += [copy(1 + j, me, (*chip, c), src=x_ref) for j, chip in enumerate(chips)]
        for cp in first:
            cp.start()
        passed = [copy(4 + j, (*chip, c), sibling) for j, chip in enumerate(chips)]
        for j, chip in enumerate(chips):
            copy(1 + j, (*chip, c), me).wait_recv()
            passed[j].start()
        copy(0, sibling, me).wait_recv()
        for j, chip in enumerate(chips):
            copy(4 + j, (*chip, 1 - c), me).wait_recv()
        for cp in first + passed:
            cp.wait_send()
        mine.wait()

    return pl.pallas_call(
        body, name=name, out_shape=jax.ShapeDtypeStruct((8, R, C), block.dtype),
        in_specs=[pl.BlockSpec(memory_space=pltpu.VMEM)], out_specs=pl.BlockSpec(memory_space=pltpu.VMEM),
        scratch_shapes=[pltpu.SemaphoreType.DMA((7,)), pltpu.SemaphoreType.DMA((7,)), pltpu.SemaphoreType.DMA],
        compiler_params=pltpu.CompilerParams(vmem_limit_bytes=_vmem(10 * _nbytes((R, C), block.dtype))),
    )(block)


_MATS = (
    ("conv_w_in", "conv_w_in", 0, "col", True),
    ("conv_w_out", "conv_w_out", 0, "row", False),
    ("mlp_up0", "mlp_up", 0, "col", False),
    ("mlp_down0", "mlp_down", 0, "row", False),
    ("ple_proj0", "ple_proj", 0, "col", False),
    ("ple_gate0", "ple_gate", 0, "row", False),
    ("w_kv", "w_kv", None, "col", True),
    ("attn_w_q", "attn_w_q", 0, "col", False),
    ("attn_w_o", "attn_w_o", 0, "row", False),
    ("mlp_up1", "mlp_up", 1, "col", False),
    ("mlp_down1", "mlp_down", 1, "row", False),
    ("ple_proj1", "ple_proj", 1, "col", False),
    ("ple_gate1", "ple_gate", 1, "row", False),
)


def _gather_weights(shards):
    names = sorted({m[1] for m in _MATS})
    ins = [shards[n] for n in names]
    out_shape, geo = [], []
    for name, src, layer, kind, split in _MATS:
        s = shards[src]
        ks, ns = s.shape[-2:]
        K, N = (ks, ns * N_CHIPS) if kind == "col" else (ks * N_CHIPS, ns)
        out_shape.append(jax.ShapeDtypeStruct((2, K, N // 2) if split else (K, N), BF16))
        geo.append((names.index(src), layer if s.ndim == 3 else None, kind, split, K, N))
    T = len(_MATS)

    def body(*refs):
        in_refs, out_refs = refs[:len(ins)], refs[len(ins):len(ins) + T]
        s_ici, r_ici, s_d2d, r_d2d, lsem = refs[len(ins) + T:]
        x, y, c, chips = _place()
        me = 2 * x + y
        sibling = (x, y, 1 - c)
        idx = [2 * cx + cy for cx, cy in chips]

        def src_ref(t):
            i, layer, _, _, _, _ = geo[t]
            return in_refs[i] if layer is None else in_refs[i].at[layer]

        def src_half(t, h):
            _, _, kind, _, K, N = geo[t]
            if kind == "col":
                return src_ref(t).at[pl.ds(h * (K // 2), K // 2), :]
            return src_ref(t).at[:, pl.ds(h * (N // 2), N // 2)]

        def dst(t, j, h):
            _, _, kind, split, K, N = geo[t]
            n, k = N // N_CHIPS, K // N_CHIPS
            if kind == "col":
                rows = slice(None) if h is None else pl.ds(h * (K // 2), K // 2)
                if split:
                    return out_refs[t].at[j // 2, rows, pl.ds((j % 2) * n, n)]
                return out_refs[t].at[rows, pl.ds(j * n, n)]
            cols = slice(None) if h is None else pl.ds(h * (N // 2), N // 2)
            return out_refs[t].at[pl.ds(j * k, k), cols]

        local = [pltpu.make_async_copy(src_ref(t), dst(t, me, None), lsem.at[t]) for t in range(T)]
        for cp in local:
            cp.start()
        sends = [_remote(src_half(t, c), dst(t, me, c), s_ici.at[3 * t + kk], r_ici.at[3 * t + kk], (*chips[kk], c))
                 for t in range(T) for kk in range(3)]
        for cp in sends:
            cp.start()
        passed = []
        for t in range(T):
            for kk in range(3):
                landed = dst(t, idx[kk], c)
                _remote(landed, landed, s_ici.at[3 * t + kk], r_ici.at[3 * t + kk], sibling).wait_recv()
                fwd = _remote(landed, landed, s_d2d.at[3 * t + kk], r_d2d.at[3 * t + kk], sibling)
                fwd.start()
                passed.append(fwd)
        for t in range(T):
            for kk in range(3):
                landed = dst(t, idx[kk], 1 - c)
                _remote(landed, landed, s_d2d.at[3 * t + kk], r_d2d.at[3 * t + kk], sibling).wait_recv()
        for cp in sends + passed:
            cp.wait_send()
        for cp in local:
            cp.wait()

    outs = pl.pallas_call(
        body, name="gather_weights", out_shape=out_shape, in_specs=[ANY] * len(ins), out_specs=[ANY] * T,
        scratch_shapes=[pltpu.SemaphoreType.DMA((3 * T,)) for _ in range(4)] + [pltpu.SemaphoreType.DMA((T,))],
    )(*ins)
    return {m[0]: o for m, o in zip(_MATS, outs)}


def _sibling_exchange(gw, c_arr):
    names = [m[0] for m in _MATS]
    ins = [gw[n] for n in names]
    T = len(names)

    def body(*refs):
        in_refs, out_refs = refs[:T], refs[T:2 * T]
        ssem, rsem = refs[2 * T:]
        x, y, c, _ = _place()
        sibling = (x, y, 1 - c)
        cps = [_remote(in_refs[t].at[1 - c], out_refs[t], ssem.at[t], rsem.at[t], sibling) for t in range(T)]
        for cp in cps:
            cp.start()
        for cp in cps:
            cp.wait()

    outs = pl.pallas_call(
        body, name="grad_sibling_exchange", out_shape=[jax.ShapeDtypeStruct(a.shape[1:], BF16) for a in ins],
        in_specs=[ANY] * T, out_specs=[ANY] * T,
        scratch_shapes=[pltpu.SemaphoreType.DMA((T,)), pltpu.SemaphoreType.DMA((T,))],
    )(*ins)
    return dict(zip(names, outs))


def _pair_sum(own, landed, c_arr, name):
    _, ns, r, cc = own.shape
    rows = ns * r
    tr = min(512, rows)
    assert rows % tr == 0

    def body(c_ref, a_ref, b_ref, o_ref):
        o_ref[...] = (a_ref[...].astype(F32) + b_ref[...].astype(F32)).astype(o_ref.dtype)

    out = pl.pallas_call(
        body, name=name, out_shape=jax.ShapeDtypeStruct((rows, cc), BF16),
        grid_spec=pltpu.PrefetchScalarGridSpec(
            num_scalar_prefetch=1, grid=(rows // tr,),
            in_specs=[pl.BlockSpec((None, tr, cc), lambda i, c_ref: (c_ref[0], i, 0)),
                      pl.BlockSpec((tr, cc), lambda i, c_ref: (i, 0))],
            out_specs=pl.BlockSpec((tr, cc), lambda i, c_ref: (i, 0))),
        compiler_params=pltpu.CompilerParams(dimension_semantics=("parallel",), vmem_limit_bytes=_vmem(8 * tr * cc * 4)),
    )(c_arr, own.reshape(2, rows, cc), landed.reshape(rows, cc))
    return out.reshape(ns, r, cc)


def _chip_scatter(ps):
    names = [m[0] for m in _MATS]
    ins = [ps[n] for n in names]
    T = len(names)

    def body(*refs):
        in_refs, out_refs = refs[:T], refs[T:2 * T]
        ssem, rsem, lsem = refs[2 * T:]
        x, y, c, chips = _place()
        me = 2 * x + y
        idx = [2 * cx + cy for cx, cy in chips]
        local = [pltpu.make_async_copy(in_refs[t].at[me], out_refs[t].at[me], lsem.at[t]) for t in range(T)]
        for cp in local:
            cp.start()
        sends = [_remote(in_refs[t].at[idx[kk]], out_refs[t].at[me], ssem.at[3 * t + kk], rsem.at[3 * t + kk],
                         (*chips[kk], c)) for t in range(T) for kk in range(3)]
        for cp in sends:
            cp.start()
        for t in range(T):
            for kk in range(3):
                landed = out_refs[t].at[idx[kk]]
                _remote(landed, landed, ssem.at[3 * t + kk], rsem.at[3 * t + kk], (*chips[kk], c)).wait_recv()
        for cp in sends:
            cp.wait_send()
        for cp in local:
            cp.wait()

    outs = pl.pallas_call(
        body, name="grad_chip_scatter", out_shape=[jax.ShapeDtypeStruct(a.shape, BF16) for a in ins],
        in_specs=[ANY] * T, out_specs=[ANY] * T,
        scratch_shapes=[pltpu.SemaphoreType.DMA((3 * T,)), pltpu.SemaphoreType.DMA((3 * T,)),
                        pltpu.SemaphoreType.DMA((T,))],
    )(*ins)
    return dict(zip(names, outs))


def _chip_sum(parts, c_arr, name):
    _, r, cc = parts.shape
    tr = min(256, r)
    assert r % tr == 0

    def body(c_ref, p_ref, o_ref):
        acc = p_ref[0].astype(F32)
        for j in range(1, N_CHIPS):
            acc = acc + p_ref[j].astype(F32)
        o_ref[...] = acc

    return pl.pallas_call(
        body, name=name, out_shape=jax.ShapeDtypeStruct((2, r, cc), F32),
        grid_spec=pltpu.PrefetchScalarGridSpec(
            num_scalar_prefetch=1, grid=(r // tr,),
            in_specs=[pl.BlockSpec((N_CHIPS, tr, cc), lambda i, c_ref: (0, i, 0))],
            out_specs=pl.BlockSpec((None, tr, cc), lambda i, c_ref: (c_ref[0], i, 0))),
        compiler_params=pltpu.CompilerParams(dimension_semantics=("parallel",), vmem_limit_bytes=_vmem(12 * tr * cc * 4)),
    )(c_arr, parts)


def _sibling_share(gh):
    names = [m[0] for m in _MATS]
    ins = [gh[n] for n in names]
    T = len(names)

    def body(*refs):
        out_refs = refs[T:2 * T]
        ssem, rsem = refs[2 * T:]
        x, y, c, _ = _place()
        sibling = (x, y, 1 - c)
        cps = [_remote(out_refs[t].at[c], out_refs[t].at[c], ssem.at[t], rsem.at[t], sibling) for t in range(T)]
        for cp in cps:
            cp.start()
        for t in range(T):
            other = out_refs[t].at[1 - c]
            _remote(other, other, ssem.at[t], rsem.at[t], sibling).wait_recv()
        for cp in cps:
            cp.wait_send()

    outs = pl.pallas_call(
        body, name="grad_sibling_share", out_shape=[jax.ShapeDtypeStruct(a.shape, F32) for a in ins],
        in_specs=[ANY] * T, out_specs=[ANY] * T, input_output_aliases={t: t for t in range(T)},
        scratch_shapes=[pltpu.SemaphoreType.DMA((T,)), pltpu.SemaphoreType.DMA((T,))],
    )(*ins)
    return dict(zip(names, outs))


def _adamw_math(w, g, m, v):
    m2 = ADAM_B1 * m + (1.0 - ADAM_B1) * g
    v2 = ADAM_B2 * v + (1.0 - ADAM_B2) * jnp.square(g)
    m_hat = m2 / (1.0 - ADAM_B1 ** ADAM_STEP)
    v_hat = v2 / (1.0 - ADAM_B2 ** ADAM_STEP)
    delta = -ADAM_LR * (m_hat / (jnp.sqrt(v_hat) + ADAM_EPS) + ADAM_WD * w)
    return delta, m2, v2


def _adamw_mat(g2, w, m, v, layer, kind, prev, name):
    shape = w.shape
    ks, ns = shape[-2:]
    _, r, cc = g2.shape
    tr, tc = _fit(256, r), _fit(1024, cc)
    assert (r, cc) == ((ks // 2, ns) if kind == "col" else (ks, ns // 2))
    assert r % tr == 0 and cc % tc == 0
    rb, cb = r // tr, cc // tc
    if kind == "col":
        g_spec = pl.BlockSpec((None, tr, tc), lambda i, j: (i // rb, i % rb, j))
    else:
        g_spec = pl.BlockSpec((None, tr, tc), lambda i, j: (j // cb, i, j % cb))
    if w.ndim == 3:
        w_spec = pl.BlockSpec((None, tr, tc), lambda i, j: (layer, i, j))
    else:
        w_spec = pl.BlockSpec((tr, tc), lambda i, j: (i, j))
    n_prev = 0 if prev is None else 4

    def body(*refs):
        g_ref, w_ref, m_ref, v_ref = refs[:4]
        go_ref, d_ref, mo_ref, vo_ref = refs[4 + n_prev:]
        g = g_ref[...]
        delta, m2, v2 = _adamw_math(w_ref[...], g, m_ref[...], v_ref[...])
        go_ref[...] = g
        d_ref[...] = delta
        mo_ref[...] = m2
        vo_ref[...] = v2

    return pl.pallas_call(
        body, name=name, grid=(ks // tr, ns // tc),
        in_specs=[g_spec, w_spec, w_spec, w_spec] + [ANY] * n_prev, out_specs=[w_spec] * 4,
        out_shape=[jax.ShapeDtypeStruct(shape, F32)] * 4,
        input_output_aliases={4 + i: i for i in range(n_prev)},
        compiler_params=pltpu.CompilerParams(dimension_semantics=("parallel", "parallel"),
                                             vmem_limit_bytes=_vmem(16 * tr * tc * 4)),
    )(g2, w, m, v, *(prev or ()))


def _adamw_small(g, w, m, v, name):
    def body(g_ref, w_ref, m_ref, v_ref, d_ref, mo_ref, vo_ref):
        delta, m2, v2 = _adamw_math(w_ref[...], g_ref[...], m_ref[...], v_ref[...])
        d_ref[...] = delta
        mo_ref[...] = m2
        vo_ref[...] = v2

    return pl.pallas_call(body, name=name, out_shape=[jax.ShapeDtypeStruct(w.shape, F32)] * 3)(g, w, m, v)


def _sum8(parts, name):
    def body(p_ref, o_ref):
        acc = p_ref[0]
        for j in range(1, 8):
            acc = acc + p_ref[j]
        o_ref[...] = acc

    return pl.pallas_call(body, name=name, out_shape=jax.ShapeDtypeStruct(parts.shape[1:], F32),
                          compiler_params=pltpu.CompilerParams(vmem_limit_bytes=_vmem(12 * _nbytes(parts.shape[1:], F32))))(parts)


_WEIGHTS = ("conv_w_in", "conv_b_in", "conv_dw", "conv_dw_b", "conv_ln_g", "conv_ln_b", "conv_w_out", "kv_ln_g",
            "kv_ln_b", "w_kv", "attn_w_q", "attn_w_o", "ln1_g", "ln1_b", "mlp_up", "mlp_down", "ln2_g", "ln2_b",
            "ple_proj", "ple_gate")
_SHARDED_VECS = ("conv_b_in", "conv_dw", "conv_dw_b", "conv_ln_g", "conv_ln_b")
_REPLICATED_VECS = ("kv_ln_g", "kv_ln_b", "ln1_g", "ln1_b", "ln2_g", "ln2_b")


def _pad_rows(a, rows):
    return jnp.concatenate([a, jnp.zeros((rows - a.shape[0], a.shape[1]), a.dtype)], axis=0) if a.shape[0] < rows else a


def _pack_sharded(d):
    n = d["conv_dw_b"].shape[-1]
    rows = [d["conv_b_in"].reshape(2, n), d["conv_dw"].reshape(CONV_WIDTH, n), d["conv_dw_b"].reshape(1, n),
            d["conv_ln_g"].reshape(1, n), d["conv_ln_b"].reshape(1, n)]
    return _pad_rows(jnp.concatenate(rows, axis=0), 40)


def _unpack_sharded(pack, like):
    n = pack.shape[1]
    return {"conv_b_in": pack[0:2].reshape(like["conv_b_in"].shape),
            "conv_dw": pack[2:2 + CONV_WIDTH].reshape(like["conv_dw"].shape),
            "conv_dw_b": pack[33:34].reshape(like["conv_dw_b"].shape),
            "conv_ln_g": pack[34:35].reshape(like["conv_ln_g"].shape),
            "conv_ln_b": pack[35:36].reshape(like["conv_ln_b"].shape)}


def _pack_replicated(d):
    D = d["kv_ln_g"].shape[-1]
    rows = [d[n].reshape(-1, D) for n in _REPLICATED_VECS]
    return _pad_rows(jnp.concatenate(rows, axis=0), 16)


def _unpack_replicated(pack, like):
    out, r = {}, 0
    for n in _REPLICATED_VECS:
        k = like[n].size // pack.shape[1]
        out[n] = pack[r:r + k].reshape(like[n].shape)
        r += k
    return out


def kernel(x, p, positions, conv_w_in, conv_b_in, conv_dw, conv_dw_b, conv_ln_g, conv_ln_b, conv_w_out, kv_ln_g, kv_ln_b, w_kv, attn_w_q, attn_w_o, ln1_g, ln1_b, mlp_up, mlp_down, ln2_g, ln2_b, ple_proj, ple_gate, loss_target, m_conv_w_in, m_conv_b_in, m_conv_dw, m_conv_dw_b, m_conv_ln_g, m_conv_ln_b, m_conv_w_out, m_kv_ln_g, m_kv_ln_b, m_w_kv, m_attn_w_q, m_attn_w_o, m_ln1_g, m_ln1_b, m_mlp_up, m_mlp_down, m_ln2_g, m_ln2_b, m_ple_proj, m_ple_gate, v_conv_w_in, v_conv_b_in, v_conv_dw, v_conv_dw_b, v_conv_ln_g, v_conv_ln_b, v_conv_w_out, v_kv_ln_g, v_kv_ln_b, v_w_kv, v_attn_w_q, v_attn_w_o, v_ln1_g, v_ln1_b, v_mlp_up, v_mlp_down, v_ln2_g, v_ln2_b, v_ple_proj, v_ple_gate):
    args = dict(locals())
    w = {n: args[n] for n in _WEIGHTS}
    mom = {n: args["m_" + n] for n in _WEIGHTS}
    var = {n: args["v_" + n] for n in _WEIGHTS}
    S, D = x.shape[1:]
    n4 = D // N_CHIPS
    chip = 2 * lax.axis_index("x") + lax.axis_index("y")
    c_arr = lax.axis_index("c").astype(jnp.int32).reshape(1)

    mats = sorted({m[1] for m in _MATS})
    W = _gather_weights({n: w[n].astype(BF16) for n in mats})
    vec_all = _allgather8(_pack_sharded(w), "gather_vectors")
    vec_full = jnp.concatenate([vec_all[2 * j] for j in range(N_CHIPS)], axis=1)
    b_in = vec_all[0::2, 0:2, :].reshape(1, 2 * D)
    V = {"conv_b_a": b_in[:, :D], "conv_b_g": b_in[:, D:],
         "conv_dw": _pad_rows(vec_full[2:2 + CONV_WIDTH], CONV_PAD), "conv_dw_b": vec_full[33:34],
         "conv_ln_g": vec_full[34:35], "conv_ln_b": vec_full[35:36],
         "kv_ln_g": kv_ln_g.reshape(1, D), "kv_ln_b": kv_ln_b.reshape(1, D)}
    for l in range(2):
        for n in ("ln1_g", "ln1_b", "ln2_g", "ln2_b"):
            V[f"{n}{l}"] = w[n][l].reshape(1, D)

    half = HEAD_DIM // 2
    inv_freq = ROPE_THETA ** (-jnp.arange(half, dtype=F32) * (2.0 / HEAD_DIM))
    ang = positions[0].astype(F32)[:, None] * inv_freq
    cos, sin = jnp.cos(ang), jnp.sin(ang)
    cosf = jnp.concatenate([cos, cos], axis=-1)
    sinf = jnp.concatenate([-sin, sin], axis=-1)

    loss_cols, grad_x, gw, gv = _local_step(x[0], p[:, 0], cosf, sinf, loss_target[0], W, V)
    loss = lax.psum(jnp.sum(loss_cols), ("x", "y", "c"))

    landed = _sibling_exchange(gw, c_arr)
    ps = {n: _pair_sum(gw[n], landed[n], c_arr, f"pair_sum_{n}") for n in gw}
    parts = _chip_scatter(ps)
    gh = {n: _chip_sum(parts[n], c_arr, f"chip_sum_{n}") for n in parts}
    g2 = _sibling_share(gh)

    out = {}
    for name, src, layer, kind, split in _MATS:
        out[src] = _adamw_mat(g2[name], w[src], mom[src], var[src], layer or 0, kind, out.get(src), f"adamw_{name}")

    gpack = jnp.concatenate([gv["conv_b_a"], gv["conv_b_g"], gv["conv_dw"][:CONV_WIDTH], gv["conv_dw_b"],
                             gv["conv_ln_g"], gv["conv_ln_b"], gv["kv_ln_g"], gv["kv_ln_b"],
                             gv["ln1_g0"], gv["ln1_g1"], gv["ln1_b0"], gv["ln1_b1"],
                             gv["ln2_g0"], gv["ln2_g1"], gv["ln2_b0"], gv["ln2_b1"]], axis=0)
    gsum = _sum8(_allgather8(_pad_rows(gpack, 48), "gather_vector_grads"), "sum_vector_grads")
    g_b = lax.dynamic_slice_in_dim(jnp.concatenate([gsum[0:1], gsum[1:2]], axis=1), chip * 2 * n4, 2 * n4, axis=1)
    g_sh = lax.dynamic_slice_in_dim(gsum[2:36], chip * n4, n4, axis=1)
    g_sh = _pad_rows(jnp.concatenate([g_b.reshape(2, n4), g_sh], axis=0), 40)
    d_sh, m_sh, v_sh = _adamw_small(g_sh, _pack_sharded(w), _pack_sharded(mom), _pack_sharded(var), "adamw_sharded_vectors")
    g_rep = _pad_rows(gsum[36:46], 16)
    d_rep, m_rep, v_rep = _adamw_small(g_rep, _pack_replicated(w), _pack_replicated(mom), _pack_replicated(var),
                                       "adamw_replicated_vectors")
    small = {}
    for i, (sh, rep) in enumerate(((g_sh, g_rep), (d_sh, d_rep), (m_sh, m_rep), (v_sh, v_rep))):
        d = {**_unpack_sharded(sh, w), **_unpack_replicated(rep, w)}
        for n, val in d.items():
            small.setdefault(n, [None] * 4)[i] = val
    for n in small:
        out[n] = small[n]

    res = [loss, grad_x[None]]
    for i in range(4):
        res += [out[n][i] for n in _WEIGHTS]
    return tuple(res)
```

```python
import functools

import jax
import jax.numpy as jnp
from jax import lax
from jax.experimental import pallas as pl
from jax.experimental.pallas import tpu as pltpu

F32 = jnp.float32
BF16 = jnp.bfloat16

HEAD_DIM = 128
ATTN_BLOCK = 128
DILATIONS = (1, 4, 16)
N_GROUPS = 3
CONV_WIDTH = 31
CONV_PAD = 32
ROPE_THETA = 10000.0
LN_EPS = 1e-5
ALPHA = 4.0 ** 0.25
ATTN_SCALE = HEAD_DIM ** -0.5
NEG = -1e30

ADAM_LR = 0.001
ADAM_B1 = 0.9
ADAM_B2 = 0.999
ADAM_EPS = 1e-08
ADAM_WD = 0.01
ADAM_STEP = 10

N_CHIPS = 4
VMEM_CAP = 60 << 20
MESH = pl.DeviceIdType.MESH
ANY = pl.BlockSpec(memory_space=pl.ANY)


def _vmem(nbytes):
    return int(min(max(2 * nbytes + (8 << 20), 24 << 20), VMEM_CAP))


def _fit(tile, n):
    if n <= tile:
        return n
    t = tile - tile % 128
    while n % t:
        t -= 128
    return t


def _nbytes(shape, dtype):
    n = 1
    for s in shape:
        n *= s
    return n * jnp.dtype(dtype).itemsize


_DIMS = {"nn": (((1,), (0,)), ((), ())), "nt": (((1,), (1,)), ((), ())), "tn": (((0,), (0,)), ((), ()))}


def _pcall(body, *, name, grid, in_specs, out_specs, out_shape, operands, scratch_shapes=(), vmem, carry=None):
    if carry is None:
        return pl.pallas_call(
            body, name=name, grid=grid, in_specs=in_specs, out_specs=out_specs, out_shape=out_shape,
            scratch_shapes=list(scratch_shapes),
            compiler_params=pltpu.CompilerParams(dimension_semantics=("arbitrary",) * len(grid), vmem_limit_bytes=vmem),
        )(*operands)
    n_in, n_out, n_scr = len(in_specs), len(out_specs), len(scratch_shapes)
    c_in, c_out = len(carry.ins), len(carry.out_shape)

    def wrapped(*refs):
        ins, refs = refs[:n_in], refs[n_in:]
        c_ins, refs = refs[:c_in], refs[c_in:]
        outs, refs = refs[:n_out], refs[n_out:]
        c_outs, refs = refs[:c_out], refs[c_out:]
        scr, c_sems = refs[:n_scr], refs[n_scr:]
        first = functools.reduce(jnp.logical_and, [pl.program_id(d) == 0 for d in range(len(grid))])
        last = functools.reduce(jnp.logical_and, [pl.program_id(d) == grid[d] - 1 for d in range(len(grid))])
        pl.when(first)(lambda: carry.start(c_ins, c_outs, c_sems))
        body(*ins, *outs, *scr)
        pl.when(last)(lambda: carry.finish(c_ins, c_outs, c_sems))

    res = pl.pallas_call(
        wrapped, name=name, grid=grid, in_specs=list(in_specs) + [ANY] * c_in, out_specs=list(out_specs) + [ANY] * c_out,
        out_shape=list(out_shape) + list(carry.out_shape), scratch_shapes=list(scratch_shapes) + list(carry.scratch),
        compiler_params=pltpu.CompilerParams(dimension_semantics=("arbitrary",) * len(grid), vmem_limit_bytes=vmem),
    )(*operands, *carry.ins)
    carry.result = dict(zip(carry.names, res[n_out:]))
    return res[:n_out]


def _mm(a, b, *, mode, outs, name, epilogue=None, extras=(), rextras=(), vecs=(), a_sel=None, b_sel=None,
        tm=None, tn=2048, tk=None, layout=None, carry=None):
    a2, b2 = a.shape[-2:], b.shape[-2:]
    if mode == "nn":
        (M, K), (K2, N) = a2, b2
    elif mode == "nt":
        (M, K), (N, K2) = a2, b2
    else:
        (K, M), (K2, N) = a2, b2
    assert K == K2, (a.shape, b.shape, mode)
    if tm is None:
        tm = 1024 if mode == "tn" else 512
    if tk is None:
        tk = 1024 if mode == "tn" else 2048
    if layout is not None:
        kind, nslots = layout
        r, c = (M // 2, N // nslots) if kind == "col" else (M // nslots, N // 2)
        tm, tn = _fit(tm, r), _fit(tn, c)
        assert r % tm == 0 and c % tn == 0
    else:
        tm, tn = _fit(tm, M), _fit(tn, N)
    tk = _fit(tk, K)
    assert M % tm == 0 and N % tn == 0 and K % tk == 0, (M, N, K, tm, tn, tk)
    nk = K // tk
    grid = (N // tn, M // tm, nk)

    def spec(arr, sel, blk, imap):
        if arr.ndim == 3:
            return pl.BlockSpec((None,) + blk, lambda j, i, k: (sel,) + imap(j, i, k))
        return pl.BlockSpec(blk, imap)

    if mode == "tn":
        a_spec = spec(a, a_sel, (tk, tm), lambda j, i, k: (k, i))
    else:
        a_spec = spec(a, a_sel, (tm, tk), lambda j, i, k: (i, k))
    if mode == "nt":
        b_spec = spec(b, b_sel, (tn, tk), lambda j, i, k: (j, k))
    else:
        b_spec = spec(b, b_sel, (tk, tn), lambda j, i, k: (k, j))
    in_specs = [a_spec, b_spec]
    in_specs += [pl.BlockSpec((tm, tn), lambda j, i, k: (i, j)) for _ in extras]
    in_specs += [pl.BlockSpec((tm, e.shape[1]), lambda j, i, k: (i, 0)) for e in rextras]
    in_specs += [pl.BlockSpec((1, tn), lambda j, i, k: (0, j)) for _ in vecs]

    if layout is None:
        out_shape = [jax.ShapeDtypeStruct((M, N), d) for d in outs]
        out_specs = [pl.BlockSpec((tm, tn), lambda j, i, k: (i, j)) for _ in outs]
    else:
        assert len(outs) == 1
        out_shape = [jax.ShapeDtypeStruct((2, nslots, r, c), outs[0])]
        rb, cb = r // tm, c // tn
        if kind == "col":
            omap = lambda j, i, k: (i // rb, j // cb, i % rb, j % cb)
        else:
            omap = lambda j, i, k: (j // cb, i // rb, i % rb, j % cb)
        out_specs = [pl.BlockSpec((None, None, tm, tn), omap)]

    ne, nr, nv, no = len(extras), len(rextras), len(vecs), len(outs)
    dims = _DIMS[mode]

    def body(*refs):
        a_ref, b_ref = refs[0], refs[1]
        rest = refs[2:2 + ne + nr + nv]
        o_refs = refs[2 + ne + nr + nv:2 + ne + nr + nv + no]

        def finish(total):
            res = (total,) * no if epilogue is None else epilogue(total, *[x[...] for x in rest])
            for o, val in zip(o_refs, res):
                o[...] = val.astype(o.dtype)

        part = lax.dot_general(a_ref[...].astype(BF16), b_ref[...].astype(BF16), dims, preferred_element_type=F32)
        if nk == 1:
            finish(part)
            return
        acc = refs[-1]
        k = pl.program_id(2)

        @pl.when(k == 0)
        def _():
            acc[...] = part

        @pl.when((k > 0) & (k < nk - 1))
        def _():
            acc[...] += part

        @pl.when(k == nk - 1)
        def _():
            finish(acc[...] + part)

    blk = (_nbytes((tm, tk), a.dtype) + _nbytes((tk, tn), b.dtype) + sum(_nbytes((tm, tn), e.dtype) for e in extras)
           + sum(_nbytes((tm, tn), d) for d in outs) + 2 * tm * tn * 4)
    res = _pcall(body, name=name, grid=grid, in_specs=in_specs, out_specs=out_specs, out_shape=out_shape,
                 operands=(a, b, *extras, *rextras, *vecs),
                 scratch_shapes=[pltpu.VMEM((tm, tn), F32)] if nk > 1 else [], vmem=_vmem(blk), carry=carry)
    return res[0] if no == 1 else tuple(res)


def _rows(fn, rows, vecs, outs, sums, *, tm, name, carry=None):
    S = rows[0].shape[0]
    tm = min(tm, S)
    assert S % tm == 0
    nr, nv, no, ns = len(rows), len(vecs), len(outs), len(sums)

    def body(*refs):
        vals = fn(*[r[...] for r in refs[:nr + nv]])
        o_refs = refs[nr + nv:nr + nv + no]
        s_refs = refs[nr + nv + no:]
        for o, val in zip(o_refs, vals[:no]):
            o[...] = val.astype(o.dtype)
        if ns:
            @pl.when(pl.program_id(0) == 0)
            def _():
                for s in s_refs:
                    s[...] = jnp.zeros_like(s)

            for s, val in zip(s_refs, vals[no:]):
                s[...] += jnp.sum(val.astype(F32), axis=0, keepdims=True)

    in_specs = [pl.BlockSpec((tm, r.shape[1]), lambda i: (i, 0)) for r in rows]
    in_specs += [pl.BlockSpec(v.shape, lambda i: (0, 0)) for v in vecs]
    out_specs = [pl.BlockSpec((tm, c), lambda i: (i, 0)) for c, _ in outs]
    out_specs += [pl.BlockSpec((1, c), lambda i: (0, 0)) for c in sums]
    out_shape = [jax.ShapeDtypeStruct((S, c), d) for c, d in outs]
    out_shape += [jax.ShapeDtypeStruct((1, c), F32) for c in sums]
    blk = sum(_nbytes((tm, r.shape[1]), r.dtype) for r in rows) + sum(_nbytes((tm, c), d) for c, d in outs)
    blk += 6 * tm * max(r.shape[1] for r in rows) * 4
    res = _pcall(body, name=name, grid=(S // tm,), in_specs=in_specs, out_specs=out_specs, out_shape=out_shape,
                 operands=(*rows, *vecs), vmem=_vmem(blk), carry=carry)
    return tuple(res)


def _ln_norm(z):
    mu = jnp.mean(z, axis=-1, keepdims=True)
    d = z - mu
    var = jnp.mean(d * d, axis=-1, keepdims=True)
    rstd = lax.rsqrt(var + LN_EPS)
    return d * rstd, rstd


def _ln(z, g, b):
    return _ln_norm(z)[0] * g + b


def _ln_bwd(dy, n, rstd, g):
    dn = dy * g
    return rstd * (dn - jnp.mean(dn, axis=-1, keepdims=True) - n * jnp.mean(dn * n, axis=-1, keepdims=True))


def _sigmoid(x):
    return 1.0 / (1.0 + jnp.exp(-x))


def _per_head(x, fn):
    h = x.shape[1] // HEAD_DIM
    return jnp.concatenate([fn(x[:, i * HEAD_DIM:(i + 1) * HEAD_DIM], i) for i in range(h)], axis=1)


def _rot(x, cosf, sinf):
    return _per_head(x, lambda xh, i: xh * cosf + pltpu.roll(xh, HEAD_DIM // 2, 1) * sinf)


def _rot_t(dy, cosf, sinf):
    return _per_head(dy, lambda dh, i: dh * cosf + pltpu.roll(dh * sinf, HEAD_DIM // 2, 1))


def _conv_fwd(glu, dw, dwb, *, tm=256, tc=512, name="conv_fwd", carry=None):
    S, D = glu.shape
    tm, tc = min(tm, S), min(tc, D)
    ni = S // tm

    def body(cur_ref, prev_ref, dw_ref, dwb_ref, o_ref, win):
        i = pl.program_id(1)
        tail = prev_ref[tm - CONV_PAD:tm, :]
        win[0:CONV_PAD, :] = jnp.where(i > 0, tail, jnp.zeros_like(tail))
        win[CONV_PAD:CONV_PAD + tm, :] = cur_ref[...]
        acc = jnp.zeros((tm, tc), F32) + dwb_ref[...]
        for k in range(CONV_WIDTH):
            acc = acc + win[pl.ds(CONV_PAD - CONV_WIDTH + 1 + k, tm), :] * dw_ref[k:k + 1, :]
        o_ref[...] = acc

    return _pcall(
        body, name=name, grid=(D // tc, ni),
        in_specs=[pl.BlockSpec((tm, tc), lambda j, i: (i, j)),
                  pl.BlockSpec((tm, tc), lambda j, i: (jnp.maximum(i - 1, 0), j)),
                  pl.BlockSpec((CONV_PAD, tc), lambda j, i: (0, j)),
                  pl.BlockSpec((1, tc), lambda j, i: (0, j))],
        out_specs=[pl.BlockSpec((tm, tc), lambda j, i: (i, j))],
        out_shape=[jax.ShapeDtypeStruct((S, D), F32)],
        scratch_shapes=[pltpu.VMEM((tm + CONV_PAD, tc), F32)],
        operands=(glu, glu, dw, dwb), vmem=_vmem(8 * tm * tc * 4), carry=carry)[0]


def _conv_bwd(dc, glu, a_pre, g_pre, dw, ba, bg, *, tm=256, tc=512):
    S, D = dc.shape
    tm, tc = min(tm, S), min(tc, D)
    ni = S // tm

    def body(dc_ref, dcn_ref, glu_ref, glup_ref, a_ref, g_ref, dw_ref, ba_ref, bg_ref,
             da_ref, dg_ref, ddw_ref, dba_ref, dbg_ref, dwin, gwin):
        i = pl.program_id(1)

        @pl.when(i == 0)
        def _():
            ddw_ref[...] = jnp.zeros_like(ddw_ref)
            dba_ref[...] = jnp.zeros_like(dba_ref)
            dbg_ref[...] = jnp.zeros_like(dbg_ref)

        dcur = dc_ref[...]
        head = dcn_ref[0:CONV_PAD, :]
        dwin[0:tm, :] = dcur
        dwin[tm:tm + CONV_PAD, :] = jnp.where(i < ni - 1, head, jnp.zeros_like(head))
        tail = glup_ref[tm - CONV_PAD:tm, :]
        gwin[0:CONV_PAD, :] = jnp.where(i > 0, tail, jnp.zeros_like(tail))
        gwin[CONV_PAD:CONV_PAD + tm, :] = glu_ref[...]
        dglu = jnp.zeros((tm, tc), F32)
        for k in range(CONV_WIDTH):
            dglu = dglu + dwin[pl.ds(CONV_WIDTH - 1 - k, tm), :] * dw_ref[k:k + 1, :]
            shifted = gwin[pl.ds(CONV_PAD - CONV_WIDTH + 1 + k, tm), :]
            ddw_ref[k:k + 1, :] += jnp.sum(dcur * shifted, axis=0, keepdims=True)
        a = a_ref[...] + ba_ref[...]
        sg = _sigmoid(g_ref[...] + bg_ref[...])
        da = dglu * sg
        dg = dglu * a * sg * (1.0 - sg)
        da_ref[...] = da.astype(BF16)
        dg_ref[...] = dg.astype(BF16)
        dba_ref[...] += jnp.sum(da, axis=0, keepdims=True)
        dbg_ref[...] += jnp.sum(dg, axis=0, keepdims=True)

    tile = lambda f: pl.BlockSpec((tm, tc), f)
    vec = pl.BlockSpec((1, tc), lambda j, i: (0, j))
    return pl.pallas_call(
        body, name="conv_bwd", grid=(D // tc, ni),
        in_specs=[tile(lambda j, i: (i, j)), tile(lambda j, i: (jnp.minimum(i + 1, ni - 1), j)),
                  tile(lambda j, i: (i, j)), tile(lambda j, i: (jnp.maximum(i - 1, 0), j)),
                  tile(lambda j, i: (i, j)), tile(lambda j, i: (i, j)),
                  pl.BlockSpec((CONV_PAD, tc), lambda j, i: (0, j)), vec, vec],
        out_specs=[tile(lambda j, i: (i, j)), tile(lambda j, i: (i, j)),
                   pl.BlockSpec((CONV_PAD, tc), lambda j, i: (0, j)), vec, vec],
        out_shape=[jax.ShapeDtypeStruct((S, D), BF16), jax.ShapeDtypeStruct((S, D), BF16),
                   jax.ShapeDtypeStruct((CONV_PAD, D), F32), jax.ShapeDtypeStruct((1, D), F32),
                   jax.ShapeDtypeStruct((1, D), F32)],
        scratch_shapes=[pltpu.VMEM((tm + CONV_PAD, tc), F32), pltpu.VMEM((tm + CONV_PAD, tc), F32)],
        compiler_params=pltpu.CompilerParams(dimension_semantics=("parallel", "arbitrary"),
                                             vmem_limit_bytes=_vmem(16 * tm * tc * 4)),
    )(dc, dc, glu, glu, a_pre, g_pre, dw, ba, bg)


def _nt(a, b):
    return lax.dot_general(a, b, _DIMS["nt"], preferred_element_type=F32)


def _tn(a, b):
    return lax.dot_general(a, b, _DIMS["tn"], preferred_element_type=F32)


def _window_mask(qi, kj, first_key):
    B = ATTN_BLOCK
    return ((kj < B) & (kj >= qi) & (kj >= first_key)) | ((kj >= B) & (kj - B <= qi))


def _attn_fwd(q_rot, k, v, g, dil):
    S, D = k.shape
    H = D // HEAD_DIM
    L = S // dil
    nb_count = L // ATTN_BLOCK
    B = ATTN_BLOCK

    def body(q_ref, kc_ref, kp_ref, vc_ref, vp_ref, o_ref, lse_ref):
        nb = pl.program_id(1)
        qi = lax.broadcasted_iota(jnp.int32, (B, 2 * B), 0)
        kj = lax.broadcasted_iota(jnp.int32, (B, 2 * B), 1)
        valid = _window_mask(qi, kj, jnp.where(nb > 0, 0, B))
        for h in range(H):
            hs = slice(h * HEAD_DIM, (h + 1) * HEAD_DIM)
            kk = jnp.concatenate([kp_ref[:, hs], kc_ref[:, hs]], axis=0)
            vv = jnp.concatenate([vp_ref[:, hs], vc_ref[:, hs]], axis=0)
            s = jnp.where(valid, _nt(q_ref[:, hs], kk) * ATTN_SCALE, NEG)
            m = jnp.max(s, axis=1, keepdims=True)
            p = jnp.exp(s - m)
            l = jnp.sum(p, axis=1, keepdims=True)
            o = jnp.dot(p.astype(BF16), vv, preferred_element_type=F32) / l
            o_ref[:, hs] = o.astype(o_ref.dtype)
            lse_ref[:, hs] = jnp.broadcast_to(m + jnp.log(l), (B, HEAD_DIM))

    blk = lambda f: pl.BlockSpec((B, D), f)
    cur = lambda r, nb: (nb, r)
    prev = lambda r, nb: (jnp.maximum(nb - 1, 0), r)
    o, lse = pl.pallas_call(
        body, name=f"attn_fwd_g{g}", grid=(dil, nb_count),
        in_specs=[blk(lambda r, nb: (nb, r * N_GROUPS + g)), blk(cur), blk(prev), blk(cur), blk(prev)],
        out_specs=[blk(cur), blk(cur)],
        out_shape=[jax.ShapeDtypeStruct((L, dil * D), BF16), jax.ShapeDtypeStruct((L, dil * D), F32)],
        compiler_params=pltpu.CompilerParams(dimension_semantics=("parallel", "arbitrary"),
                                             vmem_limit_bytes=_vmem(12 * B * D * 4)),
    )(q_rot.reshape(L, dil * N_GROUPS * D), k.reshape(L, dil * D), k.reshape(L, dil * D),
      v.reshape(L, dil * D), v.reshape(L, dil * D))
    return o.reshape(S, D), lse.reshape(S, D)


def _attn_bwd(q_rot, k, v, do, lse, dlt, g, dil):
    S, D = k.shape
    H = D // HEAD_DIM
    L = S // dil
    nb_count = L // ATTN_BLOCK
    B = ATTN_BLOCK

    def body(q_ref, qn_ref, kc_ref, kp_ref, vc_ref, vp_ref, do_ref, don_ref, l_ref, ln_ref, d_ref, dn_ref,
             dq_ref, dk_ref, dv_ref):
        nb = pl.program_id(1)
        qi = lax.broadcasted_iota(jnp.int32, (B, 2 * B), 0)
        kj = lax.broadcasted_iota(jnp.int32, (B, 2 * B), 1)
        valid_q = _window_mask(qi, kj, jnp.where(nb > 0, 0, B))
        qr = lax.broadcasted_iota(jnp.int32, (2 * B, B), 0)
        kc_i = lax.broadcasted_iota(jnp.int32, (2 * B, B), 1)
        q_end = jnp.where(nb < nb_count - 1, 2 * B, B)
        valid_k = ((qr < B) & (kc_i <= qr)) | ((qr >= B) & (qr < q_end) & (kc_i >= qr - B))
        for h in range(H):
            hs = slice(h * HEAD_DIM, (h + 1) * HEAD_DIM)
            q, qn = q_ref[:, hs], qn_ref[:, hs]
            kc, vc = kc_ref[:, hs], vc_ref[:, hs]
            dout, doutn = do_ref[:, hs], don_ref[:, hs]
            lq, lqn = l_ref[:, hs][:, 0:1], ln_ref[:, hs][:, 0:1]
            dq_, dqn_ = d_ref[:, hs][:, 0:1], dn_ref[:, hs][:, 0:1]
            kk = jnp.concatenate([kp_ref[:, hs], kc], axis=0)
            vv = jnp.concatenate([vp_ref[:, hs], vc], axis=0)
            s = jnp.where(valid_q, _nt(q, kk) * ATTN_SCALE, NEG)
            p = jnp.exp(s - lq)
            ds = p * (_nt(dout, vv) - dq_)
            dq_ref[:, hs] = (jnp.dot(ds.astype(BF16), kk, preferred_element_type=F32) * ATTN_SCALE).astype(dq_ref.dtype)
            qq = jnp.concatenate([q, qn], axis=0)
            dd = jnp.concatenate([dout, doutn], axis=0)
            ll = jnp.concatenate([lq, lqn], axis=0)
            dl = jnp.concatenate([dq_, dqn_], axis=0)
            s2 = jnp.where(valid_k, _nt(qq, kc) * ATTN_SCALE, NEG)
            p2 = jnp.exp(s2 - ll)
            dv_ref[:, hs] = _tn(p2.astype(BF16), dd)
            ds2 = p2 * (_nt(dd, vc) - dl)
            dk_ref[:, hs] = _tn(ds2.astype(BF16), qq) * ATTN_SCALE

    blk = lambda f: pl.BlockSpec((B, D), f)
    cur = lambda r, nb: (nb, r)
    prev = lambda r, nb: (jnp.maximum(nb - 1, 0), r)
    nxt = lambda r, nb: (jnp.minimum(nb + 1, nb_count - 1), r)
    qcur = lambda r, nb: (nb, r * N_GROUPS + g)
    qnxt = lambda r, nb: (jnp.minimum(nb + 1, nb_count - 1), r * N_GROUPS + g)
    qv = q_rot.reshape(L, dil * N_GROUPS * D)
    view = lambda t: t.reshape(L, dil * D)
    dq, dk, dv = pl.pallas_call(
        body, name=f"attn_bwd_g{g}", grid=(dil, nb_count),
        in_specs=[blk(qcur), blk(qnxt), blk(cur), blk(prev), blk(cur), blk(prev), blk(cur), blk(nxt),
                  blk(cur), blk(nxt), blk(cur), blk(nxt)],
        out_specs=[blk(cur), blk(cur), blk(cur)],
        out_shape=[jax.ShapeDtypeStruct((L, dil * D), BF16), jax.ShapeDtypeStruct((L, dil * D), F32),
                   jax.ShapeDtypeStruct((L, dil * D), F32)],
        compiler_params=pltpu.CompilerParams(dimension_semantics=("parallel", "arbitrary"),
                                             vmem_limit_bytes=_vmem(24 * B * D * 4)),
    )(qv, qv, view(k), view(k), view(v), view(v), view(do), view(do), view(lse), view(lse), view(dlt), view(dlt))
    return dq.reshape(S, D), dk.reshape(S, D), dv.reshape(S, D)


def _mlp_ple_fwd(z1, h1b, p_l, W, vec, l, run):
    D = z1.shape[1]
    g1, b1, g2, b2 = vec

    def act_ep(acc):
        return acc, jnp.square(jnp.maximum(acc, 0.0))

    t, act = run(_mm, h1b, W[f"mlp_up{l}"], mode="nn", outs=[BF16, BF16], epilogue=act_ep, name=f"mlp_up{l}")
    f = run(_mm, act, W[f"mlp_down{l}"], mode="nn", outs=[F32], name=f"mlp_down{l}")

    def z2_fn(z1_t, f_t, g1_, b1_, g2_, b2_):
        z2 = ALPHA * _ln(z1_t, g1_, b1_) + f_t
        return z2, _ln(z2, g2_, b2_)

    z2, h2b = _rows(z2_fn, [z1, f], [g1, b1, g2, b2], [(D, F32), (D, BF16)], [], tm=256, name=f"ln2_fwd{l}")
    pp = run(_mm, p_l, W[f"ple_proj{l}"], mode="nn", outs=[F32], name=f"ple_proj{l}")
    gpre = run(_mm, h2b, W[f"ple_gate{l}"], mode="nn", outs=[F32], name=f"ple_gate{l}")
    return t, act, z2, h2b, pp, gpre


def _mlp_ple_bwd(dy, d_pp, d_gpre, p_l, z1, h1b, t, act, z2, h2b, wts, vec, l):
    D = z1.shape[1]
    up, down, pp_w, pg_w = wts
    g1, b1, g2, b2 = vec
    gw = {}
    gw[f"ple_proj{l}"] = _mm(p_l, d_pp, mode="tn", outs=[BF16], layout=("col", N_CHIPS), name=f"d_ple_proj{l}")
    gw[f"ple_gate{l}"] = _mm(h2b, d_gpre, mode="tn", outs=[BF16], layout=("row", N_CHIPS), name=f"d_ple_gate{l}")
    dh2 = _mm(d_gpre, pg_w, mode="nt", outs=[F32], extras=[dy], epilogue=lambda acc, e: (acc + e,),
              name=f"dh2_{l}")

    def ln2_bwd(dh2_t, z2_t, g2_):
        n, rstd = _ln_norm(z2_t)
        dz2 = _ln_bwd(dh2_t, n, rstd, g2_)
        return dz2, dz2, dh2_t * n, dh2_t

    dz2, dz2b, dg2, db2 = _rows(ln2_bwd, [dh2, z2], [g2], [(D, F32), (D, BF16)], [D, D], tm=256,
                                name=f"ln2_bwd{l}")
    gw[f"mlp_down{l}"] = _mm(act, dz2b, mode="tn", outs=[BF16], layout=("row", N_CHIPS), name=f"d_mlp_down{l}")
    dt = _mm(dz2b, down, mode="nt", outs=[BF16], extras=[t],
             epilogue=lambda acc, t_: (acc * 2.0 * jnp.maximum(t_.astype(F32), 0.0),), name=f"dt{l}", tm=512)
    gw[f"mlp_up{l}"] = _mm(h1b, dt, mode="tn", outs=[BF16], layout=("col", N_CHIPS), name=f"d_mlp_up{l}")
    dh1 = _mm(dt, up, mode="nt", outs=[F32], extras=[dz2], epilogue=lambda acc, e: (acc + ALPHA * e,),
              name=f"dh1_{l}")

    def ln1_bwd(dh1_t, z1_t, g1_):
        n, rstd = _ln_norm(z1_t)
        dz1 = _ln_bwd(dh1_t, n, rstd, g1_)
        return dz1, dz1, dh1_t * n, dh1_t

    dz1, dz1b, dg1, db1 = _rows(ln1_bwd, [dh1, z1], [g1], [(D, F32), (D, BF16)], [D, D], tm=256,
                                name=f"ln1_bwd{l}")
    return dz1, dz1b, gw, (dg1, db1, dg2, db2)


def _ple_out(z2, pp, gpre, g2, b2):
    gt = _sigmoid(gpre)
    return _ln(z2, g2, b2) + pp * gt, gt


_GATHER_AT = {
    "conv_in_a": ("conv_w_out",),
    "conv_in_g": ("ple_gate0", "ple_proj0"),
    "conv_fwd": ("mlp_up0",),
    "mlp_up0": ("mlp_down0",),
    "mlp_down0": ("attn_w_q",),
    "ple_gate0": ("w_kv",),
    "kv_k": ("attn_w_o",),
    "kv_v": ("ple_proj1", "ple_gate1"),
    "attn_q": ("mlp_up1",),
    "mlp_up1": ("mlp_down1",),
}
_GATHER_FIRST = ("conv_w_in",)


def _local_step(x, p, cosf, sinf, target, W, V, shards=None):
    S, D = x.shape
    gw, gv = {}, {}
    if shards is not None:
        W = dict(_Gather(_GATHER_FIRST, shards).run_alone("gather_first"))

    def run(fn, *args, name, **kw):
        carry = _Gather(_GATHER_AT[name], shards) if (shards is not None and name in _GATHER_AT) else None
        out = fn(*args, name=name, carry=carry, **kw)
        if carry is not None:
            W.update(carry.result)
        return out

    a_pre = run(_mm, x, W["conv_w_in"], b_sel=0, mode="nn", outs=[F32], name="conv_in_a")
    g_pre = run(_mm, x, W["conv_w_in"], b_sel=1, mode="nn", outs=[F32], name="conv_in_g")
    (glu,) = _rows(lambda a, g, ba, bg: ((a + ba) * _sigmoid(g + bg),), [a_pre, g_pre], [V["conv_b_a"], V["conv_b_g"]],
                   [(D, F32)], [], tm=256, name="glu_fwd")
    cv = run(_conv_fwd, glu, V["conv_dw"], V["conv_dw_b"], name="conv_fwd")

    def silu_ln(c, g_, b_):
        y = _ln(c, g_, b_)
        return (y * _sigmoid(y),)

    (sb,) = _rows(silu_ln, [cv], [V["conv_ln_g"], V["conv_ln_b"]], [(D, BF16)], [], tm=256, name="conv_ln_fwd")
    mix0 = _mm(sb, W["conv_w_out"], mode="nn", outs=[F32], name="conv_out")

    def z1_fn(x_t, mix_t, g_, b_):
        z1 = ALPHA * x_t + mix_t
        return z1, _ln(z1, g_, b_)

    vec0 = (V["ln1_g0"], V["ln1_b0"], V["ln2_g0"], V["ln2_b0"])
    vec1 = (V["ln1_g1"], V["ln1_b1"], V["ln2_g1"], V["ln2_b1"])
    z1_0, h1b_0 = _rows(z1_fn, [x, mix0], [vec0[0], vec0[1]], [(D, F32), (D, BF16)], [], tm=256, name="ln1_fwd0")
    t0, act0, z2_0, h2b_0, pp0, gpre0 = _mlp_ple_fwd(z1_0, h1b_0, p[0], W, vec0, 0, run)

    def x1_fn(z2, pp, gpre, g2, b2, kg, kb):
        x1, _ = _ple_out(z2, pp, gpre, g2, b2)
        return x1, _ln(x1, kg, kb)

    x1, kvn = _rows(x1_fn, [z2_0, pp0, gpre0], [vec0[2], vec0[3], V["kv_ln_g"], V["kv_ln_b"]],
                    [(D, F32), (D, BF16)], [], tm=256, name="ple_out0")

    rot_ep = lambda acc, c_, s_: (_rot(acc, c_, s_),)
    k_rot = run(_mm, kvn, W["w_kv"], b_sel=0, mode="nn", outs=[BF16], rextras=[cosf, sinf], epilogue=rot_ep, name="kv_k")
    v_b = run(_mm, kvn, W["w_kv"], b_sel=1, mode="nn", outs=[BF16], name="kv_v")
    q_rot = run(_mm, x1, W["attn_w_q"], mode="nn", outs=[BF16], rextras=[cosf, sinf], epilogue=rot_ep, name="attn_q")
    og, lg = [], []
    for g, dil in enumerate(DILATIONS):
        o_g, l_g = _attn_fwd(q_rot, k_rot, v_b, g, dil)
        og.append(o_g)
        lg.append(l_g)

    def merge(o0, o1, o2, l0, l1, l2):
        m = jnp.maximum(jnp.maximum(l0, l1), l2)
        e0, e1, e2 = jnp.exp(l0 - m), jnp.exp(l1 - m), jnp.exp(l2 - m)
        den = e0 + e1 + e2
        o = (e0 * o0.astype(F32) + e1 * o1.astype(F32) + e2 * o2.astype(F32)) / den
        return o, m + jnp.log(den)

    ob, lse = _rows(merge, og + lg, [], [(D, BF16), (D, F32)], [], tm=256, name="attn_merge")
    mix1 = _mm(ob, W["attn_w_o"], mode="nn", outs=[F32], name="attn_out")
    z1_1, h1b_1 = _rows(z1_fn, [x1, mix1], [vec1[0], vec1[1]], [(D, F32), (D, BF16)], [], tm=256, name="ln1_fwd1")
    t1, act1, z2_1, h2b_1, pp1, gpre1 = _mlp_ple_fwd(z1_1, h1b_1, p[1], W, vec1, 1, run)
    wts0 = (W["mlp_up0"], W["mlp_down0"], W["ple_proj0"], W["ple_gate0"])
    wts1 = (W["mlp_up1"], W["mlp_down1"], W["ple_proj1"], W["ple_gate1"])

    def head(z2, pp, gpre, tgt, g2, b2):
        y, gt = _ple_out(z2, pp, gpre, g2, b2)
        err = y - tgt
        dy = err * (1.0 / D)
        return dy, dy * gt, dy * pp * gt * (1.0 - gt), 0.5 * err * err * (1.0 / D)

    dy1, d_pp1, d_gpre1, loss_cols = _rows(head, [z2_1, pp1, gpre1, target], [vec1[2], vec1[3]],
                                           [(D, F32), (D, BF16), (D, BF16)], [D], tm=256, name="loss_head")

    dz1_1, dz1b_1, gw1, (gv["ln1_g1"], gv["ln1_b1"], gv["ln2_g1"], gv["ln2_b1"]) = _mlp_ple_bwd(
        dy1, d_pp1, d_gpre1, p[1], z1_1, h1b_1, t1, act1, z2_1, h2b_1, wts1, vec1, 1)
    gw.update(gw1)
    gw["attn_w_o"] = _mm(ob, dz1b_1, mode="tn", outs=[BF16], layout=("row", N_CHIPS), name="d_attn_w_o")

    def do_ep(acc, o_t):
        prod = acc * o_t.astype(F32)
        dlt = _per_head(prod, lambda ph, i: jnp.broadcast_to(jnp.sum(ph, axis=1, keepdims=True), ph.shape))
        return acc, dlt

    do_b, dlt = _mm(dz1b_1, W["attn_w_o"], mode="nt", outs=[BF16, F32], extras=[ob], epilogue=do_ep, name="attn_do",
                    tm=512)
    dqs, dks, dvs = [], [], []
    for g, dil in enumerate(DILATIONS):
        dq_g, dk_g, dv_g = _attn_bwd(q_rot, k_rot, v_b, do_b, lse, dlt, g, dil)
        dqs.append(dq_g)
        dks.append(dk_g)
        dvs.append(dv_g)

    def unrot(q0, q1, q2, k0, k1, k2, v0, v1, v2, c_, s_):
        dq = jnp.concatenate([_rot_t(t_.astype(F32), c_, s_) for t_ in (q0, q1, q2)], axis=1)
        return dq, _rot_t(k0 + k1 + k2, c_, s_), v0 + v1 + v2

    dq, dk, dv = _rows(unrot, dqs + dks + dvs + [cosf, sinf], [], [(N_GROUPS * D, BF16), (D, BF16), (D, BF16)], [],
                       tm=128, name="attn_unrot")
    gw["attn_w_q"] = _mm(x1, dq, mode="tn", outs=[BF16], layout=("col", N_CHIPS), tn=512, name="d_attn_w_q")
    dx1_q = _mm(dq, W["attn_w_q"], mode="nt", outs=[F32], extras=[dz1_1], epilogue=lambda acc, e: (acc + ALPHA * e,),
                name="dx1_q")
    gw["w_kv"] = jnp.concatenate(
        [_mm(kvn, dk, mode="tn", outs=[BF16], layout=("col", 2), name="d_w_kv_k"),
         _mm(kvn, dv, mode="tn", outs=[BF16], layout=("col", 2), name="d_w_kv_v")], axis=1)
    dkvn_k = _mm(dk, W["w_kv"], b_sel=0, mode="nt", outs=[F32], name="dkvn_k")
    dkvn = _mm(dv, W["w_kv"], b_sel=1, mode="nt", outs=[F32], extras=[dkvn_k], epilogue=lambda acc, e: (acc + e,),
               name="dkvn_v")

    def x1_bwd(dx1q_t, dkvn_t, x1_t, pp, gpre, kg):
        n, rstd = _ln_norm(x1_t)
        dy = dx1q_t + _ln_bwd(dkvn_t, n, rstd, kg)
        gt = _sigmoid(gpre)
        return dy, dy * gt, dy * pp * gt * (1.0 - gt), dkvn_t * n, dkvn_t

    dy0, d_pp0, d_gpre0, gv["kv_ln_g"], gv["kv_ln_b"] = _rows(
        x1_bwd, [dx1_q, dkvn, x1, pp0, gpre0], [V["kv_ln_g"]], [(D, F32), (D, BF16), (D, BF16)], [D, D], tm=256,
        name="x1_bwd")

    dz1_0, dz1b_0, gw0, (gv["ln1_g0"], gv["ln1_b0"], gv["ln2_g0"], gv["ln2_b0"]) = _mlp_ple_bwd(
        dy0, d_pp0, d_gpre0, p[0], z1_0, h1b_0, t0, act0, z2_0, h2b_0, wts0, vec0, 0)
    gw.update(gw0)
    gw["conv_w_out"] = _mm(sb, dz1b_0, mode="tn", outs=[BF16], layout=("row", N_CHIPS), name="d_conv_w_out")
    ds = _mm(dz1b_0, W["conv_w_out"], mode="nt", outs=[F32], name="conv_ds")

    def conv_ln_bwd(ds_t, c_t, g_, b_):
        n, rstd = _ln_norm(c_t)
        y = n * g_ + b_
        sg = _sigmoid(y)
        dln = ds_t * sg * (1.0 + y * (1.0 - sg))
        dc = _ln_bwd(dln, n, rstd, g_)
        return dc, dln * n, dln, dc

    dc, gv["conv_ln_g"], gv["conv_ln_b"], gv["conv_dw_b"] = _rows(
        conv_ln_bwd, [ds, cv], [V["conv_ln_g"], V["conv_ln_b"]], [(D, F32)], [D, D, D], tm=256, name="conv_ln_bwd")
    da, dg, gv["conv_dw"], gv["conv_b_a"], gv["conv_b_g"] = _conv_bwd(
        dc, glu, a_pre, g_pre, V["conv_dw"], V["conv_b_a"], V["conv_b_g"])
    gw["conv_w_in"] = jnp.concatenate(
        [_mm(x, da, mode="tn", outs=[BF16], layout=("col", 2), name="d_conv_w_in_a"),
         _mm(x, dg, mode="tn", outs=[BF16], layout=("col", 2), name="d_conv_w_in_g")], axis=1)
    dx_a = _mm(da, W["conv_w_in"], b_sel=0, mode="nt", outs=[F32], extras=[dz1_0],
               epilogue=lambda acc, e: (acc + ALPHA * e,), name="dx_a")
    grad_x = _mm(dg, W["conv_w_in"], b_sel=1, mode="nt", outs=[F32], extras=[dx_a], epilogue=lambda acc, e: (acc + e,),
                 name="dx_g")
    return loss_cols, grad_x, gw, gv


def _place():
    x, y, c = lax.axis_index("x"), lax.axis_index("y"), lax.axis_index("c")
    chips = [(1 - x, y), (x, 1 - y), (1 - x, 1 - y)]
    return x, y, c, chips


def _remote(src, dst, ssem, rsem, dev):
    return pltpu.make_async_remote_copy(src_ref=src, dst_ref=dst, send_sem=ssem, recv_sem=rsem, device_id=dev,
                                        device_id_type=MESH)


def _allgather8(block, name):
    R, C = block.shape

    def body(x_ref, out_ref, send_sems, recv_sems, local_sem):
        x, y, c, chips = _place()
        me, sibling = (x, y, c), (x, y, 1 - c)

        def slot(px, py, pc):
            return out_ref.at[4 * px + 2 * py + pc]

        def copy(k, blockpos, to, src=None):
            return _remote(slot(*blockpos) if src is None else src, slot(*blockpos), send_sems.at[k], recv_sems.at[k], to)

        mine = pltpu.make_async_copy(x_ref, slot(*me), local_sem)
        mine.start()
        first = [copy(0, me, sibling, src=x_ref)]
        first += [copy(1 + j, me, (*chip, c), src=x_ref) for j, chip in enumerate(chips)]
        for cp in first:
            cp.start()
        passed = [copy(4 + j, (*chip, c), sibling) for j, chip in enumerate(chips)]
        for j, chip in enumerate(chips):
            copy(1 + j, (*chip, c), me).wait_recv()
            passed[j].start()
        copy(0, sibling, me).wait_recv()
        for j, chip in enumerate(chips):
            copy(4 + j, (*chip, 1 - c), me).wait_recv()
        for cp in first + passed:
            cp.wait_send()
        mine.wait()

    return pl.pallas_call(
        body, name=name, out_shape=jax.ShapeDtypeStruct((8, R, C), block.dtype),
        in_specs=[pl.BlockSpec(memory_space=pltpu.VMEM)], out_specs=pl.BlockSpec(memory_space=pltpu.VMEM),
        scratch_shapes=[pltpu.SemaphoreType.DMA((7,)), pltpu.SemaphoreType.DMA((7,)), pltpu.SemaphoreType.DMA],
        compiler_params=pltpu.CompilerParams(vmem_limit_bytes=_vmem(10 * _nbytes((R, C), block.dtype))),
    )(block)


_MATS = (
    ("conv_w_in", "conv_w_in", 0, "col", True),
    ("conv_w_out", "conv_w_out", 0, "row", False),
    ("mlp_up0", "mlp_up", 0, "col", False),
    ("mlp_down0", "mlp_down", 0, "row", False),
    ("ple_proj0", "ple_proj", 0, "col", False),
    ("ple_gate0", "ple_gate", 0, "row", False),
    ("w_kv", "w_kv", None, "col", True),
    ("attn_w_q", "attn_w_q", 0, "col", False),
    ("attn_w_o", "attn_w_o", 0, "row", False),
    ("mlp_up1", "mlp_up", 1, "col", False),
    ("mlp_down1", "mlp_down", 1, "row", False),
    ("ple_proj1", "ple_proj", 1, "col", False),
    ("ple_gate1", "ple_gate", 1, "row", False),
)


class _Gather:
    def __init__(self, names, shards):
        mats = [m for m in _MATS if m[0] in names]
        srcs = sorted({m[1] for m in mats})
        self.names = [m[0] for m in mats]
        self.ins = [shards[n] for n in srcs]
        self.out_shape, self.geo = [], []
        for name, src, layer, kind, split in mats:
            s = shards[src]
            ks, ns = s.shape[-2:]
            K, N = (ks, ns * N_CHIPS) if kind == "col" else (ks * N_CHIPS, ns)
            self.out_shape.append(jax.ShapeDtypeStruct((2, K, N // 2) if split else (K, N), BF16))
            self.geo.append((srcs.index(src), layer if s.ndim == 3 else None, kind, split, K, N))
        T = len(mats)
        self.scratch = [pltpu.SemaphoreType.DMA((3 * T,)) for _ in range(4)] + [pltpu.SemaphoreType.DMA((T,))]
        self.result = None

    def _copies(self, in_refs, out_refs, sems):
        geo, T = self.geo, len(self.geo)
        s_ici, r_ici, s_d2d, r_d2d, lsem = sems
        x, y, c, chips = _place()
        me = 2 * x + y
        sibling = (x, y, 1 - c)
        idx = [2 * cx + cy for cx, cy in chips]

        def src_ref(t):
            i, layer, _, _, _, _ = geo[t]
            return in_refs[i] if layer is None else in_refs[i].at[layer]

        def src_half(t, h):
            _, _, kind, _, K, N = geo[t]
            if kind == "col":
                return src_ref(t).at[pl.ds(h * (K // 2), K // 2), :]
            return src_ref(t).at[:, pl.ds(h * (N // 2), N // 2)]

        def dst(t, j, h):
            _, _, kind, split, K, N = geo[t]
            n, k = N // N_CHIPS, K // N_CHIPS
            if kind == "col":
                rows = slice(None) if h is None else pl.ds(h * (K // 2), K // 2)
                if split:
                    return out_refs[t].at[j // 2, rows, pl.ds((j % 2) * n, n)]
                return out_refs[t].at[rows, pl.ds(j * n, n)]
            cols = slice(None) if h is None else pl.ds(h * (N // 2), N // 2)
            return out_refs[t].at[pl.ds(j * k, k), cols]

        local = [pltpu.make_async_copy(src_ref(t), dst(t, me, None), lsem.at[t]) for t in range(T)]
        sends = [_remote(src_half(t, c), dst(t, me, c), s_ici.at[3 * t + kk], r_ici.at[3 * t + kk], (*chips[kk], c))
                 for t in range(T) for kk in range(3)]
        hops = []
        for t in range(T):
            for kk in range(3):
                mine, theirs = dst(t, idx[kk], c), dst(t, idx[kk], 1 - c)
                hops.append((_remote(mine, mine, s_ici.at[3 * t + kk], r_ici.at[3 * t + kk], sibling),
                             _remote(mine, mine, s_d2d.at[3 * t + kk], r_d2d.at[3 * t + kk], sibling),
                             _remote(theirs, theirs, s_d2d.at[3 * t + kk], r_d2d.at[3 * t + kk], sibling)))
        return local, sends, hops

    def start(self, in_refs, out_refs, sems):
        local, sends, _ = self._copies(in_refs, out_refs, sems)
        for cp in local + sends:
            cp.start()

    def finish(self, in_refs, out_refs, sems):
        local, sends, hops = self._copies(in_refs, out_refs, sems)
        for landed, forward, _ in hops:
            landed.wait_recv()
            forward.start()
        for _, _, from_sibling in hops:
            from_sibling.wait_recv()
        for cp in sends + [h[1] for h in hops]:
            cp.wait_send()
        for cp in local:
            cp.wait()

    def run_alone(self, name):
        n_in, n_out = len(self.ins), len(self.out_shape)

        def body(*refs):
            in_refs, out_refs, sems = refs[:n_in], refs[n_in:n_in + n_out], refs[n_in + n_out:]
            self.start(in_refs, out_refs, sems)
            self.finish(in_refs, out_refs, sems)

        outs = pl.pallas_call(body, name=name, out_shape=self.out_shape, in_specs=[ANY] * n_in, out_specs=[ANY] * n_out,
                              scratch_shapes=self.scratch)(*self.ins)
        self.result = dict(zip(self.names, outs))
        return self.result


def _sibling_exchange(gw, c_arr):
    names = [m[0] for m in _MATS]
    ins = [gw[n] for n in names]
    T = len(names)

    def body(*refs):
        in_refs, out_refs = refs[:T], refs[T:2 * T]
        ssem, rsem = refs[2 * T:]
        x, y, c, _ = _place()
        sibling = (x, y, 1 - c)
        cps = [_remote(in_refs[t].at[1 - c], out_refs[t], ssem.at[t], rsem.at[t], sibling) for t in range(T)]
        for cp in cps:
            cp.start()
        for cp in cps:
            cp.wait()

    outs = pl.pallas_call(
        body, name="grad_sibling_exchange", out_shape=[jax.ShapeDtypeStruct(a.shape[1:], BF16) for a in ins],
        in_specs=[ANY] * T, out_specs=[ANY] * T,
        scratch_shapes=[pltpu.SemaphoreType.DMA((T,)), pltpu.SemaphoreType.DMA((T,))],
    )(*ins)
    return dict(zip(names, outs))


def _pair_sum(own, landed, c_arr, name):
    _, ns, r, cc = own.shape
    rows = ns * r
    tr = min(512, rows)
    assert rows % tr == 0

    def body(c_ref, a_ref, b_ref, o_ref):
        o_ref[...] = (a_ref[...].astype(F32) + b_ref[...].astype(F32)).astype(o_ref.dtype)

    out = pl.pallas_call(
        body, name=name, out_shape=jax.ShapeDtypeStruct((rows, cc), BF16),
        grid_spec=pltpu.PrefetchScalarGridSpec(
            num_scalar_prefetch=1, grid=(rows // tr,),
            in_specs=[pl.BlockSpec((None, tr, cc), lambda i, c_ref: (c_ref[0], i, 0)),
                      pl.BlockSpec((tr, cc), lambda i, c_ref: (i, 0))],
            out_specs=pl.BlockSpec((tr, cc), lambda i, c_ref: (i, 0))),
        compiler_params=pltpu.CompilerParams(dimension_semantics=("parallel",), vmem_limit_bytes=_vmem(8 * tr * cc * 4)),
    )(c_arr, own.reshape(2, rows, cc), landed.reshape(rows, cc))
    return out.reshape(ns, r, cc)


def _chip_scatter(ps):
    names = [m[0] for m in _MATS]
    ins = [ps[n] for n in names]
    T = len(names)

    def body(*refs):
        in_refs, out_refs = refs[:T], refs[T:2 * T]
        ssem, rsem, lsem = refs[2 * T:]
        x, y, c, chips = _place()
        me = 2 * x + y
        idx = [2 * cx + cy for cx, cy in chips]
        local = [pltpu.make_async_copy(in_refs[t].at[me], out_refs[t].at[me], lsem.at[t]) for t in range(T)]
        for cp in local:
            cp.start()
        sends = [_remote(in_refs[t].at[idx[kk]], out_refs[t].at[me], ssem.at[3 * t + kk], rsem.at[3 * t + kk],
                         (*chips[kk], c)) for t in range(T) for kk in range(3)]
        for cp in sends:
            cp.start()
        for t in range(T):
            for kk in range(3):
                landed = out_refs[t].at[idx[kk]]
                _remote(landed, landed, ssem.at[3 * t + kk], rsem.at[3 * t + kk], (*chips[kk], c)).wait_recv()
        for cp in sends:
            cp.wait_send()
        for cp in local:
            cp.wait()

    outs = pl.pallas_call(
        body, name="grad_chip_scatter", out_shape=[jax.ShapeDtypeStruct(a.shape, BF16) for a in ins],
        in_specs=[ANY] * T, out_specs=[ANY] * T,
        scratch_shapes=[pltpu.SemaphoreType.DMA((3 * T,)), pltpu.SemaphoreType.DMA((3 * T,)),
                        pltpu.SemaphoreType.DMA((T,))],
    )(*ins)
    return dict(zip(names, outs))


def _chip_sum(parts, c_arr, name):
    _, r, cc = parts.shape
    tr = min(256, r)
    assert r % tr == 0

    def body(c_ref, p_ref, o_ref):
        acc = p_ref[0].astype(F32)
        for j in range(1, N_CHIPS):
            acc = acc + p_ref[j].astype(F32)
        o_ref[...] = acc

    return pl.pallas_call(
        body, name=name, out_shape=jax.ShapeDtypeStruct((2, r, cc), F32),
        grid_spec=pltpu.PrefetchScalarGridSpec(
            num_scalar_prefetch=1, grid=(r // tr,),
            in_specs=[pl.BlockSpec((N_CHIPS, tr, cc), lambda i, c_ref: (0, i, 0))],
            out_specs=pl.BlockSpec((None, tr, cc), lambda i, c_ref: (c_ref[0], i, 0))),
        compiler_params=pltpu.CompilerParams(dimension_semantics=("parallel",), vmem_limit_bytes=_vmem(12 * tr * cc * 4)),
    )(c_arr, parts)


def _sibling_share(gh):
    names = [m[0] for m in _MATS]
    ins = [gh[n] for n in names]
    T = len(names)

    def body(*refs):
        out_refs = refs[T:2 * T]
        ssem, rsem = refs[2 * T:]
        x, y, c, _ = _place()
        sibling = (x, y, 1 - c)
        cps = [_remote(out_refs[t].at[c], out_refs[t].at[c], ssem.at[t], rsem.at[t], sibling) for t in range(T)]
        for cp in cps:
            cp.start()
        for t in range(T):
            other = out_refs[t].at[1 - c]
            _remote(other, other, ssem.at[t], rsem.at[t], sibling).wait_recv()
        for cp in cps:
            cp.wait_send()

    outs = pl.pallas_call(
        body, name="grad_sibling_share", out_shape=[jax.ShapeDtypeStruct(a.shape, F32) for a in ins],
        in_specs=[ANY] * T, out_specs=[ANY] * T, input_output_aliases={t: t for t in range(T)},
        scratch_shapes=[pltpu.SemaphoreType.DMA((T,)), pltpu.SemaphoreType.DMA((T,))],
    )(*ins)
    return dict(zip(names, outs))


def _adamw_math(w, g, m, v):
    m2 = ADAM_B1 * m + (1.0 - ADAM_B1) * g
    v2 = ADAM_B2 * v + (1.0 - ADAM_B2) * jnp.square(g)
    m_hat = m2 / (1.0 - ADAM_B1 ** ADAM_STEP)
    v_hat = v2 / (1.0 - ADAM_B2 ** ADAM_STEP)
    delta = -ADAM_LR * (m_hat / (jnp.sqrt(v_hat) + ADAM_EPS) + ADAM_WD * w)
    return delta, m2, v2


def _adamw_mat(g2, w, m, v, layer, kind, prev, name):
    shape = w.shape
    ks, ns = shape[-2:]
    _, r, cc = g2.shape
    tr, tc = _fit(256, r), _fit(1024, cc)
    assert (r, cc) == ((ks // 2, ns) if kind == "col" else (ks, ns // 2))
    assert r % tr == 0 and cc % tc == 0
    rb, cb = r // tr, cc // tc
    if kind == "col":
        g_spec = pl.BlockSpec((None, tr, tc), lambda i, j: (i // rb, i % rb, j))
    else:
        g_spec = pl.BlockSpec((None, tr, tc), lambda i, j: (j // cb, i, j % cb))
    if w.ndim == 3:
        w_spec = pl.BlockSpec((None, tr, tc), lambda i, j: (layer, i, j))
    else:
        w_spec = pl.BlockSpec((tr, tc), lambda i, j: (i, j))
    n_prev = 0 if prev is None else 4

    def body(*refs):
        g_ref, w_ref, m_ref, v_ref = refs[:4]
        go_ref, d_ref, mo_ref, vo_ref = refs[4 + n_prev:]
        g = g_ref[...]
        delta, m2, v2 = _adamw_math(w_ref[...], g, m_ref[...], v_ref[...])
        go_ref[...] = g
        d_ref[...] = delta
        mo_ref[...] = m2
        vo_ref[...] = v2

    return pl.pallas_call(
        body, name=name, grid=(ks // tr, ns // tc),
        in_specs=[g_spec, w_spec, w_spec, w_spec] + [ANY] * n_prev, out_specs=[w_spec] * 4,
        out_shape=[jax.ShapeDtypeStruct(shape, F32)] * 4,
        input_output_aliases={4 + i: i for i in range(n_prev)},
        compiler_params=pltpu.CompilerParams(dimension_semantics=("parallel", "parallel"),
                                             vmem_limit_bytes=_vmem(16 * tr * tc * 4)),
    )(g2, w, m, v, *(prev or ()))


def _adamw_small(g, w, m, v, name):
    def body(g_ref, w_ref, m_ref, v_ref, d_ref, mo_ref, vo_ref):
        delta, m2, v2 = _adamw_math(w_ref[...], g_ref[...], m_ref[...], v_ref[...])
        d_ref[...] = delta
        mo_ref[...] = m2
        vo_ref[...] = v2

    return pl.pallas_call(body, name=name, out_shape=[jax.ShapeDtypeStruct(w.shape, F32)] * 3)(g, w, m, v)


def _sum8(parts, name):
    def body(p_ref, o_ref):
        acc = p_ref[0]
        for j in range(1, 8):
            acc = acc + p_ref[j]
        o_ref[...] = acc

    return pl.pallas_call(body, name=name, out_shape=jax.ShapeDtypeStruct(parts.shape[1:], F32),
                          compiler_params=pltpu.CompilerParams(vmem_limit_bytes=_vmem(12 * _nbytes(parts.shape[1:], F32))))(parts)


_WEIGHTS = ("conv_w_in", "conv_b_in", "conv_dw", "conv_dw_b", "conv_ln_g", "conv_ln_b", "conv_w_out", "kv_ln_g",
            "kv_ln_b", "w_kv", "attn_w_q", "attn_w_o", "ln1_g", "ln1_b", "mlp_up", "mlp_down", "ln2_g", "ln2_b",
            "ple_proj", "ple_gate")
_SHARDED_VECS = ("conv_b_in", "conv_dw", "conv_dw_b", "conv_ln_g", "conv_ln_b")
_REPLICATED_VECS = ("kv_ln_g", "kv_ln_b", "ln1_g", "ln1_b", "ln2_g", "ln2_b")


def _pad_rows(a, rows):
    return jnp.concatenate([a, jnp.zeros((rows - a.shape[0], a.shape[1]), a.dtype)], axis=0) if a.shape[0] < rows else a


def _pack_sharded(d):
    n = d["conv_dw_b"].shape[-1]
    rows = [d["conv_b_in"].reshape(2, n), d["conv_dw"].reshape(CONV_WIDTH, n), d["conv_dw_b"].reshape(1, n),
            d["conv_ln_g"].reshape(1, n), d["conv_ln_b"].reshape(1, n)]
    return _pad_rows(jnp.concatenate(rows, axis=0), 40)


def _unpack_sharded(pack, like):
    n = pack.shape[1]
    return {"conv_b_in": pack[0:2].reshape(like["conv_b_in"].shape),
            "conv_dw": pack[2:2 + CONV_WIDTH].reshape(like["conv_dw"].shape),
            "conv_dw_b": pack[33:34].reshape(like["conv_dw_b"].shape),
            "conv_ln_g": pack[34:35].reshape(like["conv_ln_g"].shape),
            "conv_ln_b": pack[35:36].reshape(like["conv_ln_b"].shape)}


def _pack_replicated(d):
    D = d["kv_ln_g"].shape[-1]
    rows = [d[n].reshape(-1, D) for n in _REPLICATED_VECS]
    return _pad_rows(jnp.concatenate(rows, axis=0), 16)


def _unpack_replicated(pack, like):
    out, r = {}, 0
    for n in _REPLICATED_VECS:
        k = like[n].size // pack.shape[1]
        out[n] = pack[r:r + k].reshape(like[n].shape)
        r += k
    return out


def kernel(x, p, positions, conv_w_in, conv_b_in, conv_dw, conv_dw_b, conv_ln_g, conv_ln_b, conv_w_out, kv_ln_g, kv_ln_b, w_kv, attn_w_q, attn_w_o, ln1_g, ln1_b, mlp_up, mlp_down, ln2_g, ln2_b, ple_proj, ple_gate, loss_target, m_conv_w_in, m_conv_b_in, m_conv_dw, m_conv_dw_b, m_conv_ln_g, m_conv_ln_b, m_conv_w_out, m_kv_ln_g, m_kv_ln_b, m_w_kv, m_attn_w_q, m_attn_w_o, m_ln1_g, m_ln1_b, m_mlp_up, m_mlp_down, m_ln2_g, m_ln2_b, m_ple_proj, m_ple_gate, v_conv_w_in, v_conv_b_in, v_conv_dw, v_conv_dw_b, v_conv_ln_g, v_conv_ln_b, v_conv_w_out, v_kv_ln_g, v_kv_ln_b, v_w_kv, v_attn_w_q, v_attn_w_o, v_ln1_g, v_ln1_b, v_mlp_up, v_mlp_down, v_ln2_g, v_ln2_b, v_ple_proj, v_ple_gate):
    args = dict(locals())
    w = {n: args[n] for n in _WEIGHTS}
    mom = {n: args["m_" + n] for n in _WEIGHTS}
    var = {n: args["v_" + n] for n in _WEIGHTS}
    S, D = x.shape[1:]
    n4 = D // N_CHIPS
    chip = 2 * lax.axis_index("x") + lax.axis_index("y")
    c_arr = lax.axis_index("c").astype(jnp.int32).reshape(1)

    shards = {n: w[n].astype(BF16) for n in sorted({m[1] for m in _MATS})}
    vec_all = _allgather8(_pack_sharded(w), "gather_vectors")
    vec_full = jnp.concatenate([vec_all[2 * j] for j in range(N_CHIPS)], axis=1)
    b_in = vec_all[0::2, 0:2, :].reshape(1, 2 * D)
    V = {"conv_b_a": b_in[:, :D], "conv_b_g": b_in[:, D:],
         "conv_dw": _pad_rows(vec_full[2:2 + CONV_WIDTH], CONV_PAD), "conv_dw_b": vec_full[33:34],
         "conv_ln_g": vec_full[34:35], "conv_ln_b": vec_full[35:36],
         "kv_ln_g": kv_ln_g.reshape(1, D), "kv_ln_b": kv_ln_b.reshape(1, D)}
    for l in range(2):
        for n in ("ln1_g", "ln1_b", "ln2_g", "ln2_b"):
            V[f"{n}{l}"] = w[n][l].reshape(1, D)

    half = HEAD_DIM // 2
    inv_freq = ROPE_THETA ** (-jnp.arange(half, dtype=F32) * (2.0 / HEAD_DIM))
    ang = positions[0].astype(F32)[:, None] * inv_freq
    cos, sin = jnp.cos(ang), jnp.sin(ang)
    cosf = jnp.concatenate([cos, cos], axis=-1)
    sinf = jnp.concatenate([-sin, sin], axis=-1)

    loss_cols, grad_x, gw, gv = _local_step(x[0], p[:, 0], cosf, sinf, loss_target[0], None, V, shards)
    loss = lax.psum(jnp.sum(loss_cols), ("x", "y", "c"))

    landed = _sibling_exchange(gw, c_arr)
    ps = {n: _pair_sum(gw[n], landed[n], c_arr, f"pair_sum_{n}") for n in gw}
    parts = _chip_scatter(ps)
    gh = {n: _chip_sum(parts[n], c_arr, f"chip_sum_{n}") for n in parts}
    g2 = _sibling_share(gh)

    out = {}
    for name, src, layer, kind, split in _MATS:
        out[src] = _adamw_mat(g2[name], w[src], mom[src], var[src], layer or 0, kind, out.get(src), f"adamw_{name}")

    gpack = jnp.concatenate([gv["conv_b_a"], gv["conv_b_g"], gv["conv_dw"][:CONV_WIDTH], gv["conv_dw_b"],
                             gv["conv_ln_g"], gv["conv_ln_b"], gv["kv_ln_g"], gv["kv_ln_b"],
                             gv["ln1_g0"], gv["ln1_g1"], gv["ln1_b0"], gv["ln1_b1"],
                             gv["ln2_g0"], gv["ln2_g1"], gv["ln2_b0"], gv["ln2_b1"]], axis=0)
    gsum = _sum8(_allgather8(_pad_rows(gpack, 48), "gather_vector_grads"), "sum_vector_grads")
    g_b = lax.dynamic_slice_in_dim(jnp.concatenate([gsum[0:1], gsum[1:2]], axis=1), chip * 2 * n4, 2 * n4, axis=1)
    g_sh = lax.dynamic_slice_in_dim(gsum[2:36], chip * n4, n4, axis=1)
    g_sh = _pad_rows(jnp.concatenate([g_b.reshape(2, n4), g_sh], axis=0), 40)
    d_sh, m_sh, v_sh = _adamw_small(g_sh, _pack_sharded(w), _pack_sharded(mom), _pack_sharded(var), "adamw_sharded_vectors")
    g_rep = _pad_rows(gsum[36:46], 16)
    d_rep, m_rep, v_rep = _adamw_small(g_rep, _pack_replicated(w), _pack_replicated(mom), _pack_replicated(var),
                                       "adamw_replicated_vectors")
    small = {}
    for i, (sh, rep) in enumerate(((g_sh, g_rep), (d_sh, d_rep), (m_sh, m_rep), (v_sh, v_rep))):
        d = {**_unpack_sharded(sh, w), **_unpack_replicated(rep, w)}
        for n, val in d.items():
            small.setdefault(n, [None] * 4)[i] = val
    for n in small:
        out[n] = small[n]

    res = [loss, grad_x[None]]
    for i in range(4):
        res += [out[n][i] for n in _WEIGHTS]
    return tuple(res)
```

```python
import functools

import jax
import jax.numpy as jnp
from jax import lax
from jax.experimental import pallas as pl
from jax.experimental.pallas import tpu as pltpu

F32 = jnp.float32
BF16 = jnp.bfloat16

HEAD_DIM = 128
ATTN_BLOCK = 128
DILATIONS = (1, 4, 16)
N_GROUPS = 3
CONV_WIDTH = 31
CONV_PAD = 32
ROPE_THETA = 10000.0
LN_EPS = 1e-5
ALPHA = 4.0 ** 0.25
ATTN_SCALE = HEAD_DIM ** -0.5
NEG = -1e30

ADAM_LR = 0.001
ADAM_B1 = 0.9
ADAM_B2 = 0.999
ADAM_EPS = 1e-08
ADAM_WD = 0.01
ADAM_STEP = 10

N_CHIPS = 4
VMEM_CAP = 60 << 20
MESH = pl.DeviceIdType.MESH
ANY = pl.BlockSpec(memory_space=pl.ANY)


def _vmem(nbytes):
    return int(min(max(2 * nbytes + (8 << 20), 24 << 20), VMEM_CAP))


def _fit(tile, n):
    if n <= tile:
        return n
    t = tile - tile % 128
    while n % t:
        t -= 128
    return t


def _nbytes(shape, dtype):
    n = 1
    for s in shape:
        n *= s
    return n * jnp.dtype(dtype).itemsize


_DIMS = {"nn": (((1,), (0,)), ((), ())), "nt": (((1,), (1,)), ((), ())), "tn": (((0,), (0,)), ((), ()))}


def _pcall(body, *, name, grid, in_specs, out_specs, out_shape, operands, scratch_shapes=(), vmem, carry=None):
    if carry is None:
        return pl.pallas_call(
            body, name=name, grid=grid, in_specs=in_specs, out_specs=out_specs, out_shape=out_shape,
            scratch_shapes=list(scratch_shapes),
            compiler_params=pltpu.CompilerParams(dimension_semantics=("arbitrary",) * len(grid), vmem_limit_bytes=vmem),
        )(*operands)
    n_in, n_out, n_scr = len(in_specs), len(out_specs), len(scratch_shapes)
    c_in, c_out = len(carry.ins), len(carry.out_shape)

    def wrapped(*refs):
        ins, refs = refs[:n_in], refs[n_in:]
        c_ins, refs = refs[:c_in], refs[c_in:]
        outs, refs = refs[:n_out], refs[n_out:]
        c_outs, refs = refs[:c_out], refs[c_out:]
        scr, c_sems = refs[:n_scr], refs[n_scr:]
        first = functools.reduce(jnp.logical_and, [pl.program_id(d) == 0 for d in range(len(grid))])
        last = functools.reduce(jnp.logical_and, [pl.program_id(d) == grid[d] - 1 for d in range(len(grid))])
        pl.when(first)(lambda: carry.start(c_ins, c_outs, c_sems))
        body(*ins, *outs, *scr)
        pl.when(last)(lambda: carry.finish(c_ins, c_outs, c_sems))

    res = pl.pallas_call(
        wrapped, name=name, grid=grid, in_specs=list(in_specs) + [ANY] * c_in, out_specs=list(out_specs) + [ANY] * c_out,
        out_shape=list(out_shape) + list(carry.out_shape), scratch_shapes=list(scratch_shapes) + list(carry.scratch),
        compiler_params=pltpu.CompilerParams(dimension_semantics=("arbitrary",) * len(grid), vmem_limit_bytes=vmem),
    )(*operands, *carry.ins)
    carry.set_result(res[n_out:])
    return res[:n_out]


def _mm(a, b, *, mode, outs, name, epilogue=None, extras=(), rextras=(), vecs=(), a_sel=None, b_sel=None,
        tm=None, tn=2048, tk=None, layout=None, carry=None):
    a2, b2 = a.shape[-2:], b.shape[-2:]
    if mode == "nn":
        (M, K), (K2, N) = a2, b2
    elif mode == "nt":
        (M, K), (N, K2) = a2, b2
    else:
        (K, M), (K2, N) = a2, b2
    assert K == K2, (a.shape, b.shape, mode)
    if tm is None:
        tm = 1024 if mode == "tn" else 512
    if tk is None:
        tk = 1024 if mode == "tn" else 2048
    if layout is not None:
        kind, nslots = layout
        r, c = (M // 2, N // nslots) if kind == "col" else (M // nslots, N // 2)
        tm, tn = _fit(tm, r), _fit(tn, c)
        assert r % tm == 0 and c % tn == 0
    else:
        tm, tn = _fit(tm, M), _fit(tn, N)
    tk = _fit(tk, K)
    assert M % tm == 0 and N % tn == 0 and K % tk == 0, (M, N, K, tm, tn, tk)
    nk = K // tk
    grid = (N // tn, M // tm, nk)

    def spec(arr, sel, blk, imap):
        if arr.ndim == 3:
            return pl.BlockSpec((None,) + blk, lambda j, i, k: (sel,) + imap(j, i, k))
        return pl.BlockSpec(blk, imap)

    if mode == "tn":
        a_spec = spec(a, a_sel, (tk, tm), lambda j, i, k: (k, i))
    else:
        a_spec = spec(a, a_sel, (tm, tk), lambda j, i, k: (i, k))
    if mode == "nt":
        b_spec = spec(b, b_sel, (tn, tk), lambda j, i, k: (j, k))
    else:
        b_spec = spec(b, b_sel, (tk, tn), lambda j, i, k: (k, j))
    in_specs = [a_spec, b_spec]
    in_specs += [pl.BlockSpec((tm, tn), lambda j, i, k: (i, j)) for _ in extras]
    in_specs += [pl.BlockSpec((tm, e.shape[1]), lambda j, i, k: (i, 0)) for e in rextras]
    in_specs += [pl.BlockSpec((1, tn), lambda j, i, k: (0, j)) for _ in vecs]

    if layout is None:
        out_shape = [jax.ShapeDtypeStruct((M, N), d) for d in outs]
        out_specs = [pl.BlockSpec((tm, tn), lambda j, i, k: (i, j)) for _ in outs]
    else:
        assert len(outs) == 1
        out_shape = [jax.ShapeDtypeStruct((2, nslots, r, c), outs[0])]
        rb, cb = r // tm, c // tn
        if kind == "col":
            omap = lambda j, i, k: (i // rb, j // cb, i % rb, j % cb)
        else:
            omap = lambda j, i, k: (j // cb, i // rb, i % rb, j % cb)
        out_specs = [pl.BlockSpec((None, None, tm, tn), omap)]

    ne, nr, nv, no = len(extras), len(rextras), len(vecs), len(outs)
    dims = _DIMS[mode]

    def body(*refs):
        a_ref, b_ref = refs[0], refs[1]
        rest = refs[2:2 + ne + nr + nv]
        o_refs = refs[2 + ne + nr + nv:2 + ne + nr + nv + no]

        def finish(total):
            res = (total,) * no if epilogue is None else epilogue(total, *[x[...] for x in rest])
            for o, val in zip(o_refs, res):
                o[...] = val.astype(o.dtype)

        part = lax.dot_general(a_ref[...].astype(BF16), b_ref[...].astype(BF16), dims, preferred_element_type=F32)
        if nk == 1:
            finish(part)
            return
        acc = refs[-1]
        k = pl.program_id(2)

        @pl.when(k == 0)
        def _():
            acc[...] = part

        @pl.when((k > 0) & (k < nk - 1))
        def _():
            acc[...] += part

        @pl.when(k == nk - 1)
        def _():
            finish(acc[...] + part)

    blk = (_nbytes((tm, tk), a.dtype) + _nbytes((tk, tn), b.dtype) + sum(_nbytes((tm, tn), e.dtype) for e in extras)
           + sum(_nbytes((tm, tn), d) for d in outs) + 2 * tm * tn * 4)
    res = _pcall(body, name=name, grid=grid, in_specs=in_specs, out_specs=out_specs, out_shape=out_shape,
                 operands=(a, b, *extras, *rextras, *vecs),
                 scratch_shapes=[pltpu.VMEM((tm, tn), F32)] if nk > 1 else [], vmem=_vmem(blk), carry=carry)
    return res[0] if no == 1 else tuple(res)


def _rows(fn, rows, vecs, outs, sums, *, tm, name, carry=None):
    S = rows[0].shape[0]
    tm = min(tm, S)
    assert S % tm == 0
    nr, nv, no, ns = len(rows), len(vecs), len(outs), len(sums)

    def body(*refs):
        vals = fn(*[r[...] for r in refs[:nr + nv]])
        o_refs = refs[nr + nv:nr + nv + no]
        s_refs = refs[nr + nv + no:]
        for o, val in zip(o_refs, vals[:no]):
            o[...] = val.astype(o.dtype)
        if ns:
            @pl.when(pl.program_id(0) == 0)
            def _():
                for s in s_refs:
                    s[...] = jnp.zeros_like(s)

            for s, val in zip(s_refs, vals[no:]):
                s[...] += jnp.sum(val.astype(F32), axis=0, keepdims=True)

    in_specs = [pl.BlockSpec((tm, r.shape[1]), lambda i: (i, 0)) for r in rows]
    in_specs += [pl.BlockSpec(v.shape, lambda i: (0, 0)) for v in vecs]
    out_specs = [pl.BlockSpec((tm, c), lambda i: (i, 0)) for c, _ in outs]
    out_specs += [pl.BlockSpec((1, c), lambda i: (0, 0)) for c in sums]
    out_shape = [jax.ShapeDtypeStruct((S, c), d) for c, d in outs]
    out_shape += [jax.ShapeDtypeStruct((1, c), F32) for c in sums]
    blk = sum(_nbytes((tm, r.shape[1]), r.dtype) for r in rows) + sum(_nbytes((tm, c), d) for c, d in outs)
    blk += 6 * tm * max(r.shape[1] for r in rows) * 4
    res = _pcall(body, name=name, grid=(S // tm,), in_specs=in_specs, out_specs=out_specs, out_shape=out_shape,
                 operands=(*rows, *vecs), vmem=_vmem(blk), carry=carry)
    return tuple(res)


def _ln_norm(z):
    mu = jnp.mean(z, axis=-1, keepdims=True)
    d = z - mu
    var = jnp.mean(d * d, axis=-1, keepdims=True)
    rstd = lax.rsqrt(var + LN_EPS)
    return d * rstd, rstd


def _ln(z, g, b):
    return _ln_norm(z)[0] * g + b


def _ln_bwd(dy, n, rstd, g):
    dn = dy * g
    return rstd * (dn - jnp.mean(dn, axis=-1, keepdims=True) - n * jnp.mean(dn * n, axis=-1, keepdims=True))


def _sigmoid(x):
    return 1.0 / (1.0 + jnp.exp(-x))


def _per_head(x, fn):
    h = x.shape[1] // HEAD_DIM
    return jnp.concatenate([fn(x[:, i * HEAD_DIM:(i + 1) * HEAD_DIM], i) for i in range(h)], axis=1)


def _rot(x, cosf, sinf):
    return _per_head(x, lambda xh, i: xh * cosf + pltpu.roll(xh, HEAD_DIM // 2, 1) * sinf)


def _rot_t(dy, cosf, sinf):
    return _per_head(dy, lambda dh, i: dh * cosf + pltpu.roll(dh * sinf, HEAD_DIM // 2, 1))


def _conv_fwd(glu, dw, dwb, *, tm=256, tc=512, name="conv_fwd", carry=None):
    S, D = glu.shape
    tm, tc = min(tm, S), min(tc, D)
    ni = S // tm

    def body(cur_ref, prev_ref, dw_ref, dwb_ref, o_ref, win):
        i = pl.program_id(1)
        tail = prev_ref[tm - CONV_PAD:tm, :]
        win[0:CONV_PAD, :] = jnp.where(i > 0, tail, jnp.zeros_like(tail))
        win[CONV_PAD:CONV_PAD + tm, :] = cur_ref[...]
        acc = jnp.zeros((tm, tc), F32) + dwb_ref[...]
        for k in range(CONV_WIDTH):
            acc = acc + win[pl.ds(CONV_PAD - CONV_WIDTH + 1 + k, tm), :] * dw_ref[k:k + 1, :]
        o_ref[...] = acc

    return _pcall(
        body, name=name, grid=(D // tc, ni),
        in_specs=[pl.BlockSpec((tm, tc), lambda j, i: (i, j)),
                  pl.BlockSpec((tm, tc), lambda j, i: (jnp.maximum(i - 1, 0), j)),
                  pl.BlockSpec((CONV_PAD, tc), lambda j, i: (0, j)),
                  pl.BlockSpec((1, tc), lambda j, i: (0, j))],
        out_specs=[pl.BlockSpec((tm, tc), lambda j, i: (i, j))],
        out_shape=[jax.ShapeDtypeStruct((S, D), F32)],
        scratch_shapes=[pltpu.VMEM((tm + CONV_PAD, tc), F32)],
        operands=(glu, glu, dw, dwb), vmem=_vmem(8 * tm * tc * 4), carry=carry)[0]


def _conv_bwd(dc, glu, a_pre, g_pre, dw, ba, bg, *, tm=256, tc=512, name="conv_bwd", carry=None):
    S, D = dc.shape
    tm, tc = min(tm, S), min(tc, D)
    ni = S // tm

    def body(dc_ref, dcn_ref, glu_ref, glup_ref, a_ref, g_ref, dw_ref, ba_ref, bg_ref,
             da_ref, dg_ref, ddw_ref, dba_ref, dbg_ref, dwin, gwin):
        i = pl.program_id(1)

        @pl.when(i == 0)
        def _():
            ddw_ref[...] = jnp.zeros_like(ddw_ref)
            dba_ref[...] = jnp.zeros_like(dba_ref)
            dbg_ref[...] = jnp.zeros_like(dbg_ref)

        dcur = dc_ref[...]
        head = dcn_ref[0:CONV_PAD, :]
        dwin[0:tm, :] = dcur
        dwin[tm:tm + CONV_PAD, :] = jnp.where(i < ni - 1, head, jnp.zeros_like(head))
        tail = glup_ref[tm - CONV_PAD:tm, :]
        gwin[0:CONV_PAD, :] = jnp.where(i > 0, tail, jnp.zeros_like(tail))
        gwin[CONV_PAD:CONV_PAD + tm, :] = glu_ref[...]
        dglu = jnp.zeros((tm, tc), F32)
        for k in range(CONV_WIDTH):
            dglu = dglu + dwin[pl.ds(CONV_WIDTH - 1 - k, tm), :] * dw_ref[k:k + 1, :]
            shifted = gwin[pl.ds(CONV_PAD - CONV_WIDTH + 1 + k, tm), :]
            ddw_ref[k:k + 1, :] += jnp.sum(dcur * shifted, axis=0, keepdims=True)
        a = a_ref[...] + ba_ref[...]
        sg = _sigmoid(g_ref[...] + bg_ref[...])
        da = dglu * sg
        dg = dglu * a * sg * (1.0 - sg)
        da_ref[...] = da.astype(BF16)
        dg_ref[...] = dg.astype(BF16)
        dba_ref[...] += jnp.sum(da, axis=0, keepdims=True)
        dbg_ref[...] += jnp.sum(dg, axis=0, keepdims=True)

    tile = lambda f: pl.BlockSpec((tm, tc), f)
    vec = pl.BlockSpec((1, tc), lambda j, i: (0, j))
    return _pcall(
        body, name=name, grid=(D // tc, ni),
        in_specs=[tile(lambda j, i: (i, j)), tile(lambda j, i: (jnp.minimum(i + 1, ni - 1), j)),
                  tile(lambda j, i: (i, j)), tile(lambda j, i: (jnp.maximum(i - 1, 0), j)),
                  tile(lambda j, i: (i, j)), tile(lambda j, i: (i, j)),
                  pl.BlockSpec((CONV_PAD, tc), lambda j, i: (0, j)), vec, vec],
        out_specs=[tile(lambda j, i: (i, j)), tile(lambda j, i: (i, j)),
                   pl.BlockSpec((CONV_PAD, tc), lambda j, i: (0, j)), vec, vec],
        out_shape=[jax.ShapeDtypeStruct((S, D), BF16), jax.ShapeDtypeStruct((S, D), BF16),
                   jax.ShapeDtypeStruct((CONV_PAD, D), F32), jax.ShapeDtypeStruct((1, D), F32),
                   jax.ShapeDtypeStruct((1, D), F32)],
        scratch_shapes=[pltpu.VMEM((tm + CONV_PAD, tc), F32), pltpu.VMEM((tm + CONV_PAD, tc), F32)],
        operands=(dc, dc, glu, glu, a_pre, g_pre, dw, ba, bg), vmem=_vmem(16 * tm * tc * 4), carry=carry)


def _nt(a, b):
    return lax.dot_general(a, b, _DIMS["nt"], preferred_element_type=F32)


def _tn(a, b):
    return lax.dot_general(a, b, _DIMS["tn"], preferred_element_type=F32)


def _window_mask(qi, kj, first_key):
    B = ATTN_BLOCK
    return ((kj < B) & (kj >= qi) & (kj >= first_key)) | ((kj >= B) & (kj - B <= qi))


def _attn_fwd(q_rot, k, v, g, dil):
    S, D = k.shape
    H = D // HEAD_DIM
    L = S // dil
    nb_count = L // ATTN_BLOCK
    B = ATTN_BLOCK

    def body(q_ref, kc_ref, kp_ref, vc_ref, vp_ref, o_ref, lse_ref):
        nb = pl.program_id(1)
        qi = lax.broadcasted_iota(jnp.int32, (B, 2 * B), 0)
        kj = lax.broadcasted_iota(jnp.int32, (B, 2 * B), 1)
        valid = _window_mask(qi, kj, jnp.where(nb > 0, 0, B))
        for h in range(H):
            hs = slice(h * HEAD_DIM, (h + 1) * HEAD_DIM)
            kk = jnp.concatenate([kp_ref[:, hs], kc_ref[:, hs]], axis=0)
            vv = jnp.concatenate([vp_ref[:, hs], vc_ref[:, hs]], axis=0)
            s = jnp.where(valid, _nt(q_ref[:, hs], kk) * ATTN_SCALE, NEG)
            m = jnp.max(s, axis=1, keepdims=True)
            p = jnp.exp(s - m)
            l = jnp.sum(p, axis=1, keepdims=True)
            o = jnp.dot(p.astype(BF16), vv, preferred_element_type=F32) / l
            o_ref[:, hs] = o.astype(o_ref.dtype)
            lse_ref[:, hs] = jnp.broadcast_to(m + jnp.log(l), (B, HEAD_DIM))

    blk = lambda f: pl.BlockSpec((B, D), f)
    cur = lambda r, nb: (nb, r)
    prev = lambda r, nb: (jnp.maximum(nb - 1, 0), r)
    o, lse = pl.pallas_call(
        body, name=f"attn_fwd_g{g}", grid=(dil, nb_count),
        in_specs=[blk(lambda r, nb: (nb, r * N_GROUPS + g)), blk(cur), blk(prev), blk(cur), blk(prev)],
        out_specs=[blk(cur), blk(cur)],
        out_shape=[jax.ShapeDtypeStruct((L, dil * D), BF16), jax.ShapeDtypeStruct((L, dil * D), F32)],
        compiler_params=pltpu.CompilerParams(dimension_semantics=("parallel", "arbitrary"),
                                             vmem_limit_bytes=_vmem(12 * B * D * 4)),
    )(q_rot.reshape(L, dil * N_GROUPS * D), k.reshape(L, dil * D), k.reshape(L, dil * D),
      v.reshape(L, dil * D), v.reshape(L, dil * D))
    return o.reshape(S, D), lse.reshape(S, D)


def _attn_bwd(q_rot, k, v, do, lse, dlt, g, dil, *, name, carry=None):
    S, D = k.shape
    H = D // HEAD_DIM
    L = S // dil
    nb_count = L // ATTN_BLOCK
    B = ATTN_BLOCK

    def body(q_ref, qn_ref, kc_ref, kp_ref, vc_ref, vp_ref, do_ref, don_ref, l_ref, ln_ref, d_ref, dn_ref,
             dq_ref, dk_ref, dv_ref):
        nb = pl.program_id(1)
        qi = lax.broadcasted_iota(jnp.int32, (B, 2 * B), 0)
        kj = lax.broadcasted_iota(jnp.int32, (B, 2 * B), 1)
        valid_q = _window_mask(qi, kj, jnp.where(nb > 0, 0, B))
        qr = lax.broadcasted_iota(jnp.int32, (2 * B, B), 0)
        kc_i = lax.broadcasted_iota(jnp.int32, (2 * B, B), 1)
        q_end = jnp.where(nb < nb_count - 1, 2 * B, B)
        valid_k = ((qr < B) & (kc_i <= qr)) | ((qr >= B) & (qr < q_end) & (kc_i >= qr - B))
        for h in range(H):
            hs = slice(h * HEAD_DIM, (h + 1) * HEAD_DIM)
            q, qn = q_ref[:, hs], qn_ref[:, hs]
            kc, vc = kc_ref[:, hs], vc_ref[:, hs]
            dout, doutn = do_ref[:, hs], don_ref[:, hs]
            lq, lqn = l_ref[:, hs][:, 0:1], ln_ref[:, hs][:, 0:1]
            dq_, dqn_ = d_ref[:, hs][:, 0:1], dn_ref[:, hs][:, 0:1]
            kk = jnp.concatenate([kp_ref[:, hs], kc], axis=0)
            vv = jnp.concatenate([vp_ref[:, hs], vc], axis=0)
            s = jnp.where(valid_q, _nt(q, kk) * ATTN_SCALE, NEG)
            p = jnp.exp(s - lq)
            ds = p * (_nt(dout, vv) - dq_)
            dq_ref[:, hs] = (jnp.dot(ds.astype(BF16), kk, preferred_element_type=F32) * ATTN_SCALE).astype(dq_ref.dtype)
            qq = jnp.concatenate([q, qn], axis=0)
            dd = jnp.concatenate([dout, doutn], axis=0)
            ll = jnp.concatenate([lq, lqn], axis=0)
            dl = jnp.concatenate([dq_, dqn_], axis=0)
            s2 = jnp.where(valid_k, _nt(qq, kc) * ATTN_SCALE, NEG)
            p2 = jnp.exp(s2 - ll)
            dv_ref[:, hs] = _tn(p2.astype(BF16), dd)
            ds2 = p2 * (_nt(dd, vc) - dl)
            dk_ref[:, hs] = _tn(ds2.astype(BF16), qq) * ATTN_SCALE

    blk = lambda f: pl.BlockSpec((B, D), f)
    cur = lambda r, nb: (nb, r)
    prev = lambda r, nb: (jnp.maximum(nb - 1, 0), r)
    nxt = lambda r, nb: (jnp.minimum(nb + 1, nb_count - 1), r)
    qcur = lambda r, nb: (nb, r * N_GROUPS + g)
    qnxt = lambda r, nb: (jnp.minimum(nb + 1, nb_count - 1), r * N_GROUPS + g)
    qv = q_rot.reshape(L, dil * N_GROUPS * D)
    view = lambda t: t.reshape(L, dil * D)
    dq, dk, dv = _pcall(
        body, name=name, grid=(dil, nb_count),
        in_specs=[blk(qcur), blk(qnxt), blk(cur), blk(prev), blk(cur), blk(prev), blk(cur), blk(nxt),
                  blk(cur), blk(nxt), blk(cur), blk(nxt)],
        out_specs=[blk(cur), blk(cur), blk(cur)],
        out_shape=[jax.ShapeDtypeStruct((L, dil * D), BF16), jax.ShapeDtypeStruct((L, dil * D), F32),
                   jax.ShapeDtypeStruct((L, dil * D), F32)],
        operands=(qv, qv, view(k), view(k), view(v), view(v), view(do), view(do), view(lse), view(lse), view(dlt),
                  view(dlt)),
        vmem=_vmem(24 * B * D * 4), carry=carry)
    return dq.reshape(S, D), dk.reshape(S, D), dv.reshape(S, D)


def _mlp_ple_fwd(z1, h1b, p_l, W, vec, l, run):
    D = z1.shape[1]
    g1, b1, g2, b2 = vec

    def act_ep(acc):
        return acc, jnp.square(jnp.maximum(acc, 0.0))

    t, act = run(_mm, h1b, W[f"mlp_up{l}"], mode="nn", outs=[BF16, BF16], epilogue=act_ep, name=f"mlp_up{l}")
    f = run(_mm, act, W[f"mlp_down{l}"], mode="nn", outs=[F32], name=f"mlp_down{l}")

    def z2_fn(z1_t, f_t, g1_, b1_, g2_, b2_):
        z2 = ALPHA * _ln(z1_t, g1_, b1_) + f_t
        return z2, _ln(z2, g2_, b2_)

    z2, h2b = _rows(z2_fn, [z1, f], [g1, b1, g2, b2], [(D, F32), (D, BF16)], [], tm=256, name=f"ln2_fwd{l}")
    pp = run(_mm, p_l, W[f"ple_proj{l}"], mode="nn", outs=[F32], name=f"ple_proj{l}")
    gpre = run(_mm, h2b, W[f"ple_gate{l}"], mode="nn", outs=[F32], name=f"ple_gate{l}")
    return t, act, z2, h2b, pp, gpre


def _mlp_ple_bwd(dy, d_pp, d_gpre, p_l, z1, h1b, t, act, z2, h2b, wts, vec, l, run, produce):
    D = z1.shape[1]
    up, down, pp_w, pg_w = wts
    g1, b1, g2, b2 = vec
    produce(f"ple_proj{l}", run(_mm, p_l, d_pp, mode="tn", outs=[BF16], layout=("col", N_CHIPS), name=f"d_ple_proj{l}"))
    produce(f"ple_gate{l}", run(_mm, h2b, d_gpre, mode="tn", outs=[BF16], layout=("row", N_CHIPS), name=f"d_ple_gate{l}"))
    dh2 = run(_mm, d_gpre, pg_w, mode="nt", outs=[F32], extras=[dy], epilogue=lambda acc, e: (acc + e,),
              name=f"dh2_{l}")

    def ln2_bwd(dh2_t, z2_t, g2_):
        n, rstd = _ln_norm(z2_t)
        dz2 = _ln_bwd(dh2_t, n, rstd, g2_)
        return dz2, dz2, dh2_t * n, dh2_t

    dz2, dz2b, dg2, db2 = _rows(ln2_bwd, [dh2, z2], [g2], [(D, F32), (D, BF16)], [D, D], tm=256,
                                name=f"ln2_bwd{l}")
    produce(f"mlp_down{l}", run(_mm, act, dz2b, mode="tn", outs=[BF16], layout=("row", N_CHIPS), name=f"d_mlp_down{l}"))
    dt = run(_mm, dz2b, down, mode="nt", outs=[BF16], extras=[t],
             epilogue=lambda acc, t_: (acc * 2.0 * jnp.maximum(t_.astype(F32), 0.0),), name=f"dt{l}")
    produce(f"mlp_up{l}", run(_mm, h1b, dt, mode="tn", outs=[BF16], layout=("col", N_CHIPS), name=f"d_mlp_up{l}"))
    dh1 = run(_mm, dt, up, mode="nt", outs=[F32], extras=[dz2], epilogue=lambda acc, e: (acc + ALPHA * e,),
              name=f"dh1_{l}")

    def ln1_bwd(dh1_t, z1_t, g1_):
        n, rstd = _ln_norm(z1_t)
        dz1 = _ln_bwd(dh1_t, n, rstd, g1_)
        return dz1, dz1, dh1_t * n, dh1_t

    dz1, dz1b, dg1, db1 = _rows(ln1_bwd, [dh1, z1], [g1], [(D, F32), (D, BF16)], [D, D], tm=256,
                                name=f"ln1_bwd{l}")
    return dz1, dz1b, (dg1, db1, dg2, db2)


def _ple_out(z2, pp, gpre, g2, b2):
    gt = _sigmoid(gpre)
    return _ln(z2, g2, b2) + pp * gt, gt


_GATHER_AT = {
    "conv_in_a": ("conv_w_out",),
    "conv_in_g": ("ple_gate0", "ple_proj0"),
    "conv_fwd": ("mlp_up0",),
    "mlp_up0": ("mlp_down0",),
    "mlp_down0": ("attn_w_q",),
    "ple_gate0": ("w_kv",),
    "kv_k": ("attn_w_o",),
    "kv_v": ("ple_proj1", "ple_gate1"),
    "attn_q": ("mlp_up1",),
    "mlp_up1": ("mlp_down1",),
}
_GATHER_FIRST = ("conv_w_in",)


def _local_step(x, p, cosf, sinf, target, W, V, shards=None, reducer=None):
    S, D = x.shape
    gw, gv = {}, {}
    if shards is not None:
        W = dict(_Gather(_GATHER_FIRST, shards).run_alone("gather_first"))

    def run(fn, *args, name, **kw):
        gather = _Gather(_GATHER_AT[name], shards) if (shards is not None and name in _GATHER_AT) else None
        carry = gather if reducer is None or gather is not None else reducer.carry(name)
        out = fn(*args, name=name, carry=carry, **kw)
        if gather is not None:
            W.update(gather.result)
        elif reducer is not None:
            reducer.carried()
        return out

    a_pre = run(_mm, x, W["conv_w_in"], b_sel=0, mode="nn", outs=[F32], name="conv_in_a")
    g_pre = run(_mm, x, W["conv_w_in"], b_sel=1, mode="nn", outs=[F32], name="conv_in_g")
    (glu,) = _rows(lambda a, g, ba, bg: ((a + ba) * _sigmoid(g + bg),), [a_pre, g_pre], [V["conv_b_a"], V["conv_b_g"]],
                   [(D, F32)], [], tm=256, name="glu_fwd")
    cv = run(_conv_fwd, glu, V["conv_dw"], V["conv_dw_b"], name="conv_fwd")

    def silu_ln(c, g_, b_):
        y = _ln(c, g_, b_)
        return (y * _sigmoid(y),)

    (sb,) = _rows(silu_ln, [cv], [V["conv_ln_g"], V["conv_ln_b"]], [(D, BF16)], [], tm=256, name="conv_ln_fwd")
    mix0 = _mm(sb, W["conv_w_out"], mode="nn", outs=[F32], name="conv_out")

    def z1_fn(x_t, mix_t, g_, b_):
        z1 = ALPHA * x_t + mix_t
        return z1, _ln(z1, g_, b_)

    vec0 = (V["ln1_g0"], V["ln1_b0"], V["ln2_g0"], V["ln2_b0"])
    vec1 = (V["ln1_g1"], V["ln1_b1"], V["ln2_g1"], V["ln2_b1"])
    z1_0, h1b_0 = _rows(z1_fn, [x, mix0], [vec0[0], vec0[1]], [(D, F32), (D, BF16)], [], tm=256, name="ln1_fwd0")
    t0, act0, z2_0, h2b_0, pp0, gpre0 = _mlp_ple_fwd(z1_0, h1b_0, p[0], W, vec0, 0, run)

    def x1_fn(z2, pp, gpre, g2, b2, kg, kb):
        x1, _ = _ple_out(z2, pp, gpre, g2, b2)
        return x1, _ln(x1, kg, kb)

    x1, kvn = _rows(x1_fn, [z2_0, pp0, gpre0], [vec0[2], vec0[3], V["kv_ln_g"], V["kv_ln_b"]],
                    [(D, F32), (D, BF16)], [], tm=256, name="ple_out0")

    rot_ep = lambda acc, c_, s_: (_rot(acc, c_, s_),)
    k_rot = run(_mm, kvn, W["w_kv"], b_sel=0, mode="nn", outs=[BF16], rextras=[cosf, sinf], epilogue=rot_ep, name="kv_k")
    v_b = run(_mm, kvn, W["w_kv"], b_sel=1, mode="nn", outs=[BF16], name="kv_v")
    q_rot = run(_mm, x1, W["attn_w_q"], mode="nn", outs=[BF16], rextras=[cosf, sinf], epilogue=rot_ep, name="attn_q")
    og, lg = [], []
    for g, dil in enumerate(DILATIONS):
        o_g, l_g = _attn_fwd(q_rot, k_rot, v_b, g, dil)
        og.append(o_g)
        lg.append(l_g)

    def merge(o0, o1, o2, l0, l1, l2):
        m = jnp.maximum(jnp.maximum(l0, l1), l2)
        e0, e1, e2 = jnp.exp(l0 - m), jnp.exp(l1 - m), jnp.exp(l2 - m)
        den = e0 + e1 + e2
        o = (e0 * o0.astype(F32) + e1 * o1.astype(F32) + e2 * o2.astype(F32)) / den
        return o, m + jnp.log(den)

    ob, lse = _rows(merge, og + lg, [], [(D, BF16), (D, F32)], [], tm=256, name="attn_merge")
    mix1 = _mm(ob, W["attn_w_o"], mode="nn", outs=[F32], name="attn_out")
    z1_1, h1b_1 = _rows(z1_fn, [x1, mix1], [vec1[0], vec1[1]], [(D, F32), (D, BF16)], [], tm=256, name="ln1_fwd1")
    t1, act1, z2_1, h2b_1, pp1, gpre1 = _mlp_ple_fwd(z1_1, h1b_1, p[1], W, vec1, 1, run)
    wts0 = (W["mlp_up0"], W["mlp_down0"], W["ple_proj0"], W["ple_gate0"])
    wts1 = (W["mlp_up1"], W["mlp_down1"], W["ple_proj1"], W["ple_gate1"])

    def head(z2, pp, gpre, tgt, g2, b2):
        y, gt = _ple_out(z2, pp, gpre, g2, b2)
        err = y - tgt
        dy = err * (1.0 / D)
        return dy, dy * gt, dy * pp * gt * (1.0 - gt), 0.5 * err * err * (1.0 / D)

    dy1, d_pp1, d_gpre1, loss_cols = _rows(head, [z2_1, pp1, gpre1, target], [vec1[2], vec1[3]],
                                           [(D, F32), (D, BF16), (D, BF16)], [D], tm=256, name="loss_head")

    def produce(name, grad):
        gw[name] = grad
        if reducer is not None:
            reducer.produced(name, grad)

    dz1_1, dz1b_1, (gv["ln1_g1"], gv["ln1_b1"], gv["ln2_g1"], gv["ln2_b1"]) = _mlp_ple_bwd(
        dy1, d_pp1, d_gpre1, p[1], z1_1, h1b_1, t1, act1, z2_1, h2b_1, wts1, vec1, 1, run, produce)
    produce("attn_w_o", run(_mm, ob, dz1b_1, mode="tn", outs=[BF16], layout=("row", N_CHIPS), name="d_attn_w_o"))

    def do_ep(acc, o_t):
        prod = acc * o_t.astype(F32)
        dlt = _per_head(prod, lambda ph, i: jnp.broadcast_to(jnp.sum(ph, axis=1, keepdims=True), ph.shape))
        return acc, dlt

    do_b, dlt = run(_mm, dz1b_1, W["attn_w_o"], mode="nt", outs=[BF16, F32], extras=[ob], epilogue=do_ep, name="attn_do")
    dqs, dks, dvs = [], [], []
    for g, dil in enumerate(DILATIONS):
        dq_g, dk_g, dv_g = run(_attn_bwd, q_rot, k_rot, v_b, do_b, lse, dlt, g, dil, name=f"attn_bwd_g{g}")
        dqs.append(dq_g)
        dks.append(dk_g)
        dvs.append(dv_g)

    def unrot(q0, q1, q2, k0, k1, k2, v0, v1, v2, c_, s_):
        dq = jnp.concatenate([_rot_t(t_.astype(F32), c_, s_) for t_ in (q0, q1, q2)], axis=1)
        return dq, _rot_t(k0 + k1 + k2, c_, s_), v0 + v1 + v2

    dq, dk, dv = run(_rows, unrot, dqs + dks + dvs + [cosf, sinf], [], [(N_GROUPS * D, BF16), (D, BF16), (D, BF16)], [],
                     tm=128, name="attn_unrot")
    produce("attn_w_q", run(_mm, x1, dq, mode="tn", outs=[BF16], layout=("col", N_CHIPS), name="d_attn_w_q"))
    dx1_q = run(_mm, dq, W["attn_w_q"], mode="nt", outs=[F32], extras=[dz1_1],
                epilogue=lambda acc, e: (acc + ALPHA * e,), name="dx1_q")
    produce("w_kv", jnp.concatenate(
        [run(_mm, kvn, dk, mode="tn", outs=[BF16], layout=("col", 2), name="d_w_kv_k"),
         run(_mm, kvn, dv, mode="tn", outs=[BF16], layout=("col", 2), name="d_w_kv_v")], axis=1))
    dkvn_k = run(_mm, dk, W["w_kv"], b_sel=0, mode="nt", outs=[F32], name="dkvn_k")
    dkvn = run(_mm, dv, W["w_kv"], b_sel=1, mode="nt", outs=[F32], extras=[dkvn_k], epilogue=lambda acc, e: (acc + e,),
               name="dkvn_v")

    def x1_bwd(dx1q_t, dkvn_t, x1_t, pp, gpre, kg):
        n, rstd = _ln_norm(x1_t)
        dy = dx1q_t + _ln_bwd(dkvn_t, n, rstd, kg)
        gt = _sigmoid(gpre)
        return dy, dy * gt, dy * pp * gt * (1.0 - gt), dkvn_t * n, dkvn_t

    dy0, d_pp0, d_gpre0, gv["kv_ln_g"], gv["kv_ln_b"] = _rows(
        x1_bwd, [dx1_q, dkvn, x1, pp0, gpre0], [V["kv_ln_g"]], [(D, F32), (D, BF16), (D, BF16)], [D, D], tm=256,
        name="x1_bwd")

    dz1_0, dz1b_0, (gv["ln1_g0"], gv["ln1_b0"], gv["ln2_g0"], gv["ln2_b0"]) = _mlp_ple_bwd(
        dy0, d_pp0, d_gpre0, p[0], z1_0, h1b_0, t0, act0, z2_0, h2b_0, wts0, vec0, 0, run, produce)
    produce("conv_w_out", run(_mm, sb, dz1b_0, mode="tn", outs=[BF16], layout=("row", N_CHIPS), name="d_conv_w_out"))
    ds = run(_mm, dz1b_0, W["conv_w_out"], mode="nt", outs=[F32], name="conv_ds")

    def conv_ln_bwd(ds_t, c_t, g_, b_):
        n, rstd = _ln_norm(c_t)
        y = n * g_ + b_
        sg = _sigmoid(y)
        dln = ds_t * sg * (1.0 + y * (1.0 - sg))
        dc = _ln_bwd(dln, n, rstd, g_)
        return dc, dln * n, dln, dc

    dc, gv["conv_ln_g"], gv["conv_ln_b"], gv["conv_dw_b"] = _rows(
        conv_ln_bwd, [ds, cv], [V["conv_ln_g"], V["conv_ln_b"]], [(D, F32)], [D, D, D], tm=256, name="conv_ln_bwd")
    da, dg, gv["conv_dw"], gv["conv_b_a"], gv["conv_b_g"] = run(
        _conv_bwd, dc, glu, a_pre, g_pre, V["conv_dw"], V["conv_b_a"], V["conv_b_g"], name="conv_bwd")
    produce("conv_w_in", jnp.concatenate(
        [run(_mm, x, da, mode="tn", outs=[BF16], layout=("col", 2), name="d_conv_w_in_a"),
         run(_mm, x, dg, mode="tn", outs=[BF16], layout=("col", 2), name="d_conv_w_in_g")], axis=1))
    dx_a = run(_mm, da, W["conv_w_in"], b_sel=0, mode="nt", outs=[F32], extras=[dz1_0],
               epilogue=lambda acc, e: (acc + ALPHA * e,), name="dx_a")
    grad_x = run(_mm, dg, W["conv_w_in"], b_sel=1, mode="nt", outs=[F32], extras=[dx_a],
                 epilogue=lambda acc, e: (acc + e,), name="dx_g")
    if reducer is not None:
        reducer.carry("share_last").run_alone("share_last")
        reducer.carried()
    return loss_cols, grad_x, gw, gv


def _place():
    x, y, c = lax.axis_index("x"), lax.axis_index("y"), lax.axis_index("c")
    chips = [(1 - x, y), (x, 1 - y), (1 - x, 1 - y)]
    return x, y, c, chips


def _remote(src, dst, ssem, rsem, dev):
    return pltpu.make_async_remote_copy(src_ref=src, dst_ref=dst, send_sem=ssem, recv_sem=rsem, device_id=dev,
                                        device_id_type=MESH)


def _allgather8(block, name):
    R, C = block.shape

    def body(x_ref, out_ref, send_sems, recv_sems, local_sem):
        x, y, c, chips = _place()
        me, sibling = (x, y, c), (x, y, 1 - c)

        def slot(px, py, pc):
            return out_ref.at[4 * px + 2 * py + pc]

        def copy(k, blockpos, to, src=None):
            return _remote(slot(*blockpos) if src is None else src, slot(*blockpos), send_sems.at[k], recv_sems.at[k], to)

        mine = pltpu.make_async_copy(x_ref, slot(*me), local_sem)
        mine.start()
        first = [copy(0, me, sibling, src=x_ref)]
        first += [copy(1 + j, me, (*chip, c), src=x_ref) for j, chip in enumerate(chips)]
        for cp in first:
            cp.start()
        passed = [copy(4 + j, (*chip, c), sibling) for j, chip in enumerate(chips)]
        for j, chip in enumerate(chips):
            copy(1 + j, (*chip, c), me).wait_recv()
            passed[j].start()
        copy(0, sibling, me).wait_recv()
        for j, chip in enumerate(chips):
            copy(4 + j, (*chip, 1 - c), me).wait_recv()
        for cp in first + passed:
            cp.wait_send()
        mine.wait()

    return pl.pallas_call(
        body, name=name, out_shape=jax.ShapeDtypeStruct((8, R, C), block.dtype),
        in_specs=[pl.BlockSpec(memory_space=pltpu.VMEM)], out_specs=pl.BlockSpec(memory_space=pltpu.VMEM),
        scratch_shapes=[pltpu.SemaphoreType.DMA((7,)), pltpu.SemaphoreType.DMA((7,)), pltpu.SemaphoreType.DMA],
        compiler_params=pltpu.CompilerParams(vmem_limit_bytes=_vmem(10 * _nbytes((R, C), block.dtype))),
    )(block)


_MATS = (
    ("conv_w_in", "conv_w_in", 0, "col", True),
    ("conv_w_out", "conv_w_out", 0, "row", False),
    ("mlp_up0", "mlp_up", 0, "col", False),
    ("mlp_down0", "mlp_down", 0, "row", False),
    ("ple_proj0", "ple_proj", 0, "col", False),
    ("ple_gate0", "ple_gate", 0, "row", False),
    ("w_kv", "w_kv", None, "col", True),
    ("attn_w_q", "attn_w_q", 0, "col", False),
    ("attn_w_o", "attn_w_o", 0, "row", False),
    ("mlp_up1", "mlp_up", 1, "col", False),
    ("mlp_down1", "mlp_down", 1, "row", False),
    ("ple_proj1", "ple_proj", 1, "col", False),
    ("ple_gate1", "ple_gate", 1, "row", False),
)


class _Carry:
    result = None

    def set_result(self, outs):
        self.result = dict(zip(self.names, outs))

    def run_alone(self, name):
        n_in, n_out = len(self.ins), len(self.out_shape)

        def body(*refs):
            in_refs, out_refs, sems = refs[:n_in], refs[n_in:n_in + n_out], refs[n_in + n_out:]
            self.start(in_refs, out_refs, sems)
            self.finish(in_refs, out_refs, sems)

        outs = pl.pallas_call(body, name=name, out_shape=self.out_shape, in_specs=[ANY] * n_in, out_specs=[ANY] * n_out,
                              scratch_shapes=self.scratch)(*self.ins)
        self.set_result(outs)
        return self.result


class _Gather(_Carry):
    def __init__(self, names, shards):
        mats = [m for m in _MATS if m[0] in names]
        srcs = sorted({m[1] for m in mats})
        self.names = [m[0] for m in mats]
        self.ins = [shards[n] for n in srcs]
        self.out_shape, self.geo = [], []
        for name, src, layer, kind, split in mats:
            s = shards[src]
            ks, ns = s.shape[-2:]
            K, N = (ks, ns * N_CHIPS) if kind == "col" else (ks * N_CHIPS, ns)
            self.out_shape.append(jax.ShapeDtypeStruct((2, K, N // 2) if split else (K, N), BF16))
            self.geo.append((srcs.index(src), layer if s.ndim == 3 else None, kind, split, K, N))
        T = len(mats)
        self.scratch = [pltpu.SemaphoreType.DMA((3 * T,)) for _ in range(4)] + [pltpu.SemaphoreType.DMA((T,))]
        self.result = None

    def _copies(self, in_refs, out_refs, sems):
        geo, T = self.geo, len(self.geo)
        s_ici, r_ici, s_d2d, r_d2d, lsem = sems
        x, y, c, chips = _place()
        me = 2 * x + y
        sibling = (x, y, 1 - c)
        idx = [2 * cx + cy for cx, cy in chips]

        def src_ref(t):
            i, layer, _, _, _, _ = geo[t]
            return in_refs[i] if layer is None else in_refs[i].at[layer]

        def src_half(t, h):
            _, _, kind, _, K, N = geo[t]
            if kind == "col":
                return src_ref(t).at[pl.ds(h * (K // 2), K // 2), :]
            return src_ref(t).at[:, pl.ds(h * (N // 2), N // 2)]

        def dst(t, j, h):
            _, _, kind, split, K, N = geo[t]
            n, k = N // N_CHIPS, K // N_CHIPS
            if kind == "col":
                rows = slice(None) if h is None else pl.ds(h * (K // 2), K // 2)
                if split:
                    return out_refs[t].at[j // 2, rows, pl.ds((j % 2) * n, n)]
                return out_refs[t].at[rows, pl.ds(j * n, n)]
            cols = slice(None) if h is None else pl.ds(h * (N // 2), N // 2)
            return out_refs[t].at[pl.ds(j * k, k), cols]

        local = [pltpu.make_async_copy(src_ref(t), dst(t, me, None), lsem.at[t]) for t in range(T)]
        sends = [_remote(src_half(t, c), dst(t, me, c), s_ici.at[3 * t + kk], r_ici.at[3 * t + kk], (*chips[kk], c))
                 for t in range(T) for kk in range(3)]
        hops = []
        for t in range(T):
            for kk in range(3):
                mine, theirs = dst(t, idx[kk], c), dst(t, idx[kk], 1 - c)
                hops.append((_remote(mine, mine, s_ici.at[3 * t + kk], r_ici.at[3 * t + kk], sibling),
                             _remote(mine, mine, s_d2d.at[3 * t + kk], r_d2d.at[3 * t + kk], sibling),
                             _remote(theirs, theirs, s_d2d.at[3 * t + kk], r_d2d.at[3 * t + kk], sibling)))
        return local, sends, hops

    def start(self, in_refs, out_refs, sems):
        local, sends, _ = self._copies(in_refs, out_refs, sems)
        for cp in local + sends:
            cp.start()

    def finish(self, in_refs, out_refs, sems):
        local, sends, hops = self._copies(in_refs, out_refs, sems)
        for landed, forward, _ in hops:
            landed.wait_recv()
            forward.start()
        for _, _, from_sibling in hops:
            from_sibling.wait_recv()
        for cp in sends + [h[1] for h in hops]:
            cp.wait_send()
        for cp in local:
            cp.wait()


class _Multi(_Carry):
    def __init__(self, parts):
        self.parts = parts
        self.ins = [a for p in parts for a in p.ins]
        self.out_shape = [a for p in parts for a in p.out_shape]
        self.scratch = [a for p in parts for a in p.scratch]

    def _split(self, seq, field):
        out, at = [], 0
        for p in self.parts:
            n = len(getattr(p, field))
            out.append(seq[at:at + n])
            at += n
        return out

    def _each(self, method, in_refs, out_refs, sems):
        for p, i, o, s in zip(self.parts, self._split(in_refs, "ins"), self._split(out_refs, "out_shape"),
                              self._split(sems, "scratch")):
            getattr(p, method)(i, o, s)

    def start(self, in_refs, out_refs, sems):
        self._each("start", in_refs, out_refs, sems)

    def finish(self, in_refs, out_refs, sems):
        self._each("finish", in_refs, out_refs, sems)

    def set_result(self, outs):
        for p, o in zip(self.parts, self._split(list(outs), "out_shape")):
            p.set_result(o)


class _PairSend(_Carry):
    def __init__(self, grads):
        self.names = list(grads)
        self.ins = [grads[n] for n in self.names]
        self.out_shape = [jax.ShapeDtypeStruct(a.shape[1:], BF16) for a in self.ins]
        T = len(self.names)
        self.scratch = [pltpu.SemaphoreType.DMA((T,)), pltpu.SemaphoreType.DMA((T,))]

    def _copies(self, in_refs, out_refs, sems):
        x, y, c, _ = _place()
        return [_remote(in_refs[t].at[1 - c], out_refs[t], sems[0].at[t], sems[1].at[t], (x, y, 1 - c))
                for t in range(len(self.names))]

    def start(self, in_refs, out_refs, sems):
        for cp in self._copies(in_refs, out_refs, sems):
            cp.start()

    def finish(self, in_refs, out_refs, sems):
        for cp in self._copies(in_refs, out_refs, sems):
            cp.wait()


class _ChipScatter(_Carry):
    def __init__(self, sums):
        self.names = list(sums)
        self.ins = [sums[n] for n in self.names]
        self.out_shape = [jax.ShapeDtypeStruct(a.shape, BF16) for a in self.ins]
        T = len(self.names)
        self.scratch = [pltpu.SemaphoreType.DMA((3 * T,)), pltpu.SemaphoreType.DMA((3 * T,)), pltpu.SemaphoreType.DMA((T,))]

    def _copies(self, in_refs, out_refs, sems):
        ssem, rsem, lsem = sems
        x, y, c, chips = _place()
        me = 2 * x + y
        idx = [2 * cx + cy for cx, cy in chips]
        T = len(self.names)
        local = [pltpu.make_async_copy(in_refs[t].at[me], out_refs[t].at[me], lsem.at[t]) for t in range(T)]
        sends = [_remote(in_refs[t].at[idx[kk]], out_refs[t].at[me], ssem.at[3 * t + kk], rsem.at[3 * t + kk],
                         (*chips[kk], c)) for t in range(T) for kk in range(3)]
        lands = [_remote(out_refs[t].at[idx[kk]], out_refs[t].at[idx[kk]], ssem.at[3 * t + kk], rsem.at[3 * t + kk],
                         (*chips[kk], c)) for t in range(T) for kk in range(3)]
        return local, sends, lands

    def start(self, in_refs, out_refs, sems):
        local, sends, _ = self._copies(in_refs, out_refs, sems)
        for cp in local + sends:
            cp.start()

    def finish(self, in_refs, out_refs, sems):
        local, sends, lands = self._copies(in_refs, out_refs, sems)
        for cp in lands:
            cp.wait_recv()
        for cp in sends:
            cp.wait_send()
        for cp in local:
            cp.wait()


class _PairShare(_Carry):
    def __init__(self, halves):
        self.names = list(halves)
        self.ins = [halves[n] for n in self.names]
        self.out_shape = [jax.ShapeDtypeStruct((2,) + a.shape, F32) for a in self.ins]
        T = len(self.names)
        self.scratch = [pltpu.SemaphoreType.DMA((T,)), pltpu.SemaphoreType.DMA((T,)), pltpu.SemaphoreType.DMA((T,))]

    def _copies(self, in_refs, out_refs, sems):
        ssem, rsem, lsem = sems
        x, y, c, _ = _place()
        sibling = (x, y, 1 - c)
        T = len(self.names)
        local = [pltpu.make_async_copy(in_refs[t], out_refs[t].at[c], lsem.at[t]) for t in range(T)]
        sends = [_remote(in_refs[t], out_refs[t].at[c], ssem.at[t], rsem.at[t], sibling) for t in range(T)]
        lands = [_remote(out_refs[t].at[1 - c], out_refs[t].at[1 - c], ssem.at[t], rsem.at[t], sibling) for t in range(T)]
        return local, sends, lands

    def start(self, in_refs, out_refs, sems):
        local, sends, _ = self._copies(in_refs, out_refs, sems)
        for cp in local + sends:
            cp.start()

    def finish(self, in_refs, out_refs, sems):
        local, sends, lands = self._copies(in_refs, out_refs, sems)
        for cp in lands:
            cp.wait_recv()
        for cp in sends:
            cp.wait_send()
        for cp in local:
            cp.wait()


def _pair_sum(own, landed, c_arr, name):
    _, ns, r, cc = own.shape
    rows = ns * r
    tr = min(512, rows)
    assert rows % tr == 0

    def body(c_ref, a_ref, b_ref, o_ref):
        o_ref[...] = (a_ref[...].astype(F32) + b_ref[...].astype(F32)).astype(o_ref.dtype)

    out = pl.pallas_call(
        body, name=name, out_shape=jax.ShapeDtypeStruct((rows, cc), BF16),
        grid_spec=pltpu.PrefetchScalarGridSpec(
            num_scalar_prefetch=1, grid=(rows // tr,),
            in_specs=[pl.BlockSpec((None, tr, cc), lambda i, c_ref: (c_ref[0], i, 0)),
                      pl.BlockSpec((tr, cc), lambda i, c_ref: (i, 0))],
            out_specs=pl.BlockSpec((tr, cc), lambda i, c_ref: (i, 0))),
        compiler_params=pltpu.CompilerParams(dimension_semantics=("parallel",), vmem_limit_bytes=_vmem(8 * tr * cc * 4)),
    )(c_arr, own.reshape(2, rows, cc), landed.reshape(rows, cc))
    return out.reshape(ns, r, cc)


def _chip_sum(parts, name):
    _, r, cc = parts.shape
    tr = _fit(256, r)

    def body(p_ref, o_ref):
        acc = p_ref[0].astype(F32)
        for j in range(1, N_CHIPS):
            acc = acc + p_ref[j].astype(F32)
        o_ref[...] = acc

    return pl.pallas_call(
        body, name=name, out_shape=jax.ShapeDtypeStruct((r, cc), F32), grid=(r // tr,),
        in_specs=[pl.BlockSpec((N_CHIPS, tr, cc), lambda i: (0, i, 0))], out_specs=pl.BlockSpec((tr, cc), lambda i: (i, 0)),
        compiler_params=pltpu.CompilerParams(dimension_semantics=("parallel",), vmem_limit_bytes=_vmem(12 * tr * cc * 4)),
    )(parts)


def _adamw_math(w, g, m, v):
    m2 = ADAM_B1 * m + (1.0 - ADAM_B1) * g
    v2 = ADAM_B2 * v + (1.0 - ADAM_B2) * jnp.square(g)
    m_hat = m2 / (1.0 - ADAM_B1 ** ADAM_STEP)
    v_hat = v2 / (1.0 - ADAM_B2 ** ADAM_STEP)
    delta = -ADAM_LR * (m_hat / (jnp.sqrt(v_hat) + ADAM_EPS) + ADAM_WD * w)
    return delta, m2, v2


def _adamw_mat(g2, w, m, v, layer, kind, prev, name):
    shape = w.shape
    ks, ns = shape[-2:]
    _, r, cc = g2.shape
    tr, tc = _fit(256, r), _fit(1024, cc)
    assert (r, cc) == ((ks // 2, ns) if kind == "col" else (ks, ns // 2))
    assert r % tr == 0 and cc % tc == 0
    rb, cb = r // tr, cc // tc
    if kind == "col":
        g_spec = pl.BlockSpec((None, tr, tc), lambda i, j: (i // rb, i % rb, j))
    else:
        g_spec = pl.BlockSpec((None, tr, tc), lambda i, j: (j // cb, i, j % cb))
    if w.ndim == 3:
        w_spec = pl.BlockSpec((None, tr, tc), lambda i, j: (layer, i, j))
    else:
        w_spec = pl.BlockSpec((tr, tc), lambda i, j: (i, j))
    n_prev = 0 if prev is None else 4

    def body(*refs):
        g_ref, w_ref, m_ref, v_ref = refs[:4]
        go_ref, d_ref, mo_ref, vo_ref = refs[4 + n_prev:]
        g = g_ref[...]
        delta, m2, v2 = _adamw_math(w_ref[...], g, m_ref[...], v_ref[...])
        go_ref[...] = g
        d_ref[...] = delta
        mo_ref[...] = m2
        vo_ref[...] = v2

    return pl.pallas_call(
        body, name=name, grid=(ks // tr, ns // tc),
        in_specs=[g_spec, w_spec, w_spec, w_spec] + [ANY] * n_prev, out_specs=[w_spec] * 4,
        out_shape=[jax.ShapeDtypeStruct(shape, F32)] * 4,
        input_output_aliases={4 + i: i for i in range(n_prev)},
        compiler_params=pltpu.CompilerParams(dimension_semantics=("parallel", "parallel"),
                                             vmem_limit_bytes=_vmem(16 * tr * tc * 4)),
    )(g2, w, m, v, *(prev or ()))


def _adamw_small(g, w, m, v, name):
    def body(g_ref, w_ref, m_ref, v_ref, d_ref, mo_ref, vo_ref):
        delta, m2, v2 = _adamw_math(w_ref[...], g_ref[...], m_ref[...], v_ref[...])
        d_ref[...] = delta
        mo_ref[...] = m2
        vo_ref[...] = v2

    return pl.pallas_call(body, name=name, out_shape=[jax.ShapeDtypeStruct(w.shape, F32)] * 3)(g, w, m, v)


def _sum8(parts, name):
    def body(p_ref, o_ref):
        acc = p_ref[0]
        for j in range(1, 8):
            acc = acc + p_ref[j]
        o_ref[...] = acc

    return pl.pallas_call(body, name=name, out_shape=jax.ShapeDtypeStruct(parts.shape[1:], F32),
                          compiler_params=pltpu.CompilerParams(vmem_limit_bytes=_vmem(12 * _nbytes(parts.shape[1:], F32))))(parts)


_REDUCE_AT = {
    "d_ple_gate1": (("A", "ple_proj1"),),
    "dh2_1": (("A", "ple_gate1"),),
    "d_mlp_down1": (("B", "ple_proj1"), ("B", "ple_gate1")),
    "dt1": (("A", "mlp_down1"),),
    "d_mlp_up1": (("B", "mlp_down1"), ("C", "ple_proj1"), ("C", "ple_gate1")),
    "dh1_1": (("A", "mlp_up1"),),
    "d_attn_w_o": (("C", "mlp_down1"),),
    "attn_do": (("A", "attn_w_o"),),
    "attn_bwd_g0": (("B", "mlp_up1"),),
    "attn_bwd_g1": (("B", "attn_w_o"),),
    "attn_unrot": (("C", "mlp_up1"), ("C", "attn_w_o")),
    "dx1_q": (("A", "attn_w_q"),),
    "dkvn_k": (("A", "w_kv"),),
    "d_ple_gate0": (("A", "ple_proj0"),),
    "dh2_0": (("A", "ple_gate0"),),
    "d_mlp_down0": (("B", "attn_w_q"), ("B", "ple_proj0")),
    "dt0": (("B", "w_kv"), ("B", "ple_gate0"), ("A", "mlp_down0")),
    "d_mlp_up0": (("B", "mlp_down0"), ("C", "attn_w_q"), ("C", "ple_proj0"), ("C", "w_kv"), ("C", "ple_gate0")),
    "dh1_0": (("A", "mlp_up0"),),
    "d_conv_w_out": (("C", "mlp_down0"),),
    "conv_ds": (("A", "conv_w_out"),),
    "conv_bwd": (("B", "mlp_up0"), ("B", "conv_w_out")),
    "d_conv_w_in_g": (("C", "mlp_up0"), ("C", "conv_w_out")),
    "dx_a": (("A", "conv_w_in"),),
    "dx_g": (("B", "conv_w_in"),),
    "share_last": (("C", "conv_w_in"),),
}


class _Reducer:
    def __init__(self, w, mom, var, c_arr):
        self.w, self.mom, self.var, self.c_arr = w, mom, var, c_arr
        self.mats = {m[0]: m for m in _MATS}
        self.grads, self.pair_sums, self.chip_sums, self.out = {}, {}, {}, {}

    def produced(self, name, grad):
        self.grads[name] = grad

    def carry(self, call):
        parts = []
        for cls, stage, src in ((_PairSend, "A", self.grads), (_ChipScatter, "B", self.pair_sums),
                                (_PairShare, "C", self.chip_sums)):
            names = [n for s, n in _REDUCE_AT.get(call, ()) if s == stage]
            if names:
                parts.append((stage, cls({n: src[n] for n in names})))
        self._parts = parts
        return _Multi([p for _, p in parts]) if parts else None

    def carried(self):
        for stage, part in self._parts:
            for name, val in part.result.items():
                if stage == "A":
                    self.pair_sums[name] = _pair_sum(self.grads[name], val, self.c_arr, f"pair_sum_{name}")
                elif stage == "B":
                    self.chip_sums[name] = _chip_sum(val, f"chip_sum_{name}")
                else:
                    _, src, layer, kind, _ = self.mats[name]
                    self.out[src] = _adamw_mat(val, self.w[src], self.mom[src], self.var[src], layer or 0, kind,
                                               self.out.get(src), f"adamw_{name}")
        self._parts = []


_WEIGHTS = ("conv_w_in", "conv_b_in", "conv_dw", "conv_dw_b", "conv_ln_g", "conv_ln_b", "conv_w_out", "kv_ln_g",
            "kv_ln_b", "w_kv", "attn_w_q", "attn_w_o", "ln1_g", "ln1_b", "mlp_up", "mlp_down", "ln2_g", "ln2_b",
            "ple_proj", "ple_gate")
_SHARDED_VECS = ("conv_b_in", "conv_dw", "conv_dw_b", "conv_ln_g", "conv_ln_b")
_REPLICATED_VECS = ("kv_ln_g", "kv_ln_b", "ln1_g", "ln1_b", "ln2_g", "ln2_b")


def _pad_rows(a, rows):
    return jnp.concatenate([a, jnp.zeros((rows - a.shape[0], a.shape[1]), a.dtype)], axis=0) if a.shape[0] < rows else a


def _pack_sharded(d):
    n = d["conv_dw_b"].shape[-1]
    rows = [d["conv_b_in"].reshape(2, n), d["conv_dw"].reshape(CONV_WIDTH, n), d["conv_dw_b"].reshape(1, n),
            d["conv_ln_g"].reshape(1, n), d["conv_ln_b"].reshape(1, n)]
    return _pad_rows(jnp.concatenate(rows, axis=0), 40)


def _unpack_sharded(pack, like):
    n = pack.shape[1]
    return {"conv_b_in": pack[0:2].reshape(like["conv_b_in"].shape),
            "conv_dw": pack[2:2 + CONV_WIDTH].reshape(like["conv_dw"].shape),
            "conv_dw_b": pack[33:34].reshape(like["conv_dw_b"].shape),
            "conv_ln_g": pack[34:35].reshape(like["conv_ln_g"].shape),
            "conv_ln_b": pack[35:36].reshape(like["conv_ln_b"].shape)}


def _pack_replicated(d):
    D = d["kv_ln_g"].shape[-1]
    rows = [d[n].reshape(-1, D) for n in _REPLICATED_VECS]
    return _pad_rows(jnp.concatenate(rows, axis=0), 16)


def _unpack_replicated(pack, like):
    out, r = {}, 0
    for n in _REPLICATED_VECS:
        k = like[n].size // pack.shape[1]
        out[n] = pack[r:r + k].reshape(like[n].shape)
        r += k
    return out


def kernel(x, p, positions, conv_w_in, conv_b_in, conv_dw, conv_dw_b, conv_ln_g, conv_ln_b, conv_w_out, kv_ln_g, kv_ln_b, w_kv, attn_w_q, attn_w_o, ln1_g, ln1_b, mlp_up, mlp_down, ln2_g, ln2_b, ple_proj, ple_gate, loss_target, m_conv_w_in, m_conv_b_in, m_conv_dw, m_conv_dw_b, m_conv_ln_g, m_conv_ln_b, m_conv_w_out, m_kv_ln_g, m_kv_ln_b, m_w_kv, m_attn_w_q, m_attn_w_o, m_ln1_g, m_ln1_b, m_mlp_up, m_mlp_down, m_ln2_g, m_ln2_b, m_ple_proj, m_ple_gate, v_conv_w_in, v_conv_b_in, v_conv_dw, v_conv_dw_b, v_conv_ln_g, v_conv_ln_b, v_conv_w_out, v_kv_ln_g, v_kv_ln_b, v_w_kv, v_attn_w_q, v_attn_w_o, v_ln1_g, v_ln1_b, v_mlp_up, v_mlp_down, v_ln2_g, v_ln2_b, v_ple_proj, v_ple_gate):
    args = dict(locals())
    w = {n: args[n] for n in _WEIGHTS}
    mom = {n: args["m_" + n] for n in _WEIGHTS}
    var = {n: args["v_" + n] for n in _WEIGHTS}
    S, D = x.shape[1:]
    n4 = D // N_CHIPS
    chip = 2 * lax.axis_index("x") + lax.axis_index("y")
    c_arr = lax.axis_index("c").astype(jnp.int32).reshape(1)

    shards = {n: w[n].astype(BF16) for n in sorted({m[1] for m in _MATS})}
    vec_all = _allgather8(_pack_sharded(w), "gather_vectors")
    vec_full = jnp.concatenate([vec_all[2 * j] for j in range(N_CHIPS)], axis=1)
    b_in = vec_all[0::2, 0:2, :].reshape(1, 2 * D)
    V = {"conv_b_a": b_in[:, :D], "conv_b_g": b_in[:, D:],
         "conv_dw": _pad_rows(vec_full[2:2 + CONV_WIDTH], CONV_PAD), "conv_dw_b": vec_full[33:34],
         "conv_ln_g": vec_full[34:35], "conv_ln_b": vec_full[35:36],
         "kv_ln_g": kv_ln_g.reshape(1, D), "kv_ln_b": kv_ln_b.reshape(1, D)}
    for l in range(2):
        for n in ("ln1_g", "ln1_b", "ln2_g", "ln2_b"):
            V[f"{n}{l}"] = w[n][l].reshape(1, D)

    half = HEAD_DIM // 2
    inv_freq = ROPE_THETA ** (-jnp.arange(half, dtype=F32) * (2.0 / HEAD_DIM))
    ang = positions[0].astype(F32)[:, None] * inv_freq
    cos, sin = jnp.cos(ang), jnp.sin(ang)
    cosf = jnp.concatenate([cos, cos], axis=-1)
    sinf = jnp.concatenate([-sin, sin], axis=-1)

    reducer = _Reducer(w, mom, var, c_arr)
    loss_cols, grad_x, _, gv = _local_step(x[0], p[:, 0], cosf, sinf, loss_target[0], None, V, shards, reducer)
    loss = lax.psum(jnp.sum(loss_cols), ("x", "y", "c"))
    out = dict(reducer.out)

    gpack = jnp.concatenate([gv["conv_b_a"], gv["conv_b_g"], gv["conv_dw"][:CONV_WIDTH], gv["conv_dw_b"],
                             gv["conv_ln_g"], gv["conv_ln_b"], gv["kv_ln_g"], gv["kv_ln_b"],
                             gv["ln1_g0"], gv["ln1_g1"], gv["ln1_b0"], gv["ln1_b1"],
                             gv["ln2_g0"], gv["ln2_g1"], gv["ln2_b0"], gv["ln2_b1"]], axis=0)
    gsum = _sum8(_allgather8(_pad_rows(gpack, 48), "gather_vector_grads"), "sum_vector_grads")
    g_b = lax.dynamic_slice_in_dim(jnp.concatenate([gsum[0:1], gsum[1:2]], axis=1), chip * 2 * n4, 2 * n4, axis=1)
    g_sh = lax.dynamic_slice_in_dim(gsum[2:36], chip * n4, n4, axis=1)
    g_sh = _pad_rows(jnp.concatenate([g_b.reshape(2, n4), g_sh], axis=0), 40)
    d_sh, m_sh, v_sh = _adamw_small(g_sh, _pack_sharded(w), _pack_sharded(mom), _pack_sharded(var), "adamw_sharded_vectors")
    g_rep = _pad_rows(gsum[36:46], 16)
    d_rep, m_rep, v_rep = _adamw_small(g_rep, _pack_replicated(w), _pack_replicated(mom), _pack_replicated(var),
                                       "adamw_replicated_vectors")
    small = {}
    for i, (sh, rep) in enumerate(((g_sh, g_rep), (d_sh, d_rep), (m_sh, m_rep), (v_sh, v_rep))):
        d = {**_unpack_sharded(sh, w), **_unpack_replicated(rep, w)}
        for n, val in d.items():
            small.setdefault(n, [None] * 4)[i] = val
    for n in small:
        out[n] = small[n]

    res = [loss, grad_x[None]]
    for i in range(4):
        res += [out[n][i] for n in _WEIGHTS]
    return tuple(res)
```

```python
import functools

import jax
import jax.numpy as jnp
from jax import lax
from jax.experimental import pallas as pl
from jax.experimental.pallas import tpu as pltpu

F32 = jnp.float32
BF16 = jnp.bfloat16

HEAD_DIM = 128
ATTN_BLOCK = 128
DILATIONS = (1, 4, 16)
N_GROUPS = 3
CONV_WIDTH = 31
CONV_PAD = 32
ROPE_THETA = 10000.0
LN_EPS = 1e-5
ALPHA = 4.0 ** 0.25
ATTN_SCALE = HEAD_DIM ** -0.5
NEG = -1e30

ADAM_LR = 0.001
ADAM_B1 = 0.9
ADAM_B2 = 0.999
ADAM_EPS = 1e-08
ADAM_WD = 0.01
ADAM_STEP = 10

N_CHIPS = 4
VMEM_CAP = 60 << 20
MESH = pl.DeviceIdType.MESH
ANY = pl.BlockSpec(memory_space=pl.ANY)


def _vmem(nbytes):
    return int(min(max(2 * nbytes + (8 << 20), 24 << 20), VMEM_CAP))


def _fit(tile, n):
    if n <= tile:
        return n
    t = tile - tile % 128
    while n % t:
        t -= 128
    return t


def _nbytes(shape, dtype):
    n = 1
    for s in shape:
        n *= s
    return n * jnp.dtype(dtype).itemsize


_DIMS = {"nn": (((1,), (0,)), ((), ())), "nt": (((1,), (1,)), ((), ())), "tn": (((0,), (0,)), ((), ()))}


def _pcall(body, *, name, grid, in_specs, out_specs, out_shape, operands, scratch_shapes=(), vmem, carry=None):
    if carry is None:
        return pl.pallas_call(
            body, name=name, grid=grid, in_specs=in_specs, out_specs=out_specs, out_shape=out_shape,
            scratch_shapes=list(scratch_shapes),
            compiler_params=pltpu.CompilerParams(dimension_semantics=("arbitrary",) * len(grid), vmem_limit_bytes=vmem),
        )(*operands)
    n_in, n_out, n_scr = len(in_specs), len(out_specs), len(scratch_shapes)
    c_in, c_out = len(carry.ins), len(carry.out_shape)

    def wrapped(*refs):
        ins, refs = refs[:n_in], refs[n_in:]
        c_ins, refs = refs[:c_in], refs[c_in:]
        outs, refs = refs[:n_out], refs[n_out:]
        c_outs, refs = refs[:c_out], refs[c_out:]
        scr, c_sems = refs[:n_scr], refs[n_scr:]
        first = functools.reduce(jnp.logical_and, [pl.program_id(d) == 0 for d in range(len(grid))])
        last = functools.reduce(jnp.logical_and, [pl.program_id(d) == grid[d] - 1 for d in range(len(grid))])
        pl.when(first)(lambda: carry.start(c_ins, c_outs, c_sems))
        body(*ins, *outs, *scr)
        pl.when(last)(lambda: carry.finish(c_ins, c_outs, c_sems))

    res = pl.pallas_call(
        wrapped, name=name, grid=grid, in_specs=list(in_specs) + [ANY] * c_in, out_specs=list(out_specs) + [ANY] * c_out,
        out_shape=list(out_shape) + list(carry.out_shape), scratch_shapes=list(scratch_shapes) + list(carry.scratch),
        compiler_params=pltpu.CompilerParams(dimension_semantics=("arbitrary",) * len(grid), vmem_limit_bytes=vmem),
    )(*operands, *carry.ins)
    carry.set_result(res[n_out:])
    return res[:n_out]


def _mm(a, b, *, mode, outs, name, epilogue=None, extras=(), rextras=(), vecs=(), a_sel=None, b_sel=None,
        tm=None, tn=2048, tk=None, layout=None, carry=None):
    a2, b2 = a.shape[-2:], b.shape[-2:]
    if mode == "nn":
        (M, K), (K2, N) = a2, b2
    elif mode == "nt":
        (M, K), (N, K2) = a2, b2
    else:
        (K, M), (K2, N) = a2, b2
    assert K == K2, (a.shape, b.shape, mode)
    if tm is None:
        tm = 1024 if mode == "tn" else 512
    if tk is None:
        tk = 1024 if mode == "tn" else 2048
    if layout is not None:
        kind, nslots = layout
        r, c = (M // 2, N // nslots) if kind == "col" else (M // nslots, N // 2)
        tm, tn = _fit(tm, r), _fit(tn, c)
        assert r % tm == 0 and c % tn == 0
    else:
        tm, tn = _fit(tm, M), _fit(tn, N)
    tk = _fit(tk, K)
    assert M % tm == 0 and N % tn == 0 and K % tk == 0, (M, N, K, tm, tn, tk)
    nk = K // tk
    grid = (N // tn, M // tm, nk)

    def spec(arr, sel, blk, imap):
        if arr.ndim == 3:
            return pl.BlockSpec((None,) + blk, lambda j, i, k: (sel,) + imap(j, i, k))
        return pl.BlockSpec(blk, imap)

    if mode == "tn":
        a_spec = spec(a, a_sel, (tk, tm), lambda j, i, k: (k, i))
    else:
        a_spec = spec(a, a_sel, (tm, tk), lambda j, i, k: (i, k))
    if mode == "nt":
        b_spec = spec(b, b_sel, (tn, tk), lambda j, i, k: (j, k))
    else:
        b_spec = spec(b, b_sel, (tk, tn), lambda j, i, k: (k, j))
    in_specs = [a_spec, b_spec]
    in_specs += [pl.BlockSpec((tm, tn), lambda j, i, k: (i, j)) for _ in extras]
    in_specs += [pl.BlockSpec((tm, e.shape[1]), lambda j, i, k: (i, 0)) for e in rextras]
    in_specs += [pl.BlockSpec((1, tn), lambda j, i, k: (0, j)) for _ in vecs]

    if layout is None:
        out_shape = [jax.ShapeDtypeStruct((M, N), d) for d in outs]
        out_specs = [pl.BlockSpec((tm, tn), lambda j, i, k: (i, j)) for _ in outs]
    else:
        assert len(outs) == 1
        out_shape = [jax.ShapeDtypeStruct((2, nslots, r, c), outs[0])]
        rb, cb = r // tm, c // tn
        if kind == "col":
            omap = lambda j, i, k: (i // rb, j // cb, i % rb, j % cb)
        else:
            omap = lambda j, i, k: (j // cb, i // rb, i % rb, j % cb)
        out_specs = [pl.BlockSpec((None, None, tm, tn), omap)]

    ne, nr, nv, no = len(extras), len(rextras), len(vecs), len(outs)
    dims = _DIMS[mode]

    def body(*refs):
        a_ref, b_ref = refs[0], refs[1]
        rest = refs[2:2 + ne + nr + nv]
        o_refs = refs[2 + ne + nr + nv:2 + ne + nr + nv + no]

        def finish(total):
            res = (total,) * no if epilogue is None else epilogue(total, *[x[...] for x in rest])
            for o, val in zip(o_refs, res):
                o[...] = val.astype(o.dtype)

        part = lax.dot_general(a_ref[...].astype(BF16), b_ref[...].astype(BF16), dims, preferred_element_type=F32)
        if nk == 1:
            finish(part)
            return
        acc = refs[-1]
        k = pl.program_id(2)

        @pl.when(k == 0)
        def _():
            acc[...] = part

        @pl.when((k > 0) & (k < nk - 1))
        def _():
            acc[...] += part

        @pl.when(k == nk - 1)
        def _():
            finish(acc[...] + part)

    blk = (_nbytes((tm, tk), a.dtype) + _nbytes((tk, tn), b.dtype) + sum(_nbytes((tm, tn), e.dtype) for e in extras)
           + sum(_nbytes((tm, tn), d) for d in outs) + 2 * tm * tn * 4)
    res = _pcall(body, name=name, grid=grid, in_specs=in_specs, out_specs=out_specs, out_shape=out_shape,
                 operands=(a, b, *extras, *rextras, *vecs),
                 scratch_shapes=[pltpu.VMEM((tm, tn), F32)] if nk > 1 else [], vmem=_vmem(blk), carry=carry)
    return res[0] if no == 1 else tuple(res)


def _rows(fn, rows, vecs, outs, sums, *, tm, name, carry=None):
    S = rows[0].shape[0]
    tm = min(tm, S)
    assert S % tm == 0
    nr, nv, no, ns = len(rows), len(vecs), len(outs), len(sums)

    def body(*refs):
        vals = fn(*[r[...] for r in refs[:nr + nv]])
        o_refs = refs[nr + nv:nr + nv + no]
        s_refs = refs[nr + nv + no:]
        for o, val in zip(o_refs, vals[:no]):
            o[...] = val.astype(o.dtype)
        if ns:
            @pl.when(pl.program_id(0) == 0)
            def _():
                for s in s_refs:
                    s[...] = jnp.zeros_like(s)

            for s, val in zip(s_refs, vals[no:]):
                s[...] += jnp.sum(val.astype(F32), axis=0, keepdims=True)

    in_specs = [pl.BlockSpec((tm, r.shape[1]), lambda i: (i, 0)) for r in rows]
    in_specs += [pl.BlockSpec(v.shape, lambda i: (0, 0)) for v in vecs]
    out_specs = [pl.BlockSpec((tm, c), lambda i: (i, 0)) for c, _ in outs]
    out_specs += [pl.BlockSpec((1, c), lambda i: (0, 0)) for c in sums]
    out_shape = [jax.ShapeDtypeStruct((S, c), d) for c, d in outs]
    out_shape += [jax.ShapeDtypeStruct((1, c), F32) for c in sums]
    blk = sum(_nbytes((tm, r.shape[1]), r.dtype) for r in rows) + sum(_nbytes((tm, c), d) for c, d in outs)
    blk += 6 * tm * max(r.shape[1] for r in rows) * 4
    res = _pcall(body, name=name, grid=(S // tm,), in_specs=in_specs, out_specs=out_specs, out_shape=out_shape,
                 operands=(*rows, *vecs), vmem=_vmem(blk), carry=carry)
    return tuple(res)


def _ln_norm(z):
    mu = jnp.mean(z, axis=-1, keepdims=True)
    d = z - mu
    var = jnp.mean(d * d, axis=-1, keepdims=True)
    rstd = lax.rsqrt(var + LN_EPS)
    return d * rstd, rstd


def _ln(z, g, b):
    return _ln_norm(z)[0] * g + b


def _ln_bwd(dy, n, rstd, g):
    dn = dy * g
    return rstd * (dn - jnp.mean(dn, axis=-1, keepdims=True) - n * jnp.mean(dn * n, axis=-1, keepdims=True))


def _sigmoid(x):
    return 1.0 / (1.0 + jnp.exp(-x))


def _per_head(x, fn):
    h = x.shape[1] // HEAD_DIM
    return jnp.concatenate([fn(x[:, i * HEAD_DIM:(i + 1) * HEAD_DIM], i) for i in range(h)], axis=1)


def _rot(x, cosf, sinf):
    return _per_head(x, lambda xh, i: xh * cosf + pltpu.roll(xh, HEAD_DIM // 2, 1) * sinf)


def _rot_t(dy, cosf, sinf):
    return _per_head(dy, lambda dh, i: dh * cosf + pltpu.roll(dh * sinf, HEAD_DIM // 2, 1))


def _conv_fwd(glu, dw, dwb, *, tm=256, tc=512, name="conv_fwd", carry=None):
    S, D = glu.shape
    tm, tc = min(tm, S), min(tc, D)
    ni = S // tm

    def body(cur_ref, prev_ref, dw_ref, dwb_ref, o_ref, win):
        i = pl.program_id(1)
        tail = prev_ref[tm - CONV_PAD:tm, :]
        win[0:CONV_PAD, :] = jnp.where(i > 0, tail, jnp.zeros_like(tail))
        win[CONV_PAD:CONV_PAD + tm, :] = cur_ref[...]
        acc = jnp.zeros((tm, tc), F32) + dwb_ref[...]
        for k in range(CONV_WIDTH):
            acc = acc + win[pl.ds(CONV_PAD - CONV_WIDTH + 1 + k, tm), :] * dw_ref[k:k + 1, :]
        o_ref[...] = acc

    return _pcall(
        body, name=name, grid=(D // tc, ni),
        in_specs=[pl.BlockSpec((tm, tc), lambda j, i: (i, j)),
                  pl.BlockSpec((tm, tc), lambda j, i: (jnp.maximum(i - 1, 0), j)),
                  pl.BlockSpec((CONV_PAD, tc), lambda j, i: (0, j)),
                  pl.BlockSpec((1, tc), lambda j, i: (0, j))],
        out_specs=[pl.BlockSpec((tm, tc), lambda j, i: (i, j))],
        out_shape=[jax.ShapeDtypeStruct((S, D), F32)],
        scratch_shapes=[pltpu.VMEM((tm + CONV_PAD, tc), F32)],
        operands=(glu, glu, dw, dwb), vmem=_vmem(8 * tm * tc * 4), carry=carry)[0]


def _conv_bwd(dc, glu, a_pre, g_pre, dw, ba, bg, *, tm=256, tc=512, name="conv_bwd", carry=None):
    S, D = dc.shape
    tm, tc = min(tm, S), min(tc, D)
    ni = S // tm

    def body(dc_ref, dcn_ref, glu_ref, glup_ref, a_ref, g_ref, dw_ref, ba_ref, bg_ref,
             da_ref, dg_ref, ddw_ref, dba_ref, dbg_ref, dwin, gwin):
        i = pl.program_id(1)

        @pl.when(i == 0)
        def _():
            ddw_ref[...] = jnp.zeros_like(ddw_ref)
            dba_ref[...] = jnp.zeros_like(dba_ref)
            dbg_ref[...] = jnp.zeros_like(dbg_ref)

        dcur = dc_ref[...]
        head = dcn_ref[0:CONV_PAD, :]
        dwin[0:tm, :] = dcur
        dwin[tm:tm + CONV_PAD, :] = jnp.where(i < ni - 1, head, jnp.zeros_like(head))
        tail = glup_ref[tm - CONV_PAD:tm, :]
        gwin[0:CONV_PAD, :] = jnp.where(i > 0, tail, jnp.zeros_like(tail))
        gwin[CONV_PAD:CONV_PAD + tm, :] = glu_ref[...]
        dglu = jnp.zeros((tm, tc), F32)
        for k in range(CONV_WIDTH):
            dglu = dglu + dwin[pl.ds(CONV_WIDTH - 1 - k, tm), :] * dw_ref[k:k + 1, :]
            shifted = gwin[pl.ds(CONV_PAD - CONV_WIDTH + 1 + k, tm), :]
            ddw_ref[k:k + 1, :] += jnp.sum(dcur * shifted, axis=0, keepdims=True)
        a = a_ref[...] + ba_ref[...]
        sg = _sigmoid(g_ref[...] + bg_ref[...])
        da = dglu * sg
        dg = dglu * a * sg * (1.0 - sg)
        da_ref[...] = da.astype(BF16)
        dg_ref[...] = dg.astype(BF16)
        dba_ref[...] += jnp.sum(da, axis=0, keepdims=True)
        dbg_ref[...] += jnp.sum(dg, axis=0, keepdims=True)

    tile = lambda f: pl.BlockSpec((tm, tc), f)
    vec = pl.BlockSpec((1, tc), lambda j, i: (0, j))
    return _pcall(
        body, name=name, grid=(D // tc, ni),
        in_specs=[tile(lambda j, i: (i, j)), tile(lambda j, i: (jnp.minimum(i + 1, ni - 1), j)),
                  tile(lambda j, i: (i, j)), tile(lambda j, i: (jnp.maximum(i - 1, 0), j)),
                  tile(lambda j, i: (i, j)), tile(lambda j, i: (i, j)),
                  pl.BlockSpec((CONV_PAD, tc), lambda j, i: (0, j)), vec, vec],
        out_specs=[tile(lambda j, i: (i, j)), tile(lambda j, i: (i, j)),
                   pl.BlockSpec((CONV_PAD, tc), lambda j, i: (0, j)), vec, vec],
        out_shape=[jax.ShapeDtypeStruct((S, D), BF16), jax.ShapeDtypeStruct((S, D), BF16),
                   jax.ShapeDtypeStruct((CONV_PAD, D), F32), jax.ShapeDtypeStruct((1, D), F32),
                   jax.ShapeDtypeStruct((1, D), F32)],
        scratch_shapes=[pltpu.VMEM((tm + CONV_PAD, tc), F32), pltpu.VMEM((tm + CONV_PAD, tc), F32)],
        operands=(dc, dc, glu, glu, a_pre, g_pre, dw, ba, bg), vmem=_vmem(16 * tm * tc * 4), carry=carry)


def _nt(a, b):
    return lax.dot_general(a, b, _DIMS["nt"], preferred_element_type=F32)


def _tn(a, b):
    return lax.dot_general(a, b, _DIMS["tn"], preferred_element_type=F32)


def _window_mask(qi, kj, first_key):
    B = ATTN_BLOCK
    return ((kj < B) & (kj >= qi) & (kj >= first_key)) | ((kj >= B) & (kj - B <= qi))


def _attn_fwd(q_rot, k, v, g, dil):
    S, D = k.shape
    H = D // HEAD_DIM
    L = S // dil
    nb_count = L // ATTN_BLOCK
    B = ATTN_BLOCK

    def body(q_ref, kc_ref, kp_ref, vc_ref, vp_ref, o_ref, lse_ref):
        nb = pl.program_id(1)
        qi = lax.broadcasted_iota(jnp.int32, (B, 2 * B), 0)
        kj = lax.broadcasted_iota(jnp.int32, (B, 2 * B), 1)
        valid = _window_mask(qi, kj, jnp.where(nb > 0, 0, B))
        for h in range(H):
            hs = slice(h * HEAD_DIM, (h + 1) * HEAD_DIM)
            kk = jnp.concatenate([kp_ref[:, hs], kc_ref[:, hs]], axis=0)
            vv = jnp.concatenate([vp_ref[:, hs], vc_ref[:, hs]], axis=0)
            s = jnp.where(valid, _nt(q_ref[:, hs], kk) * ATTN_SCALE, NEG)
            m = jnp.max(s, axis=1, keepdims=True)
            p = jnp.exp(s - m)
            l = jnp.sum(p, axis=1, keepdims=True)
            o = jnp.dot(p.astype(BF16), vv, preferred_element_type=F32) / l
            o_ref[:, hs] = o.astype(o_ref.dtype)
            lse_ref[:, hs] = jnp.broadcast_to(m + jnp.log(l), (B, HEAD_DIM))

    blk = lambda f: pl.BlockSpec((B, D), f)
    cur = lambda r, nb: (nb, r)
    prev = lambda r, nb: (jnp.maximum(nb - 1, 0), r)
    o, lse = pl.pallas_call(
        body, name=f"attn_fwd_g{g}", grid=(dil, nb_count),
        in_specs=[blk(lambda r, nb: (nb, r * N_GROUPS + g)), blk(cur), blk(prev), blk(cur), blk(prev)],
        out_specs=[blk(cur), blk(cur)],
        out_shape=[jax.ShapeDtypeStruct((L, dil * D), BF16), jax.ShapeDtypeStruct((L, dil * D), F32)],
        compiler_params=pltpu.CompilerParams(dimension_semantics=("parallel", "arbitrary"),
                                             vmem_limit_bytes=_vmem(12 * B * D * 4)),
    )(q_rot.reshape(L, dil * N_GROUPS * D), k.reshape(L, dil * D), k.reshape(L, dil * D),
      v.reshape(L, dil * D), v.reshape(L, dil * D))
    return o.reshape(S, D), lse.reshape(S, D)


def _attn_bwd(q_rot, k, v, do, lse, dlt, g, dil, *, name, carry=None):
    S, D = k.shape
    H = D // HEAD_DIM
    L = S // dil
    nb_count = L // ATTN_BLOCK
    B = ATTN_BLOCK

    def body(q_ref, qn_ref, kc_ref, kp_ref, vc_ref, vp_ref, do_ref, don_ref, l_ref, ln_ref, d_ref, dn_ref,
             dq_ref, dk_ref, dv_ref):
        nb = pl.program_id(1)
        qi = lax.broadcasted_iota(jnp.int32, (B, 2 * B), 0)
        kj = lax.broadcasted_iota(jnp.int32, (B, 2 * B), 1)
        valid_q = _window_mask(qi, kj, jnp.where(nb > 0, 0, B))
        qr = lax.broadcasted_iota(jnp.int32, (2 * B, B), 0)
        kc_i = lax.broadcasted_iota(jnp.int32, (2 * B, B), 1)
        q_end = jnp.where(nb < nb_count - 1, 2 * B, B)
        valid_k = ((qr < B) & (kc_i <= qr)) | ((qr >= B) & (qr < q_end) & (kc_i >= qr - B))
        for h in range(H):
            hs = slice(h * HEAD_DIM, (h + 1) * HEAD_DIM)
            q, qn = q_ref[:, hs], qn_ref[:, hs]
            kc, vc = kc_ref[:, hs], vc_ref[:, hs]
            dout, doutn = do_ref[:, hs], don_ref[:, hs]
            lq, lqn = l_ref[:, hs][:, 0:1], ln_ref[:, hs][:, 0:1]
            dq_, dqn_ = d_ref[:, hs][:, 0:1], dn_ref[:, hs][:, 0:1]
            kk = jnp.concatenate([kp_ref[:, hs], kc], axis=0)
            vv = jnp.concatenate([vp_ref[:, hs], vc], axis=0)
            s = jnp.where(valid_q, _nt(q, kk) * ATTN_SCALE, NEG)
            p = jnp.exp(s - lq)
            ds = p * (_nt(dout, vv) - dq_)
            dq_ref[:, hs] = (jnp.dot(ds.astype(BF16), kk, preferred_element_type=F32) * ATTN_SCALE).astype(dq_ref.dtype)
            qq = jnp.concatenate([q, qn], axis=0)
            dd = jnp.concatenate([dout, doutn], axis=0)
            ll = jnp.concatenate([lq, lqn], axis=0)
            dl = jnp.concatenate([dq_, dqn_], axis=0)
            s2 = jnp.where(valid_k, _nt(qq, kc) * ATTN_SCALE, NEG)
            p2 = jnp.exp(s2 - ll)
            dv_ref[:, hs] = _tn(p2.astype(BF16), dd)
            ds2 = p2 * (_nt(dd, vc) - dl)
            dk_ref[:, hs] = _tn(ds2.astype(BF16), qq) * ATTN_SCALE

    blk = lambda f: pl.BlockSpec((B, D), f)
    cur = lambda r, nb: (nb, r)
    prev = lambda r, nb: (jnp.maximum(nb - 1, 0), r)
    nxt = lambda r, nb: (jnp.minimum(nb + 1, nb_count - 1), r)
    qcur = lambda r, nb: (nb, r * N_GROUPS + g)
    qnxt = lambda r, nb: (jnp.minimum(nb + 1, nb_count - 1), r * N_GROUPS + g)
    qv = q_rot.reshape(L, dil * N_GROUPS * D)
    view = lambda t: t.reshape(L, dil * D)
    dq, dk, dv = _pcall(
        body, name=name, grid=(dil, nb_count),
        in_specs=[blk(qcur), blk(qnxt), blk(cur), blk(prev), blk(cur), blk(prev), blk(cur), blk(nxt),
                  blk(cur), blk(nxt), blk(cur), blk(nxt)],
        out_specs=[blk(cur), blk(cur), blk(cur)],
        out_shape=[jax.ShapeDtypeStruct((L, dil * D), BF16), jax.ShapeDtypeStruct((L, dil * D), F32),
                   jax.ShapeDtypeStruct((L, dil * D), F32)],
        operands=(qv, qv, view(k), view(k), view(v), view(v), view(do), view(do), view(lse), view(lse), view(dlt),
                  view(dlt)),
        vmem=_vmem(24 * B * D * 4), carry=carry)
    return dq.reshape(S, D), dk.reshape(S, D), dv.reshape(S, D)


def _mlp_ple_fwd(z1, h1b, p_l, W, vec, l, run):
    D = z1.shape[1]
    g1, b1, g2, b2 = vec

    def act_ep(acc):
        return acc, jnp.square(jnp.maximum(acc, 0.0))

    t, act = run(_mm, h1b, W[f"mlp_up{l}"], mode="nn", outs=[BF16, BF16], epilogue=act_ep, name=f"mlp_up{l}")
    f = run(_mm, act, W[f"mlp_down{l}"], mode="nn", outs=[F32], name=f"mlp_down{l}")

    def z2_fn(z1_t, f_t, g1_, b1_, g2_, b2_):
        z2 = ALPHA * _ln(z1_t, g1_, b1_) + f_t
        return z2, _ln(z2, g2_, b2_)

    z2, h2b = _rows(z2_fn, [z1, f], [g1, b1, g2, b2], [(D, F32), (D, BF16)], [], tm=256, name=f"ln2_fwd{l}")
    pp = run(_mm, p_l, W[f"ple_proj{l}"], mode="nn", outs=[F32], name=f"ple_proj{l}")
    gpre = run(_mm, h2b, W[f"ple_gate{l}"], mode="nn", outs=[F32], name=f"ple_gate{l}")
    return t, act, z2, h2b, pp, gpre


def _mlp_ple_bwd(dy, d_pp, d_gpre, p_l, z1, h1b, t, act, z2, h2b, wts, vec, l, run, produce):
    D = z1.shape[1]
    up, down, pp_w, pg_w = wts
    g1, b1, g2, b2 = vec
    produce(f"ple_proj{l}", run(_mm, p_l, d_pp, mode="tn", outs=[BF16], layout=("col", N_CHIPS), name=f"d_ple_proj{l}"))
    produce(f"ple_gate{l}", run(_mm, h2b, d_gpre, mode="tn", outs=[BF16], layout=("row", N_CHIPS), name=f"d_ple_gate{l}"))
    dh2 = run(_mm, d_gpre, pg_w, mode="nt", outs=[F32], extras=[dy], epilogue=lambda acc, e: (acc + e,),
              name=f"dh2_{l}")

    def ln2_bwd(dh2_t, z2_t, g2_):
        n, rstd = _ln_norm(z2_t)
        dz2 = _ln_bwd(dh2_t, n, rstd, g2_)
        return dz2, dz2, dh2_t * n, dh2_t

    dz2, dz2b, dg2, db2 = _rows(ln2_bwd, [dh2, z2], [g2], [(D, F32), (D, BF16)], [D, D], tm=256,
                                name=f"ln2_bwd{l}")
    produce(f"mlp_down{l}", run(_mm, act, dz2b, mode="tn", outs=[BF16], layout=("row", N_CHIPS), name=f"d_mlp_down{l}"))
    dt = run(_mm, dz2b, down, mode="nt", outs=[BF16], extras=[t],
             epilogue=lambda acc, t_: (acc * 2.0 * jnp.maximum(t_.astype(F32), 0.0),), name=f"dt{l}")
    produce(f"mlp_up{l}", run(_mm, h1b, dt, mode="tn", outs=[BF16], layout=("col", N_CHIPS), name=f"d_mlp_up{l}"))
    dh1 = run(_mm, dt, up, mode="nt", outs=[F32], extras=[dz2], epilogue=lambda acc, e: (acc + ALPHA * e,),
              name=f"dh1_{l}")

    def ln1_bwd(dh1_t, z1_t, g1_):
        n, rstd = _ln_norm(z1_t)
        dz1 = _ln_bwd(dh1_t, n, rstd, g1_)
        return dz1, dz1, dh1_t * n, dh1_t

    dz1, dz1b, dg1, db1 = _rows(ln1_bwd, [dh1, z1], [g1], [(D, F32), (D, BF16)], [D, D], tm=256,
                                name=f"ln1_bwd{l}")
    return dz1, dz1b, (dg1, db1, dg2, db2)


def _ple_out(z2, pp, gpre, g2, b2):
    gt = _sigmoid(gpre)
    return _ln(z2, g2, b2) + pp * gt, gt


_GATHER_AT = {
    "conv_in_a": ("conv_w_out",),
    "conv_in_g": ("ple_gate0", "ple_proj0"),
    "conv_fwd": ("mlp_up0",),
    "mlp_up0": ("mlp_down0",),
    "mlp_down0": ("attn_w_q",),
    "ple_gate0": ("w_kv",),
    "kv_k": ("attn_w_o",),
    "kv_v": ("ple_proj1", "ple_gate1"),
    "attn_q": ("mlp_up1",),
    "mlp_up1": ("mlp_down1",),
}
_GATHER_FIRST = ("conv_w_in",)


def _local_step(x, p, cosf, sinf, target, W, V, shards=None, reducer=None):
    S, D = x.shape
    gw, gv = {}, {}
    if shards is not None:
        W = dict(_Gather(_GATHER_FIRST, shards).run_alone("gather_first"))

    def run(fn, *args, name, **kw):
        gather = _Gather(_GATHER_AT[name], shards) if (shards is not None and name in _GATHER_AT) else None
        carry = gather if reducer is None or gather is not None else reducer.carry(name)
        out = fn(*args, name=name, carry=carry, **kw)
        if gather is not None:
            W.update(gather.result)
        elif reducer is not None:
            reducer.carried()
        return out

    a_pre = run(_mm, x, W["conv_w_in"], b_sel=0, mode="nn", outs=[F32], name="conv_in_a")
    g_pre = run(_mm, x, W["conv_w_in"], b_sel=1, mode="nn", outs=[F32], name="conv_in_g")
    (glu,) = _rows(lambda a, g, ba, bg: ((a + ba) * _sigmoid(g + bg),), [a_pre, g_pre], [V["conv_b_a"], V["conv_b_g"]],
                   [(D, F32)], [], tm=256, name="glu_fwd")
    cv = run(_conv_fwd, glu, V["conv_dw"], V["conv_dw_b"], name="conv_fwd")

    def silu_ln(c, g_, b_):
        y = _ln(c, g_, b_)
        return (y * _sigmoid(y),)

    (sb,) = _rows(silu_ln, [cv], [V["conv_ln_g"], V["conv_ln_b"]], [(D, BF16)], [], tm=256, name="conv_ln_fwd")
    mix0 = _mm(sb, W["conv_w_out"], mode="nn", outs=[F32], name="conv_out")

    def z1_fn(x_t, mix_t, g_, b_):
        z1 = ALPHA * x_t + mix_t
        return z1, _ln(z1, g_, b_)

    vec0 = (V["ln1_g0"], V["ln1_b0"], V["ln2_g0"], V["ln2_b0"])
    vec1 = (V["ln1_g1"], V["ln1_b1"], V["ln2_g1"], V["ln2_b1"])
    z1_0, h1b_0 = _rows(z1_fn, [x, mix0], [vec0[0], vec0[1]], [(D, F32), (D, BF16)], [], tm=256, name="ln1_fwd0")
    t0, act0, z2_0, h2b_0, pp0, gpre0 = _mlp_ple_fwd(z1_0, h1b_0, p[0], W, vec0, 0, run)

    def x1_fn(z2, pp, gpre, g2, b2, kg, kb):
        x1, _ = _ple_out(z2, pp, gpre, g2, b2)
        return x1, _ln(x1, kg, kb)

    x1, kvn = _rows(x1_fn, [z2_0, pp0, gpre0], [vec0[2], vec0[3], V["kv_ln_g"], V["kv_ln_b"]],
                    [(D, F32), (D, BF16)], [], tm=256, name="ple_out0")

    rot_ep = lambda acc, c_, s_: (_rot(acc, c_, s_),)
    k_rot = run(_mm, kvn, W["w_kv"], b_sel=0, mode="nn", outs=[BF16], rextras=[cosf, sinf], epilogue=rot_ep, name="kv_k")
    v_b = run(_mm, kvn, W["w_kv"], b_sel=1, mode="nn", outs=[BF16], name="kv_v")
    q_rot = run(_mm, x1, W["attn_w_q"], mode="nn", outs=[BF16], rextras=[cosf, sinf], epilogue=rot_ep, name="attn_q")
    og, lg = [], []
    for g, dil in enumerate(DILATIONS):
        o_g, l_g = _attn_fwd(q_rot, k_rot, v_b, g, dil)
        og.append(o_g)
        lg.append(l_g)

    def merge(o0, o1, o2, l0, l1, l2):
        m = jnp.maximum(jnp.maximum(l0, l1), l2)
        e0, e1, e2 = jnp.exp(l0 - m), jnp.exp(l1 - m), jnp.exp(l2 - m)
        den = e0 + e1 + e2
        o = (e0 * o0.astype(F32) + e1 * o1.astype(F32) + e2 * o2.astype(F32)) / den
        return o, m + jnp.log(den)

    ob, lse = _rows(merge, og + lg, [], [(D, BF16), (D, F32)], [], tm=256, name="attn_merge")
    mix1 = _mm(ob, W["attn_w_o"], mode="nn", outs=[F32], name="attn_out")
    z1_1, h1b_1 = _rows(z1_fn, [x1, mix1], [vec1[0], vec1[1]], [(D, F32), (D, BF16)], [], tm=256, name="ln1_fwd1")
    t1, act1, z2_1, h2b_1, pp1, gpre1 = _mlp_ple_fwd(z1_1, h1b_1, p[1], W, vec1, 1, run)
    wts0 = (W["mlp_up0"], W["mlp_down0"], W["ple_proj0"], W["ple_gate0"])
    wts1 = (W["mlp_up1"], W["mlp_down1"], W["ple_proj1"], W["ple_gate1"])

    def head(z2, pp, gpre, tgt, g2, b2):
        y, gt = _ple_out(z2, pp, gpre, g2, b2)
        err = y - tgt
        dy = err * (1.0 / D)
        return dy, dy * gt, dy * pp * gt * (1.0 - gt), 0.5 * err * err * (1.0 / D)

    dy1, d_pp1, d_gpre1, loss_cols = _rows(head, [z2_1, pp1, gpre1, target], [vec1[2], vec1[3]],
                                           [(D, F32), (D, BF16), (D, BF16)], [D], tm=256, name="loss_head")

    def produce(name, grad):
        gw[name] = grad
        if reducer is not None:
            reducer.produced(name, grad)

    dz1_1, dz1b_1, (gv["ln1_g1"], gv["ln1_b1"], gv["ln2_g1"], gv["ln2_b1"]) = _mlp_ple_bwd(
        dy1, d_pp1, d_gpre1, p[1], z1_1, h1b_1, t1, act1, z2_1, h2b_1, wts1, vec1, 1, run, produce)
    produce("attn_w_o", run(_mm, ob, dz1b_1, mode="tn", outs=[BF16], layout=("row", N_CHIPS), name="d_attn_w_o"))

    def do_ep(acc, o_t):
        prod = acc * o_t.astype(F32)
        dlt = _per_head(prod, lambda ph, i: jnp.broadcast_to(jnp.sum(ph, axis=1, keepdims=True), ph.shape))
        return acc, dlt

    do_b, dlt = run(_mm, dz1b_1, W["attn_w_o"], mode="nt", outs=[BF16, F32], extras=[ob], epilogue=do_ep, name="attn_do")
    dqs, dks, dvs = [], [], []
    for g, dil in enumerate(DILATIONS):
        dq_g, dk_g, dv_g = run(_attn_bwd, q_rot, k_rot, v_b, do_b, lse, dlt, g, dil, name=f"attn_bwd_g{g}")
        dqs.append(dq_g)
        dks.append(dk_g)
        dvs.append(dv_g)

    def unrot(q0, q1, q2, k0, k1, k2, v0, v1, v2, c_, s_):
        dq = jnp.concatenate([_rot_t(t_.astype(F32), c_, s_) for t_ in (q0, q1, q2)], axis=1)
        return dq, _rot_t(k0 + k1 + k2, c_, s_), v0 + v1 + v2

    dq, dk, dv = run(_rows, unrot, dqs + dks + dvs + [cosf, sinf], [], [(N_GROUPS * D, BF16), (D, BF16), (D, BF16)], [],
                     tm=128, name="attn_unrot")
    produce("attn_w_q", run(_mm, x1, dq, mode="tn", outs=[BF16], layout=("col", N_CHIPS), name="d_attn_w_q"))
    dx1_q = run(_mm, dq, W["attn_w_q"], mode="nt", outs=[F32], extras=[dz1_1],
                epilogue=lambda acc, e: (acc + ALPHA * e,), name="dx1_q")
    produce("w_kv", jnp.concatenate(
        [run(_mm, kvn, dk, mode="tn", outs=[BF16], layout=("col", 2), name="d_w_kv_k"),
         run(_mm, kvn, dv, mode="tn", outs=[BF16], layout=("col", 2), name="d_w_kv_v")], axis=1))
    dkvn_k = run(_mm, dk, W["w_kv"], b_sel=0, mode="nt", outs=[F32], name="dkvn_k")
    dkvn = run(_mm, dv, W["w_kv"], b_sel=1, mode="nt", outs=[F32], extras=[dkvn_k], epilogue=lambda acc, e: (acc + e,),
               name="dkvn_v")

    def x1_bwd(dx1q_t, dkvn_t, x1_t, pp, gpre, kg):
        n, rstd = _ln_norm(x1_t)
        dy = dx1q_t + _ln_bwd(dkvn_t, n, rstd, kg)
        gt = _sigmoid(gpre)
        return dy, dy * gt, dy * pp * gt * (1.0 - gt), dkvn_t * n, dkvn_t

    dy0, d_pp0, d_gpre0, gv["kv_ln_g"], gv["kv_ln_b"] = _rows(
        x1_bwd, [dx1_q, dkvn, x1, pp0, gpre0], [V["kv_ln_g"]], [(D, F32), (D, BF16), (D, BF16)], [D, D], tm=256,
        name="x1_bwd")

    dz1_0, dz1b_0, (gv["ln1_g0"], gv["ln1_b0"], gv["ln2_g0"], gv["ln2_b0"]) = _mlp_ple_bwd(
        dy0, d_pp0, d_gpre0, p[0], z1_0, h1b_0, t0, act0, z2_0, h2b_0, wts0, vec0, 0, run, produce)
    produce("conv_w_out", run(_mm, sb, dz1b_0, mode="tn", outs=[BF16], layout=("row", N_CHIPS), name="d_conv_w_out"))
    ds = run(_mm, dz1b_0, W["conv_w_out"], mode="nt", outs=[F32], name="conv_ds")

    def conv_ln_bwd(ds_t, c_t, g_, b_):
        n, rstd = _ln_norm(c_t)
        y = n * g_ + b_
        sg = _sigmoid(y)
        dln = ds_t * sg * (1.0 + y * (1.0 - sg))
        dc = _ln_bwd(dln, n, rstd, g_)
        return dc, dln * n, dln, dc

    dc, gv["conv_ln_g"], gv["conv_ln_b"], gv["conv_dw_b"] = _rows(
        conv_ln_bwd, [ds, cv], [V["conv_ln_g"], V["conv_ln_b"]], [(D, F32)], [D, D, D], tm=256, name="conv_ln_bwd")
    da, dg, gv["conv_dw"], gv["conv_b_a"], gv["conv_b_g"] = run(
        _conv_bwd, dc, glu, a_pre, g_pre, V["conv_dw"], V["conv_b_a"], V["conv_b_g"], name="conv_bwd")
    produce("conv_w_in", jnp.concatenate(
        [run(_mm, x, da, mode="tn", outs=[BF16], layout=("col", 2), name="d_conv_w_in_a"),
         run(_mm, x, dg, mode="tn", outs=[BF16], layout=("col", 2), name="d_conv_w_in_g")], axis=1))
    dx_a = run(_mm, da, W["conv_w_in"], b_sel=0, mode="nt", outs=[F32], extras=[dz1_0],
               epilogue=lambda acc, e: (acc + ALPHA * e,), name="dx_a")
    grad_x = run(_mm, dg, W["conv_w_in"], b_sel=1, mode="nt", outs=[F32], extras=[dx_a],
                 epilogue=lambda acc, e: (acc + e,), name="dx_g")
    if reducer is not None:
        reducer.carry("share_last").run_alone("share_last")
        reducer.carried()
    return loss_cols, grad_x, gw, gv


def _place():
    x, y, c = lax.axis_index("x"), lax.axis_index("y"), lax.axis_index("c")
    chips = [(1 - x, y), (x, 1 - y), (1 - x, 1 - y)]
    return x, y, c, chips


def _remote(src, dst, ssem, rsem, dev):
    return pltpu.make_async_remote_copy(src_ref=src, dst_ref=dst, send_sem=ssem, recv_sem=rsem, device_id=dev,
                                        device_id_type=MESH)


LOCAL_CHUNKS = 8


def _local_copies(src, dst, sems, t):
    rows = src.shape[0] // LOCAL_CHUNKS
    return [pltpu.make_async_copy(src.at[pl.ds(i * rows, rows), :], dst.at[pl.ds(i * rows, rows), :],
                                  sems.at[t * LOCAL_CHUNKS + i]) for i in range(LOCAL_CHUNKS)]


def _allgather8(block, name):
    R, C = block.shape

    def body(x_ref, out_ref, send_sems, recv_sems, local_sem):
        x, y, c, chips = _place()
        me, sibling = (x, y, c), (x, y, 1 - c)

        def slot(px, py, pc):
            return out_ref.at[4 * px + 2 * py + pc]

        def copy(k, blockpos, to, src=None):
            return _remote(slot(*blockpos) if src is None else src, slot(*blockpos), send_sems.at[k], recv_sems.at[k], to)

        mine = pltpu.make_async_copy(x_ref, slot(*me), local_sem)
        mine.start()
        first = [copy(0, me, sibling, src=x_ref)]
        first += [copy(1 + j, me, (*chip, c), src=x_ref) for j, chip in enumerate(chips)]
        for cp in first:
            cp.start()
        passed = [copy(4 + j, (*chip, c), sibling) for j, chip in enumerate(chips)]
        for j, chip in enumerate(chips):
            copy(1 + j, (*chip, c), me).wait_recv()
            passed[j].start()
        copy(0, sibling, me).wait_recv()
        for j, chip in enumerate(chips):
            copy(4 + j, (*chip, 1 - c), me).wait_recv()
        for cp in first + passed:
            cp.wait_send()
        mine.wait()

    return pl.pallas_call(
        body, name=name, out_shape=jax.ShapeDtypeStruct((8, R, C), block.dtype),
        in_specs=[pl.BlockSpec(memory_space=pltpu.VMEM)], out_specs=pl.BlockSpec(memory_space=pltpu.VMEM),
        scratch_shapes=[pltpu.SemaphoreType.DMA((7,)), pltpu.SemaphoreType.DMA((7,)), pltpu.SemaphoreType.DMA],
        compiler_params=pltpu.CompilerParams(vmem_limit_bytes=_vmem(10 * _nbytes((R, C), block.dtype))),
    )(block)


_MATS = (
    ("conv_w_in", "conv_w_in", 0, "col", True),
    ("conv_w_out", "conv_w_out", 0, "row", False),
    ("mlp_up0", "mlp_up", 0, "col", False),
    ("mlp_down0", "mlp_down", 0, "row", False),
    ("ple_proj0", "ple_proj", 0, "col", False),
    ("ple_gate0", "ple_gate", 0, "row", False),
    ("w_kv", "w_kv", None, "col", True),
    ("attn_w_q", "attn_w_q", 0, "col", False),
    ("attn_w_o", "attn_w_o", 0, "row", False),
    ("mlp_up1", "mlp_up", 1, "col", False),
    ("mlp_down1", "mlp_down", 1, "row", False),
    ("ple_proj1", "ple_proj", 1, "col", False),
    ("ple_gate1", "ple_gate", 1, "row", False),
)


class _Carry:
    result = None

    def set_result(self, outs):
        self.result = dict(zip(self.names, outs))

    def run_alone(self, name):
        n_in, n_out = len(self.ins), len(self.out_shape)

        def body(*refs):
            in_refs, out_refs, sems = refs[:n_in], refs[n_in:n_in + n_out], refs[n_in + n_out:]
            self.start(in_refs, out_refs, sems)
            self.finish(in_refs, out_refs, sems)

        outs = pl.pallas_call(body, name=name, out_shape=self.out_shape, in_specs=[ANY] * n_in, out_specs=[ANY] * n_out,
                              scratch_shapes=self.scratch)(*self.ins)
        self.set_result(outs)
        return self.result


class _Gather(_Carry):
    def __init__(self, names, shards):
        mats = [m for m in _MATS if m[0] in names]
        srcs = sorted({m[1] for m in mats})
        self.names = [m[0] for m in mats]
        self.ins = [shards[n] for n in srcs]
        self.out_shape, self.geo = [], []
        for name, src, layer, kind, split in mats:
            s = shards[src]
            ks, ns = s.shape[-2:]
            K, N = (ks, ns * N_CHIPS) if kind == "col" else (ks * N_CHIPS, ns)
            self.out_shape.append(jax.ShapeDtypeStruct((2, K, N // 2) if split else (K, N), BF16))
            self.geo.append((srcs.index(src), layer if s.ndim == 3 else None, kind, split, K, N))
        T = len(mats)
        self.scratch = [pltpu.SemaphoreType.DMA((3 * T,)) for _ in range(4)] + [pltpu.SemaphoreType.DMA((T * LOCAL_CHUNKS,))]
        self.result = None

    def _copies(self, in_refs, out_refs, sems):
        geo, T = self.geo, len(self.geo)
        s_ici, r_ici, s_d2d, r_d2d, lsem = sems
        x, y, c, chips = _place()
        me = 2 * x + y
        sibling = (x, y, 1 - c)
        idx = [2 * cx + cy for cx, cy in chips]

        def src_ref(t):
            i, layer, _, _, _, _ = geo[t]
            return in_refs[i] if layer is None else in_refs[i].at[layer]

        def src_half(t, h):
            _, _, kind, _, K, N = geo[t]
            if kind == "col":
                return src_ref(t).at[pl.ds(h * (K // 2), K // 2), :]
            return src_ref(t).at[:, pl.ds(h * (N // 2), N // 2)]

        def dst(t, j, h):
            _, _, kind, split, K, N = geo[t]
            n, k = N // N_CHIPS, K // N_CHIPS
            if kind == "col":
                rows = slice(None) if h is None else pl.ds(h * (K // 2), K // 2)
                if split:
                    return out_refs[t].at[j // 2, rows, pl.ds((j % 2) * n, n)]
                return out_refs[t].at[rows, pl.ds(j * n, n)]
            cols = slice(None) if h is None else pl.ds(h * (N // 2), N // 2)
            return out_refs[t].at[pl.ds(j * k, k), cols]

        local = [cp for t in range(T) for cp in _local_copies(src_ref(t), dst(t, me, None), lsem, t)]
        sends = [_remote(src_half(t, c), dst(t, me, c), s_ici.at[3 * t + kk], r_ici.at[3 * t + kk], (*chips[kk], c))
                 for t in range(T) for kk in range(3)]
        hops = []
        for t in range(T):
            for kk in range(3):
                mine, theirs = dst(t, idx[kk], c), dst(t, idx[kk], 1 - c)
                hops.append((_remote(mine, mine, s_ici.at[3 * t + kk], r_ici.at[3 * t + kk], sibling),
                             _remote(mine, mine, s_d2d.at[3 * t + kk], r_d2d.at[3 * t + kk], sibling),
                             _remote(theirs, theirs, s_d2d.at[3 * t + kk], r_d2d.at[3 * t + kk], sibling)))
        return local, sends, hops

    def start(self, in_refs, out_refs, sems):
        local, sends, _ = self._copies(in_refs, out_refs, sems)
        for cp in local + sends:
            cp.start()

    def finish(self, in_refs, out_refs, sems):
        local, sends, hops = self._copies(in_refs, out_refs, sems)
        for landed, forward, _ in hops:
            landed.wait_recv()
            forward.start()
        for _, _, from_sibling in hops:
            from_sibling.wait_recv()
        for cp in sends + [h[1] for h in hops]:
            cp.wait_send()
        for cp in local:
            cp.wait()


class _Multi(_Carry):
    def __init__(self, parts):
        self.parts = parts
        self.ins = [a for p in parts for a in p.ins]
        self.out_shape = [a for p in parts for a in p.out_shape]
        self.scratch = [a for p in parts for a in p.scratch]

    def _split(self, seq, field):
        out, at = [], 0
        for p in self.parts:
            n = len(getattr(p, field))
            out.append(seq[at:at + n])
            at += n
        return out

    def _each(self, method, in_refs, out_refs, sems):
        for p, i, o, s in zip(self.parts, self._split(in_refs, "ins"), self._split(out_refs, "out_shape"),
                              self._split(sems, "scratch")):
            getattr(p, method)(i, o, s)

    def start(self, in_refs, out_refs, sems):
        self._each("start", in_refs, out_refs, sems)

    def finish(self, in_refs, out_refs, sems):
        self._each("finish", in_refs, out_refs, sems)

    def set_result(self, outs):
        for p, o in zip(self.parts, self._split(list(outs), "out_shape")):
            p.set_result(o)


class _PairSend(_Carry):
    def __init__(self, grads):
        self.names = list(grads)
        self.ins = [grads[n] for n in self.names]
        self.out_shape = [jax.ShapeDtypeStruct(a.shape[1:], BF16) for a in self.ins]
        T = len(self.names)
        self.scratch = [pltpu.SemaphoreType.DMA((T,)), pltpu.SemaphoreType.DMA((T,))]

    def _copies(self, in_refs, out_refs, sems):
        x, y, c, _ = _place()
        return [_remote(in_refs[t].at[1 - c], out_refs[t], sems[0].at[t], sems[1].at[t], (x, y, 1 - c))
                for t in range(len(self.names))]

    def start(self, in_refs, out_refs, sems):
        for cp in self._copies(in_refs, out_refs, sems):
            cp.start()

    def finish(self, in_refs, out_refs, sems):
        for cp in self._copies(in_refs, out_refs, sems):
            cp.wait()


class _ChipScatter(_Carry):
    def __init__(self, sums):
        self.names = list(sums)
        self.ins = [sums[n] for n in self.names]
        self.out_shape = [jax.ShapeDtypeStruct(a.shape, BF16) for a in self.ins]
        T = len(self.names)
        self.scratch = [pltpu.SemaphoreType.DMA((3 * T,)), pltpu.SemaphoreType.DMA((3 * T,)),
                        pltpu.SemaphoreType.DMA((T * LOCAL_CHUNKS,))]

    def _copies(self, in_refs, out_refs, sems):
        ssem, rsem, lsem = sems
        x, y, c, chips = _place()
        me = 2 * x + y
        idx = [2 * cx + cy for cx, cy in chips]
        T = len(self.names)
        local = [cp for t in range(T) for cp in _local_copies(in_refs[t].at[me], out_refs[t].at[me], lsem, t)]
        sends = [_remote(in_refs[t].at[idx[kk]], out_refs[t].at[me], ssem.at[3 * t + kk], rsem.at[3 * t + kk],
                         (*chips[kk], c)) for t in range(T) for kk in range(3)]
        lands = [_remote(out_refs[t].at[idx[kk]], out_refs[t].at[idx[kk]], ssem.at[3 * t + kk], rsem.at[3 * t + kk],
                         (*chips[kk], c)) for t in range(T) for kk in range(3)]
        return local, sends, lands

    def start(self, in_refs, out_refs, sems):
        local, sends, _ = self._copies(in_refs, out_refs, sems)
        for cp in local + sends:
            cp.start()

    def finish(self, in_refs, out_refs, sems):
        local, sends, lands = self._copies(in_refs, out_refs, sems)
        for cp in lands:
            cp.wait_recv()
        for cp in sends:
            cp.wait_send()
        for cp in local:
            cp.wait()


class _PairShare(_Carry):
    def __init__(self, halves):
        self.names = list(halves)
        self.ins = [halves[n] for n in self.names]
        self.out_shape = [jax.ShapeDtypeStruct((2,) + a.shape, F32) for a in self.ins]
        T = len(self.names)
        self.scratch = [pltpu.SemaphoreType.DMA((T,)), pltpu.SemaphoreType.DMA((T,)),
                        pltpu.SemaphoreType.DMA((T * LOCAL_CHUNKS,))]

    def _copies(self, in_refs, out_refs, sems):
        ssem, rsem, lsem = sems
        x, y, c, _ = _place()
        sibling = (x, y, 1 - c)
        T = len(self.names)
        local = [cp for t in range(T) for cp in _local_copies(in_refs[t], out_refs[t].at[c], lsem, t)]
        sends = [_remote(in_refs[t], out_refs[t].at[c], ssem.at[t], rsem.at[t], sibling) for t in range(T)]
        lands = [_remote(out_refs[t].at[1 - c], out_refs[t].at[1 - c], ssem.at[t], rsem.at[t], sibling) for t in range(T)]
        return local, sends, lands

    def start(self, in_refs, out_refs, sems):
        local, sends, _ = self._copies(in_refs, out_refs, sems)
        for cp in local + sends:
            cp.start()

    def finish(self, in_refs, out_refs, sems):
        local, sends, lands = self._copies(in_refs, out_refs, sems)
        for cp in lands:
            cp.wait_recv()
        for cp in sends:
            cp.wait_send()
        for cp in local:
            cp.wait()


def _pair_sum(own, landed, c_arr, name):
    _, ns, r, cc = own.shape
    rows = ns * r
    tr = min(512, rows)
    assert rows % tr == 0

    def body(c_ref, a_ref, b_ref, o_ref):
        o_ref[...] = (a_ref[...].astype(F32) + b_ref[...].astype(F32)).astype(o_ref.dtype)

    out = pl.pallas_call(
        body, name=name, out_shape=jax.ShapeDtypeStruct((rows, cc), BF16),
        grid_spec=pltpu.PrefetchScalarGridSpec(
            num_scalar_prefetch=1, grid=(rows // tr,),
            in_specs=[pl.BlockSpec((None, tr, cc), lambda i, c_ref: (c_ref[0], i, 0)),
                      pl.BlockSpec((tr, cc), lambda i, c_ref: (i, 0))],
            out_specs=pl.BlockSpec((tr, cc), lambda i, c_ref: (i, 0))),
        compiler_params=pltpu.CompilerParams(dimension_semantics=("parallel",), vmem_limit_bytes=_vmem(8 * tr * cc * 4)),
    )(c_arr, own.reshape(2, rows, cc), landed.reshape(rows, cc))
    return out.reshape(ns, r, cc)


def _chip_sum(parts, name):
    _, r, cc = parts.shape
    tr = _fit(256, r)

    def body(p_ref, o_ref):
        acc = p_ref[0].astype(F32)
        for j in range(1, N_CHIPS):
            acc = acc + p_ref[j].astype(F32)
        o_ref[...] = acc

    return pl.pallas_call(
        body, name=name, out_shape=jax.ShapeDtypeStruct((r, cc), F32), grid=(r // tr,),
        in_specs=[pl.BlockSpec((N_CHIPS, tr, cc), lambda i: (0, i, 0))], out_specs=pl.BlockSpec((tr, cc), lambda i: (i, 0)),
        compiler_params=pltpu.CompilerParams(dimension_semantics=("parallel",), vmem_limit_bytes=_vmem(12 * tr * cc * 4)),
    )(parts)


def _adamw_math(w, g, m, v):
    m2 = ADAM_B1 * m + (1.0 - ADAM_B1) * g
    v2 = ADAM_B2 * v + (1.0 - ADAM_B2) * jnp.square(g)
    m_hat = m2 / (1.0 - ADAM_B1 ** ADAM_STEP)
    v_hat = v2 / (1.0 - ADAM_B2 ** ADAM_STEP)
    delta = -ADAM_LR * (m_hat / (jnp.sqrt(v_hat) + ADAM_EPS) + ADAM_WD * w)
    return delta, m2, v2


def _adamw_mat(g2, w, m, v, layer, kind, prev, name):
    shape = w.shape
    ks, ns = shape[-2:]
    _, r, cc = g2.shape
    tr, tc = _fit(256, r), _fit(1024, cc)
    assert (r, cc) == ((ks // 2, ns) if kind == "col" else (ks, ns // 2))
    assert r % tr == 0 and cc % tc == 0
    rb, cb = r // tr, cc // tc
    if kind == "col":
        g_spec = pl.BlockSpec((None, tr, tc), lambda i, j: (i // rb, i % rb, j))
    else:
        g_spec = pl.BlockSpec((None, tr, tc), lambda i, j: (j // cb, i, j % cb))
    if w.ndim == 3:
        w_spec = pl.BlockSpec((None, tr, tc), lambda i, j: (layer, i, j))
    else:
        w_spec = pl.BlockSpec((tr, tc), lambda i, j: (i, j))
    n_prev = 0 if prev is None else 4

    def body(*refs):
        g_ref, w_ref, m_ref, v_ref = refs[:4]
        go_ref, d_ref, mo_ref, vo_ref = refs[4 + n_prev:]
        g = g_ref[...]
        delta, m2, v2 = _adamw_math(w_ref[...], g, m_ref[...], v_ref[...])
        go_ref[...] = g
        d_ref[...] = delta
        mo_ref[...] = m2
        vo_ref[...] = v2

    return pl.pallas_call(
        body, name=name, grid=(ks // tr, ns // tc),
        in_specs=[g_spec, w_spec, w_spec, w_spec] + [ANY] * n_prev, out_specs=[w_spec] * 4,
        out_shape=[jax.ShapeDtypeStruct(shape, F32)] * 4,
        input_output_aliases={4 + i: i for i in range(n_prev)},
        compiler_params=pltpu.CompilerParams(dimension_semantics=("parallel", "parallel"),
                                             vmem_limit_bytes=_vmem(16 * tr * tc * 4)),
    )(g2, w, m, v, *(prev or ()))


def _adamw_small(g, w, m, v, name):
    def body(g_ref, w_ref, m_ref, v_ref, d_ref, mo_ref, vo_ref):
        delta, m2, v2 = _adamw_math(w_ref[...], g_ref[...], m_ref[...], v_ref[...])
        d_ref[...] = delta
        mo_ref[...] = m2
        vo_ref[...] = v2

    return pl.pallas_call(body, name=name, out_shape=[jax.ShapeDtypeStruct(w.shape, F32)] * 3)(g, w, m, v)


def _sum8(parts, name):
    def body(p_ref, o_ref):
        acc = p_ref[0]
        for j in range(1, 8):
            acc = acc + p_ref[j]
        o_ref[...] = acc

    return pl.pallas_call(body, name=name, out_shape=jax.ShapeDtypeStruct(parts.shape[1:], F32),
                          compiler_params=pltpu.CompilerParams(vmem_limit_bytes=_vmem(12 * _nbytes(parts.shape[1:], F32))))(parts)


_REDUCE_AT = {
    "d_ple_gate1": (("A", "ple_proj1"),),
    "dh2_1": (("A", "ple_gate1"),),
    "d_mlp_down1": (("B", "ple_proj1"), ("B", "ple_gate1")),
    "dt1": (("A", "mlp_down1"),),
    "d_mlp_up1": (("B", "mlp_down1"), ("C", "ple_proj1"), ("C", "ple_gate1")),
    "dh1_1": (("A", "mlp_up1"),),
    "d_attn_w_o": (("C", "mlp_down1"),),
    "attn_do": (("A", "attn_w_o"),),
    "attn_bwd_g0": (("B", "mlp_up1"),),
    "attn_bwd_g1": (("B", "attn_w_o"),),
    "attn_unrot": (("C", "mlp_up1"), ("C", "attn_w_o")),
    "dx1_q": (("A", "attn_w_q"),),
    "dkvn_k": (("A", "w_kv"),),
    "d_ple_gate0": (("A", "ple_proj0"),),
    "dh2_0": (("A", "ple_gate0"),),
    "d_mlp_down0": (("B", "attn_w_q"), ("B", "ple_proj0")),
    "dt0": (("B", "w_kv"), ("B", "ple_gate0"), ("A", "mlp_down0")),
    "d_mlp_up0": (("B", "mlp_down0"), ("C", "attn_w_q"), ("C", "ple_proj0"), ("C", "w_kv"), ("C", "ple_gate0")),
    "dh1_0": (("A", "mlp_up0"),),
    "d_conv_w_out": (("C", "mlp_down0"),),
    "conv_ds": (("A", "conv_w_out"),),
    "conv_bwd": (("B", "mlp_up0"), ("B", "conv_w_out")),
    "d_conv_w_in_g": (("C", "mlp_up0"), ("C", "conv_w_out")),
    "dx_a": (("A", "conv_w_in"),),
    "dx_g": (("B", "conv_w_in"),),
    "share_last": (("C", "conv_w_in"),),
}


class _Reducer:
    def __init__(self, w, mom, var, c_arr):
        self.w, self.mom, self.var, self.c_arr = w, mom, var, c_arr
        self.mats = {m[0]: m for m in _MATS}
        self.grads, self.pair_sums, self.chip_sums, self.out = {}, {}, {}, {}

    def produced(self, name, grad):
        self.grads[name] = grad

    def carry(self, call):
        parts = []
        for cls, stage, src in ((_PairSend, "A", self.grads), (_ChipScatter, "B", self.pair_sums),
                                (_PairShare, "C", self.chip_sums)):
            names = [n for s, n in _REDUCE_AT.get(call, ()) if s == stage]
            if names:
                parts.append((stage, cls({n: src[n] for n in names})))
        self._parts = parts
        return _Multi([p for _, p in parts]) if parts else None

    def carried(self):
        for stage, part in self._parts:
            for name, val in part.result.items():
                if stage == "A":
                    self.pair_sums[name] = _pair_sum(self.grads[name], val, self.c_arr, f"pair_sum_{name}")
                elif stage == "B":
                    self.chip_sums[name] = _chip_sum(val, f"chip_sum_{name}")
                else:
                    _, src, layer, kind, _ = self.mats[name]
                    self.out[src] = _adamw_mat(val, self.w[src], self.mom[src], self.var[src], layer or 0, kind,
                                               self.out.get(src), f"adamw_{name}")
        self._parts = []


_WEIGHTS = ("conv_w_in", "conv_b_in", "conv_dw", "conv_dw_b", "conv_ln_g", "conv_ln_b", "conv_w_out", "kv_ln_g",
            "kv_ln_b", "w_kv", "attn_w_q", "attn_w_o", "ln1_g", "ln1_b", "mlp_up", "mlp_down", "ln2_g", "ln2_b",
            "ple_proj", "ple_gate")
_SHARDED_VECS = ("conv_b_in", "conv_dw", "conv_dw_b", "conv_ln_g", "conv_ln_b")
_REPLICATED_VECS = ("kv_ln_g", "kv_ln_b", "ln1_g", "ln1_b", "ln2_g", "ln2_b")


def _pad_rows(a, rows):
    return jnp.concatenate([a, jnp.zeros((rows - a.shape[0], a.shape[1]), a.dtype)], axis=0) if a.shape[0] < rows else a


def _pack_sharded(d):
    n = d["conv_dw_b"].shape[-1]
    rows = [d["conv_b_in"].reshape(2, n), d["conv_dw"].reshape(CONV_WIDTH, n), d["conv_dw_b"].reshape(1, n),
            d["conv_ln_g"].reshape(1, n), d["conv_ln_b"].reshape(1, n)]
    return _pad_rows(jnp.concatenate(rows, axis=0), 40)


def _unpack_sharded(pack, like):
    n = pack.shape[1]
    return {"conv_b_in": pack[0:2].reshape(like["conv_b_in"].shape),
            "conv_dw": pack[2:2 + CONV_WIDTH].reshape(like["conv_dw"].shape),
            "conv_dw_b": pack[33:34].reshape(like["conv_dw_b"].shape),
            "conv_ln_g": pack[34:35].reshape(like["conv_ln_g"].shape),
            "conv_ln_b": pack[35:36].reshape(like["conv_ln_b"].shape)}


def _pack_replicated(d):
    D = d["kv_ln_g"].shape[-1]
    rows = [d[n].reshape(-1, D) for n in _REPLICATED_VECS]
    return _pad_rows(jnp.concatenate(rows, axis=0), 16)


def _unpack_replicated(pack, like):
    out, r = {}, 0
    for n in _REPLICATED_VECS:
        k = like[n].size // pack.shape[1]
        out[n] = pack[r:r + k].reshape(like[n].shape)
        r += k
    return out


def kernel(x, p, positions, conv_w_in, conv_b_in, conv_dw, conv_dw_b, conv_ln_g, conv_ln_b, conv_w_out, kv_ln_g, kv_ln_b, w_kv, attn_w_q, attn_w_o, ln1_g, ln1_b, mlp_up, mlp_down, ln2_g, ln2_b, ple_proj, ple_gate, loss_target, m_conv_w_in, m_conv_b_in, m_conv_dw, m_conv_dw_b, m_conv_ln_g, m_conv_ln_b, m_conv_w_out, m_kv_ln_g, m_kv_ln_b, m_w_kv, m_attn_w_q, m_attn_w_o, m_ln1_g, m_ln1_b, m_mlp_up, m_mlp_down, m_ln2_g, m_ln2_b, m_ple_proj, m_ple_gate, v_conv_w_in, v_conv_b_in, v_conv_dw, v_conv_dw_b, v_conv_ln_g, v_conv_ln_b, v_conv_w_out, v_kv_ln_g, v_kv_ln_b, v_w_kv, v_attn_w_q, v_attn_w_o, v_ln1_g, v_ln1_b, v_mlp_up, v_mlp_down, v_ln2_g, v_ln2_b, v_ple_proj, v_ple_gate):
    args = dict(locals())
    w = {n: args[n] for n in _WEIGHTS}
    mom = {n: args["m_" + n] for n in _WEIGHTS}
    var = {n: args["v_" + n] for n in _WEIGHTS}
    S, D = x.shape[1:]
    n4 = D // N_CHIPS
    chip = 2 * lax.axis_index("x") + lax.axis_index("y")
    c_arr = lax.axis_index("c").astype(jnp.int32).reshape(1)

    shards = {n: w[n].astype(BF16) for n in sorted({m[1] for m in _MATS})}
    vec_all = _allgather8(_pack_sharded(w), "gather_vectors")
    vec_full = jnp.concatenate([vec_all[2 * j] for j in range(N_CHIPS)], axis=1)
    b_in = vec_all[0::2, 0:2, :].reshape(1, 2 * D)
    V = {"conv_b_a": b_in[:, :D], "conv_b_g": b_in[:, D:],
         "conv_dw": _pad_rows(vec_full[2:2 + CONV_WIDTH], CONV_PAD), "conv_dw_b": vec_full[33:34],
         "conv_ln_g": vec_full[34:35], "conv_ln_b": vec_full[35:36],
         "kv_ln_g": kv_ln_g.reshape(1, D), "kv_ln_b": kv_ln_b.reshape(1, D)}
    for l in range(2):
        for n in ("ln1_g", "ln1_b", "ln2_g", "ln2_b"):
            V[f"{n}{l}"] = w[n][l].reshape(1, D)

    half = HEAD_DIM // 2
    inv_freq = ROPE_THETA ** (-jnp.arange(half, dtype=F32) * (2.0 / HEAD_DIM))
    ang = positions[0].astype(F32)[:, None] * inv_freq
    cos, sin = jnp.cos(ang), jnp.sin(ang)
    cosf = jnp.concatenate([cos, cos], axis=-1)
    sinf = jnp.concatenate([-sin, sin], axis=-1)

    reducer = _Reducer(w, mom, var, c_arr)
    loss_cols, grad_x, _, gv = _local_step(x[0], p[:, 0], cosf, sinf, loss_target[0], None, V, shards, reducer)
    loss = lax.psum(jnp.sum(loss_cols), ("x", "y", "c"))
    out = dict(reducer.out)

    gpack = jnp.concatenate([gv["conv_b_a"], gv["conv_b_g"], gv["conv_dw"][:CONV_WIDTH], gv["conv_dw_b"],
                             gv["conv_ln_g"], gv["conv_ln_b"], gv["kv_ln_g"], gv["kv_ln_b"],
                             gv["ln1_g0"], gv["ln1_g1"], gv["ln1_b0"], gv["ln1_b1"],
                             gv["ln2_g0"], gv["ln2_g1"], gv["ln2_b0"], gv["ln2_b1"]], axis=0)
    gsum = _sum8(_allgather8(_pad_rows(gpack, 48), "gather_vector_grads"), "sum_vector_grads")
    g_b = lax.dynamic_slice_in_dim(jnp.concatenate([gsum[0:1], gsum[1:2]], axis=1), chip * 2 * n4, 2 * n4, axis=1)
    g_sh = lax.dynamic_slice_in_dim(gsum[2:36], chip * n4, n4, axis=1)
    g_sh = _pad_rows(jnp.concatenate([g_b.reshape(2, n4), g_sh], axis=0), 40)
    d_sh, m_sh, v_sh = _adamw_small(g_sh, _pack_sharded(w), _pack_sharded(mom), _pack_sharded(var), "adamw_sharded_vectors")
    g_rep = _pad_rows(gsum[36:46], 16)
    d_rep, m_rep, v_rep = _adamw_small(g_rep, _pack_replicated(w), _pack_replicated(mom), _pack_replicated(var),
                                       "adamw_replicated_vectors")
    small = {}
    for i, (sh, rep) in enumerate(((g_sh, g_rep), (d_sh, d_rep), (m_sh, m_rep), (v_sh, v_rep))):
        d = {**_unpack_sharded(sh, w), **_unpack_replicated(rep, w)}
        for n, val in d.items():
            small.setdefault(n, [None] * 4)[i] = val
    for n in small:
        out[n] = small[n]

    res = [loss, grad_x[None]]
    for i in range(4):
        res += [out[n][i] for n in _WEIGHTS]
    return tuple(res)
```

```python
import functools

import jax
import jax.numpy as jnp
from jax import lax
from jax.experimental import pallas as pl
from jax.experimental.pallas import tpu as pltpu

F32 = jnp.float32
BF16 = jnp.bfloat16

HEAD_DIM = 128
ATTN_BLOCK = 128
DILATIONS = (1, 4, 16)
N_GROUPS = 3
CONV_WIDTH = 31
CONV_PAD = 32
ROPE_THETA = 10000.0
LN_EPS = 1e-5
ALPHA = 4.0 ** 0.25
ATTN_SCALE = HEAD_DIM ** -0.5
NEG = -1e30

ADAM_LR = 0.001
ADAM_B1 = 0.9
ADAM_B2 = 0.999
ADAM_EPS = 1e-08
ADAM_WD = 0.01
ADAM_STEP = 10

N_CHIPS = 4
VMEM_CAP = 60 << 20
MESH = pl.DeviceIdType.MESH
ANY = pl.BlockSpec(memory_space=pl.ANY)


def _vmem(nbytes):
    return int(min(max(2 * nbytes + (8 << 20), 24 << 20), VMEM_CAP))


def _fit(tile, n):
    if n <= tile:
        return n
    t = tile - tile % 128
    while n % t:
        t -= 128
    return t


def _nbytes(shape, dtype):
    n = 1
    for s in shape:
        n *= s
    return n * jnp.dtype(dtype).itemsize


_DIMS = {"nn": (((1,), (0,)), ((), ())), "nt": (((1,), (1,)), ((), ())), "tn": (((0,), (0,)), ((), ()))}


def _pcall(body, *, name, grid, in_specs, out_specs, out_shape, operands, scratch_shapes=(), vmem, carry=None):
    if carry is None:
        return pl.pallas_call(
            body, name=name, grid=grid, in_specs=in_specs, out_specs=out_specs, out_shape=out_shape,
            scratch_shapes=list(scratch_shapes),
            compiler_params=pltpu.CompilerParams(dimension_semantics=("arbitrary",) * len(grid), vmem_limit_bytes=vmem),
        )(*operands)
    n_in, n_out, n_scr = len(in_specs), len(out_specs), len(scratch_shapes)
    c_in, c_out = len(carry.ins), len(carry.out_shape)

    def wrapped(*refs):
        ins, refs = refs[:n_in], refs[n_in:]
        c_ins, refs = refs[:c_in], refs[c_in:]
        outs, refs = refs[:n_out], refs[n_out:]
        c_outs, refs = refs[:c_out], refs[c_out:]
        scr, c_sems = refs[:n_scr], refs[n_scr:]
        first = functools.reduce(jnp.logical_and, [pl.program_id(d) == 0 for d in range(len(grid))])
        last = functools.reduce(jnp.logical_and, [pl.program_id(d) == grid[d] - 1 for d in range(len(grid))])
        pl.when(first)(lambda: carry.start(c_ins, c_outs, c_sems))
        body(*ins, *outs, *scr)
        pl.when(last)(lambda: carry.finish(c_ins, c_outs, c_sems))

    res = pl.pallas_call(
        wrapped, name=name, grid=grid, in_specs=list(in_specs) + [ANY] * c_in, out_specs=list(out_specs) + [ANY] * c_out,
        out_shape=list(out_shape) + list(carry.out_shape), scratch_shapes=list(scratch_shapes) + list(carry.scratch),
        input_output_aliases={len(operands) + i: n_out + o for i, o in carry.aliases.items()},
        compiler_params=pltpu.CompilerParams(dimension_semantics=("arbitrary",) * len(grid), vmem_limit_bytes=vmem),
    )(*operands, *carry.ins)
    carry.set_result(res[n_out:])
    return res[:n_out]


def _mm(a, b, *, mode, outs, name, epilogue=None, extras=(), rextras=(), vecs=(), a_sel=None, b_sel=None,
        tm=None, tn=2048, tk=None, layout=None, carry=None):
    a2, b2 = a.shape[-2:], b.shape[-2:]
    if mode == "nn":
        (M, K), (K2, N) = a2, b2
    elif mode == "nt":
        (M, K), (N, K2) = a2, b2
    else:
        (K, M), (K2, N) = a2, b2
    assert K == K2, (a.shape, b.shape, mode)
    if tm is None:
        tm = 1024 if mode == "tn" else 512
    if tk is None:
        tk = 1024 if mode == "tn" else 2048
    if layout is not None:
        kind, nslots = layout
        r, c = (M // 2, N // nslots) if kind == "col" else (M // nslots, N // 2)
        tm, tn = _fit(tm, r), _fit(tn, c)
        assert r % tm == 0 and c % tn == 0
    else:
        tm, tn = _fit(tm, M), _fit(tn, N)
    tk = _fit(tk, K)
    assert M % tm == 0 and N % tn == 0 and K % tk == 0, (M, N, K, tm, tn, tk)
    nk = K // tk
    grid = (N // tn, M // tm, nk)

    def spec(arr, sel, blk, imap):
        if arr.ndim == 3:
            return pl.BlockSpec((None,) + blk, lambda j, i, k: (sel,) + imap(j, i, k))
        return pl.BlockSpec(blk, imap)

    if mode == "tn":
        a_spec = spec(a, a_sel, (tk, tm), lambda j, i, k: (k, i))
    else:
        a_spec = spec(a, a_sel, (tm, tk), lambda j, i, k: (i, k))
    if mode == "nt":
        b_spec = spec(b, b_sel, (tn, tk), lambda j, i, k: (j, k))
    else:
        b_spec = spec(b, b_sel, (tk, tn), lambda j, i, k: (k, j))
    in_specs = [a_spec, b_spec]
    in_specs += [pl.BlockSpec((tm, tn), lambda j, i, k: (i, j)) for _ in extras]
    in_specs += [pl.BlockSpec((tm, e.shape[1]), lambda j, i, k: (i, 0)) for e in rextras]
    in_specs += [pl.BlockSpec((1, tn), lambda j, i, k: (0, j)) for _ in vecs]

    if layout is None:
        out_shape = [jax.ShapeDtypeStruct((M, N), d) for d in outs]
        out_specs = [pl.BlockSpec((tm, tn), lambda j, i, k: (i, j)) for _ in outs]
    else:
        assert len(outs) == 1
        out_shape = [jax.ShapeDtypeStruct((2, nslots, r, c), outs[0])]
        rb, cb = r // tm, c // tn
        if kind == "col":
            omap = lambda j, i, k: (i // rb, j // cb, i % rb, j % cb)
        else:
            omap = lambda j, i, k: (j // cb, i // rb, i % rb, j % cb)
        out_specs = [pl.BlockSpec((None, None, tm, tn), omap)]

    ne, nr, nv, no = len(extras), len(rextras), len(vecs), len(outs)
    dims = _DIMS[mode]

    def body(*refs):
        a_ref, b_ref = refs[0], refs[1]
        rest = refs[2:2 + ne + nr + nv]
        o_refs = refs[2 + ne + nr + nv:2 + ne + nr + nv + no]

        def finish(total):
            res = (total,) * no if epilogue is None else epilogue(total, *[x[...] for x in rest])
            for o, val in zip(o_refs, res):
                o[...] = val.astype(o.dtype)

        part = lax.dot_general(a_ref[...].astype(BF16), b_ref[...].astype(BF16), dims, preferred_element_type=F32)
        if nk == 1:
            finish(part)
            return
        acc = refs[-1]
        k = pl.program_id(2)

        @pl.when(k == 0)
        def _():
            acc[...] = part

        @pl.when((k > 0) & (k < nk - 1))
        def _():
            acc[...] += part

        @pl.when(k == nk - 1)
        def _():
            finish(acc[...] + part)

    blk = (_nbytes((tm, tk), a.dtype) + _nbytes((tk, tn), b.dtype) + sum(_nbytes((tm, tn), e.dtype) for e in extras)
           + sum(_nbytes((tm, tn), d) for d in outs) + 2 * tm * tn * 4)
    res = _pcall(body, name=name, grid=grid, in_specs=in_specs, out_specs=out_specs, out_shape=out_shape,
                 operands=(a, b, *extras, *rextras, *vecs),
                 scratch_shapes=[pltpu.VMEM((tm, tn), F32)] if nk > 1 else [], vmem=_vmem(blk), carry=carry)
    return res[0] if no == 1 else tuple(res)


def _rows(fn, rows, vecs, outs, sums, *, tm, name, carry=None):
    S = rows[0].shape[0]
    tm = min(tm, S)
    assert S % tm == 0
    nr, nv, no, ns = len(rows), len(vecs), len(outs), len(sums)

    def body(*refs):
        vals = fn(*[r[...] for r in refs[:nr + nv]])
        o_refs = refs[nr + nv:nr + nv + no]
        s_refs = refs[nr + nv + no:]
        for o, val in zip(o_refs, vals[:no]):
            o[...] = val.astype(o.dtype)
        if ns:
            @pl.when(pl.program_id(0) == 0)
            def _():
                for s in s_refs:
                    s[...] = jnp.zeros_like(s)

            for s, val in zip(s_refs, vals[no:]):
                s[...] += jnp.sum(val.astype(F32), axis=0, keepdims=True)

    in_specs = [pl.BlockSpec((tm, r.shape[1]), lambda i: (i, 0)) for r in rows]
    in_specs += [pl.BlockSpec(v.shape, lambda i: (0, 0)) for v in vecs]
    out_specs = [pl.BlockSpec((tm, c), lambda i: (i, 0)) for c, _ in outs]
    out_specs += [pl.BlockSpec((1, c), lambda i: (0, 0)) for c in sums]
    out_shape = [jax.ShapeDtypeStruct((S, c), d) for c, d in outs]
    out_shape += [jax.ShapeDtypeStruct((1, c), F32) for c in sums]
    blk = sum(_nbytes((tm, r.shape[1]), r.dtype) for r in rows) + sum(_nbytes((tm, c), d) for c, d in outs)
    blk += 6 * tm * max(r.shape[1] for r in rows) * 4
    res = _pcall(body, name=name, grid=(S // tm,), in_specs=in_specs, out_specs=out_specs, out_shape=out_shape,
                 operands=(*rows, *vecs), vmem=_vmem(blk), carry=carry)
    return tuple(res)


def _ln_norm(z):
    mu = jnp.mean(z, axis=-1, keepdims=True)
    d = z - mu
    var = jnp.mean(d * d, axis=-1, keepdims=True)
    rstd = lax.rsqrt(var + LN_EPS)
    return d * rstd, rstd


def _ln(z, g, b):
    return _ln_norm(z)[0] * g + b


def _ln_bwd(dy, n, rstd, g):
    dn = dy * g
    return rstd * (dn - jnp.mean(dn, axis=-1, keepdims=True) - n * jnp.mean(dn * n, axis=-1, keepdims=True))


def _sigmoid(x):
    return 1.0 / (1.0 + jnp.exp(-x))


def _per_head(x, fn):
    h = x.shape[1] // HEAD_DIM
    return jnp.concatenate([fn(x[:, i * HEAD_DIM:(i + 1) * HEAD_DIM], i) for i in range(h)], axis=1)


def _rot(x, cosf, sinf):
    return _per_head(x, lambda xh, i: xh * cosf + pltpu.roll(xh, HEAD_DIM // 2, 1) * sinf)


def _rot_t(dy, cosf, sinf):
    return _per_head(dy, lambda dh, i: dh * cosf + pltpu.roll(dh * sinf, HEAD_DIM // 2, 1))


def _conv_fwd(glu, dw, dwb, *, tm=256, tc=512, name="conv_fwd", carry=None):
    S, D = glu.shape
    tm, tc = min(tm, S), min(tc, D)
    ni = S // tm

    def body(cur_ref, prev_ref, dw_ref, dwb_ref, o_ref, win):
        i = pl.program_id(1)
        tail = prev_ref[tm - CONV_PAD:tm, :]
        win[0:CONV_PAD, :] = jnp.where(i > 0, tail, jnp.zeros_like(tail))
        win[CONV_PAD:CONV_PAD + tm, :] = cur_ref[...]
        acc = jnp.zeros((tm, tc), F32) + dwb_ref[...]
        for k in range(CONV_WIDTH):
            acc = acc + win[pl.ds(CONV_PAD - CONV_WIDTH + 1 + k, tm), :] * dw_ref[k:k + 1, :]
        o_ref[...] = acc

    return _pcall(
        body, name=name, grid=(D // tc, ni),
        in_specs=[pl.BlockSpec((tm, tc), lambda j, i: (i, j)),
                  pl.BlockSpec((tm, tc), lambda j, i: (jnp.maximum(i - 1, 0), j)),
                  pl.BlockSpec((CONV_PAD, tc), lambda j, i: (0, j)),
                  pl.BlockSpec((1, tc), lambda j, i: (0, j))],
        out_specs=[pl.BlockSpec((tm, tc), lambda j, i: (i, j))],
        out_shape=[jax.ShapeDtypeStruct((S, D), F32)],
        scratch_shapes=[pltpu.VMEM((tm + CONV_PAD, tc), F32)],
        operands=(glu, glu, dw, dwb), vmem=_vmem(8 * tm * tc * 4), carry=carry)[0]


def _conv_bwd(dc, glu, a_pre, g_pre, dw, ba, bg, *, tm=256, tc=512, name="conv_bwd", carry=None):
    S, D = dc.shape
    tm, tc = min(tm, S), min(tc, D)
    ni = S // tm

    def body(dc_ref, dcn_ref, glu_ref, glup_ref, a_ref, g_ref, dw_ref, ba_ref, bg_ref,
             da_ref, dg_ref, ddw_ref, dba_ref, dbg_ref, dwin, gwin):
        i = pl.program_id(1)

        @pl.when(i == 0)
        def _():
            ddw_ref[...] = jnp.zeros_like(ddw_ref)
            dba_ref[...] = jnp.zeros_like(dba_ref)
            dbg_ref[...] = jnp.zeros_like(dbg_ref)

        dcur = dc_ref[...]
        head = dcn_ref[0:CONV_PAD, :]
        dwin[0:tm, :] = dcur
        dwin[tm:tm + CONV_PAD, :] = jnp.where(i < ni - 1, head, jnp.zeros_like(head))
        tail = glup_ref[tm - CONV_PAD:tm, :]
        gwin[0:CONV_PAD, :] = jnp.where(i > 0, tail, jnp.zeros_like(tail))
        gwin[CONV_PAD:CONV_PAD + tm, :] = glu_ref[...]
        dglu = jnp.zeros((tm, tc), F32)
        for k in range(CONV_WIDTH):
            dglu = dglu + dwin[pl.ds(CONV_WIDTH - 1 - k, tm), :] * dw_ref[k:k + 1, :]
            shifted = gwin[pl.ds(CONV_PAD - CONV_WIDTH + 1 + k, tm), :]
            ddw_ref[k:k + 1, :] += jnp.sum(dcur * shifted, axis=0, keepdims=True)
        a = a_ref[...] + ba_ref[...]
        sg = _sigmoid(g_ref[...] + bg_ref[...])
        da = dglu * sg
        dg = dglu * a * sg * (1.0 - sg)
        da_ref[...] = da.astype(BF16)
        dg_ref[...] = dg.astype(BF16)
        dba_ref[...] += jnp.sum(da, axis=0, keepdims=True)
        dbg_ref[...] += jnp.sum(dg, axis=0, keepdims=True)

    tile = lambda f: pl.BlockSpec((tm, tc), f)
    vec = pl.BlockSpec((1, tc), lambda j, i: (0, j))
    return _pcall(
        body, name=name, grid=(D // tc, ni),
        in_specs=[tile(lambda j, i: (i, j)), tile(lambda j, i: (jnp.minimum(i + 1, ni - 1), j)),
                  tile(lambda j, i: (i, j)), tile(lambda j, i: (jnp.maximum(i - 1, 0), j)),
                  tile(lambda j, i: (i, j)), tile(lambda j, i: (i, j)),
                  pl.BlockSpec((CONV_PAD, tc), lambda j, i: (0, j)), vec, vec],
        out_specs=[tile(lambda j, i: (i, j)), tile(lambda j, i: (i, j)),
                   pl.BlockSpec((CONV_PAD, tc), lambda j, i: (0, j)), vec, vec],
        out_shape=[jax.ShapeDtypeStruct((S, D), BF16), jax.ShapeDtypeStruct((S, D), BF16),
                   jax.ShapeDtypeStruct((CONV_PAD, D), F32), jax.ShapeDtypeStruct((1, D), F32),
                   jax.ShapeDtypeStruct((1, D), F32)],
        scratch_shapes=[pltpu.VMEM((tm + CONV_PAD, tc), F32), pltpu.VMEM((tm + CONV_PAD, tc), F32)],
        operands=(dc, dc, glu, glu, a_pre, g_pre, dw, ba, bg), vmem=_vmem(16 * tm * tc * 4), carry=carry)


def _nt(a, b):
    return lax.dot_general(a, b, _DIMS["nt"], preferred_element_type=F32)


def _tn(a, b):
    return lax.dot_general(a, b, _DIMS["tn"], preferred_element_type=F32)


def _window_mask(qi, kj, first_key):
    B = ATTN_BLOCK
    return ((kj < B) & (kj >= qi) & (kj >= first_key)) | ((kj >= B) & (kj - B <= qi))


def _attn_fwd(q_rot, k, v, g, dil):
    S, D = k.shape
    H = D // HEAD_DIM
    L = S // dil
    nb_count = L // ATTN_BLOCK
    B = ATTN_BLOCK

    def body(q_ref, kc_ref, kp_ref, vc_ref, vp_ref, o_ref, lse_ref):
        nb = pl.program_id(1)
        qi = lax.broadcasted_iota(jnp.int32, (B, 2 * B), 0)
        kj = lax.broadcasted_iota(jnp.int32, (B, 2 * B), 1)
        valid = _window_mask(qi, kj, jnp.where(nb > 0, 0, B))
        for h in range(H):
            hs = slice(h * HEAD_DIM, (h + 1) * HEAD_DIM)
            kk = jnp.concatenate([kp_ref[:, hs], kc_ref[:, hs]], axis=0)
            vv = jnp.concatenate([vp_ref[:, hs], vc_ref[:, hs]], axis=0)
            s = jnp.where(valid, _nt(q_ref[:, hs], kk) * ATTN_SCALE, NEG)
            m = jnp.max(s, axis=1, keepdims=True)
            p = jnp.exp(s - m)
            l = jnp.sum(p, axis=1, keepdims=True)
            o = jnp.dot(p.astype(BF16), vv, preferred_element_type=F32) / l
            o_ref[:, hs] = o.astype(o_ref.dtype)
            lse_ref[:, hs] = jnp.broadcast_to(m + jnp.log(l), (B, HEAD_DIM))

    blk = lambda f: pl.BlockSpec((B, D), f)
    cur = lambda r, nb: (nb, r)
    prev = lambda r, nb: (jnp.maximum(nb - 1, 0), r)
    o, lse = pl.pallas_call(
        body, name=f"attn_fwd_g{g}", grid=(dil, nb_count),
        in_specs=[blk(lambda r, nb: (nb, r * N_GROUPS + g)), blk(cur), blk(prev), blk(cur), blk(prev)],
        out_specs=[blk(cur), blk(cur)],
        out_shape=[jax.ShapeDtypeStruct((L, dil * D), BF16), jax.ShapeDtypeStruct((L, dil * D), F32)],
        compiler_params=pltpu.CompilerParams(dimension_semantics=("parallel", "arbitrary"),
                                             vmem_limit_bytes=_vmem(12 * B * D * 4)),
    )(q_rot.reshape(L, dil * N_GROUPS * D), k.reshape(L, dil * D), k.reshape(L, dil * D),
      v.reshape(L, dil * D), v.reshape(L, dil * D))
    return o.reshape(S, D), lse.reshape(S, D)


def _attn_bwd(q_rot, k, v, do, lse, dlt, g, dil, *, name, carry=None):
    S, D = k.shape
    H = D // HEAD_DIM
    L = S // dil
    nb_count = L // ATTN_BLOCK
    B = ATTN_BLOCK

    def body(q_ref, qn_ref, kc_ref, kp_ref, vc_ref, vp_ref, do_ref, don_ref, l_ref, ln_ref, d_ref, dn_ref,
             dq_ref, dk_ref, dv_ref):
        nb = pl.program_id(1)
        qi = lax.broadcasted_iota(jnp.int32, (B, 2 * B), 0)
        kj = lax.broadcasted_iota(jnp.int32, (B, 2 * B), 1)
        valid_q = _window_mask(qi, kj, jnp.where(nb > 0, 0, B))
        qr = lax.broadcasted_iota(jnp.int32, (2 * B, B), 0)
        kc_i = lax.broadcasted_iota(jnp.int32, (2 * B, B), 1)
        q_end = jnp.where(nb < nb_count - 1, 2 * B, B)
        valid_k = ((qr < B) & (kc_i <= qr)) | ((qr >= B) & (qr < q_end) & (kc_i >= qr - B))
        for h in range(H):
            hs = slice(h * HEAD_DIM, (h + 1) * HEAD_DIM)
            q, qn = q_ref[:, hs], qn_ref[:, hs]
            kc, vc = kc_ref[:, hs], vc_ref[:, hs]
            dout, doutn = do_ref[:, hs], don_ref[:, hs]
            lq, lqn = l_ref[:, hs][:, 0:1], ln_ref[:, hs][:, 0:1]
            dq_, dqn_ = d_ref[:, hs][:, 0:1], dn_ref[:, hs][:, 0:1]
            kk = jnp.concatenate([kp_ref[:, hs], kc], axis=0)
            vv = jnp.concatenate([vp_ref[:, hs], vc], axis=0)
            s = jnp.where(valid_q, _nt(q, kk) * ATTN_SCALE, NEG)
            p = jnp.exp(s - lq)
            ds = p * (_nt(dout, vv) - dq_)
            dq_ref[:, hs] = (jnp.dot(ds.astype(BF16), kk, preferred_element_type=F32) * ATTN_SCALE).astype(dq_ref.dtype)
            qq = jnp.concatenate([q, qn], axis=0)
            dd = jnp.concatenate([dout, doutn], axis=0)
            ll = jnp.concatenate([lq, lqn], axis=0)
            dl = jnp.concatenate([dq_, dqn_], axis=0)
            s2 = jnp.where(valid_k, _nt(qq, kc) * ATTN_SCALE, NEG)
            p2 = jnp.exp(s2 - ll)
            dv_ref[:, hs] = _tn(p2.astype(BF16), dd)
            ds2 = p2 * (_nt(dd, vc) - dl)
            dk_ref[:, hs] = _tn(ds2.astype(BF16), qq) * ATTN_SCALE

    blk = lambda f: pl.BlockSpec((B, D), f)
    cur = lambda r, nb: (nb, r)
    prev = lambda r, nb: (jnp.maximum(nb - 1, 0), r)
    nxt = lambda r, nb: (jnp.minimum(nb + 1, nb_count - 1), r)
    qcur = lambda r, nb: (nb, r * N_GROUPS + g)
    qnxt = lambda r, nb: (jnp.minimum(nb + 1, nb_count - 1), r * N_GROUPS + g)
    qv = q_rot.reshape(L, dil * N_GROUPS * D)
    view = lambda t: t.reshape(L, dil * D)
    dq, dk, dv = _pcall(
        body, name=name, grid=(dil, nb_count),
        in_specs=[blk(qcur), blk(qnxt), blk(cur), blk(prev), blk(cur), blk(prev), blk(cur), blk(nxt),
                  blk(cur), blk(nxt), blk(cur), blk(nxt)],
        out_specs=[blk(cur), blk(cur), blk(cur)],
        out_shape=[jax.ShapeDtypeStruct((L, dil * D), BF16), jax.ShapeDtypeStruct((L, dil * D), F32),
                   jax.ShapeDtypeStruct((L, dil * D), F32)],
        operands=(qv, qv, view(k), view(k), view(v), view(v), view(do), view(do), view(lse), view(lse), view(dlt),
                  view(dlt)),
        vmem=_vmem(24 * B * D * 4), carry=carry)
    return dq.reshape(S, D), dk.reshape(S, D), dv.reshape(S, D)


def _mlp_ple_fwd(z1, h1b, p_l, W, vec, l, run):
    D = z1.shape[1]
    g1, b1, g2, b2 = vec

    def act_ep(acc):
        return acc, jnp.square(jnp.maximum(acc, 0.0))

    t, act = run(_mm, h1b, W[f"mlp_up{l}"], mode="nn", outs=[BF16, BF16], epilogue=act_ep, name=f"mlp_up{l}")
    f = run(_mm, act, W[f"mlp_down{l}"], mode="nn", outs=[F32], name=f"mlp_down{l}")

    def z2_fn(z1_t, f_t, g1_, b1_, g2_, b2_):
        z2 = ALPHA * _ln(z1_t, g1_, b1_) + f_t
        return z2, _ln(z2, g2_, b2_)

    z2, h2b = _rows(z2_fn, [z1, f], [g1, b1, g2, b2], [(D, F32), (D, BF16)], [], tm=256, name=f"ln2_fwd{l}")
    pp = run(_mm, p_l, W[f"ple_proj{l}"], mode="nn", outs=[F32], name=f"ple_proj{l}")
    gpre = run(_mm, h2b, W[f"ple_gate{l}"], mode="nn", outs=[F32], name=f"ple_gate{l}")
    return t, act, z2, h2b, pp, gpre


def _mlp_ple_bwd(dy, d_pp, d_gpre, p_l, z1, h1b, t, act, z2, h2b, wts, vec, l, run, produce):
    D = z1.shape[1]
    up, down, pp_w, pg_w = wts
    g1, b1, g2, b2 = vec
    produce(f"ple_proj{l}", run(_mm, p_l, d_pp, mode="tn", outs=[BF16], layout=("col", N_CHIPS), name=f"d_ple_proj{l}"))
    produce(f"ple_gate{l}", run(_mm, h2b, d_gpre, mode="tn", outs=[BF16], layout=("row", N_CHIPS), name=f"d_ple_gate{l}"))
    dh2 = run(_mm, d_gpre, pg_w, mode="nt", outs=[F32], extras=[dy], epilogue=lambda acc, e: (acc + e,),
              name=f"dh2_{l}")

    def ln2_bwd(dh2_t, z2_t, g2_):
        n, rstd = _ln_norm(z2_t)
        dz2 = _ln_bwd(dh2_t, n, rstd, g2_)
        return dz2, dz2, dh2_t * n, dh2_t

    dz2, dz2b, dg2, db2 = _rows(ln2_bwd, [dh2, z2], [g2], [(D, F32), (D, BF16)], [D, D], tm=256,
                                name=f"ln2_bwd{l}")
    produce(f"mlp_down{l}", run(_mm, act, dz2b, mode="tn", outs=[BF16], layout=("row", N_CHIPS), name=f"d_mlp_down{l}"))
    dt = run(_mm, dz2b, down, mode="nt", outs=[BF16], extras=[t],
             epilogue=lambda acc, t_: (acc * 2.0 * jnp.maximum(t_.astype(F32), 0.0),), name=f"dt{l}")
    produce(f"mlp_up{l}", run(_mm, h1b, dt, mode="tn", outs=[BF16], layout=("col", N_CHIPS), name=f"d_mlp_up{l}"))
    dh1 = run(_mm, dt, up, mode="nt", outs=[F32], extras=[dz2], epilogue=lambda acc, e: (acc + ALPHA * e,),
              name=f"dh1_{l}")

    def ln1_bwd(dh1_t, z1_t, g1_):
        n, rstd = _ln_norm(z1_t)
        dz1 = _ln_bwd(dh1_t, n, rstd, g1_)
        return dz1, dz1, dh1_t * n, dh1_t

    dz1, dz1b, dg1, db1 = _rows(ln1_bwd, [dh1, z1], [g1], [(D, F32), (D, BF16)], [D, D], tm=256,
                                name=f"ln1_bwd{l}")
    return dz1, dz1b, (dg1, db1, dg2, db2)


def _ple_out(z2, pp, gpre, g2, b2):
    gt = _sigmoid(gpre)
    return _ln(z2, g2, b2) + pp * gt, gt


_GATHER_AT = {
    "conv_in_a": ("conv_w_out",),
    "conv_in_g": ("ple_gate0", "ple_proj0"),
    "conv_fwd": ("mlp_up0",),
    "mlp_up0": ("mlp_down0",),
    "mlp_down0": ("attn_w_q",),
    "ple_gate0": ("w_kv",),
    "kv_k": ("attn_w_o",),
    "kv_v": ("ple_proj1", "ple_gate1"),
    "attn_q": ("mlp_up1",),
    "mlp_up1": ("mlp_down1",),
}
_GATHER_FIRST = ("conv_w_in",)


def _local_step(x, p, cosf, sinf, target, W, V, shards=None, reducer=None, chip_arr=None):
    S, D = x.shape
    gw, gv = {}, {}
    if shards is not None:
        W = dict(_Gather(_GATHER_FIRST, shards, chip_arr).run_alone("gather_first"))

    def run(fn, *args, name, **kw):
        gather = _Gather(_GATHER_AT[name], shards, chip_arr) if (shards is not None and name in _GATHER_AT) else None
        carry = gather if reducer is None or gather is not None else reducer.carry(name)
        out = fn(*args, name=name, carry=carry, **kw)
        if gather is not None:
            W.update(gather.result)
        elif reducer is not None:
            reducer.carried()
        return out

    a_pre = run(_mm, x, W["conv_w_in"], b_sel=0, mode="nn", outs=[F32], name="conv_in_a")
    g_pre = run(_mm, x, W["conv_w_in"], b_sel=1, mode="nn", outs=[F32], name="conv_in_g")
    (glu,) = _rows(lambda a, g, ba, bg: ((a + ba) * _sigmoid(g + bg),), [a_pre, g_pre], [V["conv_b_a"], V["conv_b_g"]],
                   [(D, F32)], [], tm=256, name="glu_fwd")
    cv = run(_conv_fwd, glu, V["conv_dw"], V["conv_dw_b"], name="conv_fwd")

    def silu_ln(c, g_, b_):
        y = _ln(c, g_, b_)
        return (y * _sigmoid(y),)

    (sb,) = _rows(silu_ln, [cv], [V["conv_ln_g"], V["conv_ln_b"]], [(D, BF16)], [], tm=256, name="conv_ln_fwd")
    mix0 = _mm(sb, W["conv_w_out"], mode="nn", outs=[F32], name="conv_out")

    def z1_fn(x_t, mix_t, g_, b_):
        z1 = ALPHA * x_t + mix_t
        return z1, _ln(z1, g_, b_)

    vec0 = (V["ln1_g0"], V["ln1_b0"], V["ln2_g0"], V["ln2_b0"])
    vec1 = (V["ln1_g1"], V["ln1_b1"], V["ln2_g1"], V["ln2_b1"])
    z1_0, h1b_0 = _rows(z1_fn, [x, mix0], [vec0[0], vec0[1]], [(D, F32), (D, BF16)], [], tm=256, name="ln1_fwd0")
    t0, act0, z2_0, h2b_0, pp0, gpre0 = _mlp_ple_fwd(z1_0, h1b_0, p[0], W, vec0, 0, run)

    def x1_fn(z2, pp, gpre, g2, b2, kg, kb):
        x1, _ = _ple_out(z2, pp, gpre, g2, b2)
        return x1, _ln(x1, kg, kb)

    x1, kvn = _rows(x1_fn, [z2_0, pp0, gpre0], [vec0[2], vec0[3], V["kv_ln_g"], V["kv_ln_b"]],
                    [(D, F32), (D, BF16)], [], tm=256, name="ple_out0")

    rot_ep = lambda acc, c_, s_: (_rot(acc, c_, s_),)
    k_rot = run(_mm, kvn, W["w_kv"], b_sel=0, mode="nn", outs=[BF16], rextras=[cosf, sinf], epilogue=rot_ep, name="kv_k")
    v_b = run(_mm, kvn, W["w_kv"], b_sel=1, mode="nn", outs=[BF16], name="kv_v")
    q_rot = run(_mm, x1, W["attn_w_q"], mode="nn", outs=[BF16], rextras=[cosf, sinf], epilogue=rot_ep, name="attn_q")
    og, lg = [], []
    for g, dil in enumerate(DILATIONS):
        o_g, l_g = _attn_fwd(q_rot, k_rot, v_b, g, dil)
        og.append(o_g)
        lg.append(l_g)

    def merge(o0, o1, o2, l0, l1, l2):
        m = jnp.maximum(jnp.maximum(l0, l1), l2)
        e0, e1, e2 = jnp.exp(l0 - m), jnp.exp(l1 - m), jnp.exp(l2 - m)
        den = e0 + e1 + e2
        o = (e0 * o0.astype(F32) + e1 * o1.astype(F32) + e2 * o2.astype(F32)) / den
        return o, m + jnp.log(den)

    ob, lse = _rows(merge, og + lg, [], [(D, BF16), (D, F32)], [], tm=256, name="attn_merge")
    mix1 = _mm(ob, W["attn_w_o"], mode="nn", outs=[F32], name="attn_out")
    z1_1, h1b_1 = _rows(z1_fn, [x1, mix1], [vec1[0], vec1[1]], [(D, F32), (D, BF16)], [], tm=256, name="ln1_fwd1")
    t1, act1, z2_1, h2b_1, pp1, gpre1 = _mlp_ple_fwd(z1_1, h1b_1, p[1], W, vec1, 1, run)
    wts0 = (W["mlp_up0"], W["mlp_down0"], W["ple_proj0"], W["ple_gate0"])
    wts1 = (W["mlp_up1"], W["mlp_down1"], W["ple_proj1"], W["ple_gate1"])

    def head(z2, pp, gpre, tgt, g2, b2):
        y, gt = _ple_out(z2, pp, gpre, g2, b2)
        err = y - tgt
        dy = err * (1.0 / D)
        return dy, dy * gt, dy * pp * gt * (1.0 - gt), 0.5 * err * err * (1.0 / D)

    dy1, d_pp1, d_gpre1, loss_cols = _rows(head, [z2_1, pp1, gpre1, target], [vec1[2], vec1[3]],
                                           [(D, F32), (D, BF16), (D, BF16)], [D], tm=256, name="loss_head")

    def produce(name, grad):
        gw[name] = grad
        if reducer is not None:
            reducer.produced(name, grad)

    dz1_1, dz1b_1, (gv["ln1_g1"], gv["ln1_b1"], gv["ln2_g1"], gv["ln2_b1"]) = _mlp_ple_bwd(
        dy1, d_pp1, d_gpre1, p[1], z1_1, h1b_1, t1, act1, z2_1, h2b_1, wts1, vec1, 1, run, produce)
    produce("attn_w_o", run(_mm, ob, dz1b_1, mode="tn", outs=[BF16], layout=("row", N_CHIPS), name="d_attn_w_o"))

    def do_ep(acc, o_t):
        prod = acc * o_t.astype(F32)
        dlt = _per_head(prod, lambda ph, i: jnp.broadcast_to(jnp.sum(ph, axis=1, keepdims=True), ph.shape))
        return acc, dlt

    do_b, dlt = run(_mm, dz1b_1, W["attn_w_o"], mode="nt", outs=[BF16, F32], extras=[ob], epilogue=do_ep, name="attn_do")
    dqs, dks, dvs = [], [], []
    for g, dil in enumerate(DILATIONS):
        dq_g, dk_g, dv_g = run(_attn_bwd, q_rot, k_rot, v_b, do_b, lse, dlt, g, dil, name=f"attn_bwd_g{g}")
        dqs.append(dq_g)
        dks.append(dk_g)
        dvs.append(dv_g)

    def unrot(q0, q1, q2, k0, k1, k2, v0, v1, v2, c_, s_):
        dq = jnp.concatenate([_rot_t(t_.astype(F32), c_, s_) for t_ in (q0, q1, q2)], axis=1)
        return dq, _rot_t(k0 + k1 + k2, c_, s_), v0 + v1 + v2

    dq, dk, dv = run(_rows, unrot, dqs + dks + dvs + [cosf, sinf], [], [(N_GROUPS * D, BF16), (D, BF16), (D, BF16)], [],
                     tm=128, name="attn_unrot")
    produce("attn_w_q", run(_mm, x1, dq, mode="tn", outs=[BF16], layout=("col", N_CHIPS), name="d_attn_w_q"))
    dx1_q = run(_mm, dq, W["attn_w_q"], mode="nt", outs=[F32], extras=[dz1_1],
                epilogue=lambda acc, e: (acc + ALPHA * e,), name="dx1_q")
    produce("w_kv", jnp.concatenate(
        [run(_mm, kvn, dk, mode="tn", outs=[BF16], layout=("col", 2), name="d_w_kv_k"),
         run(_mm, kvn, dv, mode="tn", outs=[BF16], layout=("col", 2), name="d_w_kv_v")], axis=1))
    dkvn_k = run(_mm, dk, W["w_kv"], b_sel=0, mode="nt", outs=[F32], name="dkvn_k")
    dkvn = run(_mm, dv, W["w_kv"], b_sel=1, mode="nt", outs=[F32], extras=[dkvn_k], epilogue=lambda acc, e: (acc + e,),
               name="dkvn_v")

    def x1_bwd(dx1q_t, dkvn_t, x1_t, pp, gpre, kg):
        n, rstd = _ln_norm(x1_t)
        dy = dx1q_t + _ln_bwd(dkvn_t, n, rstd, kg)
        gt = _sigmoid(gpre)
        return dy, dy * gt, dy * pp * gt * (1.0 - gt), dkvn_t * n, dkvn_t

    dy0, d_pp0, d_gpre0, gv["kv_ln_g"], gv["kv_ln_b"] = _rows(
        x1_bwd, [dx1_q, dkvn, x1, pp0, gpre0], [V["kv_ln_g"]], [(D, F32), (D, BF16), (D, BF16)], [D, D], tm=256,
        name="x1_bwd")

    dz1_0, dz1b_0, (gv["ln1_g0"], gv["ln1_b0"], gv["ln2_g0"], gv["ln2_b0"]) = _mlp_ple_bwd(
        dy0, d_pp0, d_gpre0, p[0], z1_0, h1b_0, t0, act0, z2_0, h2b_0, wts0, vec0, 0, run, produce)
    produce("conv_w_out", run(_mm, sb, dz1b_0, mode="tn", outs=[BF16], layout=("row", N_CHIPS), name="d_conv_w_out"))
    ds = run(_mm, dz1b_0, W["conv_w_out"], mode="nt", outs=[F32], name="conv_ds")

    def conv_ln_bwd(ds_t, c_t, g_, b_):
        n, rstd = _ln_norm(c_t)
        y = n * g_ + b_
        sg = _sigmoid(y)
        dln = ds_t * sg * (1.0 + y * (1.0 - sg))
        dc = _ln_bwd(dln, n, rstd, g_)
        return dc, dln * n, dln, dc

    dc, gv["conv_ln_g"], gv["conv_ln_b"], gv["conv_dw_b"] = _rows(
        conv_ln_bwd, [ds, cv], [V["conv_ln_g"], V["conv_ln_b"]], [(D, F32)], [D, D, D], tm=256, name="conv_ln_bwd")
    da, dg, gv["conv_dw"], gv["conv_b_a"], gv["conv_b_g"] = run(
        _conv_bwd, dc, glu, a_pre, g_pre, V["conv_dw"], V["conv_b_a"], V["conv_b_g"], name="conv_bwd")
    produce("conv_w_in", jnp.concatenate(
        [run(_mm, x, da, mode="tn", outs=[BF16], layout=("col", 2), name="d_conv_w_in_a"),
         run(_mm, x, dg, mode="tn", outs=[BF16], layout=("col", 2), name="d_conv_w_in_g")], axis=1))
    dx_a = run(_mm, da, W["conv_w_in"], b_sel=0, mode="nt", outs=[F32], extras=[dz1_0],
               epilogue=lambda acc, e: (acc + ALPHA * e,), name="dx_a")
    grad_x = run(_mm, dg, W["conv_w_in"], b_sel=1, mode="nt", outs=[F32], extras=[dx_a],
                 epilogue=lambda acc, e: (acc + e,), name="dx_g")
    if reducer is not None:
        reducer.carry("share_last").run_alone("share_last")
        reducer.carried()
    return loss_cols, grad_x, gw, gv


def _place():
    x, y, c = lax.axis_index("x"), lax.axis_index("y"), lax.axis_index("c")
    chips = [(1 - x, y), (x, 1 - y), (1 - x, 1 - y)]
    return x, y, c, chips


def _remote(src, dst, ssem, rsem, dev):
    return pltpu.make_async_remote_copy(src_ref=src, dst_ref=dst, send_sem=ssem, recv_sem=rsem, device_id=dev,
                                        device_id_type=MESH)


def _allgather8(block, name):
    R, C = block.shape

    def body(x_ref, out_ref, send_sems, recv_sems, local_sem):
        x, y, c, chips = _place()
        me, sibling = (x, y, c), (x, y, 1 - c)

        def slot(px, py, pc):
            return out_ref.at[4 * px + 2 * py + pc]

        def copy(k, blockpos, to, src=None):
            return _remote(slot(*blockpos) if src is None else src, slot(*blockpos), send_sems.at[k], recv_sems.at[k], to)

        mine = pltpu.make_async_copy(x_ref, slot(*me), local_sem)
        mine.start()
        first = [copy(0, me, sibling, src=x_ref)]
        first += [copy(1 + j, me, (*chip, c), src=x_ref) for j, chip in enumerate(chips)]
        for cp in first:
            cp.start()
        passed = [copy(4 + j, (*chip, c), sibling) for j, chip in enumerate(chips)]
        for j, chip in enumerate(chips):
            copy(1 + j, (*chip, c), me).wait_recv()
            passed[j].start()
        copy(0, sibling, me).wait_recv()
        for j, chip in enumerate(chips):
            copy(4 + j, (*chip, 1 - c), me).wait_recv()
        for cp in first + passed:
            cp.wait_send()
        mine.wait()

    return pl.pallas_call(
        body, name=name, out_shape=jax.ShapeDtypeStruct((8, R, C), block.dtype),
        in_specs=[pl.BlockSpec(memory_space=pltpu.VMEM)], out_specs=pl.BlockSpec(memory_space=pltpu.VMEM),
        scratch_shapes=[pltpu.SemaphoreType.DMA((7,)), pltpu.SemaphoreType.DMA((7,)), pltpu.SemaphoreType.DMA],
        compiler_params=pltpu.CompilerParams(vmem_limit_bytes=_vmem(10 * _nbytes((R, C), block.dtype))),
    )(block)


_MATS = (
    ("conv_w_in", "conv_w_in", 0, "col", True),
    ("conv_w_out", "conv_w_out", 0, "row", False),
    ("mlp_up0", "mlp_up", 0, "col", False),
    ("mlp_down0", "mlp_down", 0, "row", False),
    ("ple_proj0", "ple_proj", 0, "col", False),
    ("ple_gate0", "ple_gate", 0, "row", False),
    ("w_kv", "w_kv", None, "col", True),
    ("attn_w_q", "attn_w_q", 0, "col", False),
    ("attn_w_o", "attn_w_o", 0, "row", False),
    ("mlp_up1", "mlp_up", 1, "col", False),
    ("mlp_down1", "mlp_down", 1, "row", False),
    ("ple_proj1", "ple_proj", 1, "col", False),
    ("ple_gate1", "ple_gate", 1, "row", False),
)


class _Carry:
    result = None
    aliases = {}

    def set_result(self, outs):
        self.result = dict(zip(self.names, outs))

    def run_alone(self, name):
        n_in, n_out = len(self.ins), len(self.out_shape)

        def body(*refs):
            in_refs, out_refs, sems = refs[:n_in], refs[n_in:n_in + n_out], refs[n_in + n_out:]
            self.start(in_refs, out_refs, sems)
            self.finish(in_refs, out_refs, sems)

        outs = pl.pallas_call(body, name=name, out_shape=self.out_shape, in_specs=[ANY] * n_in, out_specs=[ANY] * n_out,
                              scratch_shapes=self.scratch, input_output_aliases=dict(self.aliases))(*self.ins)
        self.set_result(outs)
        return self.result


class _Gather(_Carry):
    def __init__(self, names, shards, chip_arr):
        mats = [m for m in _MATS if m[0] in names]
        srcs = sorted({m[1] for m in mats})
        self.names = [m[0] for m in mats]
        self.out_shape, self.geo, placed = [], [], []
        for name, src, layer, kind, split in mats:
            s = shards[src]
            ks, ns = s.shape[-2:]
            K, N = (ks, ns * N_CHIPS) if kind == "col" else (ks * N_CHIPS, ns)
            self.out_shape.append(jax.ShapeDtypeStruct((2, K, N // 2) if split else (K, N), BF16))
            self.geo.append((srcs.index(src), layer if s.ndim == 3 else None, kind, split, K, N))
            placed.append(_place_shard(s, layer if s.ndim == 3 else None, kind, split, chip_arr, f"place_{name}"))
        T = len(mats)
        self.ins = [shards[n] for n in srcs] + placed
        self.aliases = {len(srcs) + t: t for t in range(T)}
        self.scratch = [pltpu.SemaphoreType.DMA((3 * T,)) for _ in range(4)]
        self.result = None

    def _copies(self, in_refs, out_refs, sems):
        geo, T = self.geo, len(self.geo)
        s_ici, r_ici, s_d2d, r_d2d = sems
        x, y, c, chips = _place()
        me = 2 * x + y
        sibling = (x, y, 1 - c)
        idx = [2 * cx + cy for cx, cy in chips]

        def src_ref(t):
            i, layer, _, _, _, _ = geo[t]
            return in_refs[i] if layer is None else in_refs[i].at[layer]

        def src_half(t, h):
            _, _, kind, _, K, N = geo[t]
            if kind == "col":
                return src_ref(t).at[pl.ds(h * (K // 2), K // 2), :]
            return src_ref(t).at[:, pl.ds(h * (N // 2), N // 2)]

        def dst(t, j, h):
            _, _, kind, split, K, N = geo[t]
            n, k = N // N_CHIPS, K // N_CHIPS
            if kind == "col":
                rows = slice(None) if h is None else pl.ds(h * (K // 2), K // 2)
                if split:
                    return out_refs[t].at[j // 2, rows, pl.ds((j % 2) * n, n)]
                return out_refs[t].at[rows, pl.ds(j * n, n)]
            cols = slice(None) if h is None else pl.ds(h * (N // 2), N // 2)
            return out_refs[t].at[pl.ds(j * k, k), cols]

        sends = [_remote(src_half(t, c), dst(t, me, c), s_ici.at[3 * t + kk], r_ici.at[3 * t + kk], (*chips[kk], c))
                 for t in range(T) for kk in range(3)]
        hops = []
        for t in range(T):
            for kk in range(3):
                mine, theirs = dst(t, idx[kk], c), dst(t, idx[kk], 1 - c)
                hops.append((_remote(mine, mine, s_ici.at[3 * t + kk], r_ici.at[3 * t + kk], sibling),
                             _remote(mine, mine, s_d2d.at[3 * t + kk], r_d2d.at[3 * t + kk], sibling),
                             _remote(theirs, theirs, s_d2d.at[3 * t + kk], r_d2d.at[3 * t + kk], sibling)))
        return sends, hops

    def start(self, in_refs, out_refs, sems):
        for cp in self._copies(in_refs, out_refs, sems)[0]:
            cp.start()

    def finish(self, in_refs, out_refs, sems):
        sends, hops = self._copies(in_refs, out_refs, sems)
        for landed, forward, _ in hops:
            landed.wait_recv()
            forward.start()
        for _, _, from_sibling in hops:
            from_sibling.wait_recv()
        for cp in sends + [h[1] for h in hops]:
            cp.wait_send()


def _place_shard(shard, layer, kind, split, chip_arr, name):
    ks, ns = shard.shape[-2:]
    K, N = (ks, ns * N_CHIPS) if kind == "col" else (ks * N_CHIPS, ns)
    tr = _fit(256, ks)
    nb = ks // tr
    if shard.ndim == 3:
        in_spec = pl.BlockSpec((None, tr, ns), lambda i, me: (layer, i, 0))
    else:
        in_spec = pl.BlockSpec((tr, ns), lambda i, me: (i, 0))
    if kind == "row":
        out_shape, out_spec = (K, N), pl.BlockSpec((tr, ns), lambda i, me: (me[0] * nb + i, 0))
    elif split:
        out_shape, out_spec = (2, K, N // 2), pl.BlockSpec((None, tr, ns), lambda i, me: (me[0] // 2, i, me[0] % 2))
    else:
        out_shape, out_spec = (K, N), pl.BlockSpec((tr, ns), lambda i, me: (i, me[0]))

    def body(me_ref, s_ref, o_ref):
        o_ref[...] = s_ref[...]

    return pl.pallas_call(
        body, name=name, out_shape=jax.ShapeDtypeStruct(out_shape, BF16),
        grid_spec=pltpu.PrefetchScalarGridSpec(num_scalar_prefetch=1, grid=(nb,), in_specs=[in_spec], out_specs=out_spec),
        compiler_params=pltpu.CompilerParams(dimension_semantics=("parallel",), vmem_limit_bytes=_vmem(4 * tr * ns * 2)),
    )(chip_arr, shard)


class _Multi(_Carry):
    def __init__(self, parts):
        self.parts = parts
        self.ins = [a for p in parts for a in p.ins]
        self.out_shape = [a for p in parts for a in p.out_shape]
        self.scratch = [a for p in parts for a in p.scratch]
        self.aliases, n_in, n_out = {}, 0, 0
        for p in parts:
            self.aliases.update({n_in + i: n_out + o for i, o in p.aliases.items()})
            n_in, n_out = n_in + len(p.ins), n_out + len(p.out_shape)

    def _split(self, seq, field):
        out, at = [], 0
        for p in self.parts:
            n = len(getattr(p, field))
            out.append(seq[at:at + n])
            at += n
        return out

    def _each(self, method, in_refs, out_refs, sems):
        for p, i, o, s in zip(self.parts, self._split(in_refs, "ins"), self._split(out_refs, "out_shape"),
                              self._split(sems, "scratch")):
            getattr(p, method)(i, o, s)

    def start(self, in_refs, out_refs, sems):
        self._each("start", in_refs, out_refs, sems)

    def finish(self, in_refs, out_refs, sems):
        self._each("finish", in_refs, out_refs, sems)

    def set_result(self, outs):
        for p, o in zip(self.parts, self._split(list(outs), "out_shape")):
            p.set_result(o)


class _PairSend(_Carry):
    def __init__(self, grads):
        self.names = list(grads)
        self.ins = [grads[n] for n in self.names]
        self.out_shape = [jax.ShapeDtypeStruct(a.shape[1:], BF16) for a in self.ins]
        T = len(self.names)
        self.scratch = [pltpu.SemaphoreType.DMA((T,)), pltpu.SemaphoreType.DMA((T,))]

    def _copies(self, in_refs, out_refs, sems):
        x, y, c, _ = _place()
        return [_remote(in_refs[t].at[1 - c], out_refs[t], sems[0].at[t], sems[1].at[t], (x, y, 1 - c))
                for t in range(len(self.names))]

    def start(self, in_refs, out_refs, sems):
        for cp in self._copies(in_refs, out_refs, sems):
            cp.start()

    def finish(self, in_refs, out_refs, sems):
        for cp in self._copies(in_refs, out_refs, sems):
            cp.wait()


class _ChipScatter(_Carry):
    def __init__(self, sums):
        self.names = list(sums)
        T = len(self.names)
        self.ins = [sums[n][0] for n in self.names] + [sums[n][1] for n in self.names]
        self.out_shape = [jax.ShapeDtypeStruct(a.shape, BF16) for a in self.ins[:T]]
        self.aliases = {T + t: t for t in range(T)}
        self.scratch = [pltpu.SemaphoreType.DMA((3 * T,)), pltpu.SemaphoreType.DMA((3 * T,))]

    def _copies(self, in_refs, out_refs, sems):
        ssem, rsem = sems
        x, y, c, chips = _place()
        me = 2 * x + y
        idx = [2 * cx + cy for cx, cy in chips]
        T = len(self.names)
        sends = [_remote(in_refs[t].at[idx[kk]], out_refs[t].at[me], ssem.at[3 * t + kk], rsem.at[3 * t + kk],
                         (*chips[kk], c)) for t in range(T) for kk in range(3)]
        lands = [_remote(out_refs[t].at[idx[kk]], out_refs[t].at[idx[kk]], ssem.at[3 * t + kk], rsem.at[3 * t + kk],
                         (*chips[kk], c)) for t in range(T) for kk in range(3)]
        return sends, lands

    def start(self, in_refs, out_refs, sems):
        for cp in self._copies(in_refs, out_refs, sems)[0]:
            cp.start()

    def finish(self, in_refs, out_refs, sems):
        sends, lands = self._copies(in_refs, out_refs, sems)
        for cp in lands:
            cp.wait_recv()
        for cp in sends:
            cp.wait_send()


class _PairShare(_Carry):
    def __init__(self, halves):
        self.names = list(halves)
        self.ins = [halves[n] for n in self.names]
        self.out_shape = [jax.ShapeDtypeStruct(a.shape, F32) for a in self.ins]
        T = len(self.names)
        self.aliases = {t: t for t in range(T)}
        self.scratch = [pltpu.SemaphoreType.DMA((T,)), pltpu.SemaphoreType.DMA((T,))]

    def _copies(self, in_refs, out_refs, sems):
        ssem, rsem = sems
        x, y, c, _ = _place()
        sibling = (x, y, 1 - c)
        T = len(self.names)
        sends = [_remote(out_refs[t].at[c], out_refs[t].at[c], ssem.at[t], rsem.at[t], sibling) for t in range(T)]
        lands = [_remote(out_refs[t].at[1 - c], out_refs[t].at[1 - c], ssem.at[t], rsem.at[t], sibling) for t in range(T)]
        return sends, lands

    def start(self, in_refs, out_refs, sems):
        for cp in self._copies(in_refs, out_refs, sems)[0]:
            cp.start()

    def finish(self, in_refs, out_refs, sems):
        sends, lands = self._copies(in_refs, out_refs, sems)
        for cp in lands:
            cp.wait_recv()
        for cp in sends:
            cp.wait_send()


def _pair_sum(own, landed, c_arr, name):
    _, ns, r, cc = own.shape
    rows = ns * r
    tr = _fit(512, rows)

    def body(c_ref, a_ref, b_ref, o_ref, o2_ref):
        total = (a_ref[...].astype(F32) + b_ref[...].astype(F32)).astype(o_ref.dtype)
        o_ref[...] = total
        o2_ref[...] = total

    tile = pl.BlockSpec((tr, cc), lambda i, c_ref: (i, 0))
    out = pl.pallas_call(
        body, name=name, out_shape=[jax.ShapeDtypeStruct((rows, cc), BF16)] * 2,
        grid_spec=pltpu.PrefetchScalarGridSpec(
            num_scalar_prefetch=1, grid=(rows // tr,),
            in_specs=[pl.BlockSpec((None, tr, cc), lambda i, c_ref: (c_ref[0], i, 0)), tile], out_specs=[tile, tile]),
        compiler_params=pltpu.CompilerParams(dimension_semantics=("parallel",), vmem_limit_bytes=_vmem(8 * tr * cc * 4)),
    )(c_arr, own.reshape(2, rows, cc), landed.reshape(rows, cc))
    return out[0].reshape(ns, r, cc), out[1].reshape(ns, r, cc)


def _chip_sum(parts, c_arr, name):
    _, r, cc = parts.shape
    tr = _fit(256, r)

    def body(c_ref, p_ref, o_ref):
        acc = p_ref[0].astype(F32)
        for j in range(1, N_CHIPS):
            acc = acc + p_ref[j].astype(F32)
        o_ref[...] = acc

    return pl.pallas_call(
        body, name=name, out_shape=jax.ShapeDtypeStruct((2, r, cc), F32),
        grid_spec=pltpu.PrefetchScalarGridSpec(
            num_scalar_prefetch=1, grid=(r // tr,),
            in_specs=[pl.BlockSpec((N_CHIPS, tr, cc), lambda i, c_ref: (0, i, 0))],
            out_specs=pl.BlockSpec((None, tr, cc), lambda i, c_ref: (c_ref[0], i, 0))),
        compiler_params=pltpu.CompilerParams(dimension_semantics=("parallel",), vmem_limit_bytes=_vmem(12 * tr * cc * 4)),
    )(c_arr, parts)


def _adamw_math(w, g, m, v):
    m2 = ADAM_B1 * m + (1.0 - ADAM_B1) * g
    v2 = ADAM_B2 * v + (1.0 - ADAM_B2) * jnp.square(g)
    m_hat = m2 / (1.0 - ADAM_B1 ** ADAM_STEP)
    v_hat = v2 / (1.0 - ADAM_B2 ** ADAM_STEP)
    delta = -ADAM_LR * (m_hat / (jnp.sqrt(v_hat) + ADAM_EPS) + ADAM_WD * w)
    return delta, m2, v2


def _adamw_mat(g2, w, m, v, layer, kind, prev, name):
    shape = w.shape
    ks, ns = shape[-2:]
    _, r, cc = g2.shape
    tr, tc = _fit(256, r), _fit(1024, cc)
    assert (r, cc) == ((ks // 2, ns) if kind == "col" else (ks, ns // 2))
    assert r % tr == 0 and cc % tc == 0
    rb, cb = r // tr, cc // tc
    if kind == "col":
        g_spec = pl.BlockSpec((None, tr, tc), lambda i, j: (i // rb, i % rb, j))
    else:
        g_spec = pl.BlockSpec((None, tr, tc), lambda i, j: (j // cb, i, j % cb))
    if w.ndim == 3:
        w_spec = pl.BlockSpec((None, tr, tc), lambda i, j: (layer, i, j))
    else:
        w_spec = pl.BlockSpec((tr, tc), lambda i, j: (i, j))
    n_prev = 0 if prev is None else 4

    def body(*refs):
        g_ref, w_ref, m_ref, v_ref = refs[:4]
        go_ref, d_ref, mo_ref, vo_ref = refs[4 + n_prev:]
        g = g_ref[...]
        delta, m2, v2 = _adamw_math(w_ref[...], g, m_ref[...], v_ref[...])
        go_ref[...] = g
        d_ref[...] = delta
        mo_ref[...] = m2
        vo_ref[...] = v2

    return pl.pallas_call(
        body, name=name, grid=(ks // tr, ns // tc),
        in_specs=[g_spec, w_spec, w_spec, w_spec] + [ANY] * n_prev, out_specs=[w_spec] * 4,
        out_shape=[jax.ShapeDtypeStruct(shape, F32)] * 4,
        input_output_aliases={4 + i: i for i in range(n_prev)},
        compiler_params=pltpu.CompilerParams(dimension_semantics=("parallel", "parallel"),
                                             vmem_limit_bytes=_vmem(16 * tr * tc * 4)),
    )(g2, w, m, v, *(prev or ()))


def _adamw_small(g, w, m, v, name):
    def body(g_ref, w_ref, m_ref, v_ref, d_ref, mo_ref, vo_ref):
        delta, m2, v2 = _adamw_math(w_ref[...], g_ref[...], m_ref[...], v_ref[...])
        d_ref[...] = delta
        mo_ref[...] = m2
        vo_ref[...] = v2

    return pl.pallas_call(body, name=name, out_shape=[jax.ShapeDtypeStruct(w.shape, F32)] * 3)(g, w, m, v)


def _sum8(parts, name):
    def body(p_ref, o_ref):
        acc = p_ref[0]
        for j in range(1, 8):
            acc = acc + p_ref[j]
        o_ref[...] = acc

    return pl.pallas_call(body, name=name, out_shape=jax.ShapeDtypeStruct(parts.shape[1:], F32),
                          compiler_params=pltpu.CompilerParams(vmem_limit_bytes=_vmem(12 * _nbytes(parts.shape[1:], F32))))(parts)


_REDUCE_AT = {
    "d_ple_gate1": (("A", "ple_proj1"),),
    "dh2_1": (("A", "ple_gate1"),),
    "d_mlp_down1": (("B", "ple_proj1"), ("B", "ple_gate1")),
    "dt1": (("A", "mlp_down1"),),
    "d_mlp_up1": (("B", "mlp_down1"), ("C", "ple_proj1"), ("C", "ple_gate1")),
    "dh1_1": (("A", "mlp_up1"),),
    "d_attn_w_o": (("C", "mlp_down1"),),
    "attn_do": (("A", "attn_w_o"),),
    "attn_bwd_g0": (("B", "mlp_up1"),),
    "attn_bwd_g1": (("B", "attn_w_o"),),
    "attn_unrot": (("C", "mlp_up1"), ("C", "attn_w_o")),
    "dx1_q": (("A", "attn_w_q"),),
    "dkvn_k": (("A", "w_kv"),),
    "d_ple_gate0": (("A", "ple_proj0"),),
    "dh2_0": (("A", "ple_gate0"),),
    "d_mlp_down0": (("B", "attn_w_q"), ("B", "ple_proj0")),
    "dt0": (("B", "w_kv"), ("B", "ple_gate0"), ("A", "mlp_down0")),
    "d_mlp_up0": (("B", "mlp_down0"), ("C", "attn_w_q"), ("C", "ple_proj0"), ("C", "w_kv"), ("C", "ple_gate0")),
    "dh1_0": (("A", "mlp_up0"),),
    "d_conv_w_out": (("C", "mlp_down0"),),
    "conv_ds": (("A", "conv_w_out"),),
    "conv_bwd": (("B", "mlp_up0"), ("B", "conv_w_out")),
    "d_conv_w_in_g": (("C", "mlp_up0"), ("C", "conv_w_out")),
    "dx_a": (("A", "conv_w_in"),),
    "dx_g": (("B", "conv_w_in"),),
    "share_last": (("C", "conv_w_in"),),
}


class _Reducer:
    def __init__(self, w, mom, var, c_arr):
        self.w, self.mom, self.var, self.c_arr = w, mom, var, c_arr
        self.mats = {m[0]: m for m in _MATS}
        self.grads, self.pair_sums, self.chip_sums, self.out = {}, {}, {}, {}

    def produced(self, name, grad):
        self.grads[name] = grad

    def carry(self, call):
        parts = []
        for cls, stage, src in ((_PairSend, "A", self.grads), (_ChipScatter, "B", self.pair_sums),
                                (_PairShare, "C", self.chip_sums)):
            names = [n for s, n in _REDUCE_AT.get(call, ()) if s == stage]
            if names:
                parts.append((stage, cls({n: src[n] for n in names})))
        self._parts = parts
        return _Multi([p for _, p in parts]) if parts else None

    def carried(self):
        for stage, part in self._parts:
            for name, val in part.result.items():
                if stage == "A":
                    self.pair_sums[name] = _pair_sum(self.grads[name], val, self.c_arr, f"pair_sum_{name}")
                elif stage == "B":
                    self.chip_sums[name] = _chip_sum(val, self.c_arr, f"chip_sum_{name}")
                else:
                    _, src, layer, kind, _ = self.mats[name]
                    self.out[src] = _adamw_mat(val, self.w[src], self.mom[src], self.var[src], layer or 0, kind,
                                               self.out.get(src), f"adamw_{name}")
        self._parts = []


_WEIGHTS = ("conv_w_in", "conv_b_in", "conv_dw", "conv_dw_b", "conv_ln_g", "conv_ln_b", "conv_w_out", "kv_ln_g",
            "kv_ln_b", "w_kv", "attn_w_q", "attn_w_o", "ln1_g", "ln1_b", "mlp_up", "mlp_down", "ln2_g", "ln2_b",
            "ple_proj", "ple_gate")
_SHARDED_VECS = ("conv_b_in", "conv_dw", "conv_dw_b", "conv_ln_g", "conv_ln_b")
_REPLICATED_VECS = ("kv_ln_g", "kv_ln_b", "ln1_g", "ln1_b", "ln2_g", "ln2_b")


def _pad_rows(a, rows):
    return jnp.concatenate([a, jnp.zeros((rows - a.shape[0], a.shape[1]), a.dtype)], axis=0) if a.shape[0] < rows else a


def _pack_sharded(d):
    n = d["conv_dw_b"].shape[-1]
    rows = [d["conv_b_in"].reshape(2, n), d["conv_dw"].reshape(CONV_WIDTH, n), d["conv_dw_b"].reshape(1, n),
            d["conv_ln_g"].reshape(1, n), d["conv_ln_b"].reshape(1, n)]
    return _pad_rows(jnp.concatenate(rows, axis=0), 40)


def _unpack_sharded(pack, like):
    n = pack.shape[1]
    return {"conv_b_in": pack[0:2].reshape(like["conv_b_in"].shape),
            "conv_dw": pack[2:2 + CONV_WIDTH].reshape(like["conv_dw"].shape),
            "conv_dw_b": pack[33:34].reshape(like["conv_dw_b"].shape),
            "conv_ln_g": pack[34:35].reshape(like["conv_ln_g"].shape),
            "conv_ln_b": pack[35:36].reshape(like["conv_ln_b"].shape)}


def _pack_replicated(d):
    D = d["kv_ln_g"].shape[-1]
    rows = [d[n].reshape(-1, D) for n in _REPLICATED_VECS]
    return _pad_rows(jnp.concatenate(rows, axis=0), 16)


def _unpack_replicated(pack, like):
    out, r = {}, 0
    for n in _REPLICATED_VECS:
        k = like[n].size // pack.shape[1]
        out[n] = pack[r:r + k].reshape(like[n].shape)
        r += k
    return out


def kernel(x, p, positions, conv_w_in, conv_b_in, conv_dw, conv_dw_b, conv_ln_g, conv_ln_b, conv_w_out, kv_ln_g, kv_ln_b, w_kv, attn_w_q, attn_w_o, ln1_g, ln1_b, mlp_up, mlp_down, ln2_g, ln2_b, ple_proj, ple_gate, loss_target, m_conv_w_in, m_conv_b_in, m_conv_dw, m_conv_dw_b, m_conv_ln_g, m_conv_ln_b, m_conv_w_out, m_kv_ln_g, m_kv_ln_b, m_w_kv, m_attn_w_q, m_attn_w_o, m_ln1_g, m_ln1_b, m_mlp_up, m_mlp_down, m_ln2_g, m_ln2_b, m_ple_proj, m_ple_gate, v_conv_w_in, v_conv_b_in, v_conv_dw, v_conv_dw_b, v_conv_ln_g, v_conv_ln_b, v_conv_w_out, v_kv_ln_g, v_kv_ln_b, v_w_kv, v_attn_w_q, v_attn_w_o, v_ln1_g, v_ln1_b, v_mlp_up, v_mlp_down, v_ln2_g, v_ln2_b, v_ple_proj, v_ple_gate):
    args = dict(locals())
    w = {n: args[n] for n in _WEIGHTS}
    mom = {n: args["m_" + n] for n in _WEIGHTS}
    var = {n: args["v_" + n] for n in _WEIGHTS}
    S, D = x.shape[1:]
    n4 = D // N_CHIPS
    chip = 2 * lax.axis_index("x") + lax.axis_index("y")
    c_arr = lax.axis_index("c").astype(jnp.int32).reshape(1)

    shards = {n: w[n].astype(BF16) for n in sorted({m[1] for m in _MATS})}
    vec_all = _allgather8(_pack_sharded(w), "gather_vectors")
    vec_full = jnp.concatenate([vec_all[2 * j] for j in range(N_CHIPS)], axis=1)
    b_in = vec_all[0::2, 0:2, :].reshape(1, 2 * D)
    V = {"conv_b_a": b_in[:, :D], "conv_b_g": b_in[:, D:],
         "conv_dw": _pad_rows(vec_full[2:2 + CONV_WIDTH], CONV_PAD), "conv_dw_b": vec_full[33:34],
         "conv_ln_g": vec_full[34:35], "conv_ln_b": vec_full[35:36],
         "kv_ln_g": kv_ln_g.reshape(1, D), "kv_ln_b": kv_ln_b.reshape(1, D)}
    for l in range(2):
        for n in ("ln1_g", "ln1_b", "ln2_g", "ln2_b"):
            V[f"{n}{l}"] = w[n][l].reshape(1, D)

    half = HEAD_DIM // 2
    inv_freq = ROPE_THETA ** (-jnp.arange(half, dtype=F32) * (2.0 / HEAD_DIM))
    ang = positions[0].astype(F32)[:, None] * inv_freq
    cos, sin = jnp.cos(ang), jnp.sin(ang)
    cosf = jnp.concatenate([cos, cos], axis=-1)
    sinf = jnp.concatenate([-sin, sin], axis=-1)

    reducer = _Reducer(w, mom, var, c_arr)
    loss_cols, grad_x, _, gv = _local_step(x[0], p[:, 0], cosf, sinf, loss_target[0], None, V, shards, reducer,
                                           chip.astype(jnp.int32).reshape(1))
    loss = lax.psum(jnp.sum(loss_cols), ("x", "y", "c"))
    out = dict(reducer.out)

    gpack = jnp.concatenate([gv["conv_b_a"], gv["conv_b_g"], gv["conv_dw"][:CONV_WIDTH], gv["conv_dw_b"],
                             gv["conv_ln_g"], gv["conv_ln_b"], gv["kv_ln_g"], gv["kv_ln_b"],
                             gv["ln1_g0"], gv["ln1_g1"], gv["ln1_b0"], gv["ln1_b1"],
                             gv["ln2_g0"], gv["ln2_g1"], gv["ln2_b0"], gv["ln2_b1"]], axis=0)
    gsum = _sum8(_allgather8(_pad_rows(gpack, 48), "gather_vector_grads"), "sum_vector_grads")
    g_b = lax.dynamic_slice_in_dim(jnp.concatenate([gsum[0:1], gsum[1:2]], axis=1), chip * 2 * n4, 2 * n4, axis=1)
    g_sh = lax.dynamic_slice_in_dim(gsum[2:36], chip * n4, n4, axis=1)
    g_sh = _pad_rows(jnp.concatenate([g_b.reshape(2, n4), g_sh], axis=0), 40)
    d_sh, m_sh, v_sh = _adamw_small(g_sh, _pack_sharded(w), _pack_sharded(mom), _pack_sharded(var), "adamw_sharded_vectors")
    g_rep = _pad_rows(gsum[36:46], 16)
    d_rep, m_rep, v_rep = _adamw_small(g_rep, _pack_replicated(w), _pack_replicated(mom), _pack_replicated(var),
                                       "adamw_replicated_vectors")
    small = {}
    for i, (sh, rep) in enumerate(((g_sh, g_rep), (d_sh, d_rep), (m_sh, m_rep), (v_sh, v_rep))):
        d = {**_unpack_sharded(sh, w), **_unpack_replicated(rep, w)}
        for n, val in d.items():
            small.setdefault(n, [None] * 4)[i] = val
    for n in small:
        out[n] = small[n]

    res = [loss, grad_x[None]]
    for i in range(4):
        res += [out[n][i] for n in _WEIGHTS]
    return tuple(res)
```

```python
import functools

import jax
import jax.numpy as jnp
from jax import lax
from jax.experimental import pallas as pl
from jax.experimental.pallas import tpu as pltpu

F32 = jnp.float32
BF16 = jnp.bfloat16

HEAD_DIM = 128
ATTN_BLOCK = 128
DILATIONS = (1, 4, 16)
N_GROUPS = 3
CONV_WIDTH = 31
CONV_PAD = 32
CONV_ROWS = 32
EPILOGUE_ROWS = 128
ROPE_THETA = 10000.0
LN_EPS = 1e-5
ALPHA = 4.0 ** 0.25
ATTN_SCALE = HEAD_DIM ** -0.5
NEG = -1e30

ADAM_LR = 0.001
ADAM_B1 = 0.9
ADAM_B2 = 0.999
ADAM_EPS = 1e-08
ADAM_WD = 0.01
ADAM_STEP = 10

N_CHIPS = 4
VMEM_CAP = 60 << 20
MESH = pl.DeviceIdType.MESH
ANY = pl.BlockSpec(memory_space=pl.ANY)


def _vmem(nbytes):
    return int(min(max(2 * nbytes + (8 << 20), 24 << 20), VMEM_CAP))


def _fit(tile, n):
    if n <= tile:
        return n
    t = tile - tile % 128
    while n % t:
        t -= 128
    return t


def _nbytes(shape, dtype):
    n = 1
    for s in shape:
        n *= s
    return n * jnp.dtype(dtype).itemsize


_DIMS = {"nn": (((1,), (0,)), ((), ())), "nt": (((1,), (1,)), ((), ())), "tn": (((0,), (0,)), ((), ()))}


def _pcall(body, *, name, grid, in_specs, out_specs, out_shape, operands, scratch_shapes=(), vmem, carry=None):
    if carry is None:
        return pl.pallas_call(
            body, name=name, grid=grid, in_specs=in_specs, out_specs=out_specs, out_shape=out_shape,
            scratch_shapes=list(scratch_shapes),
            compiler_params=pltpu.CompilerParams(dimension_semantics=("arbitrary",) * len(grid), vmem_limit_bytes=vmem),
        )(*operands)
    n_in, n_out, n_scr = len(in_specs), len(out_specs), len(scratch_shapes)
    c_in, c_out = len(carry.ins), len(carry.out_shape)

    def wrapped(*refs):
        ins, refs = refs[:n_in], refs[n_in:]
        c_ins, refs = refs[:c_in], refs[c_in:]
        outs, refs = refs[:n_out], refs[n_out:]
        c_outs, refs = refs[:c_out], refs[c_out:]
        scr, c_sems = refs[:n_scr], refs[n_scr:]
        first = functools.reduce(jnp.logical_and, [pl.program_id(d) == 0 for d in range(len(grid))])
        last = functools.reduce(jnp.logical_and, [pl.program_id(d) == grid[d] - 1 for d in range(len(grid))])
        pl.when(first)(lambda: carry.start(c_ins, c_outs, c_sems))
        body(*ins, *outs, *scr)
        pl.when(last)(lambda: carry.finish(c_ins, c_outs, c_sems))

    res = pl.pallas_call(
        wrapped, name=name, grid=grid, in_specs=list(in_specs) + [ANY] * c_in, out_specs=list(out_specs) + [ANY] * c_out,
        out_shape=list(out_shape) + list(carry.out_shape), scratch_shapes=list(scratch_shapes) + list(carry.scratch),
        input_output_aliases={len(operands) + i: n_out + o for i, o in carry.aliases.items()},
        compiler_params=pltpu.CompilerParams(dimension_semantics=("arbitrary",) * len(grid), vmem_limit_bytes=vmem),
    )(*operands, *carry.ins)
    carry.set_result(res[n_out:])
    return res[:n_out]


def _mm(a, b, *, mode, outs, name, epilogue=None, extras=(), rextras=(), vecs=(), a_sel=None, b_sel=None,
        tm=None, tn=2048, tk=None, layout=None, carry=None, ep_rows=None):
    a2, b2 = a.shape[-2:], b.shape[-2:]
    if mode == "nn":
        (M, K), (K2, N) = a2, b2
    elif mode == "nt":
        (M, K), (N, K2) = a2, b2
    else:
        (K, M), (K2, N) = a2, b2
    assert K == K2, (a.shape, b.shape, mode)
    if tm is None:
        tm = 1024 if mode == "tn" else 512
    if tk is None:
        tk = 1024 if mode == "tn" else 2048
    if layout is not None:
        kind, nslots = layout
        r, c = (M // 2, N // nslots) if kind == "col" else (M // nslots, N // 2)
        tm, tn = _fit(tm, r), _fit(tn, c)
        assert r % tm == 0 and c % tn == 0
    else:
        tm, tn = _fit(tm, M), _fit(tn, N)
    tk = _fit(tk, K)
    assert M % tm == 0 and N % tn == 0 and K % tk == 0, (M, N, K, tm, tn, tk)
    nk = K // tk
    grid = (N // tn, M // tm, nk)

    def spec(arr, sel, blk, imap):
        if arr.ndim == 3:
            return pl.BlockSpec((None,) + blk, lambda j, i, k: (sel,) + imap(j, i, k))
        return pl.BlockSpec(blk, imap)

    if mode == "tn":
        a_spec = spec(a, a_sel, (tk, tm), lambda j, i, k: (k, i))
    else:
        a_spec = spec(a, a_sel, (tm, tk), lambda j, i, k: (i, k))
    if mode == "nt":
        b_spec = spec(b, b_sel, (tn, tk), lambda j, i, k: (j, k))
    else:
        b_spec = spec(b, b_sel, (tk, tn), lambda j, i, k: (k, j))
    in_specs = [a_spec, b_spec]
    in_specs += [pl.BlockSpec((tm, tn), lambda j, i, k: (i, j)) for _ in extras]
    in_specs += [pl.BlockSpec((tm, e.shape[1]), lambda j, i, k: (i, 0)) for e in rextras]
    in_specs += [pl.BlockSpec((1, tn), lambda j, i, k: (0, j)) for _ in vecs]

    if layout is None:
        out_shape = [jax.ShapeDtypeStruct((M, N), d) for d in outs]
        out_specs = [pl.BlockSpec((tm, tn), lambda j, i, k: (i, j)) for _ in outs]
    else:
        assert len(outs) == 1
        out_shape = [jax.ShapeDtypeStruct((2, nslots, r, c), outs[0])]
        rb, cb = r // tm, c // tn
        if kind == "col":
            omap = lambda j, i, k: (i // rb, j // cb, i % rb, j % cb)
        else:
            omap = lambda j, i, k: (j // cb, i // rb, i % rb, j % cb)
        out_specs = [pl.BlockSpec((None, None, tm, tn), omap)]

    ne, nr, nv, no = len(extras), len(rextras), len(vecs), len(outs)
    dims = _DIMS[mode]

    def body(*refs):
        a_ref, b_ref = refs[0], refs[1]
        rest = refs[2:2 + ne + nr + nv]
        o_refs = refs[2 + ne + nr + nv:2 + ne + nr + nv + no]

        def finish(total):
            if epilogue is None:
                for o in o_refs:
                    o[...] = total.astype(o.dtype)
                return
            step = min(ep_rows or tm, tm)
            for r0 in range(0, tm, step):
                rows = slice(r0, r0 + step)
                tiles = [x[rows, :] for x in rest[:ne + nr]] + [x[...] for x in rest[ne + nr:]]
                for o, val in zip(o_refs, epilogue(total[rows, :], *tiles)):
                    o[rows, :] = val.astype(o.dtype)

        part = lax.dot_general(a_ref[...].astype(BF16), b_ref[...].astype(BF16), dims, preferred_element_type=F32)
        if nk == 1:
            finish(part)
            return
        acc = refs[-1]
        k = pl.program_id(2)

        @pl.when(k == 0)
        def _():
            acc[...] = part

        @pl.when((k > 0) & (k < nk - 1))
        def _():
            acc[...] += part

        @pl.when(k == nk - 1)
        def _():
            finish(acc[...] + part)

    blk = (_nbytes((tm, tk), a.dtype) + _nbytes((tk, tn), b.dtype) + sum(_nbytes((tm, tn), e.dtype) for e in extras)
           + sum(_nbytes((tm, tn), d) for d in outs) + 2 * tm * tn * 4)
    res = _pcall(body, name=name, grid=grid, in_specs=in_specs, out_specs=out_specs, out_shape=out_shape,
                 operands=(a, b, *extras, *rextras, *vecs),
                 scratch_shapes=[pltpu.VMEM((tm, tn), F32)] if nk > 1 else [], vmem=_vmem(blk), carry=carry)
    return res[0] if no == 1 else tuple(res)


def _rows(fn, rows, vecs, outs, sums, *, tm, name, carry=None):
    S = rows[0].shape[0]
    tm = min(tm, S)
    assert S % tm == 0
    nr, nv, no, ns = len(rows), len(vecs), len(outs), len(sums)

    def body(*refs):
        vals = fn(*[r[...] for r in refs[:nr + nv]])
        o_refs = refs[nr + nv:nr + nv + no]
        s_refs = refs[nr + nv + no:]
        for o, val in zip(o_refs, vals[:no]):
            o[...] = val.astype(o.dtype)
        if ns:
            @pl.when(pl.program_id(0) == 0)
            def _():
                for s in s_refs:
                    s[...] = jnp.zeros_like(s)

            for s, val in zip(s_refs, vals[no:]):
                s[...] += jnp.sum(val.astype(F32), axis=0, keepdims=True)

    in_specs = [pl.BlockSpec((tm, r.shape[1]), lambda i: (i, 0)) for r in rows]
    in_specs += [pl.BlockSpec(v.shape, lambda i: (0, 0)) for v in vecs]
    out_specs = [pl.BlockSpec((tm, c), lambda i: (i, 0)) for c, _ in outs]
    out_specs += [pl.BlockSpec((1, c), lambda i: (0, 0)) for c in sums]
    out_shape = [jax.ShapeDtypeStruct((S, c), d) for c, d in outs]
    out_shape += [jax.ShapeDtypeStruct((1, c), F32) for c in sums]
    blk = sum(_nbytes((tm, r.shape[1]), r.dtype) for r in rows) + sum(_nbytes((tm, c), d) for c, d in outs)
    blk += 6 * tm * max(r.shape[1] for r in rows) * 4
    res = _pcall(body, name=name, grid=(S // tm,), in_specs=in_specs, out_specs=out_specs, out_shape=out_shape,
                 operands=(*rows, *vecs), vmem=_vmem(blk), carry=carry)
    return tuple(res)


def _ln_norm(z):
    mu = jnp.mean(z, axis=-1, keepdims=True)
    d = z - mu
    var = jnp.mean(d * d, axis=-1, keepdims=True)
    rstd = lax.rsqrt(var + LN_EPS)
    return d * rstd, rstd


def _ln(z, g, b):
    return _ln_norm(z)[0] * g + b


def _ln_bwd(dy, n, rstd, g):
    dn = dy * g
    return rstd * (dn - jnp.mean(dn, axis=-1, keepdims=True) - n * jnp.mean(dn * n, axis=-1, keepdims=True))


def _sigmoid(x):
    return 1.0 / (1.0 + jnp.exp(-x))


def _per_head(x, fn):
    h = x.shape[1] // HEAD_DIM
    return jnp.concatenate([fn(x[:, i * HEAD_DIM:(i + 1) * HEAD_DIM], i) for i in range(h)], axis=1)


def _rot(x, cosf, sinf):
    return _per_head(x, lambda xh, i: xh * cosf + pltpu.roll(xh, HEAD_DIM // 2, 1) * sinf)


def _rot_t(dy, cosf, sinf):
    return _per_head(dy, lambda dh, i: dh * cosf + pltpu.roll(dh * sinf, HEAD_DIM // 2, 1))


def _shift_copies(win):
    rows = win.shape[1] - 8
    for s in range(1, 8):
        win[s, 0:rows, :] = win[0, s:s + rows, :]


def _rows_at(win, start):
    s = start % 8
    return win[s, start - s:start - s + CONV_ROWS, :]


def _conv_fwd(glu, dw, dwb, *, tm=256, tc=512, name="conv_fwd", carry=None):
    S, D = glu.shape
    tm, tc = min(tm, S), min(tc, D)
    ni = S // tm

    def body(cur_ref, prev_ref, dw_ref, dwb_ref, o_ref, win):
        i = pl.program_id(1)
        tail = prev_ref[tm - CONV_PAD:tm, :]
        win[0, 0:CONV_PAD, :] = jnp.where(i > 0, tail, jnp.zeros_like(tail))
        win[0, CONV_PAD:CONV_PAD + tm, :] = cur_ref[...]
        _shift_copies(win)
        first = CONV_PAD - CONV_WIDTH + 1
        for r0 in range(0, tm, CONV_ROWS):
            acc = jnp.zeros((CONV_ROWS, tc), F32) + dwb_ref[...]
            for k in range(CONV_WIDTH):
                acc = acc + _rows_at(win, r0 + first + k) * dw_ref[k:k + 1, :]
            o_ref[r0:r0 + CONV_ROWS, :] = acc

    return _pcall(
        body, name=name, grid=(D // tc, ni),
        in_specs=[pl.BlockSpec((tm, tc), lambda j, i: (i, j)),
                  pl.BlockSpec((tm, tc), lambda j, i: (jnp.maximum(i - 1, 0), j)),
                  pl.BlockSpec((CONV_PAD, tc), lambda j, i: (0, j)),
                  pl.BlockSpec((1, tc), lambda j, i: (0, j))],
        out_specs=[pl.BlockSpec((tm, tc), lambda j, i: (i, j))],
        out_shape=[jax.ShapeDtypeStruct((S, D), F32)],
        scratch_shapes=[pltpu.VMEM((8, tm + CONV_PAD, tc), F32)],
        operands=(glu, glu, dw, dwb), vmem=_vmem(8 * tm * tc * 4), carry=carry)[0]


def _conv_bwd(dc, glu, a_pre, g_pre, dw, ba, bg, *, tm=256, tc=512, name="conv_bwd", carry=None):
    S, D = dc.shape
    tm, tc = min(tm, S), min(tc, D)
    ni = S // tm

    def fold8(v):
        out = v[0:8]
        for r in range(8, CONV_ROWS, 8):
            out = out + v[r:r + 8]
        return out

    def body(dc_ref, dcn_ref, glu_ref, glup_ref, a_ref, g_ref, dw_ref, ba_ref, bg_ref,
             da_ref, dg_ref, ddw_ref, dba_ref, dbg_ref, dwin, gwin, taps):
        i = pl.program_id(1)

        @pl.when(i == 0)
        def _():
            ddw_ref[...] = jnp.zeros_like(ddw_ref)
            dba_ref[...] = jnp.zeros_like(dba_ref)
            dbg_ref[...] = jnp.zeros_like(dbg_ref)

        head = dcn_ref[0:CONV_PAD, :]
        dwin[0, 0:tm, :] = dc_ref[...]
        dwin[0, tm:tm + CONV_PAD, :] = jnp.where(i < ni - 1, head, jnp.zeros_like(head))
        tail = glup_ref[tm - CONV_PAD:tm, :]
        gwin[0, 0:CONV_PAD, :] = jnp.where(i > 0, tail, jnp.zeros_like(tail))
        gwin[0, CONV_PAD:CONV_PAD + tm, :] = glu_ref[...]
        _shift_copies(dwin)
        _shift_copies(gwin)
        taps[...] = jnp.zeros_like(taps)
        first = CONV_PAD - CONV_WIDTH + 1
        sum_a = jnp.zeros((8, tc), F32)
        sum_g = jnp.zeros((8, tc), F32)
        for r0 in range(0, tm, CONV_ROWS):
            dcur = dc_ref[r0:r0 + CONV_ROWS, :]
            dglu = jnp.zeros((CONV_ROWS, tc), F32)
            for k in range(CONV_WIDTH):
                dglu = dglu + _rows_at(dwin, r0 + CONV_WIDTH - 1 - k) * dw_ref[k:k + 1, :]
                taps[k] += fold8(dcur * _rows_at(gwin, r0 + first + k))
            a = a_ref[r0:r0 + CONV_ROWS, :] + ba_ref[...]
            sg = _sigmoid(g_ref[r0:r0 + CONV_ROWS, :] + bg_ref[...])
            da = dglu * sg
            dg = dglu * a * sg * (1.0 - sg)
            da_ref[r0:r0 + CONV_ROWS, :] = da.astype(BF16)
            dg_ref[r0:r0 + CONV_ROWS, :] = dg.astype(BF16)
            sum_a = sum_a + fold8(da)
            sum_g = sum_g + fold8(dg)
        ddw_ref[...] += jnp.sum(taps[...], axis=1)
        dba_ref[...] += jnp.sum(sum_a, axis=0, keepdims=True)
        dbg_ref[...] += jnp.sum(sum_g, axis=0, keepdims=True)

    tile = lambda f: pl.BlockSpec((tm, tc), f)
    vec = pl.BlockSpec((1, tc), lambda j, i: (0, j))
    return _pcall(
        body, name=name, grid=(D // tc, ni),
        in_specs=[tile(lambda j, i: (i, j)), tile(lambda j, i: (jnp.minimum(i + 1, ni - 1), j)),
                  tile(lambda j, i: (i, j)), tile(lambda j, i: (jnp.maximum(i - 1, 0), j)),
                  tile(lambda j, i: (i, j)), tile(lambda j, i: (i, j)),
                  pl.BlockSpec((CONV_PAD, tc), lambda j, i: (0, j)), vec, vec],
        out_specs=[tile(lambda j, i: (i, j)), tile(lambda j, i: (i, j)),
                   pl.BlockSpec((CONV_PAD, tc), lambda j, i: (0, j)), vec, vec],
        out_shape=[jax.ShapeDtypeStruct((S, D), BF16), jax.ShapeDtypeStruct((S, D), BF16),
                   jax.ShapeDtypeStruct((CONV_PAD, D), F32), jax.ShapeDtypeStruct((1, D), F32),
                   jax.ShapeDtypeStruct((1, D), F32)],
        scratch_shapes=[pltpu.VMEM((8, tm + CONV_PAD, tc), F32), pltpu.VMEM((8, tm + CONV_PAD, tc), F32),
                        pltpu.VMEM((CONV_PAD, 8, tc), F32)],
        operands=(dc, dc, glu, glu, a_pre, g_pre, dw, ba, bg), vmem=_vmem(16 * tm * tc * 4), carry=carry)


def _nt(a, b):
    return lax.dot_general(a, b, _DIMS["nt"], preferred_element_type=F32)


def _tn(a, b):
    return lax.dot_general(a, b, _DIMS["tn"], preferred_element_type=F32)


def _window_mask(qi, kj, first_key):
    B = ATTN_BLOCK
    return ((kj < B) & (kj >= qi) & (kj >= first_key)) | ((kj >= B) & (kj - B <= qi))


def _attn_fwd(q_rot, k, v, g, dil):
    S, D = k.shape
    H = D // HEAD_DIM
    L = S // dil
    nb_count = L // ATTN_BLOCK
    B = ATTN_BLOCK

    def body(q_ref, kc_ref, kp_ref, vc_ref, vp_ref, o_ref, lse_ref):
        nb = pl.program_id(1)
        qi = lax.broadcasted_iota(jnp.int32, (B, 2 * B), 0)
        kj = lax.broadcasted_iota(jnp.int32, (B, 2 * B), 1)
        valid = _window_mask(qi, kj, jnp.where(nb > 0, 0, B))
        for h in range(H):
            hs = slice(h * HEAD_DIM, (h + 1) * HEAD_DIM)
            kk = jnp.concatenate([kp_ref[:, hs], kc_ref[:, hs]], axis=0)
            vv = jnp.concatenate([vp_ref[:, hs], vc_ref[:, hs]], axis=0)
            s = jnp.where(valid, _nt(q_ref[:, hs], kk) * ATTN_SCALE, NEG)
            m = jnp.max(s, axis=1, keepdims=True)
            p = jnp.exp(s - m)
            l = jnp.sum(p, axis=1, keepdims=True)
            o = jnp.dot(p.astype(BF16), vv, preferred_element_type=F32) / l
            o_ref[:, hs] = o.astype(o_ref.dtype)
            lse_ref[:, hs] = jnp.broadcast_to(m + jnp.log(l), (B, HEAD_DIM))

    blk = lambda f: pl.BlockSpec((B, D), f)
    cur = lambda r, nb: (nb, r)
    prev = lambda r, nb: (jnp.maximum(nb - 1, 0), r)
    o, lse = pl.pallas_call(
        body, name=f"attn_fwd_g{g}", grid=(dil, nb_count),
        in_specs=[blk(lambda r, nb: (nb, r * N_GROUPS + g)), blk(cur), blk(prev), blk(cur), blk(prev)],
        out_specs=[blk(cur), blk(cur)],
        out_shape=[jax.ShapeDtypeStruct((L, dil * D), BF16), jax.ShapeDtypeStruct((L, dil * D), F32)],
        compiler_params=pltpu.CompilerParams(dimension_semantics=("parallel", "arbitrary"),
                                             vmem_limit_bytes=_vmem(12 * B * D * 4)),
    )(q_rot.reshape(L, dil * N_GROUPS * D), k.reshape(L, dil * D), k.reshape(L, dil * D),
      v.reshape(L, dil * D), v.reshape(L, dil * D))
    return o.reshape(S, D), lse.reshape(S, D)


def _attn_bwd(q_rot, k, v, do, lse, dlt, g, dil, *, name, carry=None):
    S, D = k.shape
    H = D // HEAD_DIM
    L = S // dil
    nb_count = L // ATTN_BLOCK
    B = ATTN_BLOCK

    def body(q_ref, qn_ref, kc_ref, kp_ref, vc_ref, vp_ref, do_ref, don_ref, l_ref, ln_ref, d_ref, dn_ref,
             dq_ref, dk_ref, dv_ref):
        nb = pl.program_id(1)
        qi = lax.broadcasted_iota(jnp.int32, (B, 2 * B), 0)
        kj = lax.broadcasted_iota(jnp.int32, (B, 2 * B), 1)
        valid_q = _window_mask(qi, kj, jnp.where(nb > 0, 0, B))
        qr = lax.broadcasted_iota(jnp.int32, (2 * B, B), 0)
        kc_i = lax.broadcasted_iota(jnp.int32, (2 * B, B), 1)
        q_end = jnp.where(nb < nb_count - 1, 2 * B, B)
        valid_k = ((qr < B) & (kc_i <= qr)) | ((qr >= B) & (qr < q_end) & (kc_i >= qr - B))
        for h in range(H):
            hs = slice(h * HEAD_DIM, (h + 1) * HEAD_DIM)
            q, qn = q_ref[:, hs], qn_ref[:, hs]
            kc, vc = kc_ref[:, hs], vc_ref[:, hs]
            dout, doutn = do_ref[:, hs], don_ref[:, hs]
            lq, lqn = l_ref[:, hs][:, 0:1], ln_ref[:, hs][:, 0:1]
            dq_, dqn_ = d_ref[:, hs][:, 0:1], dn_ref[:, hs][:, 0:1]
            kk = jnp.concatenate([kp_ref[:, hs], kc], axis=0)
            vv = jnp.concatenate([vp_ref[:, hs], vc], axis=0)
            s = jnp.where(valid_q, _nt(q, kk) * ATTN_SCALE, NEG)
            p = jnp.exp(s - lq)
            ds = p * (_nt(dout, vv) - dq_)
            dq_ref[:, hs] = (jnp.dot(ds.astype(BF16), kk, preferred_element_type=F32) * ATTN_SCALE).astype(dq_ref.dtype)
            qq = jnp.concatenate([q, qn], axis=0)
            dd = jnp.concatenate([dout, doutn], axis=0)
            ll = jnp.concatenate([lq, lqn], axis=0)
            dl = jnp.concatenate([dq_, dqn_], axis=0)
            s2 = jnp.where(valid_k, _nt(qq, kc) * ATTN_SCALE, NEG)
            p2 = jnp.exp(s2 - ll)
            dv_ref[:, hs] = _tn(p2.astype(BF16), dd).astype(dv_ref.dtype)
            ds2 = p2 * (_nt(dd, vc) - dl)
            dk_ref[:, hs] = (_tn(ds2.astype(BF16), qq) * ATTN_SCALE).astype(dk_ref.dtype)

    blk = lambda f: pl.BlockSpec((B, D), f)
    cur = lambda r, nb: (nb, r)
    prev = lambda r, nb: (jnp.maximum(nb - 1, 0), r)
    nxt = lambda r, nb: (jnp.minimum(nb + 1, nb_count - 1), r)
    qcur = lambda r, nb: (nb, r * N_GROUPS + g)
    qnxt = lambda r, nb: (jnp.minimum(nb + 1, nb_count - 1), r * N_GROUPS + g)
    qv = q_rot.reshape(L, dil * N_GROUPS * D)
    view = lambda t: t.reshape(L, dil * D)
    dq, dk, dv = _pcall(
        body, name=name, grid=(dil, nb_count),
        in_specs=[blk(qcur), blk(qnxt), blk(cur), blk(prev), blk(cur), blk(prev), blk(cur), blk(nxt),
                  blk(cur), blk(nxt), blk(cur), blk(nxt)],
        out_specs=[blk(cur), blk(cur), blk(cur)],
        out_shape=[jax.ShapeDtypeStruct((L, dil * D), BF16)] * 3,
        operands=(qv, qv, view(k), view(k), view(v), view(v), view(do), view(do), view(lse), view(lse), view(dlt),
                  view(dlt)),
        vmem=_vmem(24 * B * D * 4), carry=carry)
    return dq.reshape(S, D), dk.reshape(S, D), dv.reshape(S, D)


def _mlp_ple_fwd(z1, h1b, p_l, W, vec, l, run, kv_vec=None):
    D = z1.shape[1]
    g1, b1, g2, b2 = vec

    def act_ep(acc):
        return acc, jnp.square(jnp.maximum(acc, 0.0))

    t, act = run(_mm, h1b, W[f"mlp_up{l}"], mode="nn", outs=[BF16, BF16], epilogue=act_ep, name=f"mlp_up{l}")
    def z2_ep(acc, z1_t, g1_, b1_, g2_, b2_):
        z2 = ALPHA * _ln(z1_t, g1_, b1_) + acc
        return z2, _ln(z2, g2_, b2_)

    z2, h2b = run(_mm, act, W[f"mlp_down{l}"], mode="nn", outs=[F32, BF16], extras=[z1], vecs=[g1, b1, g2, b2],
                  epilogue=z2_ep, ep_rows=EPILOGUE_ROWS, name=f"mlp_down{l}")
    pp = run(_mm, p_l, W[f"ple_proj{l}"], mode="nn", outs=[F32], name=f"ple_proj{l}")
    if kv_vec is None:
        gpre = run(_mm, h2b, W[f"ple_gate{l}"], mode="nn", outs=[F32], name=f"ple_gate{l}")
        return t, act, z2, h2b, pp, gpre

    def x1_ep(acc, z2_t, pp_t, g2_, b2_, kg, kb):
        x1, _ = _ple_out(z2_t, pp_t, acc, g2_, b2_)
        return acc, x1, _ln(x1, kg, kb)

    gpre, x1, kvn = run(_mm, h2b, W[f"ple_gate{l}"], mode="nn", outs=[F32, F32, BF16], extras=[z2, pp],
                        vecs=[g2, b2, *kv_vec], epilogue=x1_ep, ep_rows=EPILOGUE_ROWS, tm=256, name=f"ple_gate{l}")
    return t, act, z2, h2b, pp, gpre, x1, kvn


def _mlp_ple_bwd(dy, d_pp, d_gpre, p_l, z1, h1b, t, act, z2, h2b, wts, vec, l, run, produce):
    D = z1.shape[1]
    up, down, pp_w, pg_w = wts
    g1, b1, g2, b2 = vec
    produce(f"ple_proj{l}", run(_mm, p_l, d_pp, mode="tn", outs=[BF16], layout=("col", N_CHIPS), name=f"d_ple_proj{l}"))
    produce(f"ple_gate{l}", run(_mm, h2b, d_gpre, mode="tn", outs=[BF16], layout=("row", N_CHIPS), name=f"d_ple_gate{l}"))
    dh2 = run(_mm, d_gpre, pg_w, mode="nt", outs=[F32], extras=[dy], epilogue=lambda acc, e: (acc + e,),
              name=f"dh2_{l}")

    def ln2_bwd(dh2_t, z2_t, g2_):
        n, rstd = _ln_norm(z2_t)
        dz2 = _ln_bwd(dh2_t, n, rstd, g2_)
        return dz2, dz2, dh2_t * n, dh2_t

    dz2, dz2b, dg2, db2 = _rows(ln2_bwd, [dh2, z2], [g2], [(D, F32), (D, BF16)], [D, D], tm=256,
                                name=f"ln2_bwd{l}")
    produce(f"mlp_down{l}", run(_mm, act, dz2b, mode="tn", outs=[BF16], layout=("row", N_CHIPS), name=f"d_mlp_down{l}"))
    dt = run(_mm, dz2b, down, mode="nt", outs=[BF16], extras=[t],
             epilogue=lambda acc, t_: (acc * 2.0 * jnp.maximum(t_.astype(F32), 0.0),), name=f"dt{l}")
    produce(f"mlp_up{l}", run(_mm, h1b, dt, mode="tn", outs=[BF16], layout=("col", N_CHIPS), name=f"d_mlp_up{l}"))
    dh1 = run(_mm, dt, up, mode="nt", outs=[F32], extras=[dz2], epilogue=lambda acc, e: (acc + ALPHA * e,),
              name=f"dh1_{l}")

    def ln1_bwd(dh1_t, z1_t, g1_):
        n, rstd = _ln_norm(z1_t)
        dz1 = _ln_bwd(dh1_t, n, rstd, g1_)
        return dz1, dz1, dh1_t * n, dh1_t

    dz1, dz1b, dg1, db1 = _rows(ln1_bwd, [dh1, z1], [g1], [(D, F32), (D, BF16)], [D, D], tm=256,
                                name=f"ln1_bwd{l}")
    return dz1, dz1b, (dg1, db1, dg2, db2)


def _ple_out(z2, pp, gpre, g2, b2):
    gt = _sigmoid(gpre)
    return _ln(z2, g2, b2) + pp * gt, gt


_GATHER_AT = {
    "conv_in_a": ("conv_w_out",),
    "conv_in_g": ("ple_gate0", "ple_proj0"),
    "conv_fwd": ("mlp_up0",),
    "mlp_up0": ("mlp_down0",),
    "mlp_down0": ("attn_w_q", "w_kv"),
    "kv_k": ("attn_w_o",),
    "kv_v": ("ple_proj1", "ple_gate1"),
    "attn_q": ("mlp_up1",),
    "mlp_up1": ("mlp_down1",),
}
_GATHER_FIRST = ("conv_w_in",)


def _local_step(x, p, cosf, sinf, target, W, V, shards=None, reducer=None, chip_arr=None):
    S, D = x.shape
    gw, gv = {}, {}
    if shards is not None:
        W = dict(_Gather(_GATHER_FIRST, shards, chip_arr).run_alone("gather_first"))

    def run(fn, *args, name, **kw):
        gather = _Gather(_GATHER_AT[name], shards, chip_arr) if (shards is not None and name in _GATHER_AT) else None
        carry = gather if reducer is None or gather is not None else reducer.carry(name)
        out = fn(*args, name=name, carry=carry, **kw)
        if gather is not None:
            W.update(gather.result)
        elif reducer is not None:
            reducer.carried()
        return out

    a_pre = run(_mm, x, W["conv_w_in"], b_sel=0, mode="nn", outs=[F32], name="conv_in_a")
    g_pre = run(_mm, x, W["conv_w_in"], b_sel=1, mode="nn", outs=[F32], name="conv_in_g")
    (glu,) = _rows(lambda a, g, ba, bg: ((a + ba) * _sigmoid(g + bg),), [a_pre, g_pre], [V["conv_b_a"], V["conv_b_g"]],
                   [(D, F32)], [], tm=256, name="glu_fwd")
    cv = run(_conv_fwd, glu, V["conv_dw"], V["conv_dw_b"], name="conv_fwd")

    def silu_ln(c, g_, b_):
        y = _ln(c, g_, b_)
        return (y * _sigmoid(y),)

    (sb,) = _rows(silu_ln, [cv], [V["conv_ln_g"], V["conv_ln_b"]], [(D, BF16)], [], tm=256, name="conv_ln_fwd")
    def z1_ep(acc, x_t, g_, b_):
        z1 = ALPHA * x_t + acc
        return z1, _ln(z1, g_, b_)

    vec0 = (V["ln1_g0"], V["ln1_b0"], V["ln2_g0"], V["ln2_b0"])
    vec1 = (V["ln1_g1"], V["ln1_b1"], V["ln2_g1"], V["ln2_b1"])
    z1_0, h1b_0 = _mm(sb, W["conv_w_out"], mode="nn", outs=[F32, BF16], extras=[x], vecs=[vec0[0], vec0[1]],
                      epilogue=z1_ep, ep_rows=EPILOGUE_ROWS, name="conv_out")
    t0, act0, z2_0, h2b_0, pp0, gpre0, x1, kvn = _mlp_ple_fwd(z1_0, h1b_0, p[0], W, vec0, 0, run,
                                                              (V["kv_ln_g"], V["kv_ln_b"]))

    rot_ep = lambda acc, c_, s_: (_rot(acc, c_, s_),)
    k_rot = run(_mm, kvn, W["w_kv"], b_sel=0, mode="nn", outs=[BF16], rextras=[cosf, sinf], epilogue=rot_ep, name="kv_k")
    v_b = run(_mm, kvn, W["w_kv"], b_sel=1, mode="nn", outs=[BF16], name="kv_v")
    q_rot = run(_mm, x1, W["attn_w_q"], mode="nn", outs=[BF16], rextras=[cosf, sinf], epilogue=rot_ep, name="attn_q")
    og, lg = [], []
    for g, dil in enumerate(DILATIONS):
        o_g, l_g = _attn_fwd(q_rot, k_rot, v_b, g, dil)
        og.append(o_g)
        lg.append(l_g)

    def merge(o0, o1, o2, l0, l1, l2):
        m = jnp.maximum(jnp.maximum(l0, l1), l2)
        e0, e1, e2 = jnp.exp(l0 - m), jnp.exp(l1 - m), jnp.exp(l2 - m)
        den = e0 + e1 + e2
        o = (e0 * o0.astype(F32) + e1 * o1.astype(F32) + e2 * o2.astype(F32)) / den
        return o, m + jnp.log(den)

    ob, lse = _rows(merge, og + lg, [], [(D, BF16), (D, F32)], [], tm=256, name="attn_merge")
    z1_1, h1b_1 = _mm(ob, W["attn_w_o"], mode="nn", outs=[F32, BF16], extras=[x1], vecs=[vec1[0], vec1[1]],
                      epilogue=z1_ep, ep_rows=EPILOGUE_ROWS, name="attn_out")
    t1, act1, z2_1, h2b_1, pp1, gpre1 = _mlp_ple_fwd(z1_1, h1b_1, p[1], W, vec1, 1, run)
    wts0 = (W["mlp_up0"], W["mlp_down0"], W["ple_proj0"], W["ple_gate0"])
    wts1 = (W["mlp_up1"], W["mlp_down1"], W["ple_proj1"], W["ple_gate1"])

    def head(z2, pp, gpre, tgt, g2, b2):
        y, gt = _ple_out(z2, pp, gpre, g2, b2)
        err = y - tgt
        dy = err * (1.0 / D)
        return dy, dy * gt, dy * pp * gt * (1.0 - gt), 0.5 * err * err * (1.0 / D)

    dy1, d_pp1, d_gpre1, loss_cols = _rows(head, [z2_1, pp1, gpre1, target], [vec1[2], vec1[3]],
                                           [(D, F32), (D, BF16), (D, BF16)], [D], tm=256, name="loss_head")

    def produce(name, grad):
        gw[name] = grad
        if reducer is not None:
            reducer.produced(name, grad)

    dz1_1, dz1b_1, (gv["ln1_g1"], gv["ln1_b1"], gv["ln2_g1"], gv["ln2_b1"]) = _mlp_ple_bwd(
        dy1, d_pp1, d_gpre1, p[1], z1_1, h1b_1, t1, act1, z2_1, h2b_1, wts1, vec1, 1, run, produce)
    produce("attn_w_o", run(_mm, ob, dz1b_1, mode="tn", outs=[BF16], layout=("row", N_CHIPS), name="d_attn_w_o"))

    def do_ep(acc, o_t):
        prod = acc * o_t.astype(F32)
        dlt = _per_head(prod, lambda ph, i: jnp.broadcast_to(jnp.sum(ph, axis=1, keepdims=True), ph.shape))
        return acc, dlt

    do_b, dlt = run(_mm, dz1b_1, W["attn_w_o"], mode="nt", outs=[BF16, F32], extras=[ob], epilogue=do_ep, name="attn_do")
    dqs, dks, dvs = [], [], []
    for g, dil in enumerate(DILATIONS):
        dq_g, dk_g, dv_g = run(_attn_bwd, q_rot, k_rot, v_b, do_b, lse, dlt, g, dil, name=f"attn_bwd_g{g}")
        dqs.append(dq_g)
        dks.append(dk_g)
        dvs.append(dv_g)

    def unrot(q0, q1, q2, k0, k1, k2, v0, v1, v2, c_, s_):
        dq = jnp.concatenate([_rot_t(t_.astype(F32), c_, s_) for t_ in (q0, q1, q2)], axis=1)
        f = lambda t_: t_.astype(F32)
        return dq, _rot_t(f(k0) + f(k1) + f(k2), c_, s_), f(v0) + f(v1) + f(v2)

    dq, dk, dv = run(_rows, unrot, dqs + dks + dvs + [cosf, sinf], [], [(N_GROUPS * D, BF16), (D, BF16), (D, BF16)], [],
                     tm=128, name="attn_unrot")
    produce("attn_w_q", run(_mm, x1, dq, mode="tn", outs=[BF16], layout=("col", N_CHIPS), name="d_attn_w_q"))
    dx1_q = run(_mm, dq, W["attn_w_q"], mode="nt", outs=[F32], extras=[dz1_1],
                epilogue=lambda acc, e: (acc + ALPHA * e,), name="dx1_q")
    produce("w_kv", jnp.concatenate(
        [run(_mm, kvn, dk, mode="tn", outs=[BF16], layout=("col", 2), name="d_w_kv_k"),
         run(_mm, kvn, dv, mode="tn", outs=[BF16], layout=("col", 2), name="d_w_kv_v")], axis=1))
    dkvn_k = run(_mm, dk, W["w_kv"], b_sel=0, mode="nt", outs=[F32], name="dkvn_k")
    dkvn = run(_mm, dv, W["w_kv"], b_sel=1, mode="nt", outs=[F32], extras=[dkvn_k], epilogue=lambda acc, e: (acc + e,),
               name="dkvn_v")

    def x1_bwd(dx1q_t, dkvn_t, x1_t, pp, gpre, kg):
        n, rstd = _ln_norm(x1_t)
        dy = dx1q_t + _ln_bwd(dkvn_t, n, rstd, kg)
        gt = _sigmoid(gpre)
        return dy, dy * gt, dy * pp * gt * (1.0 - gt), dkvn_t * n, dkvn_t

    dy0, d_pp0, d_gpre0, gv["kv_ln_g"], gv["kv_ln_b"] = _rows(
        x1_bwd, [dx1_q, dkvn, x1, pp0, gpre0], [V["kv_ln_g"]], [(D, F32), (D, BF16), (D, BF16)], [D, D], tm=256,
        name="x1_bwd")

    dz1_0, dz1b_0, (gv["ln1_g0"], gv["ln1_b0"], gv["ln2_g0"], gv["ln2_b0"]) = _mlp_ple_bwd(
        dy0, d_pp0, d_gpre0, p[0], z1_0, h1b_0, t0, act0, z2_0, h2b_0, wts0, vec0, 0, run, produce)
    produce("conv_w_out", run(_mm, sb, dz1b_0, mode="tn", outs=[BF16], layout=("row", N_CHIPS), name="d_conv_w_out"))
    ds = run(_mm, dz1b_0, W["conv_w_out"], mode="nt", outs=[F32], name="conv_ds")

    def conv_ln_bwd(ds_t, c_t, g_, b_):
        n, rstd = _ln_norm(c_t)
        y = n * g_ + b_
        sg = _sigmoid(y)
        dln = ds_t * sg * (1.0 + y * (1.0 - sg))
        dc = _ln_bwd(dln, n, rstd, g_)
        return dc, dln * n, dln, dc

    dc, gv["conv_ln_g"], gv["conv_ln_b"], gv["conv_dw_b"] = _rows(
        conv_ln_bwd, [ds, cv], [V["conv_ln_g"], V["conv_ln_b"]], [(D, F32)], [D, D, D], tm=256, name="conv_ln_bwd")
    da, dg, gv["conv_dw"], gv["conv_b_a"], gv["conv_b_g"] = run(
        _conv_bwd, dc, glu, a_pre, g_pre, V["conv_dw"], V["conv_b_a"], V["conv_b_g"], name="conv_bwd")
    produce("conv_w_in", jnp.concatenate(
        [run(_mm, x, da, mode="tn", outs=[BF16], layout=("col", 2), name="d_conv_w_in_a"),
         run(_mm, x, dg, mode="tn", outs=[BF16], layout=("col", 2), name="d_conv_w_in_g")], axis=1))
    dx_a = run(_mm, da, W["conv_w_in"], b_sel=0, mode="nt", outs=[F32], extras=[dz1_0],
               epilogue=lambda acc, e: (acc + ALPHA * e,), name="dx_a")
    grad_x = run(_mm, dg, W["conv_w_in"], b_sel=1, mode="nt", outs=[F32], extras=[dx_a],
                 epilogue=lambda acc, e: (acc + e,), name="dx_g")
    if reducer is not None:
        reducer.carry("share_last").run_alone("share_last")
        reducer.carried()
    return loss_cols, grad_x, gw, gv


def _place():
    x, y, c = lax.axis_index("x"), lax.axis_index("y"), lax.axis_index("c")
    chips = [(1 - x, y), (x, 1 - y), (1 - x, 1 - y)]
    return x, y, c, chips


def _remote(src, dst, ssem, rsem, dev):
    return pltpu.make_async_remote_copy(src_ref=src, dst_ref=dst, send_sem=ssem, recv_sem=rsem, device_id=dev,
                                        device_id_type=MESH)


def _allgather8(block, name):
    R, C = block.shape

    def body(x_ref, out_ref, send_sems, recv_sems, local_sem):
        x, y, c, chips = _place()
        me, sibling = (x, y, c), (x, y, 1 - c)

        def slot(px, py, pc):
            return out_ref.at[4 * px + 2 * py + pc]

        def copy(k, blockpos, to, src=None):
            return _remote(slot(*blockpos) if src is None else src, slot(*blockpos), send_sems.at[k], recv_sems.at[k], to)

        mine = pltpu.make_async_copy(x_ref, slot(*me), local_sem)
        mine.start()
        first = [copy(0, me, sibling, src=x_ref)]
        first += [copy(1 + j, me, (*chip, c), src=x_ref) for j, chip in enumerate(chips)]
        for cp in first:
            cp.start()
        passed = [copy(4 + j, (*chip, c), sibling) for j, chip in enumerate(chips)]
        for j, chip in enumerate(chips):
            copy(1 + j, (*chip, c), me).wait_recv()
            passed[j].start()
        copy(0, sibling, me).wait_recv()
        for j, chip in enumerate(chips):
            copy(4 + j, (*chip, 1 - c), me).wait_recv()
        for cp in first + passed:
            cp.wait_send()
        mine.wait()

    return pl.pallas_call(
        body, name=name, out_shape=jax.ShapeDtypeStruct((8, R, C), block.dtype),
        in_specs=[pl.BlockSpec(memory_space=pltpu.VMEM)], out_specs=pl.BlockSpec(memory_space=pltpu.VMEM),
        scratch_shapes=[pltpu.SemaphoreType.DMA((7,)), pltpu.SemaphoreType.DMA((7,)), pltpu.SemaphoreType.DMA],
        compiler_params=pltpu.CompilerParams(vmem_limit_bytes=_vmem(10 * _nbytes((R, C), block.dtype))),
    )(block)


_MATS = (
    ("conv_w_in", "conv_w_in", 0, "col", True),
    ("conv_w_out", "conv_w_out", 0, "row", False),
    ("mlp_up0", "mlp_up", 0, "col", False),
    ("mlp_down0", "mlp_down", 0, "row", False),
    ("ple_proj0", "ple_proj", 0, "col", False),
    ("ple_gate0", "ple_gate", 0, "row", False),
    ("w_kv", "w_kv", None, "col", True),
    ("attn_w_q", "attn_w_q", 0, "col", False),
    ("attn_w_o", "attn_w_o", 0, "row", False),
    ("mlp_up1", "mlp_up", 1, "col", False),
    ("mlp_down1", "mlp_down", 1, "row", False),
    ("ple_proj1", "ple_proj", 1, "col", False),
    ("ple_gate1", "ple_gate", 1, "row", False),
)


class _Carry:
    result = None
    aliases = {}

    def set_result(self, outs):
        self.result = dict(zip(self.names, outs))

    def run_alone(self, name):
        n_in, n_out = len(self.ins), len(self.out_shape)

        def body(*refs):
            in_refs, out_refs, sems = refs[:n_in], refs[n_in:n_in + n_out], refs[n_in + n_out:]
            self.start(in_refs, out_refs, sems)
            self.finish(in_refs, out_refs, sems)

        outs = pl.pallas_call(body, name=name, out_shape=self.out_shape, in_specs=[ANY] * n_in, out_specs=[ANY] * n_out,
                              scratch_shapes=self.scratch, input_output_aliases=dict(self.aliases))(*self.ins)
        self.set_result(outs)
        return self.result


class _Gather(_Carry):
    def __init__(self, names, shards, chip_arr):
        mats = [m for m in _MATS if m[0] in names]
        srcs = sorted({m[1] for m in mats})
        self.names = [m[0] for m in mats]
        self.out_shape, self.geo, placed = [], [], []
        for name, src, layer, kind, split in mats:
            s = shards[src]
            ks, ns = s.shape[-2:]
            K, N = (ks, ns * N_CHIPS) if kind == "col" else (ks * N_CHIPS, ns)
            self.out_shape.append(jax.ShapeDtypeStruct((2, K, N // 2) if split else (K, N), BF16))
            self.geo.append((srcs.index(src), layer if s.ndim == 3 else None, kind, split, K, N))
            placed.append(_place_shard(s, layer if s.ndim == 3 else None, kind, split, chip_arr, f"place_{name}"))
        T = len(mats)
        self.ins = [shards[n] for n in srcs] + placed
        self.aliases = {len(srcs) + t: t for t in range(T)}
        self.scratch = [pltpu.SemaphoreType.DMA((3 * T,)) for _ in range(4)]
        self.result = None

    def _copies(self, in_refs, out_refs, sems):
        geo, T = self.geo, len(self.geo)
        s_ici, r_ici, s_d2d, r_d2d = sems
        x, y, c, chips = _place()
        me = 2 * x + y
        sibling = (x, y, 1 - c)
        idx = [2 * cx + cy for cx, cy in chips]

        def src_ref(t):
            i, layer, _, _, _, _ = geo[t]
            return in_refs[i] if layer is None else in_refs[i].at[layer]

        def src_half(t, h):
            _, _, kind, _, K, N = geo[t]
            if kind == "col":
                return src_ref(t).at[pl.ds(h * (K // 2), K // 2), :]
            return src_ref(t).at[:, pl.ds(h * (N // 2), N // 2)]

        def dst(t, j, h):
            _, _, kind, split, K, N = geo[t]
            n, k = N // N_CHIPS, K // N_CHIPS
            if kind == "col":
                rows = slice(None) if h is None else pl.ds(h * (K // 2), K // 2)
                if split:
                    return out_refs[t].at[j // 2, rows, pl.ds((j % 2) * n, n)]
                return out_refs[t].at[rows, pl.ds(j * n, n)]
            cols = slice(None) if h is None else pl.ds(h * (N // 2), N // 2)
            return out_refs[t].at[pl.ds(j * k, k), cols]

        sends = [_remote(src_half(t, c), dst(t, me, c), s_ici.at[3 * t + kk], r_ici.at[3 * t + kk], (*chips[kk], c))
                 for t in range(T) for kk in range(3)]
        hops = []
        for t in range(T):
            for kk in range(3):
                mine, theirs = dst(t, idx[kk], c), dst(t, idx[kk], 1 - c)
                hops.append((_remote(mine, mine, s_ici.at[3 * t + kk], r_ici.at[3 * t + kk], sibling),
                             _remote(mine, mine, s_d2d.at[3 * t + kk], r_d2d.at[3 * t + kk], sibling),
                             _remote(theirs, theirs, s_d2d.at[3 * t + kk], r_d2d.at[3 * t + kk], sibling)))
        return sends, hops

    def start(self, in_refs, out_refs, sems):
        for cp in self._copies(in_refs, out_refs, sems)[0]:
            cp.start()

    def finish(self, in_refs, out_refs, sems):
        sends, hops = self._copies(in_refs, out_refs, sems)
        for landed, forward, _ in hops:
            landed.wait_recv()
            forward.start()
        for _, _, from_sibling in hops:
            from_sibling.wait_recv()
        for cp in sends + [h[1] for h in hops]:
            cp.wait_send()


def _place_shard(shard, layer, kind, split, chip_arr, name):
    ks, ns = shard.shape[-2:]
    K, N = (ks, ns * N_CHIPS) if kind == "col" else (ks * N_CHIPS, ns)
    tr = _fit(256, ks)
    nb = ks // tr
    if shard.ndim == 3:
        in_spec = pl.BlockSpec((None, tr, ns), lambda i, me: (layer, i, 0))
    else:
        in_spec = pl.BlockSpec((tr, ns), lambda i, me: (i, 0))
    if kind == "row":
        out_shape, out_spec = (K, N), pl.BlockSpec((tr, ns), lambda i, me: (me[0] * nb + i, 0))
    elif split:
        out_shape, out_spec = (2, K, N // 2), pl.BlockSpec((None, tr, ns), lambda i, me: (me[0] // 2, i, me[0] % 2))
    else:
        out_shape, out_spec = (K, N), pl.BlockSpec((tr, ns), lambda i, me: (i, me[0]))

    def body(me_ref, s_ref, o_ref):
        o_ref[...] = s_ref[...]

    return pl.pallas_call(
        body, name=name, out_shape=jax.ShapeDtypeStruct(out_shape, BF16),
        grid_spec=pltpu.PrefetchScalarGridSpec(num_scalar_prefetch=1, grid=(nb,), in_specs=[in_spec], out_specs=out_spec),
        compiler_params=pltpu.CompilerParams(dimension_semantics=("parallel",), vmem_limit_bytes=_vmem(4 * tr * ns * 2)),
    )(chip_arr, shard)


class _Multi(_Carry):
    def __init__(self, parts):
        self.parts = parts
        self.ins = [a for p in parts for a in p.ins]
        self.out_shape = [a for p in parts for a in p.out_shape]
        self.scratch = [a for p in parts for a in p.scratch]
        self.aliases, n_in, n_out = {}, 0, 0
        for p in parts:
            self.aliases.update({n_in + i: n_out + o for i, o in p.aliases.items()})
            n_in, n_out = n_in + len(p.ins), n_out + len(p.out_shape)

    def _split(self, seq, field):
        out, at = [], 0
        for p in self.parts:
            n = len(getattr(p, field))
            out.append(seq[at:at + n])
            at += n
        return out

    def _each(self, method, in_refs, out_refs, sems):
        for p, i, o, s in zip(self.parts, self._split(in_refs, "ins"), self._split(out_refs, "out_shape"),
                              self._split(sems, "scratch")):
            getattr(p, method)(i, o, s)

    def start(self, in_refs, out_refs, sems):
        self._each("start", in_refs, out_refs, sems)

    def finish(self, in_refs, out_refs, sems):
        self._each("finish", in_refs, out_refs, sems)

    def set_result(self, outs):
        for p, o in zip(self.parts, self._split(list(outs), "out_shape")):
            p.set_result(o)


class _PairSend(_Carry):
    def __init__(self, grads):
        self.names = list(grads)
        self.ins = [grads[n] for n in self.names]
        self.out_shape = [jax.ShapeDtypeStruct(a.shape[1:], BF16) for a in self.ins]
        T = len(self.names)
        self.scratch = [pltpu.SemaphoreType.DMA((T,)), pltpu.SemaphoreType.DMA((T,))]

    def _copies(self, in_refs, out_refs, sems):
        x, y, c, _ = _place()
        return [_remote(in_refs[t].at[1 - c], out_refs[t], sems[0].at[t], sems[1].at[t], (x, y, 1 - c))
                for t in range(len(self.names))]

    def start(self, in_refs, out_refs, sems):
        for cp in self._copies(in_refs, out_refs, sems):
            cp.start()

    def finish(self, in_refs, out_refs, sems):
        for cp in self._copies(in_refs, out_refs, sems):
            cp.wait()


class _ChipScatter(_Carry):
    def __init__(self, sums):
        self.names = list(sums)
        T = len(self.names)
        self.ins = [sums[n][0] for n in self.names] + [sums[n][1] for n in self.names]
        self.out_shape = [jax.ShapeDtypeStruct(a.shape, BF16) for a in self.ins[:T]]
        self.aliases = {T + t: t for t in range(T)}
        self.scratch = [pltpu.SemaphoreType.DMA((3 * T,)), pltpu.SemaphoreType.DMA((3 * T,))]

    def _copies(self, in_refs, out_refs, sems):
        ssem, rsem = sems
        x, y, c, chips = _place()
        me = 2 * x + y
        idx = [2 * cx + cy for cx, cy in chips]
        T = len(self.names)
        sends = [_remote(in_refs[t].at[idx[kk]], out_refs[t].at[me], ssem.at[3 * t + kk], rsem.at[3 * t + kk],
                         (*chips[kk], c)) for t in range(T) for kk in range(3)]
        lands = [_remote(out_refs[t].at[idx[kk]], out_refs[t].at[idx[kk]], ssem.at[3 * t + kk], rsem.at[3 * t + kk],
                         (*chips[kk], c)) for t in range(T) for kk in range(3)]
        return sends, lands

    def start(self, in_refs, out_refs, sems):
        for cp in self._copies(in_refs, out_refs, sems)[0]:
            cp.start()

    def finish(self, in_refs, out_refs, sems):
        sends, lands = self._copies(in_refs, out_refs, sems)
        for cp in lands:
            cp.wait_recv()
        for cp in sends:
            cp.wait_send()


class _PairShare(_Carry):
    def __init__(self, halves):
        self.names = list(halves)
        self.ins = [halves[n] for n in self.names]
        self.out_shape = [jax.ShapeDtypeStruct(a.shape, F32) for a in self.ins]
        T = len(self.names)
        self.aliases = {t: t for t in range(T)}
        self.scratch = [pltpu.SemaphoreType.DMA((T,)), pltpu.SemaphoreType.DMA((T,))]

    def _copies(self, in_refs, out_refs, sems):
        ssem, rsem = sems
        x, y, c, _ = _place()
        sibling = (x, y, 1 - c)
        T = len(self.names)
        sends = [_remote(out_refs[t].at[c], out_refs[t].at[c], ssem.at[t], rsem.at[t], sibling) for t in range(T)]
        lands = [_remote(out_refs[t].at[1 - c], out_refs[t].at[1 - c], ssem.at[t], rsem.at[t], sibling) for t in range(T)]
        return sends, lands

    def start(self, in_refs, out_refs, sems):
        for cp in self._copies(in_refs, out_refs, sems)[0]:
            cp.start()

    def finish(self, in_refs, out_refs, sems):
        sends, lands = self._copies(in_refs, out_refs, sems)
        for cp in lands:
            cp.wait_recv()
        for cp in sends:
            cp.wait_send()


def _pair_sum(own, landed, c_arr, name):
    _, ns, r, cc = own.shape
    rows = ns * r
    tr = _fit(512, rows)

    def body(c_ref, a_ref, b_ref, o_ref, o2_ref):
        total = (a_ref[...].astype(F32) + b_ref[...].astype(F32)).astype(o_ref.dtype)
        o_ref[...] = total
        o2_ref[...] = total

    tile = pl.BlockSpec((tr, cc), lambda i, c_ref: (i, 0))
    out = pl.pallas_call(
        body, name=name, out_shape=[jax.ShapeDtypeStruct((rows, cc), BF16)] * 2,
        grid_spec=pltpu.PrefetchScalarGridSpec(
            num_scalar_prefetch=1, grid=(rows // tr,),
            in_specs=[pl.BlockSpec((None, tr, cc), lambda i, c_ref: (c_ref[0], i, 0)), tile], out_specs=[tile, tile]),
        compiler_params=pltpu.CompilerParams(dimension_semantics=("parallel",), vmem_limit_bytes=_vmem(8 * tr * cc * 4)),
    )(c_arr, own.reshape(2, rows, cc), landed.reshape(rows, cc))
    return out[0].reshape(ns, r, cc), out[1].reshape(ns, r, cc)


def _chip_sum(parts, c_arr, name):
    _, r, cc = parts.shape
    tr = _fit(256, r)

    def body(c_ref, p_ref, o_ref):
        acc = p_ref[0].astype(F32)
        for j in range(1, N_CHIPS):
            acc = acc + p_ref[j].astype(F32)
        o_ref[...] = acc

    return pl.pallas_call(
        body, name=name, out_shape=jax.ShapeDtypeStruct((2, r, cc), F32),
        grid_spec=pltpu.PrefetchScalarGridSpec(
            num_scalar_prefetch=1, grid=(r // tr,),
            in_specs=[pl.BlockSpec((N_CHIPS, tr, cc), lambda i, c_ref: (0, i, 0))],
            out_specs=pl.BlockSpec((None, tr, cc), lambda i, c_ref: (c_ref[0], i, 0))),
        compiler_params=pltpu.CompilerParams(dimension_semantics=("parallel",), vmem_limit_bytes=_vmem(12 * tr * cc * 4)),
    )(c_arr, parts)


def _adamw_math(w, g, m, v):
    m2 = ADAM_B1 * m + (1.0 - ADAM_B1) * g
    v2 = ADAM_B2 * v + (1.0 - ADAM_B2) * jnp.square(g)
    m_hat = m2 / (1.0 - ADAM_B1 ** ADAM_STEP)
    v_hat = v2 / (1.0 - ADAM_B2 ** ADAM_STEP)
    delta = -ADAM_LR * (m_hat / (jnp.sqrt(v_hat) + ADAM_EPS) + ADAM_WD * w)
    return delta, m2, v2


def _adamw_mat(g2, w, m, v, layer, kind, prev, name):
    shape = w.shape
    ks, ns = shape[-2:]
    _, r, cc = g2.shape
    tr, tc = _fit(256, r), _fit(1024, cc)
    assert (r, cc) == ((ks // 2, ns) if kind == "col" else (ks, ns // 2))
    assert r % tr == 0 and cc % tc == 0
    rb, cb = r // tr, cc // tc
    if kind == "col":
        g_spec = pl.BlockSpec((None, tr, tc), lambda i, j: (i // rb, i % rb, j))
    else:
        g_spec = pl.BlockSpec((None, tr, tc), lambda i, j: (j // cb, i, j % cb))
    if w.ndim == 3:
        w_spec = pl.BlockSpec((None, tr, tc), lambda i, j: (layer, i, j))
    else:
        w_spec = pl.BlockSpec((tr, tc), lambda i, j: (i, j))
    n_prev = 0 if prev is None else 4

    def body(*refs):
        g_ref, w_ref, m_ref, v_ref = refs[:4]
        go_ref, d_ref, mo_ref, vo_ref = refs[4 + n_prev:]
        g = g_ref[...]
        delta, m2, v2 = _adamw_math(w_ref[...], g, m_ref[...], v_ref[...])
        go_ref[...] = g
        d_ref[...] = delta
        mo_ref[...] = m2
        vo_ref[...] = v2

    return pl.pallas_call(
        body, name=name, grid=(ks // tr, ns // tc),
        in_specs=[g_spec, w_spec, w_spec, w_spec] + [ANY] * n_prev, out_specs=[w_spec] * 4,
        out_shape=[jax.ShapeDtypeStruct(shape, F32)] * 4,
        input_output_aliases={4 + i: i for i in range(n_prev)},
        compiler_params=pltpu.CompilerParams(dimension_semantics=("parallel", "parallel"),
                                             vmem_limit_bytes=_vmem(16 * tr * tc * 4)),
    )(g2, w, m, v, *(prev or ()))


def _adamw_small(g, w, m, v, name):
    def body(g_ref, w_ref, m_ref, v_ref, d_ref, mo_ref, vo_ref):
        delta, m2, v2 = _adamw_math(w_ref[...], g_ref[...], m_ref[...], v_ref[...])
        d_ref[...] = delta
        mo_ref[...] = m2
        vo_ref[...] = v2

    return pl.pallas_call(body, name=name, out_shape=[jax.ShapeDtypeStruct(w.shape, F32)] * 3)(g, w, m, v)


def _sum8(parts, name):
    def body(p_ref, o_ref):
        acc = p_ref[0]
        for j in range(1, 8):
            acc = acc + p_ref[j]
        o_ref[...] = acc

    return pl.pallas_call(body, name=name, out_shape=jax.ShapeDtypeStruct(parts.shape[1:], F32),
                          compiler_params=pltpu.CompilerParams(vmem_limit_bytes=_vmem(12 * _nbytes(parts.shape[1:], F32))))(parts)


_REDUCE_AT = {
    "d_ple_gate1": (("A", "ple_proj1"),),
    "dh2_1": (("A", "ple_gate1"),),
    "d_mlp_down1": (("B", "ple_proj1"), ("B", "ple_gate1")),
    "dt1": (("A", "mlp_down1"),),
    "d_mlp_up1": (("B", "mlp_down1"), ("C", "ple_proj1"), ("C", "ple_gate1")),
    "dh1_1": (("A", "mlp_up1"),),
    "d_attn_w_o": (("C", "mlp_down1"),),
    "attn_do": (("A", "attn_w_o"),),
    "attn_bwd_g0": (("B", "mlp_up1"),),
    "attn_bwd_g1": (("B", "attn_w_o"),),
    "attn_unrot": (("C", "mlp_up1"), ("C", "attn_w_o")),
    "dx1_q": (("A", "attn_w_q"),),
    "dkvn_k": (("A", "w_kv"),),
    "d_ple_gate0": (("A", "ple_proj0"),),
    "dh2_0": (("A", "ple_gate0"),),
    "d_mlp_down0": (("B", "attn_w_q"), ("B", "ple_proj0")),
    "dt0": (("B", "w_kv"), ("B", "ple_gate0"), ("A", "mlp_down0")),
    "d_mlp_up0": (("B", "mlp_down0"), ("C", "attn_w_q"), ("C", "ple_proj0"), ("C", "w_kv"), ("C", "ple_gate0")),
    "dh1_0": (("A", "mlp_up0"),),
    "d_conv_w_out": (("C", "mlp_down0"),),
    "conv_ds": (("A", "conv_w_out"),),
    "conv_bwd": (("B", "mlp_up0"), ("B", "conv_w_out")),
    "d_conv_w_in_g": (("C", "mlp_up0"), ("C", "conv_w_out")),
    "dx_a": (("A", "conv_w_in"),),
    "dx_g": (("B", "conv_w_in"),),
    "share_last": (("C", "conv_w_in"),),
}


class _Reducer:
    def __init__(self, w, mom, var, c_arr):
        self.w, self.mom, self.var, self.c_arr = w, mom, var, c_arr
        self.mats = {m[0]: m for m in _MATS}
        self.grads, self.pair_sums, self.chip_sums, self.out = {}, {}, {}, {}

    def produced(self, name, grad):
        self.grads[name] = grad

    def carry(self, call):
        parts = []
        for cls, stage, src in ((_PairSend, "A", self.grads), (_ChipScatter, "B", self.pair_sums),
                                (_PairShare, "C", self.chip_sums)):
            names = [n for s, n in _REDUCE_AT.get(call, ()) if s == stage]
            if names:
                parts.append((stage, cls({n: src[n] for n in names})))
        self._parts = parts
        return _Multi([p for _, p in parts]) if parts else None

    def carried(self):
        for stage, part in self._parts:
            for name, val in part.result.items():
                if stage == "A":
                    self.pair_sums[name] = _pair_sum(self.grads[name], val, self.c_arr, f"pair_sum_{name}")
                elif stage == "B":
                    self.chip_sums[name] = _chip_sum(val, self.c_arr, f"chip_sum_{name}")
                else:
                    _, src, layer, kind, _ = self.mats[name]
                    self.out[src] = _adamw_mat(val, self.w[src], self.mom[src], self.var[src], layer or 0, kind,
                                               self.out.get(src), f"adamw_{name}")
        self._parts = []


_WEIGHTS = ("conv_w_in", "conv_b_in", "conv_dw", "conv_dw_b", "conv_ln_g", "conv_ln_b", "conv_w_out", "kv_ln_g",
            "kv_ln_b", "w_kv", "attn_w_q", "attn_w_o", "ln1_g", "ln1_b", "mlp_up", "mlp_down", "ln2_g", "ln2_b",
            "ple_proj", "ple_gate")
_SHARDED_VECS = ("conv_b_in", "conv_dw", "conv_dw_b", "conv_ln_g", "conv_ln_b")
_REPLICATED_VECS = ("kv_ln_g", "kv_ln_b", "ln1_g", "ln1_b", "ln2_g", "ln2_b")


def _pad_rows(a, rows):
    return jnp.concatenate([a, jnp.zeros((rows - a.shape[0], a.shape[1]), a.dtype)], axis=0) if a.shape[0] < rows else a


def _pack_sharded(d):
    n = d["conv_dw_b"].shape[-1]
    rows = [d["conv_b_in"].reshape(2, n), d["conv_dw"].reshape(CONV_WIDTH, n), d["conv_dw_b"].reshape(1, n),
            d["conv_ln_g"].reshape(1, n), d["conv_ln_b"].reshape(1, n)]
    return _pad_rows(jnp.concatenate(rows, axis=0), 40)


def _unpack_sharded(pack, like):
    n = pack.shape[1]
    return {"conv_b_in": pack[0:2].reshape(like["conv_b_in"].shape),
            "conv_dw": pack[2:2 + CONV_WIDTH].reshape(like["conv_dw"].shape),
            "conv_dw_b": pack[33:34].reshape(like["conv_dw_b"].shape),
            "conv_ln_g": pack[34:35].reshape(like["conv_ln_g"].shape),
            "conv_ln_b": pack[35:36].reshape(like["conv_ln_b"].shape)}


def _pack_replicated(d):
    D = d["kv_ln_g"].shape[-1]
    rows = [d[n].reshape(-1, D) for n in _REPLICATED_VECS]
    return _pad_rows(jnp.concatenate(rows, axis=0), 16)


def _unpack_replicated(pack, like):
    out, r = {}, 0
    for n in _REPLICATED_VECS:
        k = like[n].size // pack.shape[1]
        out[n] = pack[r:r + k].reshape(like[n].shape)
        r += k
    return out


def kernel(x, p, positions, conv_w_in, conv_b_in, conv_dw, conv_dw_b, conv_ln_g, conv_ln_b, conv_w_out, kv_ln_g, kv_ln_b, w_kv, attn_w_q, attn_w_o, ln1_g, ln1_b, mlp_up, mlp_down, ln2_g, ln2_b, ple_proj, ple_gate, loss_target, m_conv_w_in, m_conv_b_in, m_conv_dw, m_conv_dw_b, m_conv_ln_g, m_conv_ln_b, m_conv_w_out, m_kv_ln_g, m_kv_ln_b, m_w_kv, m_attn_w_q, m_attn_w_o, m_ln1_g, m_ln1_b, m_mlp_up, m_mlp_down, m_ln2_g, m_ln2_b, m_ple_proj, m_ple_gate, v_conv_w_in, v_conv_b_in, v_conv_dw, v_conv_dw_b, v_conv_ln_g, v_conv_ln_b, v_conv_w_out, v_kv_ln_g, v_kv_ln_b, v_w_kv, v_attn_w_q, v_attn_w_o, v_ln1_g, v_ln1_b, v_mlp_up, v_mlp_down, v_ln2_g, v_ln2_b, v_ple_proj, v_ple_gate):
    args = dict(locals())
    w = {n: args[n] for n in _WEIGHTS}
    mom = {n: args["m_" + n] for n in _WEIGHTS}
    var = {n: args["v_" + n] for n in _WEIGHTS}
    S, D = x.shape[1:]
    n4 = D // N_CHIPS
    chip = 2 * lax.axis_index("x") + lax.axis_index("y")
    c_arr = lax.axis_index("c").astype(jnp.int32).reshape(1)

    shards = {n: w[n].astype(BF16) for n in sorted({m[1] for m in _MATS})}
    vec_all = _allgather8(_pack_sharded(w), "gather_vectors")
    vec_full = jnp.concatenate([vec_all[2 * j] for j in range(N_CHIPS)], axis=1)
    b_in = vec_all[0::2, 0:2, :].reshape(1, 2 * D)
    V = {"conv_b_a": b_in[:, :D], "conv_b_g": b_in[:, D:],
         "conv_dw": _pad_rows(vec_full[2:2 + CONV_WIDTH], CONV_PAD), "conv_dw_b": vec_full[33:34],
         "conv_ln_g": vec_full[34:35], "conv_ln_b": vec_full[35:36],
         "kv_ln_g": kv_ln_g.reshape(1, D), "kv_ln_b": kv_ln_b.reshape(1, D)}
    for l in range(2):
        for n in ("ln1_g", "ln1_b", "ln2_g", "ln2_b"):
            V[f"{n}{l}"] = w[n][l].reshape(1, D)

    half = HEAD_DIM // 2
    inv_freq = ROPE_THETA ** (-jnp.arange(half, dtype=F32) * (2.0 / HEAD_DIM))
    ang = positions[0].astype(F32)[:, None] * inv_freq
    cos, sin = jnp.cos(ang), jnp.sin(ang)
    cosf = jnp.concatenate([cos, cos], axis=-1)
    sinf = jnp.concatenate([-sin, sin], axis=-1)

    reducer = _Reducer(w, mom, var, c_arr)
    loss_cols, grad_x, _, gv = _local_step(x[0], p[:, 0], cosf, sinf, loss_target[0], None, V, shards, reducer,
                                           chip.astype(jnp.int32).reshape(1))
    loss = lax.psum(jnp.sum(loss_cols), ("x", "y", "c"))
    out = dict(reducer.out)

    gpack = jnp.concatenate([gv["conv_b_a"], gv["conv_b_g"], gv["conv_dw"][:CONV_WIDTH], gv["conv_dw_b"],
                             gv["conv_ln_g"], gv["conv_ln_b"], gv["kv_ln_g"], gv["kv_ln_b"],
                             gv["ln1_g0"], gv["ln1_g1"], gv["ln1_b0"], gv["ln1_b1"],
                             gv["ln2_g0"], gv["ln2_g1"], gv["ln2_b0"], gv["ln2_b1"]], axis=0)
    gsum = _sum8(_allgather8(_pad_rows(gpack, 48), "gather_vector_grads"), "sum_vector_grads")
    g_b = lax.dynamic_slice_in_dim(jnp.concatenate([gsum[0:1], gsum[1:2]], axis=1), chip * 2 * n4, 2 * n4, axis=1)
    g_sh = lax.dynamic_slice_in_dim(gsum[2:36], chip * n4, n4, axis=1)
    g_sh = _pad_rows(jnp.concatenate([g_b.reshape(2, n4), g_sh], axis=0), 40)
    d_sh, m_sh, v_sh = _adamw_small(g_sh, _pack_sharded(w), _pack_sharded(mom), _pack_sharded(var), "adamw_sharded_vectors")
    g_rep = _pad_rows(gsum[36:46], 16)
    d_rep, m_rep, v_rep = _adamw_small(g_rep, _pack_replicated(w), _pack_replicated(mom), _pack_replicated(var),
                                       "adamw_replicated_vectors")
    small = {}
    for i, (sh, rep) in enumerate(((g_sh, g_rep), (d_sh, d_rep), (m_sh, m_rep), (v_sh, v_rep))):
        d = {**_unpack_sharded(sh, w), **_unpack_replicated(rep, w)}
        for n, val in d.items():
            small.setdefault(n, [None] * 4)[i] = val
    for n in small:
        out[n] = small[n]

    res = [loss, grad_x[None]]
    for i in range(4):
        res += [out[n][i] for n in _WEIGHTS]
    return tuple(res)
```

```python
import functools

import jax
import jax.numpy as jnp
from jax import lax
from jax.experimental import pallas as pl
from jax.experimental.pallas import tpu as pltpu

F32 = jnp.float32
BF16 = jnp.bfloat16

HEAD_DIM = 128
ATTN_BLOCK = 128
DILATIONS = (1, 4, 16)
N_GROUPS = 3
CONV_WIDTH = 31
CONV_PAD = 32
CONV_ROWS = 32
EPILOGUE_ROWS = 128
ROPE_THETA = 10000.0
LN_EPS = 1e-5
ALPHA = 4.0 ** 0.25
ATTN_SCALE = HEAD_DIM ** -0.5
NEG = -1e30

ADAM_LR = 0.001
ADAM_B1 = 0.9
ADAM_B2 = 0.999
ADAM_EPS = 1e-08
ADAM_WD = 0.01
ADAM_STEP = 10

N_CHIPS = 4
VMEM_CAP = 60 << 20
MESH = pl.DeviceIdType.MESH
ANY = pl.BlockSpec(memory_space=pl.ANY)


def _vmem(nbytes):
    return int(min(max(2 * nbytes + (8 << 20), 24 << 20), VMEM_CAP))


def _fit(tile, n):
    if n <= tile:
        return n
    t = tile - tile % 128
    while n % t:
        t -= 128
    return t


def _nbytes(shape, dtype):
    n = 1
    for s in shape:
        n *= s
    return n * jnp.dtype(dtype).itemsize


_DIMS = {"nn": (((1,), (0,)), ((), ())), "nt": (((1,), (1,)), ((), ())), "tn": (((0,), (0,)), ((), ()))}


def _pcall(body, *, name, grid, in_specs, out_specs, out_shape, operands, scratch_shapes=(), vmem, carry=None):
    if carry is None:
        return pl.pallas_call(
            body, name=name, grid=grid, in_specs=in_specs, out_specs=out_specs, out_shape=out_shape,
            scratch_shapes=list(scratch_shapes),
            compiler_params=pltpu.CompilerParams(dimension_semantics=("arbitrary",) * len(grid), vmem_limit_bytes=vmem),
        )(*operands)
    n_in, n_out, n_scr = len(in_specs), len(out_specs), len(scratch_shapes)
    c_in, c_out = len(carry.ins), len(carry.out_shape)

    def wrapped(*refs):
        ins, refs = refs[:n_in], refs[n_in:]
        c_ins, refs = refs[:c_in], refs[c_in:]
        outs, refs = refs[:n_out], refs[n_out:]
        c_outs, refs = refs[:c_out], refs[c_out:]
        scr, c_sems = refs[:n_scr], refs[n_scr:]
        first = functools.reduce(jnp.logical_and, [pl.program_id(d) == 0 for d in range(len(grid))])
        last = functools.reduce(jnp.logical_and, [pl.program_id(d) == grid[d] - 1 for d in range(len(grid))])
        pl.when(first)(lambda: carry.start(c_ins, c_outs, c_sems))
        body(*ins, *outs, *scr)
        pl.when(last)(lambda: carry.finish(c_ins, c_outs, c_sems))

    res = pl.pallas_call(
        wrapped, name=name, grid=grid, in_specs=list(in_specs) + [ANY] * c_in, out_specs=list(out_specs) + [ANY] * c_out,
        out_shape=list(out_shape) + list(carry.out_shape), scratch_shapes=list(scratch_shapes) + list(carry.scratch),
        input_output_aliases={len(operands) + i: n_out + o for i, o in carry.aliases.items()},
        compiler_params=pltpu.CompilerParams(dimension_semantics=("arbitrary",) * len(grid), vmem_limit_bytes=vmem),
    )(*operands, *carry.ins)
    carry.set_result(res[n_out:])
    return res[:n_out]


def _mm(a, b, *, mode, outs, name, epilogue=None, extras=(), rextras=(), vecs=(), a_sel=None, b_sel=None,
        tm=None, tn=2048, tk=None, layout=None, carry=None, ep_rows=None, a_fn=None):
    a2, b2 = a.shape[-2:], b.shape[-2:]
    if mode == "nn":
        (M, K), (K2, N) = a2, b2
    elif mode == "nt":
        (M, K), (N, K2) = a2, b2
    else:
        (K, M), (K2, N) = a2, b2
    assert K == K2, (a.shape, b.shape, mode)
    if tm is None:
        tm = 1024 if mode == "tn" else 512
    if tk is None:
        tk = 1024 if mode == "tn" else 2048
    if layout is not None:
        kind, nslots = layout
        r, c = (M // 2, N // nslots) if kind == "col" else (M // nslots, N // 2)
        tm, tn = _fit(tm, r), _fit(tn, c)
        assert r % tm == 0 and c % tn == 0
    else:
        tm, tn = _fit(tm, M), _fit(tn, N)
    tk = _fit(tk, K)
    assert M % tm == 0 and N % tn == 0 and K % tk == 0, (M, N, K, tm, tn, tk)
    nk = K // tk
    grid = (N // tn, M // tm, nk)

    def spec(arr, sel, blk, imap):
        if arr.ndim == 3:
            return pl.BlockSpec((None,) + blk, lambda j, i, k: (sel,) + imap(j, i, k))
        return pl.BlockSpec(blk, imap)

    if mode == "tn":
        a_spec = spec(a, a_sel, (tk, tm), lambda j, i, k: (k, i))
    else:
        a_spec = spec(a, a_sel, (tm, tk), lambda j, i, k: (i, k))
    if mode == "nt":
        b_spec = spec(b, b_sel, (tn, tk), lambda j, i, k: (j, k))
    else:
        b_spec = spec(b, b_sel, (tk, tn), lambda j, i, k: (k, j))
    in_specs = [a_spec, b_spec]
    in_specs += [pl.BlockSpec((tm, tn), lambda j, i, k: (i, j)) for _ in extras]
    in_specs += [pl.BlockSpec((tm, e.shape[1]), lambda j, i, k: (i, 0)) for e in rextras]
    in_specs += [pl.BlockSpec((1, tn), lambda j, i, k: (0, j)) for _ in vecs]

    if layout is None:
        out_shape = [jax.ShapeDtypeStruct((M, N), d) for d in outs]
        out_specs = [pl.BlockSpec((tm, tn), lambda j, i, k: (i, j)) for _ in outs]
    else:
        assert len(outs) == 1
        out_shape = [jax.ShapeDtypeStruct((2, nslots, r, c), outs[0])]
        rb, cb = r // tm, c // tn
        if kind == "col":
            omap = lambda j, i, k: (i // rb, j // cb, i % rb, j % cb)
        else:
            omap = lambda j, i, k: (j // cb, i // rb, i % rb, j % cb)
        out_specs = [pl.BlockSpec((None, None, tm, tn), omap)]

    ne, nr, nv, no = len(extras), len(rextras), len(vecs), len(outs)
    dims = _DIMS[mode]

    def body(*refs):
        a_ref, b_ref = refs[0], refs[1]
        rest = refs[2:2 + ne + nr + nv]
        o_refs = refs[2 + ne + nr + nv:2 + ne + nr + nv + no]

        def finish(total):
            if epilogue is None:
                for o in o_refs:
                    o[...] = total.astype(o.dtype)
                return
            step = min(ep_rows or tm, tm)
            for r0 in range(0, tm, step):
                rows = slice(r0, r0 + step)
                tiles = [x[rows, :] for x in rest[:ne + nr]] + [x[...] for x in rest[ne + nr:]]
                for o, val in zip(o_refs, epilogue(total[rows, :], *tiles)):
                    o[rows, :] = val.astype(o.dtype)

        a_tile = a_ref[...] if a_fn is None else a_fn(a_ref[...])
        part = lax.dot_general(a_tile.astype(BF16), b_ref[...].astype(BF16), dims, preferred_element_type=F32)
        if nk == 1:
            finish(part)
            return
        acc = refs[-1]
        k = pl.program_id(2)

        @pl.when(k == 0)
        def _():
            acc[...] = part

        @pl.when((k > 0) & (k < nk - 1))
        def _():
            acc[...] += part

        @pl.when(k == nk - 1)
        def _():
            finish(acc[...] + part)

    blk = (_nbytes((tm, tk), a.dtype) + _nbytes((tk, tn), b.dtype) + sum(_nbytes((tm, tn), e.dtype) for e in extras)
           + sum(_nbytes((tm, tn), d) for d in outs) + 2 * tm * tn * 4)
    res = _pcall(body, name=name, grid=grid, in_specs=in_specs, out_specs=out_specs, out_shape=out_shape,
                 operands=(a, b, *extras, *rextras, *vecs),
                 scratch_shapes=[pltpu.VMEM((tm, tn), F32)] if nk > 1 else [], vmem=_vmem(blk), carry=carry)
    return res[0] if no == 1 else tuple(res)


def _rows(fn, rows, vecs, outs, sums, *, tm, name, carry=None):
    S = rows[0].shape[0]
    tm = min(tm, S)
    assert S % tm == 0
    nr, nv, no, ns = len(rows), len(vecs), len(outs), len(sums)

    def body(*refs):
        vals = fn(*[r[...] for r in refs[:nr + nv]])
        o_refs = refs[nr + nv:nr + nv + no]
        s_refs = refs[nr + nv + no:]
        for o, val in zip(o_refs, vals[:no]):
            o[...] = val.astype(o.dtype)
        if ns:
            @pl.when(pl.program_id(0) == 0)
            def _():
                for s in s_refs:
                    s[...] = jnp.zeros_like(s)

            for s, val in zip(s_refs, vals[no:]):
                s[...] += jnp.sum(val.astype(F32), axis=0, keepdims=True)

    in_specs = [pl.BlockSpec((tm, r.shape[1]), lambda i: (i, 0)) for r in rows]
    in_specs += [pl.BlockSpec(v.shape, lambda i: (0, 0)) for v in vecs]
    out_specs = [pl.BlockSpec((tm, c), lambda i: (i, 0)) for c, _ in outs]
    out_specs += [pl.BlockSpec((1, c), lambda i: (0, 0)) for c in sums]
    out_shape = [jax.ShapeDtypeStruct((S, c), d) for c, d in outs]
    out_shape += [jax.ShapeDtypeStruct((1, c), F32) for c in sums]
    blk = sum(_nbytes((tm, r.shape[1]), r.dtype) for r in rows) + sum(_nbytes((tm, c), d) for c, d in outs)
    blk += 6 * tm * max(r.shape[1] for r in rows) * 4
    res = _pcall(body, name=name, grid=(S // tm,), in_specs=in_specs, out_specs=out_specs, out_shape=out_shape,
                 operands=(*rows, *vecs), vmem=_vmem(blk), carry=carry)
    return tuple(res)


def _ln_norm(z):
    mu = jnp.mean(z, axis=-1, keepdims=True)
    d = z - mu
    var = jnp.mean(d * d, axis=-1, keepdims=True)
    rstd = lax.rsqrt(var + LN_EPS)
    return d * rstd, rstd


def _ln(z, g, b):
    return _ln_norm(z)[0] * g + b


def _ln_bwd(dy, n, rstd, g):
    dn = dy * g
    return rstd * (dn - jnp.mean(dn, axis=-1, keepdims=True) - n * jnp.mean(dn * n, axis=-1, keepdims=True))


def _sq_relu(t):
    return jnp.square(jnp.maximum(t.astype(F32), 0.0))


def _sigmoid(x):
    return 1.0 / (1.0 + jnp.exp(-x))


def _per_head(x, fn):
    h = x.shape[1] // HEAD_DIM
    return jnp.concatenate([fn(x[:, i * HEAD_DIM:(i + 1) * HEAD_DIM], i) for i in range(h)], axis=1)


def _rot(x, cosf, sinf):
    return _per_head(x, lambda xh, i: xh * cosf + pltpu.roll(xh, HEAD_DIM // 2, 1) * sinf)


def _rot_t(dy, cosf, sinf):
    return _per_head(dy, lambda dh, i: dh * cosf + pltpu.roll(dh * sinf, HEAD_DIM // 2, 1))


def _shift_copies(win):
    rows = win.shape[1] - 8
    for s in range(1, 8):
        win[s, 0:rows, :] = win[0, s:s + rows, :]


def _rows_at(win, start):
    s = start % 8
    return win[s, start - s:start - s + CONV_ROWS, :]


def _conv_fwd(glu, dw, dwb, *, tm=256, tc=512, name="conv_fwd", carry=None):
    S, D = glu.shape
    tm, tc = min(tm, S), min(tc, D)
    ni = S // tm

    def body(cur_ref, prev_ref, dw_ref, dwb_ref, o_ref, win):
        i = pl.program_id(1)
        tail = prev_ref[tm - CONV_PAD:tm, :]
        win[0, 0:CONV_PAD, :] = jnp.where(i > 0, tail, jnp.zeros_like(tail))
        win[0, CONV_PAD:CONV_PAD + tm, :] = cur_ref[...]
        _shift_copies(win)
        first = CONV_PAD - CONV_WIDTH + 1
        for r0 in range(0, tm, CONV_ROWS):
            acc = jnp.zeros((CONV_ROWS, tc), F32) + dwb_ref[...]
            for k in range(CONV_WIDTH):
                acc = acc + _rows_at(win, r0 + first + k) * dw_ref[k:k + 1, :]
            o_ref[r0:r0 + CONV_ROWS, :] = acc

    return _pcall(
        body, name=name, grid=(D // tc, ni),
        in_specs=[pl.BlockSpec((tm, tc), lambda j, i: (i, j)),
                  pl.BlockSpec((tm, tc), lambda j, i: (jnp.maximum(i - 1, 0), j)),
                  pl.BlockSpec((CONV_PAD, tc), lambda j, i: (0, j)),
                  pl.BlockSpec((1, tc), lambda j, i: (0, j))],
        out_specs=[pl.BlockSpec((tm, tc), lambda j, i: (i, j))],
        out_shape=[jax.ShapeDtypeStruct((S, D), F32)],
        scratch_shapes=[pltpu.VMEM((8, tm + CONV_PAD, tc), F32)],
        operands=(glu, glu, dw, dwb), vmem=_vmem(8 * tm * tc * 4), carry=carry)[0]


def _conv_bwd(dc, glu, a_pre, g_pre, dw, ba, bg, *, tm=256, tc=512, name="conv_bwd", carry=None):
    S, D = dc.shape
    tm, tc = min(tm, S), min(tc, D)
    ni = S // tm

    def fold8(v):
        out = v[0:8]
        for r in range(8, CONV_ROWS, 8):
            out = out + v[r:r + 8]
        return out

    def body(dc_ref, dcn_ref, glu_ref, glup_ref, a_ref, g_ref, dw_ref, ba_ref, bg_ref,
             da_ref, dg_ref, ddw_ref, dba_ref, dbg_ref, dwin, gwin, taps):
        i = pl.program_id(1)

        @pl.when(i == 0)
        def _():
            ddw_ref[...] = jnp.zeros_like(ddw_ref)
            dba_ref[...] = jnp.zeros_like(dba_ref)
            dbg_ref[...] = jnp.zeros_like(dbg_ref)

        head = dcn_ref[0:CONV_PAD, :]
        dwin[0, 0:tm, :] = dc_ref[...]
        dwin[0, tm:tm + CONV_PAD, :] = jnp.where(i < ni - 1, head, jnp.zeros_like(head))
        tail = glup_ref[tm - CONV_PAD:tm, :]
        gwin[0, 0:CONV_PAD, :] = jnp.where(i > 0, tail, jnp.zeros_like(tail))
        gwin[0, CONV_PAD:CONV_PAD + tm, :] = glu_ref[...]
        _shift_copies(dwin)
        _shift_copies(gwin)
        taps[...] = jnp.zeros_like(taps)
        first = CONV_PAD - CONV_WIDTH + 1
        sum_a = jnp.zeros((8, tc), F32)
        sum_g = jnp.zeros((8, tc), F32)
        for r0 in range(0, tm, CONV_ROWS):
            dcur = dc_ref[r0:r0 + CONV_ROWS, :]
            dglu = jnp.zeros((CONV_ROWS, tc), F32)
            for k in range(CONV_WIDTH):
                dglu = dglu + _rows_at(dwin, r0 + CONV_WIDTH - 1 - k) * dw_ref[k:k + 1, :]
                taps[k] += fold8(dcur * _rows_at(gwin, r0 + first + k))
            a = a_ref[r0:r0 + CONV_ROWS, :] + ba_ref[...]
            sg = _sigmoid(g_ref[r0:r0 + CONV_ROWS, :] + bg_ref[...])
            da = dglu * sg
            dg = dglu * a * sg * (1.0 - sg)
            da_ref[r0:r0 + CONV_ROWS, :] = da.astype(BF16)
            dg_ref[r0:r0 + CONV_ROWS, :] = dg.astype(BF16)
            sum_a = sum_a + fold8(da)
            sum_g = sum_g + fold8(dg)
        ddw_ref[...] += jnp.sum(taps[...], axis=1)
        dba_ref[...] += jnp.sum(sum_a, axis=0, keepdims=True)
        dbg_ref[...] += jnp.sum(sum_g, axis=0, keepdims=True)

    tile = lambda f: pl.BlockSpec((tm, tc), f)
    vec = pl.BlockSpec((1, tc), lambda j, i: (0, j))
    return _pcall(
        body, name=name, grid=(D // tc, ni),
        in_specs=[tile(lambda j, i: (i, j)), tile(lambda j, i: (jnp.minimum(i + 1, ni - 1), j)),
                  tile(lambda j, i: (i, j)), tile(lambda j, i: (jnp.maximum(i - 1, 0), j)),
                  tile(lambda j, i: (i, j)), tile(lambda j, i: (i, j)),
                  pl.BlockSpec((CONV_PAD, tc), lambda j, i: (0, j)), vec, vec],
        out_specs=[tile(lambda j, i: (i, j)), tile(lambda j, i: (i, j)),
                   pl.BlockSpec((CONV_PAD, tc), lambda j, i: (0, j)), vec, vec],
        out_shape=[jax.ShapeDtypeStruct((S, D), BF16), jax.ShapeDtypeStruct((S, D), BF16),
                   jax.ShapeDtypeStruct((CONV_PAD, D), F32), jax.ShapeDtypeStruct((1, D), F32),
                   jax.ShapeDtypeStruct((1, D), F32)],
        scratch_shapes=[pltpu.VMEM((8, tm + CONV_PAD, tc), F32), pltpu.VMEM((8, tm + CONV_PAD, tc), F32),
                        pltpu.VMEM((CONV_PAD, 8, tc), F32)],
        operands=(dc, dc, glu, glu, a_pre, g_pre, dw, ba, bg), vmem=_vmem(16 * tm * tc * 4), carry=carry)


def _nt(a, b):
    return lax.dot_general(a, b, _DIMS["nt"], preferred_element_type=F32)


def _tn(a, b):
    return lax.dot_general(a, b, _DIMS["tn"], preferred_element_type=F32)


STAT_LANES = 128


def _per_head_pack(cols):
    rows = cols[0].shape[0]
    lane = lax.broadcasted_iota(jnp.int32, (rows, STAT_LANES), 1)
    out = jnp.zeros((rows, STAT_LANES), F32)
    for h, col in enumerate(cols):
        out = jnp.where(lane == h, col, out)
    return out


def _window_mask(qi, kj, first_key):
    B = ATTN_BLOCK
    return ((kj < B) & (kj >= qi) & (kj >= first_key)) | ((kj >= B) & (kj - B <= qi))


def _attn_fwd(q_rot, k, v, g, dil):
    S, D = k.shape
    H = D // HEAD_DIM
    L = S // dil
    nb_count = L // ATTN_BLOCK
    B = ATTN_BLOCK

    def body(q_ref, kc_ref, kp_ref, vc_ref, vp_ref, o_ref, lse_ref):
        nb = pl.program_id(1)
        qi = lax.broadcasted_iota(jnp.int32, (B, 2 * B), 0)
        kj = lax.broadcasted_iota(jnp.int32, (B, 2 * B), 1)
        valid = _window_mask(qi, kj, jnp.where(nb > 0, 0, B))
        stats = []
        for h in range(H):
            hs = slice(h * HEAD_DIM, (h + 1) * HEAD_DIM)
            kk = jnp.concatenate([kp_ref[:, hs], kc_ref[:, hs]], axis=0)
            vv = jnp.concatenate([vp_ref[:, hs], vc_ref[:, hs]], axis=0)
            s = jnp.where(valid, _nt(q_ref[:, hs], kk) * ATTN_SCALE, NEG)
            m = jnp.max(s, axis=1, keepdims=True)
            p = jnp.exp(s - m)
            l = jnp.sum(p, axis=1, keepdims=True)
            o = jnp.dot(p.astype(BF16), vv, preferred_element_type=F32) / l
            o_ref[:, hs] = o.astype(o_ref.dtype)
            stats.append(m + jnp.log(l))
        lse_ref[...] = _per_head_pack(stats)

    blk = lambda f: pl.BlockSpec((B, D), f)
    cur = lambda r, nb: (nb, r)
    prev = lambda r, nb: (jnp.maximum(nb - 1, 0), r)
    o, lse = pl.pallas_call(
        body, name=f"attn_fwd_g{g}", grid=(dil, nb_count),
        in_specs=[blk(lambda r, nb: (nb, r * N_GROUPS + g)), blk(cur), blk(prev), blk(cur), blk(prev)],
        out_specs=[blk(cur), pl.BlockSpec((B, STAT_LANES), cur)],
        out_shape=[jax.ShapeDtypeStruct((L, dil * D), BF16), jax.ShapeDtypeStruct((L, dil * STAT_LANES), F32)],
        compiler_params=pltpu.CompilerParams(dimension_semantics=("parallel", "arbitrary"),
                                             vmem_limit_bytes=_vmem(12 * B * D * 4)),
    )(q_rot.reshape(L, dil * N_GROUPS * D), k.reshape(L, dil * D), k.reshape(L, dil * D),
      v.reshape(L, dil * D), v.reshape(L, dil * D))
    return o.reshape(S, D), lse.reshape(S, STAT_LANES)


def _attn_bwd(q_rot, k, v, do, lse, dlt, g, dil, *, name, carry=None):
    S, D = k.shape
    H = D // HEAD_DIM
    L = S // dil
    nb_count = L // ATTN_BLOCK
    B = ATTN_BLOCK

    def body(q_ref, qn_ref, kc_ref, kp_ref, vc_ref, vp_ref, do_ref, don_ref, l_ref, ln_ref, d_ref, dn_ref,
             dq_ref, dk_ref, dv_ref):
        nb = pl.program_id(1)
        qi = lax.broadcasted_iota(jnp.int32, (B, 2 * B), 0)
        kj = lax.broadcasted_iota(jnp.int32, (B, 2 * B), 1)
        valid_q = _window_mask(qi, kj, jnp.where(nb > 0, 0, B))
        qr = lax.broadcasted_iota(jnp.int32, (2 * B, B), 0)
        kc_i = lax.broadcasted_iota(jnp.int32, (2 * B, B), 1)
        q_end = jnp.where(nb < nb_count - 1, 2 * B, B)
        valid_k = ((qr < B) & (kc_i <= qr)) | ((qr >= B) & (qr < q_end) & (kc_i >= qr - B))
        for h in range(H):
            hs = slice(h * HEAD_DIM, (h + 1) * HEAD_DIM)
            q, qn = q_ref[:, hs], qn_ref[:, hs]
            kc, vc = kc_ref[:, hs], vc_ref[:, hs]
            dout, doutn = do_ref[:, hs], don_ref[:, hs]
            lq, lqn = l_ref[:, h:h + 1], ln_ref[:, h:h + 1]
            dq_, dqn_ = d_ref[:, h:h + 1], dn_ref[:, h:h + 1]
            kk = jnp.concatenate([kp_ref[:, hs], kc], axis=0)
            vv = jnp.concatenate([vp_ref[:, hs], vc], axis=0)
            s = jnp.where(valid_q, _nt(q, kk) * ATTN_SCALE, NEG)
            p = jnp.exp(s - lq)
            ds = p * (_nt(dout, vv) - dq_)
            dq_ref[:, hs] = (jnp.dot(ds.astype(BF16), kk, preferred_element_type=F32) * ATTN_SCALE).astype(dq_ref.dtype)
            qq = jnp.concatenate([q, qn], axis=0)
            dd = jnp.concatenate([dout, doutn], axis=0)
            ll = jnp.concatenate([lq, lqn], axis=0)
            dl = jnp.concatenate([dq_, dqn_], axis=0)
            s2 = jnp.where(valid_k, _nt(qq, kc) * ATTN_SCALE, NEG)
            p2 = jnp.exp(s2 - ll)
            dv_ref[:, hs] = _tn(p2.astype(BF16), dd).astype(dv_ref.dtype)
            ds2 = p2 * (_nt(dd, vc) - dl)
            dk_ref[:, hs] = (_tn(ds2.astype(BF16), qq) * ATTN_SCALE).astype(dk_ref.dtype)

    blk = lambda f: pl.BlockSpec((B, D), f)
    cur = lambda r, nb: (nb, r)
    prev = lambda r, nb: (jnp.maximum(nb - 1, 0), r)
    nxt = lambda r, nb: (jnp.minimum(nb + 1, nb_count - 1), r)
    qcur = lambda r, nb: (nb, r * N_GROUPS + g)
    qnxt = lambda r, nb: (jnp.minimum(nb + 1, nb_count - 1), r * N_GROUPS + g)
    qv = q_rot.reshape(L, dil * N_GROUPS * D)
    view = lambda t: t.reshape(L, dil * D)
    sview = lambda t: t.reshape(L, dil * STAT_LANES)
    stat = lambda f: pl.BlockSpec((B, STAT_LANES), f)
    dq, dk, dv = _pcall(
        body, name=name, grid=(dil, nb_count),
        in_specs=[blk(qcur), blk(qnxt), blk(cur), blk(prev), blk(cur), blk(prev), blk(cur), blk(nxt),
                  stat(cur), stat(nxt), stat(cur), stat(nxt)],
        out_specs=[blk(cur), blk(cur), blk(cur)],
        out_shape=[jax.ShapeDtypeStruct((L, dil * D), BF16)] * 3,
        operands=(qv, qv, view(k), view(k), view(v), view(v), view(do), view(do), sview(lse), sview(lse), sview(dlt),
                  sview(dlt)),
        vmem=_vmem(24 * B * D * 4), carry=carry)
    return dq.reshape(S, D), dk.reshape(S, D), dv.reshape(S, D)


def _mlp_ple_fwd(z1, h1b, p_l, W, vec, l, run, kv_vec=None):
    D = z1.shape[1]
    g1, b1, g2, b2 = vec

    t = run(_mm, h1b, W[f"mlp_up{l}"], mode="nn", outs=[BF16], name=f"mlp_up{l}")

    def z2_ep(acc, z1_t, g1_, b1_, g2_, b2_):
        z2 = ALPHA * _ln(z1_t, g1_, b1_) + acc
        return z2, _ln(z2, g2_, b2_)

    z2, h2b = run(_mm, t, W[f"mlp_down{l}"], mode="nn", outs=[F32, BF16], extras=[z1], vecs=[g1, b1, g2, b2], a_fn=_sq_relu,
                  epilogue=z2_ep, ep_rows=EPILOGUE_ROWS, name=f"mlp_down{l}")
    act = None
    pp = run(_mm, p_l, W[f"ple_proj{l}"], mode="nn", outs=[F32], name=f"ple_proj{l}")
    if kv_vec is None:
        gpre = run(_mm, h2b, W[f"ple_gate{l}"], mode="nn", outs=[F32], name=f"ple_gate{l}")
        return t, act, z2, h2b, pp, gpre

    def x1_ep(acc, z2_t, pp_t, g2_, b2_, kg, kb):
        x1, _ = _ple_out(z2_t, pp_t, acc, g2_, b2_)
        return acc, x1, _ln(x1, kg, kb)

    gpre, x1, kvn = run(_mm, h2b, W[f"ple_gate{l}"], mode="nn", outs=[F32, F32, BF16], extras=[z2, pp],
                        vecs=[g2, b2, *kv_vec], epilogue=x1_ep, ep_rows=EPILOGUE_ROWS, tm=256, name=f"ple_gate{l}")
    return t, act, z2, h2b, pp, gpre, x1, kvn


def _mlp_ple_bwd(dy, d_pp, d_gpre, p_l, z1, h1b, t, act, z2, h2b, wts, vec, l, run, produce):
    D = z1.shape[1]
    up, down, pp_w, pg_w = wts
    g1, b1, g2, b2 = vec
    produce(f"ple_proj{l}", run(_mm, p_l, d_pp, mode="tn", outs=[BF16], layout=("col", N_CHIPS), name=f"d_ple_proj{l}"))
    produce(f"ple_gate{l}", run(_mm, h2b, d_gpre, mode="tn", outs=[BF16], layout=("row", N_CHIPS), name=f"d_ple_gate{l}"))
    dh2 = run(_mm, d_gpre, pg_w, mode="nt", outs=[F32], extras=[dy], epilogue=lambda acc, e: (acc + e,),
              name=f"dh2_{l}")

    def ln2_bwd(dh2_t, z2_t, g2_):
        n, rstd = _ln_norm(z2_t)
        dz2 = _ln_bwd(dh2_t, n, rstd, g2_)
        return dz2, dz2, dh2_t * n, dh2_t

    dz2, dz2b, dg2, db2 = _rows(ln2_bwd, [dh2, z2], [g2], [(D, F32), (D, BF16)], [D, D], tm=256,
                                name=f"ln2_bwd{l}")
    produce(f"mlp_down{l}", run(_mm, t, dz2b, mode="tn", outs=[BF16], layout=("row", N_CHIPS), a_fn=_sq_relu,
                                name=f"d_mlp_down{l}"))
    dt = run(_mm, dz2b, down, mode="nt", outs=[BF16], extras=[t],
             epilogue=lambda acc, t_: (acc * 2.0 * jnp.maximum(t_.astype(F32), 0.0),), name=f"dt{l}")
    produce(f"mlp_up{l}", run(_mm, h1b, dt, mode="tn", outs=[BF16], layout=("col", N_CHIPS), name=f"d_mlp_up{l}"))
    dh1 = run(_mm, dt, up, mode="nt", outs=[F32], extras=[dz2], epilogue=lambda acc, e: (acc + ALPHA * e,),
              name=f"dh1_{l}")

    def ln1_bwd(dh1_t, z1_t, g1_):
        n, rstd = _ln_norm(z1_t)
        dz1 = _ln_bwd(dh1_t, n, rstd, g1_)
        return dz1, dz1, dh1_t * n, dh1_t

    dz1, dz1b, dg1, db1 = _rows(ln1_bwd, [dh1, z1], [g1], [(D, F32), (D, BF16)], [D, D], tm=256,
                                name=f"ln1_bwd{l}")
    return dz1, dz1b, (dg1, db1, dg2, db2)


def _ple_out(z2, pp, gpre, g2, b2):
    gt = _sigmoid(gpre)
    return _ln(z2, g2, b2) + pp * gt, gt


_GATHER_AT = {
    "conv_in_a": ("conv_w_out",),
    "conv_in_g": ("ple_gate0", "ple_proj0"),
    "conv_fwd": ("mlp_up0",),
    "mlp_up0": ("mlp_down0",),
    "mlp_down0": ("attn_w_q", "w_kv"),
    "kv_k": ("attn_w_o",),
    "kv_v": ("ple_proj1", "ple_gate1"),
    "attn_q": ("mlp_up1",),
    "mlp_up1": ("mlp_down1",),
}
_GATHER_FIRST = ("conv_w_in",)


def _local_step(x, p, cosf, sinf, target, W, V, shards=None, reducer=None, chip_arr=None):
    S, D = x.shape
    gw, gv = {}, {}
    if shards is not None:
        W = dict(_Gather(_GATHER_FIRST, shards, chip_arr).run_alone("gather_first"))

    def run(fn, *args, name, **kw):
        gather = _Gather(_GATHER_AT[name], shards, chip_arr) if (shards is not None and name in _GATHER_AT) else None
        carry = gather if reducer is None or gather is not None else reducer.carry(name)
        out = fn(*args, name=name, carry=carry, **kw)
        if gather is not None:
            W.update(gather.result)
        elif reducer is not None:
            reducer.carried()
        return out

    a_pre = run(_mm, x, W["conv_w_in"], b_sel=0, mode="nn", outs=[F32], name="conv_in_a")
    g_pre = run(_mm, x, W["conv_w_in"], b_sel=1, mode="nn", outs=[F32], name="conv_in_g")
    (glu,) = _rows(lambda a, g, ba, bg: ((a + ba) * _sigmoid(g + bg),), [a_pre, g_pre], [V["conv_b_a"], V["conv_b_g"]],
                   [(D, F32)], [], tm=256, name="glu_fwd")
    cv = run(_conv_fwd, glu, V["conv_dw"], V["conv_dw_b"], name="conv_fwd")

    def silu_ln(c, g_, b_):
        y = _ln(c, g_, b_)
        return (y * _sigmoid(y),)

    (sb,) = _rows(silu_ln, [cv], [V["conv_ln_g"], V["conv_ln_b"]], [(D, BF16)], [], tm=256, name="conv_ln_fwd")
    def z1_ep(acc, x_t, g_, b_):
        z1 = ALPHA * x_t + acc
        return z1, _ln(z1, g_, b_)

    vec0 = (V["ln1_g0"], V["ln1_b0"], V["ln2_g0"], V["ln2_b0"])
    vec1 = (V["ln1_g1"], V["ln1_b1"], V["ln2_g1"], V["ln2_b1"])
    z1_0, h1b_0 = _mm(sb, W["conv_w_out"], mode="nn", outs=[F32, BF16], extras=[x], vecs=[vec0[0], vec0[1]],
                      epilogue=z1_ep, ep_rows=EPILOGUE_ROWS, name="conv_out")
    t0, act0, z2_0, h2b_0, pp0, gpre0, x1, kvn = _mlp_ple_fwd(z1_0, h1b_0, p[0], W, vec0, 0, run,
                                                              (V["kv_ln_g"], V["kv_ln_b"]))

    rot_ep = lambda acc, c_, s_: (_rot(acc, c_, s_),)
    k_rot = run(_mm, kvn, W["w_kv"], b_sel=0, mode="nn", outs=[BF16], rextras=[cosf, sinf], epilogue=rot_ep, name="kv_k")
    v_b = run(_mm, kvn, W["w_kv"], b_sel=1, mode="nn", outs=[BF16], name="kv_v")
    q_rot = run(_mm, x1, W["attn_w_q"], mode="nn", outs=[BF16], rextras=[cosf, sinf], epilogue=rot_ep, name="attn_q")
    og, lg = [], []
    for g, dil in enumerate(DILATIONS):
        o_g, l_g = _attn_fwd(q_rot, k_rot, v_b, g, dil)
        og.append(o_g)
        lg.append(l_g)

    def merge(o0, o1, o2, l0, l1, l2):
        m = jnp.maximum(jnp.maximum(l0, l1), l2)
        e = [jnp.exp(l0 - m), jnp.exp(l1 - m), jnp.exp(l2 - m)]
        den = e[0] + e[1] + e[2]
        w = [e_g / den for e_g in e]
        o = _per_head(o0, lambda oh, h: sum(w[g][:, h:h + 1] * (o0, o1, o2)[g][:, h * HEAD_DIM:(h + 1) * HEAD_DIM].astype(F32)
                                            for g in range(N_GROUPS)))
        return o, m + jnp.log(den)

    ob, lse = _rows(merge, og + lg, [], [(D, BF16), (STAT_LANES, F32)], [], tm=256, name="attn_merge")
    z1_1, h1b_1 = _mm(ob, W["attn_w_o"], mode="nn", outs=[F32, BF16], extras=[x1], vecs=[vec1[0], vec1[1]],
                      epilogue=z1_ep, ep_rows=EPILOGUE_ROWS, name="attn_out")
    t1, act1, z2_1, h2b_1, pp1, gpre1 = _mlp_ple_fwd(z1_1, h1b_1, p[1], W, vec1, 1, run)
    wts0 = (W["mlp_up0"], W["mlp_down0"], W["ple_proj0"], W["ple_gate0"])
    wts1 = (W["mlp_up1"], W["mlp_down1"], W["ple_proj1"], W["ple_gate1"])

    def head(z2, pp, gpre, tgt, g2, b2):
        y, gt = _ple_out(z2, pp, gpre, g2, b2)
        err = y - tgt
        dy = err * (1.0 / D)
        return dy, dy * gt, dy * pp * gt * (1.0 - gt), 0.5 * err * err * (1.0 / D)

    dy1, d_pp1, d_gpre1, loss_cols = _rows(head, [z2_1, pp1, gpre1, target], [vec1[2], vec1[3]],
                                           [(D, F32), (D, BF16), (D, BF16)], [D], tm=256, name="loss_head")

    def produce(name, grad):
        gw[name] = grad
        if reducer is not None:
            reducer.produced(name, grad)

    dz1_1, dz1b_1, (gv["ln1_g1"], gv["ln1_b1"], gv["ln2_g1"], gv["ln2_b1"]) = _mlp_ple_bwd(
        dy1, d_pp1, d_gpre1, p[1], z1_1, h1b_1, t1, act1, z2_1, h2b_1, wts1, vec1, 1, run, produce)
    produce("attn_w_o", run(_mm, ob, dz1b_1, mode="tn", outs=[BF16], layout=("row", N_CHIPS), name="d_attn_w_o"))

    do_b = run(_mm, dz1b_1, W["attn_w_o"], mode="nt", outs=[BF16], name="attn_do")

    def delta(do_t, o_t):
        prod = do_t.astype(F32) * o_t.astype(F32)
        H = D // HEAD_DIM
        return (_per_head_pack([jnp.sum(prod[:, h * HEAD_DIM:(h + 1) * HEAD_DIM], axis=1, keepdims=True) for h in range(H)]),)

    (dlt,) = _rows(delta, [do_b, ob], [], [(STAT_LANES, F32)], [], tm=256, name="attn_delta")
    dqs, dks, dvs = [], [], []
    for g, dil in enumerate(DILATIONS):
        dq_g, dk_g, dv_g = run(_attn_bwd, q_rot, k_rot, v_b, do_b, lse, dlt, g, dil, name=f"attn_bwd_g{g}")
        dqs.append(dq_g)
        dks.append(dk_g)
        dvs.append(dv_g)

    def unrot(q0, q1, q2, k0, k1, k2, v0, v1, v2, c_, s_):
        dq = jnp.concatenate([_rot_t(t_.astype(F32), c_, s_) for t_ in (q0, q1, q2)], axis=1)
        f = lambda t_: t_.astype(F32)
        return dq, _rot_t(f(k0) + f(k1) + f(k2), c_, s_), f(v0) + f(v1) + f(v2)

    dq, dk, dv = run(_rows, unrot, dqs + dks + dvs + [cosf, sinf], [], [(N_GROUPS * D, BF16), (D, BF16), (D, BF16)], [],
                     tm=128, name="attn_unrot")
    produce("attn_w_q", run(_mm, x1, dq, mode="tn", outs=[BF16], layout=("col", N_CHIPS), name="d_attn_w_q"))
    dx1_q = run(_mm, dq, W["attn_w_q"], mode="nt", outs=[F32], extras=[dz1_1],
                epilogue=lambda acc, e: (acc + ALPHA * e,), name="dx1_q")
    produce("w_kv", jnp.concatenate(
        [run(_mm, kvn, dk, mode="tn", outs=[BF16], layout=("col", 2), name="d_w_kv_k"),
         run(_mm, kvn, dv, mode="tn", outs=[BF16], layout=("col", 2), name="d_w_kv_v")], axis=1))
    dkvn_k = run(_mm, dk, W["w_kv"], b_sel=0, mode="nt", outs=[F32], name="dkvn_k")
    dkvn = run(_mm, dv, W["w_kv"], b_sel=1, mode="nt", outs=[F32], extras=[dkvn_k], epilogue=lambda acc, e: (acc + e,),
               name="dkvn_v")

    def x1_bwd(dx1q_t, dkvn_t, x1_t, pp, gpre, kg):
        n, rstd = _ln_norm(x1_t)
        dy = dx1q_t + _ln_bwd(dkvn_t, n, rstd, kg)
        gt = _sigmoid(gpre)
        return dy, dy * gt, dy * pp * gt * (1.0 - gt), dkvn_t * n, dkvn_t

    dy0, d_pp0, d_gpre0, gv["kv_ln_g"], gv["kv_ln_b"] = _rows(
        x1_bwd, [dx1_q, dkvn, x1, pp0, gpre0], [V["kv_ln_g"]], [(D, F32), (D, BF16), (D, BF16)], [D, D], tm=256,
        name="x1_bwd")

    dz1_0, dz1b_0, (gv["ln1_g0"], gv["ln1_b0"], gv["ln2_g0"], gv["ln2_b0"]) = _mlp_ple_bwd(
        dy0, d_pp0, d_gpre0, p[0], z1_0, h1b_0, t0, act0, z2_0, h2b_0, wts0, vec0, 0, run, produce)
    produce("conv_w_out", run(_mm, sb, dz1b_0, mode="tn", outs=[BF16], layout=("row", N_CHIPS), name="d_conv_w_out"))
    ds = run(_mm, dz1b_0, W["conv_w_out"], mode="nt", outs=[F32], name="conv_ds")

    def conv_ln_bwd(ds_t, c_t, g_, b_):
        n, rstd = _ln_norm(c_t)
        y = n * g_ + b_
        sg = _sigmoid(y)
        dln = ds_t * sg * (1.0 + y * (1.0 - sg))
        dc = _ln_bwd(dln, n, rstd, g_)
        return dc, dln * n, dln, dc

    dc, gv["conv_ln_g"], gv["conv_ln_b"], gv["conv_dw_b"] = _rows(
        conv_ln_bwd, [ds, cv], [V["conv_ln_g"], V["conv_ln_b"]], [(D, F32)], [D, D, D], tm=256, name="conv_ln_bwd")
    da, dg, gv["conv_dw"], gv["conv_b_a"], gv["conv_b_g"] = run(
        _conv_bwd, dc, glu, a_pre, g_pre, V["conv_dw"], V["conv_b_a"], V["conv_b_g"], name="conv_bwd")
    produce("conv_w_in", jnp.concatenate(
        [run(_mm, x, da, mode="tn", outs=[BF16], layout=("col", 2), name="d_conv_w_in_a"),
         run(_mm, x, dg, mode="tn", outs=[BF16], layout=("col", 2), name="d_conv_w_in_g")], axis=1))
    dx_a = run(_mm, da, W["conv_w_in"], b_sel=0, mode="nt", outs=[F32], extras=[dz1_0],
               epilogue=lambda acc, e: (acc + ALPHA * e,), name="dx_a")
    grad_x = run(_mm, dg, W["conv_w_in"], b_sel=1, mode="nt", outs=[F32], extras=[dx_a],
                 epilogue=lambda acc, e: (acc + e,), name="dx_g")
    if reducer is not None:
        reducer.carry("share_last").run_alone("share_last")
        reducer.carried()
    return loss_cols, grad_x, gw, gv


def _place():
    x, y, c = lax.axis_index("x"), lax.axis_index("y"), lax.axis_index("c")
    chips = [(1 - x, y), (x, 1 - y), (1 - x, 1 - y)]
    return x, y, c, chips


def _remote(src, dst, ssem, rsem, dev):
    return pltpu.make_async_remote_copy(src_ref=src, dst_ref=dst, send_sem=ssem, recv_sem=rsem, device_id=dev,
                                        device_id_type=MESH)


def _allgather8(block, name):
    R, C = block.shape

    def body(x_ref, out_ref, send_sems, recv_sems, local_sem):
        x, y, c, chips = _place()
        me, sibling = (x, y, c), (x, y, 1 - c)

        def slot(px, py, pc):
            return out_ref.at[4 * px + 2 * py + pc]

        def copy(k, blockpos, to, src=None):
            return _remote(slot(*blockpos) if src is None else src, slot(*blockpos), send_sems.at[k], recv_sems.at[k], to)

        mine = pltpu.make_async_copy(x_ref, slot(*me), local_sem)
        mine.start()
        first = [copy(0, me, sibling, src=x_ref)]
        first += [copy(1 + j, me, (*chip, c), src=x_ref) for j, chip in enumerate(chips)]
        for cp in first:
            cp.start()
        passed = [copy(4 + j, (*chip, c), sibling) for j, chip in enumerate(chips)]
        for j, chip in enumerate(chips):
            copy(1 + j, (*chip, c), me).wait_recv()
            passed[j].start()
        copy(0, sibling, me).wait_recv()
        for j, chip in enumerate(chips):
            copy(4 + j, (*chip, 1 - c), me).wait_recv()
        for cp in first + passed:
            cp.wait_send()
        mine.wait()

    return pl.pallas_call(
        body, name=name, out_shape=jax.ShapeDtypeStruct((8, R, C), block.dtype),
        in_specs=[pl.BlockSpec(memory_space=pltpu.VMEM)], out_specs=pl.BlockSpec(memory_space=pltpu.VMEM),
        scratch_shapes=[pltpu.SemaphoreType.DMA((7,)), pltpu.SemaphoreType.DMA((7,)), pltpu.SemaphoreType.DMA],
        compiler_params=pltpu.CompilerParams(vmem_limit_bytes=_vmem(10 * _nbytes((R, C), block.dtype))),
    )(block)


_MATS = (
    ("conv_w_in", "conv_w_in", 0, "col", True),
    ("conv_w_out", "conv_w_out", 0, "row", False),
    ("mlp_up0", "mlp_up", 0, "col", False),
    ("mlp_down0", "mlp_down", 0, "row", False),
    ("ple_proj0", "ple_proj", 0, "col", False),
    ("ple_gate0", "ple_gate", 0, "row", False),
    ("w_kv", "w_kv", None, "col", True),
    ("attn_w_q", "attn_w_q", 0, "col", False),
    ("attn_w_o", "attn_w_o", 0, "row", False),
    ("mlp_up1", "mlp_up", 1, "col", False),
    ("mlp_down1", "mlp_down", 1, "row", False),
    ("ple_proj1", "ple_proj", 1, "col", False),
    ("ple_gate1", "ple_gate", 1, "row", False),
)


class _Carry:
    result = None
    aliases = {}

    def set_result(self, outs):
        self.result = dict(zip(self.names, outs))

    def run_alone(self, name):
        n_in, n_out = len(self.ins), len(self.out_shape)

        def body(*refs):
            in_refs, out_refs, sems = refs[:n_in], refs[n_in:n_in + n_out], refs[n_in + n_out:]
            self.start(in_refs, out_refs, sems)
            self.finish(in_refs, out_refs, sems)

        outs = pl.pallas_call(body, name=name, out_shape=self.out_shape, in_specs=[ANY] * n_in, out_specs=[ANY] * n_out,
                              scratch_shapes=self.scratch, input_output_aliases=dict(self.aliases))(*self.ins)
        self.set_result(outs)
        return self.result


class _Gather(_Carry):
    def __init__(self, names, shards, chip_arr):
        mats = [m for m in _MATS if m[0] in names]
        srcs = sorted({m[1] for m in mats})
        self.names = [m[0] for m in mats]
        self.out_shape, self.geo, placed = [], [], []
        for name, src, layer, kind, split in mats:
            s = shards[src]
            ks, ns = s.shape[-2:]
            K, N = (ks, ns * N_CHIPS) if kind == "col" else (ks * N_CHIPS, ns)
            self.out_shape.append(jax.ShapeDtypeStruct((2, K, N // 2) if split else (K, N), BF16))
            self.geo.append((srcs.index(src), layer if s.ndim == 3 else None, kind, split, K, N))
            placed.append(_place_shard(s, layer if s.ndim == 3 else None, kind, split, chip_arr, f"place_{name}"))
        T = len(mats)
        self.ins = [shards[n] for n in srcs] + placed
        self.aliases = {len(srcs) + t: t for t in range(T)}
        self.scratch = [pltpu.SemaphoreType.DMA((3 * T,)) for _ in range(4)]
        self.result = None

    def _copies(self, in_refs, out_refs, sems):
        geo, T = self.geo, len(self.geo)
        s_ici, r_ici, s_d2d, r_d2d = sems
        x, y, c, chips = _place()
        me = 2 * x + y
        sibling = (x, y, 1 - c)
        idx = [2 * cx + cy for cx, cy in chips]

        def src_ref(t):
            i, layer, _, _, _, _ = geo[t]
            return in_refs[i] if layer is None else in_refs[i].at[layer]

        def src_half(t, h):
            _, _, kind, _, K, N = geo[t]
            if kind == "col":
                return src_ref(t).at[pl.ds(h * (K // 2), K // 2), :]
            return src_ref(t).at[:, pl.ds(h * (N // 2), N // 2)]

        def dst(t, j, h):
            _, _, kind, split, K, N = geo[t]
            n, k = N // N_CHIPS, K // N_CHIPS
            if kind == "col":
                rows = slice(None) if h is None else pl.ds(h * (K // 2), K // 2)
                if split:
                    return out_refs[t].at[j // 2, rows, pl.ds((j % 2) * n, n)]
                return out_refs[t].at[rows, pl.ds(j * n, n)]
            cols = slice(None) if h is None else pl.ds(h * (N // 2), N // 2)
            return out_refs[t].at[pl.ds(j * k, k), cols]

        sends = [_remote(src_half(t, c), dst(t, me, c), s_ici.at[3 * t + kk], r_ici.at[3 * t + kk], (*chips[kk], c))
                 for t in range(T) for kk in range(3)]
        hops = []
        for t in range(T):
            for kk in range(3):
                mine, theirs = dst(t, idx[kk], c), dst(t, idx[kk], 1 - c)
                hops.append((_remote(mine, mine, s_ici.at[3 * t + kk], r_ici.at[3 * t + kk], sibling),
                             _remote(mine, mine, s_d2d.at[3 * t + kk], r_d2d.at[3 * t + kk], sibling),
                             _remote(theirs, theirs, s_d2d.at[3 * t + kk], r_d2d.at[3 * t + kk], sibling)))
        return sends, hops

    def start(self, in_refs, out_refs, sems):
        for cp in self._copies(in_refs, out_refs, sems)[0]:
            cp.start()

    def finish(self, in_refs, out_refs, sems):
        sends, hops = self._copies(in_refs, out_refs, sems)
        for landed, forward, _ in hops:
            landed.wait_recv()
            forward.start()
        for _, _, from_sibling in hops:
            from_sibling.wait_recv()
        for cp in sends + [h[1] for h in hops]:
            cp.wait_send()


def _place_shard(shard, layer, kind, split, chip_arr, name):
    ks, ns = shard.shape[-2:]
    K, N = (ks, ns * N_CHIPS) if kind == "col" else (ks * N_CHIPS, ns)
    tr = _fit(256, ks)
    nb = ks // tr
    if shard.ndim == 3:
        in_spec = pl.BlockSpec((None, tr, ns), lambda i, me: (layer, i, 0))
    else:
        in_spec = pl.BlockSpec((tr, ns), lambda i, me: (i, 0))
    if kind == "row":
        out_shape, out_spec = (K, N), pl.BlockSpec((tr, ns), lambda i, me: (me[0] * nb + i, 0))
    elif split:
        out_shape, out_spec = (2, K, N // 2), pl.BlockSpec((None, tr, ns), lambda i, me: (me[0] // 2, i, me[0] % 2))
    else:
        out_shape, out_spec = (K, N), pl.BlockSpec((tr, ns), lambda i, me: (i, me[0]))

    def body(me_ref, s_ref, o_ref):
        o_ref[...] = s_ref[...]

    return pl.pallas_call(
        body, name=name, out_shape=jax.ShapeDtypeStruct(out_shape, BF16),
        grid_spec=pltpu.PrefetchScalarGridSpec(num_scalar_prefetch=1, grid=(nb,), in_specs=[in_spec], out_specs=out_spec),
        compiler_params=pltpu.CompilerParams(dimension_semantics=("parallel",), vmem_limit_bytes=_vmem(4 * tr * ns * 2)),
    )(chip_arr, shard)


class _Multi(_Carry):
    def __init__(self, parts):
        self.parts = parts
        self.ins = [a for p in parts for a in p.ins]
        self.out_shape = [a for p in parts for a in p.out_shape]
        self.scratch = [a for p in parts for a in p.scratch]
        self.aliases, n_in, n_out = {}, 0, 0
        for p in parts:
            self.aliases.update({n_in + i: n_out + o for i, o in p.aliases.items()})
            n_in, n_out = n_in + len(p.ins), n_out + len(p.out_shape)

    def _split(self, seq, field):
        out, at = [], 0
        for p in self.parts:
            n = len(getattr(p, field))
            out.append(seq[at:at + n])
            at += n
        return out

    def _each(self, method, in_refs, out_refs, sems):
        for p, i, o, s in zip(self.parts, self._split(in_refs, "ins"), self._split(out_refs, "out_shape"),
                              self._split(sems, "scratch")):
            getattr(p, method)(i, o, s)

    def start(self, in_refs, out_refs, sems):
        self._each("start", in_refs, out_refs, sems)

    def finish(self, in_refs, out_refs, sems):
        self._each("finish", in_refs, out_refs, sems)

    def set_result(self, outs):
        for p, o in zip(self.parts, self._split(list(outs), "out_shape")):
            p.set_result(o)


class _PairSend(_Carry):
    def __init__(self, grads):
        self.names = list(grads)
        self.ins = [grads[n] for n in self.names]
        self.out_shape = [jax.ShapeDtypeStruct(a.shape[1:], BF16) for a in self.ins]
        T = len(self.names)
        self.scratch = [pltpu.SemaphoreType.DMA((T,)), pltpu.SemaphoreType.DMA((T,))]

    def _copies(self, in_refs, out_refs, sems):
        x, y, c, _ = _place()
        return [_remote(in_refs[t].at[1 - c], out_refs[t], sems[0].at[t], sems[1].at[t], (x, y, 1 - c))
                for t in range(len(self.names))]

    def start(self, in_refs, out_refs, sems):
        for cp in self._copies(in_refs, out_refs, sems):
            cp.start()

    def finish(self, in_refs, out_refs, sems):
        for cp in self._copies(in_refs, out_refs, sems):
            cp.wait()


class _ChipScatter(_Carry):
    def __init__(self, sums):
        self.names = list(sums)
        T = len(self.names)
        self.ins = [sums[n][0] for n in self.names] + [sums[n][1] for n in self.names]
        self.out_shape = [jax.ShapeDtypeStruct(a.shape, BF16) for a in self.ins[:T]]
        self.aliases = {T + t: t for t in range(T)}
        self.scratch = [pltpu.SemaphoreType.DMA((3 * T,)), pltpu.SemaphoreType.DMA((3 * T,))]

    def _copies(self, in_refs, out_refs, sems):
        ssem, rsem = sems
        x, y, c, chips = _place()
        me = 2 * x + y
        idx = [2 * cx + cy for cx, cy in chips]
        T = len(self.names)
        sends = [_remote(in_refs[t].at[idx[kk]], out_refs[t].at[me], ssem.at[3 * t + kk], rsem.at[3 * t + kk],
                         (*chips[kk], c)) for t in range(T) for kk in range(3)]
        lands = [_remote(out_refs[t].at[idx[kk]], out_refs[t].at[idx[kk]], ssem.at[3 * t + kk], rsem.at[3 * t + kk],
                         (*chips[kk], c)) for t in range(T) for kk in range(3)]
        return sends, lands

    def start(self, in_refs, out_refs, sems):
        for cp in self._copies(in_refs, out_refs, sems)[0]:
            cp.start()

    def finish(self, in_refs, out_refs, sems):
        sends, lands = self._copies(in_refs, out_refs, sems)
        for cp in lands:
            cp.wait_recv()
        for cp in sends:
            cp.wait_send()


class _PairShare(_Carry):
    def __init__(self, halves):
        self.names = list(halves)
        self.ins = [halves[n] for n in self.names]
        self.out_shape = [jax.ShapeDtypeStruct(a.shape, F32) for a in self.ins]
        T = len(self.names)
        self.aliases = {t: t for t in range(T)}
        self.scratch = [pltpu.SemaphoreType.DMA((T,)), pltpu.SemaphoreType.DMA((T,))]

    def _copies(self, in_refs, out_refs, sems):
        ssem, rsem = sems
        x, y, c, _ = _place()
        sibling = (x, y, 1 - c)
        T = len(self.names)
        sends = [_remote(out_refs[t].at[c], out_refs[t].at[c], ssem.at[t], rsem.at[t], sibling) for t in range(T)]
        lands = [_remote(out_refs[t].at[1 - c], out_refs[t].at[1 - c], ssem.at[t], rsem.at[t], sibling) for t in range(T)]
        return sends, lands

    def start(self, in_refs, out_refs, sems):
        for cp in self._copies(in_refs, out_refs, sems)[0]:
            cp.start()

    def finish(self, in_refs, out_refs, sems):
        sends, lands = self._copies(in_refs, out_refs, sems)
        for cp in lands:
            cp.wait_recv()
        for cp in sends:
            cp.wait_send()


def _pair_sum(own, landed, c_arr, name):
    _, ns, r, cc = own.shape
    rows = ns * r
    tr = _fit(512, rows)

    def body(c_ref, a_ref, b_ref, o_ref, o2_ref):
        total = (a_ref[...].astype(F32) + b_ref[...].astype(F32)).astype(o_ref.dtype)
        o_ref[...] = total
        o2_ref[...] = total

    tile = pl.BlockSpec((tr, cc), lambda i, c_ref: (i, 0))
    out = pl.pallas_call(
        body, name=name, out_shape=[jax.ShapeDtypeStruct((rows, cc), BF16)] * 2,
        grid_spec=pltpu.PrefetchScalarGridSpec(
            num_scalar_prefetch=1, grid=(rows // tr,),
            in_specs=[pl.BlockSpec((None, tr, cc), lambda i, c_ref: (c_ref[0], i, 0)), tile], out_specs=[tile, tile]),
        compiler_params=pltpu.CompilerParams(dimension_semantics=("parallel",), vmem_limit_bytes=_vmem(8 * tr * cc * 4)),
    )(c_arr, own.reshape(2, rows, cc), landed.reshape(rows, cc))
    return out[0].reshape(ns, r, cc), out[1].reshape(ns, r, cc)


def _chip_sum(parts, c_arr, name):
    _, r, cc = parts.shape
    tr = _fit(256, r)

    def body(c_ref, p_ref, o_ref):
        acc = p_ref[0].astype(F32)
        for j in range(1, N_CHIPS):
            acc = acc + p_ref[j].astype(F32)
        o_ref[...] = acc

    return pl.pallas_call(
        body, name=name, out_shape=jax.ShapeDtypeStruct((2, r, cc), F32),
        grid_spec=pltpu.PrefetchScalarGridSpec(
            num_scalar_prefetch=1, grid=(r // tr,),
            in_specs=[pl.BlockSpec((N_CHIPS, tr, cc), lambda i, c_ref: (0, i, 0))],
            out_specs=pl.BlockSpec((None, tr, cc), lambda i, c_ref: (c_ref[0], i, 0))),
        compiler_params=pltpu.CompilerParams(dimension_semantics=("parallel",), vmem_limit_bytes=_vmem(12 * tr * cc * 4)),
    )(c_arr, parts)


def _adamw_math(w, g, m, v):
    m2 = ADAM_B1 * m + (1.0 - ADAM_B1) * g
    v2 = ADAM_B2 * v + (1.0 - ADAM_B2) * jnp.square(g)
    m_hat = m2 / (1.0 - ADAM_B1 ** ADAM_STEP)
    v_hat = v2 / (1.0 - ADAM_B2 ** ADAM_STEP)
    delta = -ADAM_LR * (m_hat / (jnp.sqrt(v_hat) + ADAM_EPS) + ADAM_WD * w)
    return delta, m2, v2


def _adamw_mat(g2, w, m, v, layer, kind, prev, name):
    shape = w.shape
    ks, ns = shape[-2:]
    _, r, cc = g2.shape
    tr, tc = _fit(256, r), _fit(1024, cc)
    assert (r, cc) == ((ks // 2, ns) if kind == "col" else (ks, ns // 2))
    assert r % tr == 0 and cc % tc == 0
    rb, cb = r // tr, cc // tc
    if kind == "col":
        g_spec = pl.BlockSpec((None, tr, tc), lambda i, j: (i // rb, i % rb, j))
    else:
        g_spec = pl.BlockSpec((None, tr, tc), lambda i, j: (j // cb, i, j % cb))
    if w.ndim == 3:
        w_spec = pl.BlockSpec((None, tr, tc), lambda i, j: (layer, i, j))
    else:
        w_spec = pl.BlockSpec((tr, tc), lambda i, j: (i, j))
    n_prev = 0 if prev is None else 4

    def body(*refs):
        g_ref, w_ref, m_ref, v_ref = refs[:4]
        go_ref, d_ref, mo_ref, vo_ref = refs[4 + n_prev:]
        g = g_ref[...]
        delta, m2, v2 = _adamw_math(w_ref[...], g, m_ref[...], v_ref[...])
        go_ref[...] = g
        d_ref[...] = delta
        mo_ref[...] = m2
        vo_ref[...] = v2

    return pl.pallas_call(
        body, name=name, grid=(ks // tr, ns // tc),
        in_specs=[g_spec, w_spec, w_spec, w_spec] + [ANY] * n_prev, out_specs=[w_spec] * 4,
        out_shape=[jax.ShapeDtypeStruct(shape, F32)] * 4,
        input_output_aliases={4 + i: i for i in range(n_prev)},
        compiler_params=pltpu.CompilerParams(dimension_semantics=("parallel", "parallel"),
                                             vmem_limit_bytes=_vmem(16 * tr * tc * 4)),
    )(g2, w, m, v, *(prev or ()))


def _adamw_small(g, w, m, v, name):
    def body(g_ref, w_ref, m_ref, v_ref, d_ref, mo_ref, vo_ref):
        delta, m2, v2 = _adamw_math(w_ref[...], g_ref[...], m_ref[...], v_ref[...])
        d_ref[...] = delta
        mo_ref[...] = m2
        vo_ref[...] = v2

    return pl.pallas_call(body, name=name, out_shape=[jax.ShapeDtypeStruct(w.shape, F32)] * 3)(g, w, m, v)


def _sum8(parts, name):
    def body(p_ref, o_ref):
        acc = p_ref[0]
        for j in range(1, 8):
            acc = acc + p_ref[j]
        o_ref[...] = acc

    return pl.pallas_call(body, name=name, out_shape=jax.ShapeDtypeStruct(parts.shape[1:], F32),
                          compiler_params=pltpu.CompilerParams(vmem_limit_bytes=_vmem(12 * _nbytes(parts.shape[1:], F32))))(parts)


_REDUCE_AT = {
    "d_ple_gate1": (("A", "ple_proj1"),),
    "dh2_1": (("A", "ple_gate1"),),
    "d_mlp_down1": (("B", "ple_proj1"), ("B", "ple_gate1")),
    "dt1": (("A", "mlp_down1"),),
    "d_mlp_up1": (("B", "mlp_down1"), ("C", "ple_proj1"), ("C", "ple_gate1")),
    "dh1_1": (("A", "mlp_up1"),),
    "d_attn_w_o": (("C", "mlp_down1"),),
    "attn_do": (("A", "attn_w_o"),),
    "attn_bwd_g0": (("B", "mlp_up1"),),
    "attn_bwd_g1": (("B", "attn_w_o"),),
    "attn_unrot": (("C", "mlp_up1"), ("C", "attn_w_o")),
    "dx1_q": (("A", "attn_w_q"),),
    "dkvn_k": (("A", "w_kv"),),
    "d_ple_gate0": (("A", "ple_proj0"),),
    "dh2_0": (("A", "ple_gate0"),),
    "d_mlp_down0": (("B", "attn_w_q"), ("B", "ple_proj0")),
    "dt0": (("B", "w_kv"), ("B", "ple_gate0"), ("A", "mlp_down0")),
    "d_mlp_up0": (("B", "mlp_down0"), ("C", "attn_w_q"), ("C", "ple_proj0"), ("C", "w_kv"), ("C", "ple_gate0")),
    "dh1_0": (("A", "mlp_up0"),),
    "d_conv_w_out": (("C", "mlp_down0"),),
    "conv_ds": (("A", "conv_w_out"),),
    "conv_bwd": (("B", "mlp_up0"), ("B", "conv_w_out")),
    "d_conv_w_in_g": (("C", "mlp_up0"), ("C", "conv_w_out")),
    "dx_a": (("A", "conv_w_in"),),
    "dx_g": (("B", "conv_w_in"),),
    "share_last": (("C", "conv_w_in"),),
}


class _Reducer:
    def __init__(self, w, mom, var, c_arr):
        self.w, self.mom, self.var, self.c_arr = w, mom, var, c_arr
        self.mats = {m[0]: m for m in _MATS}
        self.grads, self.pair_sums, self.chip_sums, self.out = {}, {}, {}, {}

    def produced(self, name, grad):
        self.grads[name] = grad

    def carry(self, call):
        parts = []
        for cls, stage, src in ((_PairSend, "A", self.grads), (_ChipScatter, "B", self.pair_sums),
                                (_PairShare, "C", self.chip_sums)):
            names = [n for s, n in _REDUCE_AT.get(call, ()) if s == stage]
            if names:
                parts.append((stage, cls({n: src[n] for n in names})))
        self._parts = parts
        return _Multi([p for _, p in parts]) if parts else None

    def carried(self):
        for stage, part in self._parts:
            for name, val in part.result.items():
                if stage == "A":
                    self.pair_sums[name] = _pair_sum(self.grads[name], val, self.c_arr, f"pair_sum_{name}")
                elif stage == "B":
                    self.chip_sums[name] = _chip_sum(val, self.c_arr, f"chip_sum_{name}")
                else:
                    _, src, layer, kind, _ = self.mats[name]
                    self.out[src] = _adamw_mat(val, self.w[src], self.mom[src], self.var[src], layer or 0, kind,
                                               self.out.get(src), f"adamw_{name}")
        self._parts = []


_WEIGHTS = ("conv_w_in", "conv_b_in", "conv_dw", "conv_dw_b", "conv_ln_g", "conv_ln_b", "conv_w_out", "kv_ln_g",
            "kv_ln_b", "w_kv", "attn_w_q", "attn_w_o", "ln1_g", "ln1_b", "mlp_up", "mlp_down", "ln2_g", "ln2_b",
            "ple_proj", "ple_gate")
_SHARDED_VECS = ("conv_b_in", "conv_dw", "conv_dw_b", "conv_ln_g", "conv_ln_b")
_REPLICATED_VECS = ("kv_ln_g", "kv_ln_b", "ln1_g", "ln1_b", "ln2_g", "ln2_b")


def _pad_rows(a, rows):
    return jnp.concatenate([a, jnp.zeros((rows - a.shape[0], a.shape[1]), a.dtype)], axis=0) if a.shape[0] < rows else a


def _pack_sharded(d):
    n = d["conv_dw_b"].shape[-1]
    rows = [d["conv_b_in"].reshape(2, n), d["conv_dw"].reshape(CONV_WIDTH, n), d["conv_dw_b"].reshape(1, n),
            d["conv_ln_g"].reshape(1, n), d["conv_ln_b"].reshape(1, n)]
    return _pad_rows(jnp.concatenate(rows, axis=0), 40)


def _unpack_sharded(pack, like):
    n = pack.shape[1]
    return {"conv_b_in": pack[0:2].reshape(like["conv_b_in"].shape),
            "conv_dw": pack[2:2 + CONV_WIDTH].reshape(like["conv_dw"].shape),
            "conv_dw_b": pack[33:34].reshape(like["conv_dw_b"].shape),
            "conv_ln_g": pack[34:35].reshape(like["conv_ln_g"].shape),
            "conv_ln_b": pack[35:36].reshape(like["conv_ln_b"].shape)}


def _pack_replicated(d):
    D = d["kv_ln_g"].shape[-1]
    rows = [d[n].reshape(-1, D) for n in _REPLICATED_VECS]
    return _pad_rows(jnp.concatenate(rows, axis=0), 16)


def _unpack_replicated(pack, like):
    out, r = {}, 0
    for n in _REPLICATED_VECS:
        k = like[n].size // pack.shape[1]
        out[n] = pack[r:r + k].reshape(like[n].shape)
        r += k
    return out


def kernel(x, p, positions, conv_w_in, conv_b_in, conv_dw, conv_dw_b, conv_ln_g, conv_ln_b, conv_w_out, kv_ln_g, kv_ln_b, w_kv, attn_w_q, attn_w_o, ln1_g, ln1_b, mlp_up, mlp_down, ln2_g, ln2_b, ple_proj, ple_gate, loss_target, m_conv_w_in, m_conv_b_in, m_conv_dw, m_conv_dw_b, m_conv_ln_g, m_conv_ln_b, m_conv_w_out, m_kv_ln_g, m_kv_ln_b, m_w_kv, m_attn_w_q, m_attn_w_o, m_ln1_g, m_ln1_b, m_mlp_up, m_mlp_down, m_ln2_g, m_ln2_b, m_ple_proj, m_ple_gate, v_conv_w_in, v_conv_b_in, v_conv_dw, v_conv_dw_b, v_conv_ln_g, v_conv_ln_b, v_conv_w_out, v_kv_ln_g, v_kv_ln_b, v_w_kv, v_attn_w_q, v_attn_w_o, v_ln1_g, v_ln1_b, v_mlp_up, v_mlp_down, v_ln2_g, v_ln2_b, v_ple_proj, v_ple_gate):
    args = dict(locals())
    w = {n: args[n] for n in _WEIGHTS}
    mom = {n: args["m_" + n] for n in _WEIGHTS}
    var = {n: args["v_" + n] for n in _WEIGHTS}
    S, D = x.shape[1:]
    n4 = D // N_CHIPS
    chip = 2 * lax.axis_index("x") + lax.axis_index("y")
    c_arr = lax.axis_index("c").astype(jnp.int32).reshape(1)

    shards = {n: w[n].astype(BF16) for n in sorted({m[1] for m in _MATS})}
    vec_all = _allgather8(_pack_sharded(w), "gather_vectors")
    vec_full = jnp.concatenate([vec_all[2 * j] for j in range(N_CHIPS)], axis=1)
    b_in = vec_all[0::2, 0:2, :].reshape(1, 2 * D)
    V = {"conv_b_a": b_in[:, :D], "conv_b_g": b_in[:, D:],
         "conv_dw": _pad_rows(vec_full[2:2 + CONV_WIDTH], CONV_PAD), "conv_dw_b": vec_full[33:34],
         "conv_ln_g": vec_full[34:35], "conv_ln_b": vec_full[35:36],
         "kv_ln_g": kv_ln_g.reshape(1, D), "kv_ln_b": kv_ln_b.reshape(1, D)}
    for l in range(2):
        for n in ("ln1_g", "ln1_b", "ln2_g", "ln2_b"):
            V[f"{n}{l}"] = w[n][l].reshape(1, D)

    half = HEAD_DIM // 2
    inv_freq = ROPE_THETA ** (-jnp.arange(half, dtype=F32) * (2.0 / HEAD_DIM))
    ang = positions[0].astype(F32)[:, None] * inv_freq
    cos, sin = jnp.cos(ang), jnp.sin(ang)
    cosf = jnp.concatenate([cos, cos], axis=-1)
    sinf = jnp.concatenate([-sin, sin], axis=-1)

    reducer = _Reducer(w, mom, var, c_arr)
    loss_cols, grad_x, _, gv = _local_step(x[0], p[:, 0], cosf, sinf, loss_target[0], None, V, shards, reducer,
                                           chip.astype(jnp.int32).reshape(1))
    loss = lax.psum(jnp.sum(loss_cols), ("x", "y", "c"))
    out = dict(reducer.out)

    gpack = jnp.concatenate([gv["conv_b_a"], gv["conv_b_g"], gv["conv_dw"][:CONV_WIDTH], gv["conv_dw_b"],
                             gv["conv_ln_g"], gv["conv_ln_b"], gv["kv_ln_g"], gv["kv_ln_b"],
                             gv["ln1_g0"], gv["ln1_g1"], gv["ln1_b0"], gv["ln1_b1"],
                             gv["ln2_g0"], gv["ln2_g1"], gv["ln2_b0"], gv["ln2_b1"]], axis=0)
    gsum = _sum8(_allgather8(_pad_rows(gpack, 48), "gather_vector_grads"), "sum_vector_grads")
    g_b = lax.dynamic_slice_in_dim(jnp.concatenate([gsum[0:1], gsum[1:2]], axis=1), chip * 2 * n4, 2 * n4, axis=1)
    g_sh = lax.dynamic_slice_in_dim(gsum[2:36], chip * n4, n4, axis=1)
    g_sh = _pad_rows(jnp.concatenate([g_b.reshape(2, n4), g_sh], axis=0), 40)
    d_sh, m_sh, v_sh = _adamw_small(g_sh, _pack_sharded(w), _pack_sharded(mom), _pack_sharded(var), "adamw_sharded_vectors")
    g_rep = _pad_rows(gsum[36:46], 16)
    d_rep, m_rep, v_rep = _adamw_small(g_rep, _pack_replicated(w), _pack_replicated(mom), _pack_replicated(var),
                                       "adamw_replicated_vectors")
    small = {}
    for i, (sh, rep) in enumerate(((g_sh, g_rep), (d_sh, d_rep), (m_sh, m_rep), (v_sh, v_rep))):
        d = {**_unpack_sharded(sh, w), **_unpack_replicated(rep, w)}
        for n, val in d.items():
            small.setdefault(n, [None] * 4)[i] = val
    for n in small:
        out[n] = small[n]

    res = [loss, grad_x[None]]
    for i in range(4):
        res += [out[n][i] for n in _WEIGHTS]
    return tuple(res)
```

```python
import functools

import jax
import jax.numpy as jnp
from jax import lax
from jax.experimental import pallas as pl
from jax.experimental.pallas import tpu as pltpu

F32 = jnp.float32
BF16 = jnp.bfloat16

HEAD_DIM = 128
ATTN_BLOCK = 128
DILATIONS = (1, 4, 16)
N_GROUPS = 3
CONV_WIDTH = 31
CONV_PAD = 32
CONV_ROWS = 32
EPILOGUE_ROWS = 128
ROPE_THETA = 10000.0
LN_EPS = 1e-5
ALPHA = 4.0 ** 0.25
ATTN_SCALE = HEAD_DIM ** -0.5
NEG = -1e30

ADAM_LR = 0.001
ADAM_B1 = 0.9
ADAM_B2 = 0.999
ADAM_EPS = 1e-08
ADAM_WD = 0.01
ADAM_STEP = 10

N_CHIPS = 4
VMEM_CAP = 60 << 20
MESH = pl.DeviceIdType.MESH
ANY = pl.BlockSpec(memory_space=pl.ANY)


def _vmem(nbytes):
    return int(min(max(2 * nbytes + (8 << 20), 24 << 20), VMEM_CAP))


def _fit(tile, n):
    if n <= tile:
        return n
    t = tile - tile % 128
    while n % t:
        t -= 128
    return t


def _nbytes(shape, dtype):
    n = 1
    for s in shape:
        n *= s
    return n * jnp.dtype(dtype).itemsize


_DIMS = {"nn": (((1,), (0,)), ((), ())), "nt": (((1,), (1,)), ((), ())), "tn": (((0,), (0,)), ((), ()))}


def _pcall(body, *, name, grid, in_specs, out_specs, out_shape, operands, scratch_shapes=(), vmem, carry=None):
    if carry is None:
        return pl.pallas_call(
            body, name=name, grid=grid, in_specs=in_specs, out_specs=out_specs, out_shape=out_shape,
            scratch_shapes=list(scratch_shapes),
            compiler_params=pltpu.CompilerParams(dimension_semantics=("arbitrary",) * len(grid), vmem_limit_bytes=vmem),
        )(*operands)
    n_in, n_out, n_scr = len(in_specs), len(out_specs), len(scratch_shapes)
    c_in, c_out = len(carry.ins), len(carry.out_shape)

    def wrapped(*refs):
        ins, refs = refs[:n_in], refs[n_in:]
        c_ins, refs = refs[:c_in], refs[c_in:]
        outs, refs = refs[:n_out], refs[n_out:]
        c_outs, refs = refs[:c_out], refs[c_out:]
        scr, c_sems = refs[:n_scr], refs[n_scr:]
        first = functools.reduce(jnp.logical_and, [pl.program_id(d) == 0 for d in range(len(grid))])
        last = functools.reduce(jnp.logical_and, [pl.program_id(d) == grid[d] - 1 for d in range(len(grid))])
        pl.when(first)(lambda: carry.start(c_ins, c_outs, c_sems))
        body(*ins, *outs, *scr)
        pl.when(last)(lambda: carry.finish(c_ins, c_outs, c_sems))

    res = pl.pallas_call(
        wrapped, name=name, grid=grid, in_specs=list(in_specs) + [ANY] * c_in, out_specs=list(out_specs) + [ANY] * c_out,
        out_shape=list(out_shape) + list(carry.out_shape), scratch_shapes=list(scratch_shapes) + list(carry.scratch),
        input_output_aliases={len(operands) + i: n_out + o for i, o in carry.aliases.items()},
        compiler_params=pltpu.CompilerParams(dimension_semantics=("arbitrary",) * len(grid), vmem_limit_bytes=vmem),
    )(*operands, *carry.ins)
    carry.set_result(res[n_out:])
    return res[:n_out]


def _mm(a, b, *, mode, outs, name, epilogue=None, extras=(), rextras=(), vecs=(), a_sel=None, b_sel=None,
        tm=None, tn=2048, tk=None, layout=None, carry=None, ep_rows=None, a_fn=None):
    a2, b2 = a.shape[-2:], b.shape[-2:]
    if mode == "nn":
        (M, K), (K2, N) = a2, b2
    elif mode == "nt":
        (M, K), (N, K2) = a2, b2
    else:
        (K, M), (K2, N) = a2, b2
    assert K == K2, (a.shape, b.shape, mode)
    if tm is None:
        tm = 1024 if mode == "tn" else 512
    if tk is None:
        tk = 1024 if mode == "tn" else 2048
    if layout is not None:
        kind, nslots = layout
        r, c = (M // 2, N // nslots) if kind == "col" else (M // nslots, N // 2)
        tm, tn = _fit(tm, r), _fit(tn, c)
        assert r % tm == 0 and c % tn == 0
    else:
        tm, tn = _fit(tm, M), _fit(tn, N)
    tk = _fit(tk, K)
    assert M % tm == 0 and N % tn == 0 and K % tk == 0, (M, N, K, tm, tn, tk)
    nk = K // tk
    grid = (N // tn, M // tm, nk)

    def spec(arr, sel, blk, imap):
        if arr.ndim == 3:
            return pl.BlockSpec((None,) + blk, lambda j, i, k: (sel,) + imap(j, i, k))
        return pl.BlockSpec(blk, imap)

    if mode == "tn":
        a_spec = spec(a, a_sel, (tk, tm), lambda j, i, k: (k, i))
    else:
        a_spec = spec(a, a_sel, (tm, tk), lambda j, i, k: (i, k))
    if mode == "nt":
        b_spec = spec(b, b_sel, (tn, tk), lambda j, i, k: (j, k))
    else:
        b_spec = spec(b, b_sel, (tk, tn), lambda j, i, k: (k, j))
    in_specs = [a_spec, b_spec]
    in_specs += [pl.BlockSpec((tm, tn), lambda j, i, k: (i, j)) for _ in extras]
    in_specs += [pl.BlockSpec((tm, e.shape[1]), lambda j, i, k: (i, 0)) for e in rextras]
    in_specs += [pl.BlockSpec((1, tn), lambda j, i, k: (0, j)) for _ in vecs]

    if layout is None:
        out_shape = [jax.ShapeDtypeStruct((M, N), d) for d in outs]
        out_specs = [pl.BlockSpec((tm, tn), lambda j, i, k: (i, j)) for _ in outs]
    else:
        assert len(outs) == 1
        out_shape = [jax.ShapeDtypeStruct((2, nslots, r, c), outs[0])]
        rb, cb = r // tm, c // tn
        if kind == "col":
            omap = lambda j, i, k: (i // rb, j // cb, i % rb, j % cb)
        else:
            omap = lambda j, i, k: (j // cb, i // rb, i % rb, j % cb)
        out_specs = [pl.BlockSpec((None, None, tm, tn), omap)]

    ne, nr, nv, no = len(extras), len(rextras), len(vecs), len(outs)
    dims = _DIMS[mode]

    def body(*refs):
        a_ref, b_ref = refs[0], refs[1]
        rest = refs[2:2 + ne + nr + nv]
        o_refs = refs[2 + ne + nr + nv:2 + ne + nr + nv + no]

        def finish(total):
            if epilogue is None:
                for o in o_refs:
                    o[...] = total.astype(o.dtype)
                return
            step = min(ep_rows or tm, tm)
            for r0 in range(0, tm, step):
                rows = slice(r0, r0 + step)
                tiles = [x[rows, :] for x in rest[:ne + nr]] + [x[...] for x in rest[ne + nr:]]
                for o, val in zip(o_refs, epilogue(total[rows, :], *tiles)):
                    o[rows, :] = val.astype(o.dtype)

        def product():
            a_tile = a_ref[...] if a_fn is None else a_fn(a_ref[...])
            return lax.dot_general(a_tile.astype(BF16), b_ref[...].astype(BF16), dims, preferred_element_type=F32)

        if nk == 1:
            finish(product())
            return
        acc = refs[-1]
        k = pl.program_id(2)

        @pl.when(k == 0)
        def _():
            acc[...] = product()

        @pl.when(k > 0)
        def _():
            acc[...] += product()

        @pl.when(k == nk - 1)
        def _():
            finish(acc[...])

    blk = (_nbytes((tm, tk), a.dtype) + _nbytes((tk, tn), b.dtype) + sum(_nbytes((tm, tn), e.dtype) for e in extras)
           + sum(_nbytes((tm, tn), d) for d in outs) + 2 * tm * tn * 4)
    res = _pcall(body, name=name, grid=grid, in_specs=in_specs, out_specs=out_specs, out_shape=out_shape,
                 operands=(a, b, *extras, *rextras, *vecs),
                 scratch_shapes=[pltpu.VMEM((tm, tn), F32)] if nk > 1 else [], vmem=_vmem(blk), carry=carry)
    return res[0] if no == 1 else tuple(res)


def _rows(fn, rows, vecs, outs, sums, *, tm, name, carry=None):
    S = rows[0].shape[0]
    tm = min(tm, S)
    assert S % tm == 0
    nr, nv, no, ns = len(rows), len(vecs), len(outs), len(sums)

    def body(*refs):
        vals = fn(*[r[...] for r in refs[:nr + nv]])
        o_refs = refs[nr + nv:nr + nv + no]
        s_refs = refs[nr + nv + no:]
        for o, val in zip(o_refs, vals[:no]):
            o[...] = val.astype(o.dtype)
        if ns:
            @pl.when(pl.program_id(0) == 0)
            def _():
                for s in s_refs:
                    s[...] = jnp.zeros_like(s)

            for s, val in zip(s_refs, vals[no:]):
                s[...] += jnp.sum(val.astype(F32), axis=0, keepdims=True)

    in_specs = [pl.BlockSpec((tm, r.shape[1]), lambda i: (i, 0)) for r in rows]
    in_specs += [pl.BlockSpec(v.shape, lambda i: (0, 0)) for v in vecs]
    out_specs = [pl.BlockSpec((tm, c), lambda i: (i, 0)) for c, _ in outs]
    out_specs += [pl.BlockSpec((1, c), lambda i: (0, 0)) for c in sums]
    out_shape = [jax.ShapeDtypeStruct((S, c), d) for c, d in outs]
    out_shape += [jax.ShapeDtypeStruct((1, c), F32) for c in sums]
    blk = sum(_nbytes((tm, r.shape[1]), r.dtype) for r in rows) + sum(_nbytes((tm, c), d) for c, d in outs)
    blk += 6 * tm * max(r.shape[1] for r in rows) * 4
    res = _pcall(body, name=name, grid=(S // tm,), in_specs=in_specs, out_specs=out_specs, out_shape=out_shape,
                 operands=(*rows, *vecs), vmem=_vmem(blk), carry=carry)
    return tuple(res)


def _ln_norm(z):
    mu = jnp.mean(z, axis=-1, keepdims=True)
    d = z - mu
    var = jnp.mean(d * d, axis=-1, keepdims=True)
    rstd = lax.rsqrt(var + LN_EPS)
    return d * rstd, rstd


def _ln(z, g, b):
    return _ln_norm(z)[0] * g + b


def _ln_bwd(dy, n, rstd, g):
    dn = dy * g
    return rstd * (dn - jnp.mean(dn, axis=-1, keepdims=True) - n * jnp.mean(dn * n, axis=-1, keepdims=True))


def _sq_relu(t):
    return jnp.square(jnp.maximum(t.astype(F32), 0.0))


def _sigmoid(x):
    return 1.0 / (1.0 + jnp.exp(-x))


def _per_head(x, fn):
    h = x.shape[1] // HEAD_DIM
    return jnp.concatenate([fn(x[:, i * HEAD_DIM:(i + 1) * HEAD_DIM], i) for i in range(h)], axis=1)


def _rot(x, cosf, sinf):
    return _per_head(x, lambda xh, i: xh * cosf + pltpu.roll(xh, HEAD_DIM // 2, 1) * sinf)


def _rot_t(dy, cosf, sinf):
    return _per_head(dy, lambda dh, i: dh * cosf + pltpu.roll(dh * sinf, HEAD_DIM // 2, 1))


def _shift_copies(win):
    rows = win.shape[1] - 8
    for s in range(1, 8):
        win[s, 0:rows, :] = win[0, s:s + rows, :]


def _rows_at(win, start):
    s = start % 8
    return win[s, start - s:start - s + CONV_ROWS, :]


def _conv_fwd(glu, dw, dwb, *, tm=256, tc=512, name="conv_fwd", carry=None):
    S, D = glu.shape
    tm, tc = min(tm, S), min(tc, D)
    ni = S // tm

    def body(cur_ref, prev_ref, dw_ref, dwb_ref, o_ref, win):
        i = pl.program_id(1)
        tail = prev_ref[tm - CONV_PAD:tm, :]
        win[0, 0:CONV_PAD, :] = jnp.where(i > 0, tail, jnp.zeros_like(tail))
        win[0, CONV_PAD:CONV_PAD + tm, :] = cur_ref[...]
        _shift_copies(win)
        first = CONV_PAD - CONV_WIDTH + 1
        for r0 in range(0, tm, CONV_ROWS):
            acc = jnp.zeros((CONV_ROWS, tc), F32) + dwb_ref[...]
            for k in range(CONV_WIDTH):
                acc = acc + _rows_at(win, r0 + first + k) * dw_ref[k:k + 1, :]
            o_ref[r0:r0 + CONV_ROWS, :] = acc

    return _pcall(
        body, name=name, grid=(D // tc, ni),
        in_specs=[pl.BlockSpec((tm, tc), lambda j, i: (i, j)),
                  pl.BlockSpec((tm, tc), lambda j, i: (jnp.maximum(i - 1, 0), j)),
                  pl.BlockSpec((CONV_PAD, tc), lambda j, i: (0, j)),
                  pl.BlockSpec((1, tc), lambda j, i: (0, j))],
        out_specs=[pl.BlockSpec((tm, tc), lambda j, i: (i, j))],
        out_shape=[jax.ShapeDtypeStruct((S, D), F32)],
        scratch_shapes=[pltpu.VMEM((8, tm + CONV_PAD, tc), F32)],
        operands=(glu, glu, dw, dwb), vmem=_vmem(8 * tm * tc * 4), carry=carry)[0]


def _conv_bwd(dc, glu, a_pre, g_pre, dw, ba, bg, *, tm=256, tc=512, name="conv_bwd", carry=None):
    S, D = dc.shape
    tm, tc = min(tm, S), min(tc, D)
    ni = S // tm

    def fold8(v):
        out = v[0:8]
        for r in range(8, CONV_ROWS, 8):
            out = out + v[r:r + 8]
        return out

    def body(dc_ref, dcn_ref, glu_ref, glup_ref, a_ref, g_ref, dw_ref, ba_ref, bg_ref,
             da_ref, dg_ref, ddw_ref, dba_ref, dbg_ref, dwin, gwin, taps):
        i = pl.program_id(1)

        @pl.when(i == 0)
        def _():
            ddw_ref[...] = jnp.zeros_like(ddw_ref)
            dba_ref[...] = jnp.zeros_like(dba_ref)
            dbg_ref[...] = jnp.zeros_like(dbg_ref)

        head = dcn_ref[0:CONV_PAD, :]
        dwin[0, 0:tm, :] = dc_ref[...]
        dwin[0, tm:tm + CONV_PAD, :] = jnp.where(i < ni - 1, head, jnp.zeros_like(head))
        tail = glup_ref[tm - CONV_PAD:tm, :]
        gwin[0, 0:CONV_PAD, :] = jnp.where(i > 0, tail, jnp.zeros_like(tail))
        gwin[0, CONV_PAD:CONV_PAD + tm, :] = glu_ref[...]
        _shift_copies(dwin)
        _shift_copies(gwin)
        taps[...] = jnp.zeros_like(taps)
        first = CONV_PAD - CONV_WIDTH + 1
        sum_a = jnp.zeros((8, tc), F32)
        sum_g = jnp.zeros((8, tc), F32)
        for r0 in range(0, tm, CONV_ROWS):
            dcur = dc_ref[r0:r0 + CONV_ROWS, :]
            dglu = jnp.zeros((CONV_ROWS, tc), F32)
            for k in range(CONV_WIDTH):
                dglu = dglu + _rows_at(dwin, r0 + CONV_WIDTH - 1 - k) * dw_ref[k:k + 1, :]
                taps[k] += fold8(dcur * _rows_at(gwin, r0 + first + k))
            a = a_ref[r0:r0 + CONV_ROWS, :] + ba_ref[...]
            sg = _sigmoid(g_ref[r0:r0 + CONV_ROWS, :] + bg_ref[...])
            da = dglu * sg
            dg = dglu * a * sg * (1.0 - sg)
            da_ref[r0:r0 + CONV_ROWS, :] = da.astype(BF16)
            dg_ref[r0:r0 + CONV_ROWS, :] = dg.astype(BF16)
            sum_a = sum_a + fold8(da)
            sum_g = sum_g + fold8(dg)
        ddw_ref[...] += jnp.sum(taps[...], axis=1)
        dba_ref[...] += jnp.sum(sum_a, axis=0, keepdims=True)
        dbg_ref[...] += jnp.sum(sum_g, axis=0, keepdims=True)

    tile = lambda f: pl.BlockSpec((tm, tc), f)
    vec = pl.BlockSpec((1, tc), lambda j, i: (0, j))
    return _pcall(
        body, name=name, grid=(D // tc, ni),
        in_specs=[tile(lambda j, i: (i, j)), tile(lambda j, i: (jnp.minimum(i + 1, ni - 1), j)),
                  tile(lambda j, i: (i, j)), tile(lambda j, i: (jnp.maximum(i - 1, 0), j)),
                  tile(lambda j, i: (i, j)), tile(lambda j, i: (i, j)),
                  pl.BlockSpec((CONV_PAD, tc), lambda j, i: (0, j)), vec, vec],
        out_specs=[tile(lambda j, i: (i, j)), tile(lambda j, i: (i, j)),
                   pl.BlockSpec((CONV_PAD, tc), lambda j, i: (0, j)), vec, vec],
        out_shape=[jax.ShapeDtypeStruct((S, D), BF16), jax.ShapeDtypeStruct((S, D), BF16),
                   jax.ShapeDtypeStruct((CONV_PAD, D), F32), jax.ShapeDtypeStruct((1, D), F32),
                   jax.ShapeDtypeStruct((1, D), F32)],
        scratch_shapes=[pltpu.VMEM((8, tm + CONV_PAD, tc), F32), pltpu.VMEM((8, tm + CONV_PAD, tc), F32),
                        pltpu.VMEM((CONV_PAD, 8, tc), F32)],
        operands=(dc, dc, glu, glu, a_pre, g_pre, dw, ba, bg), vmem=_vmem(16 * tm * tc * 4), carry=carry)


def _nt(a, b):
    return lax.dot_general(a, b, _DIMS["nt"], preferred_element_type=F32)


def _tn(a, b):
    return lax.dot_general(a, b, _DIMS["tn"], preferred_element_type=F32)


STAT_LANES = 128


def _per_head_pack(cols):
    rows = cols[0].shape[0]
    lane = lax.broadcasted_iota(jnp.int32, (rows, STAT_LANES), 1)
    out = jnp.zeros((rows, STAT_LANES), F32)
    for h, col in enumerate(cols):
        out = jnp.where(lane == h, col, out)
    return out


def _window_mask(qi, kj, first_key):
    B = ATTN_BLOCK
    return ((kj < B) & (kj >= qi) & (kj >= first_key)) | ((kj >= B) & (kj - B <= qi))


def _attn_fwd(q_rot, k, v, g, dil):
    S, D = k.shape
    H = D // HEAD_DIM
    L = S // dil
    nb_count = L // ATTN_BLOCK
    B = ATTN_BLOCK

    def body(q_ref, kc_ref, kp_ref, vc_ref, vp_ref, o_ref, lse_ref):
        nb = pl.program_id(1)
        qi = lax.broadcasted_iota(jnp.int32, (B, 2 * B), 0)
        kj = lax.broadcasted_iota(jnp.int32, (B, 2 * B), 1)
        valid = _window_mask(qi, kj, jnp.where(nb > 0, 0, B))
        stats = []
        for h in range(H):
            hs = slice(h * HEAD_DIM, (h + 1) * HEAD_DIM)
            kk = jnp.concatenate([kp_ref[:, hs], kc_ref[:, hs]], axis=0)
            vv = jnp.concatenate([vp_ref[:, hs], vc_ref[:, hs]], axis=0)
            s = jnp.where(valid, _nt(q_ref[:, hs], kk) * ATTN_SCALE, NEG)
            m = jnp.max(s, axis=1, keepdims=True)
            p = jnp.exp(s - m)
            l = jnp.sum(p, axis=1, keepdims=True)
            o = jnp.dot(p.astype(BF16), vv, preferred_element_type=F32) / l
            o_ref[:, hs] = o.astype(o_ref.dtype)
            stats.append(m + jnp.log(l))
        lse_ref[...] = _per_head_pack(stats)

    blk = lambda f: pl.BlockSpec((B, D), f)
    cur = lambda r, nb: (nb, r)
    prev = lambda r, nb: (jnp.maximum(nb - 1, 0), r)
    o, lse = pl.pallas_call(
        body, name=f"attn_fwd_g{g}", grid=(dil, nb_count),
        in_specs=[blk(lambda r, nb: (nb, r * N_GROUPS + g)), blk(cur), blk(prev), blk(cur), blk(prev)],
        out_specs=[blk(cur), pl.BlockSpec((B, STAT_LANES), cur)],
        out_shape=[jax.ShapeDtypeStruct((L, dil * D), BF16), jax.ShapeDtypeStruct((L, dil * STAT_LANES), F32)],
        compiler_params=pltpu.CompilerParams(dimension_semantics=("parallel", "arbitrary"),
                                             vmem_limit_bytes=_vmem(12 * B * D * 4)),
    )(q_rot.reshape(L, dil * N_GROUPS * D), k.reshape(L, dil * D), k.reshape(L, dil * D),
      v.reshape(L, dil * D), v.reshape(L, dil * D))
    return o.reshape(S, D), lse.reshape(S, STAT_LANES)


def _attn_bwd(q_rot, k, v, do, lse, dlt, g, dil, *, name, carry=None):
    S, D = k.shape
    H = D // HEAD_DIM
    L = S // dil
    nb_count = L // ATTN_BLOCK
    B = ATTN_BLOCK

    def body(q_ref, qn_ref, kc_ref, kp_ref, vc_ref, vp_ref, do_ref, don_ref, l_ref, ln_ref, d_ref, dn_ref,
             dq_ref, dk_ref, dv_ref):
        nb = pl.program_id(1)
        qi = lax.broadcasted_iota(jnp.int32, (B, 2 * B), 0)
        kj = lax.broadcasted_iota(jnp.int32, (B, 2 * B), 1)
        valid_q = _window_mask(qi, kj, jnp.where(nb > 0, 0, B))
        qr = lax.broadcasted_iota(jnp.int32, (2 * B, B), 0)
        kc_i = lax.broadcasted_iota(jnp.int32, (2 * B, B), 1)
        q_end = jnp.where(nb < nb_count - 1, 2 * B, B)
        valid_k = ((qr < B) & (kc_i <= qr)) | ((qr >= B) & (qr < q_end) & (kc_i >= qr - B))
        for h in range(H):
            hs = slice(h * HEAD_DIM, (h + 1) * HEAD_DIM)
            q, qn = q_ref[:, hs], qn_ref[:, hs]
            kc, vc = kc_ref[:, hs], vc_ref[:, hs]
            dout, doutn = do_ref[:, hs], don_ref[:, hs]
            lq, lqn = l_ref[:, h:h + 1], ln_ref[:, h:h + 1]
            dq_, dqn_ = d_ref[:, h:h + 1], dn_ref[:, h:h + 1]
            kk = jnp.concatenate([kp_ref[:, hs], kc], axis=0)
            vv = jnp.concatenate([vp_ref[:, hs], vc], axis=0)
            s = jnp.where(valid_q, _nt(q, kk) * ATTN_SCALE, NEG)
            p = jnp.exp(s - lq)
            ds = p * (_nt(dout, vv) - dq_)
            dq_ref[:, hs] = (jnp.dot(ds.astype(BF16), kk, preferred_element_type=F32) * ATTN_SCALE).astype(dq_ref.dtype)
            qq = jnp.concatenate([q, qn], axis=0)
            dd = jnp.concatenate([dout, doutn], axis=0)
            ll = jnp.concatenate([lq, lqn], axis=0)
            dl = jnp.concatenate([dq_, dqn_], axis=0)
            s2 = jnp.where(valid_k, _nt(qq, kc) * ATTN_SCALE, NEG)
            p2 = jnp.exp(s2 - ll)
            dv_ref[:, hs] = _tn(p2.astype(BF16), dd).astype(dv_ref.dtype)
            ds2 = p2 * (_nt(dd, vc) - dl)
            dk_ref[:, hs] = (_tn(ds2.astype(BF16), qq) * ATTN_SCALE).astype(dk_ref.dtype)

    blk = lambda f: pl.BlockSpec((B, D), f)
    cur = lambda r, nb: (nb, r)
    prev = lambda r, nb: (jnp.maximum(nb - 1, 0), r)
    nxt = lambda r, nb: (jnp.minimum(nb + 1, nb_count - 1), r)
    qcur = lambda r, nb: (nb, r * N_GROUPS + g)
    qnxt = lambda r, nb: (jnp.minimum(nb + 1, nb_count - 1), r * N_GROUPS + g)
    qv = q_rot.reshape(L, dil * N_GROUPS * D)
    view = lambda t: t.reshape(L, dil * D)
    sview = lambda t: t.reshape(L, dil * STAT_LANES)
    stat = lambda f: pl.BlockSpec((B, STAT_LANES), f)
    dq, dk, dv = _pcall(
        body, name=name, grid=(dil, nb_count),
        in_specs=[blk(qcur), blk(qnxt), blk(cur), blk(prev), blk(cur), blk(prev), blk(cur), blk(nxt),
                  stat(cur), stat(nxt), stat(cur), stat(nxt)],
        out_specs=[blk(cur), blk(cur), blk(cur)],
        out_shape=[jax.ShapeDtypeStruct((L, dil * D), BF16)] * 3,
        operands=(qv, qv, view(k), view(k), view(v), view(v), view(do), view(do), sview(lse), sview(lse), sview(dlt),
                  sview(dlt)),
        vmem=_vmem(24 * B * D * 4), carry=carry)
    return dq.reshape(S, D), dk.reshape(S, D), dv.reshape(S, D)


def _mlp_ple_fwd(z1, h1b, p_l, W, vec, l, run, kv_vec=None):
    D = z1.shape[1]
    g1, b1, g2, b2 = vec

    t = run(_mm, h1b, W[f"mlp_up{l}"], mode="nn", outs=[BF16], name=f"mlp_up{l}")

    def z2_ep(acc, z1_t, g1_, b1_, g2_, b2_):
        z2 = ALPHA * _ln(z1_t, g1_, b1_) + acc
        return z2, _ln(z2, g2_, b2_)

    z2, h2b = run(_mm, t, W[f"mlp_down{l}"], mode="nn", outs=[F32, BF16], extras=[z1], vecs=[g1, b1, g2, b2], a_fn=_sq_relu,
                  epilogue=z2_ep, ep_rows=EPILOGUE_ROWS, name=f"mlp_down{l}")
    act = None
    pp = run(_mm, p_l, W[f"ple_proj{l}"], mode="nn", outs=[F32], name=f"ple_proj{l}")
    if kv_vec is None:
        gpre = run(_mm, h2b, W[f"ple_gate{l}"], mode="nn", outs=[F32], name=f"ple_gate{l}")
        return t, act, z2, h2b, pp, gpre

    def x1_ep(acc, z2_t, pp_t, g2_, b2_, kg, kb):
        x1, _ = _ple_out(z2_t, pp_t, acc, g2_, b2_)
        return acc, x1, _ln(x1, kg, kb)

    gpre, x1, kvn = run(_mm, h2b, W[f"ple_gate{l}"], mode="nn", outs=[F32, F32, BF16], extras=[z2, pp],
                        vecs=[g2, b2, *kv_vec], epilogue=x1_ep, ep_rows=EPILOGUE_ROWS, tm=256, name=f"ple_gate{l}")
    return t, act, z2, h2b, pp, gpre, x1, kvn


def _mlp_ple_bwd(dy, d_pp, d_gpre, p_l, z1, h1b, t, act, z2, h2b, wts, vec, l, run, produce):
    D = z1.shape[1]
    up, down, pp_w, pg_w = wts
    g1, b1, g2, b2 = vec
    produce(f"ple_proj{l}", run(_mm, p_l, d_pp, mode="tn", outs=[BF16], layout=("col", N_CHIPS), name=f"d_ple_proj{l}"))
    produce(f"ple_gate{l}", run(_mm, h2b, d_gpre, mode="tn", outs=[BF16], layout=("row", N_CHIPS), name=f"d_ple_gate{l}"))
    dh2 = run(_mm, d_gpre, pg_w, mode="nt", outs=[F32], extras=[dy], epilogue=lambda acc, e: (acc + e,),
              name=f"dh2_{l}")

    def ln2_bwd(dh2_t, z2_t, g2_):
        n, rstd = _ln_norm(z2_t)
        dz2 = _ln_bwd(dh2_t, n, rstd, g2_)
        return dz2, dz2, dh2_t * n, dh2_t

    dz2, dz2b, dg2, db2 = _rows(ln2_bwd, [dh2, z2], [g2], [(D, F32), (D, BF16)], [D, D], tm=256,
                                name=f"ln2_bwd{l}")
    produce(f"mlp_down{l}", run(_mm, t, dz2b, mode="tn", outs=[BF16], layout=("row", N_CHIPS), a_fn=_sq_relu,
                                name=f"d_mlp_down{l}"))
    dt = run(_mm, dz2b, down, mode="nt", outs=[BF16], extras=[t],
             epilogue=lambda acc, t_: (acc * 2.0 * jnp.maximum(t_.astype(F32), 0.0),), name=f"dt{l}")
    produce(f"mlp_up{l}", run(_mm, h1b, dt, mode="tn", outs=[BF16], layout=("col", N_CHIPS), name=f"d_mlp_up{l}"))
    dh1 = run(_mm, dt, up, mode="nt", outs=[F32], extras=[dz2], epilogue=lambda acc, e: (acc + ALPHA * e,),
              name=f"dh1_{l}")

    def ln1_bwd(dh1_t, z1_t, g1_):
        n, rstd = _ln_norm(z1_t)
        dz1 = _ln_bwd(dh1_t, n, rstd, g1_)
        return dz1, dz1, dh1_t * n, dh1_t

    dz1, dz1b, dg1, db1 = _rows(ln1_bwd, [dh1, z1], [g1], [(D, F32), (D, BF16)], [D, D], tm=256,
                                name=f"ln1_bwd{l}")
    return dz1, dz1b, (dg1, db1, dg2, db2)


def _ple_out(z2, pp, gpre, g2, b2):
    gt = _sigmoid(gpre)
    return _ln(z2, g2, b2) + pp * gt, gt


_GATHER_AT = {
    "conv_in_a": ("conv_w_out",),
    "conv_in_g": ("ple_gate0", "ple_proj0"),
    "conv_fwd": ("mlp_up0",),
    "mlp_up0": ("mlp_down0",),
    "mlp_down0": ("attn_w_q", "w_kv"),
    "kv_k": ("attn_w_o",),
    "kv_v": ("ple_proj1", "ple_gate1"),
    "attn_q": ("mlp_up1",),
    "mlp_up1": ("mlp_down1",),
}
_GATHER_FIRST = ("conv_w_in",)


def _local_step(x, p, cosf, sinf, target, W, V, shards=None, reducer=None, chip_arr=None):
    S, D = x.shape
    gw, gv = {}, {}
    if shards is not None:
        W = dict(_Gather(_GATHER_FIRST, shards, chip_arr).run_alone("gather_first"))

    def run(fn, *args, name, **kw):
        gather = _Gather(_GATHER_AT[name], shards, chip_arr) if (shards is not None and name in _GATHER_AT) else None
        carry = gather if reducer is None or gather is not None else reducer.carry(name)
        out = fn(*args, name=name, carry=carry, **kw)
        if gather is not None:
            W.update(gather.result)
        elif reducer is not None:
            reducer.carried()
        return out

    a_pre = run(_mm, x, W["conv_w_in"], b_sel=0, mode="nn", outs=[F32], name="conv_in_a")
    g_pre = run(_mm, x, W["conv_w_in"], b_sel=1, mode="nn", outs=[F32], name="conv_in_g")
    (glu,) = _rows(lambda a, g, ba, bg: ((a + ba) * _sigmoid(g + bg),), [a_pre, g_pre], [V["conv_b_a"], V["conv_b_g"]],
                   [(D, F32)], [], tm=256, name="glu_fwd")
    cv = run(_conv_fwd, glu, V["conv_dw"], V["conv_dw_b"], name="conv_fwd")

    def silu_ln(c, g_, b_):
        y = _ln(c, g_, b_)
        return (y * _sigmoid(y),)

    (sb,) = _rows(silu_ln, [cv], [V["conv_ln_g"], V["conv_ln_b"]], [(D, BF16)], [], tm=256, name="conv_ln_fwd")
    def z1_ep(acc, x_t, g_, b_):
        z1 = ALPHA * x_t + acc
        return z1, _ln(z1, g_, b_)

    vec0 = (V["ln1_g0"], V["ln1_b0"], V["ln2_g0"], V["ln2_b0"])
    vec1 = (V["ln1_g1"], V["ln1_b1"], V["ln2_g1"], V["ln2_b1"])
    z1_0, h1b_0 = _mm(sb, W["conv_w_out"], mode="nn", outs=[F32, BF16], extras=[x], vecs=[vec0[0], vec0[1]],
                      epilogue=z1_ep, ep_rows=EPILOGUE_ROWS, name="conv_out")
    t0, act0, z2_0, h2b_0, pp0, gpre0, x1, kvn = _mlp_ple_fwd(z1_0, h1b_0, p[0], W, vec0, 0, run,
                                                              (V["kv_ln_g"], V["kv_ln_b"]))

    rot_ep = lambda acc, c_, s_: (_rot(acc, c_, s_),)
    k_rot = run(_mm, kvn, W["w_kv"], b_sel=0, mode="nn", outs=[BF16], rextras=[cosf, sinf], epilogue=rot_ep, name="kv_k")
    v_b = run(_mm, kvn, W["w_kv"], b_sel=1, mode="nn", outs=[BF16], name="kv_v")
    q_rot = run(_mm, x1, W["attn_w_q"], mode="nn", outs=[BF16], rextras=[cosf, sinf], epilogue=rot_ep, name="attn_q")
    og, lg = [], []
    for g, dil in enumerate(DILATIONS):
        o_g, l_g = _attn_fwd(q_rot, k_rot, v_b, g, dil)
        og.append(o_g)
        lg.append(l_g)

    def merge(o0, o1, o2, l0, l1, l2):
        m = jnp.maximum(jnp.maximum(l0, l1), l2)
        e = [jnp.exp(l0 - m), jnp.exp(l1 - m), jnp.exp(l2 - m)]
        den = e[0] + e[1] + e[2]
        w = [e_g / den for e_g in e]
        o = _per_head(o0, lambda oh, h: sum(w[g][:, h:h + 1] * (o0, o1, o2)[g][:, h * HEAD_DIM:(h + 1) * HEAD_DIM].astype(F32)
                                            for g in range(N_GROUPS)))
        return o, m + jnp.log(den)

    ob, lse = _rows(merge, og + lg, [], [(D, BF16), (STAT_LANES, F32)], [], tm=256, name="attn_merge")
    z1_1, h1b_1 = _mm(ob, W["attn_w_o"], mode="nn", outs=[F32, BF16], extras=[x1], vecs=[vec1[0], vec1[1]],
                      epilogue=z1_ep, ep_rows=EPILOGUE_ROWS, name="attn_out")
    t1, act1, z2_1, h2b_1, pp1, gpre1 = _mlp_ple_fwd(z1_1, h1b_1, p[1], W, vec1, 1, run)
    wts0 = (W["mlp_up0"], W["mlp_down0"], W["ple_proj0"], W["ple_gate0"])
    wts1 = (W["mlp_up1"], W["mlp_down1"], W["ple_proj1"], W["ple_gate1"])

    def head(z2, pp, gpre, tgt, g2, b2):
        y, gt = _ple_out(z2, pp, gpre, g2, b2)
        err = y - tgt
        dy = err * (1.0 / D)
        return dy, dy * gt, dy * pp * gt * (1.0 - gt), 0.5 * err * err * (1.0 / D)

    dy1, d_pp1, d_gpre1, loss_cols = _rows(head, [z2_1, pp1, gpre1, target], [vec1[2], vec1[3]],
                                           [(D, F32), (D, BF16), (D, BF16)], [D], tm=256, name="loss_head")

    def produce(name, grad):
        gw[name] = grad
        if reducer is not None:
            reducer.produced(name, grad)

    dz1_1, dz1b_1, (gv["ln1_g1"], gv["ln1_b1"], gv["ln2_g1"], gv["ln2_b1"]) = _mlp_ple_bwd(
        dy1, d_pp1, d_gpre1, p[1], z1_1, h1b_1, t1, act1, z2_1, h2b_1, wts1, vec1, 1, run, produce)
    produce("attn_w_o", run(_mm, ob, dz1b_1, mode="tn", outs=[BF16], layout=("row", N_CHIPS), name="d_attn_w_o"))

    do_b = run(_mm, dz1b_1, W["attn_w_o"], mode="nt", outs=[BF16], name="attn_do")

    def delta(do_t, o_t):
        prod = do_t.astype(F32) * o_t.astype(F32)
        H = D // HEAD_DIM
        return (_per_head_pack([jnp.sum(prod[:, h * HEAD_DIM:(h + 1) * HEAD_DIM], axis=1, keepdims=True) for h in range(H)]),)

    (dlt,) = _rows(delta, [do_b, ob], [], [(STAT_LANES, F32)], [], tm=256, name="attn_delta")
    dqs, dks, dvs = [], [], []
    for g, dil in enumerate(DILATIONS):
        dq_g, dk_g, dv_g = run(_attn_bwd, q_rot, k_rot, v_b, do_b, lse, dlt, g, dil, name=f"attn_bwd_g{g}")
        dqs.append(dq_g)
        dks.append(dk_g)
        dvs.append(dv_g)

    def unrot(q0, q1, q2, k0, k1, k2, v0, v1, v2, c_, s_):
        dq = jnp.concatenate([_rot_t(t_.astype(F32), c_, s_) for t_ in (q0, q1, q2)], axis=1)
        f = lambda t_: t_.astype(F32)
        return dq, _rot_t(f(k0) + f(k1) + f(k2), c_, s_), f(v0) + f(v1) + f(v2)

    dq, dk, dv = run(_rows, unrot, dqs + dks + dvs + [cosf, sinf], [], [(N_GROUPS * D, BF16), (D, BF16), (D, BF16)], [],
                     tm=128, name="attn_unrot")
    produce("attn_w_q", run(_mm, x1, dq, mode="tn", outs=[BF16], layout=("col", N_CHIPS), name="d_attn_w_q"))
    dx1_q = run(_mm, dq, W["attn_w_q"], mode="nt", outs=[F32], extras=[dz1_1],
                epilogue=lambda acc, e: (acc + ALPHA * e,), name="dx1_q")
    produce("w_kv", jnp.concatenate(
        [run(_mm, kvn, dk, mode="tn", outs=[BF16], layout=("col", 2), name="d_w_kv_k"),
         run(_mm, kvn, dv, mode="tn", outs=[BF16], layout=("col", 2), name="d_w_kv_v")], axis=1))
    dkvn_k = run(_mm, dk, W["w_kv"], b_sel=0, mode="nt", outs=[F32], name="dkvn_k")
    dkvn = run(_mm, dv, W["w_kv"], b_sel=1, mode="nt", outs=[F32], extras=[dkvn_k], epilogue=lambda acc, e: (acc + e,),
               name="dkvn_v")

    def x1_bwd(dx1q_t, dkvn_t, x1_t, pp, gpre, kg):
        n, rstd = _ln_norm(x1_t)
        dy = dx1q_t + _ln_bwd(dkvn_t, n, rstd, kg)
        gt = _sigmoid(gpre)
        return dy, dy * gt, dy * pp * gt * (1.0 - gt), dkvn_t * n, dkvn_t

    dy0, d_pp0, d_gpre0, gv["kv_ln_g"], gv["kv_ln_b"] = _rows(
        x1_bwd, [dx1_q, dkvn, x1, pp0, gpre0], [V["kv_ln_g"]], [(D, F32), (D, BF16), (D, BF16)], [D, D], tm=256,
        name="x1_bwd")

    dz1_0, dz1b_0, (gv["ln1_g0"], gv["ln1_b0"], gv["ln2_g0"], gv["ln2_b0"]) = _mlp_ple_bwd(
        dy0, d_pp0, d_gpre0, p[0], z1_0, h1b_0, t0, act0, z2_0, h2b_0, wts0, vec0, 0, run, produce)
    produce("conv_w_out", run(_mm, sb, dz1b_0, mode="tn", outs=[BF16], layout=("row", N_CHIPS), name="d_conv_w_out"))
    ds = run(_mm, dz1b_0, W["conv_w_out"], mode="nt", outs=[F32], name="conv_ds")

    def conv_ln_bwd(ds_t, c_t, g_, b_):
        n, rstd = _ln_norm(c_t)
        y = n * g_ + b_
        sg = _sigmoid(y)
        dln = ds_t * sg * (1.0 + y * (1.0 - sg))
        dc = _ln_bwd(dln, n, rstd, g_)
        return dc, dln * n, dln, dc

    dc, gv["conv_ln_g"], gv["conv_ln_b"], gv["conv_dw_b"] = _rows(
        conv_ln_bwd, [ds, cv], [V["conv_ln_g"], V["conv_ln_b"]], [(D, F32)], [D, D, D], tm=256, name="conv_ln_bwd")
    da, dg, gv["conv_dw"], gv["conv_b_a"], gv["conv_b_g"] = run(
        _conv_bwd, dc, glu, a_pre, g_pre, V["conv_dw"], V["conv_b_a"], V["conv_b_g"], name="conv_bwd")
    produce("conv_w_in", jnp.concatenate(
        [run(_mm, x, da, mode="tn", outs=[BF16], layout=("col", 2), name="d_conv_w_in_a"),
         run(_mm, x, dg, mode="tn", outs=[BF16], layout=("col", 2), name="d_conv_w_in_g")], axis=1))
    dx_a = run(_mm, da, W["conv_w_in"], b_sel=0, mode="nt", outs=[F32], extras=[dz1_0],
               epilogue=lambda acc, e: (acc + ALPHA * e,), name="dx_a")
    grad_x = run(_mm, dg, W["conv_w_in"], b_sel=1, mode="nt", outs=[F32], extras=[dx_a],
                 epilogue=lambda acc, e: (acc + e,), name="dx_g")
    if reducer is not None:
        reducer.carry("share_last").run_alone("share_last")
        reducer.carried()
    return loss_cols, grad_x, gw, gv


def _place():
    x, y, c = lax.axis_index("x"), lax.axis_index("y"), lax.axis_index("c")
    chips = [(1 - x, y), (x, 1 - y), (1 - x, 1 - y)]
    return x, y, c, chips


def _remote(src, dst, ssem, rsem, dev):
    return pltpu.make_async_remote_copy(src_ref=src, dst_ref=dst, send_sem=ssem, recv_sem=rsem, device_id=dev,
                                        device_id_type=MESH)


def _allgather8(block, name):
    R, C = block.shape

    def body(x_ref, out_ref, send_sems, recv_sems, local_sem):
        x, y, c, chips = _place()
        me, sibling = (x, y, c), (x, y, 1 - c)

        def slot(px, py, pc):
            return out_ref.at[4 * px + 2 * py + pc]

        def copy(k, blockpos, to, src=None):
            return _remote(slot(*blockpos) if src is None else src, slot(*blockpos), send_sems.at[k], recv_sems.at[k], to)

        mine = pltpu.make_async_copy(x_ref, slot(*me), local_sem)
        mine.start()
        first = [copy(0, me, sibling, src=x_ref)]
        first += [copy(1 + j, me, (*chip, c), src=x_ref) for j, chip in enumerate(chips)]
        for cp in first:
            cp.start()
        passed = [copy(4 + j, (*chip, c), sibling) for j, chip in enumerate(chips)]
        for j, chip in enumerate(chips):
            copy(1 + j, (*chip, c), me).wait_recv()
            passed[j].start()
        copy(0, sibling, me).wait_recv()
        for j, chip in enumerate(chips):
            copy(4 + j, (*chip, 1 - c), me).wait_recv()
        for cp in first + passed:
            cp.wait_send()
        mine.wait()

    return pl.pallas_call(
        body, name=name, out_shape=jax.ShapeDtypeStruct((8, R, C), block.dtype),
        in_specs=[pl.BlockSpec(memory_space=pltpu.VMEM)], out_specs=pl.BlockSpec(memory_space=pltpu.VMEM),
        scratch_shapes=[pltpu.SemaphoreType.DMA((7,)), pltpu.SemaphoreType.DMA((7,)), pltpu.SemaphoreType.DMA],
        compiler_params=pltpu.CompilerParams(vmem_limit_bytes=_vmem(10 * _nbytes((R, C), block.dtype))),
    )(block)


_MATS = (
    ("conv_w_in", "conv_w_in", 0, "col", True),
    ("conv_w_out", "conv_w_out", 0, "row", False),
    ("mlp_up0", "mlp_up", 0, "col", False),
    ("mlp_down0", "mlp_down", 0, "row", False),
    ("ple_proj0", "ple_proj", 0, "col", False),
    ("ple_gate0", "ple_gate", 0, "row", False),
    ("w_kv", "w_kv", None, "col", True),
    ("attn_w_q", "attn_w_q", 0, "col", False),
    ("attn_w_o", "attn_w_o", 0, "row", False),
    ("mlp_up1", "mlp_up", 1, "col", False),
    ("mlp_down1", "mlp_down", 1, "row", False),
    ("ple_proj1", "ple_proj", 1, "col", False),
    ("ple_gate1", "ple_gate", 1, "row", False),
)


class _Carry:
    result = None
    aliases = {}

    def set_result(self, outs):
        self.result = dict(zip(self.names, outs))

    def run_alone(self, name):
        n_in, n_out = len(self.ins), len(self.out_shape)

        def body(*refs):
            in_refs, out_refs, sems = refs[:n_in], refs[n_in:n_in + n_out], refs[n_in + n_out:]
            self.start(in_refs, out_refs, sems)
            self.finish(in_refs, out_refs, sems)

        outs = pl.pallas_call(body, name=name, out_shape=self.out_shape, in_specs=[ANY] * n_in, out_specs=[ANY] * n_out,
                              scratch_shapes=self.scratch, input_output_aliases=dict(self.aliases))(*self.ins)
        self.set_result(outs)
        return self.result


class _Gather(_Carry):
    def __init__(self, names, shards, chip_arr):
        mats = [m for m in _MATS if m[0] in names]
        srcs = sorted({m[1] for m in mats})
        self.names = [m[0] for m in mats]
        self.out_shape, self.geo, placed = [], [], []
        for name, src, layer, kind, split in mats:
            s = shards[src]
            ks, ns = s.shape[-2:]
            K, N = (ks, ns * N_CHIPS) if kind == "col" else (ks * N_CHIPS, ns)
            self.out_shape.append(jax.ShapeDtypeStruct((2, K, N // 2) if split else (K, N), BF16))
            self.geo.append((srcs.index(src), layer if s.ndim == 3 else None, kind, split, K, N))
            placed.append(_place_shard(s, layer if s.ndim == 3 else None, kind, split, chip_arr, f"place_{name}"))
        T = len(mats)
        self.ins = [shards[n] for n in srcs] + placed
        self.aliases = {len(srcs) + t: t for t in range(T)}
        self.scratch = [pltpu.SemaphoreType.DMA((3 * T,)) for _ in range(4)]
        self.result = None

    def _copies(self, in_refs, out_refs, sems):
        geo, T = self.geo, len(self.geo)
        s_ici, r_ici, s_d2d, r_d2d = sems
        x, y, c, chips = _place()
        me = 2 * x + y
        sibling = (x, y, 1 - c)
        idx = [2 * cx + cy for cx, cy in chips]

        def src_ref(t):
            i, layer, _, _, _, _ = geo[t]
            return in_refs[i] if layer is None else in_refs[i].at[layer]

        def src_half(t, h):
            _, _, kind, _, K, N = geo[t]
            if kind == "col":
                return src_ref(t).at[pl.ds(h * (K // 2), K // 2), :]
            return src_ref(t).at[:, pl.ds(h * (N // 2), N // 2)]

        def dst(t, j, h):
            _, _, kind, split, K, N = geo[t]
            n, k = N // N_CHIPS, K // N_CHIPS
            if kind == "col":
                rows = slice(None) if h is None else pl.ds(h * (K // 2), K // 2)
                if split:
                    return out_refs[t].at[j // 2, rows, pl.ds((j % 2) * n, n)]
                return out_refs[t].at[rows, pl.ds(j * n, n)]
            cols = slice(None) if h is None else pl.ds(h * (N // 2), N // 2)
            return out_refs[t].at[pl.ds(j * k, k), cols]

        sends = [_remote(src_half(t, c), dst(t, me, c), s_ici.at[3 * t + kk], r_ici.at[3 * t + kk], (*chips[kk], c))
                 for t in range(T) for kk in range(3)]
        hops = []
        for t in range(T):
            for kk in range(3):
                mine, theirs = dst(t, idx[kk], c), dst(t, idx[kk], 1 - c)
                hops.append((_remote(mine, mine, s_ici.at[3 * t + kk], r_ici.at[3 * t + kk], sibling),
                             _remote(mine, mine, s_d2d.at[3 * t + kk], r_d2d.at[3 * t + kk], sibling),
                             _remote(theirs, theirs, s_d2d.at[3 * t + kk], r_d2d.at[3 * t + kk], sibling)))
        return sends, hops

    def start(self, in_refs, out_refs, sems):
        for cp in self._copies(in_refs, out_refs, sems)[0]:
            cp.start()

    def finish(self, in_refs, out_refs, sems):
        sends, hops = self._copies(in_refs, out_refs, sems)
        for landed, forward, _ in hops:
            landed.wait_recv()
            forward.start()
        for _, _, from_sibling in hops:
            from_sibling.wait_recv()
        for cp in sends + [h[1] for h in hops]:
            cp.wait_send()


def _place_shard(shard, layer, kind, split, chip_arr, name):
    ks, ns = shard.shape[-2:]
    K, N = (ks, ns * N_CHIPS) if kind == "col" else (ks * N_CHIPS, ns)
    tr = _fit(256, ks)
    nb = ks // tr
    if shard.ndim == 3:
        in_spec = pl.BlockSpec((None, tr, ns), lambda i, me: (layer, i, 0))
    else:
        in_spec = pl.BlockSpec((tr, ns), lambda i, me: (i, 0))
    if kind == "row":
        out_shape, out_spec = (K, N), pl.BlockSpec((tr, ns), lambda i, me: (me[0] * nb + i, 0))
    elif split:
        out_shape, out_spec = (2, K, N // 2), pl.BlockSpec((None, tr, ns), lambda i, me: (me[0] // 2, i, me[0] % 2))
    else:
        out_shape, out_spec = (K, N), pl.BlockSpec((tr, ns), lambda i, me: (i, me[0]))

    def body(me_ref, s_ref, o_ref):
        o_ref[...] = s_ref[...]

    return pl.pallas_call(
        body, name=name, out_shape=jax.ShapeDtypeStruct(out_shape, BF16),
        grid_spec=pltpu.PrefetchScalarGridSpec(num_scalar_prefetch=1, grid=(nb,), in_specs=[in_spec], out_specs=out_spec),
        compiler_params=pltpu.CompilerParams(dimension_semantics=("parallel",), vmem_limit_bytes=_vmem(4 * tr * ns * 2)),
    )(chip_arr, shard)


class _Multi(_Carry):
    def __init__(self, parts):
        self.parts = parts
        self.ins = [a for p in parts for a in p.ins]
        self.out_shape = [a for p in parts for a in p.out_shape]
        self.scratch = [a for p in parts for a in p.scratch]
        self.aliases, n_in, n_out = {}, 0, 0
        for p in parts:
            self.aliases.update({n_in + i: n_out + o for i, o in p.aliases.items()})
            n_in, n_out = n_in + len(p.ins), n_out + len(p.out_shape)

    def _split(self, seq, field):
        out, at = [], 0
        for p in self.parts:
            n = len(getattr(p, field))
            out.append(seq[at:at + n])
            at += n
        return out

    def _each(self, method, in_refs, out_refs, sems):
        for p, i, o, s in zip(self.parts, self._split(in_refs, "ins"), self._split(out_refs, "out_shape"),
                              self._split(sems, "scratch")):
            getattr(p, method)(i, o, s)

    def start(self, in_refs, out_refs, sems):
        self._each("start", in_refs, out_refs, sems)

    def finish(self, in_refs, out_refs, sems):
        self._each("finish", in_refs, out_refs, sems)

    def set_result(self, outs):
        for p, o in zip(self.parts, self._split(list(outs), "out_shape")):
            p.set_result(o)


class _PairSend(_Carry):
    def __init__(self, grads):
        self.names = list(grads)
        self.ins = [grads[n] for n in self.names]
        self.out_shape = [jax.ShapeDtypeStruct(a.shape[1:], BF16) for a in self.ins]
        T = len(self.names)
        self.scratch = [pltpu.SemaphoreType.DMA((T,)), pltpu.SemaphoreType.DMA((T,))]

    def _copies(self, in_refs, out_refs, sems):
        x, y, c, _ = _place()
        return [_remote(in_refs[t].at[1 - c], out_refs[t], sems[0].at[t], sems[1].at[t], (x, y, 1 - c))
                for t in range(len(self.names))]

    def start(self, in_refs, out_refs, sems):
        for cp in self._copies(in_refs, out_refs, sems):
            cp.start()

    def finish(self, in_refs, out_refs, sems):
        for cp in self._copies(in_refs, out_refs, sems):
            cp.wait()


class _ChipScatter(_Carry):
    def __init__(self, sums):
        self.names = list(sums)
        T = len(self.names)
        self.ins = [sums[n][0] for n in self.names] + [sums[n][1] for n in self.names]
        self.out_shape = [jax.ShapeDtypeStruct(a.shape, BF16) for a in self.ins[:T]]
        self.aliases = {T + t: t for t in range(T)}
        self.scratch = [pltpu.SemaphoreType.DMA((3 * T,)), pltpu.SemaphoreType.DMA((3 * T,))]

    def _copies(self, in_refs, out_refs, sems):
        ssem, rsem = sems
        x, y, c, chips = _place()
        me = 2 * x + y
        idx = [2 * cx + cy for cx, cy in chips]
        T = len(self.names)
        sends = [_remote(in_refs[t].at[idx[kk]], out_refs[t].at[me], ssem.at[3 * t + kk], rsem.at[3 * t + kk],
                         (*chips[kk], c)) for t in range(T) for kk in range(3)]
        lands = [_remote(out_refs[t].at[idx[kk]], out_refs[t].at[idx[kk]], ssem.at[3 * t + kk], rsem.at[3 * t + kk],
                         (*chips[kk], c)) for t in range(T) for kk in range(3)]
        return sends, lands

    def start(self, in_refs, out_refs, sems):
        for cp in self._copies(in_refs, out_refs, sems)[0]:
            cp.start()

    def finish(self, in_refs, out_refs, sems):
        sends, lands = self._copies(in_refs, out_refs, sems)
        for cp in lands:
            cp.wait_recv()
        for cp in sends:
            cp.wait_send()


class _PairShare(_Carry):
    def __init__(self, halves):
        self.names = list(halves)
        self.ins = [halves[n] for n in self.names]
        self.out_shape = [jax.ShapeDtypeStruct(a.shape, F32) for a in self.ins]
        T = len(self.names)
        self.aliases = {t: t for t in range(T)}
        self.scratch = [pltpu.SemaphoreType.DMA((T,)), pltpu.SemaphoreType.DMA((T,))]

    def _copies(self, in_refs, out_refs, sems):
        ssem, rsem = sems
        x, y, c, _ = _place()
        sibling = (x, y, 1 - c)
        T = len(self.names)
        sends = [_remote(out_refs[t].at[c], out_refs[t].at[c], ssem.at[t], rsem.at[t], sibling) for t in range(T)]
        lands = [_remote(out_refs[t].at[1 - c], out_refs[t].at[1 - c], ssem.at[t], rsem.at[t], sibling) for t in range(T)]
        return sends, lands

    def start(self, in_refs, out_refs, sems):
        for cp in self._copies(in_refs, out_refs, sems)[0]:
            cp.start()

    def finish(self, in_refs, out_refs, sems):
        sends, lands = self._copies(in_refs, out_refs, sems)
        for cp in lands:
            cp.wait_recv()
        for cp in sends:
            cp.wait_send()


def _pair_sum(own, landed, c_arr, name):
    _, ns, r, cc = own.shape
    rows = ns * r
    tr = _fit(512, rows)

    def body(c_ref, a_ref, b_ref, o_ref, o2_ref):
        total = (a_ref[...].astype(F32) + b_ref[...].astype(F32)).astype(o_ref.dtype)
        o_ref[...] = total
        o2_ref[...] = total

    tile = pl.BlockSpec((tr, cc), lambda i, c_ref: (i, 0))
    out = pl.pallas_call(
        body, name=name, out_shape=[jax.ShapeDtypeStruct((rows, cc), BF16)] * 2,
        grid_spec=pltpu.PrefetchScalarGridSpec(
            num_scalar_prefetch=1, grid=(rows // tr,),
            in_specs=[pl.BlockSpec((None, tr, cc), lambda i, c_ref: (c_ref[0], i, 0)), tile], out_specs=[tile, tile]),
        compiler_params=pltpu.CompilerParams(dimension_semantics=("parallel",), vmem_limit_bytes=_vmem(8 * tr * cc * 4)),
    )(c_arr, own.reshape(2, rows, cc), landed.reshape(rows, cc))
    return out[0].reshape(ns, r, cc), out[1].reshape(ns, r, cc)


def _chip_sum(parts, c_arr, name):
    _, r, cc = parts.shape
    tr = _fit(256, r)

    def body(c_ref, p_ref, o_ref):
        acc = p_ref[0].astype(F32)
        for j in range(1, N_CHIPS):
            acc = acc + p_ref[j].astype(F32)
        o_ref[...] = acc

    return pl.pallas_call(
        body, name=name, out_shape=jax.ShapeDtypeStruct((2, r, cc), F32),
        grid_spec=pltpu.PrefetchScalarGridSpec(
            num_scalar_prefetch=1, grid=(r // tr,),
            in_specs=[pl.BlockSpec((N_CHIPS, tr, cc), lambda i, c_ref: (0, i, 0))],
            out_specs=pl.BlockSpec((None, tr, cc), lambda i, c_ref: (c_ref[0], i, 0))),
        compiler_params=pltpu.CompilerParams(dimension_semantics=("parallel",), vmem_limit_bytes=_vmem(12 * tr * cc * 4)),
    )(c_arr, parts)


def _adamw_math(w, g, m, v):
    m2 = ADAM_B1 * m + (1.0 - ADAM_B1) * g
    v2 = ADAM_B2 * v + (1.0 - ADAM_B2) * jnp.square(g)
    m_hat = m2 / (1.0 - ADAM_B1 ** ADAM_STEP)
    v_hat = v2 / (1.0 - ADAM_B2 ** ADAM_STEP)
    delta = -ADAM_LR * (m_hat / (jnp.sqrt(v_hat) + ADAM_EPS) + ADAM_WD * w)
    return delta, m2, v2


def _adamw_mat(g2, w, m, v, layer, kind, prev, name):
    shape = w.shape
    ks, ns = shape[-2:]
    _, r, cc = g2.shape
    tr, tc = _fit(256, r), _fit(1024, cc)
    assert (r, cc) == ((ks // 2, ns) if kind == "col" else (ks, ns // 2))
    assert r % tr == 0 and cc % tc == 0
    rb, cb = r // tr, cc // tc
    if kind == "col":
        g_spec = pl.BlockSpec((None, tr, tc), lambda i, j: (i // rb, i % rb, j))
    else:
        g_spec = pl.BlockSpec((None, tr, tc), lambda i, j: (j // cb, i, j % cb))
    if w.ndim == 3:
        w_spec = pl.BlockSpec((None, tr, tc), lambda i, j: (layer, i, j))
    else:
        w_spec = pl.BlockSpec((tr, tc), lambda i, j: (i, j))
    n_prev = 0 if prev is None else 4

    def body(*refs):
        g_ref, w_ref, m_ref, v_ref = refs[:4]
        go_ref, d_ref, mo_ref, vo_ref = refs[4 + n_prev:]
        g = g_ref[...]
        delta, m2, v2 = _adamw_math(w_ref[...], g, m_ref[...], v_ref[...])
        go_ref[...] = g
        d_ref[...] = delta
        mo_ref[...] = m2
        vo_ref[...] = v2

    return pl.pallas_call(
        body, name=name, grid=(ks // tr, ns // tc),
        in_specs=[g_spec, w_spec, w_spec, w_spec] + [ANY] * n_prev, out_specs=[w_spec] * 4,
        out_shape=[jax.ShapeDtypeStruct(shape, F32)] * 4,
        input_output_aliases={4 + i: i for i in range(n_prev)},
        compiler_params=pltpu.CompilerParams(dimension_semantics=("parallel", "parallel"),
                                             vmem_limit_bytes=_vmem(16 * tr * tc * 4)),
    )(g2, w, m, v, *(prev or ()))


def _adamw_small(g, w, m, v, name):
    def body(g_ref, w_ref, m_ref, v_ref, d_ref, mo_ref, vo_ref):
        delta, m2, v2 = _adamw_math(w_ref[...], g_ref[...], m_ref[...], v_ref[...])
        d_ref[...] = delta
        mo_ref[...] = m2
        vo_ref[...] = v2

    return pl.pallas_call(body, name=name, out_shape=[jax.ShapeDtypeStruct(w.shape, F32)] * 3)(g, w, m, v)


def _sum8(parts, name):
    def body(p_ref, o_ref):
        acc = p_ref[0]
        for j in range(1, 8):
            acc = acc + p_ref[j]
        o_ref[...] = acc

    return pl.pallas_call(body, name=name, out_shape=jax.ShapeDtypeStruct(parts.shape[1:], F32),
                          compiler_params=pltpu.CompilerParams(vmem_limit_bytes=_vmem(12 * _nbytes(parts.shape[1:], F32))))(parts)


_REDUCE_AT = {
    "d_ple_gate1": (("A", "ple_proj1"),),
    "dh2_1": (("A", "ple_gate1"),),
    "d_mlp_down1": (("B", "ple_proj1"), ("B", "ple_gate1")),
    "dt1": (("A", "mlp_down1"),),
    "d_mlp_up1": (("B", "mlp_down1"), ("C", "ple_proj1"), ("C", "ple_gate1")),
    "dh1_1": (("A", "mlp_up1"),),
    "d_attn_w_o": (("C", "mlp_down1"),),
    "attn_do": (("A", "attn_w_o"),),
    "attn_bwd_g0": (("B", "mlp_up1"),),
    "attn_bwd_g1": (("B", "attn_w_o"),),
    "attn_unrot": (("C", "mlp_up1"), ("C", "attn_w_o")),
    "dx1_q": (("A", "attn_w_q"),),
    "dkvn_k": (("A", "w_kv"),),
    "d_ple_gate0": (("A", "ple_proj0"),),
    "dh2_0": (("A", "ple_gate0"),),
    "d_mlp_down0": (("B", "attn_w_q"), ("B", "ple_proj0")),
    "dt0": (("B", "w_kv"), ("B", "ple_gate0"), ("A", "mlp_down0")),
    "d_mlp_up0": (("B", "mlp_down0"), ("C", "attn_w_q"), ("C", "ple_proj0"), ("C", "w_kv"), ("C", "ple_gate0")),
    "dh1_0": (("A", "mlp_up0"),),
    "d_conv_w_out": (("C", "mlp_down0"),),
    "conv_ds": (("A", "conv_w_out"),),
    "conv_bwd": (("B", "mlp_up0"), ("B", "conv_w_out")),
    "d_conv_w_in_g": (("C", "mlp_up0"), ("C", "conv_w_out")),
    "dx_a": (("A", "conv_w_in"),),
    "dx_g": (("B", "conv_w_in"),),
    "share_last": (("C", "conv_w_in"),),
}


class _Reducer:
    def __init__(self, w, mom, var, c_arr):
        self.w, self.mom, self.var, self.c_arr = w, mom, var, c_arr
        self.mats = {m[0]: m for m in _MATS}
        self.grads, self.pair_sums, self.chip_sums, self.out = {}, {}, {}, {}

    def produced(self, name, grad):
        self.grads[name] = grad

    def carry(self, call):
        parts = []
        for cls, stage, src in ((_PairSend, "A", self.grads), (_ChipScatter, "B", self.pair_sums),
                                (_PairShare, "C", self.chip_sums)):
            names = [n for s, n in _REDUCE_AT.get(call, ()) if s == stage]
            if names:
                parts.append((stage, cls({n: src[n] for n in names})))
        self._parts = parts
        return _Multi([p for _, p in parts]) if parts else None

    def carried(self):
        for stage, part in self._parts:
            for name, val in part.result.items():
                if stage == "A":
                    self.pair_sums[name] = _pair_sum(self.grads[name], val, self.c_arr, f"pair_sum_{name}")
                elif stage == "B":
                    self.chip_sums[name] = _chip_sum(val, self.c_arr, f"chip_sum_{name}")
                else:
                    _, src, layer, kind, _ = self.mats[name]
                    self.out[src] = _adamw_mat(val, self.w[src], self.mom[src], self.var[src], layer or 0, kind,
                                               self.out.get(src), f"adamw_{name}")
        self._parts = []


_WEIGHTS = ("conv_w_in", "conv_b_in", "conv_dw", "conv_dw_b", "conv_ln_g", "conv_ln_b", "conv_w_out", "kv_ln_g",
            "kv_ln_b", "w_kv", "attn_w_q", "attn_w_o", "ln1_g", "ln1_b", "mlp_up", "mlp_down", "ln2_g", "ln2_b",
            "ple_proj", "ple_gate")
_SHARDED_VECS = ("conv_b_in", "conv_dw", "conv_dw_b", "conv_ln_g", "conv_ln_b")
_REPLICATED_VECS = ("kv_ln_g", "kv_ln_b", "ln1_g", "ln1_b", "ln2_g", "ln2_b")


def _pad_rows(a, rows):
    return jnp.concatenate([a, jnp.zeros((rows - a.shape[0], a.shape[1]), a.dtype)], axis=0) if a.shape[0] < rows else a


def _pack_sharded(d):
    n = d["conv_dw_b"].shape[-1]
    rows = [d["conv_b_in"].reshape(2, n), d["conv_dw"].reshape(CONV_WIDTH, n), d["conv_dw_b"].reshape(1, n),
            d["conv_ln_g"].reshape(1, n), d["conv_ln_b"].reshape(1, n)]
    return _pad_rows(jnp.concatenate(rows, axis=0), 40)


def _unpack_sharded(pack, like):
    n = pack.shape[1]
    return {"conv_b_in": pack[0:2].reshape(like["conv_b_in"].shape),
            "conv_dw": pack[2:2 + CONV_WIDTH].reshape(like["conv_dw"].shape),
            "conv_dw_b": pack[33:34].reshape(like["conv_dw_b"].shape),
            "conv_ln_g": pack[34:35].reshape(like["conv_ln_g"].shape),
            "conv_ln_b": pack[35:36].reshape(like["conv_ln_b"].shape)}


def _pack_replicated(d):
    D = d["kv_ln_g"].shape[-1]
    rows = [d[n].reshape(-1, D) for n in _REPLICATED_VECS]
    return _pad_rows(jnp.concatenate(rows, axis=0), 16)


def _unpack_replicated(pack, like):
    out, r = {}, 0
    for n in _REPLICATED_VECS:
        k = like[n].size // pack.shape[1]
        out[n] = pack[r:r + k].reshape(like[n].shape)
        r += k
    return out


def kernel(x, p, positions, conv_w_in, conv_b_in, conv_dw, conv_dw_b, conv_ln_g, conv_ln_b, conv_w_out, kv_ln_g, kv_ln_b, w_kv, attn_w_q, attn_w_o, ln1_g, ln1_b, mlp_up, mlp_down, ln2_g, ln2_b, ple_proj, ple_gate, loss_target, m_conv_w_in, m_conv_b_in, m_conv_dw, m_conv_dw_b, m_conv_ln_g, m_conv_ln_b, m_conv_w_out, m_kv_ln_g, m_kv_ln_b, m_w_kv, m_attn_w_q, m_attn_w_o, m_ln1_g, m_ln1_b, m_mlp_up, m_mlp_down, m_ln2_g, m_ln2_b, m_ple_proj, m_ple_gate, v_conv_w_in, v_conv_b_in, v_conv_dw, v_conv_dw_b, v_conv_ln_g, v_conv_ln_b, v_conv_w_out, v_kv_ln_g, v_kv_ln_b, v_w_kv, v_attn_w_q, v_attn_w_o, v_ln1_g, v_ln1_b, v_mlp_up, v_mlp_down, v_ln2_g, v_ln2_b, v_ple_proj, v_ple_gate):
    args = dict(locals())
    w = {n: args[n] for n in _WEIGHTS}
    mom = {n: args["m_" + n] for n in _WEIGHTS}
    var = {n: args["v_" + n] for n in _WEIGHTS}
    S, D = x.shape[1:]
    n4 = D // N_CHIPS
    chip = 2 * lax.axis_index("x") + lax.axis_index("y")
    c_arr = lax.axis_index("c").astype(jnp.int32).reshape(1)

    shards = {n: w[n].astype(BF16) for n in sorted({m[1] for m in _MATS})}
    vec_all = _allgather8(_pack_sharded(w), "gather_vectors")
    vec_full = jnp.concatenate([vec_all[2 * j] for j in range(N_CHIPS)], axis=1)
    b_in = vec_all[0::2, 0:2, :].reshape(1, 2 * D)
    V = {"conv_b_a": b_in[:, :D], "conv_b_g": b_in[:, D:],
         "conv_dw": _pad_rows(vec_full[2:2 + CONV_WIDTH], CONV_PAD), "conv_dw_b": vec_full[33:34],
         "conv_ln_g": vec_full[34:35], "conv_ln_b": vec_full[35:36],
         "kv_ln_g": kv_ln_g.reshape(1, D), "kv_ln_b": kv_ln_b.reshape(1, D)}
    for l in range(2):
        for n in ("ln1_g", "ln1_b", "ln2_g", "ln2_b"):
            V[f"{n}{l}"] = w[n][l].reshape(1, D)

    half = HEAD_DIM // 2
    inv_freq = ROPE_THETA ** (-jnp.arange(half, dtype=F32) * (2.0 / HEAD_DIM))
    ang = positions[0].astype(F32)[:, None] * inv_freq
    cos, sin = jnp.cos(ang), jnp.sin(ang)
    cosf = jnp.concatenate([cos, cos], axis=-1)
    sinf = jnp.concatenate([-sin, sin], axis=-1)

    reducer = _Reducer(w, mom, var, c_arr)
    loss_cols, grad_x, _, gv = _local_step(x[0], p[:, 0], cosf, sinf, loss_target[0], None, V, shards, reducer,
                                           chip.astype(jnp.int32).reshape(1))
    loss = lax.psum(jnp.sum(loss_cols), ("x", "y", "c"))
    out = dict(reducer.out)

    gpack = jnp.concatenate([gv["conv_b_a"], gv["conv_b_g"], gv["conv_dw"][:CONV_WIDTH], gv["conv_dw_b"],
                             gv["conv_ln_g"], gv["conv_ln_b"], gv["kv_ln_g"], gv["kv_ln_b"],
                             gv["ln1_g0"], gv["ln1_g1"], gv["ln1_b0"], gv["ln1_b1"],
                             gv["ln2_g0"], gv["ln2_g1"], gv["ln2_b0"], gv["ln2_b1"]], axis=0)
    gsum = _sum8(_allgather8(_pad_rows(gpack, 48), "gather_vector_grads"), "sum_vector_grads")
    g_b = lax.dynamic_slice_in_dim(jnp.concatenate([gsum[0:1], gsum[1:2]], axis=1), chip * 2 * n4, 2 * n4, axis=1)
    g_sh = lax.dynamic_slice_in_dim(gsum[2:36], chip * n4, n4, axis=1)
    g_sh = _pad_rows(jnp.concatenate([g_b.reshape(2, n4), g_sh], axis=0), 40)
    d_sh, m_sh, v_sh = _adamw_small(g_sh, _pack_sharded(w), _pack_sharded(mom), _pack_sharded(var), "adamw_sharded_vectors")
    g_rep = _pad_rows(gsum[36:46], 16)
    d_rep, m_rep, v_rep = _adamw_small(g_rep, _pack_replicated(w), _pack_replicated(mom), _pack_replicated(var),
                                       "adamw_replicated_vectors")
    small = {}
    for i, (sh, rep) in enumerate(((g_sh, g_rep), (d_sh, d_rep), (m_sh, m_rep), (v_sh, v_rep))):
        d = {**_unpack_sharded(sh, w), **_unpack_replicated(rep, w)}
        for n, val in d.items():
            small.setdefault(n, [None] * 4)[i] = val
    for n in small:
        out[n] = small[n]

    res = [loss, grad_x[None]]
    for i in range(4):
        res += [out[n][i] for n in _WEIGHTS]
    return tuple(res)
```

```python
import functools

import jax
import jax.numpy as jnp
from jax import lax
from jax.experimental import pallas as pl
from jax.experimental.pallas import tpu as pltpu

F32 = jnp.float32
BF16 = jnp.bfloat16

HEAD_DIM = 128
ATTN_BLOCK = 128
DILATIONS = (1, 4, 16)
N_GROUPS = 3
CONV_WIDTH = 31
CONV_PAD = 32
CONV_ROWS = 32
EPILOGUE_ROWS = 128
ROPE_THETA = 10000.0
LN_EPS = 1e-5
ALPHA = 4.0 ** 0.25
ATTN_SCALE = HEAD_DIM ** -0.5
NEG = -1e30

ADAM_LR = 0.001
ADAM_B1 = 0.9
ADAM_B2 = 0.999
ADAM_EPS = 1e-08
ADAM_WD = 0.01
ADAM_STEP = 10

N_CHIPS = 4
VMEM_CAP = 60 << 20
MESH = pl.DeviceIdType.MESH
ANY = pl.BlockSpec(memory_space=pl.ANY)


def _vmem(nbytes):
    return int(min(max(2 * nbytes + (8 << 20), 24 << 20), VMEM_CAP))


def _fit(tile, n):
    if n <= tile:
        return n
    t = tile - tile % 128
    while n % t:
        t -= 128
    return t


def _nbytes(shape, dtype):
    n = 1
    for s in shape:
        n *= s
    return n * jnp.dtype(dtype).itemsize


_DIMS = {"nn": (((1,), (0,)), ((), ())), "nt": (((1,), (1,)), ((), ())), "tn": (((0,), (0,)), ((), ()))}


def _pcall(body, *, name, grid, in_specs, out_specs, out_shape, operands, scratch_shapes=(), vmem, carry=None):
    if carry is None:
        return pl.pallas_call(
            body, name=name, grid=grid, in_specs=in_specs, out_specs=out_specs, out_shape=out_shape,
            scratch_shapes=list(scratch_shapes),
            compiler_params=pltpu.CompilerParams(dimension_semantics=("arbitrary",) * len(grid), vmem_limit_bytes=vmem),
        )(*operands)
    n_in, n_out, n_scr = len(in_specs), len(out_specs), len(scratch_shapes)
    c_in, c_out = len(carry.ins), len(carry.out_shape)

    def wrapped(*refs):
        ins, refs = refs[:n_in], refs[n_in:]
        c_ins, refs = refs[:c_in], refs[c_in:]
        outs, refs = refs[:n_out], refs[n_out:]
        c_outs, refs = refs[:c_out], refs[c_out:]
        scr, c_sems = refs[:n_scr], refs[n_scr:]
        first = functools.reduce(jnp.logical_and, [pl.program_id(d) == 0 for d in range(len(grid))])
        last = functools.reduce(jnp.logical_and, [pl.program_id(d) == grid[d] - 1 for d in range(len(grid))])
        pl.when(first)(lambda: carry.start(c_ins, c_outs, c_sems))
        body(*ins, *outs, *scr)
        pl.when(last)(lambda: carry.finish(c_ins, c_outs, c_sems))

    res = pl.pallas_call(
        wrapped, name=name, grid=grid, in_specs=list(in_specs) + [ANY] * c_in, out_specs=list(out_specs) + [ANY] * c_out,
        out_shape=list(out_shape) + list(carry.out_shape), scratch_shapes=list(scratch_shapes) + list(carry.scratch),
        input_output_aliases={len(operands) + i: n_out + o for i, o in carry.aliases.items()},
        compiler_params=pltpu.CompilerParams(dimension_semantics=("arbitrary",) * len(grid), vmem_limit_bytes=vmem),
    )(*operands, *carry.ins)
    carry.set_result(res[n_out:])
    return res[:n_out]


def _mm(a, b, *, mode, outs, name, epilogue=None, extras=(), rextras=(), vecs=(), a_sel=None, b_sel=None,
        tm=None, tn=2048, tk=None, layout=None, carry=None, ep_rows=None, a_fn=None):
    a2, b2 = a.shape[-2:], b.shape[-2:]
    if mode == "nn":
        (M, K), (K2, N) = a2, b2
    elif mode == "nt":
        (M, K), (N, K2) = a2, b2
    else:
        (K, M), (K2, N) = a2, b2
    assert K == K2, (a.shape, b.shape, mode)
    if tm is None:
        tm = 1024 if mode == "tn" else 512
    if tk is None:
        tk = 1024 if mode == "tn" else 2048
    if layout is not None:
        kind, nslots = layout
        r, c = (M // 2, N // nslots) if kind == "col" else (M // nslots, N // 2)
        tm, tn = _fit(tm, r), _fit(tn, c)
        assert r % tm == 0 and c % tn == 0
    else:
        tm, tn = _fit(tm, M), _fit(tn, N)
    tk = _fit(tk, K)
    assert M % tm == 0 and N % tn == 0 and K % tk == 0, (M, N, K, tm, tn, tk)
    nk = K // tk
    grid = (N // tn, M // tm, nk)

    def spec(arr, sel, blk, imap):
        if arr.ndim == 3:
            return pl.BlockSpec((None,) + blk, lambda j, i, k: (sel,) + imap(j, i, k))
        return pl.BlockSpec(blk, imap)

    if mode == "tn":
        a_spec = spec(a, a_sel, (tk, tm), lambda j, i, k: (k, i))
    else:
        a_spec = spec(a, a_sel, (tm, tk), lambda j, i, k: (i, k))
    if mode == "nt":
        b_spec = spec(b, b_sel, (tn, tk), lambda j, i, k: (j, k))
    else:
        b_spec = spec(b, b_sel, (tk, tn), lambda j, i, k: (k, j))
    in_specs = [a_spec, b_spec]
    in_specs += [pl.BlockSpec((tm, tn), lambda j, i, k: (i, j)) for _ in extras]
    in_specs += [pl.BlockSpec((tm, e.shape[1]), lambda j, i, k: (i, 0)) for e in rextras]
    in_specs += [pl.BlockSpec((1, tn), lambda j, i, k: (0, j)) for _ in vecs]

    if layout is None:
        out_shape = [jax.ShapeDtypeStruct((M, N), d) for d in outs]
        out_specs = [pl.BlockSpec((tm, tn), lambda j, i, k: (i, j)) for _ in outs]
    else:
        assert len(outs) == 1
        out_shape = [jax.ShapeDtypeStruct((2, nslots, r, c), outs[0])]
        rb, cb = r // tm, c // tn
        if kind == "col":
            omap = lambda j, i, k: (i // rb, j // cb, i % rb, j % cb)
        else:
            omap = lambda j, i, k: (j // cb, i // rb, i % rb, j % cb)
        out_specs = [pl.BlockSpec((None, None, tm, tn), omap)]

    ne, nr, nv, no = len(extras), len(rextras), len(vecs), len(outs)
    dims = _DIMS[mode]

    def body(*refs):
        a_ref, b_ref = refs[0], refs[1]
        rest = refs[2:2 + ne + nr + nv]
        o_refs = refs[2 + ne + nr + nv:2 + ne + nr + nv + no]

        def finish(total):
            if epilogue is None:
                for o in o_refs:
                    o[...] = total.astype(o.dtype)
                return
            step = min(ep_rows or tm, tm)
            for r0 in range(0, tm, step):
                rows = slice(r0, r0 + step)
                tiles = [x[rows, :] for x in rest[:ne + nr]] + [x[...] for x in rest[ne + nr:]]
                for o, val in zip(o_refs, epilogue(total[rows, :], *tiles)):
                    o[rows, :] = val.astype(o.dtype)

        def product():
            a_tile = a_ref[...] if a_fn is None else a_fn(a_ref[...])
            return lax.dot_general(a_tile.astype(BF16), b_ref[...].astype(BF16), dims, preferred_element_type=F32)

        if nk == 1:
            finish(product())
            return
        acc = refs[-1]
        k = pl.program_id(2)

        @pl.when(k == 0)
        def _():
            acc[...] = product()

        @pl.when(k > 0)
        def _():
            acc[...] += product()

        @pl.when(k == nk - 1)
        def _():
            finish(acc[...])

    blk = (_nbytes((tm, tk), a.dtype) + _nbytes((tk, tn), b.dtype) + sum(_nbytes((tm, tn), e.dtype) for e in extras)
           + sum(_nbytes((tm, tn), d) for d in outs) + 2 * tm * tn * 4)
    res = _pcall(body, name=name, grid=grid, in_specs=in_specs, out_specs=out_specs, out_shape=out_shape,
                 operands=(a, b, *extras, *rextras, *vecs),
                 scratch_shapes=[pltpu.VMEM((tm, tn), F32)] if nk > 1 else [], vmem=_vmem(blk), carry=carry)
    return res[0] if no == 1 else tuple(res)


def _rows(fn, rows, vecs, outs, sums, *, tm, name, carry=None):
    S = rows[0].shape[0]
    tm = min(tm, S)
    assert S % tm == 0
    nr, nv, no, ns = len(rows), len(vecs), len(outs), len(sums)

    def body(*refs):
        vals = fn(*[r[...] for r in refs[:nr + nv]])
        o_refs = refs[nr + nv:nr + nv + no]
        s_refs = refs[nr + nv + no:]
        for o, val in zip(o_refs, vals[:no]):
            o[...] = val.astype(o.dtype)
        if ns:
            @pl.when(pl.program_id(0) == 0)
            def _():
                for s in s_refs:
                    s[...] = jnp.zeros_like(s)

            for s, val in zip(s_refs, vals[no:]):
                s[...] += jnp.sum(val.astype(F32), axis=0, keepdims=True)

    in_specs = [pl.BlockSpec((tm, r.shape[1]), lambda i: (i, 0)) for r in rows]
    in_specs += [pl.BlockSpec(v.shape, lambda i: (0, 0)) for v in vecs]
    out_specs = [pl.BlockSpec((tm, c), lambda i: (i, 0)) for c, _ in outs]
    out_specs += [pl.BlockSpec((1, c), lambda i: (0, 0)) for c in sums]
    out_shape = [jax.ShapeDtypeStruct((S, c), d) for c, d in outs]
    out_shape += [jax.ShapeDtypeStruct((1, c), F32) for c in sums]
    blk = sum(_nbytes((tm, r.shape[1]), r.dtype) for r in rows) + sum(_nbytes((tm, c), d) for c, d in outs)
    blk += 6 * tm * max(r.shape[1] for r in rows) * 4
    res = _pcall(body, name=name, grid=(S // tm,), in_specs=in_specs, out_specs=out_specs, out_shape=out_shape,
                 operands=(*rows, *vecs), vmem=_vmem(blk), carry=carry)
    return tuple(res)


def _ln_norm(z):
    mu = jnp.mean(z, axis=-1, keepdims=True)
    d = z - mu
    var = jnp.mean(d * d, axis=-1, keepdims=True)
    rstd = lax.rsqrt(var + LN_EPS)
    return d * rstd, rstd


def _ln(z, g, b):
    return _ln_norm(z)[0] * g + b


def _ln_bwd(dy, n, rstd, g):
    dn = dy * g
    return rstd * (dn - jnp.mean(dn, axis=-1, keepdims=True) - n * jnp.mean(dn * n, axis=-1, keepdims=True))


def _sq_relu(t):
    return jnp.square(jnp.maximum(t.astype(F32), 0.0))


def _sigmoid(x):
    return 1.0 / (1.0 + jnp.exp(-x))


def _per_head(x, fn):
    h = x.shape[1] // HEAD_DIM
    return jnp.concatenate([fn(x[:, i * HEAD_DIM:(i + 1) * HEAD_DIM], i) for i in range(h)], axis=1)


def _rot(x, cosf, sinf):
    return _per_head(x, lambda xh, i: xh * cosf + pltpu.roll(xh, HEAD_DIM // 2, 1) * sinf)


def _rot_t(dy, cosf, sinf):
    return _per_head(dy, lambda dh, i: dh * cosf + pltpu.roll(dh * sinf, HEAD_DIM // 2, 1))


def _shift_copies(win):
    rows = win.shape[1] - 8
    for s in range(1, 8):
        win[s, 0:rows, :] = win[0, s:s + rows, :]


def _rows_at(win, start):
    s = start % 8
    return win[s, start - s:start - s + CONV_ROWS, :]


def _conv_fwd(glu, dw, dwb, *, tm=256, tc=512, name="conv_fwd", carry=None):
    S, D = glu.shape
    tm, tc = min(tm, S), min(tc, D)
    ni = S // tm

    def body(cur_ref, prev_ref, dw_ref, dwb_ref, o_ref, win):
        i = pl.program_id(1)
        tail = prev_ref[tm - CONV_PAD:tm, :]
        win[0, 0:CONV_PAD, :] = jnp.where(i > 0, tail, jnp.zeros_like(tail))
        win[0, CONV_PAD:CONV_PAD + tm, :] = cur_ref[...]
        _shift_copies(win)
        first = CONV_PAD - CONV_WIDTH + 1
        for r0 in range(0, tm, CONV_ROWS):
            acc = jnp.zeros((CONV_ROWS, tc), F32) + dwb_ref[...]
            for k in range(CONV_WIDTH):
                acc = acc + _rows_at(win, r0 + first + k) * dw_ref[k:k + 1, :]
            o_ref[r0:r0 + CONV_ROWS, :] = acc

    return _pcall(
        body, name=name, grid=(D // tc, ni),
        in_specs=[pl.BlockSpec((tm, tc), lambda j, i: (i, j)),
                  pl.BlockSpec((tm, tc), lambda j, i: (jnp.maximum(i - 1, 0), j)),
                  pl.BlockSpec((CONV_PAD, tc), lambda j, i: (0, j)),
                  pl.BlockSpec((1, tc), lambda j, i: (0, j))],
        out_specs=[pl.BlockSpec((tm, tc), lambda j, i: (i, j))],
        out_shape=[jax.ShapeDtypeStruct((S, D), F32)],
        scratch_shapes=[pltpu.VMEM((8, tm + CONV_PAD, tc), F32)],
        operands=(glu, glu, dw, dwb), vmem=_vmem(8 * tm * tc * 4), carry=carry)[0]


def _conv_bwd(dc, glu, a_pre, g_pre, dw, ba, bg, *, tm=256, tc=512, name="conv_bwd", carry=None):
    S, D = dc.shape
    tm, tc = min(tm, S), min(tc, D)
    ni = S // tm

    def fold8(v):
        out = v[0:8]
        for r in range(8, CONV_ROWS, 8):
            out = out + v[r:r + 8]
        return out

    def body(dc_ref, dcn_ref, glu_ref, glup_ref, a_ref, g_ref, dw_ref, ba_ref, bg_ref,
             da_ref, dg_ref, ddw_ref, dba_ref, dbg_ref, dwin, gwin, taps):
        i = pl.program_id(1)

        @pl.when(i == 0)
        def _():
            ddw_ref[...] = jnp.zeros_like(ddw_ref)
            dba_ref[...] = jnp.zeros_like(dba_ref)
            dbg_ref[...] = jnp.zeros_like(dbg_ref)

        head = dcn_ref[0:CONV_PAD, :]
        dwin[0, 0:tm, :] = dc_ref[...]
        dwin[0, tm:tm + CONV_PAD, :] = jnp.where(i < ni - 1, head, jnp.zeros_like(head))
        tail = glup_ref[tm - CONV_PAD:tm, :]
        gwin[0, 0:CONV_PAD, :] = jnp.where(i > 0, tail, jnp.zeros_like(tail))
        gwin[0, CONV_PAD:CONV_PAD + tm, :] = glu_ref[...]
        _shift_copies(dwin)
        _shift_copies(gwin)
        taps[...] = jnp.zeros_like(taps)
        first = CONV_PAD - CONV_WIDTH + 1
        sum_a = jnp.zeros((8, tc), F32)
        sum_g = jnp.zeros((8, tc), F32)
        for r0 in range(0, tm, CONV_ROWS):
            dcur = dc_ref[r0:r0 + CONV_ROWS, :]
            dglu = jnp.zeros((CONV_ROWS, tc), F32)
            for k in range(CONV_WIDTH):
                dglu = dglu + _rows_at(dwin, r0 + CONV_WIDTH - 1 - k) * dw_ref[k:k + 1, :]
                taps[k] += fold8(dcur * _rows_at(gwin, r0 + first + k))
            a = a_ref[r0:r0 + CONV_ROWS, :] + ba_ref[...]
            sg = _sigmoid(g_ref[r0:r0 + CONV_ROWS, :] + bg_ref[...])
            da = dglu * sg
            dg = dglu * a * sg * (1.0 - sg)
            da_ref[r0:r0 + CONV_ROWS, :] = da.astype(BF16)
            dg_ref[r0:r0 + CONV_ROWS, :] = dg.astype(BF16)
            sum_a = sum_a + fold8(da)
            sum_g = sum_g + fold8(dg)
        ddw_ref[...] += jnp.sum(taps[...], axis=1)
        dba_ref[...] += jnp.sum(sum_a, axis=0, keepdims=True)
        dbg_ref[...] += jnp.sum(sum_g, axis=0, keepdims=True)

    tile = lambda f: pl.BlockSpec((tm, tc), f)
    vec = pl.BlockSpec((1, tc), lambda j, i: (0, j))
    return _pcall(
        body, name=name, grid=(D // tc, ni),
        in_specs=[tile(lambda j, i: (i, j)), tile(lambda j, i: (jnp.minimum(i + 1, ni - 1), j)),
                  tile(lambda j, i: (i, j)), tile(lambda j, i: (jnp.maximum(i - 1, 0), j)),
                  tile(lambda j, i: (i, j)), tile(lambda j, i: (i, j)),
                  pl.BlockSpec((CONV_PAD, tc), lambda j, i: (0, j)), vec, vec],
        out_specs=[tile(lambda j, i: (i, j)), tile(lambda j, i: (i, j)),
                   pl.BlockSpec((CONV_PAD, tc), lambda j, i: (0, j)), vec, vec],
        out_shape=[jax.ShapeDtypeStruct((S, D), BF16), jax.ShapeDtypeStruct((S, D), BF16),
                   jax.ShapeDtypeStruct((CONV_PAD, D), F32), jax.ShapeDtypeStruct((1, D), F32),
                   jax.ShapeDtypeStruct((1, D), F32)],
        scratch_shapes=[pltpu.VMEM((8, tm + CONV_PAD, tc), F32), pltpu.VMEM((8, tm + CONV_PAD, tc), F32),
                        pltpu.VMEM((CONV_PAD, 8, tc), F32)],
        operands=(dc, dc, glu, glu, a_pre, g_pre, dw, ba, bg), vmem=_vmem(16 * tm * tc * 4), carry=carry)


def _nt(a, b):
    return lax.dot_general(a, b, _DIMS["nt"], preferred_element_type=F32)


def _tn(a, b):
    return lax.dot_general(a, b, _DIMS["tn"], preferred_element_type=F32)


STAT_LANES = 128


def _per_head_pack(cols):
    rows = cols[0].shape[0]
    lane = lax.broadcasted_iota(jnp.int32, (rows, STAT_LANES), 1)
    out = jnp.zeros((rows, STAT_LANES), F32)
    for h, col in enumerate(cols):
        out = jnp.where(lane == h, col, out)
    return out


def _window_mask(qi, kj, first_key):
    B = ATTN_BLOCK
    return ((kj < B) & (kj >= qi) & (kj >= first_key)) | ((kj >= B) & (kj - B <= qi))


def _attn_fwd(q_rot, k, v, g, dil):
    S, D = k.shape
    H = D // HEAD_DIM
    L = S // dil
    nb_count = L // ATTN_BLOCK
    B = ATTN_BLOCK

    def body(q_ref, kc_ref, kp_ref, vc_ref, vp_ref, o_ref, lse_ref):
        nb = pl.program_id(1)
        qi = lax.broadcasted_iota(jnp.int32, (B, 2 * B), 0)
        kj = lax.broadcasted_iota(jnp.int32, (B, 2 * B), 1)
        valid = _window_mask(qi, kj, jnp.where(nb > 0, 0, B))
        stats = []
        for h in range(H):
            hs = slice(h * HEAD_DIM, (h + 1) * HEAD_DIM)
            kk = jnp.concatenate([kp_ref[:, hs], kc_ref[:, hs]], axis=0)
            vv = jnp.concatenate([vp_ref[:, hs], vc_ref[:, hs]], axis=0)
            s = jnp.where(valid, _nt(q_ref[:, hs], kk) * ATTN_SCALE, NEG)
            m = jnp.max(s, axis=1, keepdims=True)
            p = jnp.exp(s - m)
            l = jnp.sum(p, axis=1, keepdims=True)
            o = jnp.dot(p.astype(BF16), vv, preferred_element_type=F32) / l
            o_ref[:, hs] = o.astype(o_ref.dtype)
            stats.append(m + jnp.log(l))
        lse_ref[...] = _per_head_pack(stats)

    blk = lambda f: pl.BlockSpec((B, D), f)
    cur = lambda r, nb: (nb, r)
    prev = lambda r, nb: (jnp.maximum(nb - 1, 0), r)
    o, lse = pl.pallas_call(
        body, name=f"attn_fwd_g{g}", grid=(dil, nb_count),
        in_specs=[blk(lambda r, nb: (nb, r * N_GROUPS + g)), blk(cur), blk(prev), blk(cur), blk(prev)],
        out_specs=[blk(cur), pl.BlockSpec((B, STAT_LANES), cur)],
        out_shape=[jax.ShapeDtypeStruct((L, dil * D), BF16), jax.ShapeDtypeStruct((L, dil * STAT_LANES), F32)],
        compiler_params=pltpu.CompilerParams(dimension_semantics=("parallel", "arbitrary"),
                                             vmem_limit_bytes=_vmem(12 * B * D * 4)),
    )(q_rot.reshape(L, dil * N_GROUPS * D), k.reshape(L, dil * D), k.reshape(L, dil * D),
      v.reshape(L, dil * D), v.reshape(L, dil * D))
    return o.reshape(S, D), lse.reshape(S, STAT_LANES)


def _attn_bwd(q_rot, k, v, do, lse, dlt, g, dil, *, name, carry=None):
    S, D = k.shape
    H = D // HEAD_DIM
    L = S // dil
    nb_count = L // ATTN_BLOCK
    B = ATTN_BLOCK

    def body(q_ref, kc_ref, kp_ref, vc_ref, vp_ref, do_ref, l_ref, d_ref, dq_ref, dk_ref, dv_ref, keep_k, keep_v):
        s_id = pl.program_id(1)

        @pl.when(s_id == 0)
        def _():
            keep_k[...] = jnp.zeros_like(keep_k)
            keep_v[...] = jnp.zeros_like(keep_v)

        @pl.when(s_id < nb_count)
        def _():
            qi = lax.broadcasted_iota(jnp.int32, (B, 2 * B), 0)
            kj = lax.broadcasted_iota(jnp.int32, (B, 2 * B), 1)
            valid = _window_mask(qi, kj, jnp.where(s_id > 0, 0, B))
            for h in range(H):
                hs = slice(h * HEAD_DIM, (h + 1) * HEAD_DIM)
                q, dout = q_ref[:, hs], do_ref[:, hs]
                kk = jnp.concatenate([kp_ref[:, hs], kc_ref[:, hs]], axis=0)
                vv = jnp.concatenate([vp_ref[:, hs], vc_ref[:, hs]], axis=0)
                s = jnp.where(valid, _nt(q, kk) * ATTN_SCALE, NEG)
                p = jnp.exp(s - l_ref[:, h:h + 1])
                ds = (p * (_nt(dout, vv) - d_ref[:, h:h + 1])).astype(BF16)
                dq_ref[:, hs] = (jnp.dot(ds, kk, preferred_element_type=F32) * ATTN_SCALE).astype(dq_ref.dtype)
                dkk = _tn(ds, q) * ATTN_SCALE
                dvv = _tn(p.astype(BF16), dout)
                dk_ref[:, hs] = (keep_k[:, hs] + dkk[0:B]).astype(dk_ref.dtype)
                dv_ref[:, hs] = (keep_v[:, hs] + dvv[0:B]).astype(dv_ref.dtype)
                keep_k[:, hs] = dkk[B:2 * B]
                keep_v[:, hs] = dvv[B:2 * B]

        @pl.when(s_id == nb_count)
        def _():
            dk_ref[...] = keep_k[...].astype(dk_ref.dtype)
            dv_ref[...] = keep_v[...].astype(dv_ref.dtype)

    last = nb_count - 1
    blk = lambda f: pl.BlockSpec((B, D), f)
    cur = lambda r, s: (jnp.minimum(s, last), r)
    prev = lambda r, s: (jnp.maximum(jnp.minimum(s, last) - 1, 0), r)
    lag = lambda r, s: (jnp.maximum(s - 1, 0), r)
    qcur = lambda r, s: (jnp.minimum(s, last), r * N_GROUPS + g)
    qv = q_rot.reshape(L, dil * N_GROUPS * D)
    view = lambda t: t.reshape(L, dil * D)
    sview = lambda t: t.reshape(L, dil * STAT_LANES)
    stat = lambda f: pl.BlockSpec((B, STAT_LANES), f)
    dq, dk, dv = _pcall(
        body, name=name, grid=(dil, nb_count + 1),
        in_specs=[blk(qcur), blk(cur), blk(prev), blk(cur), blk(prev), blk(cur), stat(cur), stat(cur)],
        out_specs=[blk(cur), blk(lag), blk(lag)],
        out_shape=[jax.ShapeDtypeStruct((L, dil * D), BF16)] * 3,
        scratch_shapes=[pltpu.VMEM((B, D), F32), pltpu.VMEM((B, D), F32)],
        operands=(qv, view(k), view(k), view(v), view(v), view(do), sview(lse), sview(dlt)),
        vmem=_vmem(24 * B * D * 4), carry=carry)
    return dq.reshape(S, D), dk.reshape(S, D), dv.reshape(S, D)


def _mlp_ple_fwd(z1, h1b, p_l, W, vec, l, run, kv_vec=None):
    D = z1.shape[1]
    g1, b1, g2, b2 = vec

    t = run(_mm, h1b, W[f"mlp_up{l}"], mode="nn", outs=[BF16], name=f"mlp_up{l}")

    def z2_ep(acc, z1_t, g1_, b1_, g2_, b2_):
        z2 = ALPHA * _ln(z1_t, g1_, b1_) + acc
        return z2, _ln(z2, g2_, b2_)

    z2, h2b = run(_mm, t, W[f"mlp_down{l}"], mode="nn", outs=[F32, BF16], extras=[z1], vecs=[g1, b1, g2, b2], a_fn=_sq_relu,
                  epilogue=z2_ep, ep_rows=EPILOGUE_ROWS, name=f"mlp_down{l}")
    act = None
    pp = run(_mm, p_l, W[f"ple_proj{l}"], mode="nn", outs=[F32], name=f"ple_proj{l}")
    if kv_vec is None:
        gpre = run(_mm, h2b, W[f"ple_gate{l}"], mode="nn", outs=[F32], name=f"ple_gate{l}")
        return t, act, z2, h2b, pp, gpre

    def x1_ep(acc, z2_t, pp_t, g2_, b2_, kg, kb):
        x1, _ = _ple_out(z2_t, pp_t, acc, g2_, b2_)
        return acc, x1, _ln(x1, kg, kb)

    gpre, x1, kvn = run(_mm, h2b, W[f"ple_gate{l}"], mode="nn", outs=[F32, F32, BF16], extras=[z2, pp],
                        vecs=[g2, b2, *kv_vec], epilogue=x1_ep, ep_rows=EPILOGUE_ROWS, tm=256, name=f"ple_gate{l}")
    return t, act, z2, h2b, pp, gpre, x1, kvn


def _mlp_ple_bwd(dy, d_pp, d_gpre, p_l, z1, h1b, t, act, z2, h2b, wts, vec, l, run, produce):
    D = z1.shape[1]
    up, down, pp_w, pg_w = wts
    g1, b1, g2, b2 = vec
    produce(f"ple_proj{l}", run(_mm, p_l, d_pp, mode="tn", outs=[BF16], layout=("col", N_CHIPS), name=f"d_ple_proj{l}"))
    produce(f"ple_gate{l}", run(_mm, h2b, d_gpre, mode="tn", outs=[BF16], layout=("row", N_CHIPS), name=f"d_ple_gate{l}"))
    dh2 = run(_mm, d_gpre, pg_w, mode="nt", outs=[F32], extras=[dy], epilogue=lambda acc, e: (acc + e,),
              name=f"dh2_{l}")

    def ln2_bwd(dh2_t, z2_t, g2_):
        n, rstd = _ln_norm(z2_t)
        dz2 = _ln_bwd(dh2_t, n, rstd, g2_)
        return dz2, dz2, dh2_t * n, dh2_t

    dz2, dz2b, dg2, db2 = _rows(ln2_bwd, [dh2, z2], [g2], [(D, F32), (D, BF16)], [D, D], tm=256,
                                name=f"ln2_bwd{l}")
    produce(f"mlp_down{l}", run(_mm, t, dz2b, mode="tn", outs=[BF16], layout=("row", N_CHIPS), a_fn=_sq_relu,
                                name=f"d_mlp_down{l}"))
    dt = run(_mm, dz2b, down, mode="nt", outs=[BF16], extras=[t],
             epilogue=lambda acc, t_: (acc * 2.0 * jnp.maximum(t_.astype(F32), 0.0),), name=f"dt{l}")
    produce(f"mlp_up{l}", run(_mm, h1b, dt, mode="tn", outs=[BF16], layout=("col", N_CHIPS), name=f"d_mlp_up{l}"))
    dh1 = run(_mm, dt, up, mode="nt", outs=[F32], extras=[dz2], epilogue=lambda acc, e: (acc + ALPHA * e,),
              name=f"dh1_{l}")

    def ln1_bwd(dh1_t, z1_t, g1_):
        n, rstd = _ln_norm(z1_t)
        dz1 = _ln_bwd(dh1_t, n, rstd, g1_)
        return dz1, dz1, dh1_t * n, dh1_t

    dz1, dz1b, dg1, db1 = _rows(ln1_bwd, [dh1, z1], [g1], [(D, F32), (D, BF16)], [D, D], tm=256,
                                name=f"ln1_bwd{l}")
    return dz1, dz1b, (dg1, db1, dg2, db2)


def _ple_out(z2, pp, gpre, g2, b2):
    gt = _sigmoid(gpre)
    return _ln(z2, g2, b2) + pp * gt, gt


_GATHER_AT = {
    "conv_in_a": ("conv_w_out",),
    "conv_in_g": ("ple_gate0", "ple_proj0"),
    "conv_fwd": ("mlp_up0",),
    "mlp_up0": ("mlp_down0",),
    "mlp_down0": ("attn_w_q", "w_kv"),
    "kv_k": ("attn_w_o",),
    "kv_v": ("ple_proj1", "ple_gate1"),
    "attn_q": ("mlp_up1",),
    "mlp_up1": ("mlp_down1",),
}
_GATHER_FIRST = ("conv_w_in",)


def _local_step(x, p, cosf, sinf, target, W, V, shards=None, reducer=None, chip_arr=None):
    S, D = x.shape
    gw, gv = {}, {}
    if shards is not None:
        W = dict(_Gather(_GATHER_FIRST, shards, chip_arr).run_alone("gather_first"))

    def run(fn, *args, name, **kw):
        gather = _Gather(_GATHER_AT[name], shards, chip_arr) if (shards is not None and name in _GATHER_AT) else None
        carry = gather if reducer is None or gather is not None else reducer.carry(name)
        out = fn(*args, name=name, carry=carry, **kw)
        if gather is not None:
            W.update(gather.result)
        elif reducer is not None:
            reducer.carried()
        return out

    a_pre = run(_mm, x, W["conv_w_in"], b_sel=0, mode="nn", outs=[F32], name="conv_in_a")
    g_pre = run(_mm, x, W["conv_w_in"], b_sel=1, mode="nn", outs=[F32], name="conv_in_g")
    (glu,) = _rows(lambda a, g, ba, bg: ((a + ba) * _sigmoid(g + bg),), [a_pre, g_pre], [V["conv_b_a"], V["conv_b_g"]],
                   [(D, F32)], [], tm=256, name="glu_fwd")
    cv = run(_conv_fwd, glu, V["conv_dw"], V["conv_dw_b"], name="conv_fwd")

    def silu_ln(c, g_, b_):
        y = _ln(c, g_, b_)
        return (y * _sigmoid(y),)

    (sb,) = _rows(silu_ln, [cv], [V["conv_ln_g"], V["conv_ln_b"]], [(D, BF16)], [], tm=256, name="conv_ln_fwd")
    def z1_ep(acc, x_t, g_, b_):
        z1 = ALPHA * x_t + acc
        return z1, _ln(z1, g_, b_)

    vec0 = (V["ln1_g0"], V["ln1_b0"], V["ln2_g0"], V["ln2_b0"])
    vec1 = (V["ln1_g1"], V["ln1_b1"], V["ln2_g1"], V["ln2_b1"])
    z1_0, h1b_0 = _mm(sb, W["conv_w_out"], mode="nn", outs=[F32, BF16], extras=[x], vecs=[vec0[0], vec0[1]],
                      epilogue=z1_ep, ep_rows=EPILOGUE_ROWS, name="conv_out")
    t0, act0, z2_0, h2b_0, pp0, gpre0, x1, kvn = _mlp_ple_fwd(z1_0, h1b_0, p[0], W, vec0, 0, run,
                                                              (V["kv_ln_g"], V["kv_ln_b"]))

    rot_ep = lambda acc, c_, s_: (_rot(acc, c_, s_),)
    k_rot = run(_mm, kvn, W["w_kv"], b_sel=0, mode="nn", outs=[BF16], rextras=[cosf, sinf], epilogue=rot_ep, name="kv_k")
    v_b = run(_mm, kvn, W["w_kv"], b_sel=1, mode="nn", outs=[BF16], name="kv_v")
    q_rot = run(_mm, x1, W["attn_w_q"], mode="nn", outs=[BF16], rextras=[cosf, sinf], epilogue=rot_ep, name="attn_q")
    og, lg = [], []
    for g, dil in enumerate(DILATIONS):
        o_g, l_g = _attn_fwd(q_rot, k_rot, v_b, g, dil)
        og.append(o_g)
        lg.append(l_g)

    def merge(o0, o1, o2, l0, l1, l2):
        m = jnp.maximum(jnp.maximum(l0, l1), l2)
        e = [jnp.exp(l0 - m), jnp.exp(l1 - m), jnp.exp(l2 - m)]
        den = e[0] + e[1] + e[2]
        w = [e_g / den for e_g in e]
        o = _per_head(o0, lambda oh, h: sum(w[g][:, h:h + 1] * (o0, o1, o2)[g][:, h * HEAD_DIM:(h + 1) * HEAD_DIM].astype(F32)
                                            for g in range(N_GROUPS)))
        return o, m + jnp.log(den)

    ob, lse = _rows(merge, og + lg, [], [(D, BF16), (STAT_LANES, F32)], [], tm=256, name="attn_merge")
    z1_1, h1b_1 = _mm(ob, W["attn_w_o"], mode="nn", outs=[F32, BF16], extras=[x1], vecs=[vec1[0], vec1[1]],
                      epilogue=z1_ep, ep_rows=EPILOGUE_ROWS, name="attn_out")
    t1, act1, z2_1, h2b_1, pp1, gpre1 = _mlp_ple_fwd(z1_1, h1b_1, p[1], W, vec1, 1, run)
    wts0 = (W["mlp_up0"], W["mlp_down0"], W["ple_proj0"], W["ple_gate0"])
    wts1 = (W["mlp_up1"], W["mlp_down1"], W["ple_proj1"], W["ple_gate1"])

    def head(z2, pp, gpre, tgt, g2, b2):
        y, gt = _ple_out(z2, pp, gpre, g2, b2)
        err = y - tgt
        dy = err * (1.0 / D)
        return dy, dy * gt, dy * pp * gt * (1.0 - gt), 0.5 * err * err * (1.0 / D)

    dy1, d_pp1, d_gpre1, loss_cols = _rows(head, [z2_1, pp1, gpre1, target], [vec1[2], vec1[3]],
                                           [(D, F32), (D, BF16), (D, BF16)], [D], tm=256, name="loss_head")

    def produce(name, grad):
        gw[name] = grad
        if reducer is not None:
            reducer.produced(name, grad)

    dz1_1, dz1b_1, (gv["ln1_g1"], gv["ln1_b1"], gv["ln2_g1"], gv["ln2_b1"]) = _mlp_ple_bwd(
        dy1, d_pp1, d_gpre1, p[1], z1_1, h1b_1, t1, act1, z2_1, h2b_1, wts1, vec1, 1, run, produce)
    produce("attn_w_o", run(_mm, ob, dz1b_1, mode="tn", outs=[BF16], layout=("row", N_CHIPS), name="d_attn_w_o"))

    do_b = run(_mm, dz1b_1, W["attn_w_o"], mode="nt", outs=[BF16], name="attn_do")

    def delta(do_t, o_t):
        prod = do_t.astype(F32) * o_t.astype(F32)
        H = D // HEAD_DIM
        return (_per_head_pack([jnp.sum(prod[:, h * HEAD_DIM:(h + 1) * HEAD_DIM], axis=1, keepdims=True) for h in range(H)]),)

    (dlt,) = _rows(delta, [do_b, ob], [], [(STAT_LANES, F32)], [], tm=256, name="attn_delta")
    dqs, dks, dvs = [], [], []
    for g, dil in enumerate(DILATIONS):
        dq_g, dk_g, dv_g = run(_attn_bwd, q_rot, k_rot, v_b, do_b, lse, dlt, g, dil, name=f"attn_bwd_g{g}")
        dqs.append(dq_g)
        dks.append(dk_g)
        dvs.append(dv_g)

    def unrot(q0, q1, q2, k0, k1, k2, v0, v1, v2, c_, s_):
        dq = jnp.concatenate([_rot_t(t_.astype(F32), c_, s_) for t_ in (q0, q1, q2)], axis=1)
        f = lambda t_: t_.astype(F32)
        return dq, _rot_t(f(k0) + f(k1) + f(k2), c_, s_), f(v0) + f(v1) + f(v2)

    dq, dk, dv = run(_rows, unrot, dqs + dks + dvs + [cosf, sinf], [], [(N_GROUPS * D, BF16), (D, BF16), (D, BF16)], [],
                     tm=128, name="attn_unrot")
    produce("attn_w_q", run(_mm, x1, dq, mode="tn", outs=[BF16], layout=("col", N_CHIPS), name="d_attn_w_q"))
    dx1_q = run(_mm, dq, W["attn_w_q"], mode="nt", outs=[F32], extras=[dz1_1],
                epilogue=lambda acc, e: (acc + ALPHA * e,), name="dx1_q")
    produce("w_kv", jnp.concatenate(
        [run(_mm, kvn, dk, mode="tn", outs=[BF16], layout=("col", 2), name="d_w_kv_k"),
         run(_mm, kvn, dv, mode="tn", outs=[BF16], layout=("col", 2), name="d_w_kv_v")], axis=1))
    dkvn_k = run(_mm, dk, W["w_kv"], b_sel=0, mode="nt", outs=[F32], name="dkvn_k")
    dkvn = run(_mm, dv, W["w_kv"], b_sel=1, mode="nt", outs=[F32], extras=[dkvn_k], epilogue=lambda acc, e: (acc + e,),
               name="dkvn_v")

    def x1_bwd(dx1q_t, dkvn_t, x1_t, pp, gpre, kg):
        n, rstd = _ln_norm(x1_t)
        dy = dx1q_t + _ln_bwd(dkvn_t, n, rstd, kg)
        gt = _sigmoid(gpre)
        return dy, dy * gt, dy * pp * gt * (1.0 - gt), dkvn_t * n, dkvn_t

    dy0, d_pp0, d_gpre0, gv["kv_ln_g"], gv["kv_ln_b"] = _rows(
        x1_bwd, [dx1_q, dkvn, x1, pp0, gpre0], [V["kv_ln_g"]], [(D, F32), (D, BF16), (D, BF16)], [D, D], tm=256,
        name="x1_bwd")

    dz1_0, dz1b_0, (gv["ln1_g0"], gv["ln1_b0"], gv["ln2_g0"], gv["ln2_b0"]) = _mlp_ple_bwd(
        dy0, d_pp0, d_gpre0, p[0], z1_0, h1b_0, t0, act0, z2_0, h2b_0, wts0, vec0, 0, run, produce)
    produce("conv_w_out", run(_mm, sb, dz1b_0, mode="tn", outs=[BF16], layout=("row", N_CHIPS), name="d_conv_w_out"))
    ds = run(_mm, dz1b_0, W["conv_w_out"], mode="nt", outs=[F32], name="conv_ds")

    def conv_ln_bwd(ds_t, c_t, g_, b_):
        n, rstd = _ln_norm(c_t)
        y = n * g_ + b_
        sg = _sigmoid(y)
        dln = ds_t * sg * (1.0 + y * (1.0 - sg))
        dc = _ln_bwd(dln, n, rstd, g_)
        return dc, dln * n, dln, dc

    dc, gv["conv_ln_g"], gv["conv_ln_b"], gv["conv_dw_b"] = _rows(
        conv_ln_bwd, [ds, cv], [V["conv_ln_g"], V["conv_ln_b"]], [(D, F32)], [D, D, D], tm=256, name="conv_ln_bwd")
    da, dg, gv["conv_dw"], gv["conv_b_a"], gv["conv_b_g"] = run(
        _conv_bwd, dc, glu, a_pre, g_pre, V["conv_dw"], V["conv_b_a"], V["conv_b_g"], name="conv_bwd")
    produce("conv_w_in", jnp.concatenate(
        [run(_mm, x, da, mode="tn", outs=[BF16], layout=("col", 2), name="d_conv_w_in_a"),
         run(_mm, x, dg, mode="tn", outs=[BF16], layout=("col", 2), name="d_conv_w_in_g")], axis=1))
    dx_a = run(_mm, da, W["conv_w_in"], b_sel=0, mode="nt", outs=[F32], extras=[dz1_0],
               epilogue=lambda acc, e: (acc + ALPHA * e,), name="dx_a")
    grad_x = run(_mm, dg, W["conv_w_in"], b_sel=1, mode="nt", outs=[F32], extras=[dx_a],
                 epilogue=lambda acc, e: (acc + e,), name="dx_g")
    if reducer is not None:
        reducer.carry("share_last").run_alone("share_last")
        reducer.carried()
    return loss_cols, grad_x, gw, gv


def _place():
    x, y, c = lax.axis_index("x"), lax.axis_index("y"), lax.axis_index("c")
    chips = [(1 - x, y), (x, 1 - y), (1 - x, 1 - y)]
    return x, y, c, chips


def _remote(src, dst, ssem, rsem, dev):
    return pltpu.make_async_remote_copy(src_ref=src, dst_ref=dst, send_sem=ssem, recv_sem=rsem, device_id=dev,
                                        device_id_type=MESH)


def _allgather8(block, name):
    R, C = block.shape

    def body(x_ref, out_ref, send_sems, recv_sems, local_sem):
        x, y, c, chips = _place()
        me, sibling = (x, y, c), (x, y, 1 - c)

        def slot(px, py, pc):
            return out_ref.at[4 * px + 2 * py + pc]

        def copy(k, blockpos, to, src=None):
            return _remote(slot(*blockpos) if src is None else src, slot(*blockpos), send_sems.at[k], recv_sems.at[k], to)

        mine = pltpu.make_async_copy(x_ref, slot(*me), local_sem)
        mine.start()
        first = [copy(0, me, sibling, src=x_ref)]
        first += [copy(1 + j, me, (*chip, c), src=x_ref) for j, chip in enumerate(chips)]
        for cp in first:
            cp.start()
        passed = [copy(4 + j, (*chip, c), sibling) for j, chip in enumerate(chips)]
        for j, chip in enumerate(chips):
            copy(1 + j, (*chip, c), me).wait_recv()
            passed[j].start()
        copy(0, sibling, me).wait_recv()
        for j, chip in enumerate(chips):
            copy(4 + j, (*chip, 1 - c), me).wait_recv()
        for cp in first + passed:
            cp.wait_send()
        mine.wait()

    return pl.pallas_call(
        body, name=name, out_shape=jax.ShapeDtypeStruct((8, R, C), block.dtype),
        in_specs=[pl.BlockSpec(memory_space=pltpu.VMEM)], out_specs=pl.BlockSpec(memory_space=pltpu.VMEM),
        scratch_shapes=[pltpu.SemaphoreType.DMA((7,)), pltpu.SemaphoreType.DMA((7,)), pltpu.SemaphoreType.DMA],
        compiler_params=pltpu.CompilerParams(vmem_limit_bytes=_vmem(10 * _nbytes((R, C), block.dtype))),
    )(block)


_MATS = (
    ("conv_w_in", "conv_w_in", 0, "col", True),
    ("conv_w_out", "conv_w_out", 0, "row", False),
    ("mlp_up0", "mlp_up", 0, "col", False),
    ("mlp_down0", "mlp_down", 0, "row", False),
    ("ple_proj0", "ple_proj", 0, "col", False),
    ("ple_gate0", "ple_gate", 0, "row", False),
    ("w_kv", "w_kv", None, "col", True),
    ("attn_w_q", "attn_w_q", 0, "col", False),
    ("attn_w_o", "attn_w_o", 0, "row", False),
    ("mlp_up1", "mlp_up", 1, "col", False),
    ("mlp_down1", "mlp_down", 1, "row", False),
    ("ple_proj1", "ple_proj", 1, "col", False),
    ("ple_gate1", "ple_gate", 1, "row", False),
)


class _Carry:
    result = None
    aliases = {}

    def set_result(self, outs):
        self.result = dict(zip(self.names, outs))

    def run_alone(self, name):
        n_in, n_out = len(self.ins), len(self.out_shape)

        def body(*refs):
            in_refs, out_refs, sems = refs[:n_in], refs[n_in:n_in + n_out], refs[n_in + n_out:]
            self.start(in_refs, out_refs, sems)
            self.finish(in_refs, out_refs, sems)

        outs = pl.pallas_call(body, name=name, out_shape=self.out_shape, in_specs=[ANY] * n_in, out_specs=[ANY] * n_out,
                              scratch_shapes=self.scratch, input_output_aliases=dict(self.aliases))(*self.ins)
        self.set_result(outs)
        return self.result


class _Gather(_Carry):
    def __init__(self, names, shards, chip_arr):
        mats = [m for m in _MATS if m[0] in names]
        srcs = sorted({m[1] for m in mats})
        self.names = [m[0] for m in mats]
        self.out_shape, self.geo, placed = [], [], []
        for name, src, layer, kind, split in mats:
            s = shards[src]
            ks, ns = s.shape[-2:]
            K, N = (ks, ns * N_CHIPS) if kind == "col" else (ks * N_CHIPS, ns)
            self.out_shape.append(jax.ShapeDtypeStruct((2, K, N // 2) if split else (K, N), BF16))
            self.geo.append((srcs.index(src), layer if s.ndim == 3 else None, kind, split, K, N))
            placed.append(_place_shard(s, layer if s.ndim == 3 else None, kind, split, chip_arr, f"place_{name}"))
        T = len(mats)
        self.ins = [shards[n] for n in srcs] + placed
        self.aliases = {len(srcs) + t: t for t in range(T)}
        self.scratch = [pltpu.SemaphoreType.DMA((3 * T,)) for _ in range(4)]
        self.result = None

    def _copies(self, in_refs, out_refs, sems):
        geo, T = self.geo, len(self.geo)
        s_ici, r_ici, s_d2d, r_d2d = sems
        x, y, c, chips = _place()
        me = 2 * x + y
        sibling = (x, y, 1 - c)
        idx = [2 * cx + cy for cx, cy in chips]

        def src_ref(t):
            i, layer, _, _, _, _ = geo[t]
            return in_refs[i] if layer is None else in_refs[i].at[layer]

        def src_half(t, h):
            _, _, kind, _, K, N = geo[t]
            if kind == "col":
                return src_ref(t).at[pl.ds(h * (K // 2), K // 2), :]
            return src_ref(t).at[:, pl.ds(h * (N // 2), N // 2)]

        def dst(t, j, h):
            _, _, kind, split, K, N = geo[t]
            n, k = N // N_CHIPS, K // N_CHIPS
            if kind == "col":
                rows = slice(None) if h is None else pl.ds(h * (K // 2), K // 2)
                if split:
                    return out_refs[t].at[j // 2, rows, pl.ds((j % 2) * n, n)]
                return out_refs[t].at[rows, pl.ds(j * n, n)]
            cols = slice(None) if h is None else pl.ds(h * (N // 2), N // 2)
            return out_refs[t].at[pl.ds(j * k, k), cols]

        sends = [_remote(src_half(t, c), dst(t, me, c), s_ici.at[3 * t + kk], r_ici.at[3 * t + kk], (*chips[kk], c))
                 for t in range(T) for kk in range(3)]
        hops = []
        for t in range(T):
            for kk in range(3):
                mine, theirs = dst(t, idx[kk], c), dst(t, idx[kk], 1 - c)
                hops.append((_remote(mine, mine, s_ici.at[3 * t + kk], r_ici.at[3 * t + kk], sibling),
                             _remote(mine, mine, s_d2d.at[3 * t + kk], r_d2d.at[3 * t + kk], sibling),
                             _remote(theirs, theirs, s_d2d.at[3 * t + kk], r_d2d.at[3 * t + kk], sibling)))
        return sends, hops

    def start(self, in_refs, out_refs, sems):
        for cp in self._copies(in_refs, out_refs, sems)[0]:
            cp.start()

    def finish(self, in_refs, out_refs, sems):
        sends, hops = self._copies(in_refs, out_refs, sems)
        for landed, forward, _ in hops:
            landed.wait_recv()
            forward.start()
        for _, _, from_sibling in hops:
            from_sibling.wait_recv()
        for cp in sends + [h[1] for h in hops]:
            cp.wait_send()


def _place_shard(shard, layer, kind, split, chip_arr, name):
    ks, ns = shard.shape[-2:]
    K, N = (ks, ns * N_CHIPS) if kind == "col" else (ks * N_CHIPS, ns)
    tr = _fit(256, ks)
    nb = ks // tr
    if shard.ndim == 3:
        in_spec = pl.BlockSpec((None, tr, ns), lambda i, me: (layer, i, 0))
    else:
        in_spec = pl.BlockSpec((tr, ns), lambda i, me: (i, 0))
    if kind == "row":
        out_shape, out_spec = (K, N), pl.BlockSpec((tr, ns), lambda i, me: (me[0] * nb + i, 0))
    elif split:
        out_shape, out_spec = (2, K, N // 2), pl.BlockSpec((None, tr, ns), lambda i, me: (me[0] // 2, i, me[0] % 2))
    else:
        out_shape, out_spec = (K, N), pl.BlockSpec((tr, ns), lambda i, me: (i, me[0]))

    def body(me_ref, s_ref, o_ref):
        o_ref[...] = s_ref[...]

    return pl.pallas_call(
        body, name=name, out_shape=jax.ShapeDtypeStruct(out_shape, BF16),
        grid_spec=pltpu.PrefetchScalarGridSpec(num_scalar_prefetch=1, grid=(nb,), in_specs=[in_spec], out_specs=out_spec),
        compiler_params=pltpu.CompilerParams(dimension_semantics=("parallel",), vmem_limit_bytes=_vmem(4 * tr * ns * 2)),
    )(chip_arr, shard)


class _Multi(_Carry):
    def __init__(self, parts):
        self.parts = parts
        self.ins = [a for p in parts for a in p.ins]
        self.out_shape = [a for p in parts for a in p.out_shape]
        self.scratch = [a for p in parts for a in p.scratch]
        self.aliases, n_in, n_out = {}, 0, 0
        for p in parts:
            self.aliases.update({n_in + i: n_out + o for i, o in p.aliases.items()})
            n_in, n_out = n_in + len(p.ins), n_out + len(p.out_shape)

    def _split(self, seq, field):
        out, at = [], 0
        for p in self.parts:
            n = len(getattr(p, field))
            out.append(seq[at:at + n])
            at += n
        return out

    def _each(self, method, in_refs, out_refs, sems):
        for p, i, o, s in zip(self.parts, self._split(in_refs, "ins"), self._split(out_refs, "out_shape"),
                              self._split(sems, "scratch")):
            getattr(p, method)(i, o, s)

    def start(self, in_refs, out_refs, sems):
        self._each("start", in_refs, out_refs, sems)

    def finish(self, in_refs, out_refs, sems):
        self._each("finish", in_refs, out_refs, sems)

    def set_result(self, outs):
        for p, o in zip(self.parts, self._split(list(outs), "out_shape")):
            p.set_result(o)


class _PairSend(_Carry):
    def __init__(self, grads):
        self.names = list(grads)
        self.ins = [grads[n] for n in self.names]
        self.out_shape = [jax.ShapeDtypeStruct(a.shape[1:], BF16) for a in self.ins]
        T = len(self.names)
        self.scratch = [pltpu.SemaphoreType.DMA((T,)), pltpu.SemaphoreType.DMA((T,))]

    def _copies(self, in_refs, out_refs, sems):
        x, y, c, _ = _place()
        return [_remote(in_refs[t].at[1 - c], out_refs[t], sems[0].at[t], sems[1].at[t], (x, y, 1 - c))
                for t in range(len(self.names))]

    def start(self, in_refs, out_refs, sems):
        for cp in self._copies(in_refs, out_refs, sems):
            cp.start()

    def finish(self, in_refs, out_refs, sems):
        for cp in self._copies(in_refs, out_refs, sems):
            cp.wait()


class _ChipScatter(_Carry):
    def __init__(self, sums):
        self.names = list(sums)
        T = len(self.names)
        self.ins = [sums[n][0] for n in self.names] + [sums[n][1] for n in self.names]
        self.out_shape = [jax.ShapeDtypeStruct(a.shape, BF16) for a in self.ins[:T]]
        self.aliases = {T + t: t for t in range(T)}
        self.scratch = [pltpu.SemaphoreType.DMA((3 * T,)), pltpu.SemaphoreType.DMA((3 * T,))]

    def _copies(self, in_refs, out_refs, sems):
        ssem, rsem = sems
        x, y, c, chips = _place()
        me = 2 * x + y
        idx = [2 * cx + cy for cx, cy in chips]
        T = len(self.names)
        sends = [_remote(in_refs[t].at[idx[kk]], out_refs[t].at[me], ssem.at[3 * t + kk], rsem.at[3 * t + kk],
                         (*chips[kk], c)) for t in range(T) for kk in range(3)]
        lands = [_remote(out_refs[t].at[idx[kk]], out_refs[t].at[idx[kk]], ssem.at[3 * t + kk], rsem.at[3 * t + kk],
                         (*chips[kk], c)) for t in range(T) for kk in range(3)]
        return sends, lands

    def start(self, in_refs, out_refs, sems):
        for cp in self._copies(in_refs, out_refs, sems)[0]:
            cp.start()

    def finish(self, in_refs, out_refs, sems):
        sends, lands = self._copies(in_refs, out_refs, sems)
        for cp in lands:
            cp.wait_recv()
        for cp in sends:
            cp.wait_send()


class _PairShare(_Carry):
    def __init__(self, halves):
        self.names = list(halves)
        self.ins = [halves[n] for n in self.names]
        self.out_shape = [jax.ShapeDtypeStruct(a.shape, F32) for a in self.ins]
        T = len(self.names)
        self.aliases = {t: t for t in range(T)}
        self.scratch = [pltpu.SemaphoreType.DMA((T,)), pltpu.SemaphoreType.DMA((T,))]

    def _copies(self, in_refs, out_refs, sems):
        ssem, rsem = sems
        x, y, c, _ = _place()
        sibling = (x, y, 1 - c)
        T = len(self.names)
        sends = [_remote(out_refs[t].at[c], out_refs[t].at[c], ssem.at[t], rsem.at[t], sibling) for t in range(T)]
        lands = [_remote(out_refs[t].at[1 - c], out_refs[t].at[1 - c], ssem.at[t], rsem.at[t], sibling) for t in range(T)]
        return sends, lands

    def start(self, in_refs, out_refs, sems):
        for cp in self._copies(in_refs, out_refs, sems)[0]:
            cp.start()

    def finish(self, in_refs, out_refs, sems):
        sends, lands = self._copies(in_refs, out_refs, sems)
        for cp in lands:
            cp.wait_recv()
        for cp in sends:
            cp.wait_send()


def _pair_sum(own, landed, c_arr, name):
    _, ns, r, cc = own.shape
    rows = ns * r
    tr = _fit(512, rows)

    def body(c_ref, a_ref, b_ref, o_ref, o2_ref):
        total = (a_ref[...].astype(F32) + b_ref[...].astype(F32)).astype(o_ref.dtype)
        o_ref[...] = total
        o2_ref[...] = total

    tile = pl.BlockSpec((tr, cc), lambda i, c_ref: (i, 0))
    out = pl.pallas_call(
        body, name=name, out_shape=[jax.ShapeDtypeStruct((rows, cc), BF16)] * 2,
        grid_spec=pltpu.PrefetchScalarGridSpec(
            num_scalar_prefetch=1, grid=(rows // tr,),
            in_specs=[pl.BlockSpec((None, tr, cc), lambda i, c_ref: (c_ref[0], i, 0)), tile], out_specs=[tile, tile]),
        compiler_params=pltpu.CompilerParams(dimension_semantics=("parallel",), vmem_limit_bytes=_vmem(8 * tr * cc * 4)),
    )(c_arr, own.reshape(2, rows, cc), landed.reshape(rows, cc))
    return out[0].reshape(ns, r, cc), out[1].reshape(ns, r, cc)


def _chip_sum(parts, c_arr, name):
    _, r, cc = parts.shape
    tr = _fit(256, r)

    def body(c_ref, p_ref, o_ref):
        acc = p_ref[0].astype(F32)
        for j in range(1, N_CHIPS):
            acc = acc + p_ref[j].astype(F32)
        o_ref[...] = acc

    return pl.pallas_call(
        body, name=name, out_shape=jax.ShapeDtypeStruct((2, r, cc), F32),
        grid_spec=pltpu.PrefetchScalarGridSpec(
            num_scalar_prefetch=1, grid=(r // tr,),
            in_specs=[pl.BlockSpec((N_CHIPS, tr, cc), lambda i, c_ref: (0, i, 0))],
            out_specs=pl.BlockSpec((None, tr, cc), lambda i, c_ref: (c_ref[0], i, 0))),
        compiler_params=pltpu.CompilerParams(dimension_semantics=("parallel",), vmem_limit_bytes=_vmem(12 * tr * cc * 4)),
    )(c_arr, parts)


def _adamw_math(w, g, m, v):
    m2 = ADAM_B1 * m + (1.0 - ADAM_B1) * g
    v2 = ADAM_B2 * v + (1.0 - ADAM_B2) * jnp.square(g)
    m_hat = m2 / (1.0 - ADAM_B1 ** ADAM_STEP)
    v_hat = v2 / (1.0 - ADAM_B2 ** ADAM_STEP)
    delta = -ADAM_LR * (m_hat / (jnp.sqrt(v_hat) + ADAM_EPS) + ADAM_WD * w)
    return delta, m2, v2


def _adamw_mat(g2, w, m, v, layer, kind, prev, name):
    shape = w.shape
    ks, ns = shape[-2:]
    _, r, cc = g2.shape
    tr, tc = _fit(256, r), _fit(1024, cc)
    assert (r, cc) == ((ks // 2, ns) if kind == "col" else (ks, ns // 2))
    assert r % tr == 0 and cc % tc == 0
    rb, cb = r // tr, cc // tc
    if kind == "col":
        g_spec = pl.BlockSpec((None, tr, tc), lambda i, j: (i // rb, i % rb, j))
    else:
        g_spec = pl.BlockSpec((None, tr, tc), lambda i, j: (j // cb, i, j % cb))
    if w.ndim == 3:
        w_spec = pl.BlockSpec((None, tr, tc), lambda i, j: (layer, i, j))
    else:
        w_spec = pl.BlockSpec((tr, tc), lambda i, j: (i, j))
    n_prev = 0 if prev is None else 4

    def body(*refs):
        g_ref, w_ref, m_ref, v_ref = refs[:4]
        go_ref, d_ref, mo_ref, vo_ref = refs[4 + n_prev:]
        g = g_ref[...]
        delta, m2, v2 = _adamw_math(w_ref[...], g, m_ref[...], v_ref[...])
        go_ref[...] = g
        d_ref[...] = delta
        mo_ref[...] = m2
        vo_ref[...] = v2

    return pl.pallas_call(
        body, name=name, grid=(ks // tr, ns // tc),
        in_specs=[g_spec, w_spec, w_spec, w_spec] + [ANY] * n_prev, out_specs=[w_spec] * 4,
        out_shape=[jax.ShapeDtypeStruct(shape, F32)] * 4,
        input_output_aliases={4 + i: i for i in range(n_prev)},
        compiler_params=pltpu.CompilerParams(dimension_semantics=("parallel", "parallel"),
                                             vmem_limit_bytes=_vmem(16 * tr * tc * 4)),
    )(g2, w, m, v, *(prev or ()))


def _adamw_small(g, w, m, v, name):
    def body(g_ref, w_ref, m_ref, v_ref, d_ref, mo_ref, vo_ref):
        delta, m2, v2 = _adamw_math(w_ref[...], g_ref[...], m_ref[...], v_ref[...])
        d_ref[...] = delta
        mo_ref[...] = m2
        vo_ref[...] = v2

    return pl.pallas_call(body, name=name, out_shape=[jax.ShapeDtypeStruct(w.shape, F32)] * 3)(g, w, m, v)


def _sum8(parts, name):
    def body(p_ref, o_ref):
        acc = p_ref[0]
        for j in range(1, 8):
            acc = acc + p_ref[j]
        o_ref[...] = acc

    return pl.pallas_call(body, name=name, out_shape=jax.ShapeDtypeStruct(parts.shape[1:], F32),
                          compiler_params=pltpu.CompilerParams(vmem_limit_bytes=_vmem(12 * _nbytes(parts.shape[1:], F32))))(parts)


_REDUCE_AT = {
    "d_ple_gate1": (("A", "ple_proj1"),),
    "dh2_1": (("A", "ple_gate1"),),
    "d_mlp_down1": (("B", "ple_proj1"), ("B", "ple_gate1")),
    "dt1": (("A", "mlp_down1"),),
    "d_mlp_up1": (("B", "mlp_down1"), ("C", "ple_proj1"), ("C", "ple_gate1")),
    "dh1_1": (("A", "mlp_up1"),),
    "d_attn_w_o": (("C", "mlp_down1"),),
    "attn_do": (("A", "attn_w_o"),),
    "attn_bwd_g0": (("B", "mlp_up1"),),
    "attn_bwd_g1": (("B", "attn_w_o"),),
    "attn_unrot": (("C", "mlp_up1"), ("C", "attn_w_o")),
    "dx1_q": (("A", "attn_w_q"),),
    "dkvn_k": (("A", "w_kv"),),
    "d_ple_gate0": (("A", "ple_proj0"),),
    "dh2_0": (("A", "ple_gate0"),),
    "d_mlp_down0": (("B", "attn_w_q"), ("B", "ple_proj0")),
    "dt0": (("B", "w_kv"), ("B", "ple_gate0"), ("A", "mlp_down0")),
    "d_mlp_up0": (("B", "mlp_down0"), ("C", "attn_w_q"), ("C", "ple_proj0"), ("C", "w_kv"), ("C", "ple_gate0")),
    "dh1_0": (("A", "mlp_up0"),),
    "d_conv_w_out": (("C", "mlp_down0"),),
    "conv_ds": (("A", "conv_w_out"),),
    "conv_bwd": (("B", "mlp_up0"), ("B", "conv_w_out")),
    "d_conv_w_in_g": (("C", "mlp_up0"), ("C", "conv_w_out")),
    "dx_a": (("A", "conv_w_in"),),
    "dx_g": (("B", "conv_w_in"),),
    "share_last": (("C", "conv_w_in"),),
}


class _Reducer:
    def __init__(self, w, mom, var, c_arr):
        self.w, self.mom, self.var, self.c_arr = w, mom, var, c_arr
        self.mats = {m[0]: m for m in _MATS}
        self.grads, self.pair_sums, self.chip_sums, self.out = {}, {}, {}, {}

    def produced(self, name, grad):
        self.grads[name] = grad

    def carry(self, call):
        parts = []
        for cls, stage, src in ((_PairSend, "A", self.grads), (_ChipScatter, "B", self.pair_sums),
                                (_PairShare, "C", self.chip_sums)):
            names = [n for s, n in _REDUCE_AT.get(call, ()) if s == stage]
            if names:
                parts.append((stage, cls({n: src[n] for n in names})))
        self._parts = parts
        return _Multi([p for _, p in parts]) if parts else None

    def carried(self):
        for stage, part in self._parts:
            for name, val in part.result.items():
                if stage == "A":
                    self.pair_sums[name] = _pair_sum(self.grads[name], val, self.c_arr, f"pair_sum_{name}")
                elif stage == "B":
                    self.chip_sums[name] = _chip_sum(val, self.c_arr, f"chip_sum_{name}")
                else:
                    _, src, layer, kind, _ = self.mats[name]
                    self.out[src] = _adamw_mat(val, self.w[src], self.mom[src], self.var[src], layer or 0, kind,
                                               self.out.get(src), f"adamw_{name}")
        self._parts = []


_WEIGHTS = ("conv_w_in", "conv_b_in", "conv_dw", "conv_dw_b", "conv_ln_g", "conv_ln_b", "conv_w_out", "kv_ln_g",
            "kv_ln_b", "w_kv", "attn_w_q", "attn_w_o", "ln1_g", "ln1_b", "mlp_up", "mlp_down", "ln2_g", "ln2_b",
            "ple_proj", "ple_gate")
_SHARDED_VECS = ("conv_b_in", "conv_dw", "conv_dw_b", "conv_ln_g", "conv_ln_b")
_REPLICATED_VECS = ("kv_ln_g", "kv_ln_b", "ln1_g", "ln1_b", "ln2_g", "ln2_b")


def _pad_rows(a, rows):
    return jnp.concatenate([a, jnp.zeros((rows - a.shape[0], a.shape[1]), a.dtype)], axis=0) if a.shape[0] < rows else a


def _pack_sharded(d):
    n = d["conv_dw_b"].shape[-1]
    rows = [d["conv_b_in"].reshape(2, n), d["conv_dw"].reshape(CONV_WIDTH, n), d["conv_dw_b"].reshape(1, n),
            d["conv_ln_g"].reshape(1, n), d["conv_ln_b"].reshape(1, n)]
    return _pad_rows(jnp.concatenate(rows, axis=0), 40)


def _unpack_sharded(pack, like):
    n = pack.shape[1]
    return {"conv_b_in": pack[0:2].reshape(like["conv_b_in"].shape),
            "conv_dw": pack[2:2 + CONV_WIDTH].reshape(like["conv_dw"].shape),
            "conv_dw_b": pack[33:34].reshape(like["conv_dw_b"].shape),
            "conv_ln_g": pack[34:35].reshape(like["conv_ln_g"].shape),
            "conv_ln_b": pack[35:36].reshape(like["conv_ln_b"].shape)}


def _pack_replicated(d):
    D = d["kv_ln_g"].shape[-1]
    rows = [d[n].reshape(-1, D) for n in _REPLICATED_VECS]
    return _pad_rows(jnp.concatenate(rows, axis=0), 16)


def _unpack_replicated(pack, like):
    out, r = {}, 0
    for n in _REPLICATED_VECS:
        k = like[n].size // pack.shape[1]
        out[n] = pack[r:r + k].reshape(like[n].shape)
        r += k
    return out


def kernel(x, p, positions, conv_w_in, conv_b_in, conv_dw, conv_dw_b, conv_ln_g, conv_ln_b, conv_w_out, kv_ln_g, kv_ln_b, w_kv, attn_w_q, attn_w_o, ln1_g, ln1_b, mlp_up, mlp_down, ln2_g, ln2_b, ple_proj, ple_gate, loss_target, m_conv_w_in, m_conv_b_in, m_conv_dw, m_conv_dw_b, m_conv_ln_g, m_conv_ln_b, m_conv_w_out, m_kv_ln_g, m_kv_ln_b, m_w_kv, m_attn_w_q, m_attn_w_o, m_ln1_g, m_ln1_b, m_mlp_up, m_mlp_down, m_ln2_g, m_ln2_b, m_ple_proj, m_ple_gate, v_conv_w_in, v_conv_b_in, v_conv_dw, v_conv_dw_b, v_conv_ln_g, v_conv_ln_b, v_conv_w_out, v_kv_ln_g, v_kv_ln_b, v_w_kv, v_attn_w_q, v_attn_w_o, v_ln1_g, v_ln1_b, v_mlp_up, v_mlp_down, v_ln2_g, v_ln2_b, v_ple_proj, v_ple_gate):
    args = dict(locals())
    w = {n: args[n] for n in _WEIGHTS}
    mom = {n: args["m_" + n] for n in _WEIGHTS}
    var = {n: args["v_" + n] for n in _WEIGHTS}
    S, D = x.shape[1:]
    n4 = D // N_CHIPS
    chip = 2 * lax.axis_index("x") + lax.axis_index("y")
    c_arr = lax.axis_index("c").astype(jnp.int32).reshape(1)

    shards = {n: w[n].astype(BF16) for n in sorted({m[1] for m in _MATS})}
    vec_all = _allgather8(_pack_sharded(w), "gather_vectors")
    vec_full = jnp.concatenate([vec_all[2 * j] for j in range(N_CHIPS)], axis=1)
    b_in = vec_all[0::2, 0:2, :].reshape(1, 2 * D)
    V = {"conv_b_a": b_in[:, :D], "conv_b_g": b_in[:, D:],
         "conv_dw": _pad_rows(vec_full[2:2 + CONV_WIDTH], CONV_PAD), "conv_dw_b": vec_full[33:34],
         "conv_ln_g": vec_full[34:35], "conv_ln_b": vec_full[35:36],
         "kv_ln_g": kv_ln_g.reshape(1, D), "kv_ln_b": kv_ln_b.reshape(1, D)}
    for l in range(2):
        for n in ("ln1_g", "ln1_b", "ln2_g", "ln2_b"):
            V[f"{n}{l}"] = w[n][l].reshape(1, D)

    half = HEAD_DIM // 2
    inv_freq = ROPE_THETA ** (-jnp.arange(half, dtype=F32) * (2.0 / HEAD_DIM))
    ang = positions[0].astype(F32)[:, None] * inv_freq
    cos, sin = jnp.cos(ang), jnp.sin(ang)
    cosf = jnp.concatenate([cos, cos], axis=-1)
    sinf = jnp.concatenate([-sin, sin], axis=-1)

    reducer = _Reducer(w, mom, var, c_arr)
    loss_cols, grad_x, _, gv = _local_step(x[0], p[:, 0], cosf, sinf, loss_target[0], None, V, shards, reducer,
                                           chip.astype(jnp.int32).reshape(1))
    loss = lax.psum(jnp.sum(loss_cols), ("x", "y", "c"))
    out = dict(reducer.out)

    gpack = jnp.concatenate([gv["conv_b_a"], gv["conv_b_g"], gv["conv_dw"][:CONV_WIDTH], gv["conv_dw_b"],
                             gv["conv_ln_g"], gv["conv_ln_b"], gv["kv_ln_g"], gv["kv_ln_b"],
                             gv["ln1_g0"], gv["ln1_g1"], gv["ln1_b0"], gv["ln1_b1"],
                             gv["ln2_g0"], gv["ln2_g1"], gv["ln2_b0"], gv["ln2_b1"]], axis=0)
    gsum = _sum8(_allgather8(_pad_rows(gpack, 48), "gather_vector_grads"), "sum_vector_grads")
    g_b = lax.dynamic_slice_in_dim(jnp.concatenate([gsum[0:1], gsum[1:2]], axis=1), chip * 2 * n4, 2 * n4, axis=1)
    g_sh = lax.dynamic_slice_in_dim(gsum[2:36], chip * n4, n4, axis=1)
    g_sh = _pad_rows(jnp.concatenate([g_b.reshape(2, n4), g_sh], axis=0), 40)
    d_sh, m_sh, v_sh = _adamw_small(g_sh, _pack_sharded(w), _pack_sharded(mom), _pack_sharded(var), "adamw_sharded_vectors")
    g_rep = _pad_rows(gsum[36:46], 16)
    d_rep, m_rep, v_rep = _adamw_small(g_rep, _pack_replicated(w), _pack_replicated(mom), _pack_replicated(var),
                                       "adamw_replicated_vectors")
    small = {}
    for i, (sh, rep) in enumerate(((g_sh, g_rep), (d_sh, d_rep), (m_sh, m_rep), (v_sh, v_rep))):
        d = {**_unpack_sharded(sh, w), **_unpack_replicated(rep, w)}
        for n, val in d.items():
            small.setdefault(n, [None] * 4)[i] = val
    for n in small:
        out[n] = small[n]

    res = [loss, grad_x[None]]
    for i in range(4):
        res += [out[n][i] for n in _WEIGHTS]
    return tuple(res)
```

```python
import functools

import jax
import jax.numpy as jnp
from jax import lax
from jax.experimental import pallas as pl
from jax.experimental.pallas import tpu as pltpu

F32 = jnp.float32
BF16 = jnp.bfloat16

HEAD_DIM = 128
ATTN_BLOCK = 128
DILATIONS = (1, 4, 16)
N_GROUPS = 3
CONV_WIDTH = 31
CONV_PAD = 32
CONV_ROWS = 32
EPILOGUE_ROWS = 128
ROPE_THETA = 10000.0
LN_EPS = 1e-5
ALPHA = 4.0 ** 0.25
ATTN_SCALE = HEAD_DIM ** -0.5
NEG = -1e30

ADAM_LR = 0.001
ADAM_B1 = 0.9
ADAM_B2 = 0.999
ADAM_EPS = 1e-08
ADAM_WD = 0.01
ADAM_STEP = 10

N_CHIPS = 4
VMEM_CAP = 60 << 20
MESH = pl.DeviceIdType.MESH
ANY = pl.BlockSpec(memory_space=pl.ANY)


def _vmem(nbytes):
    return int(min(max(2 * nbytes + (8 << 20), 24 << 20), VMEM_CAP))


def _fit(tile, n):
    if n <= tile:
        return n
    t = tile - tile % 128
    while n % t:
        t -= 128
    return t


def _nbytes(shape, dtype):
    n = 1
    for s in shape:
        n *= s
    return n * jnp.dtype(dtype).itemsize


_DIMS = {"nn": (((1,), (0,)), ((), ())), "nt": (((1,), (1,)), ((), ())), "tn": (((0,), (0,)), ((), ()))}


def _pcall(body, *, name, grid, in_specs, out_specs, out_shape, operands, scratch_shapes=(), vmem, carry=None):
    if carry is None:
        return pl.pallas_call(
            body, name=name, grid=grid, in_specs=in_specs, out_specs=out_specs, out_shape=out_shape,
            scratch_shapes=list(scratch_shapes),
            compiler_params=pltpu.CompilerParams(dimension_semantics=("arbitrary",) * len(grid), vmem_limit_bytes=vmem),
        )(*operands)
    n_in, n_out, n_scr = len(in_specs), len(out_specs), len(scratch_shapes)
    c_in, c_out = len(carry.ins), len(carry.out_shape)

    def wrapped(*refs):
        ins, refs = refs[:n_in], refs[n_in:]
        c_ins, refs = refs[:c_in], refs[c_in:]
        outs, refs = refs[:n_out], refs[n_out:]
        c_outs, refs = refs[:c_out], refs[c_out:]
        scr, c_sems = refs[:n_scr], refs[n_scr:]
        first = functools.reduce(jnp.logical_and, [pl.program_id(d) == 0 for d in range(len(grid))])
        last = functools.reduce(jnp.logical_and, [pl.program_id(d) == grid[d] - 1 for d in range(len(grid))])
        pl.when(first)(lambda: carry.start(c_ins, c_outs, c_sems))
        body(*ins, *outs, *scr)
        pl.when(last)(lambda: carry.finish(c_ins, c_outs, c_sems))

    res = pl.pallas_call(
        wrapped, name=name, grid=grid, in_specs=list(in_specs) + [ANY] * c_in, out_specs=list(out_specs) + [ANY] * c_out,
        out_shape=list(out_shape) + list(carry.out_shape), scratch_shapes=list(scratch_shapes) + list(carry.scratch),
        input_output_aliases={len(operands) + i: n_out + o for i, o in carry.aliases.items()},
        compiler_params=pltpu.CompilerParams(dimension_semantics=("arbitrary",) * len(grid), vmem_limit_bytes=vmem),
    )(*operands, *carry.ins)
    carry.set_result(res[n_out:])
    return res[:n_out]


def _mm(a, b, *, mode, outs, name, epilogue=None, extras=(), rextras=(), vecs=(), a_sel=None, b_sel=None,
        tm=None, tn=2048, tk=None, layout=None, carry=None, ep_rows=None, a_fn=None):
    a2, b2 = a.shape[-2:], b.shape[-2:]
    if mode == "nn":
        (M, K), (K2, N) = a2, b2
    elif mode == "nt":
        (M, K), (N, K2) = a2, b2
    else:
        (K, M), (K2, N) = a2, b2
    assert K == K2, (a.shape, b.shape, mode)
    if tm is None:
        tm = 1024 if mode == "tn" else 512
    if tk is None:
        tk = 1024 if mode == "tn" else 2048
    if layout is not None:
        kind, nslots = layout
        r, c = (M // 2, N // nslots) if kind == "col" else (M // nslots, N // 2)
        tm, tn = _fit(tm, r), _fit(tn, c)
        assert r % tm == 0 and c % tn == 0
    else:
        tm, tn = _fit(tm, M), _fit(tn, N)
    tk = _fit(tk, K)
    assert M % tm == 0 and N % tn == 0 and K % tk == 0, (M, N, K, tm, tn, tk)
    nk = K // tk
    grid = (N // tn, M // tm, nk)

    def spec(arr, sel, blk, imap):
        if arr.ndim == 3:
            return pl.BlockSpec((None,) + blk, lambda j, i, k: (sel,) + imap(j, i, k))
        return pl.BlockSpec(blk, imap)

    if mode == "tn":
        a_spec = spec(a, a_sel, (tk, tm), lambda j, i, k: (k, i))
    else:
        a_spec = spec(a, a_sel, (tm, tk), lambda j, i, k: (i, k))
    if mode == "nt":
        b_spec = spec(b, b_sel, (tn, tk), lambda j, i, k: (j, k))
    else:
        b_spec = spec(b, b_sel, (tk, tn), lambda j, i, k: (k, j))
    in_specs = [a_spec, b_spec]
    in_specs += [pl.BlockSpec((tm, tn), lambda j, i, k: (i, j)) for _ in extras]
    in_specs += [pl.BlockSpec((tm, e.shape[1]), lambda j, i, k: (i, 0)) for e in rextras]
    in_specs += [pl.BlockSpec((1, tn), lambda j, i, k: (0, j)) for _ in vecs]

    if layout is None:
        out_shape = [jax.ShapeDtypeStruct((M, N), d) for d in outs]
        out_specs = [pl.BlockSpec((tm, tn), lambda j, i, k: (i, j)) for _ in outs]
    else:
        assert len(outs) == 1
        out_shape = [jax.ShapeDtypeStruct((2, nslots, r, c), outs[0])]
        rb, cb = r // tm, c // tn
        if kind == "col":
            omap = lambda j, i, k: (i // rb, j // cb, i % rb, j % cb)
        else:
            omap = lambda j, i, k: (j // cb, i // rb, i % rb, j % cb)
        out_specs = [pl.BlockSpec((None, None, tm, tn), omap)]

    ne, nr, nv, no = len(extras), len(rextras), len(vecs), len(outs)
    dims = _DIMS[mode]

    def body(*refs):
        a_ref, b_ref = refs[0], refs[1]
        rest = refs[2:2 + ne + nr + nv]
        o_refs = refs[2 + ne + nr + nv:2 + ne + nr + nv + no]

        def finish(total):
            if epilogue is None:
                for o in o_refs:
                    o[...] = total.astype(o.dtype)
                return
            step = min(ep_rows or tm, tm)
            for r0 in range(0, tm, step):
                rows = slice(r0, r0 + step)
                tiles = [x[rows, :] for x in rest[:ne + nr]] + [x[...] for x in rest[ne + nr:]]
                for o, val in zip(o_refs, epilogue(total[rows, :], *tiles)):
                    o[rows, :] = val.astype(o.dtype)

        def product():
            a_tile = a_ref[...] if a_fn is None else a_fn(a_ref[...])
            return lax.dot_general(a_tile.astype(BF16), b_ref[...].astype(BF16), dims, preferred_element_type=F32)

        if nk == 1:
            finish(product())
            return
        acc = refs[-1]
        k = pl.program_id(2)

        @pl.when(k == 0)
        def _():
            acc[...] = product()

        @pl.when(k > 0)
        def _():
            acc[...] += product()

        @pl.when(k == nk - 1)
        def _():
            finish(acc[...])

    blk = (_nbytes((tm, tk), a.dtype) + _nbytes((tk, tn), b.dtype) + sum(_nbytes((tm, tn), e.dtype) for e in extras)
           + sum(_nbytes((tm, tn), d) for d in outs) + 2 * tm * tn * 4)
    res = _pcall(body, name=name, grid=grid, in_specs=in_specs, out_specs=out_specs, out_shape=out_shape,
                 operands=(a, b, *extras, *rextras, *vecs),
                 scratch_shapes=[pltpu.VMEM((tm, tn), F32)] if nk > 1 else [], vmem=_vmem(blk), carry=carry)
    return res[0] if no == 1 else tuple(res)


def _rows(fn, rows, vecs, outs, sums, *, tm, name, carry=None):
    S = rows[0].shape[0]
    tm = min(tm, S)
    assert S % tm == 0
    nr, nv, no, ns = len(rows), len(vecs), len(outs), len(sums)

    def body(*refs):
        vals = fn(*[r[...] for r in refs[:nr + nv]])
        o_refs = refs[nr + nv:nr + nv + no]
        s_refs = refs[nr + nv + no:]
        for o, val in zip(o_refs, vals[:no]):
            o[...] = val.astype(o.dtype)
        if ns:
            @pl.when(pl.program_id(0) == 0)
            def _():
                for s in s_refs:
                    s[...] = jnp.zeros_like(s)

            for s, val in zip(s_refs, vals[no:]):
                s[...] += jnp.sum(val.astype(F32), axis=0, keepdims=True)

    in_specs = [pl.BlockSpec((tm, r.shape[1]), lambda i: (i, 0)) for r in rows]
    in_specs += [pl.BlockSpec(v.shape, lambda i: (0, 0)) for v in vecs]
    out_specs = [pl.BlockSpec((tm, c), lambda i: (i, 0)) for c, _ in outs]
    out_specs += [pl.BlockSpec((1, c), lambda i: (0, 0)) for c in sums]
    out_shape = [jax.ShapeDtypeStruct((S, c), d) for c, d in outs]
    out_shape += [jax.ShapeDtypeStruct((1, c), F32) for c in sums]
    blk = sum(_nbytes((tm, r.shape[1]), r.dtype) for r in rows) + sum(_nbytes((tm, c), d) for c, d in outs)
    blk += 6 * tm * max(r.shape[1] for r in rows) * 4
    res = _pcall(body, name=name, grid=(S // tm,), in_specs=in_specs, out_specs=out_specs, out_shape=out_shape,
                 operands=(*rows, *vecs), vmem=_vmem(blk), carry=carry)
    return tuple(res)


def _ln_norm(z):
    mu = jnp.mean(z, axis=-1, keepdims=True)
    d = z - mu
    var = jnp.mean(d * d, axis=-1, keepdims=True)
    rstd = lax.rsqrt(var + LN_EPS)
    return d * rstd, rstd


def _ln(z, g, b):
    return _ln_norm(z)[0] * g + b


def _ln_bwd(dy, n, rstd, g):
    dn = dy * g
    return rstd * (dn - jnp.mean(dn, axis=-1, keepdims=True) - n * jnp.mean(dn * n, axis=-1, keepdims=True))


def _sq_relu(t):
    return jnp.square(jnp.maximum(t.astype(F32), 0.0))


def _sigmoid(x):
    return 1.0 / (1.0 + jnp.exp(-x))


def _per_head(x, fn):
    h = x.shape[1] // HEAD_DIM
    return jnp.concatenate([fn(x[:, i * HEAD_DIM:(i + 1) * HEAD_DIM], i) for i in range(h)], axis=1)


def _rot(x, cosf, sinf):
    return _per_head(x, lambda xh, i: xh * cosf + pltpu.roll(xh, HEAD_DIM // 2, 1) * sinf)


def _rot_t(dy, cosf, sinf):
    return _per_head(dy, lambda dh, i: dh * cosf + pltpu.roll(dh * sinf, HEAD_DIM // 2, 1))


def _shift_copies(win):
    rows = win.shape[1] - 8
    for s in range(1, 8):
        win[s, 0:rows, :] = win[0, s:s + rows, :]


def _rows_at(win, start):
    s = start % 8
    return win[s, start - s:start - s + CONV_ROWS, :]


def _conv_fwd(glu, dw, dwb, *, tm=256, tc=512, name="conv_fwd", carry=None):
    S, D = glu.shape
    tm, tc = min(tm, S), min(tc, D)
    ni = S // tm

    def body(cur_ref, prev_ref, dw_ref, dwb_ref, o_ref, win):
        i = pl.program_id(1)
        tail = prev_ref[tm - CONV_PAD:tm, :]
        win[0, 0:CONV_PAD, :] = jnp.where(i > 0, tail, jnp.zeros_like(tail))
        win[0, CONV_PAD:CONV_PAD + tm, :] = cur_ref[...]
        _shift_copies(win)
        first = CONV_PAD - CONV_WIDTH + 1
        for r0 in range(0, tm, CONV_ROWS):
            acc = jnp.zeros((CONV_ROWS, tc), F32) + dwb_ref[...]
            for k in range(CONV_WIDTH):
                acc = acc + _rows_at(win, r0 + first + k) * dw_ref[k:k + 1, :]
            o_ref[r0:r0 + CONV_ROWS, :] = acc

    return _pcall(
        body, name=name, grid=(D // tc, ni),
        in_specs=[pl.BlockSpec((tm, tc), lambda j, i: (i, j)),
                  pl.BlockSpec((tm, tc), lambda j, i: (jnp.maximum(i - 1, 0), j)),
                  pl.BlockSpec((CONV_PAD, tc), lambda j, i: (0, j)),
                  pl.BlockSpec((1, tc), lambda j, i: (0, j))],
        out_specs=[pl.BlockSpec((tm, tc), lambda j, i: (i, j))],
        out_shape=[jax.ShapeDtypeStruct((S, D), F32)],
        scratch_shapes=[pltpu.VMEM((8, tm + CONV_PAD, tc), F32)],
        operands=(glu, glu, dw, dwb), vmem=_vmem(8 * tm * tc * 4), carry=carry)[0]


def _conv_bwd(dc, glu, a_pre, g_pre, dw, ba, bg, *, tm=256, tc=512, name="conv_bwd", carry=None):
    S, D = dc.shape
    tm, tc = min(tm, S), min(tc, D)
    ni = S // tm

    def fold8(v):
        out = v[0:8]
        for r in range(8, CONV_ROWS, 8):
            out = out + v[r:r + 8]
        return out

    def body(dc_ref, dcn_ref, glu_ref, glup_ref, a_ref, g_ref, dw_ref, ba_ref, bg_ref,
             da_ref, dg_ref, ddw_ref, dba_ref, dbg_ref, dwin, gwin, taps):
        i = pl.program_id(1)

        @pl.when(i == 0)
        def _():
            ddw_ref[...] = jnp.zeros_like(ddw_ref)
            dba_ref[...] = jnp.zeros_like(dba_ref)
            dbg_ref[...] = jnp.zeros_like(dbg_ref)

        head = dcn_ref[0:CONV_PAD, :]
        dwin[0, 0:tm, :] = dc_ref[...]
        dwin[0, tm:tm + CONV_PAD, :] = jnp.where(i < ni - 1, head, jnp.zeros_like(head))
        tail = glup_ref[tm - CONV_PAD:tm, :]
        gwin[0, 0:CONV_PAD, :] = jnp.where(i > 0, tail, jnp.zeros_like(tail))
        gwin[0, CONV_PAD:CONV_PAD + tm, :] = glu_ref[...]
        _shift_copies(dwin)
        _shift_copies(gwin)
        taps[...] = jnp.zeros_like(taps)
        first = CONV_PAD - CONV_WIDTH + 1
        sum_a = jnp.zeros((8, tc), F32)
        sum_g = jnp.zeros((8, tc), F32)
        for r0 in range(0, tm, CONV_ROWS):
            dcur = dc_ref[r0:r0 + CONV_ROWS, :]
            dglu = jnp.zeros((CONV_ROWS, tc), F32)
            for k in range(CONV_WIDTH):
                dglu = dglu + _rows_at(dwin, r0 + CONV_WIDTH - 1 - k) * dw_ref[k:k + 1, :]
                taps[k] += fold8(dcur * _rows_at(gwin, r0 + first + k))
            a = a_ref[r0:r0 + CONV_ROWS, :] + ba_ref[...]
            sg = _sigmoid(g_ref[r0:r0 + CONV_ROWS, :] + bg_ref[...])
            da = dglu * sg
            dg = dglu * a * sg * (1.0 - sg)
            da_ref[r0:r0 + CONV_ROWS, :] = da.astype(BF16)
            dg_ref[r0:r0 + CONV_ROWS, :] = dg.astype(BF16)
            sum_a = sum_a + fold8(da)
            sum_g = sum_g + fold8(dg)
        ddw_ref[...] += jnp.sum(taps[...], axis=1)
        dba_ref[...] += jnp.sum(sum_a, axis=0, keepdims=True)
        dbg_ref[...] += jnp.sum(sum_g, axis=0, keepdims=True)

    tile = lambda f: pl.BlockSpec((tm, tc), f)
    vec = pl.BlockSpec((1, tc), lambda j, i: (0, j))
    return _pcall(
        body, name=name, grid=(D // tc, ni),
        in_specs=[tile(lambda j, i: (i, j)), tile(lambda j, i: (jnp.minimum(i + 1, ni - 1), j)),
                  tile(lambda j, i: (i, j)), tile(lambda j, i: (jnp.maximum(i - 1, 0), j)),
                  tile(lambda j, i: (i, j)), tile(lambda j, i: (i, j)),
                  pl.BlockSpec((CONV_PAD, tc), lambda j, i: (0, j)), vec, vec],
        out_specs=[tile(lambda j, i: (i, j)), tile(lambda j, i: (i, j)),
                   pl.BlockSpec((CONV_PAD, tc), lambda j, i: (0, j)), vec, vec],
        out_shape=[jax.ShapeDtypeStruct((S, D), BF16), jax.ShapeDtypeStruct((S, D), BF16),
                   jax.ShapeDtypeStruct((CONV_PAD, D), F32), jax.ShapeDtypeStruct((1, D), F32),
                   jax.ShapeDtypeStruct((1, D), F32)],
        scratch_shapes=[pltpu.VMEM((8, tm + CONV_PAD, tc), F32), pltpu.VMEM((8, tm + CONV_PAD, tc), F32),
                        pltpu.VMEM((CONV_PAD, 8, tc), F32)],
        operands=(dc, dc, glu, glu, a_pre, g_pre, dw, ba, bg), vmem=_vmem(16 * tm * tc * 4), carry=carry)


def _nt(a, b):
    return lax.dot_general(a, b, _DIMS["nt"], preferred_element_type=F32)


def _tn(a, b):
    return lax.dot_general(a, b, _DIMS["tn"], preferred_element_type=F32)


HEADS_TOGETHER = 8
STAT_LANES = 128


def _per_head_pack(cols):
    rows = cols[0].shape[0]
    lane = lax.broadcasted_iota(jnp.int32, (rows, STAT_LANES), 1)
    out = jnp.zeros((rows, STAT_LANES), F32)
    for h, col in enumerate(cols):
        out = jnp.where(lane == h, col, out)
    return out


def _window_mask(qi, kj, first_key):
    B = ATTN_BLOCK
    return ((kj < B) & (kj >= qi) & (kj >= first_key)) | ((kj >= B) & (kj - B <= qi))


def _attn_fwd(q_rot, k, v, g, dil):
    S, D = k.shape
    H = D // HEAD_DIM
    L = S // dil
    nb_count = L // ATTN_BLOCK
    B = ATTN_BLOCK

    def body(q_ref, kc_ref, kp_ref, vc_ref, vp_ref, o_ref, lse_ref):
        nb = pl.program_id(1)
        qi = lax.broadcasted_iota(jnp.int32, (B, 2 * B), 0)
        kj = lax.broadcasted_iota(jnp.int32, (B, 2 * B), 1)
        valid = _window_mask(qi, kj, jnp.where(nb > 0, 0, B))
        stats = []
        for h0 in range(0, H, HEADS_TOGETHER):
            heads = range(h0, min(h0 + HEADS_TOGETHER, H))
            hs = [slice(h * HEAD_DIM, (h + 1) * HEAD_DIM) for h in heads]
            kk = [jnp.concatenate([kp_ref[:, c], kc_ref[:, c]], axis=0) for c in hs]
            vv = [jnp.concatenate([vp_ref[:, c], vc_ref[:, c]], axis=0) for c in hs]
            s = [jnp.where(valid, _nt(q_ref[:, c], kk_) * ATTN_SCALE, NEG) for c, kk_ in zip(hs, kk)]
            m = [jnp.max(s_, axis=1, keepdims=True) for s_ in s]
            p = [jnp.exp(s_ - m_) for s_, m_ in zip(s, m)]
            l = [jnp.sum(p_, axis=1, keepdims=True) for p_ in p]
            o = [jnp.dot(p_.astype(BF16), vv_, preferred_element_type=F32) / l_ for p_, vv_, l_ in zip(p, vv, l)]
            for c, o_ in zip(hs, o):
                o_ref[:, c] = o_.astype(o_ref.dtype)
            stats += [m_ + jnp.log(l_) for m_, l_ in zip(m, l)]
        lse_ref[...] = _per_head_pack(stats)

    blk = lambda f: pl.BlockSpec((B, D), f)
    cur = lambda r, nb: (nb, r)
    prev = lambda r, nb: (jnp.maximum(nb - 1, 0), r)
    o, lse = pl.pallas_call(
        body, name=f"attn_fwd_g{g}", grid=(dil, nb_count),
        in_specs=[blk(lambda r, nb: (nb, r * N_GROUPS + g)), blk(cur), blk(prev), blk(cur), blk(prev)],
        out_specs=[blk(cur), pl.BlockSpec((B, STAT_LANES), cur)],
        out_shape=[jax.ShapeDtypeStruct((L, dil * D), BF16), jax.ShapeDtypeStruct((L, dil * STAT_LANES), F32)],
        compiler_params=pltpu.CompilerParams(dimension_semantics=("parallel", "arbitrary"),
                                             vmem_limit_bytes=_vmem(12 * B * D * 4)),
    )(q_rot.reshape(L, dil * N_GROUPS * D), k.reshape(L, dil * D), k.reshape(L, dil * D),
      v.reshape(L, dil * D), v.reshape(L, dil * D))
    return o.reshape(S, D), lse.reshape(S, STAT_LANES)


def _attn_bwd(q_rot, k, v, do, lse, dlt, g, dil, *, name, carry=None):
    S, D = k.shape
    H = D // HEAD_DIM
    L = S // dil
    nb_count = L // ATTN_BLOCK
    B = ATTN_BLOCK

    def body(q_ref, kc_ref, kp_ref, vc_ref, vp_ref, do_ref, l_ref, d_ref, dq_ref, dk_ref, dv_ref, keep_k, keep_v):
        s_id = pl.program_id(1)

        @pl.when(s_id == 0)
        def _():
            keep_k[...] = jnp.zeros_like(keep_k)
            keep_v[...] = jnp.zeros_like(keep_v)

        @pl.when(s_id < nb_count)
        def _():
            qi = lax.broadcasted_iota(jnp.int32, (B, 2 * B), 0)
            kj = lax.broadcasted_iota(jnp.int32, (B, 2 * B), 1)
            valid = _window_mask(qi, kj, jnp.where(s_id > 0, 0, B))
            for h0 in range(0, H, HEADS_TOGETHER):
                heads = list(range(h0, min(h0 + HEADS_TOGETHER, H)))
                hs = [slice(h * HEAD_DIM, (h + 1) * HEAD_DIM) for h in heads]
                q = [q_ref[:, c] for c in hs]
                dout = [do_ref[:, c] for c in hs]
                kk = [jnp.concatenate([kp_ref[:, c], kc_ref[:, c]], axis=0) for c in hs]
                vv = [jnp.concatenate([vp_ref[:, c], vc_ref[:, c]], axis=0) for c in hs]
                s = [jnp.where(valid, _nt(q_, kk_) * ATTN_SCALE, NEG) for q_, kk_ in zip(q, kk)]
                dp = [_nt(do_, vv_) for do_, vv_ in zip(dout, vv)]
                p = [jnp.exp(s_ - l_ref[:, h:h + 1]) for s_, h in zip(s, heads)]
                ds = [(p_ * (dp_ - d_ref[:, h:h + 1])).astype(BF16) for p_, dp_, h in zip(p, dp, heads)]
                dq = [jnp.dot(ds_, kk_, preferred_element_type=F32) * ATTN_SCALE for ds_, kk_ in zip(ds, kk)]
                dkk = [_tn(ds_, q_) * ATTN_SCALE for ds_, q_ in zip(ds, q)]
                dvv = [_tn(p_.astype(BF16), do_) for p_, do_ in zip(p, dout)]
                for c, dq_, dkk_, dvv_ in zip(hs, dq, dkk, dvv):
                    dq_ref[:, c] = dq_.astype(dq_ref.dtype)
                    dk_ref[:, c] = (keep_k[:, c] + dkk_[0:B]).astype(dk_ref.dtype)
                    dv_ref[:, c] = (keep_v[:, c] + dvv_[0:B]).astype(dv_ref.dtype)
                    keep_k[:, c] = dkk_[B:2 * B]
                    keep_v[:, c] = dvv_[B:2 * B]

        @pl.when(s_id == nb_count)
        def _():
            dk_ref[...] = keep_k[...].astype(dk_ref.dtype)
            dv_ref[...] = keep_v[...].astype(dv_ref.dtype)

    last = nb_count - 1
    blk = lambda f: pl.BlockSpec((B, D), f)
    cur = lambda r, s: (jnp.minimum(s, last), r)
    prev = lambda r, s: (jnp.maximum(jnp.minimum(s, last) - 1, 0), r)
    lag = lambda r, s: (jnp.maximum(s - 1, 0), r)
    qcur = lambda r, s: (jnp.minimum(s, last), r * N_GROUPS + g)
    qv = q_rot.reshape(L, dil * N_GROUPS * D)
    view = lambda t: t.reshape(L, dil * D)
    sview = lambda t: t.reshape(L, dil * STAT_LANES)
    stat = lambda f: pl.BlockSpec((B, STAT_LANES), f)
    dq, dk, dv = _pcall(
        body, name=name, grid=(dil, nb_count + 1),
        in_specs=[blk(qcur), blk(cur), blk(prev), blk(cur), blk(prev), blk(cur), stat(cur), stat(cur)],
        out_specs=[blk(cur), blk(lag), blk(lag)],
        out_shape=[jax.ShapeDtypeStruct((L, dil * D), BF16)] * 3,
        scratch_shapes=[pltpu.VMEM((B, D), F32), pltpu.VMEM((B, D), F32)],
        operands=(qv, view(k), view(k), view(v), view(v), view(do), sview(lse), sview(dlt)),
        vmem=_vmem(24 * B * D * 4), carry=carry)
    return dq.reshape(S, D), dk.reshape(S, D), dv.reshape(S, D)


def _mlp_ple_fwd(z1, h1b, p_l, W, vec, l, run, kv_vec=None):
    D = z1.shape[1]
    g1, b1, g2, b2 = vec

    t = run(_mm, h1b, W[f"mlp_up{l}"], mode="nn", outs=[BF16], tm=1024, name=f"mlp_up{l}")

    def z2_ep(acc, z1_t, g1_, b1_, g2_, b2_):
        z2 = ALPHA * _ln(z1_t, g1_, b1_) + acc
        return z2, _ln(z2, g2_, b2_)

    z2, h2b = run(_mm, t, W[f"mlp_down{l}"], mode="nn", outs=[F32, BF16], extras=[z1], vecs=[g1, b1, g2, b2], a_fn=_sq_relu,
                  epilogue=z2_ep, ep_rows=EPILOGUE_ROWS, name=f"mlp_down{l}")
    act = None
    pp = run(_mm, p_l, W[f"ple_proj{l}"], mode="nn", outs=[F32], name=f"ple_proj{l}")
    if kv_vec is None:
        gpre = run(_mm, h2b, W[f"ple_gate{l}"], mode="nn", outs=[F32], name=f"ple_gate{l}")
        return t, act, z2, h2b, pp, gpre

    def x1_ep(acc, z2_t, pp_t, g2_, b2_, kg, kb):
        x1, _ = _ple_out(z2_t, pp_t, acc, g2_, b2_)
        return acc, x1, _ln(x1, kg, kb)

    gpre, x1, kvn = run(_mm, h2b, W[f"ple_gate{l}"], mode="nn", outs=[F32, F32, BF16], extras=[z2, pp],
                        vecs=[g2, b2, *kv_vec], epilogue=x1_ep, ep_rows=EPILOGUE_ROWS, tm=256, name=f"ple_gate{l}")
    return t, act, z2, h2b, pp, gpre, x1, kvn


def _mlp_ple_bwd(dy, d_pp, d_gpre, p_l, z1, h1b, t, act, z2, h2b, wts, vec, l, run, produce):
    D = z1.shape[1]
    up, down, pp_w, pg_w = wts
    g1, b1, g2, b2 = vec
    produce(f"ple_proj{l}", run(_mm, p_l, d_pp, mode="tn", outs=[BF16], layout=("col", N_CHIPS), name=f"d_ple_proj{l}"))
    produce(f"ple_gate{l}", run(_mm, h2b, d_gpre, mode="tn", outs=[BF16], layout=("row", N_CHIPS), name=f"d_ple_gate{l}"))
    dh2 = run(_mm, d_gpre, pg_w, mode="nt", outs=[F32], extras=[dy], epilogue=lambda acc, e: (acc + e,),
              name=f"dh2_{l}")

    def ln2_bwd(dh2_t, z2_t, g2_):
        n, rstd = _ln_norm(z2_t)
        dz2 = _ln_bwd(dh2_t, n, rstd, g2_)
        return dz2, dz2, dh2_t * n, dh2_t

    dz2, dz2b, dg2, db2 = _rows(ln2_bwd, [dh2, z2], [g2], [(D, F32), (D, BF16)], [D, D], tm=256,
                                name=f"ln2_bwd{l}")
    produce(f"mlp_down{l}", run(_mm, t, dz2b, mode="tn", outs=[BF16], layout=("row", N_CHIPS), a_fn=_sq_relu,
                                name=f"d_mlp_down{l}"))
    dt = run(_mm, dz2b, down, mode="nt", outs=[BF16], extras=[t],
             epilogue=lambda acc, t_: (acc * 2.0 * jnp.maximum(t_.astype(F32), 0.0),), name=f"dt{l}")
    produce(f"mlp_up{l}", run(_mm, h1b, dt, mode="tn", outs=[BF16], layout=("col", N_CHIPS), name=f"d_mlp_up{l}"))
    dh1 = run(_mm, dt, up, mode="nt", outs=[F32], extras=[dz2], epilogue=lambda acc, e: (acc + ALPHA * e,),
              name=f"dh1_{l}")

    def ln1_bwd(dh1_t, z1_t, g1_):
        n, rstd = _ln_norm(z1_t)
        dz1 = _ln_bwd(dh1_t, n, rstd, g1_)
        return dz1, dz1, dh1_t * n, dh1_t

    dz1, dz1b, dg1, db1 = _rows(ln1_bwd, [dh1, z1], [g1], [(D, F32), (D, BF16)], [D, D], tm=256,
                                name=f"ln1_bwd{l}")
    return dz1, dz1b, (dg1, db1, dg2, db2)


def _ple_out(z2, pp, gpre, g2, b2):
    gt = _sigmoid(gpre)
    return _ln(z2, g2, b2) + pp * gt, gt


_GATHER_AT = {
    "conv_in_a": ("conv_w_out",),
    "conv_in_g": ("ple_gate0", "ple_proj0"),
    "conv_fwd": ("mlp_up0",),
    "mlp_up0": ("mlp_down0",),
    "mlp_down0": ("attn_w_q", "w_kv"),
    "kv_k": ("attn_w_o",),
    "kv_v": ("ple_proj1", "ple_gate1"),
    "attn_q": ("mlp_up1",),
    "mlp_up1": ("mlp_down1",),
}
_GATHER_FIRST = ("conv_w_in",)


def _local_step(x, p, cosf, sinf, target, W, V, shards=None, reducer=None, chip_arr=None):
    S, D = x.shape
    gw, gv = {}, {}
    if shards is not None:
        W = dict(_Gather(_GATHER_FIRST, shards, chip_arr).run_alone("gather_first"))

    def run(fn, *args, name, **kw):
        gather = _Gather(_GATHER_AT[name], shards, chip_arr) if (shards is not None and name in _GATHER_AT) else None
        carry = gather if reducer is None or gather is not None else reducer.carry(name)
        out = fn(*args, name=name, carry=carry, **kw)
        if gather is not None:
            W.update(gather.result)
        elif reducer is not None:
            reducer.carried()
        return out

    a_pre = run(_mm, x, W["conv_w_in"], b_sel=0, mode="nn", outs=[F32], name="conv_in_a")
    g_pre = run(_mm, x, W["conv_w_in"], b_sel=1, mode="nn", outs=[F32], name="conv_in_g")
    (glu,) = _rows(lambda a, g, ba, bg: ((a + ba) * _sigmoid(g + bg),), [a_pre, g_pre], [V["conv_b_a"], V["conv_b_g"]],
                   [(D, F32)], [], tm=256, name="glu_fwd")
    cv = run(_conv_fwd, glu, V["conv_dw"], V["conv_dw_b"], name="conv_fwd")

    def silu_ln(c, g_, b_):
        y = _ln(c, g_, b_)
        return (y * _sigmoid(y),)

    (sb,) = _rows(silu_ln, [cv], [V["conv_ln_g"], V["conv_ln_b"]], [(D, BF16)], [], tm=256, name="conv_ln_fwd")
    def z1_ep(acc, x_t, g_, b_):
        z1 = ALPHA * x_t + acc
        return z1, _ln(z1, g_, b_)

    vec0 = (V["ln1_g0"], V["ln1_b0"], V["ln2_g0"], V["ln2_b0"])
    vec1 = (V["ln1_g1"], V["ln1_b1"], V["ln2_g1"], V["ln2_b1"])
    z1_0, h1b_0 = _mm(sb, W["conv_w_out"], mode="nn", outs=[F32, BF16], extras=[x], vecs=[vec0[0], vec0[1]],
                      epilogue=z1_ep, ep_rows=EPILOGUE_ROWS, name="conv_out")
    t0, act0, z2_0, h2b_0, pp0, gpre0, x1, kvn = _mlp_ple_fwd(z1_0, h1b_0, p[0], W, vec0, 0, run,
                                                              (V["kv_ln_g"], V["kv_ln_b"]))

    rot_ep = lambda acc, c_, s_: (_rot(acc, c_, s_),)
    k_rot = run(_mm, kvn, W["w_kv"], b_sel=0, mode="nn", outs=[BF16], rextras=[cosf, sinf], epilogue=rot_ep, name="kv_k")
    v_b = run(_mm, kvn, W["w_kv"], b_sel=1, mode="nn", outs=[BF16], name="kv_v")
    q_rot = run(_mm, x1, W["attn_w_q"], mode="nn", outs=[BF16], rextras=[cosf, sinf], epilogue=rot_ep, name="attn_q")
    og, lg = [], []
    for g, dil in enumerate(DILATIONS):
        o_g, l_g = _attn_fwd(q_rot, k_rot, v_b, g, dil)
        og.append(o_g)
        lg.append(l_g)

    def merge(o0, o1, o2, l0, l1, l2):
        m = jnp.maximum(jnp.maximum(l0, l1), l2)
        e = [jnp.exp(l0 - m), jnp.exp(l1 - m), jnp.exp(l2 - m)]
        den = e[0] + e[1] + e[2]
        w = [e_g / den for e_g in e]
        o = _per_head(o0, lambda oh, h: sum(w[g][:, h:h + 1] * (o0, o1, o2)[g][:, h * HEAD_DIM:(h + 1) * HEAD_DIM].astype(F32)
                                            for g in range(N_GROUPS)))
        return o, m + jnp.log(den)

    ob, lse = _rows(merge, og + lg, [], [(D, BF16), (STAT_LANES, F32)], [], tm=256, name="attn_merge")
    z1_1, h1b_1 = _mm(ob, W["attn_w_o"], mode="nn", outs=[F32, BF16], extras=[x1], vecs=[vec1[0], vec1[1]],
                      epilogue=z1_ep, ep_rows=EPILOGUE_ROWS, name="attn_out")
    t1, act1, z2_1, h2b_1, pp1, gpre1 = _mlp_ple_fwd(z1_1, h1b_1, p[1], W, vec1, 1, run)
    wts0 = (W["mlp_up0"], W["mlp_down0"], W["ple_proj0"], W["ple_gate0"])
    wts1 = (W["mlp_up1"], W["mlp_down1"], W["ple_proj1"], W["ple_gate1"])

    def head(z2, pp, gpre, tgt, g2, b2):
        y, gt = _ple_out(z2, pp, gpre, g2, b2)
        err = y - tgt
        dy = err * (1.0 / D)
        return dy, dy * gt, dy * pp * gt * (1.0 - gt), 0.5 * err * err * (1.0 / D)

    dy1, d_pp1, d_gpre1, loss_cols = _rows(head, [z2_1, pp1, gpre1, target], [vec1[2], vec1[3]],
                                           [(D, F32), (D, BF16), (D, BF16)], [D], tm=256, name="loss_head")

    def produce(name, grad):
        gw[name] = grad
        if reducer is not None:
            reducer.produced(name, grad)

    dz1_1, dz1b_1, (gv["ln1_g1"], gv["ln1_b1"], gv["ln2_g1"], gv["ln2_b1"]) = _mlp_ple_bwd(
        dy1, d_pp1, d_gpre1, p[1], z1_1, h1b_1, t1, act1, z2_1, h2b_1, wts1, vec1, 1, run, produce)
    produce("attn_w_o", run(_mm, ob, dz1b_1, mode="tn", outs=[BF16], layout=("row", N_CHIPS), name="d_attn_w_o"))

    do_b = run(_mm, dz1b_1, W["attn_w_o"], mode="nt", outs=[BF16], name="attn_do")

    def delta(do_t, o_t):
        prod = do_t.astype(F32) * o_t.astype(F32)
        H = D // HEAD_DIM
        return (_per_head_pack([jnp.sum(prod[:, h * HEAD_DIM:(h + 1) * HEAD_DIM], axis=1, keepdims=True) for h in range(H)]),)

    (dlt,) = _rows(delta, [do_b, ob], [], [(STAT_LANES, F32)], [], tm=256, name="attn_delta")
    dqs, dks, dvs = [], [], []
    for g, dil in enumerate(DILATIONS):
        dq_g, dk_g, dv_g = run(_attn_bwd, q_rot, k_rot, v_b, do_b, lse, dlt, g, dil, name=f"attn_bwd_g{g}")
        dqs.append(dq_g)
        dks.append(dk_g)
        dvs.append(dv_g)

    def unrot(q0, q1, q2, k0, k1, k2, v0, v1, v2, c_, s_):
        dq = jnp.concatenate([_rot_t(t_.astype(F32), c_, s_) for t_ in (q0, q1, q2)], axis=1)
        f = lambda t_: t_.astype(F32)
        return dq, _rot_t(f(k0) + f(k1) + f(k2), c_, s_), f(v0) + f(v1) + f(v2)

    dq, dk, dv = run(_rows, unrot, dqs + dks + dvs + [cosf, sinf], [], [(N_GROUPS * D, BF16), (D, BF16), (D, BF16)], [],
                     tm=128, name="attn_unrot")
    produce("attn_w_q", run(_mm, x1, dq, mode="tn", outs=[BF16], layout=("col", N_CHIPS), name="d_attn_w_q"))
    dx1_q = run(_mm, dq, W["attn_w_q"], mode="nt", outs=[F32], extras=[dz1_1],
                epilogue=lambda acc, e: (acc + ALPHA * e,), name="dx1_q")
    produce("w_kv", jnp.concatenate(
        [run(_mm, kvn, dk, mode="tn", outs=[BF16], layout=("col", 2), name="d_w_kv_k"),
         run(_mm, kvn, dv, mode="tn", outs=[BF16], layout=("col", 2), name="d_w_kv_v")], axis=1))
    dkvn_k = run(_mm, dk, W["w_kv"], b_sel=0, mode="nt", outs=[F32], name="dkvn_k")
    dkvn = run(_mm, dv, W["w_kv"], b_sel=1, mode="nt", outs=[F32], extras=[dkvn_k], epilogue=lambda acc, e: (acc + e,),
               name="dkvn_v")

    def x1_bwd(dx1q_t, dkvn_t, x1_t, pp, gpre, kg):
        n, rstd = _ln_norm(x1_t)
        dy = dx1q_t + _ln_bwd(dkvn_t, n, rstd, kg)
        gt = _sigmoid(gpre)
        return dy, dy * gt, dy * pp * gt * (1.0 - gt), dkvn_t * n, dkvn_t

    dy0, d_pp0, d_gpre0, gv["kv_ln_g"], gv["kv_ln_b"] = _rows(
        x1_bwd, [dx1_q, dkvn, x1, pp0, gpre0], [V["kv_ln_g"]], [(D, F32), (D, BF16), (D, BF16)], [D, D], tm=256,
        name="x1_bwd")

    dz1_0, dz1b_0, (gv["ln1_g0"], gv["ln1_b0"], gv["ln2_g0"], gv["ln2_b0"]) = _mlp_ple_bwd(
        dy0, d_pp0, d_gpre0, p[0], z1_0, h1b_0, t0, act0, z2_0, h2b_0, wts0, vec0, 0, run, produce)
    produce("conv_w_out", run(_mm, sb, dz1b_0, mode="tn", outs=[BF16], layout=("row", N_CHIPS), name="d_conv_w_out"))
    ds = run(_mm, dz1b_0, W["conv_w_out"], mode="nt", outs=[F32], name="conv_ds")

    def conv_ln_bwd(ds_t, c_t, g_, b_):
        n, rstd = _ln_norm(c_t)
        y = n * g_ + b_
        sg = _sigmoid(y)
        dln = ds_t * sg * (1.0 + y * (1.0 - sg))
        dc = _ln_bwd(dln, n, rstd, g_)
        return dc, dln * n, dln, dc

    dc, gv["conv_ln_g"], gv["conv_ln_b"], gv["conv_dw_b"] = _rows(
        conv_ln_bwd, [ds, cv], [V["conv_ln_g"], V["conv_ln_b"]], [(D, F32)], [D, D, D], tm=256, name="conv_ln_bwd")
    da, dg, gv["conv_dw"], gv["conv_b_a"], gv["conv_b_g"] = run(
        _conv_bwd, dc, glu, a_pre, g_pre, V["conv_dw"], V["conv_b_a"], V["conv_b_g"], name="conv_bwd")
    produce("conv_w_in", jnp.concatenate(
        [run(_mm, x, da, mode="tn", outs=[BF16], layout=("col", 2), name="d_conv_w_in_a"),
         run(_mm, x, dg, mode="tn", outs=[BF16], layout=("col", 2), name="d_conv_w_in_g")], axis=1))
    dx_a = run(_mm, da, W["conv_w_in"], b_sel=0, mode="nt", outs=[F32], extras=[dz1_0],
               epilogue=lambda acc, e: (acc + ALPHA * e,), name="dx_a")
    grad_x = run(_mm, dg, W["conv_w_in"], b_sel=1, mode="nt", outs=[F32], extras=[dx_a],
                 epilogue=lambda acc, e: (acc + e,), name="dx_g")
    if reducer is not None:
        reducer.carry("share_last").run_alone("share_last")
        reducer.carried()
    return loss_cols, grad_x, gw, gv


def _place():
    x, y, c = lax.axis_index("x"), lax.axis_index("y"), lax.axis_index("c")
    chips = [(1 - x, y), (x, 1 - y), (1 - x, 1 - y)]
    return x, y, c, chips


def _remote(src, dst, ssem, rsem, dev):
    return pltpu.make_async_remote_copy(src_ref=src, dst_ref=dst, send_sem=ssem, recv_sem=rsem, device_id=dev,
                                        device_id_type=MESH)


def _allgather8(block, name):
    R, C = block.shape

    def body(x_ref, out_ref, send_sems, recv_sems, local_sem):
        x, y, c, chips = _place()
        me, sibling = (x, y, c), (x, y, 1 - c)

        def slot(px, py, pc):
            return out_ref.at[4 * px + 2 * py + pc]

        def copy(k, blockpos, to, src=None):
            return _remote(slot(*blockpos) if src is None else src, slot(*blockpos), send_sems.at[k], recv_sems.at[k], to)

        mine = pltpu.make_async_copy(x_ref, slot(*me), local_sem)
        mine.start()
        first = [copy(0, me, sibling, src=x_ref)]
        first += [copy(1 + j, me, (*chip, c), src=x_ref) for j, chip in enumerate(chips)]
        for cp in first:
            cp.start()
        passed = [copy(4 + j, (*chip, c), sibling) for j, chip in enumerate(chips)]
        for j, chip in enumerate(chips):
            copy(1 + j, (*chip, c), me).wait_recv()
            passed[j].start()
        copy(0, sibling, me).wait_recv()
        for j, chip in enumerate(chips):
            copy(4 + j, (*chip, 1 - c), me).wait_recv()
        for cp in first + passed:
            cp.wait_send()
        mine.wait()

    return pl.pallas_call(
        body, name=name, out_shape=jax.ShapeDtypeStruct((8, R, C), block.dtype),
        in_specs=[pl.BlockSpec(memory_space=pltpu.VMEM)], out_specs=pl.BlockSpec(memory_space=pltpu.VMEM),
        scratch_shapes=[pltpu.SemaphoreType.DMA((7,)), pltpu.SemaphoreType.DMA((7,)), pltpu.SemaphoreType.DMA],
        compiler_params=pltpu.CompilerParams(vmem_limit_bytes=_vmem(10 * _nbytes((R, C), block.dtype))),
    )(block)


_MATS = (
    ("conv_w_in", "conv_w_in", 0, "col", True),
    ("conv_w_out", "conv_w_out", 0, "row", False),
    ("mlp_up0", "mlp_up", 0, "col", False),
    ("mlp_down0", "mlp_down", 0, "row", False),
    ("ple_proj0", "ple_proj", 0, "col", False),
    ("ple_gate0", "ple_gate", 0, "row", False),
    ("w_kv", "w_kv", None, "col", True),
    ("attn_w_q", "attn_w_q", 0, "col", False),
    ("attn_w_o", "attn_w_o", 0, "row", False),
    ("mlp_up1", "mlp_up", 1, "col", False),
    ("mlp_down1", "mlp_down", 1, "row", False),
    ("ple_proj1", "ple_proj", 1, "col", False),
    ("ple_gate1", "ple_gate", 1, "row", False),
)


class _Carry:
    result = None
    aliases = {}

    def set_result(self, outs):
        self.result = dict(zip(self.names, outs))

    def run_alone(self, name):
        n_in, n_out = len(self.ins), len(self.out_shape)

        def body(*refs):
            in_refs, out_refs, sems = refs[:n_in], refs[n_in:n_in + n_out], refs[n_in + n_out:]
            self.start(in_refs, out_refs, sems)
            self.finish(in_refs, out_refs, sems)

        outs = pl.pallas_call(body, name=name, out_shape=self.out_shape, in_specs=[ANY] * n_in, out_specs=[ANY] * n_out,
                              scratch_shapes=self.scratch, input_output_aliases=dict(self.aliases))(*self.ins)
        self.set_result(outs)
        return self.result


class _Gather(_Carry):
    def __init__(self, names, shards, chip_arr):
        mats = [m for m in _MATS if m[0] in names]
        srcs = sorted({m[1] for m in mats})
        self.names = [m[0] for m in mats]
        self.out_shape, self.geo, placed = [], [], []
        for name, src, layer, kind, split in mats:
            s = shards[src]
            ks, ns = s.shape[-2:]
            K, N = (ks, ns * N_CHIPS) if kind == "col" else (ks * N_CHIPS, ns)
            self.out_shape.append(jax.ShapeDtypeStruct((2, K, N // 2) if split else (K, N), BF16))
            self.geo.append((srcs.index(src), layer if s.ndim == 3 else None, kind, split, K, N))
            placed.append(_place_shard(s, layer if s.ndim == 3 else None, kind, split, chip_arr, f"place_{name}"))
        T = len(mats)
        self.ins = [shards[n] for n in srcs] + placed
        self.aliases = {len(srcs) + t: t for t in range(T)}
        self.scratch = [pltpu.SemaphoreType.DMA((3 * T,)) for _ in range(4)]
        self.result = None

    def _copies(self, in_refs, out_refs, sems):
        geo, T = self.geo, len(self.geo)
        s_ici, r_ici, s_d2d, r_d2d = sems
        x, y, c, chips = _place()
        me = 2 * x + y
        sibling = (x, y, 1 - c)
        idx = [2 * cx + cy for cx, cy in chips]

        def src_ref(t):
            i, layer, _, _, _, _ = geo[t]
            return in_refs[i] if layer is None else in_refs[i].at[layer]

        def src_half(t, h):
            _, _, kind, _, K, N = geo[t]
            if kind == "col":
                return src_ref(t).at[pl.ds(h * (K // 2), K // 2), :]
            return src_ref(t).at[:, pl.ds(h * (N // 2), N // 2)]

        def dst(t, j, h):
            _, _, kind, split, K, N = geo[t]
            n, k = N // N_CHIPS, K // N_CHIPS
            if kind == "col":
                rows = slice(None) if h is None else pl.ds(h * (K // 2), K // 2)
                if split:
                    return out_refs[t].at[j // 2, rows, pl.ds((j % 2) * n, n)]
                return out_refs[t].at[rows, pl.ds(j * n, n)]
            cols = slice(None) if h is None else pl.ds(h * (N // 2), N // 2)
            return out_refs[t].at[pl.ds(j * k, k), cols]

        sends = [_remote(src_half(t, c), dst(t, me, c), s_ici.at[3 * t + kk], r_ici.at[3 * t + kk], (*chips[kk], c))
                 for t in range(T) for kk in range(3)]
        hops = []
        for t in range(T):
            for kk in range(3):
                mine, theirs = dst(t, idx[kk], c), dst(t, idx[kk], 1 - c)
                hops.append((_remote(mine, mine, s_ici.at[3 * t + kk], r_ici.at[3 * t + kk], sibling),
                             _remote(mine, mine, s_d2d.at[3 * t + kk], r_d2d.at[3 * t + kk], sibling),
                             _remote(theirs, theirs, s_d2d.at[3 * t + kk], r_d2d.at[3 * t + kk], sibling)))
        return sends, hops

    def start(self, in_refs, out_refs, sems):
        for cp in self._copies(in_refs, out_refs, sems)[0]:
            cp.start()

    def finish(self, in_refs, out_refs, sems):
        sends, hops = self._copies(in_refs, out_refs, sems)
        for landed, forward, _ in hops:
            landed.wait_recv()
            forward.start()
        for _, _, from_sibling in hops:
            from_sibling.wait_recv()
        for cp in sends + [h[1] for h in hops]:
            cp.wait_send()


def _place_shard(shard, layer, kind, split, chip_arr, name):
    ks, ns = shard.shape[-2:]
    K, N = (ks, ns * N_CHIPS) if kind == "col" else (ks * N_CHIPS, ns)
    tr = _fit(256, ks)
    nb = ks // tr
    if shard.ndim == 3:
        in_spec = pl.BlockSpec((None, tr, ns), lambda i, me: (layer, i, 0))
    else:
        in_spec = pl.BlockSpec((tr, ns), lambda i, me: (i, 0))
    if kind == "row":
        out_shape, out_spec = (K, N), pl.BlockSpec((tr, ns), lambda i, me: (me[0] * nb + i, 0))
    elif split:
        out_shape, out_spec = (2, K, N // 2), pl.BlockSpec((None, tr, ns), lambda i, me: (me[0] // 2, i, me[0] % 2))
    else:
        out_shape, out_spec = (K, N), pl.BlockSpec((tr, ns), lambda i, me: (i, me[0]))

    def body(me_ref, s_ref, o_ref):
        o_ref[...] = s_ref[...]

    return pl.pallas_call(
        body, name=name, out_shape=jax.ShapeDtypeStruct(out_shape, BF16),
        grid_spec=pltpu.PrefetchScalarGridSpec(num_scalar_prefetch=1, grid=(nb,), in_specs=[in_spec], out_specs=out_spec),
        compiler_params=pltpu.CompilerParams(dimension_semantics=("parallel",), vmem_limit_bytes=_vmem(4 * tr * ns * 2)),
    )(chip_arr, shard)


class _Multi(_Carry):
    def __init__(self, parts):
        self.parts = parts
        self.ins = [a for p in parts for a in p.ins]
        self.out_shape = [a for p in parts for a in p.out_shape]
        self.scratch = [a for p in parts for a in p.scratch]
        self.aliases, n_in, n_out = {}, 0, 0
        for p in parts:
            self.aliases.update({n_in + i: n_out + o for i, o in p.aliases.items()})
            n_in, n_out = n_in + len(p.ins), n_out + len(p.out_shape)

    def _split(self, seq, field):
        out, at = [], 0
        for p in self.parts:
            n = len(getattr(p, field))
            out.append(seq[at:at + n])
            at += n
        return out

    def _each(self, method, in_refs, out_refs, sems):
        for p, i, o, s in zip(self.parts, self._split(in_refs, "ins"), self._split(out_refs, "out_shape"),
                              self._split(sems, "scratch")):
            getattr(p, method)(i, o, s)

    def start(self, in_refs, out_refs, sems):
        self._each("start", in_refs, out_refs, sems)

    def finish(self, in_refs, out_refs, sems):
        self._each("finish", in_refs, out_refs, sems)

    def set_result(self, outs):
        for p, o in zip(self.parts, self._split(list(outs), "out_shape")):
            p.set_result(o)


class _PairSend(_Carry):
    def __init__(self, grads):
        self.names = list(grads)
        self.ins = [grads[n] for n in self.names]
        self.out_shape = [jax.ShapeDtypeStruct(a.shape[1:], BF16) for a in self.ins]
        T = len(self.names)
        self.scratch = [pltpu.SemaphoreType.DMA((T,)), pltpu.SemaphoreType.DMA((T,))]

    def _copies(self, in_refs, out_refs, sems):
        x, y, c, _ = _place()
        return [_remote(in_refs[t].at[1 - c], out_refs[t], sems[0].at[t], sems[1].at[t], (x, y, 1 - c))
                for t in range(len(self.names))]

    def start(self, in_refs, out_refs, sems):
        for cp in self._copies(in_refs, out_refs, sems):
            cp.start()

    def finish(self, in_refs, out_refs, sems):
        for cp in self._copies(in_refs, out_refs, sems):
            cp.wait()


class _ChipScatter(_Carry):
    def __init__(self, sums):
        self.names = list(sums)
        T = len(self.names)
        self.ins = [sums[n][0] for n in self.names] + [sums[n][1] for n in self.names]
        self.out_shape = [jax.ShapeDtypeStruct(a.shape, BF16) for a in self.ins[:T]]
        self.aliases = {T + t: t for t in range(T)}
        self.scratch = [pltpu.SemaphoreType.DMA((3 * T,)), pltpu.SemaphoreType.DMA((3 * T,))]

    def _copies(self, in_refs, out_refs, sems):
        ssem, rsem = sems
        x, y, c, chips = _place()
        me = 2 * x + y
        idx = [2 * cx + cy for cx, cy in chips]
        T = len(self.names)
        sends = [_remote(in_refs[t].at[idx[kk]], out_refs[t].at[me], ssem.at[3 * t + kk], rsem.at[3 * t + kk],
                         (*chips[kk], c)) for t in range(T) for kk in range(3)]
        lands = [_remote(out_refs[t].at[idx[kk]], out_refs[t].at[idx[kk]], ssem.at[3 * t + kk], rsem.at[3 * t + kk],
                         (*chips[kk], c)) for t in range(T) for kk in range(3)]
        return sends, lands

    def start(self, in_refs, out_refs, sems):
        for cp in self._copies(in_refs, out_refs, sems)[0]:
            cp.start()

    def finish(self, in_refs, out_refs, sems):
        sends, lands = self._copies(in_refs, out_refs, sems)
        for cp in lands:
            cp.wait_recv()
        for cp in sends:
            cp.wait_send()


class _PairShare(_Carry):
    def __init__(self, halves):
        self.names = list(halves)
        self.ins = [halves[n] for n in self.names]
        self.out_shape = [jax.ShapeDtypeStruct(a.shape, F32) for a in self.ins]
        T = len(self.names)
        self.aliases = {t: t for t in range(T)}
        self.scratch = [pltpu.SemaphoreType.DMA((T,)), pltpu.SemaphoreType.DMA((T,))]

    def _copies(self, in_refs, out_refs, sems):
        ssem, rsem = sems
        x, y, c, _ = _place()
        sibling = (x, y, 1 - c)
        T = len(self.names)
        sends = [_remote(out_refs[t].at[c], out_refs[t].at[c], ssem.at[t], rsem.at[t], sibling) for t in range(T)]
        lands = [_remote(out_refs[t].at[1 - c], out_refs[t].at[1 - c], ssem.at[t], rsem.at[t], sibling) for t in range(T)]
        return sends, lands

    def start(self, in_refs, out_refs, sems):
        for cp in self._copies(in_refs, out_refs, sems)[0]:
            cp.start()

    def finish(self, in_refs, out_refs, sems):
        sends, lands = self._copies(in_refs, out_refs, sems)
        for cp in lands:
            cp.wait_recv()
        for cp in sends:
            cp.wait_send()


def _pair_sum(own, landed, c_arr, name):
    _, ns, r, cc = own.shape
    rows = ns * r
    tr = _fit(512, rows)

    def body(c_ref, a_ref, b_ref, o_ref, o2_ref):
        total = (a_ref[...].astype(F32) + b_ref[...].astype(F32)).astype(o_ref.dtype)
        o_ref[...] = total
        o2_ref[...] = total

    tile = pl.BlockSpec((tr, cc), lambda i, c_ref: (i, 0))
    out = pl.pallas_call(
        body, name=name, out_shape=[jax.ShapeDtypeStruct((rows, cc), BF16)] * 2,
        grid_spec=pltpu.PrefetchScalarGridSpec(
            num_scalar_prefetch=1, grid=(rows // tr,),
            in_specs=[pl.BlockSpec((None, tr, cc), lambda i, c_ref: (c_ref[0], i, 0)), tile], out_specs=[tile, tile]),
        compiler_params=pltpu.CompilerParams(dimension_semantics=("parallel",), vmem_limit_bytes=_vmem(8 * tr * cc * 4)),
    )(c_arr, own.reshape(2, rows, cc), landed.reshape(rows, cc))
    return out[0].reshape(ns, r, cc), out[1].reshape(ns, r, cc)


def _chip_sum(parts, c_arr, name):
    _, r, cc = parts.shape
    tr = _fit(256, r)

    def body(c_ref, p_ref, o_ref):
        acc = p_ref[0].astype(F32)
        for j in range(1, N_CHIPS):
            acc = acc + p_ref[j].astype(F32)
        o_ref[...] = acc

    return pl.pallas_call(
        body, name=name, out_shape=jax.ShapeDtypeStruct((2, r, cc), F32),
        grid_spec=pltpu.PrefetchScalarGridSpec(
            num_scalar_prefetch=1, grid=(r // tr,),
            in_specs=[pl.BlockSpec((N_CHIPS, tr, cc), lambda i, c_ref: (0, i, 0))],
            out_specs=pl.BlockSpec((None, tr, cc), lambda i, c_ref: (c_ref[0], i, 0))),
        compiler_params=pltpu.CompilerParams(dimension_semantics=("parallel",), vmem_limit_bytes=_vmem(12 * tr * cc * 4)),
    )(c_arr, parts)


def _adamw_math(w, g, m, v):
    m2 = ADAM_B1 * m + (1.0 - ADAM_B1) * g
    v2 = ADAM_B2 * v + (1.0 - ADAM_B2) * jnp.square(g)
    m_hat = m2 / (1.0 - ADAM_B1 ** ADAM_STEP)
    v_hat = v2 / (1.0 - ADAM_B2 ** ADAM_STEP)
    delta = -ADAM_LR * (m_hat / (jnp.sqrt(v_hat) + ADAM_EPS) + ADAM_WD * w)
    return delta, m2, v2


def _adamw_mat(g2, w, m, v, layer, kind, prev, name):
    shape = w.shape
    ks, ns = shape[-2:]
    _, r, cc = g2.shape
    tr, tc = _fit(256, r), _fit(1024, cc)
    assert (r, cc) == ((ks // 2, ns) if kind == "col" else (ks, ns // 2))
    assert r % tr == 0 and cc % tc == 0
    rb, cb = r // tr, cc // tc
    if kind == "col":
        g_spec = pl.BlockSpec((None, tr, tc), lambda i, j: (i // rb, i % rb, j))
    else:
        g_spec = pl.BlockSpec((None, tr, tc), lambda i, j: (j // cb, i, j % cb))
    if w.ndim == 3:
        w_spec = pl.BlockSpec((None, tr, tc), lambda i, j: (layer, i, j))
    else:
        w_spec = pl.BlockSpec((tr, tc), lambda i, j: (i, j))
    n_prev = 0 if prev is None else 4

    def body(*refs):
        g_ref, w_ref, m_ref, v_ref = refs[:4]
        go_ref, d_ref, mo_ref, vo_ref = refs[4 + n_prev:]
        g = g_ref[...]
        delta, m2, v2 = _adamw_math(w_ref[...], g, m_ref[...], v_ref[...])
        go_ref[...] = g
        d_ref[...] = delta
        mo_ref[...] = m2
        vo_ref[...] = v2

    return pl.pallas_call(
        body, name=name, grid=(ks // tr, ns // tc),
        in_specs=[g_spec, w_spec, w_spec, w_spec] + [ANY] * n_prev, out_specs=[w_spec] * 4,
        out_shape=[jax.ShapeDtypeStruct(shape, F32)] * 4,
        input_output_aliases={4 + i: i for i in range(n_prev)},
        compiler_params=pltpu.CompilerParams(dimension_semantics=("parallel", "parallel"),
                                             vmem_limit_bytes=_vmem(16 * tr * tc * 4)),
    )(g2, w, m, v, *(prev or ()))


def _adamw_small(g, w, m, v, name):
    def body(g_ref, w_ref, m_ref, v_ref, d_ref, mo_ref, vo_ref):
        delta, m2, v2 = _adamw_math(w_ref[...], g_ref[...], m_ref[...], v_ref[...])
        d_ref[...] = delta
        mo_ref[...] = m2
        vo_ref[...] = v2

    return pl.pallas_call(body, name=name, out_shape=[jax.ShapeDtypeStruct(w.shape, F32)] * 3)(g, w, m, v)


def _sum8(parts, name):
    def body(p_ref, o_ref):
        acc = p_ref[0]
        for j in range(1, 8):
            acc = acc + p_ref[j]
        o_ref[...] = acc

    return pl.pallas_call(body, name=name, out_shape=jax.ShapeDtypeStruct(parts.shape[1:], F32),
                          compiler_params=pltpu.CompilerParams(vmem_limit_bytes=_vmem(12 * _nbytes(parts.shape[1:], F32))))(parts)


_REDUCE_AT = {
    "d_ple_gate1": (("A", "ple_proj1"),),
    "dh2_1": (("A", "ple_gate1"),),
    "d_mlp_down1": (("B", "ple_proj1"), ("B", "ple_gate1")),
    "dt1": (("A", "mlp_down1"),),
    "d_mlp_up1": (("B", "mlp_down1"), ("C", "ple_proj1"), ("C", "ple_gate1")),
    "dh1_1": (("A", "mlp_up1"),),
    "d_attn_w_o": (("C", "mlp_down1"),),
    "attn_do": (("A", "attn_w_o"),),
    "attn_bwd_g0": (("B", "mlp_up1"),),
    "attn_bwd_g1": (("B", "attn_w_o"),),
    "attn_unrot": (("C", "mlp_up1"), ("C", "attn_w_o")),
    "dx1_q": (("A", "attn_w_q"),),
    "dkvn_k": (("A", "w_kv"),),
    "d_ple_gate0": (("A", "ple_proj0"),),
    "dh2_0": (("A", "ple_gate0"),),
    "d_mlp_down0": (("B", "attn_w_q"), ("B", "ple_proj0")),
    "dt0": (("B", "w_kv"), ("B", "ple_gate0"), ("A", "mlp_down0")),
    "d_mlp_up0": (("B", "mlp_down0"), ("C", "attn_w_q"), ("C", "ple_proj0"), ("C", "w_kv"), ("C", "ple_gate0")),
    "dh1_0": (("A", "mlp_up0"),),
    "d_conv_w_out": (("C", "mlp_down0"),),
    "conv_ds": (("A", "conv_w_out"),),
    "conv_bwd": (("B", "mlp_up0"), ("B", "conv_w_out")),
    "d_conv_w_in_g": (("C", "mlp_up0"), ("C", "conv_w_out")),
    "dx_a": (("A", "conv_w_in"),),
    "dx_g": (("B", "conv_w_in"),),
    "share_last": (("C", "conv_w_in"),),
}


class _Reducer:
    def __init__(self, w, mom, var, c_arr):
        self.w, self.mom, self.var, self.c_arr = w, mom, var, c_arr
        self.mats = {m[0]: m for m in _MATS}
        self.grads, self.pair_sums, self.chip_sums, self.out = {}, {}, {}, {}

    def produced(self, name, grad):
        self.grads[name] = grad

    def carry(self, call):
        parts = []
        for cls, stage, src in ((_PairSend, "A", self.grads), (_ChipScatter, "B", self.pair_sums),
                                (_PairShare, "C", self.chip_sums)):
            names = [n for s, n in _REDUCE_AT.get(call, ()) if s == stage]
            if names:
                parts.append((stage, cls({n: src[n] for n in names})))
        self._parts = parts
        return _Multi([p for _, p in parts]) if parts else None

    def carried(self):
        for stage, part in self._parts:
            for name, val in part.result.items():
                if stage == "A":
                    self.pair_sums[name] = _pair_sum(self.grads[name], val, self.c_arr, f"pair_sum_{name}")
                elif stage == "B":
                    self.chip_sums[name] = _chip_sum(val, self.c_arr, f"chip_sum_{name}")
                else:
                    _, src, layer, kind, _ = self.mats[name]
                    self.out[src] = _adamw_mat(val, self.w[src], self.mom[src], self.var[src], layer or 0, kind,
                                               self.out.get(src), f"adamw_{name}")
        self._parts = []


_WEIGHTS = ("conv_w_in", "conv_b_in", "conv_dw", "conv_dw_b", "conv_ln_g", "conv_ln_b", "conv_w_out", "kv_ln_g",
            "kv_ln_b", "w_kv", "attn_w_q", "attn_w_o", "ln1_g", "ln1_b", "mlp_up", "mlp_down", "ln2_g", "ln2_b",
            "ple_proj", "ple_gate")
_SHARDED_VECS = ("conv_b_in", "conv_dw", "conv_dw_b", "conv_ln_g", "conv_ln_b")
_REPLICATED_VECS = ("kv_ln_g", "kv_ln_b", "ln1_g", "ln1_b", "ln2_g", "ln2_b")


def _pad_rows(a, rows):
    return jnp.concatenate([a, jnp.zeros((rows - a.shape[0], a.shape[1]), a.dtype)], axis=0) if a.shape[0] < rows else a


def _pack_sharded(d):
    n = d["conv_dw_b"].shape[-1]
    rows = [d["conv_b_in"].reshape(2, n), d["conv_dw"].reshape(CONV_WIDTH, n), d["conv_dw_b"].reshape(1, n),
            d["conv_ln_g"].reshape(1, n), d["conv_ln_b"].reshape(1, n)]
    return _pad_rows(jnp.concatenate(rows, axis=0), 40)


def _unpack_sharded(pack, like):
    n = pack.shape[1]
    return {"conv_b_in": pack[0:2].reshape(like["conv_b_in"].shape),
            "conv_dw": pack[2:2 + CONV_WIDTH].reshape(like["conv_dw"].shape),
            "conv_dw_b": pack[33:34].reshape(like["conv_dw_b"].shape),
            "conv_ln_g": pack[34:35].reshape(like["conv_ln_g"].shape),
            "conv_ln_b": pack[35:36].reshape(like["conv_ln_b"].shape)}


def _pack_replicated(d):
    D = d["kv_ln_g"].shape[-1]
    rows = [d[n].reshape(-1, D) for n in _REPLICATED_VECS]
    return _pad_rows(jnp.concatenate(rows, axis=0), 16)


def _unpack_replicated(pack, like):
    out, r = {}, 0
    for n in _REPLICATED_VECS:
        k = like[n].size // pack.shape[1]
        out[n] = pack[r:r + k].reshape(like[n].shape)
        r += k
    return out


def kernel(x, p, positions, conv_w_in, conv_b_in, conv_dw, conv_dw_b, conv_ln_g, conv_ln_b, conv_w_out, kv_ln_g, kv_ln_b, w_kv, attn_w_q, attn_w_o, ln1_g, ln1_b, mlp_up, mlp_down, ln2_g, ln2_b, ple_proj, ple_gate, loss_target, m_conv_w_in, m_conv_b_in, m_conv_dw, m_conv_dw_b, m_conv_ln_g, m_conv_ln_b, m_conv_w_out, m_kv_ln_g, m_kv_ln_b, m_w_kv, m_attn_w_q, m_attn_w_o, m_ln1_g, m_ln1_b, m_mlp_up, m_mlp_down, m_ln2_g, m_ln2_b, m_ple_proj, m_ple_gate, v_conv_w_in, v_conv_b_in, v_conv_dw, v_conv_dw_b, v_conv_ln_g, v_conv_ln_b, v_conv_w_out, v_kv_ln_g, v_kv_ln_b, v_w_kv, v_attn_w_q, v_attn_w_o, v_ln1_g, v_ln1_b, v_mlp_up, v_mlp_down, v_ln2_g, v_ln2_b, v_ple_proj, v_ple_gate):
    args = dict(locals())
    w = {n: args[n] for n in _WEIGHTS}
    mom = {n: args["m_" + n] for n in _WEIGHTS}
    var = {n: args["v_" + n] for n in _WEIGHTS}
    S, D = x.shape[1:]
    n4 = D // N_CHIPS
    chip = 2 * lax.axis_index("x") + lax.axis_index("y")
    c_arr = lax.axis_index("c").astype(jnp.int32).reshape(1)

    shards = {n: w[n].astype(BF16) for n in sorted({m[1] for m in _MATS})}
    vec_all = _allgather8(_pack_sharded(w), "gather_vectors")
    vec_full = jnp.concatenate([vec_all[2 * j] for j in range(N_CHIPS)], axis=1)
    b_in = vec_all[0::2, 0:2, :].reshape(1, 2 * D)
    V = {"conv_b_a": b_in[:, :D], "conv_b_g": b_in[:, D:],
         "conv_dw": _pad_rows(vec_full[2:2 + CONV_WIDTH], CONV_PAD), "conv_dw_b": vec_full[33:34],
         "conv_ln_g": vec_full[34:35], "conv_ln_b": vec_full[35:36],
         "kv_ln_g": kv_ln_g.reshape(1, D), "kv_ln_b": kv_ln_b.reshape(1, D)}
    for l in range(2):
        for n in ("ln1_g", "ln1_b", "ln2_g", "ln2_b"):
            V[f"{n}{l}"] = w[n][l].reshape(1, D)

    half = HEAD_DIM // 2
    inv_freq = ROPE_THETA ** (-jnp.arange(half, dtype=F32) * (2.0 / HEAD_DIM))
    ang = positions[0].astype(F32)[:, None] * inv_freq
    cos, sin = jnp.cos(ang), jnp.sin(ang)
    cosf = jnp.concatenate([cos, cos], axis=-1)
    sinf = jnp.concatenate([-sin, sin], axis=-1)

    reducer = _Reducer(w, mom, var, c_arr)
    loss_cols, grad_x, _, gv = _local_step(x[0], p[:, 0], cosf, sinf, loss_target[0], None, V, shards, reducer,
                                           chip.astype(jnp.int32).reshape(1))
    loss = lax.psum(jnp.sum(loss_cols), ("x", "y", "c"))
    out = dict(reducer.out)

    gpack = jnp.concatenate([gv["conv_b_a"], gv["conv_b_g"], gv["conv_dw"][:CONV_WIDTH], gv["conv_dw_b"],
                             gv["conv_ln_g"], gv["conv_ln_b"], gv["kv_ln_g"], gv["kv_ln_b"],
                             gv["ln1_g0"], gv["ln1_g1"], gv["ln1_b0"], gv["ln1_b1"],
                             gv["ln2_g0"], gv["ln2_g1"], gv["ln2_b0"], gv["ln2_b1"]], axis=0)
    gsum = _sum8(_allgather8(_pad_rows(gpack, 48), "gather_vector_grads"), "sum_vector_grads")
    g_b = lax.dynamic_slice_in_dim(jnp.concatenate([gsum[0:1], gsum[1:2]], axis=1), chip * 2 * n4, 2 * n4, axis=1)
    g_sh = lax.dynamic_slice_in_dim(gsum[2:36], chip * n4, n4, axis=1)
    g_sh = _pad_rows(jnp.concatenate([g_b.reshape(2, n4), g_sh], axis=0), 40)
    d_sh, m_sh, v_sh = _adamw_small(g_sh, _pack_sharded(w), _pack_sharded(mom), _pack_sharded(var), "adamw_sharded_vectors")
    g_rep = _pad_rows(gsum[36:46], 16)
    d_rep, m_rep, v_rep = _adamw_small(g_rep, _pack_replicated(w), _pack_replicated(mom), _pack_replicated(var),
                                       "adamw_replicated_vectors")
    small = {}
    for i, (sh, rep) in enumerate(((g_sh, g_rep), (d_sh, d_rep), (m_sh, m_rep), (v_sh, v_rep))):
        d = {**_unpack_sharded(sh, w), **_unpack_replicated(rep, w)}
        for n, val in d.items():
            small.setdefault(n, [None] * 4)[i] = val
    for n in small:
        out[n] = small[n]

    res = [loss, grad_x[None]]
    for i in range(4):
        res += [out[n][i] for n in _WEIGHTS]
    return tuple(res)
```

```python
import functools

import jax
import jax.numpy as jnp
from jax import lax
from jax.experimental import pallas as pl
from jax.experimental.pallas import tpu as pltpu

F32 = jnp.float32
BF16 = jnp.bfloat16

HEAD_DIM = 128
ATTN_BLOCK = 128
DILATIONS = (1, 4, 16)
N_GROUPS = 3
CONV_WIDTH = 31
CONV_PAD = 32
CONV_ROWS = 32
EPILOGUE_ROWS = 128
ROPE_THETA = 10000.0
LN_EPS = 1e-5
ALPHA = 4.0 ** 0.25
ATTN_SCALE = HEAD_DIM ** -0.5
NEG = -1e30

ADAM_LR = 0.001
ADAM_B1 = 0.9
ADAM_B2 = 0.999
ADAM_EPS = 1e-08
ADAM_WD = 0.01
ADAM_STEP = 10

N_CHIPS = 4
VMEM_CAP = 60 << 20
MESH = pl.DeviceIdType.MESH
ANY = pl.BlockSpec(memory_space=pl.ANY)


def _vmem(nbytes):
    return int(min(max(2 * nbytes + (8 << 20), 24 << 20), VMEM_CAP))


def _fit(tile, n):
    if n <= tile:
        return n
    t = tile - tile % 128
    while n % t:
        t -= 128
    return t


def _nbytes(shape, dtype):
    n = 1
    for s in shape:
        n *= s
    return n * jnp.dtype(dtype).itemsize


_DIMS = {"nn": (((1,), (0,)), ((), ())), "nt": (((1,), (1,)), ((), ())), "tn": (((0,), (0,)), ((), ()))}


def _pcall(body, *, name, grid, in_specs, out_specs, out_shape, operands, scratch_shapes=(), vmem, carry=None):
    if carry is None:
        return pl.pallas_call(
            body, name=name, grid=grid, in_specs=in_specs, out_specs=out_specs, out_shape=out_shape,
            scratch_shapes=list(scratch_shapes),
            compiler_params=pltpu.CompilerParams(dimension_semantics=("arbitrary",) * len(grid), vmem_limit_bytes=vmem),
        )(*operands)
    n_in, n_out, n_scr = len(in_specs), len(out_specs), len(scratch_shapes)
    c_in, c_out = len(carry.ins), len(carry.out_shape)

    def wrapped(*refs):
        ins, refs = refs[:n_in], refs[n_in:]
        c_ins, refs = refs[:c_in], refs[c_in:]
        outs, refs = refs[:n_out], refs[n_out:]
        c_outs, refs = refs[:c_out], refs[c_out:]
        scr, c_sems = refs[:n_scr], refs[n_scr:]
        first = functools.reduce(jnp.logical_and, [pl.program_id(d) == 0 for d in range(len(grid))])
        last = functools.reduce(jnp.logical_and, [pl.program_id(d) == grid[d] - 1 for d in range(len(grid))])
        pl.when(first)(lambda: carry.start(c_ins, c_outs, c_sems))
        body(*ins, *outs, *scr)
        pl.when(last)(lambda: carry.finish(c_ins, c_outs, c_sems))

    res = pl.pallas_call(
        wrapped, name=name, grid=grid, in_specs=list(in_specs) + [ANY] * c_in, out_specs=list(out_specs) + [ANY] * c_out,
        out_shape=list(out_shape) + list(carry.out_shape), scratch_shapes=list(scratch_shapes) + list(carry.scratch),
        input_output_aliases={len(operands) + i: n_out + o for i, o in carry.aliases.items()},
        compiler_params=pltpu.CompilerParams(dimension_semantics=("arbitrary",) * len(grid), vmem_limit_bytes=vmem),
    )(*operands, *carry.ins)
    carry.set_result(res[n_out:])
    return res[:n_out]


def _mm(a, b, *, mode, outs, name, epilogue=None, extras=(), rextras=(), vecs=(), a_sel=None, b_sel=None,
        tm=None, tn=2048, tk=None, layout=None, carry=None, ep_rows=None, a_fn=None, sums=0):
    a2, b2 = a.shape[-2:], b.shape[-2:]
    if mode == "nn":
        (M, K), (K2, N) = a2, b2
    elif mode == "nt":
        (M, K), (N, K2) = a2, b2
    else:
        (K, M), (K2, N) = a2, b2
    assert K == K2, (a.shape, b.shape, mode)
    if tm is None:
        tm = 1024 if mode == "tn" else 512
    if tk is None:
        tk = 1024 if mode == "tn" else 2048
    if layout is not None:
        kind, nslots = layout
        r, c = (M // 2, N // nslots) if kind == "col" else (M // nslots, N // 2)
        tm, tn = _fit(tm, r), _fit(tn, c)
        assert r % tm == 0 and c % tn == 0
    else:
        tm, tn = _fit(tm, M), _fit(tn, N)
    tk = _fit(tk, K)
    assert M % tm == 0 and N % tn == 0 and K % tk == 0, (M, N, K, tm, tn, tk)
    nk = K // tk
    grid = (N // tn, M // tm, nk)

    def spec(arr, sel, blk, imap):
        if arr.ndim == 3:
            return pl.BlockSpec((None,) + blk, lambda j, i, k: (sel,) + imap(j, i, k))
        return pl.BlockSpec(blk, imap)

    if mode == "tn":
        a_spec = spec(a, a_sel, (tk, tm), lambda j, i, k: (k, i))
    else:
        a_spec = spec(a, a_sel, (tm, tk), lambda j, i, k: (i, k))
    if mode == "nt":
        b_spec = spec(b, b_sel, (tn, tk), lambda j, i, k: (j, k))
    else:
        b_spec = spec(b, b_sel, (tk, tn), lambda j, i, k: (k, j))
    in_specs = [a_spec, b_spec]
    in_specs += [pl.BlockSpec((tm, tn), lambda j, i, k: (i, j)) for _ in extras]
    in_specs += [pl.BlockSpec((tm, e.shape[1]), lambda j, i, k: (i, 0)) for e in rextras]
    in_specs += [pl.BlockSpec((1, tn), lambda j, i, k: (0, j)) for _ in vecs]

    if layout is None:
        out_shape = [jax.ShapeDtypeStruct((M, N), d) for d in outs] + [jax.ShapeDtypeStruct((1, N), F32)] * sums
        out_specs = [pl.BlockSpec((tm, tn), lambda j, i, k: (i, j)) for _ in outs]
        out_specs += [pl.BlockSpec((1, tn), lambda j, i, k: (0, j))] * sums
    else:
        assert len(outs) == 1
        out_shape = [jax.ShapeDtypeStruct((2, nslots, r, c), outs[0])]
        rb, cb = r // tm, c // tn
        if kind == "col":
            omap = lambda j, i, k: (i // rb, j // cb, i % rb, j % cb)
        else:
            omap = lambda j, i, k: (j // cb, i // rb, i % rb, j % cb)
        out_specs = [pl.BlockSpec((None, None, tm, tn), omap)]

    ne, nr, nv, no = len(extras), len(rextras), len(vecs), len(outs)
    dims = _DIMS[mode]

    def body(*refs):
        a_ref, b_ref = refs[0], refs[1]
        rest = refs[2:2 + ne + nr + nv]
        o_refs = refs[2 + ne + nr + nv:2 + ne + nr + nv + no]
        s_refs = refs[2 + ne + nr + nv + no:2 + ne + nr + nv + no + sums]

        def finish(total):
            if epilogue is None:
                for o in o_refs:
                    o[...] = total.astype(o.dtype)
                return
            step = min(ep_rows or tm, tm)
            col_sums = [None] * sums
            for r0 in range(0, tm, step):
                rows = slice(r0, r0 + step)
                tiles = [x[rows, :] for x in rest[:ne + nr]] + [x[...] for x in rest[ne + nr:]]
                res = epilogue(total[rows, :], *tiles)
                for o, val in zip(o_refs, res[:no]):
                    o[rows, :] = val.astype(o.dtype)
                for n_, val in enumerate(res[no:]):
                    part_sum = jnp.sum(val, axis=0, keepdims=True)
                    col_sums[n_] = part_sum if col_sums[n_] is None else col_sums[n_] + part_sum
            if sums:
                @pl.when(pl.program_id(1) == 0)
                def _():
                    for s_ref, val in zip(s_refs, col_sums):
                        s_ref[...] = val

                @pl.when(pl.program_id(1) > 0)
                def _():
                    for s_ref, val in zip(s_refs, col_sums):
                        s_ref[...] += val

        def product():
            a_tile = a_ref[...] if a_fn is None else a_fn(a_ref[...])
            return lax.dot_general(a_tile.astype(BF16), b_ref[...].astype(BF16), dims, preferred_element_type=F32)

        if nk == 1:
            finish(product())
            return
        acc = refs[-1]
        k = pl.program_id(2)

        @pl.when(k == 0)
        def _():
            acc[...] = product()

        @pl.when(k > 0)
        def _():
            acc[...] += product()

        @pl.when(k == nk - 1)
        def _():
            finish(acc[...])

    blk = (_nbytes((tm, tk), a.dtype) + _nbytes((tk, tn), b.dtype) + sum(_nbytes((tm, tn), e.dtype) for e in extras)
           + sum(_nbytes((tm, tn), d) for d in outs) + 2 * tm * tn * 4)
    res = _pcall(body, name=name, grid=grid, in_specs=in_specs, out_specs=out_specs, out_shape=out_shape,
                 operands=(a, b, *extras, *rextras, *vecs),
                 scratch_shapes=[pltpu.VMEM((tm, tn), F32)] if nk > 1 else [], vmem=_vmem(blk), carry=carry)
    return res[0] if no + sums == 1 else tuple(res)


def _rows(fn, rows, vecs, outs, sums, *, tm, name, carry=None):
    S = rows[0].shape[0]
    tm = min(tm, S)
    assert S % tm == 0
    nr, nv, no, ns = len(rows), len(vecs), len(outs), len(sums)

    def body(*refs):
        vals = fn(*[r[...] for r in refs[:nr + nv]])
        o_refs = refs[nr + nv:nr + nv + no]
        s_refs = refs[nr + nv + no:]
        for o, val in zip(o_refs, vals[:no]):
            o[...] = val.astype(o.dtype)
        if ns:
            @pl.when(pl.program_id(0) == 0)
            def _():
                for s in s_refs:
                    s[...] = jnp.zeros_like(s)

            for s, val in zip(s_refs, vals[no:]):
                s[...] += jnp.sum(val.astype(F32), axis=0, keepdims=True)

    in_specs = [pl.BlockSpec((tm, r.shape[1]), lambda i: (i, 0)) for r in rows]
    in_specs += [pl.BlockSpec(v.shape, lambda i: (0, 0)) for v in vecs]
    out_specs = [pl.BlockSpec((tm, c), lambda i: (i, 0)) for c, _ in outs]
    out_specs += [pl.BlockSpec((1, c), lambda i: (0, 0)) for c in sums]
    out_shape = [jax.ShapeDtypeStruct((S, c), d) for c, d in outs]
    out_shape += [jax.ShapeDtypeStruct((1, c), F32) for c in sums]
    blk = sum(_nbytes((tm, r.shape[1]), r.dtype) for r in rows) + sum(_nbytes((tm, c), d) for c, d in outs)
    blk += 6 * tm * max(r.shape[1] for r in rows) * 4
    res = _pcall(body, name=name, grid=(S // tm,), in_specs=in_specs, out_specs=out_specs, out_shape=out_shape,
                 operands=(*rows, *vecs), vmem=_vmem(blk), carry=carry)
    return tuple(res)


def _ln_norm(z):
    mu = jnp.mean(z, axis=-1, keepdims=True)
    d = z - mu
    var = jnp.mean(d * d, axis=-1, keepdims=True)
    rstd = lax.rsqrt(var + LN_EPS)
    return d * rstd, rstd


def _ln(z, g, b):
    return _ln_norm(z)[0] * g + b


def _ln_bwd(dy, n, rstd, g):
    dn = dy * g
    return rstd * (dn - jnp.mean(dn, axis=-1, keepdims=True) - n * jnp.mean(dn * n, axis=-1, keepdims=True))


def _sq_relu(t):
    return jnp.square(jnp.maximum(t.astype(F32), 0.0))


def _sigmoid(x):
    return 1.0 / (1.0 + jnp.exp(-x))


def _per_head(x, fn):
    h = x.shape[1] // HEAD_DIM
    return jnp.concatenate([fn(x[:, i * HEAD_DIM:(i + 1) * HEAD_DIM], i) for i in range(h)], axis=1)


def _rot(x, cosf, sinf):
    return _per_head(x, lambda xh, i: xh * cosf + pltpu.roll(xh, HEAD_DIM // 2, 1) * sinf)


def _rot_t(dy, cosf, sinf):
    return _per_head(dy, lambda dh, i: dh * cosf + pltpu.roll(dh * sinf, HEAD_DIM // 2, 1))


def _shift_copies(win):
    rows = win.shape[1] - 8
    for s in range(1, 8):
        win[s, 0:rows, :] = win[0, s:s + rows, :]


def _rows_at(win, start):
    s = start % 8
    return win[s, start - s:start - s + CONV_ROWS, :]


def _conv_fwd(glu, dw, dwb, *, tm=256, tc=512, name="conv_fwd", carry=None):
    S, D = glu.shape
    tm, tc = min(tm, S), min(tc, D)
    ni = S // tm

    def body(cur_ref, prev_ref, dw_ref, dwb_ref, o_ref, win):
        i = pl.program_id(1)
        tail = prev_ref[tm - CONV_PAD:tm, :]
        win[0, 0:CONV_PAD, :] = jnp.where(i > 0, tail, jnp.zeros_like(tail))
        win[0, CONV_PAD:CONV_PAD + tm, :] = cur_ref[...]
        _shift_copies(win)
        first = CONV_PAD - CONV_WIDTH + 1
        for r0 in range(0, tm, CONV_ROWS):
            acc = jnp.zeros((CONV_ROWS, tc), F32) + dwb_ref[...]
            for k in range(CONV_WIDTH):
                acc = acc + _rows_at(win, r0 + first + k) * dw_ref[k:k + 1, :]
            o_ref[r0:r0 + CONV_ROWS, :] = acc

    return _pcall(
        body, name=name, grid=(D // tc, ni),
        in_specs=[pl.BlockSpec((tm, tc), lambda j, i: (i, j)),
                  pl.BlockSpec((tm, tc), lambda j, i: (jnp.maximum(i - 1, 0), j)),
                  pl.BlockSpec((CONV_PAD, tc), lambda j, i: (0, j)),
                  pl.BlockSpec((1, tc), lambda j, i: (0, j))],
        out_specs=[pl.BlockSpec((tm, tc), lambda j, i: (i, j))],
        out_shape=[jax.ShapeDtypeStruct((S, D), F32)],
        scratch_shapes=[pltpu.VMEM((8, tm + CONV_PAD, tc), F32)],
        operands=(glu, glu, dw, dwb), vmem=_vmem(8 * tm * tc * 4), carry=carry)[0]


def _conv_bwd(dc, glu, a_pre, g_pre, dw, ba, bg, *, tm=256, tc=512, name="conv_bwd", carry=None):
    S, D = dc.shape
    tm, tc = min(tm, S), min(tc, D)
    ni = S // tm

    def fold8(v):
        out = v[0:8]
        for r in range(8, CONV_ROWS, 8):
            out = out + v[r:r + 8]
        return out

    def body(dc_ref, dcn_ref, glu_ref, glup_ref, a_ref, g_ref, dw_ref, ba_ref, bg_ref,
             da_ref, dg_ref, ddw_ref, dba_ref, dbg_ref, dwin, gwin, taps):
        i = pl.program_id(1)

        @pl.when(i == 0)
        def _():
            ddw_ref[...] = jnp.zeros_like(ddw_ref)
            dba_ref[...] = jnp.zeros_like(dba_ref)
            dbg_ref[...] = jnp.zeros_like(dbg_ref)

        head = dcn_ref[0:CONV_PAD, :]
        dwin[0, 0:tm, :] = dc_ref[...]
        dwin[0, tm:tm + CONV_PAD, :] = jnp.where(i < ni - 1, head, jnp.zeros_like(head))
        tail = glup_ref[tm - CONV_PAD:tm, :]
        gwin[0, 0:CONV_PAD, :] = jnp.where(i > 0, tail, jnp.zeros_like(tail))
        gwin[0, CONV_PAD:CONV_PAD + tm, :] = glu_ref[...]
        _shift_copies(dwin)
        _shift_copies(gwin)
        taps[...] = jnp.zeros_like(taps)
        first = CONV_PAD - CONV_WIDTH + 1
        sum_a = jnp.zeros((8, tc), F32)
        sum_g = jnp.zeros((8, tc), F32)
        for r0 in range(0, tm, CONV_ROWS):
            dcur = dc_ref[r0:r0 + CONV_ROWS, :]
            dglu = jnp.zeros((CONV_ROWS, tc), F32)
            for k in range(CONV_WIDTH):
                dglu = dglu + _rows_at(dwin, r0 + CONV_WIDTH - 1 - k) * dw_ref[k:k + 1, :]
                taps[k] += fold8(dcur * _rows_at(gwin, r0 + first + k))
            a = a_ref[r0:r0 + CONV_ROWS, :] + ba_ref[...]
            sg = _sigmoid(g_ref[r0:r0 + CONV_ROWS, :] + bg_ref[...])
            da = dglu * sg
            dg = dglu * a * sg * (1.0 - sg)
            da_ref[r0:r0 + CONV_ROWS, :] = da.astype(BF16)
            dg_ref[r0:r0 + CONV_ROWS, :] = dg.astype(BF16)
            sum_a = sum_a + fold8(da)
            sum_g = sum_g + fold8(dg)
        ddw_ref[...] += jnp.sum(taps[...], axis=1)
        dba_ref[...] += jnp.sum(sum_a, axis=0, keepdims=True)
        dbg_ref[...] += jnp.sum(sum_g, axis=0, keepdims=True)

    tile = lambda f: pl.BlockSpec((tm, tc), f)
    vec = pl.BlockSpec((1, tc), lambda j, i: (0, j))
    return _pcall(
        body, name=name, grid=(D // tc, ni),
        in_specs=[tile(lambda j, i: (i, j)), tile(lambda j, i: (jnp.minimum(i + 1, ni - 1), j)),
                  tile(lambda j, i: (i, j)), tile(lambda j, i: (jnp.maximum(i - 1, 0), j)),
                  tile(lambda j, i: (i, j)), tile(lambda j, i: (i, j)),
                  pl.BlockSpec((CONV_PAD, tc), lambda j, i: (0, j)), vec, vec],
        out_specs=[tile(lambda j, i: (i, j)), tile(lambda j, i: (i, j)),
                   pl.BlockSpec((CONV_PAD, tc), lambda j, i: (0, j)), vec, vec],
        out_shape=[jax.ShapeDtypeStruct((S, D), BF16), jax.ShapeDtypeStruct((S, D), BF16),
                   jax.ShapeDtypeStruct((CONV_PAD, D), F32), jax.ShapeDtypeStruct((1, D), F32),
                   jax.ShapeDtypeStruct((1, D), F32)],
        scratch_shapes=[pltpu.VMEM((8, tm + CONV_PAD, tc), F32), pltpu.VMEM((8, tm + CONV_PAD, tc), F32),
                        pltpu.VMEM((CONV_PAD, 8, tc), F32)],
        operands=(dc, dc, glu, glu, a_pre, g_pre, dw, ba, bg), vmem=_vmem(16 * tm * tc * 4), carry=carry)


def _nt(a, b):
    return lax.dot_general(a, b, _DIMS["nt"], preferred_element_type=F32)


def _tn(a, b):
    return lax.dot_general(a, b, _DIMS["tn"], preferred_element_type=F32)


HEADS_TOGETHER = 8
STAT_LANES = 128


def _per_head_pack(cols):
    rows = cols[0].shape[0]
    lane = lax.broadcasted_iota(jnp.int32, (rows, STAT_LANES), 1)
    out = jnp.zeros((rows, STAT_LANES), F32)
    for h, col in enumerate(cols):
        out = jnp.where(lane == h, col, out)
    return out


def _window_mask(qi, kj, first_key):
    B = ATTN_BLOCK
    return ((kj < B) & (kj >= qi) & (kj >= first_key)) | ((kj >= B) & (kj - B <= qi))


def _attn_fwd(q_rot, k, v, g, dil):
    S, D = k.shape
    H = D // HEAD_DIM
    L = S // dil
    nb_count = L // ATTN_BLOCK
    B = ATTN_BLOCK

    def body(q_ref, kc_ref, kp_ref, vc_ref, vp_ref, o_ref, lse_ref):
        nb = pl.program_id(1)
        qi = lax.broadcasted_iota(jnp.int32, (B, 2 * B), 0)
        kj = lax.broadcasted_iota(jnp.int32, (B, 2 * B), 1)
        valid = _window_mask(qi, kj, jnp.where(nb > 0, 0, B))
        stats = []
        for h0 in range(0, H, HEADS_TOGETHER):
            heads = range(h0, min(h0 + HEADS_TOGETHER, H))
            hs = [slice(h * HEAD_DIM, (h + 1) * HEAD_DIM) for h in heads]
            kk = [jnp.concatenate([kp_ref[:, c], kc_ref[:, c]], axis=0) for c in hs]
            vv = [jnp.concatenate([vp_ref[:, c], vc_ref[:, c]], axis=0) for c in hs]
            s = [jnp.where(valid, _nt(q_ref[:, c], kk_) * ATTN_SCALE, NEG) for c, kk_ in zip(hs, kk)]
            m = [jnp.max(s_, axis=1, keepdims=True) for s_ in s]
            p = [jnp.exp(s_ - m_) for s_, m_ in zip(s, m)]
            l = [jnp.sum(p_, axis=1, keepdims=True) for p_ in p]
            o = [jnp.dot(p_.astype(BF16), vv_, preferred_element_type=F32) / l_ for p_, vv_, l_ in zip(p, vv, l)]
            for c, o_ in zip(hs, o):
                o_ref[:, c] = o_.astype(o_ref.dtype)
            stats += [m_ + jnp.log(l_) for m_, l_ in zip(m, l)]
        lse_ref[...] = _per_head_pack(stats)

    blk = lambda f: pl.BlockSpec((B, D), f)
    cur = lambda r, nb: (nb, r)
    prev = lambda r, nb: (jnp.maximum(nb - 1, 0), r)
    o, lse = pl.pallas_call(
        body, name=f"attn_fwd_g{g}", grid=(dil, nb_count),
        in_specs=[blk(lambda r, nb: (nb, r * N_GROUPS + g)), blk(cur), blk(prev), blk(cur), blk(prev)],
        out_specs=[blk(cur), pl.BlockSpec((B, STAT_LANES), cur)],
        out_shape=[jax.ShapeDtypeStruct((L, dil * D), BF16), jax.ShapeDtypeStruct((L, dil * STAT_LANES), F32)],
        compiler_params=pltpu.CompilerParams(dimension_semantics=("parallel", "arbitrary"),
                                             vmem_limit_bytes=_vmem(12 * B * D * 4)),
    )(q_rot.reshape(L, dil * N_GROUPS * D), k.reshape(L, dil * D), k.reshape(L, dil * D),
      v.reshape(L, dil * D), v.reshape(L, dil * D))
    return o.reshape(S, D), lse.reshape(S, STAT_LANES)


def _attn_bwd(q_rot, k, v, do, lse, dlt, g, dil, *, name, carry=None):
    S, D = k.shape
    H = D // HEAD_DIM
    L = S // dil
    nb_count = L // ATTN_BLOCK
    B = ATTN_BLOCK

    def body(q_ref, kc_ref, kp_ref, vc_ref, vp_ref, do_ref, l_ref, d_ref, dq_ref, dk_ref, dv_ref, keep_k, keep_v):
        s_id = pl.program_id(1)

        @pl.when(s_id == 0)
        def _():
            keep_k[...] = jnp.zeros_like(keep_k)
            keep_v[...] = jnp.zeros_like(keep_v)

        @pl.when(s_id < nb_count)
        def _():
            qi = lax.broadcasted_iota(jnp.int32, (B, 2 * B), 0)
            kj = lax.broadcasted_iota(jnp.int32, (B, 2 * B), 1)
            valid = _window_mask(qi, kj, jnp.where(s_id > 0, 0, B))
            for h0 in range(0, H, HEADS_TOGETHER):
                heads = list(range(h0, min(h0 + HEADS_TOGETHER, H)))
                hs = [slice(h * HEAD_DIM, (h + 1) * HEAD_DIM) for h in heads]
                q = [q_ref[:, c] for c in hs]
                dout = [do_ref[:, c] for c in hs]
                kk = [jnp.concatenate([kp_ref[:, c], kc_ref[:, c]], axis=0) for c in hs]
                vv = [jnp.concatenate([vp_ref[:, c], vc_ref[:, c]], axis=0) for c in hs]
                s = [jnp.where(valid, _nt(q_, kk_) * ATTN_SCALE, NEG) for q_, kk_ in zip(q, kk)]
                dp = [_nt(do_, vv_) for do_, vv_ in zip(dout, vv)]
                p = [jnp.exp(s_ - l_ref[:, h:h + 1]) for s_, h in zip(s, heads)]
                ds = [(p_ * (dp_ - d_ref[:, h:h + 1])).astype(BF16) for p_, dp_, h in zip(p, dp, heads)]
                dq = [jnp.dot(ds_, kk_, preferred_element_type=F32) * ATTN_SCALE for ds_, kk_ in zip(ds, kk)]
                dkk = [_tn(ds_, q_) * ATTN_SCALE for ds_, q_ in zip(ds, q)]
                dvv = [_tn(p_.astype(BF16), do_) for p_, do_ in zip(p, dout)]
                for c, dq_, dkk_, dvv_ in zip(hs, dq, dkk, dvv):
                    dq_ref[:, c] = dq_.astype(dq_ref.dtype)
                    dk_ref[:, c] = (keep_k[:, c] + dkk_[0:B]).astype(dk_ref.dtype)
                    dv_ref[:, c] = (keep_v[:, c] + dvv_[0:B]).astype(dv_ref.dtype)
                    keep_k[:, c] = dkk_[B:2 * B]
                    keep_v[:, c] = dvv_[B:2 * B]

        @pl.when(s_id == nb_count)
        def _():
            dk_ref[...] = keep_k[...].astype(dk_ref.dtype)
            dv_ref[...] = keep_v[...].astype(dv_ref.dtype)

    last = nb_count - 1
    blk = lambda f: pl.BlockSpec((B, D), f)
    cur = lambda r, s: (jnp.minimum(s, last), r)
    prev = lambda r, s: (jnp.maximum(jnp.minimum(s, last) - 1, 0), r)
    lag = lambda r, s: (jnp.maximum(s - 1, 0), r)
    qcur = lambda r, s: (jnp.minimum(s, last), r * N_GROUPS + g)
    qv = q_rot.reshape(L, dil * N_GROUPS * D)
    view = lambda t: t.reshape(L, dil * D)
    sview = lambda t: t.reshape(L, dil * STAT_LANES)
    stat = lambda f: pl.BlockSpec((B, STAT_LANES), f)
    dq, dk, dv = _pcall(
        body, name=name, grid=(dil, nb_count + 1),
        in_specs=[blk(qcur), blk(cur), blk(prev), blk(cur), blk(prev), blk(cur), stat(cur), stat(cur)],
        out_specs=[blk(cur), blk(lag), blk(lag)],
        out_shape=[jax.ShapeDtypeStruct((L, dil * D), BF16)] * 3,
        scratch_shapes=[pltpu.VMEM((B, D), F32), pltpu.VMEM((B, D), F32)],
        operands=(qv, view(k), view(k), view(v), view(v), view(do), sview(lse), sview(dlt)),
        vmem=_vmem(24 * B * D * 4), carry=carry)
    return dq.reshape(S, D), dk.reshape(S, D), dv.reshape(S, D)


def _mlp_ple_fwd(z1, h1b, p_l, W, vec, l, run, kv_vec=None):
    D = z1.shape[1]
    g1, b1, g2, b2 = vec

    t = run(_mm, h1b, W[f"mlp_up{l}"], mode="nn", outs=[BF16], tm=1024, name=f"mlp_up{l}")

    def z2_ep(acc, z1_t, g1_, b1_, g2_, b2_):
        z2 = ALPHA * _ln(z1_t, g1_, b1_) + acc
        return z2, _ln(z2, g2_, b2_)

    z2, h2b = run(_mm, t, W[f"mlp_down{l}"], mode="nn", outs=[F32, BF16], extras=[z1], vecs=[g1, b1, g2, b2], a_fn=_sq_relu,
                  epilogue=z2_ep, ep_rows=EPILOGUE_ROWS, name=f"mlp_down{l}")
    act = None
    pp = run(_mm, p_l, W[f"ple_proj{l}"], mode="nn", outs=[F32], name=f"ple_proj{l}")
    if kv_vec is None:
        gpre = run(_mm, h2b, W[f"ple_gate{l}"], mode="nn", outs=[F32], name=f"ple_gate{l}")
        return t, act, z2, h2b, pp, gpre

    def x1_ep(acc, z2_t, pp_t, g2_, b2_, kg, kb):
        x1, _ = _ple_out(z2_t, pp_t, acc, g2_, b2_)
        return acc, x1, _ln(x1, kg, kb)

    gpre, x1, kvn = run(_mm, h2b, W[f"ple_gate{l}"], mode="nn", outs=[F32, F32, BF16], extras=[z2, pp],
                        vecs=[g2, b2, *kv_vec], epilogue=x1_ep, ep_rows=EPILOGUE_ROWS, tm=256, name=f"ple_gate{l}")
    return t, act, z2, h2b, pp, gpre, x1, kvn


def _mlp_ple_bwd(dy, d_pp, d_gpre, p_l, z1, h1b, t, act, z2, h2b, wts, vec, l, run, produce):
    D = z1.shape[1]
    up, down, pp_w, pg_w = wts
    g1, b1, g2, b2 = vec
    produce(f"ple_proj{l}", run(_mm, p_l, d_pp, mode="tn", outs=[BF16], layout=("col", N_CHIPS), name=f"d_ple_proj{l}"))
    produce(f"ple_gate{l}", run(_mm, h2b, d_gpre, mode="tn", outs=[BF16], layout=("row", N_CHIPS), name=f"d_ple_gate{l}"))
    def ln2_bwd(acc, dy_t, z2_t, g2_):
        dh2 = acc + dy_t
        n, rstd = _ln_norm(z2_t)
        dz2 = _ln_bwd(dh2, n, rstd, g2_)
        return dz2, dz2, dh2 * n, dh2

    dz2, dz2b, dg2, db2 = run(_mm, d_gpre, pg_w, mode="nt", outs=[F32, BF16], sums=2, extras=[dy, z2], vecs=[g2],
                              epilogue=ln2_bwd, ep_rows=EPILOGUE_ROWS, tm=256, name=f"dh2_{l}")
    produce(f"mlp_down{l}", run(_mm, t, dz2b, mode="tn", outs=[BF16], layout=("row", N_CHIPS), a_fn=_sq_relu,
                                name=f"d_mlp_down{l}"))
    dt = run(_mm, dz2b, down, mode="nt", outs=[BF16], extras=[t],
             epilogue=lambda acc, t_: (acc * 2.0 * jnp.maximum(t_.astype(F32), 0.0),), name=f"dt{l}")
    produce(f"mlp_up{l}", run(_mm, h1b, dt, mode="tn", outs=[BF16], layout=("col", N_CHIPS), name=f"d_mlp_up{l}"))
    dh1 = run(_mm, dt, up, mode="nt", outs=[F32], extras=[dz2], epilogue=lambda acc, e: (acc + ALPHA * e,),
              name=f"dh1_{l}")

    def ln1_bwd(dh1_t, z1_t, g1_):
        n, rstd = _ln_norm(z1_t)
        dz1 = _ln_bwd(dh1_t, n, rstd, g1_)
        return dz1, dz1, dh1_t * n, dh1_t

    dz1, dz1b, dg1, db1 = _rows(ln1_bwd, [dh1, z1], [g1], [(D, F32), (D, BF16)], [D, D], tm=256,
                                name=f"ln1_bwd{l}")
    return dz1, dz1b, (dg1, db1, dg2, db2)


def _ple_out(z2, pp, gpre, g2, b2):
    gt = _sigmoid(gpre)
    return _ln(z2, g2, b2) + pp * gt, gt


_GATHER_AT = {
    "conv_in_a": ("conv_w_out",),
    "conv_in_g": ("ple_gate0", "ple_proj0"),
    "conv_fwd": ("mlp_up0",),
    "mlp_up0": ("mlp_down0",),
    "mlp_down0": ("attn_w_q", "w_kv"),
    "kv_k": ("attn_w_o",),
    "kv_v": ("ple_proj1", "ple_gate1"),
    "attn_q": ("mlp_up1",),
    "mlp_up1": ("mlp_down1",),
}
_GATHER_FIRST = ("conv_w_in",)


def _local_step(x, p, cosf, sinf, target, W, V, shards=None, reducer=None, chip_arr=None):
    S, D = x.shape
    gw, gv = {}, {}
    if shards is not None:
        W = dict(_Gather(_GATHER_FIRST, shards, chip_arr).run_alone("gather_first"))

    def run(fn, *args, name, **kw):
        gather = _Gather(_GATHER_AT[name], shards, chip_arr) if (shards is not None and name in _GATHER_AT) else None
        carry = gather if reducer is None or gather is not None else reducer.carry(name)
        out = fn(*args, name=name, carry=carry, **kw)
        if gather is not None:
            W.update(gather.result)
        elif reducer is not None:
            reducer.carried()
        return out

    a_pre = run(_mm, x, W["conv_w_in"], b_sel=0, mode="nn", outs=[F32], name="conv_in_a")
    g_pre = run(_mm, x, W["conv_w_in"], b_sel=1, mode="nn", outs=[F32], name="conv_in_g")
    (glu,) = _rows(lambda a, g, ba, bg: ((a + ba) * _sigmoid(g + bg),), [a_pre, g_pre], [V["conv_b_a"], V["conv_b_g"]],
                   [(D, F32)], [], tm=256, name="glu_fwd")
    cv = run(_conv_fwd, glu, V["conv_dw"], V["conv_dw_b"], name="conv_fwd")

    def silu_ln(c, g_, b_):
        y = _ln(c, g_, b_)
        return (y * _sigmoid(y),)

    (sb,) = _rows(silu_ln, [cv], [V["conv_ln_g"], V["conv_ln_b"]], [(D, BF16)], [], tm=256, name="conv_ln_fwd")
    def z1_ep(acc, x_t, g_, b_):
        z1 = ALPHA * x_t + acc
        return z1, _ln(z1, g_, b_)

    vec0 = (V["ln1_g0"], V["ln1_b0"], V["ln2_g0"], V["ln2_b0"])
    vec1 = (V["ln1_g1"], V["ln1_b1"], V["ln2_g1"], V["ln2_b1"])
    z1_0, h1b_0 = _mm(sb, W["conv_w_out"], mode="nn", outs=[F32, BF16], extras=[x], vecs=[vec0[0], vec0[1]],
                      epilogue=z1_ep, ep_rows=EPILOGUE_ROWS, name="conv_out")
    t0, act0, z2_0, h2b_0, pp0, gpre0, x1, kvn = _mlp_ple_fwd(z1_0, h1b_0, p[0], W, vec0, 0, run,
                                                              (V["kv_ln_g"], V["kv_ln_b"]))

    rot_ep = lambda acc, c_, s_: (_rot(acc, c_, s_),)
    k_rot = run(_mm, kvn, W["w_kv"], b_sel=0, mode="nn", outs=[BF16], rextras=[cosf, sinf], epilogue=rot_ep, name="kv_k")
    v_b = run(_mm, kvn, W["w_kv"], b_sel=1, mode="nn", outs=[BF16], name="kv_v")
    q_rot = run(_mm, x1, W["attn_w_q"], mode="nn", outs=[BF16], rextras=[cosf, sinf], epilogue=rot_ep, name="attn_q")
    og, lg = [], []
    for g, dil in enumerate(DILATIONS):
        o_g, l_g = _attn_fwd(q_rot, k_rot, v_b, g, dil)
        og.append(o_g)
        lg.append(l_g)

    def merge(o0, o1, o2, l0, l1, l2):
        m = jnp.maximum(jnp.maximum(l0, l1), l2)
        e = [jnp.exp(l0 - m), jnp.exp(l1 - m), jnp.exp(l2 - m)]
        den = e[0] + e[1] + e[2]
        w = [e_g / den for e_g in e]
        o = _per_head(o0, lambda oh, h: sum(w[g][:, h:h + 1] * (o0, o1, o2)[g][:, h * HEAD_DIM:(h + 1) * HEAD_DIM].astype(F32)
                                            for g in range(N_GROUPS)))
        return o, m + jnp.log(den)

    ob, lse = _rows(merge, og + lg, [], [(D, BF16), (STAT_LANES, F32)], [], tm=256, name="attn_merge")
    z1_1, h1b_1 = _mm(ob, W["attn_w_o"], mode="nn", outs=[F32, BF16], extras=[x1], vecs=[vec1[0], vec1[1]],
                      epilogue=z1_ep, ep_rows=EPILOGUE_ROWS, name="attn_out")
    t1, act1, z2_1, h2b_1, pp1, gpre1 = _mlp_ple_fwd(z1_1, h1b_1, p[1], W, vec1, 1, run)
    wts0 = (W["mlp_up0"], W["mlp_down0"], W["ple_proj0"], W["ple_gate0"])
    wts1 = (W["mlp_up1"], W["mlp_down1"], W["ple_proj1"], W["ple_gate1"])

    def head(z2, pp, gpre, tgt, g2, b2):
        y, gt = _ple_out(z2, pp, gpre, g2, b2)
        err = y - tgt
        dy = err * (1.0 / D)
        return dy, dy * gt, dy * pp * gt * (1.0 - gt), 0.5 * err * err * (1.0 / D)

    dy1, d_pp1, d_gpre1, loss_cols = _rows(head, [z2_1, pp1, gpre1, target], [vec1[2], vec1[3]],
                                           [(D, F32), (D, BF16), (D, BF16)], [D], tm=256, name="loss_head")

    def produce(name, grad):
        gw[name] = grad
        if reducer is not None:
            reducer.produced(name, grad)

    dz1_1, dz1b_1, (gv["ln1_g1"], gv["ln1_b1"], gv["ln2_g1"], gv["ln2_b1"]) = _mlp_ple_bwd(
        dy1, d_pp1, d_gpre1, p[1], z1_1, h1b_1, t1, act1, z2_1, h2b_1, wts1, vec1, 1, run, produce)
    produce("attn_w_o", run(_mm, ob, dz1b_1, mode="tn", outs=[BF16], layout=("row", N_CHIPS), name="d_attn_w_o"))

    do_b = run(_mm, dz1b_1, W["attn_w_o"], mode="nt", outs=[BF16], name="attn_do")

    def delta(do_t, o_t):
        prod = do_t.astype(F32) * o_t.astype(F32)
        H = D // HEAD_DIM
        return (_per_head_pack([jnp.sum(prod[:, h * HEAD_DIM:(h + 1) * HEAD_DIM], axis=1, keepdims=True) for h in range(H)]),)

    (dlt,) = _rows(delta, [do_b, ob], [], [(STAT_LANES, F32)], [], tm=256, name="attn_delta")
    dqs, dks, dvs = [], [], []
    for g, dil in enumerate(DILATIONS):
        dq_g, dk_g, dv_g = run(_attn_bwd, q_rot, k_rot, v_b, do_b, lse, dlt, g, dil, name=f"attn_bwd_g{g}")
        dqs.append(dq_g)
        dks.append(dk_g)
        dvs.append(dv_g)

    def unrot(q0, q1, q2, k0, k1, k2, v0, v1, v2, c_, s_):
        dq = jnp.concatenate([_rot_t(t_.astype(F32), c_, s_) for t_ in (q0, q1, q2)], axis=1)
        f = lambda t_: t_.astype(F32)
        return dq, _rot_t(f(k0) + f(k1) + f(k2), c_, s_), f(v0) + f(v1) + f(v2)

    dq, dk, dv = run(_rows, unrot, dqs + dks + dvs + [cosf, sinf], [], [(N_GROUPS * D, BF16), (D, BF16), (D, BF16)], [],
                     tm=128, name="attn_unrot")
    produce("attn_w_q", run(_mm, x1, dq, mode="tn", outs=[BF16], layout=("col", N_CHIPS), name="d_attn_w_q"))
    dx1_q = run(_mm, dq, W["attn_w_q"], mode="nt", outs=[F32], extras=[dz1_1],
                epilogue=lambda acc, e: (acc + ALPHA * e,), name="dx1_q")
    produce("w_kv", jnp.concatenate(
        [run(_mm, kvn, dk, mode="tn", outs=[BF16], layout=("col", 2), name="d_w_kv_k"),
         run(_mm, kvn, dv, mode="tn", outs=[BF16], layout=("col", 2), name="d_w_kv_v")], axis=1))
    dkvn_k = run(_mm, dk, W["w_kv"], b_sel=0, mode="nt", outs=[F32], name="dkvn_k")
    dkvn = run(_mm, dv, W["w_kv"], b_sel=1, mode="nt", outs=[F32], extras=[dkvn_k], epilogue=lambda acc, e: (acc + e,),
               name="dkvn_v")

    def x1_bwd(dx1q_t, dkvn_t, x1_t, pp, gpre, kg):
        n, rstd = _ln_norm(x1_t)
        dy = dx1q_t + _ln_bwd(dkvn_t, n, rstd, kg)
        gt = _sigmoid(gpre)
        return dy, dy * gt, dy * pp * gt * (1.0 - gt), dkvn_t * n, dkvn_t

    dy0, d_pp0, d_gpre0, gv["kv_ln_g"], gv["kv_ln_b"] = _rows(
        x1_bwd, [dx1_q, dkvn, x1, pp0, gpre0], [V["kv_ln_g"]], [(D, F32), (D, BF16), (D, BF16)], [D, D], tm=256,
        name="x1_bwd")

    dz1_0, dz1b_0, (gv["ln1_g0"], gv["ln1_b0"], gv["ln2_g0"], gv["ln2_b0"]) = _mlp_ple_bwd(
        dy0, d_pp0, d_gpre0, p[0], z1_0, h1b_0, t0, act0, z2_0, h2b_0, wts0, vec0, 0, run, produce)
    produce("conv_w_out", run(_mm, sb, dz1b_0, mode="tn", outs=[BF16], layout=("row", N_CHIPS), name="d_conv_w_out"))
    ds = run(_mm, dz1b_0, W["conv_w_out"], mode="nt", outs=[F32], name="conv_ds")

    def conv_ln_bwd(ds_t, c_t, g_, b_):
        n, rstd = _ln_norm(c_t)
        y = n * g_ + b_
        sg = _sigmoid(y)
        dln = ds_t * sg * (1.0 + y * (1.0 - sg))
        dc = _ln_bwd(dln, n, rstd, g_)
        return dc, dln * n, dln, dc

    dc, gv["conv_ln_g"], gv["conv_ln_b"], gv["conv_dw_b"] = _rows(
        conv_ln_bwd, [ds, cv], [V["conv_ln_g"], V["conv_ln_b"]], [(D, F32)], [D, D, D], tm=256, name="conv_ln_bwd")
    da, dg, gv["conv_dw"], gv["conv_b_a"], gv["conv_b_g"] = run(
        _conv_bwd, dc, glu, a_pre, g_pre, V["conv_dw"], V["conv_b_a"], V["conv_b_g"], name="conv_bwd")
    produce("conv_w_in", jnp.concatenate(
        [run(_mm, x, da, mode="tn", outs=[BF16], layout=("col", 2), name="d_conv_w_in_a"),
         run(_mm, x, dg, mode="tn", outs=[BF16], layout=("col", 2), name="d_conv_w_in_g")], axis=1))
    dx_a = run(_mm, da, W["conv_w_in"], b_sel=0, mode="nt", outs=[F32], extras=[dz1_0],
               epilogue=lambda acc, e: (acc + ALPHA * e,), name="dx_a")
    grad_x = run(_mm, dg, W["conv_w_in"], b_sel=1, mode="nt", outs=[F32], extras=[dx_a],
                 epilogue=lambda acc, e: (acc + e,), name="dx_g")
    if reducer is not None:
        reducer.carry("share_last").run_alone("share_last")
        reducer.carried()
    return loss_cols, grad_x, gw, gv


def _place():
    x, y, c = lax.axis_index("x"), lax.axis_index("y"), lax.axis_index("c")
    chips = [(1 - x, y), (x, 1 - y), (1 - x, 1 - y)]
    return x, y, c, chips


def _remote(src, dst, ssem, rsem, dev):
    return pltpu.make_async_remote_copy(src_ref=src, dst_ref=dst, send_sem=ssem, recv_sem=rsem, device_id=dev,
                                        device_id_type=MESH)


def _allgather8(block, name):
    R, C = block.shape

    def body(x_ref, out_ref, send_sems, recv_sems, local_sem):
        x, y, c, chips = _place()
        me, sibling = (x, y, c), (x, y, 1 - c)

        def slot(px, py, pc):
            return out_ref.at[4 * px + 2 * py + pc]

        def copy(k, blockpos, to, src=None):
            return _remote(slot(*blockpos) if src is None else src, slot(*blockpos), send_sems.at[k], recv_sems.at[k], to)

        mine = pltpu.make_async_copy(x_ref, slot(*me), local_sem)
        mine.start()
        first = [copy(0, me, sibling, src=x_ref)]
        first += [copy(1 + j, me, (*chip, c), src=x_ref) for j, chip in enumerate(chips)]
        for cp in first:
            cp.start()
        passed = [copy(4 + j, (*chip, c), sibling) for j, chip in enumerate(chips)]
        for j, chip in enumerate(chips):
            copy(1 + j, (*chip, c), me).wait_recv()
            passed[j].start()
        copy(0, sibling, me).wait_recv()
        for j, chip in enumerate(chips):
            copy(4 + j, (*chip, 1 - c), me).wait_recv()
        for cp in first + passed:
            cp.wait_send()
        mine.wait()

    return pl.pallas_call(
        body, name=name, out_shape=jax.ShapeDtypeStruct((8, R, C), block.dtype),
        in_specs=[pl.BlockSpec(memory_space=pltpu.VMEM)], out_specs=pl.BlockSpec(memory_space=pltpu.VMEM),
        scratch_shapes=[pltpu.SemaphoreType.DMA((7,)), pltpu.SemaphoreType.DMA((7,)), pltpu.SemaphoreType.DMA],
        compiler_params=pltpu.CompilerParams(vmem_limit_bytes=_vmem(10 * _nbytes((R, C), block.dtype))),
    )(block)


_MATS = (
    ("conv_w_in", "conv_w_in", 0, "col", True),
    ("conv_w_out", "conv_w_out", 0, "row", False),
    ("mlp_up0", "mlp_up", 0, "col", False),
    ("mlp_down0", "mlp_down", 0, "row", False),
    ("ple_proj0", "ple_proj", 0, "col", False),
    ("ple_gate0", "ple_gate", 0, "row", False),
    ("w_kv", "w_kv", None, "col", True),
    ("attn_w_q", "attn_w_q", 0, "col", False),
    ("attn_w_o", "attn_w_o", 0, "row", False),
    ("mlp_up1", "mlp_up", 1, "col", False),
    ("mlp_down1", "mlp_down", 1, "row", False),
    ("ple_proj1", "ple_proj", 1, "col", False),
    ("ple_gate1", "ple_gate", 1, "row", False),
)


class _Carry:
    result = None
    aliases = {}

    def set_result(self, outs):
        self.result = dict(zip(self.names, outs))

    def run_alone(self, name):
        n_in, n_out = len(self.ins), len(self.out_shape)

        def body(*refs):
            in_refs, out_refs, sems = refs[:n_in], refs[n_in:n_in + n_out], refs[n_in + n_out:]
            self.start(in_refs, out_refs, sems)
            self.finish(in_refs, out_refs, sems)

        outs = pl.pallas_call(body, name=name, out_shape=self.out_shape, in_specs=[ANY] * n_in, out_specs=[ANY] * n_out,
                              scratch_shapes=self.scratch, input_output_aliases=dict(self.aliases))(*self.ins)
        self.set_result(outs)
        return self.result


class _Gather(_Carry):
    def __init__(self, names, shards, chip_arr):
        mats = [m for m in _MATS if m[0] in names]
        srcs = sorted({m[1] for m in mats})
        self.names = [m[0] for m in mats]
        self.out_shape, self.geo, placed = [], [], []
        for name, src, layer, kind, split in mats:
            s = shards[src]
            ks, ns = s.shape[-2:]
            K, N = (ks, ns * N_CHIPS) if kind == "col" else (ks * N_CHIPS, ns)
            self.out_shape.append(jax.ShapeDtypeStruct((2, K, N // 2) if split else (K, N), BF16))
            self.geo.append((srcs.index(src), layer if s.ndim == 3 else None, kind, split, K, N))
            placed.append(_place_shard(s, layer if s.ndim == 3 else None, kind, split, chip_arr, f"place_{name}"))
        T = len(mats)
        self.ins = [shards[n] for n in srcs] + placed
        self.aliases = {len(srcs) + t: t for t in range(T)}
        self.scratch = [pltpu.SemaphoreType.DMA((3 * T,)) for _ in range(4)]
        self.result = None

    def _copies(self, in_refs, out_refs, sems):
        geo, T = self.geo, len(self.geo)
        s_ici, r_ici, s_d2d, r_d2d = sems
        x, y, c, chips = _place()
        me = 2 * x + y
        sibling = (x, y, 1 - c)
        idx = [2 * cx + cy for cx, cy in chips]

        def src_ref(t):
            i, layer, _, _, _, _ = geo[t]
            return in_refs[i] if layer is None else in_refs[i].at[layer]

        def src_half(t, h):
            _, _, kind, _, K, N = geo[t]
            if kind == "col":
                return src_ref(t).at[pl.ds(h * (K // 2), K // 2), :]
            return src_ref(t).at[:, pl.ds(h * (N // 2), N // 2)]

        def dst(t, j, h):
            _, _, kind, split, K, N = geo[t]
            n, k = N // N_CHIPS, K // N_CHIPS
            if kind == "col":
                rows = slice(None) if h is None else pl.ds(h * (K // 2), K // 2)
                if split:
                    return out_refs[t].at[j // 2, rows, pl.ds((j % 2) * n, n)]
                return out_refs[t].at[rows, pl.ds(j * n, n)]
            cols = slice(None) if h is None else pl.ds(h * (N // 2), N // 2)
            return out_refs[t].at[pl.ds(j * k, k), cols]

        sends = [_remote(src_half(t, c), dst(t, me, c), s_ici.at[3 * t + kk], r_ici.at[3 * t + kk], (*chips[kk], c))
                 for t in range(T) for kk in range(3)]
        hops = []
        for t in range(T):
            for kk in range(3):
                mine, theirs = dst(t, idx[kk], c), dst(t, idx[kk], 1 - c)
                hops.append((_remote(mine, mine, s_ici.at[3 * t + kk], r_ici.at[3 * t + kk], sibling),
                             _remote(mine, mine, s_d2d.at[3 * t + kk], r_d2d.at[3 * t + kk], sibling),
                             _remote(theirs, theirs, s_d2d.at[3 * t + kk], r_d2d.at[3 * t + kk], sibling)))
        return sends, hops

    def start(self, in_refs, out_refs, sems):
        for cp in self._copies(in_refs, out_refs, sems)[0]:
            cp.start()

    def finish(self, in_refs, out_refs, sems):
        sends, hops = self._copies(in_refs, out_refs, sems)
        for landed, forward, _ in hops:
            landed.wait_recv()
            forward.start()
        for _, _, from_sibling in hops:
            from_sibling.wait_recv()
        for cp in sends + [h[1] for h in hops]:
            cp.wait_send()


def _place_shard(shard, layer, kind, split, chip_arr, name):
    ks, ns = shard.shape[-2:]
    K, N = (ks, ns * N_CHIPS) if kind == "col" else (ks * N_CHIPS, ns)
    tr = _fit(256, ks)
    nb = ks // tr
    if shard.ndim == 3:
        in_spec = pl.BlockSpec((None, tr, ns), lambda i, me: (layer, i, 0))
    else:
        in_spec = pl.BlockSpec((tr, ns), lambda i, me: (i, 0))
    if kind == "row":
        out_shape, out_spec = (K, N), pl.BlockSpec((tr, ns), lambda i, me: (me[0] * nb + i, 0))
    elif split:
        out_shape, out_spec = (2, K, N // 2), pl.BlockSpec((None, tr, ns), lambda i, me: (me[0] // 2, i, me[0] % 2))
    else:
        out_shape, out_spec = (K, N), pl.BlockSpec((tr, ns), lambda i, me: (i, me[0]))

    def body(me_ref, s_ref, o_ref):
        o_ref[...] = s_ref[...]

    return pl.pallas_call(
        body, name=name, out_shape=jax.ShapeDtypeStruct(out_shape, BF16),
        grid_spec=pltpu.PrefetchScalarGridSpec(num_scalar_prefetch=1, grid=(nb,), in_specs=[in_spec], out_specs=out_spec),
        compiler_params=pltpu.CompilerParams(dimension_semantics=("parallel",), vmem_limit_bytes=_vmem(4 * tr * ns * 2)),
    )(chip_arr, shard)


class _Multi(_Carry):
    def __init__(self, parts):
        self.parts = parts
        self.ins = [a for p in parts for a in p.ins]
        self.out_shape = [a for p in parts for a in p.out_shape]
        self.scratch = [a for p in parts for a in p.scratch]
        self.aliases, n_in, n_out = {}, 0, 0
        for p in parts:
            self.aliases.update({n_in + i: n_out + o for i, o in p.aliases.items()})
            n_in, n_out = n_in + len(p.ins), n_out + len(p.out_shape)

    def _split(self, seq, field):
        out, at = [], 0
        for p in self.parts:
            n = len(getattr(p, field))
            out.append(seq[at:at + n])
            at += n
        return out

    def _each(self, method, in_refs, out_refs, sems):
        for p, i, o, s in zip(self.parts, self._split(in_refs, "ins"), self._split(out_refs, "out_shape"),
                              self._split(sems, "scratch")):
            getattr(p, method)(i, o, s)

    def start(self, in_refs, out_refs, sems):
        self._each("start", in_refs, out_refs, sems)

    def finish(self, in_refs, out_refs, sems):
        self._each("finish", in_refs, out_refs, sems)

    def set_result(self, outs):
        for p, o in zip(self.parts, self._split(list(outs), "out_shape")):
            p.set_result(o)


class _PairSend(_Carry):
    def __init__(self, grads):
        self.names = list(grads)
        self.ins = [grads[n] for n in self.names]
        self.out_shape = [jax.ShapeDtypeStruct(a.shape[1:], BF16) for a in self.ins]
        T = len(self.names)
        self.scratch = [pltpu.SemaphoreType.DMA((T,)), pltpu.SemaphoreType.DMA((T,))]

    def _copies(self, in_refs, out_refs, sems):
        x, y, c, _ = _place()
        return [_remote(in_refs[t].at[1 - c], out_refs[t], sems[0].at[t], sems[1].at[t], (x, y, 1 - c))
                for t in range(len(self.names))]

    def start(self, in_refs, out_refs, sems):
        for cp in self._copies(in_refs, out_refs, sems):
            cp.start()

    def finish(self, in_refs, out_refs, sems):
        for cp in self._copies(in_refs, out_refs, sems):
            cp.wait()


class _ChipScatter(_Carry):
    def __init__(self, sums):
        self.names = list(sums)
        T = len(self.names)
        self.ins = [sums[n][0] for n in self.names] + [sums[n][1] for n in self.names]
        self.out_shape = [jax.ShapeDtypeStruct(a.shape, BF16) for a in self.ins[:T]]
        self.aliases = {T + t: t for t in range(T)}
        self.scratch = [pltpu.SemaphoreType.DMA((3 * T,)), pltpu.SemaphoreType.DMA((3 * T,))]

    def _copies(self, in_refs, out_refs, sems):
        ssem, rsem = sems
        x, y, c, chips = _place()
        me = 2 * x + y
        idx = [2 * cx + cy for cx, cy in chips]
        T = len(self.names)
        sends = [_remote(in_refs[t].at[idx[kk]], out_refs[t].at[me], ssem.at[3 * t + kk], rsem.at[3 * t + kk],
                         (*chips[kk], c)) for t in range(T) for kk in range(3)]
        lands = [_remote(out_refs[t].at[idx[kk]], out_refs[t].at[idx[kk]], ssem.at[3 * t + kk], rsem.at[3 * t + kk],
                         (*chips[kk], c)) for t in range(T) for kk in range(3)]
        return sends, lands

    def start(self, in_refs, out_refs, sems):
        for cp in self._copies(in_refs, out_refs, sems)[0]:
            cp.start()

    def finish(self, in_refs, out_refs, sems):
        sends, lands = self._copies(in_refs, out_refs, sems)
        for cp in lands:
            cp.wait_recv()
        for cp in sends:
            cp.wait_send()


class _PairShare(_Carry):
    def __init__(self, halves):
        self.names = list(halves)
        self.ins = [halves[n] for n in self.names]
        self.out_shape = [jax.ShapeDtypeStruct(a.shape, F32) for a in self.ins]
        T = len(self.names)
        self.aliases = {t: t for t in range(T)}
        self.scratch = [pltpu.SemaphoreType.DMA((T,)), pltpu.SemaphoreType.DMA((T,))]

    def _copies(self, in_refs, out_refs, sems):
        ssem, rsem = sems
        x, y, c, _ = _place()
        sibling = (x, y, 1 - c)
        T = len(self.names)
        sends = [_remote(out_refs[t].at[c], out_refs[t].at[c], ssem.at[t], rsem.at[t], sibling) for t in range(T)]
        lands = [_remote(out_refs[t].at[1 - c], out_refs[t].at[1 - c], ssem.at[t], rsem.at[t], sibling) for t in range(T)]
        return sends, lands

    def start(self, in_refs, out_refs, sems):
        for cp in self._copies(in_refs, out_refs, sems)[0]:
            cp.start()

    def finish(self, in_refs, out_refs, sems):
        sends, lands = self._copies(in_refs, out_refs, sems)
        for cp in lands:
            cp.wait_recv()
        for cp in sends:
            cp.wait_send()


def _pair_sum(own, landed, c_arr, name):
    _, ns, r, cc = own.shape
    rows = ns * r
    tr = _fit(512, rows)

    def body(c_ref, a_ref, b_ref, o_ref, o2_ref):
        total = (a_ref[...].astype(F32) + b_ref[...].astype(F32)).astype(o_ref.dtype)
        o_ref[...] = total
        o2_ref[...] = total

    tile = pl.BlockSpec((tr, cc), lambda i, c_ref: (i, 0))
    out = pl.pallas_call(
        body, name=name, out_shape=[jax.ShapeDtypeStruct((rows, cc), BF16)] * 2,
        grid_spec=pltpu.PrefetchScalarGridSpec(
            num_scalar_prefetch=1, grid=(rows // tr,),
            in_specs=[pl.BlockSpec((None, tr, cc), lambda i, c_ref: (c_ref[0], i, 0)), tile], out_specs=[tile, tile]),
        compiler_params=pltpu.CompilerParams(dimension_semantics=("parallel",), vmem_limit_bytes=_vmem(8 * tr * cc * 4)),
    )(c_arr, own.reshape(2, rows, cc), landed.reshape(rows, cc))
    return out[0].reshape(ns, r, cc), out[1].reshape(ns, r, cc)


def _chip_sum(parts, c_arr, name):
    _, r, cc = parts.shape
    tr = _fit(256, r)

    def body(c_ref, p_ref, o_ref):
        acc = p_ref[0].astype(F32)
        for j in range(1, N_CHIPS):
            acc = acc + p_ref[j].astype(F32)
        o_ref[...] = acc

    return pl.pallas_call(
        body, name=name, out_shape=jax.ShapeDtypeStruct((2, r, cc), F32),
        grid_spec=pltpu.PrefetchScalarGridSpec(
            num_scalar_prefetch=1, grid=(r // tr,),
            in_specs=[pl.BlockSpec((N_CHIPS, tr, cc), lambda i, c_ref: (0, i, 0))],
            out_specs=pl.BlockSpec((None, tr, cc), lambda i, c_ref: (c_ref[0], i, 0))),
        compiler_params=pltpu.CompilerParams(dimension_semantics=("parallel",), vmem_limit_bytes=_vmem(12 * tr * cc * 4)),
    )(c_arr, parts)


def _adamw_math(w, g, m, v):
    m2 = ADAM_B1 * m + (1.0 - ADAM_B1) * g
    v2 = ADAM_B2 * v + (1.0 - ADAM_B2) * jnp.square(g)
    m_hat = m2 / (1.0 - ADAM_B1 ** ADAM_STEP)
    v_hat = v2 / (1.0 - ADAM_B2 ** ADAM_STEP)
    delta = -ADAM_LR * (m_hat / (jnp.sqrt(v_hat) + ADAM_EPS) + ADAM_WD * w)
    return delta, m2, v2


def _adamw_mat(g2, w, m, v, layer, kind, prev, name):
    shape = w.shape
    ks, ns = shape[-2:]
    _, r, cc = g2.shape
    tr, tc = _fit(256, r), _fit(1024, cc)
    assert (r, cc) == ((ks // 2, ns) if kind == "col" else (ks, ns // 2))
    assert r % tr == 0 and cc % tc == 0
    rb, cb = r // tr, cc // tc
    if kind == "col":
        g_spec = pl.BlockSpec((None, tr, tc), lambda i, j: (i // rb, i % rb, j))
    else:
        g_spec = pl.BlockSpec((None, tr, tc), lambda i, j: (j // cb, i, j % cb))
    if w.ndim == 3:
        w_spec = pl.BlockSpec((None, tr, tc), lambda i, j: (layer, i, j))
    else:
        w_spec = pl.BlockSpec((tr, tc), lambda i, j: (i, j))
    n_prev = 0 if prev is None else 4

    def body(*refs):
        g_ref, w_ref, m_ref, v_ref = refs[:4]
        go_ref, d_ref, mo_ref, vo_ref = refs[4 + n_prev:]
        g = g_ref[...]
        delta, m2, v2 = _adamw_math(w_ref[...], g, m_ref[...], v_ref[...])
        go_ref[...] = g
        d_ref[...] = delta
        mo_ref[...] = m2
        vo_ref[...] = v2

    return pl.pallas_call(
        body, name=name, grid=(ks // tr, ns // tc),
        in_specs=[g_spec, w_spec, w_spec, w_spec] + [ANY] * n_prev, out_specs=[w_spec] * 4,
        out_shape=[jax.ShapeDtypeStruct(shape, F32)] * 4,
        input_output_aliases={4 + i: i for i in range(n_prev)},
        compiler_params=pltpu.CompilerParams(dimension_semantics=("parallel", "parallel"),
                                             vmem_limit_bytes=_vmem(16 * tr * tc * 4)),
    )(g2, w, m, v, *(prev or ()))


def _adamw_small(g, w, m, v, name):
    def body(g_ref, w_ref, m_ref, v_ref, d_ref, mo_ref, vo_ref):
        delta, m2, v2 = _adamw_math(w_ref[...], g_ref[...], m_ref[...], v_ref[...])
        d_ref[...] = delta
        mo_ref[...] = m2
        vo_ref[...] = v2

    return pl.pallas_call(body, name=name, out_shape=[jax.ShapeDtypeStruct(w.shape, F32)] * 3)(g, w, m, v)


def _sum8(parts, name):
    def body(p_ref, o_ref):
        acc = p_ref[0]
        for j in range(1, 8):
            acc = acc + p_ref[j]
        o_ref[...] = acc

    return pl.pallas_call(body, name=name, out_shape=jax.ShapeDtypeStruct(parts.shape[1:], F32),
                          compiler_params=pltpu.CompilerParams(vmem_limit_bytes=_vmem(12 * _nbytes(parts.shape[1:], F32))))(parts)


_REDUCE_AT = {
    "d_ple_gate1": (("A", "ple_proj1"),),
    "dh2_1": (("A", "ple_gate1"),),
    "d_mlp_down1": (("B", "ple_proj1"), ("B", "ple_gate1")),
    "dt1": (("A", "mlp_down1"),),
    "d_mlp_up1": (("B", "mlp_down1"), ("C", "ple_proj1"), ("C", "ple_gate1")),
    "dh1_1": (("A", "mlp_up1"),),
    "d_attn_w_o": (("C", "mlp_down1"),),
    "attn_do": (("A", "attn_w_o"),),
    "attn_bwd_g0": (("B", "attn_w_o"),),
    "attn_unrot": (("C", "attn_w_o"),),
    "d_attn_w_q": (("B", "mlp_up1"),),
    "dx1_q": (("A", "attn_w_q"), ("C", "mlp_up1")),
    "dkvn_k": (("A", "w_kv"),),
    "d_ple_gate0": (("A", "ple_proj0"),),
    "dh2_0": (("A", "ple_gate0"),),
    "d_mlp_down0": (("B", "attn_w_q"), ("B", "ple_proj0")),
    "dt0": (("B", "w_kv"), ("B", "ple_gate0"), ("A", "mlp_down0")),
    "d_mlp_up0": (("B", "mlp_down0"), ("C", "attn_w_q"), ("C", "ple_proj0"), ("C", "w_kv"), ("C", "ple_gate0")),
    "dh1_0": (("A", "mlp_up0"),),
    "d_conv_w_out": (("C", "mlp_down0"),),
    "conv_ds": (("A", "conv_w_out"),),
    "conv_bwd": (("B", "mlp_up0"), ("B", "conv_w_out")),
    "d_conv_w_in_g": (("C", "mlp_up0"), ("C", "conv_w_out")),
    "dx_a": (("A", "conv_w_in"),),
    "dx_g": (("B", "conv_w_in"),),
    "share_last": (("C", "conv_w_in"),),
}


class _Reducer:
    def __init__(self, w, mom, var, c_arr):
        self.w, self.mom, self.var, self.c_arr = w, mom, var, c_arr
        self.mats = {m[0]: m for m in _MATS}
        self.grads, self.pair_sums, self.chip_sums, self.out = {}, {}, {}, {}

    def produced(self, name, grad):
        self.grads[name] = grad

    def carry(self, call):
        parts = []
        for cls, stage, src in ((_PairSend, "A", self.grads), (_ChipScatter, "B", self.pair_sums),
                                (_PairShare, "C", self.chip_sums)):
            names = [n for s, n in _REDUCE_AT.get(call, ()) if s == stage]
            if names:
                parts.append((stage, cls({n: src[n] for n in names})))
        self._parts = parts
        return _Multi([p for _, p in parts]) if parts else None

    def carried(self):
        for stage, part in self._parts:
            for name, val in part.result.items():
                if stage == "A":
                    self.pair_sums[name] = _pair_sum(self.grads[name], val, self.c_arr, f"pair_sum_{name}")
                elif stage == "B":
                    self.chip_sums[name] = _chip_sum(val, self.c_arr, f"chip_sum_{name}")
                else:
                    _, src, layer, kind, _ = self.mats[name]
                    self.out[src] = _adamw_mat(val, self.w[src], self.mom[src], self.var[src], layer or 0, kind,
                                               self.out.get(src), f"adamw_{name}")
        self._parts = []


_WEIGHTS = ("conv_w_in", "conv_b_in", "conv_dw", "conv_dw_b", "conv_ln_g", "conv_ln_b", "conv_w_out", "kv_ln_g",
            "kv_ln_b", "w_kv", "attn_w_q", "attn_w_o", "ln1_g", "ln1_b", "mlp_up", "mlp_down", "ln2_g", "ln2_b",
            "ple_proj", "ple_gate")
_SHARDED_VECS = ("conv_b_in", "conv_dw", "conv_dw_b", "conv_ln_g", "conv_ln_b")
_REPLICATED_VECS = ("kv_ln_g", "kv_ln_b", "ln1_g", "ln1_b", "ln2_g", "ln2_b")


def _pad_rows(a, rows):
    return jnp.concatenate([a, jnp.zeros((rows - a.shape[0], a.shape[1]), a.dtype)], axis=0) if a.shape[0] < rows else a


def _pack_sharded(d):
    n = d["conv_dw_b"].shape[-1]
    rows = [d["conv_b_in"].reshape(2, n), d["conv_dw"].reshape(CONV_WIDTH, n), d["conv_dw_b"].reshape(1, n),
            d["conv_ln_g"].reshape(1, n), d["conv_ln_b"].reshape(1, n)]
    return _pad_rows(jnp.concatenate(rows, axis=0), 40)


def _unpack_sharded(pack, like):
    n = pack.shape[1]
    return {"conv_b_in": pack[0:2].reshape(like["conv_b_in"].shape),
            "conv_dw": pack[2:2 + CONV_WIDTH].reshape(like["conv_dw"].shape),
            "conv_dw_b": pack[33:34].reshape(like["conv_dw_b"].shape),
            "conv_ln_g": pack[34:35].reshape(like["conv_ln_g"].shape),
            "conv_ln_b": pack[35:36].reshape(like["conv_ln_b"].shape)}


def _pack_replicated(d):
    D = d["kv_ln_g"].shape[-1]
    rows = [d[n].reshape(-1, D) for n in _REPLICATED_VECS]
    return _pad_rows(jnp.concatenate(rows, axis=0), 16)


def _unpack_replicated(pack, like):
    out, r = {}, 0
    for n in _REPLICATED_VECS:
        k = like[n].size // pack.shape[1]
        out[n] = pack[r:r + k].reshape(like[n].shape)
        r += k
    return out


def kernel(x, p, positions, conv_w_in, conv_b_in, conv_dw, conv_dw_b, conv_ln_g, conv_ln_b, conv_w_out, kv_ln_g, kv_ln_b, w_kv, attn_w_q, attn_w_o, ln1_g, ln1_b, mlp_up, mlp_down, ln2_g, ln2_b, ple_proj, ple_gate, loss_target, m_conv_w_in, m_conv_b_in, m_conv_dw, m_conv_dw_b, m_conv_ln_g, m_conv_ln_b, m_conv_w_out, m_kv_ln_g, m_kv_ln_b, m_w_kv, m_attn_w_q, m_attn_w_o, m_ln1_g, m_ln1_b, m_mlp_up, m_mlp_down, m_ln2_g, m_ln2_b, m_ple_proj, m_ple_gate, v_conv_w_in, v_conv_b_in, v_conv_dw, v_conv_dw_b, v_conv_ln_g, v_conv_ln_b, v_conv_w_out, v_kv_ln_g, v_kv_ln_b, v_w_kv, v_attn_w_q, v_attn_w_o, v_ln1_g, v_ln1_b, v_mlp_up, v_mlp_down, v_ln2_g, v_ln2_b, v_ple_proj, v_ple_gate):
    args = dict(locals())
    w = {n: args[n] for n in _WEIGHTS}
    mom = {n: args["m_" + n] for n in _WEIGHTS}
    var = {n: args["v_" + n] for n in _WEIGHTS}
    S, D = x.shape[1:]
    n4 = D // N_CHIPS
    chip = 2 * lax.axis_index("x") + lax.axis_index("y")
    c_arr = lax.axis_index("c").astype(jnp.int32).reshape(1)

    shards = {n: w[n].astype(BF16) for n in sorted({m[1] for m in _MATS})}
    vec_all = _allgather8(_pack_sharded(w), "gather_vectors")
    vec_full = jnp.concatenate([vec_all[2 * j] for j in range(N_CHIPS)], axis=1)
    b_in = vec_all[0::2, 0:2, :].reshape(1, 2 * D)
    V = {"conv_b_a": b_in[:, :D], "conv_b_g": b_in[:, D:],
         "conv_dw": _pad_rows(vec_full[2:2 + CONV_WIDTH], CONV_PAD), "conv_dw_b": vec_full[33:34],
         "conv_ln_g": vec_full[34:35], "conv_ln_b": vec_full[35:36],
         "kv_ln_g": kv_ln_g.reshape(1, D), "kv_ln_b": kv_ln_b.reshape(1, D)}
    for l in range(2):
        for n in ("ln1_g", "ln1_b", "ln2_g", "ln2_b"):
            V[f"{n}{l}"] = w[n][l].reshape(1, D)

    half = HEAD_DIM // 2
    inv_freq = ROPE_THETA ** (-jnp.arange(half, dtype=F32) * (2.0 / HEAD_DIM))
    ang = positions[0].astype(F32)[:, None] * inv_freq
    cos, sin = jnp.cos(ang), jnp.sin(ang)
    cosf = jnp.concatenate([cos, cos], axis=-1)
    sinf = jnp.concatenate([-sin, sin], axis=-1)

    reducer = _Reducer(w, mom, var, c_arr)
    loss_cols, grad_x, _, gv = _local_step(x[0], p[:, 0], cosf, sinf, loss_target[0], None, V, shards, reducer,
                                           chip.astype(jnp.int32).reshape(1))
    loss = lax.psum(jnp.sum(loss_cols), ("x", "y", "c"))
    out = dict(reducer.out)

    gpack = jnp.concatenate([gv["conv_b_a"], gv["conv_b_g"], gv["conv_dw"][:CONV_WIDTH], gv["conv_dw_b"],
                             gv["conv_ln_g"], gv["conv_ln_b"], gv["kv_ln_g"], gv["kv_ln_b"],
                             gv["ln1_g0"], gv["ln1_g1"], gv["ln1_b0"], gv["ln1_b1"],
                             gv["ln2_g0"], gv["ln2_g1"], gv["ln2_b0"], gv["ln2_b1"]], axis=0)
    gsum = _sum8(_allgather8(_pad_rows(gpack, 48), "gather_vector_grads"), "sum_vector_grads")
    g_b = lax.dynamic_slice_in_dim(jnp.concatenate([gsum[0:1], gsum[1:2]], axis=1), chip * 2 * n4, 2 * n4, axis=1)
    g_sh = lax.dynamic_slice_in_dim(gsum[2:36], chip * n4, n4, axis=1)
    g_sh = _pad_rows(jnp.concatenate([g_b.reshape(2, n4), g_sh], axis=0), 40)
    d_sh, m_sh, v_sh = _adamw_small(g_sh, _pack_sharded(w), _pack_sharded(mom), _pack_sharded(var), "adamw_sharded_vectors")
    g_rep = _pad_rows(gsum[36:46], 16)
    d_rep, m_rep, v_rep = _adamw_small(g_rep, _pack_replicated(w), _pack_replicated(mom), _pack_replicated(var),
                                       "adamw_replicated_vectors")
    small = {}
    for i, (sh, rep) in enumerate(((g_sh, g_rep), (d_sh, d_rep), (m_sh, m_rep), (v_sh, v_rep))):
        d = {**_unpack_sharded(sh, w), **_unpack_replicated(rep, w)}
        for n, val in d.items():
            small.setdefault(n, [None] * 4)[i] = val
    for n in small:
        out[n] = small[n]

    res = [loss, grad_x[None]]
    for i in range(4):
        res += [out[n][i] for n in _WEIGHTS]
    return tuple(res)
```

```python
import functools

import jax
import jax.numpy as jnp
from jax import lax
from jax.experimental import pallas as pl
from jax.experimental.pallas import tpu as pltpu

F32 = jnp.float32
BF16 = jnp.bfloat16

HEAD_DIM = 128
ATTN_BLOCK = 128
DILATIONS = (1, 4, 16)
N_GROUPS = 3
CONV_WIDTH = 31
CONV_PAD = 32
CONV_ROWS = 32
EPILOGUE_ROWS = 128
ROPE_THETA = 10000.0
LN_EPS = 1e-5
ALPHA = 4.0 ** 0.25
ATTN_SCALE = HEAD_DIM ** -0.5
NEG = -1e30

ADAM_LR = 0.001
ADAM_B1 = 0.9
ADAM_B2 = 0.999
ADAM_EPS = 1e-08
ADAM_WD = 0.01
ADAM_STEP = 10

N_CHIPS = 4
VMEM_CAP = 60 << 20
MESH = pl.DeviceIdType.MESH
ANY = pl.BlockSpec(memory_space=pl.ANY)


def _vmem(nbytes):
    return int(min(max(2 * nbytes + (8 << 20), 24 << 20), VMEM_CAP))


def _fit(tile, n):
    if n <= tile:
        return n
    t = tile - tile % 128
    while n % t:
        t -= 128
    return t


def _nbytes(shape, dtype):
    n = 1
    for s in shape:
        n *= s
    return n * jnp.dtype(dtype).itemsize


_DIMS = {"nn": (((1,), (0,)), ((), ())), "nt": (((1,), (1,)), ((), ())), "tn": (((0,), (0,)), ((), ()))}


def _pcall(body, *, name, grid, in_specs, out_specs, out_shape, operands, scratch_shapes=(), vmem, carry=None):
    if carry is None:
        return pl.pallas_call(
            body, name=name, grid=grid, in_specs=in_specs, out_specs=out_specs, out_shape=out_shape,
            scratch_shapes=list(scratch_shapes),
            compiler_params=pltpu.CompilerParams(dimension_semantics=("arbitrary",) * len(grid), vmem_limit_bytes=vmem),
        )(*operands)
    n_in, n_out, n_scr = len(in_specs), len(out_specs), len(scratch_shapes)
    c_in, c_out = len(carry.ins), len(carry.out_shape)

    def wrapped(*refs):
        ins, refs = refs[:n_in], refs[n_in:]
        c_ins, refs = refs[:c_in], refs[c_in:]
        outs, refs = refs[:n_out], refs[n_out:]
        c_outs, refs = refs[:c_out], refs[c_out:]
        scr, c_sems = refs[:n_scr], refs[n_scr:]
        first = functools.reduce(jnp.logical_and, [pl.program_id(d) == 0 for d in range(len(grid))])
        last = functools.reduce(jnp.logical_and, [pl.program_id(d) == grid[d] - 1 for d in range(len(grid))])
        pl.when(first)(lambda: carry.start(c_ins, c_outs, c_sems))
        body(*ins, *outs, *scr)
        pl.when(last)(lambda: carry.finish(c_ins, c_outs, c_sems))

    res = pl.pallas_call(
        wrapped, name=name, grid=grid, in_specs=list(in_specs) + [ANY] * c_in, out_specs=list(out_specs) + [ANY] * c_out,
        out_shape=list(out_shape) + list(carry.out_shape), scratch_shapes=list(scratch_shapes) + list(carry.scratch),
        input_output_aliases={len(operands) + i: n_out + o for i, o in carry.aliases.items()},
        compiler_params=pltpu.CompilerParams(dimension_semantics=("arbitrary",) * len(grid), vmem_limit_bytes=vmem),
    )(*operands, *carry.ins)
    carry.set_result(res[n_out:])
    return res[:n_out]


def _mm(a, b, *, mode, outs, name, epilogue=None, extras=(), rextras=(), vecs=(), a_sel=None, b_sel=None,
        tm=None, tn=2048, tk=None, layout=None, carry=None, ep_rows=None, a_fn=None, sums=0):
    a2, b2 = a.shape[-2:], b.shape[-2:]
    if mode == "nn":
        (M, K), (K2, N) = a2, b2
    elif mode == "nt":
        (M, K), (N, K2) = a2, b2
    else:
        (K, M), (K2, N) = a2, b2
    assert K == K2, (a.shape, b.shape, mode)
    if tm is None:
        tm = 1024 if mode == "tn" else 512
    if tk is None:
        tk = 1024 if mode == "tn" else 2048
    if layout is not None:
        kind, nslots = layout
        r, c = (M // 2, N // nslots) if kind == "col" else (M // nslots, N // 2)
        tm, tn = _fit(tm, r), _fit(tn, c)
        assert r % tm == 0 and c % tn == 0
    else:
        tm, tn = _fit(tm, M), _fit(tn, N)
    tk = _fit(tk, K)
    assert M % tm == 0 and N % tn == 0 and K % tk == 0, (M, N, K, tm, tn, tk)
    nk = K // tk
    grid = (N // tn, M // tm, nk)

    def spec(arr, sel, blk, imap):
        if arr.ndim == 3:
            return pl.BlockSpec((None,) + blk, lambda j, i, k: (sel,) + imap(j, i, k))
        return pl.BlockSpec(blk, imap)

    if mode == "tn":
        a_spec = spec(a, a_sel, (tk, tm), lambda j, i, k: (k, i))
    else:
        a_spec = spec(a, a_sel, (tm, tk), lambda j, i, k: (i, k))
    if mode == "nt":
        b_spec = spec(b, b_sel, (tn, tk), lambda j, i, k: (j, k))
    else:
        b_spec = spec(b, b_sel, (tk, tn), lambda j, i, k: (k, j))
    in_specs = [a_spec, b_spec]
    in_specs += [pl.BlockSpec((tm, tn), lambda j, i, k: (i, j)) for _ in extras]
    in_specs += [pl.BlockSpec((tm, e.shape[1]), lambda j, i, k: (i, 0)) for e in rextras]
    in_specs += [pl.BlockSpec((1, tn), lambda j, i, k: (0, j)) for _ in vecs]

    if layout is None:
        out_shape = [jax.ShapeDtypeStruct((M, N), d) for d in outs] + [jax.ShapeDtypeStruct((1, N), F32)] * sums
        out_specs = [pl.BlockSpec((tm, tn), lambda j, i, k: (i, j)) for _ in outs]
        out_specs += [pl.BlockSpec((1, tn), lambda j, i, k: (0, j))] * sums
    else:
        assert len(outs) == 1
        out_shape = [jax.ShapeDtypeStruct((2, nslots, r, c), outs[0])]
        rb, cb = r // tm, c // tn
        if kind == "col":
            omap = lambda j, i, k: (i // rb, j // cb, i % rb, j % cb)
        else:
            omap = lambda j, i, k: (j // cb, i // rb, i % rb, j % cb)
        out_specs = [pl.BlockSpec((None, None, tm, tn), omap)]

    ne, nr, nv, no = len(extras), len(rextras), len(vecs), len(outs)
    dims = _DIMS[mode]

    def body(*refs):
        a_ref, b_ref = refs[0], refs[1]
        rest = refs[2:2 + ne + nr + nv]
        o_refs = refs[2 + ne + nr + nv:2 + ne + nr + nv + no]
        s_refs = refs[2 + ne + nr + nv + no:2 + ne + nr + nv + no + sums]

        def finish(total):
            if epilogue is None:
                for o in o_refs:
                    o[...] = total.astype(o.dtype)
                return
            step = min(ep_rows or tm, tm)
            col_sums = [None] * sums
            for r0 in range(0, tm, step):
                rows = slice(r0, r0 + step)
                tiles = [x[rows, :] for x in rest[:ne + nr]] + [x[...] for x in rest[ne + nr:]]
                res = epilogue(total[rows, :], *tiles)
                for o, val in zip(o_refs, res[:no]):
                    o[rows, :] = val.astype(o.dtype)
                for n_, val in enumerate(res[no:]):
                    part_sum = jnp.sum(val, axis=0, keepdims=True)
                    col_sums[n_] = part_sum if col_sums[n_] is None else col_sums[n_] + part_sum
            if sums:
                @pl.when(pl.program_id(1) == 0)
                def _():
                    for s_ref, val in zip(s_refs, col_sums):
                        s_ref[...] = val

                @pl.when(pl.program_id(1) > 0)
                def _():
                    for s_ref, val in zip(s_refs, col_sums):
                        s_ref[...] += val

        def product():
            a_tile = a_ref[...] if a_fn is None else a_fn(a_ref[...])
            return lax.dot_general(a_tile.astype(BF16), b_ref[...].astype(BF16), dims, preferred_element_type=F32)

        if nk == 1:
            finish(product())
            return
        acc = refs[-1]
        k = pl.program_id(2)

        @pl.when(k == 0)
        def _():
            acc[...] = product()

        @pl.when(k > 0)
        def _():
            acc[...] += product()

        @pl.when(k == nk - 1)
        def _():
            finish(acc[...])

    blk = (_nbytes((tm, tk), a.dtype) + _nbytes((tk, tn), b.dtype) + sum(_nbytes((tm, tn), e.dtype) for e in extras)
           + sum(_nbytes((tm, tn), d) for d in outs) + 2 * tm * tn * 4)
    res = _pcall(body, name=name, grid=grid, in_specs=in_specs, out_specs=out_specs, out_shape=out_shape,
                 operands=(a, b, *extras, *rextras, *vecs),
                 scratch_shapes=[pltpu.VMEM((tm, tn), F32)] if nk > 1 else [], vmem=_vmem(blk), carry=carry)
    return res[0] if no + sums == 1 else tuple(res)


def _rows(fn, rows, vecs, outs, sums, *, tm, name, carry=None):
    S = rows[0].shape[0]
    tm = min(tm, S)
    assert S % tm == 0
    nr, nv, no, ns = len(rows), len(vecs), len(outs), len(sums)

    def body(*refs):
        vals = fn(*[r[...] for r in refs[:nr + nv]])
        o_refs = refs[nr + nv:nr + nv + no]
        s_refs = refs[nr + nv + no:]
        for o, val in zip(o_refs, vals[:no]):
            o[...] = val.astype(o.dtype)
        if ns:
            @pl.when(pl.program_id(0) == 0)
            def _():
                for s in s_refs:
                    s[...] = jnp.zeros_like(s)

            for s, val in zip(s_refs, vals[no:]):
                s[...] += jnp.sum(val.astype(F32), axis=0, keepdims=True)

    in_specs = [pl.BlockSpec((tm, r.shape[1]), lambda i: (i, 0)) for r in rows]
    in_specs += [pl.BlockSpec(v.shape, lambda i: (0, 0)) for v in vecs]
    out_specs = [pl.BlockSpec((tm, c), lambda i: (i, 0)) for c, _ in outs]
    out_specs += [pl.BlockSpec((1, c), lambda i: (0, 0)) for c in sums]
    out_shape = [jax.ShapeDtypeStruct((S, c), d) for c, d in outs]
    out_shape += [jax.ShapeDtypeStruct((1, c), F32) for c in sums]
    blk = sum(_nbytes((tm, r.shape[1]), r.dtype) for r in rows) + sum(_nbytes((tm, c), d) for c, d in outs)
    blk += 6 * tm * max(r.shape[1] for r in rows) * 4
    res = _pcall(body, name=name, grid=(S // tm,), in_specs=in_specs, out_specs=out_specs, out_shape=out_shape,
                 operands=(*rows, *vecs), vmem=_vmem(blk), carry=carry)
    return tuple(res)


def _ln_norm(z):
    mu = jnp.mean(z, axis=-1, keepdims=True)
    d = z - mu
    var = jnp.mean(d * d, axis=-1, keepdims=True)
    rstd = lax.rsqrt(var + LN_EPS)
    return d * rstd, rstd


def _ln(z, g, b):
    return _ln_norm(z)[0] * g + b


def _ln_bwd(dy, n, rstd, g):
    dn = dy * g
    return rstd * (dn - jnp.mean(dn, axis=-1, keepdims=True) - n * jnp.mean(dn * n, axis=-1, keepdims=True))


def _sq_relu(t):
    return jnp.square(jnp.maximum(t.astype(F32), 0.0))


def _sigmoid(x):
    return 1.0 / (1.0 + jnp.exp(-x))


def _per_head(x, fn):
    h = x.shape[1] // HEAD_DIM
    return jnp.concatenate([fn(x[:, i * HEAD_DIM:(i + 1) * HEAD_DIM], i) for i in range(h)], axis=1)


def _rot(x, cosf, sinf):
    return _per_head(x, lambda xh, i: xh * cosf + pltpu.roll(xh, HEAD_DIM // 2, 1) * sinf)


def _rot_t(dy, cosf, sinf):
    return _per_head(dy, lambda dh, i: dh * cosf + pltpu.roll(dh * sinf, HEAD_DIM // 2, 1))


def _shift_copies(win):
    rows = win.shape[1] - 8
    for s in range(1, 8):
        win[s, 0:rows, :] = win[0, s:s + rows, :]


def _rows_at(win, start):
    s = start % 8
    return win[s, start - s:start - s + CONV_ROWS, :]


def _conv_fwd(glu, dw, dwb, *, tm=256, tc=512, name="conv_fwd", carry=None):
    S, D = glu.shape
    tm, tc = min(tm, S), min(tc, D)
    ni = S // tm

    def body(cur_ref, prev_ref, dw_ref, dwb_ref, o_ref, win):
        i = pl.program_id(1)
        tail = prev_ref[tm - CONV_PAD:tm, :]
        win[0, 0:CONV_PAD, :] = jnp.where(i > 0, tail, jnp.zeros_like(tail))
        win[0, CONV_PAD:CONV_PAD + tm, :] = cur_ref[...]
        _shift_copies(win)
        first = CONV_PAD - CONV_WIDTH + 1
        for r0 in range(0, tm, CONV_ROWS):
            acc = jnp.zeros((CONV_ROWS, tc), F32) + dwb_ref[...]
            for k in range(CONV_WIDTH):
                acc = acc + _rows_at(win, r0 + first + k) * dw_ref[k:k + 1, :]
            o_ref[r0:r0 + CONV_ROWS, :] = acc

    return _pcall(
        body, name=name, grid=(D // tc, ni),
        in_specs=[pl.BlockSpec((tm, tc), lambda j, i: (i, j)),
                  pl.BlockSpec((tm, tc), lambda j, i: (jnp.maximum(i - 1, 0), j)),
                  pl.BlockSpec((CONV_PAD, tc), lambda j, i: (0, j)),
                  pl.BlockSpec((1, tc), lambda j, i: (0, j))],
        out_specs=[pl.BlockSpec((tm, tc), lambda j, i: (i, j))],
        out_shape=[jax.ShapeDtypeStruct((S, D), F32)],
        scratch_shapes=[pltpu.VMEM((8, tm + CONV_PAD, tc), F32)],
        operands=(glu, glu, dw, dwb), vmem=_vmem(8 * tm * tc * 4), carry=carry)[0]


def _conv_bwd(dc, glu, a_pre, g_pre, dw, ba, bg, *, tm=256, tc=512, name="conv_bwd", carry=None):
    S, D = dc.shape
    tm, tc = min(tm, S), min(tc, D)
    ni = S // tm

    def fold8(v):
        out = v[0:8]
        for r in range(8, CONV_ROWS, 8):
            out = out + v[r:r + 8]
        return out

    def body(dc_ref, dcn_ref, glu_ref, glup_ref, a_ref, g_ref, dw_ref, ba_ref, bg_ref,
             da_ref, dg_ref, ddw_ref, dba_ref, dbg_ref, dwin, gwin, taps):
        i = pl.program_id(1)

        @pl.when(i == 0)
        def _():
            ddw_ref[...] = jnp.zeros_like(ddw_ref)
            dba_ref[...] = jnp.zeros_like(dba_ref)
            dbg_ref[...] = jnp.zeros_like(dbg_ref)

        head = dcn_ref[0:CONV_PAD, :]
        dwin[0, 0:tm, :] = dc_ref[...]
        dwin[0, tm:tm + CONV_PAD, :] = jnp.where(i < ni - 1, head, jnp.zeros_like(head))
        tail = glup_ref[tm - CONV_PAD:tm, :]
        gwin[0, 0:CONV_PAD, :] = jnp.where(i > 0, tail, jnp.zeros_like(tail))
        gwin[0, CONV_PAD:CONV_PAD + tm, :] = glu_ref[...]
        _shift_copies(dwin)
        _shift_copies(gwin)
        taps[...] = jnp.zeros_like(taps)
        first = CONV_PAD - CONV_WIDTH + 1
        sum_a = jnp.zeros((8, tc), F32)
        sum_g = jnp.zeros((8, tc), F32)
        for r0 in range(0, tm, CONV_ROWS):
            dcur = dc_ref[r0:r0 + CONV_ROWS, :]
            dglu = jnp.zeros((CONV_ROWS, tc), F32)
            for k in range(CONV_WIDTH):
                dglu = dglu + _rows_at(dwin, r0 + CONV_WIDTH - 1 - k) * dw_ref[k:k + 1, :]
                taps[k] += fold8(dcur * _rows_at(gwin, r0 + first + k))
            a = a_ref[r0:r0 + CONV_ROWS, :] + ba_ref[...]
            sg = _sigmoid(g_ref[r0:r0 + CONV_ROWS, :] + bg_ref[...])
            da = dglu * sg
            dg = dglu * a * sg * (1.0 - sg)
            da_ref[r0:r0 + CONV_ROWS, :] = da.astype(BF16)
            dg_ref[r0:r0 + CONV_ROWS, :] = dg.astype(BF16)
            sum_a = sum_a + fold8(da)
            sum_g = sum_g + fold8(dg)
        ddw_ref[...] += jnp.sum(taps[...], axis=1)
        dba_ref[...] += jnp.sum(sum_a, axis=0, keepdims=True)
        dbg_ref[...] += jnp.sum(sum_g, axis=0, keepdims=True)

    tile = lambda f: pl.BlockSpec((tm, tc), f)
    vec = pl.BlockSpec((1, tc), lambda j, i: (0, j))
    return _pcall(
        body, name=name, grid=(D // tc, ni),
        in_specs=[tile(lambda j, i: (i, j)), tile(lambda j, i: (jnp.minimum(i + 1, ni - 1), j)),
                  tile(lambda j, i: (i, j)), tile(lambda j, i: (jnp.maximum(i - 1, 0), j)),
                  tile(lambda j, i: (i, j)), tile(lambda j, i: (i, j)),
                  pl.BlockSpec((CONV_PAD, tc), lambda j, i: (0, j)), vec, vec],
        out_specs=[tile(lambda j, i: (i, j)), tile(lambda j, i: (i, j)),
                   pl.BlockSpec((CONV_PAD, tc), lambda j, i: (0, j)), vec, vec],
        out_shape=[jax.ShapeDtypeStruct((S, D), BF16), jax.ShapeDtypeStruct((S, D), BF16),
                   jax.ShapeDtypeStruct((CONV_PAD, D), F32), jax.ShapeDtypeStruct((1, D), F32),
                   jax.ShapeDtypeStruct((1, D), F32)],
        scratch_shapes=[pltpu.VMEM((8, tm + CONV_PAD, tc), F32), pltpu.VMEM((8, tm + CONV_PAD, tc), F32),
                        pltpu.VMEM((CONV_PAD, 8, tc), F32)],
        operands=(dc, dc, glu, glu, a_pre, g_pre, dw, ba, bg), vmem=_vmem(16 * tm * tc * 4), carry=carry)


def _nt(a, b):
    return lax.dot_general(a, b, _DIMS["nt"], preferred_element_type=F32)


def _tn(a, b):
    return lax.dot_general(a, b, _DIMS["tn"], preferred_element_type=F32)


HEADS_TOGETHER = 8
STAT_LANES = 128


def _per_head_pack(cols):
    rows = cols[0].shape[0]
    lane = lax.broadcasted_iota(jnp.int32, (rows, STAT_LANES), 1)
    out = jnp.zeros((rows, STAT_LANES), F32)
    for h, col in enumerate(cols):
        out = jnp.where(lane == h, col, out)
    return out


def _window_mask(qi, kj, first_key):
    B = ATTN_BLOCK
    return ((kj < B) & (kj >= qi) & (kj >= first_key)) | ((kj >= B) & (kj - B <= qi))


def _attn_fwd(q_rot, k, v, g, dil):
    S, D = k.shape
    H = D // HEAD_DIM
    L = S // dil
    nb_count = L // ATTN_BLOCK
    B = ATTN_BLOCK

    def body(q_ref, kc_ref, kp_ref, vc_ref, vp_ref, o_ref, lse_ref):
        nb = pl.program_id(1)
        qi = lax.broadcasted_iota(jnp.int32, (B, 2 * B), 0)
        kj = lax.broadcasted_iota(jnp.int32, (B, 2 * B), 1)
        valid = _window_mask(qi, kj, jnp.where(nb > 0, 0, B))
        stats = []
        for h0 in range(0, H, HEADS_TOGETHER):
            heads = range(h0, min(h0 + HEADS_TOGETHER, H))
            hs = [slice(h * HEAD_DIM, (h + 1) * HEAD_DIM) for h in heads]
            kk = [jnp.concatenate([kp_ref[:, c], kc_ref[:, c]], axis=0) for c in hs]
            vv = [jnp.concatenate([vp_ref[:, c], vc_ref[:, c]], axis=0) for c in hs]
            s = [jnp.where(valid, _nt(q_ref[:, c], kk_) * ATTN_SCALE, NEG) for c, kk_ in zip(hs, kk)]
            m = [jnp.max(s_, axis=1, keepdims=True) for s_ in s]
            p = [jnp.exp(s_ - m_) for s_, m_ in zip(s, m)]
            l = [jnp.sum(p_, axis=1, keepdims=True) for p_ in p]
            o = [jnp.dot(p_.astype(BF16), vv_, preferred_element_type=F32) / l_ for p_, vv_, l_ in zip(p, vv, l)]
            for c, o_ in zip(hs, o):
                o_ref[:, c] = o_.astype(o_ref.dtype)
            stats += [m_ + jnp.log(l_) for m_, l_ in zip(m, l)]
        lse_ref[...] = _per_head_pack(stats)

    blk = lambda f: pl.BlockSpec((B, D), f)
    cur = lambda r, nb: (nb, r)
    prev = lambda r, nb: (jnp.maximum(nb - 1, 0), r)
    o, lse = pl.pallas_call(
        body, name=f"attn_fwd_g{g}", grid=(dil, nb_count),
        in_specs=[blk(lambda r, nb: (nb, r * N_GROUPS + g)), blk(cur), blk(prev), blk(cur), blk(prev)],
        out_specs=[blk(cur), pl.BlockSpec((B, STAT_LANES), cur)],
        out_shape=[jax.ShapeDtypeStruct((L, dil * D), BF16), jax.ShapeDtypeStruct((L, dil * STAT_LANES), F32)],
        compiler_params=pltpu.CompilerParams(dimension_semantics=("parallel", "arbitrary"),
                                             vmem_limit_bytes=_vmem(12 * B * D * 4)),
    )(q_rot.reshape(L, dil * N_GROUPS * D), k.reshape(L, dil * D), k.reshape(L, dil * D),
      v.reshape(L, dil * D), v.reshape(L, dil * D))
    return o.reshape(S, D), lse.reshape(S, STAT_LANES)


def _attn_bwd(q_rot, k, v, do, lse, dlt, g, dil, *, name, carry=None):
    S, D = k.shape
    H = D // HEAD_DIM
    L = S // dil
    nb_count = L // ATTN_BLOCK
    B = ATTN_BLOCK

    def body(q_ref, kc_ref, kp_ref, vc_ref, vp_ref, do_ref, l_ref, d_ref, dq_ref, dk_ref, dv_ref, keep_k, keep_v):
        s_id = pl.program_id(1)

        @pl.when(s_id == 0)
        def _():
            keep_k[...] = jnp.zeros_like(keep_k)
            keep_v[...] = jnp.zeros_like(keep_v)

        @pl.when(s_id < nb_count)
        def _():
            qi = lax.broadcasted_iota(jnp.int32, (B, 2 * B), 0)
            kj = lax.broadcasted_iota(jnp.int32, (B, 2 * B), 1)
            valid = _window_mask(qi, kj, jnp.where(s_id > 0, 0, B))
            for h0 in range(0, H, HEADS_TOGETHER):
                heads = list(range(h0, min(h0 + HEADS_TOGETHER, H)))
                hs = [slice(h * HEAD_DIM, (h + 1) * HEAD_DIM) for h in heads]
                q = [q_ref[:, c] for c in hs]
                dout = [do_ref[:, c] for c in hs]
                kk = [jnp.concatenate([kp_ref[:, c], kc_ref[:, c]], axis=0) for c in hs]
                vv = [jnp.concatenate([vp_ref[:, c], vc_ref[:, c]], axis=0) for c in hs]
                s = [jnp.where(valid, _nt(q_, kk_) * ATTN_SCALE, NEG) for q_, kk_ in zip(q, kk)]
                dp = [_nt(do_, vv_) for do_, vv_ in zip(dout, vv)]
                p = [jnp.exp(s_ - l_ref[:, h:h + 1]) for s_, h in zip(s, heads)]
                ds = [(p_ * (dp_ - d_ref[:, h:h + 1])).astype(BF16) for p_, dp_, h in zip(p, dp, heads)]
                dq = [jnp.dot(ds_, kk_, preferred_element_type=F32) * ATTN_SCALE for ds_, kk_ in zip(ds, kk)]
                dkk = [_tn(ds_, q_) * ATTN_SCALE for ds_, q_ in zip(ds, q)]
                dvv = [_tn(p_.astype(BF16), do_) for p_, do_ in zip(p, dout)]
                for c, dq_, dkk_, dvv_ in zip(hs, dq, dkk, dvv):
                    dq_ref[:, c] = dq_.astype(dq_ref.dtype)
                    dk_ref[:, c] = (keep_k[:, c] + dkk_[0:B]).astype(dk_ref.dtype)
                    dv_ref[:, c] = (keep_v[:, c] + dvv_[0:B]).astype(dv_ref.dtype)
                    keep_k[:, c] = dkk_[B:2 * B]
                    keep_v[:, c] = dvv_[B:2 * B]

        @pl.when(s_id == nb_count)
        def _():
            dk_ref[...] = keep_k[...].astype(dk_ref.dtype)
            dv_ref[...] = keep_v[...].astype(dv_ref.dtype)

    last = nb_count - 1
    blk = lambda f: pl.BlockSpec((B, D), f)
    cur = lambda r, s: (jnp.minimum(s, last), r)
    prev = lambda r, s: (jnp.maximum(jnp.minimum(s, last) - 1, 0), r)
    lag = lambda r, s: (jnp.maximum(s - 1, 0), r)
    qcur = lambda r, s: (jnp.minimum(s, last), r * N_GROUPS + g)
    qv = q_rot.reshape(L, dil * N_GROUPS * D)
    view = lambda t: t.reshape(L, dil * D)
    sview = lambda t: t.reshape(L, dil * STAT_LANES)
    stat = lambda f: pl.BlockSpec((B, STAT_LANES), f)
    dq, dk, dv = _pcall(
        body, name=name, grid=(dil, nb_count + 1),
        in_specs=[blk(qcur), blk(cur), blk(prev), blk(cur), blk(prev), blk(cur), stat(cur), stat(cur)],
        out_specs=[blk(cur), blk(lag), blk(lag)],
        out_shape=[jax.ShapeDtypeStruct((L, dil * D), BF16)] * 3,
        scratch_shapes=[pltpu.VMEM((B, D), F32), pltpu.VMEM((B, D), F32)],
        operands=(qv, view(k), view(k), view(v), view(v), view(do), sview(lse), sview(dlt)),
        vmem=_vmem(24 * B * D * 4), carry=carry)
    return dq.reshape(S, D), dk.reshape(S, D), dv.reshape(S, D)


def _mlp_ple_fwd(z1, h1b, p_l, W, vec, l, run, kv_vec=None, target=None):
    D = z1.shape[1]
    g1, b1, g2, b2 = vec

    t = run(_mm, h1b, W[f"mlp_up{l}"], mode="nn", outs=[BF16], tm=1024, name=f"mlp_up{l}")

    def z2_ep(acc, z1_t, g1_, b1_, g2_, b2_):
        z2 = ALPHA * _ln(z1_t, g1_, b1_) + acc
        return z2, _ln(z2, g2_, b2_)

    z2, h2b = run(_mm, t, W[f"mlp_down{l}"], mode="nn", outs=[F32, BF16], extras=[z1], vecs=[g1, b1, g2, b2], a_fn=_sq_relu,
                  epilogue=z2_ep, ep_rows=EPILOGUE_ROWS, name=f"mlp_down{l}")
    act = None
    pp = run(_mm, p_l, W[f"ple_proj{l}"], mode="nn", outs=[F32], name=f"ple_proj{l}")
    if kv_vec is None:
        def head_ep(acc, z2_t, pp_t, tgt_t, g2_, b2_):
            y, gt = _ple_out(z2_t, pp_t, acc, g2_, b2_)
            err = y - tgt_t
            dy = err * (1.0 / D)
            return dy, dy * gt, dy * pp_t * gt * (1.0 - gt), 0.5 * err * err * (1.0 / D)

        dy, d_pp, d_gpre, loss_cols = run(_mm, h2b, W[f"ple_gate{l}"], mode="nn", outs=[F32, BF16, BF16], sums=1,
                                          extras=[z2, pp, target], vecs=[g2, b2], epilogue=head_ep,
                                          ep_rows=EPILOGUE_ROWS, tm=256, name=f"ple_gate{l}")
        return t, act, z2, h2b, pp, dy, d_pp, d_gpre, loss_cols

    def x1_ep(acc, z2_t, pp_t, g2_, b2_, kg, kb):
        x1, _ = _ple_out(z2_t, pp_t, acc, g2_, b2_)
        return acc, x1, _ln(x1, kg, kb)

    gpre, x1, kvn = run(_mm, h2b, W[f"ple_gate{l}"], mode="nn", outs=[F32, F32, BF16], extras=[z2, pp],
                        vecs=[g2, b2, *kv_vec], epilogue=x1_ep, ep_rows=EPILOGUE_ROWS, tm=256, name=f"ple_gate{l}")
    return t, act, z2, h2b, pp, gpre, x1, kvn


def _mlp_ple_bwd(dy, d_pp, d_gpre, p_l, z1, h1b, t, act, z2, h2b, wts, vec, l, run, produce):
    D = z1.shape[1]
    up, down, pp_w, pg_w = wts
    g1, b1, g2, b2 = vec
    produce(f"ple_proj{l}", run(_mm, p_l, d_pp, mode="tn", outs=[BF16], layout=("col", N_CHIPS), name=f"d_ple_proj{l}"))
    produce(f"ple_gate{l}", run(_mm, h2b, d_gpre, mode="tn", outs=[BF16], layout=("row", N_CHIPS), name=f"d_ple_gate{l}"))
    def ln2_bwd(acc, dy_t, z2_t, g2_):
        dh2 = acc + dy_t
        n, rstd = _ln_norm(z2_t)
        dz2 = _ln_bwd(dh2, n, rstd, g2_)
        return dz2, dz2, dh2 * n, dh2

    dz2, dz2b, dg2, db2 = run(_mm, d_gpre, pg_w, mode="nt", outs=[F32, BF16], sums=2, extras=[dy, z2], vecs=[g2],
                              epilogue=ln2_bwd, ep_rows=EPILOGUE_ROWS, tm=256, name=f"dh2_{l}")
    produce(f"mlp_down{l}", run(_mm, t, dz2b, mode="tn", outs=[BF16], layout=("row", N_CHIPS), a_fn=_sq_relu,
                                name=f"d_mlp_down{l}"))
    dt = run(_mm, dz2b, down, mode="nt", outs=[BF16], extras=[t],
             epilogue=lambda acc, t_: (acc * 2.0 * jnp.maximum(t_.astype(F32), 0.0),), name=f"dt{l}")
    produce(f"mlp_up{l}", run(_mm, h1b, dt, mode="tn", outs=[BF16], layout=("col", N_CHIPS), name=f"d_mlp_up{l}"))
    dh1 = run(_mm, dt, up, mode="nt", outs=[F32], extras=[dz2], epilogue=lambda acc, e: (acc + ALPHA * e,),
              name=f"dh1_{l}")

    def ln1_bwd(dh1_t, z1_t, g1_):
        n, rstd = _ln_norm(z1_t)
        dz1 = _ln_bwd(dh1_t, n, rstd, g1_)
        return dz1, dz1, dh1_t * n, dh1_t

    dz1, dz1b, dg1, db1 = _rows(ln1_bwd, [dh1, z1], [g1], [(D, F32), (D, BF16)], [D, D], tm=256,
                                name=f"ln1_bwd{l}")
    return dz1, dz1b, (dg1, db1, dg2, db2)


def _ple_out(z2, pp, gpre, g2, b2):
    gt = _sigmoid(gpre)
    return _ln(z2, g2, b2) + pp * gt, gt


_GATHER_AT = {
    "conv_in_a": ("conv_w_out",),
    "conv_in_g": ("ple_gate0", "ple_proj0"),
    "conv_fwd": ("mlp_up0",),
    "mlp_up0": ("mlp_down0",),
    "mlp_down0": ("attn_w_q", "w_kv"),
    "kv_k": ("attn_w_o",),
    "kv_v": ("ple_proj1", "ple_gate1"),
    "attn_q": ("mlp_up1",),
    "mlp_up1": ("mlp_down1",),
}
_GATHER_FIRST = ("conv_w_in",)


def _local_step(x, p, cosf, sinf, target, W, V, shards=None, reducer=None, chip_arr=None):
    S, D = x.shape
    gw, gv = {}, {}
    if shards is not None:
        W = dict(_Gather(_GATHER_FIRST, shards, chip_arr).run_alone("gather_first"))

    def run(fn, *args, name, **kw):
        gather = _Gather(_GATHER_AT[name], shards, chip_arr) if (shards is not None and name in _GATHER_AT) else None
        carry = gather if reducer is None or gather is not None else reducer.carry(name)
        out = fn(*args, name=name, carry=carry, **kw)
        if gather is not None:
            W.update(gather.result)
        elif reducer is not None:
            reducer.carried()
        return out

    a_pre = run(_mm, x, W["conv_w_in"], b_sel=0, mode="nn", outs=[F32], name="conv_in_a")
    g_pre = run(_mm, x, W["conv_w_in"], b_sel=1, mode="nn", outs=[F32], name="conv_in_g")
    (glu,) = _rows(lambda a, g, ba, bg: ((a + ba) * _sigmoid(g + bg),), [a_pre, g_pre], [V["conv_b_a"], V["conv_b_g"]],
                   [(D, F32)], [], tm=256, name="glu_fwd")
    cv = run(_conv_fwd, glu, V["conv_dw"], V["conv_dw_b"], name="conv_fwd")

    def silu_ln(c, g_, b_):
        y = _ln(c, g_, b_)
        return (y * _sigmoid(y),)

    (sb,) = _rows(silu_ln, [cv], [V["conv_ln_g"], V["conv_ln_b"]], [(D, BF16)], [], tm=256, name="conv_ln_fwd")
    def z1_ep(acc, x_t, g_, b_):
        z1 = ALPHA * x_t + acc
        return z1, _ln(z1, g_, b_)

    vec0 = (V["ln1_g0"], V["ln1_b0"], V["ln2_g0"], V["ln2_b0"])
    vec1 = (V["ln1_g1"], V["ln1_b1"], V["ln2_g1"], V["ln2_b1"])
    z1_0, h1b_0 = _mm(sb, W["conv_w_out"], mode="nn", outs=[F32, BF16], extras=[x], vecs=[vec0[0], vec0[1]],
                      epilogue=z1_ep, ep_rows=EPILOGUE_ROWS, name="conv_out")
    t0, act0, z2_0, h2b_0, pp0, gpre0, x1, kvn = _mlp_ple_fwd(z1_0, h1b_0, p[0], W, vec0, 0, run,
                                                              (V["kv_ln_g"], V["kv_ln_b"]))

    rot_ep = lambda acc, c_, s_: (_rot(acc, c_, s_),)
    k_rot = run(_mm, kvn, W["w_kv"], b_sel=0, mode="nn", outs=[BF16], rextras=[cosf, sinf], epilogue=rot_ep, name="kv_k")
    v_b = run(_mm, kvn, W["w_kv"], b_sel=1, mode="nn", outs=[BF16], name="kv_v")
    q_rot = run(_mm, x1, W["attn_w_q"], mode="nn", outs=[BF16], rextras=[cosf, sinf], epilogue=rot_ep, name="attn_q")
    og, lg = [], []
    for g, dil in enumerate(DILATIONS):
        o_g, l_g = _attn_fwd(q_rot, k_rot, v_b, g, dil)
        og.append(o_g)
        lg.append(l_g)

    def merge(o0, o1, o2, l0, l1, l2):
        m = jnp.maximum(jnp.maximum(l0, l1), l2)
        e = [jnp.exp(l0 - m), jnp.exp(l1 - m), jnp.exp(l2 - m)]
        den = e[0] + e[1] + e[2]
        w = [e_g / den for e_g in e]
        o = _per_head(o0, lambda oh, h: sum(w[g][:, h:h + 1] * (o0, o1, o2)[g][:, h * HEAD_DIM:(h + 1) * HEAD_DIM].astype(F32)
                                            for g in range(N_GROUPS)))
        return o, m + jnp.log(den)

    ob, lse = _rows(merge, og + lg, [], [(D, BF16), (STAT_LANES, F32)], [], tm=256, name="attn_merge")
    z1_1, h1b_1 = _mm(ob, W["attn_w_o"], mode="nn", outs=[F32, BF16], extras=[x1], vecs=[vec1[0], vec1[1]],
                      epilogue=z1_ep, ep_rows=EPILOGUE_ROWS, name="attn_out")
    t1, act1, z2_1, h2b_1, pp1, dy1, d_pp1, d_gpre1, loss_cols = _mlp_ple_fwd(z1_1, h1b_1, p[1], W, vec1, 1, run,
                                                                              target=target)
    wts0 = (W["mlp_up0"], W["mlp_down0"], W["ple_proj0"], W["ple_gate0"])
    wts1 = (W["mlp_up1"], W["mlp_down1"], W["ple_proj1"], W["ple_gate1"])


    def produce(name, grad):
        gw[name] = grad
        if reducer is not None:
            reducer.produced(name, grad)

    dz1_1, dz1b_1, (gv["ln1_g1"], gv["ln1_b1"], gv["ln2_g1"], gv["ln2_b1"]) = _mlp_ple_bwd(
        dy1, d_pp1, d_gpre1, p[1], z1_1, h1b_1, t1, act1, z2_1, h2b_1, wts1, vec1, 1, run, produce)
    produce("attn_w_o", run(_mm, ob, dz1b_1, mode="tn", outs=[BF16], layout=("row", N_CHIPS), name="d_attn_w_o"))

    do_b = run(_mm, dz1b_1, W["attn_w_o"], mode="nt", outs=[BF16], name="attn_do")

    def delta(do_t, o_t):
        prod = do_t.astype(F32) * o_t.astype(F32)
        H = D // HEAD_DIM
        return (_per_head_pack([jnp.sum(prod[:, h * HEAD_DIM:(h + 1) * HEAD_DIM], axis=1, keepdims=True) for h in range(H)]),)

    (dlt,) = _rows(delta, [do_b, ob], [], [(STAT_LANES, F32)], [], tm=256, name="attn_delta")
    dqs, dks, dvs = [], [], []
    for g, dil in enumerate(DILATIONS):
        dq_g, dk_g, dv_g = run(_attn_bwd, q_rot, k_rot, v_b, do_b, lse, dlt, g, dil, name=f"attn_bwd_g{g}")
        dqs.append(dq_g)
        dks.append(dk_g)
        dvs.append(dv_g)

    def unrot(q0, q1, q2, k0, k1, k2, v0, v1, v2, c_, s_):
        dq = jnp.concatenate([_rot_t(t_.astype(F32), c_, s_) for t_ in (q0, q1, q2)], axis=1)
        f = lambda t_: t_.astype(F32)
        return dq, _rot_t(f(k0) + f(k1) + f(k2), c_, s_), f(v0) + f(v1) + f(v2)

    dq, dk, dv = run(_rows, unrot, dqs + dks + dvs + [cosf, sinf], [], [(N_GROUPS * D, BF16), (D, BF16), (D, BF16)], [],
                     tm=128, name="attn_unrot")
    produce("attn_w_q", run(_mm, x1, dq, mode="tn", outs=[BF16], layout=("col", N_CHIPS), name="d_attn_w_q"))
    dx1_q = run(_mm, dq, W["attn_w_q"], mode="nt", outs=[F32], extras=[dz1_1],
                epilogue=lambda acc, e: (acc + ALPHA * e,), name="dx1_q")
    produce("w_kv", jnp.concatenate(
        [run(_mm, kvn, dk, mode="tn", outs=[BF16], layout=("col", 2), name="d_w_kv_k"),
         run(_mm, kvn, dv, mode="tn", outs=[BF16], layout=("col", 2), name="d_w_kv_v")], axis=1))
    dkvn_k = run(_mm, dk, W["w_kv"], b_sel=0, mode="nt", outs=[F32], name="dkvn_k")
    dkvn = run(_mm, dv, W["w_kv"], b_sel=1, mode="nt", outs=[F32], extras=[dkvn_k], epilogue=lambda acc, e: (acc + e,),
               name="dkvn_v")

    def x1_bwd(dx1q_t, dkvn_t, x1_t, pp, gpre, kg):
        n, rstd = _ln_norm(x1_t)
        dy = dx1q_t + _ln_bwd(dkvn_t, n, rstd, kg)
        gt = _sigmoid(gpre)
        return dy, dy * gt, dy * pp * gt * (1.0 - gt), dkvn_t * n, dkvn_t

    dy0, d_pp0, d_gpre0, gv["kv_ln_g"], gv["kv_ln_b"] = _rows(
        x1_bwd, [dx1_q, dkvn, x1, pp0, gpre0], [V["kv_ln_g"]], [(D, F32), (D, BF16), (D, BF16)], [D, D], tm=256,
        name="x1_bwd")

    dz1_0, dz1b_0, (gv["ln1_g0"], gv["ln1_b0"], gv["ln2_g0"], gv["ln2_b0"]) = _mlp_ple_bwd(
        dy0, d_pp0, d_gpre0, p[0], z1_0, h1b_0, t0, act0, z2_0, h2b_0, wts0, vec0, 0, run, produce)
    produce("conv_w_out", run(_mm, sb, dz1b_0, mode="tn", outs=[BF16], layout=("row", N_CHIPS), name="d_conv_w_out"))
    ds = run(_mm, dz1b_0, W["conv_w_out"], mode="nt", outs=[F32], name="conv_ds")

    def conv_ln_bwd(ds_t, c_t, g_, b_):
        n, rstd = _ln_norm(c_t)
        y = n * g_ + b_
        sg = _sigmoid(y)
        dln = ds_t * sg * (1.0 + y * (1.0 - sg))
        dc = _ln_bwd(dln, n, rstd, g_)
        return dc, dln * n, dln, dc

    dc, gv["conv_ln_g"], gv["conv_ln_b"], gv["conv_dw_b"] = _rows(
        conv_ln_bwd, [ds, cv], [V["conv_ln_g"], V["conv_ln_b"]], [(D, F32)], [D, D, D], tm=256, name="conv_ln_bwd")
    da, dg, gv["conv_dw"], gv["conv_b_a"], gv["conv_b_g"] = run(
        _conv_bwd, dc, glu, a_pre, g_pre, V["conv_dw"], V["conv_b_a"], V["conv_b_g"], name="conv_bwd")
    produce("conv_w_in", jnp.concatenate(
        [run(_mm, x, da, mode="tn", outs=[BF16], layout=("col", 2), name="d_conv_w_in_a"),
         run(_mm, x, dg, mode="tn", outs=[BF16], layout=("col", 2), name="d_conv_w_in_g")], axis=1))
    dx_a = run(_mm, da, W["conv_w_in"], b_sel=0, mode="nt", outs=[F32], extras=[dz1_0],
               epilogue=lambda acc, e: (acc + ALPHA * e,), name="dx_a")
    grad_x = run(_mm, dg, W["conv_w_in"], b_sel=1, mode="nt", outs=[F32], extras=[dx_a],
                 epilogue=lambda acc, e: (acc + e,), name="dx_g")
    if reducer is not None:
        reducer.carry("share_last").run_alone("share_last")
        reducer.carried()
    return loss_cols, grad_x, gw, gv


def _place():
    x, y, c = lax.axis_index("x"), lax.axis_index("y"), lax.axis_index("c")
    chips = [(1 - x, y), (x, 1 - y), (1 - x, 1 - y)]
    return x, y, c, chips


def _remote(src, dst, ssem, rsem, dev):
    return pltpu.make_async_remote_copy(src_ref=src, dst_ref=dst, send_sem=ssem, recv_sem=rsem, device_id=dev,
                                        device_id_type=MESH)


def _allgather8(block, name):
    R, C = block.shape

    def body(x_ref, out_ref, send_sems, recv_sems, local_sem):
        x, y, c, chips = _place()
        me, sibling = (x, y, c), (x, y, 1 - c)

        def slot(px, py, pc):
            return out_ref.at[4 * px + 2 * py + pc]

        def copy(k, blockpos, to, src=None):
            return _remote(slot(*blockpos) if src is None else src, slot(*blockpos), send_sems.at[k], recv_sems.at[k], to)

        mine = pltpu.make_async_copy(x_ref, slot(*me), local_sem)
        mine.start()
        first = [copy(0, me, sibling, src=x_ref)]
        first += [copy(1 + j, me, (*chip, c), src=x_ref) for j, chip in enumerate(chips)]
        for cp in first:
            cp.start()
        passed = [copy(4 + j, (*chip, c), sibling) for j, chip in enumerate(chips)]
        for j, chip in enumerate(chips):
            copy(1 + j, (*chip, c), me).wait_recv()
            passed[j].start()
        copy(0, sibling, me).wait_recv()
        for j, chip in enumerate(chips):
            copy(4 + j, (*chip, 1 - c), me).wait_recv()
        for cp in first + passed:
            cp.wait_send()
        mine.wait()

    return pl.pallas_call(
        body, name=name, out_shape=jax.ShapeDtypeStruct((8, R, C), block.dtype),
        in_specs=[pl.BlockSpec(memory_space=pltpu.VMEM)], out_specs=pl.BlockSpec(memory_space=pltpu.VMEM),
        scratch_shapes=[pltpu.SemaphoreType.DMA((7,)), pltpu.SemaphoreType.DMA((7,)), pltpu.SemaphoreType.DMA],
        compiler_params=pltpu.CompilerParams(vmem_limit_bytes=_vmem(10 * _nbytes((R, C), block.dtype))),
    )(block)


_MATS = (
    ("conv_w_in", "conv_w_in", 0, "col", True),
    ("conv_w_out", "conv_w_out", 0, "row", False),
    ("mlp_up0", "mlp_up", 0, "col", False),
    ("mlp_down0", "mlp_down", 0, "row", False),
    ("ple_proj0", "ple_proj", 0, "col", False),
    ("ple_gate0", "ple_gate", 0, "row", False),
    ("w_kv", "w_kv", None, "col", True),
    ("attn_w_q", "attn_w_q", 0, "col", False),
    ("attn_w_o", "attn_w_o", 0, "row", False),
    ("mlp_up1", "mlp_up", 1, "col", False),
    ("mlp_down1", "mlp_down", 1, "row", False),
    ("ple_proj1", "ple_proj", 1, "col", False),
    ("ple_gate1", "ple_gate", 1, "row", False),
)


class _Carry:
    result = None
    aliases = {}

    def set_result(self, outs):
        self.result = dict(zip(self.names, outs))

    def run_alone(self, name):
        n_in, n_out = len(self.ins), len(self.out_shape)

        def body(*refs):
            in_refs, out_refs, sems = refs[:n_in], refs[n_in:n_in + n_out], refs[n_in + n_out:]
            self.start(in_refs, out_refs, sems)
            self.finish(in_refs, out_refs, sems)

        outs = pl.pallas_call(body, name=name, out_shape=self.out_shape, in_specs=[ANY] * n_in, out_specs=[ANY] * n_out,
                              scratch_shapes=self.scratch, input_output_aliases=dict(self.aliases))(*self.ins)
        self.set_result(outs)
        return self.result


class _Gather(_Carry):
    def __init__(self, names, shards, chip_arr):
        mats = [m for m in _MATS if m[0] in names]
        srcs = sorted({m[1] for m in mats})
        self.names = [m[0] for m in mats]
        self.out_shape, self.geo, placed = [], [], []
        for name, src, layer, kind, split in mats:
            s = shards[src]
            ks, ns = s.shape[-2:]
            K, N = (ks, ns * N_CHIPS) if kind == "col" else (ks * N_CHIPS, ns)
            self.out_shape.append(jax.ShapeDtypeStruct((2, K, N // 2) if split else (K, N), BF16))
            self.geo.append((srcs.index(src), layer if s.ndim == 3 else None, kind, split, K, N))
            placed.append(_place_shard(s, layer if s.ndim == 3 else None, kind, split, chip_arr, f"place_{name}"))
        T = len(mats)
        self.ins = [shards[n] for n in srcs] + placed
        self.aliases = {len(srcs) + t: t for t in range(T)}
        self.scratch = [pltpu.SemaphoreType.DMA((3 * T,)) for _ in range(4)]
        self.result = None

    def _copies(self, in_refs, out_refs, sems):
        geo, T = self.geo, len(self.geo)
        s_ici, r_ici, s_d2d, r_d2d = sems
        x, y, c, chips = _place()
        me = 2 * x + y
        sibling = (x, y, 1 - c)
        idx = [2 * cx + cy for cx, cy in chips]

        def src_ref(t):
            i, layer, _, _, _, _ = geo[t]
            return in_refs[i] if layer is None else in_refs[i].at[layer]

        def src_half(t, h):
            _, _, kind, _, K, N = geo[t]
            if kind == "col":
                return src_ref(t).at[pl.ds(h * (K // 2), K // 2), :]
            return src_ref(t).at[:, pl.ds(h * (N // 2), N // 2)]

        def dst(t, j, h):
            _, _, kind, split, K, N = geo[t]
            n, k = N // N_CHIPS, K // N_CHIPS
            if kind == "col":
                rows = slice(None) if h is None else pl.ds(h * (K // 2), K // 2)
                if split:
                    return out_refs[t].at[j // 2, rows, pl.ds((j % 2) * n, n)]
                return out_refs[t].at[rows, pl.ds(j * n, n)]
            cols = slice(None) if h is None else pl.ds(h * (N // 2), N // 2)
            return out_refs[t].at[pl.ds(j * k, k), cols]

        sends = [_remote(src_half(t, c), dst(t, me, c), s_ici.at[3 * t + kk], r_ici.at[3 * t + kk], (*chips[kk], c))
                 for t in range(T) for kk in range(3)]
        hops = []
        for t in range(T):
            for kk in range(3):
                mine, theirs = dst(t, idx[kk], c), dst(t, idx[kk], 1 - c)
                hops.append((_remote(mine, mine, s_ici.at[3 * t + kk], r_ici.at[3 * t + kk], sibling),
                             _remote(mine, mine, s_d2d.at[3 * t + kk], r_d2d.at[3 * t + kk], sibling),
                             _remote(theirs, theirs, s_d2d.at[3 * t + kk], r_d2d.at[3 * t + kk], sibling)))
        return sends, hops

    def start(self, in_refs, out_refs, sems):
        for cp in self._copies(in_refs, out_refs, sems)[0]:
            cp.start()

    def finish(self, in_refs, out_refs, sems):
        sends, hops = self._copies(in_refs, out_refs, sems)
        for landed, forward, _ in hops:
            landed.wait_recv()
            forward.start()
        for _, _, from_sibling in hops:
            from_sibling.wait_recv()
        for cp in sends + [h[1] for h in hops]:
            cp.wait_send()


def _place_shard(shard, layer, kind, split, chip_arr, name):
    ks, ns = shard.shape[-2:]
    K, N = (ks, ns * N_CHIPS) if kind == "col" else (ks * N_CHIPS, ns)
    tr = _fit(256, ks)
    nb = ks // tr
    if shard.ndim == 3:
        in_spec = pl.BlockSpec((None, tr, ns), lambda i, me: (layer, i, 0))
    else:
        in_spec = pl.BlockSpec((tr, ns), lambda i, me: (i, 0))
    if kind == "row":
        out_shape, out_spec = (K, N), pl.BlockSpec((tr, ns), lambda i, me: (me[0] * nb + i, 0))
    elif split:
        out_shape, out_spec = (2, K, N // 2), pl.BlockSpec((None, tr, ns), lambda i, me: (me[0] // 2, i, me[0] % 2))
    else:
        out_shape, out_spec = (K, N), pl.BlockSpec((tr, ns), lambda i, me: (i, me[0]))

    def body(me_ref, s_ref, o_ref):
        o_ref[...] = s_ref[...]

    return pl.pallas_call(
        body, name=name, out_shape=jax.ShapeDtypeStruct(out_shape, BF16),
        grid_spec=pltpu.PrefetchScalarGridSpec(num_scalar_prefetch=1, grid=(nb,), in_specs=[in_spec], out_specs=out_spec),
        compiler_params=pltpu.CompilerParams(dimension_semantics=("parallel",), vmem_limit_bytes=_vmem(4 * tr * ns * 2)),
    )(chip_arr, shard)


class _Multi(_Carry):
    def __init__(self, parts):
        self.parts = parts
        self.ins = [a for p in parts for a in p.ins]
        self.out_shape = [a for p in parts for a in p.out_shape]
        self.scratch = [a for p in parts for a in p.scratch]
        self.aliases, n_in, n_out = {}, 0, 0
        for p in parts:
            self.aliases.update({n_in + i: n_out + o for i, o in p.aliases.items()})
            n_in, n_out = n_in + len(p.ins), n_out + len(p.out_shape)

    def _split(self, seq, field):
        out, at = [], 0
        for p in self.parts:
            n = len(getattr(p, field))
            out.append(seq[at:at + n])
            at += n
        return out

    def _each(self, method, in_refs, out_refs, sems):
        for p, i, o, s in zip(self.parts, self._split(in_refs, "ins"), self._split(out_refs, "out_shape"),
                              self._split(sems, "scratch")):
            getattr(p, method)(i, o, s)

    def start(self, in_refs, out_refs, sems):
        self._each("start", in_refs, out_refs, sems)

    def finish(self, in_refs, out_refs, sems):
        self._each("finish", in_refs, out_refs, sems)

    def set_result(self, outs):
        for p, o in zip(self.parts, self._split(list(outs), "out_shape")):
            p.set_result(o)


class _PairSend(_Carry):
    def __init__(self, grads):
        self.names = list(grads)
        self.ins = [grads[n] for n in self.names]
        self.out_shape = [jax.ShapeDtypeStruct(a.shape[1:], BF16) for a in self.ins]
        T = len(self.names)
        self.scratch = [pltpu.SemaphoreType.DMA((T,)), pltpu.SemaphoreType.DMA((T,))]

    def _copies(self, in_refs, out_refs, sems):
        x, y, c, _ = _place()
        return [_remote(in_refs[t].at[1 - c], out_refs[t], sems[0].at[t], sems[1].at[t], (x, y, 1 - c))
                for t in range(len(self.names))]

    def start(self, in_refs, out_refs, sems):
        for cp in self._copies(in_refs, out_refs, sems):
            cp.start()

    def finish(self, in_refs, out_refs, sems):
        for cp in self._copies(in_refs, out_refs, sems):
            cp.wait()


class _ChipScatter(_Carry):
    def __init__(self, sums):
        self.names = list(sums)
        T = len(self.names)
        self.ins = [sums[n][0] for n in self.names] + [sums[n][1] for n in self.names]
        self.out_shape = [jax.ShapeDtypeStruct(a.shape, BF16) for a in self.ins[:T]]
        self.aliases = {T + t: t for t in range(T)}
        self.scratch = [pltpu.SemaphoreType.DMA((3 * T,)), pltpu.SemaphoreType.DMA((3 * T,))]

    def _copies(self, in_refs, out_refs, sems):
        ssem, rsem = sems
        x, y, c, chips = _place()
        me = 2 * x + y
        idx = [2 * cx + cy for cx, cy in chips]
        T = len(self.names)
        sends = [_remote(in_refs[t].at[idx[kk]], out_refs[t].at[me], ssem.at[3 * t + kk], rsem.at[3 * t + kk],
                         (*chips[kk], c)) for t in range(T) for kk in range(3)]
        lands = [_remote(out_refs[t].at[idx[kk]], out_refs[t].at[idx[kk]], ssem.at[3 * t + kk], rsem.at[3 * t + kk],
                         (*chips[kk], c)) for t in range(T) for kk in range(3)]
        return sends, lands

    def start(self, in_refs, out_refs, sems):
        for cp in self._copies(in_refs, out_refs, sems)[0]:
            cp.start()

    def finish(self, in_refs, out_refs, sems):
        sends, lands = self._copies(in_refs, out_refs, sems)
        for cp in lands:
            cp.wait_recv()
        for cp in sends:
            cp.wait_send()


class _PairShare(_Carry):
    def __init__(self, halves):
        self.names = list(halves)
        self.ins = [halves[n] for n in self.names]
        self.out_shape = [jax.ShapeDtypeStruct(a.shape, F32) for a in self.ins]
        T = len(self.names)
        self.aliases = {t: t for t in range(T)}
        self.scratch = [pltpu.SemaphoreType.DMA((T,)), pltpu.SemaphoreType.DMA((T,))]

    def _copies(self, in_refs, out_refs, sems):
        ssem, rsem = sems
        x, y, c, _ = _place()
        sibling = (x, y, 1 - c)
        T = len(self.names)
        sends = [_remote(out_refs[t].at[c], out_refs[t].at[c], ssem.at[t], rsem.at[t], sibling) for t in range(T)]
        lands = [_remote(out_refs[t].at[1 - c], out_refs[t].at[1 - c], ssem.at[t], rsem.at[t], sibling) for t in range(T)]
        return sends, lands

    def start(self, in_refs, out_refs, sems):
        for cp in self._copies(in_refs, out_refs, sems)[0]:
            cp.start()

    def finish(self, in_refs, out_refs, sems):
        sends, lands = self._copies(in_refs, out_refs, sems)
        for cp in lands:
            cp.wait_recv()
        for cp in sends:
            cp.wait_send()


def _pair_sum(own, landed, c_arr, name):
    _, ns, r, cc = own.shape
    rows = ns * r
    tr = _fit(512, rows)

    def body(c_ref, a_ref, b_ref, o_ref, o2_ref):
        total = (a_ref[...].astype(F32) + b_ref[...].astype(F32)).astype(o_ref.dtype)
        o_ref[...] = total
        o2_ref[...] = total

    tile = pl.BlockSpec((tr, cc), lambda i, c_ref: (i, 0))
    out = pl.pallas_call(
        body, name=name, out_shape=[jax.ShapeDtypeStruct((rows, cc), BF16)] * 2,
        grid_spec=pltpu.PrefetchScalarGridSpec(
            num_scalar_prefetch=1, grid=(rows // tr,),
            in_specs=[pl.BlockSpec((None, tr, cc), lambda i, c_ref: (c_ref[0], i, 0)), tile], out_specs=[tile, tile]),
        compiler_params=pltpu.CompilerParams(dimension_semantics=("parallel",), vmem_limit_bytes=_vmem(8 * tr * cc * 4)),
    )(c_arr, own.reshape(2, rows, cc), landed.reshape(rows, cc))
    return out[0].reshape(ns, r, cc), out[1].reshape(ns, r, cc)


def _chip_sum(parts, c_arr, name):
    _, r, cc = parts.shape
    tr = _fit(256, r)

    def body(c_ref, p_ref, o_ref):
        acc = p_ref[0].astype(F32)
        for j in range(1, N_CHIPS):
            acc = acc + p_ref[j].astype(F32)
        o_ref[...] = acc

    return pl.pallas_call(
        body, name=name, out_shape=jax.ShapeDtypeStruct((2, r, cc), F32),
        grid_spec=pltpu.PrefetchScalarGridSpec(
            num_scalar_prefetch=1, grid=(r // tr,),
            in_specs=[pl.BlockSpec((N_CHIPS, tr, cc), lambda i, c_ref: (0, i, 0))],
            out_specs=pl.BlockSpec((None, tr, cc), lambda i, c_ref: (c_ref[0], i, 0))),
        compiler_params=pltpu.CompilerParams(dimension_semantics=("parallel",), vmem_limit_bytes=_vmem(12 * tr * cc * 4)),
    )(c_arr, parts)


def _adamw_math(w, g, m, v):
    m2 = ADAM_B1 * m + (1.0 - ADAM_B1) * g
    v2 = ADAM_B2 * v + (1.0 - ADAM_B2) * jnp.square(g)
    m_hat = m2 / (1.0 - ADAM_B1 ** ADAM_STEP)
    v_hat = v2 / (1.0 - ADAM_B2 ** ADAM_STEP)
    delta = -ADAM_LR * (m_hat / (jnp.sqrt(v_hat) + ADAM_EPS) + ADAM_WD * w)
    return delta, m2, v2


def _adamw_mat(g2, w, m, v, layer, kind, prev, name):
    shape = w.shape
    ks, ns = shape[-2:]
    _, r, cc = g2.shape
    tr, tc = _fit(256, r), _fit(1024, cc)
    assert (r, cc) == ((ks // 2, ns) if kind == "col" else (ks, ns // 2))
    assert r % tr == 0 and cc % tc == 0
    rb, cb = r // tr, cc // tc
    if kind == "col":
        g_spec = pl.BlockSpec((None, tr, tc), lambda i, j: (i // rb, i % rb, j))
    else:
        g_spec = pl.BlockSpec((None, tr, tc), lambda i, j: (j // cb, i, j % cb))
    if w.ndim == 3:
        w_spec = pl.BlockSpec((None, tr, tc), lambda i, j: (layer, i, j))
    else:
        w_spec = pl.BlockSpec((tr, tc), lambda i, j: (i, j))
    n_prev = 0 if prev is None else 4

    def body(*refs):
        g_ref, w_ref, m_ref, v_ref = refs[:4]
        go_ref, d_ref, mo_ref, vo_ref = refs[4 + n_prev:]
        g = g_ref[...]
        delta, m2, v2 = _adamw_math(w_ref[...], g, m_ref[...], v_ref[...])
        go_ref[...] = g
        d_ref[...] = delta
        mo_ref[...] = m2
        vo_ref[...] = v2

    return pl.pallas_call(
        body, name=name, grid=(ks // tr, ns // tc),
        in_specs=[g_spec, w_spec, w_spec, w_spec] + [ANY] * n_prev, out_specs=[w_spec] * 4,
        out_shape=[jax.ShapeDtypeStruct(shape, F32)] * 4,
        input_output_aliases={4 + i: i for i in range(n_prev)},
        compiler_params=pltpu.CompilerParams(dimension_semantics=("parallel", "parallel"),
                                             vmem_limit_bytes=_vmem(16 * tr * tc * 4)),
    )(g2, w, m, v, *(prev or ()))


def _adamw_small(g, w, m, v, name):
    def body(g_ref, w_ref, m_ref, v_ref, d_ref, mo_ref, vo_ref):
        delta, m2, v2 = _adamw_math(w_ref[...], g_ref[...], m_ref[...], v_ref[...])
        d_ref[...] = delta
        mo_ref[...] = m2
        vo_ref[...] = v2

    return pl.pallas_call(body, name=name, out_shape=[jax.ShapeDtypeStruct(w.shape, F32)] * 3)(g, w, m, v)


def _sum8(parts, name):
    def body(p_ref, o_ref):
        acc = p_ref[0]
        for j in range(1, 8):
            acc = acc + p_ref[j]
        o_ref[...] = acc

    return pl.pallas_call(body, name=name, out_shape=jax.ShapeDtypeStruct(parts.shape[1:], F32),
                          compiler_params=pltpu.CompilerParams(vmem_limit_bytes=_vmem(12 * _nbytes(parts.shape[1:], F32))))(parts)


_REDUCE_AT = {
    "d_ple_gate1": (("A", "ple_proj1"),),
    "dh2_1": (("A", "ple_gate1"),),
    "d_mlp_down1": (("B", "ple_proj1"), ("B", "ple_gate1")),
    "dt1": (("A", "mlp_down1"),),
    "d_mlp_up1": (("B", "mlp_down1"), ("C", "ple_proj1"), ("C", "ple_gate1")),
    "dh1_1": (("A", "mlp_up1"),),
    "d_attn_w_o": (("C", "mlp_down1"),),
    "attn_do": (("A", "attn_w_o"),),
    "attn_bwd_g0": (("B", "attn_w_o"),),
    "attn_unrot": (("C", "attn_w_o"),),
    "dx1_q": (("A", "attn_w_q"),),
    "dkvn_k": (("A", "w_kv"),),
    "d_ple_gate0": (("A", "ple_proj0"),),
    "dh2_0": (("A", "ple_gate0"),),
    "d_mlp_down0": (("B", "mlp_up1"), ("B", "ple_proj0")),
    "dt0": (("B", "w_kv"), ("B", "ple_gate0"), ("A", "mlp_down0")),
    "d_mlp_up0": (("B", "mlp_down0"), ("C", "mlp_up1"), ("C", "ple_proj0"), ("C", "w_kv"), ("C", "ple_gate0")),
    "dh1_0": (("A", "mlp_up0"), ("B", "attn_w_q")),
    "d_conv_w_out": (("C", "mlp_down0"), ("C", "attn_w_q")),
    "conv_ds": (("A", "conv_w_out"),),
    "conv_bwd": (("B", "mlp_up0"), ("B", "conv_w_out")),
    "d_conv_w_in_g": (("C", "mlp_up0"), ("C", "conv_w_out")),
    "dx_a": (("A", "conv_w_in"),),
    "dx_g": (("B", "conv_w_in"),),
    "share_last": (("C", "conv_w_in"),),
}


class _Reducer:
    def __init__(self, w, mom, var, c_arr):
        self.w, self.mom, self.var, self.c_arr = w, mom, var, c_arr
        self.mats = {m[0]: m for m in _MATS}
        self.grads, self.pair_sums, self.chip_sums, self.out = {}, {}, {}, {}

    def produced(self, name, grad):
        self.grads[name] = grad

    def carry(self, call):
        parts = []
        for cls, stage, src in ((_PairSend, "A", self.grads), (_ChipScatter, "B", self.pair_sums),
                                (_PairShare, "C", self.chip_sums)):
            names = [n for s, n in _REDUCE_AT.get(call, ()) if s == stage]
            if names:
                parts.append((stage, cls({n: src[n] for n in names})))
        self._parts = parts
        return _Multi([p for _, p in parts]) if parts else None

    def carried(self):
        for stage, part in self._parts:
            for name, val in part.result.items():
                if stage == "A":
                    self.pair_sums[name] = _pair_sum(self.grads[name], val, self.c_arr, f"pair_sum_{name}")
                elif stage == "B":
                    self.chip_sums[name] = _chip_sum(val, self.c_arr, f"chip_sum_{name}")
                else:
                    _, src, layer, kind, _ = self.mats[name]
                    self.out[src] = _adamw_mat(val, self.w[src], self.mom[src], self.var[src], layer or 0, kind,
                                               self.out.get(src), f"adamw_{name}")
        self._parts = []


_WEIGHTS = ("conv_w_in", "conv_b_in", "conv_dw", "conv_dw_b", "conv_ln_g", "conv_ln_b", "conv_w_out", "kv_ln_g",
            "kv_ln_b", "w_kv", "attn_w_q", "attn_w_o", "ln1_g", "ln1_b", "mlp_up", "mlp_down", "ln2_g", "ln2_b",
            "ple_proj", "ple_gate")
_SHARDED_VECS = ("conv_b_in", "conv_dw", "conv_dw_b", "conv_ln_g", "conv_ln_b")
_REPLICATED_VECS = ("kv_ln_g", "kv_ln_b", "ln1_g", "ln1_b", "ln2_g", "ln2_b")


def _pad_rows(a, rows):
    return jnp.concatenate([a, jnp.zeros((rows - a.shape[0], a.shape[1]), a.dtype)], axis=0) if a.shape[0] < rows else a


def _pack_sharded(d):
    n = d["conv_dw_b"].shape[-1]
    rows = [d["conv_b_in"].reshape(2, n), d["conv_dw"].reshape(CONV_WIDTH, n), d["conv_dw_b"].reshape(1, n),
            d["conv_ln_g"].reshape(1, n), d["conv_ln_b"].reshape(1, n)]
    return _pad_rows(jnp.concatenate(rows, axis=0), 40)


def _unpack_sharded(pack, like):
    n = pack.shape[1]
    return {"conv_b_in": pack[0:2].reshape(like["conv_b_in"].shape),
            "conv_dw": pack[2:2 + CONV_WIDTH].reshape(like["conv_dw"].shape),
            "conv_dw_b": pack[33:34].reshape(like["conv_dw_b"].shape),
            "conv_ln_g": pack[34:35].reshape(like["conv_ln_g"].shape),
            "conv_ln_b": pack[35:36].reshape(like["conv_ln_b"].shape)}


def _pack_replicated(d):
    D = d["kv_ln_g"].shape[-1]
    rows = [d[n].reshape(-1, D) for n in _REPLICATED_VECS]
    return _pad_rows(jnp.concatenate(rows, axis=0), 16)


def _unpack_replicated(pack, like):
    out, r = {}, 0
    for n in _REPLICATED_VECS:
        k = like[n].size // pack.shape[1]
        out[n] = pack[r:r + k].reshape(like[n].shape)
        r += k
    return out


def kernel(x, p, positions, conv_w_in, conv_b_in, conv_dw, conv_dw_b, conv_ln_g, conv_ln_b, conv_w_out, kv_ln_g, kv_ln_b, w_kv, attn_w_q, attn_w_o, ln1_g, ln1_b, mlp_up, mlp_down, ln2_g, ln2_b, ple_proj, ple_gate, loss_target, m_conv_w_in, m_conv_b_in, m_conv_dw, m_conv_dw_b, m_conv_ln_g, m_conv_ln_b, m_conv_w_out, m_kv_ln_g, m_kv_ln_b, m_w_kv, m_attn_w_q, m_attn_w_o, m_ln1_g, m_ln1_b, m_mlp_up, m_mlp_down, m_ln2_g, m_ln2_b, m_ple_proj, m_ple_gate, v_conv_w_in, v_conv_b_in, v_conv_dw, v_conv_dw_b, v_conv_ln_g, v_conv_ln_b, v_conv_w_out, v_kv_ln_g, v_kv_ln_b, v_w_kv, v_attn_w_q, v_attn_w_o, v_ln1_g, v_ln1_b, v_mlp_up, v_mlp_down, v_ln2_g, v_ln2_b, v_ple_proj, v_ple_gate):
    args = dict(locals())
    w = {n: args[n] for n in _WEIGHTS}
    mom = {n: args["m_" + n] for n in _WEIGHTS}
    var = {n: args["v_" + n] for n in _WEIGHTS}
    S, D = x.shape[1:]
    n4 = D // N_CHIPS
    chip = 2 * lax.axis_index("x") + lax.axis_index("y")
    c_arr = lax.axis_index("c").astype(jnp.int32).reshape(1)

    shards = {n: w[n].astype(BF16) for n in sorted({m[1] for m in _MATS})}
    vec_all = _allgather8(_pack_sharded(w), "gather_vectors")
    vec_full = jnp.concatenate([vec_all[2 * j] for j in range(N_CHIPS)], axis=1)
    b_in = vec_all[0::2, 0:2, :].reshape(1, 2 * D)
    V = {"conv_b_a": b_in[:, :D], "conv_b_g": b_in[:, D:],
         "conv_dw": _pad_rows(vec_full[2:2 + CONV_WIDTH], CONV_PAD), "conv_dw_b": vec_full[33:34],
         "conv_ln_g": vec_full[34:35], "conv_ln_b": vec_full[35:36],
         "kv_ln_g": kv_ln_g.reshape(1, D), "kv_ln_b": kv_ln_b.reshape(1, D)}
    for l in range(2):
        for n in ("ln1_g", "ln1_b", "ln2_g", "ln2_b"):
            V[f"{n}{l}"] = w[n][l].reshape(1, D)

    half = HEAD_DIM // 2
    inv_freq = ROPE_THETA ** (-jnp.arange(half, dtype=F32) * (2.0 / HEAD_DIM))
    ang = positions[0].astype(F32)[:, None] * inv_freq
    cos, sin = jnp.cos(ang), jnp.sin(ang)
    cosf = jnp.concatenate([cos, cos], axis=-1)
    sinf = jnp.concatenate([-sin, sin], axis=-1)

    reducer = _Reducer(w, mom, var, c_arr)
    loss_cols, grad_x, _, gv = _local_step(x[0], p[:, 0], cosf, sinf, loss_target[0], None, V, shards, reducer,
                                           chip.astype(jnp.int32).reshape(1))
    loss = lax.psum(jnp.sum(loss_cols), ("x", "y", "c"))
    out = dict(reducer.out)

    gpack = jnp.concatenate([gv["conv_b_a"], gv["conv_b_g"], gv["conv_dw"][:CONV_WIDTH], gv["conv_dw_b"],
                             gv["conv_ln_g"], gv["conv_ln_b"], gv["kv_ln_g"], gv["kv_ln_b"],
                             gv["ln1_g0"], gv["ln1_g1"], gv["ln1_b0"], gv["ln1_b1"],
                             gv["ln2_g0"], gv["ln2_g1"], gv["ln2_b0"], gv["ln2_b1"]], axis=0)
    gsum = _sum8(_allgather8(_pad_rows(gpack, 48), "gather_vector_grads"), "sum_vector_grads")
    g_b = lax.dynamic_slice_in_dim(jnp.concatenate([gsum[0:1], gsum[1:2]], axis=1), chip * 2 * n4, 2 * n4, axis=1)
    g_sh = lax.dynamic_slice_in_dim(gsum[2:36], chip * n4, n4, axis=1)
    g_sh = _pad_rows(jnp.concatenate([g_b.reshape(2, n4), g_sh], axis=0), 40)
    d_sh, m_sh, v_sh = _adamw_small(g_sh, _pack_sharded(w), _pack_sharded(mom), _pack_sharded(var), "adamw_sharded_vectors")
    g_rep = _pad_rows(gsum[36:46], 16)
    d_rep, m_rep, v_rep = _adamw_small(g_rep, _pack_replicated(w), _pack_replicated(mom), _pack_replicated(var),
                                       "adamw_replicated_vectors")
    small = {}
    for i, (sh, rep) in enumerate(((g_sh, g_rep), (d_sh, d_rep), (m_sh, m_rep), (v_sh, v_rep))):
        d = {**_unpack_sharded(sh, w), **_unpack_replicated(rep, w)}
        for n, val in d.items():
            small.setdefault(n, [None] * 4)[i] = val
    for n in small:
        out[n] = small[n]

    res = [loss, grad_x[None]]
    for i in range(4):
        res += [out[n][i] for n in _WEIGHTS]
    return tuple(res)
```

```python
import functools

import jax
import jax.numpy as jnp
from jax import lax
from jax.experimental import pallas as pl
from jax.experimental.pallas import tpu as pltpu

F32 = jnp.float32
BF16 = jnp.bfloat16

HEAD_DIM = 128
ATTN_BLOCK = 128
DILATIONS = (1, 4, 16)
N_GROUPS = 3
CONV_WIDTH = 31
CONV_PAD = 32
CONV_ROWS = 32
EPILOGUE_ROWS = 128
ROPE_THETA = 10000.0
LN_EPS = 1e-5
ALPHA = 4.0 ** 0.25
ATTN_SCALE = HEAD_DIM ** -0.5
NEG = -1e30

ADAM_LR = 0.001
ADAM_B1 = 0.9
ADAM_B2 = 0.999
ADAM_EPS = 1e-08
ADAM_WD = 0.01
ADAM_STEP = 10

N_CHIPS = 4
VMEM_CAP = 60 << 20
MESH = pl.DeviceIdType.MESH
ANY = pl.BlockSpec(memory_space=pl.ANY)


def _vmem(nbytes):
    return int(min(max(2 * nbytes + (8 << 20), 24 << 20), VMEM_CAP))


def _fit(tile, n):
    if n <= tile:
        return n
    t = tile - tile % 128
    while n % t:
        t -= 128
    return t


def _nbytes(shape, dtype):
    n = 1
    for s in shape:
        n *= s
    return n * jnp.dtype(dtype).itemsize


_DIMS = {"nn": (((1,), (0,)), ((), ())), "nt": (((1,), (1,)), ((), ())), "tn": (((0,), (0,)), ((), ()))}


def _pcall(body, *, name, grid, in_specs, out_specs, out_shape, operands, scratch_shapes=(), vmem, carry=None):
    if carry is None:
        return pl.pallas_call(
            body, name=name, grid=grid, in_specs=in_specs, out_specs=out_specs, out_shape=out_shape,
            scratch_shapes=list(scratch_shapes),
            compiler_params=pltpu.CompilerParams(dimension_semantics=("arbitrary",) * len(grid), vmem_limit_bytes=vmem),
        )(*operands)
    n_in, n_out, n_scr = len(in_specs), len(out_specs), len(scratch_shapes)
    c_in, c_out = len(carry.ins), len(carry.out_shape)

    def wrapped(*refs):
        ins, refs = refs[:n_in], refs[n_in:]
        c_ins, refs = refs[:c_in], refs[c_in:]
        outs, refs = refs[:n_out], refs[n_out:]
        c_outs, refs = refs[:c_out], refs[c_out:]
        scr, c_sems = refs[:n_scr], refs[n_scr:]
        first = functools.reduce(jnp.logical_and, [pl.program_id(d) == 0 for d in range(len(grid))])
        last = functools.reduce(jnp.logical_and, [pl.program_id(d) == grid[d] - 1 for d in range(len(grid))])
        pl.when(first)(lambda: carry.start(c_ins, c_outs, c_sems))
        body(*ins, *outs, *scr)
        pl.when(last)(lambda: carry.finish(c_ins, c_outs, c_sems))

    res = pl.pallas_call(
        wrapped, name=name, grid=grid, in_specs=list(in_specs) + [ANY] * c_in, out_specs=list(out_specs) + [ANY] * c_out,
        out_shape=list(out_shape) + list(carry.out_shape), scratch_shapes=list(scratch_shapes) + list(carry.scratch),
        input_output_aliases={len(operands) + i: n_out + o for i, o in carry.aliases.items()},
        compiler_params=pltpu.CompilerParams(dimension_semantics=("arbitrary",) * len(grid), vmem_limit_bytes=vmem),
    )(*operands, *carry.ins)
    carry.set_result(res[n_out:])
    return res[:n_out]


def _mm(a, b, *, mode, outs, name, epilogue=None, extras=(), rextras=(), vecs=(), a_sel=None, b_sel=None,
        tm=None, tn=2048, tk=None, layout=None, carry=None, ep_rows=None, a_fn=None, sums=0):
    a2, b2 = a.shape[-2:], b.shape[-2:]
    if mode == "nn":
        (M, K), (K2, N) = a2, b2
    elif mode == "nt":
        (M, K), (N, K2) = a2, b2
    else:
        (K, M), (K2, N) = a2, b2
    assert K == K2, (a.shape, b.shape, mode)
    if tm is None:
        tm = 1024 if mode == "tn" else 512
    if tk is None:
        tk = 1024 if mode == "tn" else 2048
    if layout is not None:
        kind, nslots = layout
        r, c = (M // 2, N // nslots) if kind == "col" else (M // nslots, N // 2)
        tm, tn = _fit(tm, r), _fit(tn, c)
        assert r % tm == 0 and c % tn == 0
    else:
        tm, tn = _fit(tm, M), _fit(tn, N)
    tk = _fit(tk, K)
    assert M % tm == 0 and N % tn == 0 and K % tk == 0, (M, N, K, tm, tn, tk)
    nk = K // tk
    grid = (N // tn, M // tm, nk)

    def spec(arr, sel, blk, imap):
        if arr.ndim == 3:
            return pl.BlockSpec((None,) + blk, lambda j, i, k: (sel,) + imap(j, i, k))
        return pl.BlockSpec(blk, imap)

    if mode == "tn":
        a_spec = spec(a, a_sel, (tk, tm), lambda j, i, k: (k, i))
    else:
        a_spec = spec(a, a_sel, (tm, tk), lambda j, i, k: (i, k))
    if mode == "nt":
        b_spec = spec(b, b_sel, (tn, tk), lambda j, i, k: (j, k))
    else:
        b_spec = spec(b, b_sel, (tk, tn), lambda j, i, k: (k, j))
    in_specs = [a_spec, b_spec]
    in_specs += [pl.BlockSpec((tm, tn), lambda j, i, k: (i, j)) for _ in extras]
    in_specs += [pl.BlockSpec((tm, e.shape[1]), lambda j, i, k: (i, 0)) for e in rextras]
    in_specs += [pl.BlockSpec((1, tn), lambda j, i, k: (0, j)) for _ in vecs]

    if layout is None:
        out_shape = [jax.ShapeDtypeStruct((M, N), d) for d in outs] + [jax.ShapeDtypeStruct((1, N), F32)] * sums
        out_specs = [pl.BlockSpec((tm, tn), lambda j, i, k: (i, j)) for _ in outs]
        out_specs += [pl.BlockSpec((1, tn), lambda j, i, k: (0, j))] * sums
    else:
        assert len(outs) == 1
        out_shape = [jax.ShapeDtypeStruct((2, nslots, r, c), outs[0])]
        rb, cb = r // tm, c // tn
        if kind == "col":
            omap = lambda j, i, k: (i // rb, j // cb, i % rb, j % cb)
        else:
            omap = lambda j, i, k: (j // cb, i // rb, i % rb, j % cb)
        out_specs = [pl.BlockSpec((None, None, tm, tn), omap)]

    ne, nr, nv, no = len(extras), len(rextras), len(vecs), len(outs)
    dims = _DIMS[mode]

    def body(*refs):
        a_ref, b_ref = refs[0], refs[1]
        rest = refs[2:2 + ne + nr + nv]
        o_refs = refs[2 + ne + nr + nv:2 + ne + nr + nv + no]
        s_refs = refs[2 + ne + nr + nv + no:2 + ne + nr + nv + no + sums]

        def finish(total):
            if epilogue is None:
                for o in o_refs:
                    o[...] = total.astype(o.dtype)
                return
            step = min(ep_rows or tm, tm)
            col_sums = [None] * sums
            for r0 in range(0, tm, step):
                rows = slice(r0, r0 + step)
                tiles = [x[rows, :] for x in rest[:ne + nr]] + [x[...] for x in rest[ne + nr:]]
                res = epilogue(total[rows, :], *tiles)
                for o, val in zip(o_refs, res[:no]):
                    o[rows, :] = val.astype(o.dtype)
                for n_, val in enumerate(res[no:]):
                    part_sum = jnp.sum(val, axis=0, keepdims=True)
                    col_sums[n_] = part_sum if col_sums[n_] is None else col_sums[n_] + part_sum
            if sums:
                @pl.when(pl.program_id(1) == 0)
                def _():
                    for s_ref, val in zip(s_refs, col_sums):
                        s_ref[...] = val

                @pl.when(pl.program_id(1) > 0)
                def _():
                    for s_ref, val in zip(s_refs, col_sums):
                        s_ref[...] += val

        def product():
            a_tile = a_ref[...] if a_fn is None else a_fn(a_ref[...])
            return lax.dot_general(a_tile.astype(BF16), b_ref[...].astype(BF16), dims, preferred_element_type=F32)

        if nk == 1:
            finish(product())
            return
        acc = refs[-1]
        k = pl.program_id(2)

        @pl.when(k == 0)
        def _():
            acc[...] = product()

        @pl.when(k > 0)
        def _():
            acc[...] += product()

        @pl.when(k == nk - 1)
        def _():
            finish(acc[...])

    blk = (_nbytes((tm, tk), a.dtype) + _nbytes((tk, tn), b.dtype) + sum(_nbytes((tm, tn), e.dtype) for e in extras)
           + sum(_nbytes((tm, tn), d) for d in outs) + 2 * tm * tn * 4)
    res = _pcall(body, name=name, grid=grid, in_specs=in_specs, out_specs=out_specs, out_shape=out_shape,
                 operands=(a, b, *extras, *rextras, *vecs),
                 scratch_shapes=[pltpu.VMEM((tm, tn), F32)] if nk > 1 else [], vmem=_vmem(blk), carry=carry)
    return res[0] if no + sums == 1 else tuple(res)


def _rows(fn, rows, vecs, outs, sums, *, tm, name, carry=None):
    S = rows[0].shape[0]
    tm = min(tm, S)
    assert S % tm == 0
    nr, nv, no, ns = len(rows), len(vecs), len(outs), len(sums)

    def body(*refs):
        vals = fn(*[r[...] for r in refs[:nr + nv]])
        o_refs = refs[nr + nv:nr + nv + no]
        s_refs = refs[nr + nv + no:]
        for o, val in zip(o_refs, vals[:no]):
            o[...] = val.astype(o.dtype)
        if ns:
            @pl.when(pl.program_id(0) == 0)
            def _():
                for s in s_refs:
                    s[...] = jnp.zeros_like(s)

            for s, val in zip(s_refs, vals[no:]):
                s[...] += jnp.sum(val.astype(F32), axis=0, keepdims=True)

    in_specs = [pl.BlockSpec((tm, r.shape[1]), lambda i: (i, 0)) for r in rows]
    in_specs += [pl.BlockSpec(v.shape, lambda i: (0, 0)) for v in vecs]
    out_specs = [pl.BlockSpec((tm, c), lambda i: (i, 0)) for c, _ in outs]
    out_specs += [pl.BlockSpec((1, c), lambda i: (0, 0)) for c in sums]
    out_shape = [jax.ShapeDtypeStruct((S, c), d) for c, d in outs]
    out_shape += [jax.ShapeDtypeStruct((1, c), F32) for c in sums]
    blk = sum(_nbytes((tm, r.shape[1]), r.dtype) for r in rows) + sum(_nbytes((tm, c), d) for c, d in outs)
    blk += 6 * tm * max(r.shape[1] for r in rows) * 4
    res = _pcall(body, name=name, grid=(S // tm,), in_specs=in_specs, out_specs=out_specs, out_shape=out_shape,
                 operands=(*rows, *vecs), vmem=_vmem(blk), carry=carry)
    return tuple(res)


def _ln_norm(z):
    mu = jnp.mean(z, axis=-1, keepdims=True)
    d = z - mu
    var = jnp.mean(d * d, axis=-1, keepdims=True)
    rstd = lax.rsqrt(var + LN_EPS)
    return d * rstd, rstd


def _ln(z, g, b):
    return _ln_norm(z)[0] * g + b


def _ln_bwd(dy, n, rstd, g):
    dn = dy * g
    return rstd * (dn - jnp.mean(dn, axis=-1, keepdims=True) - n * jnp.mean(dn * n, axis=-1, keepdims=True))


def _sq_relu(t):
    return jnp.square(jnp.maximum(t.astype(F32), 0.0))


def _sigmoid(x):
    return 1.0 / (1.0 + jnp.exp(-x))


def _per_head(x, fn):
    h = x.shape[1] // HEAD_DIM
    return jnp.concatenate([fn(x[:, i * HEAD_DIM:(i + 1) * HEAD_DIM], i) for i in range(h)], axis=1)


def _rot(x, cosf, sinf):
    return _per_head(x, lambda xh, i: xh * cosf + pltpu.roll(xh, HEAD_DIM // 2, 1) * sinf)


def _rot_t(dy, cosf, sinf):
    return _per_head(dy, lambda dh, i: dh * cosf + pltpu.roll(dh * sinf, HEAD_DIM // 2, 1))


def _shift_copies(win):
    rows = win.shape[1] - 8
    for s in range(1, 8):
        win[s, 0:rows, :] = win[0, s:s + rows, :]


def _rows_at(win, start):
    s = start % 8
    return win[s, start - s:start - s + CONV_ROWS, :]


def _conv_fwd(glu, dw, dwb, *, tm=256, tc=512, name="conv_fwd", carry=None):
    S, D = glu.shape
    tm, tc = min(tm, S), min(tc, D)
    ni = S // tm

    def body(cur_ref, prev_ref, dw_ref, dwb_ref, o_ref, win):
        i = pl.program_id(1)
        tail = prev_ref[tm - CONV_PAD:tm, :]
        win[0, 0:CONV_PAD, :] = jnp.where(i > 0, tail, jnp.zeros_like(tail))
        win[0, CONV_PAD:CONV_PAD + tm, :] = cur_ref[...]
        _shift_copies(win)
        first = CONV_PAD - CONV_WIDTH + 1
        for r0 in range(0, tm, CONV_ROWS):
            acc = jnp.zeros((CONV_ROWS, tc), F32) + dwb_ref[...]
            for k in range(CONV_WIDTH):
                acc = acc + _rows_at(win, r0 + first + k) * dw_ref[k:k + 1, :]
            o_ref[r0:r0 + CONV_ROWS, :] = acc

    return _pcall(
        body, name=name, grid=(D // tc, ni),
        in_specs=[pl.BlockSpec((tm, tc), lambda j, i: (i, j)),
                  pl.BlockSpec((tm, tc), lambda j, i: (jnp.maximum(i - 1, 0), j)),
                  pl.BlockSpec((CONV_PAD, tc), lambda j, i: (0, j)),
                  pl.BlockSpec((1, tc), lambda j, i: (0, j))],
        out_specs=[pl.BlockSpec((tm, tc), lambda j, i: (i, j))],
        out_shape=[jax.ShapeDtypeStruct((S, D), F32)],
        scratch_shapes=[pltpu.VMEM((8, tm + CONV_PAD, tc), F32)],
        operands=(glu, glu, dw, dwb), vmem=_vmem(8 * tm * tc * 4), carry=carry)[0]


def _conv_bwd(dc, glu, a_pre, g_pre, dw, ba, bg, *, tm=256, tc=512, name="conv_bwd", carry=None):
    S, D = dc.shape
    tm, tc = min(tm, S), min(tc, D)
    ni = S // tm

    def fold8(v):
        out = v[0:8]
        for r in range(8, CONV_ROWS, 8):
            out = out + v[r:r + 8]
        return out

    def body(dc_ref, dcn_ref, glu_ref, glup_ref, a_ref, g_ref, dw_ref, ba_ref, bg_ref,
             da_ref, dg_ref, ddw_ref, dba_ref, dbg_ref, dwin, gwin, taps):
        i = pl.program_id(1)

        @pl.when(i == 0)
        def _():
            ddw_ref[...] = jnp.zeros_like(ddw_ref)
            dba_ref[...] = jnp.zeros_like(dba_ref)
            dbg_ref[...] = jnp.zeros_like(dbg_ref)

        head = dcn_ref[0:CONV_PAD, :]
        dwin[0, 0:tm, :] = dc_ref[...]
        dwin[0, tm:tm + CONV_PAD, :] = jnp.where(i < ni - 1, head, jnp.zeros_like(head))
        tail = glup_ref[tm - CONV_PAD:tm, :]
        gwin[0, 0:CONV_PAD, :] = jnp.where(i > 0, tail, jnp.zeros_like(tail))
        gwin[0, CONV_PAD:CONV_PAD + tm, :] = glu_ref[...]
        _shift_copies(dwin)
        _shift_copies(gwin)
        taps[...] = jnp.zeros_like(taps)
        first = CONV_PAD - CONV_WIDTH + 1
        sum_a = jnp.zeros((8, tc), F32)
        sum_g = jnp.zeros((8, tc), F32)
        for r0 in range(0, tm, CONV_ROWS):
            dcur = dc_ref[r0:r0 + CONV_ROWS, :]
            dglu = jnp.zeros((CONV_ROWS, tc), F32)
            for k in range(CONV_WIDTH):
                dglu = dglu + _rows_at(dwin, r0 + CONV_WIDTH - 1 - k) * dw_ref[k:k + 1, :]
                taps[k] += fold8(dcur * _rows_at(gwin, r0 + first + k))
            a = a_ref[r0:r0 + CONV_ROWS, :] + ba_ref[...]
            sg = _sigmoid(g_ref[r0:r0 + CONV_ROWS, :] + bg_ref[...])
            da = dglu * sg
            dg = dglu * a * sg * (1.0 - sg)
            da_ref[r0:r0 + CONV_ROWS, :] = da.astype(BF16)
            dg_ref[r0:r0 + CONV_ROWS, :] = dg.astype(BF16)
            sum_a = sum_a + fold8(da)
            sum_g = sum_g + fold8(dg)
        ddw_ref[...] += jnp.sum(taps[...], axis=1)
        dba_ref[...] += jnp.sum(sum_a, axis=0, keepdims=True)
        dbg_ref[...] += jnp.sum(sum_g, axis=0, keepdims=True)

    tile = lambda f: pl.BlockSpec((tm, tc), f)
    vec = pl.BlockSpec((1, tc), lambda j, i: (0, j))
    return _pcall(
        body, name=name, grid=(D // tc, ni),
        in_specs=[tile(lambda j, i: (i, j)), tile(lambda j, i: (jnp.minimum(i + 1, ni - 1), j)),
                  tile(lambda j, i: (i, j)), tile(lambda j, i: (jnp.maximum(i - 1, 0), j)),
                  tile(lambda j, i: (i, j)), tile(lambda j, i: (i, j)),
                  pl.BlockSpec((CONV_PAD, tc), lambda j, i: (0, j)), vec, vec],
        out_specs=[tile(lambda j, i: (i, j)), tile(lambda j, i: (i, j)),
                   pl.BlockSpec((CONV_PAD, tc), lambda j, i: (0, j)), vec, vec],
        out_shape=[jax.ShapeDtypeStruct((S, D), BF16), jax.ShapeDtypeStruct((S, D), BF16),
                   jax.ShapeDtypeStruct((CONV_PAD, D), F32), jax.ShapeDtypeStruct((1, D), F32),
                   jax.ShapeDtypeStruct((1, D), F32)],
        scratch_shapes=[pltpu.VMEM((8, tm + CONV_PAD, tc), F32), pltpu.VMEM((8, tm + CONV_PAD, tc), F32),
                        pltpu.VMEM((CONV_PAD, 8, tc), F32)],
        operands=(dc, dc, glu, glu, a_pre, g_pre, dw, ba, bg), vmem=_vmem(16 * tm * tc * 4), carry=carry)


def _nt(a, b):
    return lax.dot_general(a, b, _DIMS["nt"], preferred_element_type=F32)


def _tn(a, b):
    return lax.dot_general(a, b, _DIMS["tn"], preferred_element_type=F32)


HEADS_TOGETHER = 8
STAT_LANES = 128


def _per_head_pack(cols):
    rows = cols[0].shape[0]
    lane = lax.broadcasted_iota(jnp.int32, (rows, STAT_LANES), 1)
    out = jnp.zeros((rows, STAT_LANES), F32)
    for h, col in enumerate(cols):
        out = jnp.where(lane == h, col, out)
    return out


def _window_mask(qi, kj, first_key):
    B = ATTN_BLOCK
    return ((kj < B) & (kj >= qi) & (kj >= first_key)) | ((kj >= B) & (kj - B <= qi))


def _attn_fwd(q_rot, k, v, g, dil):
    S, D = k.shape
    H = D // HEAD_DIM
    L = S // dil
    nb_count = L // ATTN_BLOCK
    B = ATTN_BLOCK

    def body(q_ref, kc_ref, kp_ref, vc_ref, vp_ref, o_ref, lse_ref):
        nb = pl.program_id(1)
        qi = lax.broadcasted_iota(jnp.int32, (B, 2 * B), 0)
        kj = lax.broadcasted_iota(jnp.int32, (B, 2 * B), 1)
        valid = _window_mask(qi, kj, jnp.where(nb > 0, 0, B))
        stats = []
        for h0 in range(0, H, HEADS_TOGETHER):
            heads = range(h0, min(h0 + HEADS_TOGETHER, H))
            hs = [slice(h * HEAD_DIM, (h + 1) * HEAD_DIM) for h in heads]
            kk = [jnp.concatenate([kp_ref[:, c], kc_ref[:, c]], axis=0) for c in hs]
            vv = [jnp.concatenate([vp_ref[:, c], vc_ref[:, c]], axis=0) for c in hs]
            s = [jnp.where(valid, _nt(q_ref[:, c], kk_) * ATTN_SCALE, NEG) for c, kk_ in zip(hs, kk)]
            m = [jnp.max(s_, axis=1, keepdims=True) for s_ in s]
            p = [jnp.exp(s_ - m_) for s_, m_ in zip(s, m)]
            l = [jnp.sum(p_, axis=1, keepdims=True) for p_ in p]
            o = [jnp.dot(p_.astype(BF16), vv_, preferred_element_type=F32) / l_ for p_, vv_, l_ in zip(p, vv, l)]
            for c, o_ in zip(hs, o):
                o_ref[:, c] = o_.astype(o_ref.dtype)
            stats += [m_ + jnp.log(l_) for m_, l_ in zip(m, l)]
        lse_ref[...] = _per_head_pack(stats)

    blk = lambda f: pl.BlockSpec((B, D), f)
    cur = lambda r, nb: (nb, r)
    prev = lambda r, nb: (jnp.maximum(nb - 1, 0), r)
    o, lse = pl.pallas_call(
        body, name=f"attn_fwd_g{g}", grid=(dil, nb_count),
        in_specs=[blk(lambda r, nb: (nb, r * N_GROUPS + g)), blk(cur), blk(prev), blk(cur), blk(prev)],
        out_specs=[blk(cur), pl.BlockSpec((B, STAT_LANES), cur)],
        out_shape=[jax.ShapeDtypeStruct((L, dil * D), BF16), jax.ShapeDtypeStruct((L, dil * STAT_LANES), F32)],
        compiler_params=pltpu.CompilerParams(dimension_semantics=("parallel", "arbitrary"),
                                             vmem_limit_bytes=_vmem(12 * B * D * 4)),
    )(q_rot.reshape(L, dil * N_GROUPS * D), k.reshape(L, dil * D), k.reshape(L, dil * D),
      v.reshape(L, dil * D), v.reshape(L, dil * D))
    return o.reshape(S, D), lse.reshape(S, STAT_LANES)


def _attn_bwd(q_rot, k, v, do, lse, dlt, g, dil, *, name, carry=None):
    S, D = k.shape
    H = D // HEAD_DIM
    L = S // dil
    nb_count = L // ATTN_BLOCK
    B = ATTN_BLOCK

    def body(q_ref, kc_ref, kp_ref, vc_ref, vp_ref, do_ref, l_ref, d_ref, dq_ref, dk_ref, dv_ref, keep_k, keep_v):
        s_id = pl.program_id(1)

        @pl.when(s_id == 0)
        def _():
            keep_k[...] = jnp.zeros_like(keep_k)
            keep_v[...] = jnp.zeros_like(keep_v)

        @pl.when(s_id < nb_count)
        def _():
            qi = lax.broadcasted_iota(jnp.int32, (B, 2 * B), 0)
            kj = lax.broadcasted_iota(jnp.int32, (B, 2 * B), 1)
            valid = _window_mask(qi, kj, jnp.where(s_id > 0, 0, B))
            for h0 in range(0, H, HEADS_TOGETHER):
                heads = list(range(h0, min(h0 + HEADS_TOGETHER, H)))
                hs = [slice(h * HEAD_DIM, (h + 1) * HEAD_DIM) for h in heads]
                q = [q_ref[:, c] for c in hs]
                dout = [do_ref[:, c] for c in hs]
                kk = [jnp.concatenate([kp_ref[:, c], kc_ref[:, c]], axis=0) for c in hs]
                vv = [jnp.concatenate([vp_ref[:, c], vc_ref[:, c]], axis=0) for c in hs]
                s = [jnp.where(valid, _nt(q_, kk_) * ATTN_SCALE, NEG) for q_, kk_ in zip(q, kk)]
                dp = [_nt(do_, vv_) for do_, vv_ in zip(dout, vv)]
                p = [jnp.exp(s_ - l_ref[:, h:h + 1]) for s_, h in zip(s, heads)]
                ds = [(p_ * (dp_ - d_ref[:, h:h + 1])).astype(BF16) for p_, dp_, h in zip(p, dp, heads)]
                dq = [jnp.dot(ds_, kk_, preferred_element_type=F32) * ATTN_SCALE for ds_, kk_ in zip(ds, kk)]
                dkk = [_tn(ds_, q_) * ATTN_SCALE for ds_, q_ in zip(ds, q)]
                dvv = [_tn(p_.astype(BF16), do_) for p_, do_ in zip(p, dout)]
                for c, dq_, dkk_, dvv_ in zip(hs, dq, dkk, dvv):
                    dq_ref[:, c] = dq_.astype(dq_ref.dtype)
                    dk_ref[:, c] = (keep_k[:, c] + dkk_[0:B]).astype(dk_ref.dtype)
                    dv_ref[:, c] = (keep_v[:, c] + dvv_[0:B]).astype(dv_ref.dtype)
                    keep_k[:, c] = dkk_[B:2 * B]
                    keep_v[:, c] = dvv_[B:2 * B]

        @pl.when(s_id == nb_count)
        def _():
            dk_ref[...] = keep_k[...].astype(dk_ref.dtype)
            dv_ref[...] = keep_v[...].astype(dv_ref.dtype)

    last = nb_count - 1
    blk = lambda f: pl.BlockSpec((B, D), f)
    cur = lambda r, s: (jnp.minimum(s, last), r)
    prev = lambda r, s: (jnp.maximum(jnp.minimum(s, last) - 1, 0), r)
    lag = lambda r, s: (jnp.maximum(s - 1, 0), r)
    qcur = lambda r, s: (jnp.minimum(s, last), r * N_GROUPS + g)
    qv = q_rot.reshape(L, dil * N_GROUPS * D)
    view = lambda t: t.reshape(L, dil * D)
    sview = lambda t: t.reshape(L, dil * STAT_LANES)
    stat = lambda f: pl.BlockSpec((B, STAT_LANES), f)
    dq, dk, dv = _pcall(
        body, name=name, grid=(dil, nb_count + 1),
        in_specs=[blk(qcur), blk(cur), blk(prev), blk(cur), blk(prev), blk(cur), stat(cur), stat(cur)],
        out_specs=[blk(cur), blk(lag), blk(lag)],
        out_shape=[jax.ShapeDtypeStruct((L, dil * D), BF16)] * 3,
        scratch_shapes=[pltpu.VMEM((B, D), F32), pltpu.VMEM((B, D), F32)],
        operands=(qv, view(k), view(k), view(v), view(v), view(do), sview(lse), sview(dlt)),
        vmem=_vmem(24 * B * D * 4), carry=carry)
    return dq.reshape(S, D), dk.reshape(S, D), dv.reshape(S, D)


def _mlp_ple_fwd(z1, h1b, p_l, W, vec, l, run, kv_vec=None, target=None):
    D = z1.shape[1]
    g1, b1, g2, b2 = vec

    t = run(_mm, h1b, W[f"mlp_up{l}"], mode="nn", outs=[BF16], tm=1024, name=f"mlp_up{l}")

    def z2_ep(acc, z1_t, g1_, b1_, g2_, b2_):
        z2 = ALPHA * _ln(z1_t, g1_, b1_) + acc
        return z2, _ln(z2, g2_, b2_)

    z2, h2b = run(_mm, t, W[f"mlp_down{l}"], mode="nn", outs=[F32, BF16], extras=[z1], vecs=[g1, b1, g2, b2], a_fn=_sq_relu,
                  epilogue=z2_ep, ep_rows=EPILOGUE_ROWS, name=f"mlp_down{l}")
    act = None
    pp = run(_mm, p_l, W[f"ple_proj{l}"], mode="nn", outs=[F32], name=f"ple_proj{l}")
    if kv_vec is None:
        def head_ep(acc, z2_t, pp_t, tgt_t, g2_, b2_):
            y, gt = _ple_out(z2_t, pp_t, acc, g2_, b2_)
            err = y - tgt_t
            dy = err * (1.0 / D)
            return dy, dy * gt, dy * pp_t * gt * (1.0 - gt), 0.5 * err * err * (1.0 / D)

        dy, d_pp, d_gpre, loss_cols = run(_mm, h2b, W[f"ple_gate{l}"], mode="nn", outs=[F32, BF16, BF16], sums=1,
                                          extras=[z2, pp, target], vecs=[g2, b2], epilogue=head_ep,
                                          ep_rows=EPILOGUE_ROWS, tm=256, name=f"ple_gate{l}")
        return t, act, z2, h2b, pp, dy, d_pp, d_gpre, loss_cols

    def x1_ep(acc, z2_t, pp_t, g2_, b2_, kg, kb):
        x1, _ = _ple_out(z2_t, pp_t, acc, g2_, b2_)
        return acc, x1, _ln(x1, kg, kb)

    gpre, x1, kvn = run(_mm, h2b, W[f"ple_gate{l}"], mode="nn", outs=[F32, F32, BF16], extras=[z2, pp],
                        vecs=[g2, b2, *kv_vec], epilogue=x1_ep, ep_rows=EPILOGUE_ROWS, tm=256, name=f"ple_gate{l}")
    return t, act, z2, h2b, pp, gpre, x1, kvn


def _mlp_ple_bwd(dy, d_pp, d_gpre, p_l, z1, h1b, t, act, z2, h2b, wts, vec, l, run, produce):
    D = z1.shape[1]
    up, down, pp_w, pg_w = wts
    g1, b1, g2, b2 = vec
    produce(f"ple_proj{l}", run(_mm, p_l, d_pp, mode="tn", outs=[BF16], layout=("col", N_CHIPS), name=f"d_ple_proj{l}"))
    produce(f"ple_gate{l}", run(_mm, h2b, d_gpre, mode="tn", outs=[BF16], layout=("row", N_CHIPS), name=f"d_ple_gate{l}"))
    def ln2_bwd(acc, dy_t, z2_t, g2_):
        dh2 = acc + dy_t
        n, rstd = _ln_norm(z2_t)
        dz2 = _ln_bwd(dh2, n, rstd, g2_)
        return dz2, dz2, dh2 * n, dh2

    dz2, dz2b, dg2, db2 = run(_mm, d_gpre, pg_w, mode="nt", outs=[F32, BF16], sums=2, extras=[dy, z2], vecs=[g2],
                              epilogue=ln2_bwd, ep_rows=EPILOGUE_ROWS, tm=256, name=f"dh2_{l}")
    produce(f"mlp_down{l}", run(_mm, t, dz2b, mode="tn", outs=[BF16], layout=("row", N_CHIPS), a_fn=_sq_relu,
                                name=f"d_mlp_down{l}"))
    dt = run(_mm, dz2b, down, mode="nt", outs=[BF16], extras=[t],
             epilogue=lambda acc, t_: (acc * 2.0 * jnp.maximum(t_.astype(F32), 0.0),), name=f"dt{l}")
    produce(f"mlp_up{l}", run(_mm, h1b, dt, mode="tn", outs=[BF16], layout=("col", N_CHIPS), name=f"d_mlp_up{l}"))
    def ln1_bwd(acc, dz2_t, z1_t, g1_):
        dh1 = acc + ALPHA * dz2_t
        n, rstd = _ln_norm(z1_t)
        dz1 = _ln_bwd(dh1, n, rstd, g1_)
        return dz1, dz1, dh1 * n, dh1

    dz1, dz1b, dg1, db1 = run(_mm, dt, up, mode="nt", outs=[F32, BF16], sums=2, extras=[dz2, z1], vecs=[g1],
                              epilogue=ln1_bwd, ep_rows=EPILOGUE_ROWS, tk=1024, name=f"dh1_{l}")
    return dz1, dz1b, (dg1, db1, dg2, db2)


def _ple_out(z2, pp, gpre, g2, b2):
    gt = _sigmoid(gpre)
    return _ln(z2, g2, b2) + pp * gt, gt


_GATHER_AT = {
    "conv_in_a": ("conv_w_out",),
    "conv_in_g": ("ple_gate0", "ple_proj0"),
    "conv_fwd": ("mlp_up0",),
    "mlp_up0": ("mlp_down0",),
    "mlp_down0": ("attn_w_q", "w_kv"),
    "kv_k": ("attn_w_o",),
    "kv_v": ("ple_proj1", "ple_gate1"),
    "attn_q": ("mlp_up1",),
    "mlp_up1": ("mlp_down1",),
}
_GATHER_FIRST = ("conv_w_in",)


def _local_step(x, p, cosf, sinf, target, W, V, shards=None, reducer=None, chip_arr=None):
    S, D = x.shape
    gw, gv = {}, {}
    if shards is not None:
        W = dict(_Gather(_GATHER_FIRST, shards, chip_arr).run_alone("gather_first"))

    def run(fn, *args, name, **kw):
        gather = _Gather(_GATHER_AT[name], shards, chip_arr) if (shards is not None and name in _GATHER_AT) else None
        carry = gather if reducer is None or gather is not None else reducer.carry(name)
        out = fn(*args, name=name, carry=carry, **kw)
        if gather is not None:
            W.update(gather.result)
        elif reducer is not None:
            reducer.carried()
        return out

    a_pre = run(_mm, x, W["conv_w_in"], b_sel=0, mode="nn", outs=[F32], name="conv_in_a")
    g_pre, glu = run(_mm, x, W["conv_w_in"], b_sel=1, mode="nn", outs=[F32, F32], extras=[a_pre],
                     vecs=[V["conv_b_a"], V["conv_b_g"]], tm=256, ep_rows=EPILOGUE_ROWS,
                     epilogue=lambda acc, a, ba, bg: (acc, (a + ba) * _sigmoid(acc + bg)), name="conv_in_g")
    cv = run(_conv_fwd, glu, V["conv_dw"], V["conv_dw_b"], name="conv_fwd")

    def silu_ln(c, g_, b_):
        y = _ln(c, g_, b_)
        return (y * _sigmoid(y),)

    (sb,) = _rows(silu_ln, [cv], [V["conv_ln_g"], V["conv_ln_b"]], [(D, BF16)], [], tm=256, name="conv_ln_fwd")
    def z1_ep(acc, x_t, g_, b_):
        z1 = ALPHA * x_t + acc
        return z1, _ln(z1, g_, b_)

    vec0 = (V["ln1_g0"], V["ln1_b0"], V["ln2_g0"], V["ln2_b0"])
    vec1 = (V["ln1_g1"], V["ln1_b1"], V["ln2_g1"], V["ln2_b1"])
    z1_0, h1b_0 = _mm(sb, W["conv_w_out"], mode="nn", outs=[F32, BF16], extras=[x], vecs=[vec0[0], vec0[1]],
                      epilogue=z1_ep, ep_rows=EPILOGUE_ROWS, name="conv_out")
    t0, act0, z2_0, h2b_0, pp0, gpre0, x1, kvn = _mlp_ple_fwd(z1_0, h1b_0, p[0], W, vec0, 0, run,
                                                              (V["kv_ln_g"], V["kv_ln_b"]))

    rot_ep = lambda acc, c_, s_: (_rot(acc, c_, s_),)
    k_rot = run(_mm, kvn, W["w_kv"], b_sel=0, mode="nn", outs=[BF16], rextras=[cosf, sinf], epilogue=rot_ep, name="kv_k")
    v_b = run(_mm, kvn, W["w_kv"], b_sel=1, mode="nn", outs=[BF16], name="kv_v")
    q_rot = run(_mm, x1, W["attn_w_q"], mode="nn", outs=[BF16], rextras=[cosf, sinf], epilogue=rot_ep, name="attn_q")
    og, lg = [], []
    for g, dil in enumerate(DILATIONS):
        o_g, l_g = _attn_fwd(q_rot, k_rot, v_b, g, dil)
        og.append(o_g)
        lg.append(l_g)

    def merge(o0, o1, o2, l0, l1, l2):
        m = jnp.maximum(jnp.maximum(l0, l1), l2)
        e = [jnp.exp(l0 - m), jnp.exp(l1 - m), jnp.exp(l2 - m)]
        den = e[0] + e[1] + e[2]
        w = [e_g / den for e_g in e]
        o = _per_head(o0, lambda oh, h: sum(w[g][:, h:h + 1] * (o0, o1, o2)[g][:, h * HEAD_DIM:(h + 1) * HEAD_DIM].astype(F32)
                                            for g in range(N_GROUPS)))
        return o, m + jnp.log(den)

    ob, lse = _rows(merge, og + lg, [], [(D, BF16), (STAT_LANES, F32)], [], tm=256, name="attn_merge")
    z1_1, h1b_1 = _mm(ob, W["attn_w_o"], mode="nn", outs=[F32, BF16], extras=[x1], vecs=[vec1[0], vec1[1]],
                      epilogue=z1_ep, ep_rows=EPILOGUE_ROWS, name="attn_out")
    t1, act1, z2_1, h2b_1, pp1, dy1, d_pp1, d_gpre1, loss_cols = _mlp_ple_fwd(z1_1, h1b_1, p[1], W, vec1, 1, run,
                                                                              target=target)
    wts0 = (W["mlp_up0"], W["mlp_down0"], W["ple_proj0"], W["ple_gate0"])
    wts1 = (W["mlp_up1"], W["mlp_down1"], W["ple_proj1"], W["ple_gate1"])


    def produce(name, grad):
        gw[name] = grad
        if reducer is not None:
            reducer.produced(name, grad)

    dz1_1, dz1b_1, (gv["ln1_g1"], gv["ln1_b1"], gv["ln2_g1"], gv["ln2_b1"]) = _mlp_ple_bwd(
        dy1, d_pp1, d_gpre1, p[1], z1_1, h1b_1, t1, act1, z2_1, h2b_1, wts1, vec1, 1, run, produce)
    produce("attn_w_o", run(_mm, ob, dz1b_1, mode="tn", outs=[BF16], layout=("row", N_CHIPS), name="d_attn_w_o"))

    do_b = run(_mm, dz1b_1, W["attn_w_o"], mode="nt", outs=[BF16], name="attn_do")

    def delta(do_t, o_t):
        prod = do_t.astype(F32) * o_t.astype(F32)
        H = D // HEAD_DIM
        return (_per_head_pack([jnp.sum(prod[:, h * HEAD_DIM:(h + 1) * HEAD_DIM], axis=1, keepdims=True) for h in range(H)]),)

    (dlt,) = _rows(delta, [do_b, ob], [], [(STAT_LANES, F32)], [], tm=256, name="attn_delta")
    dqs, dks, dvs = [], [], []
    for g, dil in enumerate(DILATIONS):
        dq_g, dk_g, dv_g = run(_attn_bwd, q_rot, k_rot, v_b, do_b, lse, dlt, g, dil, name=f"attn_bwd_g{g}")
        dqs.append(dq_g)
        dks.append(dk_g)
        dvs.append(dv_g)

    def unrot(q0, q1, q2, k0, k1, k2, v0, v1, v2, c_, s_):
        dq = jnp.concatenate([_rot_t(t_.astype(F32), c_, s_) for t_ in (q0, q1, q2)], axis=1)
        f = lambda t_: t_.astype(F32)
        return dq, _rot_t(f(k0) + f(k1) + f(k2), c_, s_), f(v0) + f(v1) + f(v2)

    dq, dk, dv = run(_rows, unrot, dqs + dks + dvs + [cosf, sinf], [], [(N_GROUPS * D, BF16), (D, BF16), (D, BF16)], [],
                     tm=128, name="attn_unrot")
    produce("attn_w_q", run(_mm, x1, dq, mode="tn", outs=[BF16], layout=("col", N_CHIPS), name="d_attn_w_q"))
    dx1_q = run(_mm, dq, W["attn_w_q"], mode="nt", outs=[F32], extras=[dz1_1],
                epilogue=lambda acc, e: (acc + ALPHA * e,), name="dx1_q")
    produce("w_kv", jnp.concatenate(
        [run(_mm, kvn, dk, mode="tn", outs=[BF16], layout=("col", 2), name="d_w_kv_k"),
         run(_mm, kvn, dv, mode="tn", outs=[BF16], layout=("col", 2), name="d_w_kv_v")], axis=1))
    dkvn_k = run(_mm, dk, W["w_kv"], b_sel=0, mode="nt", outs=[F32], name="dkvn_k")
    def x1_bwd(acc, dkvn_k_t, dx1q_t, x1_t, pp, gpre, kg):
        dkvn_t = acc + dkvn_k_t
        n, rstd = _ln_norm(x1_t)
        dy = dx1q_t + _ln_bwd(dkvn_t, n, rstd, kg)
        gt = _sigmoid(gpre)
        return dy, dy * gt, dy * pp * gt * (1.0 - gt), dkvn_t * n, dkvn_t

    dy0, d_pp0, d_gpre0, gv["kv_ln_g"], gv["kv_ln_b"] = run(
        _mm, dv, W["w_kv"], b_sel=1, mode="nt", outs=[F32, BF16, BF16], sums=2, extras=[dkvn_k, dx1_q, x1, pp0, gpre0],
        vecs=[V["kv_ln_g"]], epilogue=x1_bwd, ep_rows=EPILOGUE_ROWS, tm=256, name="dkvn_v")

    dz1_0, dz1b_0, (gv["ln1_g0"], gv["ln1_b0"], gv["ln2_g0"], gv["ln2_b0"]) = _mlp_ple_bwd(
        dy0, d_pp0, d_gpre0, p[0], z1_0, h1b_0, t0, act0, z2_0, h2b_0, wts0, vec0, 0, run, produce)
    produce("conv_w_out", run(_mm, sb, dz1b_0, mode="tn", outs=[BF16], layout=("row", N_CHIPS), name="d_conv_w_out"))
    ds = run(_mm, dz1b_0, W["conv_w_out"], mode="nt", outs=[F32], name="conv_ds")

    def conv_ln_bwd(ds_t, c_t, g_, b_):
        n, rstd = _ln_norm(c_t)
        y = n * g_ + b_
        sg = _sigmoid(y)
        dln = ds_t * sg * (1.0 + y * (1.0 - sg))
        dc = _ln_bwd(dln, n, rstd, g_)
        return dc, dln * n, dln, dc

    dc, gv["conv_ln_g"], gv["conv_ln_b"], gv["conv_dw_b"] = _rows(
        conv_ln_bwd, [ds, cv], [V["conv_ln_g"], V["conv_ln_b"]], [(D, F32)], [D, D, D], tm=256, name="conv_ln_bwd")
    da, dg, gv["conv_dw"], gv["conv_b_a"], gv["conv_b_g"] = run(
        _conv_bwd, dc, glu, a_pre, g_pre, V["conv_dw"], V["conv_b_a"], V["conv_b_g"], name="conv_bwd")
    produce("conv_w_in", jnp.concatenate(
        [run(_mm, x, da, mode="tn", outs=[BF16], layout=("col", 2), name="d_conv_w_in_a"),
         run(_mm, x, dg, mode="tn", outs=[BF16], layout=("col", 2), name="d_conv_w_in_g")], axis=1))
    dx_a = run(_mm, da, W["conv_w_in"], b_sel=0, mode="nt", outs=[F32], extras=[dz1_0],
               epilogue=lambda acc, e: (acc + ALPHA * e,), name="dx_a")
    grad_x = run(_mm, dg, W["conv_w_in"], b_sel=1, mode="nt", outs=[F32], extras=[dx_a],
                 epilogue=lambda acc, e: (acc + e,), name="dx_g")
    if reducer is not None:
        reducer.carry("share_last").run_alone("share_last")
        reducer.carried()
    return loss_cols, grad_x, gw, gv


def _place():
    x, y, c = lax.axis_index("x"), lax.axis_index("y"), lax.axis_index("c")
    chips = [(1 - x, y), (x, 1 - y), (1 - x, 1 - y)]
    return x, y, c, chips


def _remote(src, dst, ssem, rsem, dev):
    return pltpu.make_async_remote_copy(src_ref=src, dst_ref=dst, send_sem=ssem, recv_sem=rsem, device_id=dev,
                                        device_id_type=MESH)


def _allgather8(block, name):
    R, C = block.shape

    def body(x_ref, out_ref, send_sems, recv_sems, local_sem):
        x, y, c, chips = _place()
        me, sibling = (x, y, c), (x, y, 1 - c)

        def slot(px, py, pc):
            return out_ref.at[4 * px + 2 * py + pc]

        def copy(k, blockpos, to, src=None):
            return _remote(slot(*blockpos) if src is None else src, slot(*blockpos), send_sems.at[k], recv_sems.at[k], to)

        mine = pltpu.make_async_copy(x_ref, slot(*me), local_sem)
        mine.start()
        first = [copy(0, me, sibling, src=x_ref)]
        first += [copy(1 + j, me, (*chip, c), src=x_ref) for j, chip in enumerate(chips)]
        for cp in first:
            cp.start()
        passed = [copy(4 + j, (*chip, c), sibling) for j, chip in enumerate(chips)]
        for j, chip in enumerate(chips):
            copy(1 + j, (*chip, c), me).wait_recv()
            passed[j].start()
        copy(0, sibling, me).wait_recv()
        for j, chip in enumerate(chips):
            copy(4 + j, (*chip, 1 - c), me).wait_recv()
        for cp in first + passed:
            cp.wait_send()
        mine.wait()

    return pl.pallas_call(
        body, name=name, out_shape=jax.ShapeDtypeStruct((8, R, C), block.dtype),
        in_specs=[pl.BlockSpec(memory_space=pltpu.VMEM)], out_specs=pl.BlockSpec(memory_space=pltpu.VMEM),
        scratch_shapes=[pltpu.SemaphoreType.DMA((7,)), pltpu.SemaphoreType.DMA((7,)), pltpu.SemaphoreType.DMA],
        compiler_params=pltpu.CompilerParams(vmem_limit_bytes=_vmem(10 * _nbytes((R, C), block.dtype))),
    )(block)


_MATS = (
    ("conv_w_in", "conv_w_in", 0, "col", True),
    ("conv_w_out", "conv_w_out", 0, "row", False),
    ("mlp_up0", "mlp_up", 0, "col", False),
    ("mlp_down0", "mlp_down", 0, "row", False),
    ("ple_proj0", "ple_proj", 0, "col", False),
    ("ple_gate0", "ple_gate", 0, "row", False),
    ("w_kv", "w_kv", None, "col", True),
    ("attn_w_q", "attn_w_q", 0, "col", False),
    ("attn_w_o", "attn_w_o", 0, "row", False),
    ("mlp_up1", "mlp_up", 1, "col", False),
    ("mlp_down1", "mlp_down", 1, "row", False),
    ("ple_proj1", "ple_proj", 1, "col", False),
    ("ple_gate1", "ple_gate", 1, "row", False),
)


class _Carry:
    result = None
    aliases = {}

    def set_result(self, outs):
        self.result = dict(zip(self.names, outs))

    def run_alone(self, name):
        n_in, n_out = len(self.ins), len(self.out_shape)

        def body(*refs):
            in_refs, out_refs, sems = refs[:n_in], refs[n_in:n_in + n_out], refs[n_in + n_out:]
            self.start(in_refs, out_refs, sems)
            self.finish(in_refs, out_refs, sems)

        outs = pl.pallas_call(body, name=name, out_shape=self.out_shape, in_specs=[ANY] * n_in, out_specs=[ANY] * n_out,
                              scratch_shapes=self.scratch, input_output_aliases=dict(self.aliases))(*self.ins)
        self.set_result(outs)
        return self.result


class _Gather(_Carry):
    def __init__(self, names, shards, chip_arr):
        mats = [m for m in _MATS if m[0] in names]
        srcs = sorted({m[1] for m in mats})
        self.names = [m[0] for m in mats]
        self.out_shape, self.geo, placed = [], [], []
        for name, src, layer, kind, split in mats:
            s = shards[src]
            ks, ns = s.shape[-2:]
            K, N = (ks, ns * N_CHIPS) if kind == "col" else (ks * N_CHIPS, ns)
            self.out_shape.append(jax.ShapeDtypeStruct((2, K, N // 2) if split else (K, N), BF16))
            self.geo.append((srcs.index(src), layer if s.ndim == 3 else None, kind, split, K, N))
            placed.append(_place_shard(s, layer if s.ndim == 3 else None, kind, split, chip_arr, f"place_{name}"))
        T = len(mats)
        self.ins = [shards[n] for n in srcs] + placed
        self.aliases = {len(srcs) + t: t for t in range(T)}
        self.scratch = [pltpu.SemaphoreType.DMA((3 * T,)) for _ in range(4)]
        self.result = None

    def _copies(self, in_refs, out_refs, sems):
        geo, T = self.geo, len(self.geo)
        s_ici, r_ici, s_d2d, r_d2d = sems
        x, y, c, chips = _place()
        me = 2 * x + y
        sibling = (x, y, 1 - c)
        idx = [2 * cx + cy for cx, cy in chips]

        def src_ref(t):
            i, layer, _, _, _, _ = geo[t]
            return in_refs[i] if layer is None else in_refs[i].at[layer]

        def src_half(t, h):
            _, _, kind, _, K, N = geo[t]
            if kind == "col":
                return src_ref(t).at[pl.ds(h * (K // 2), K // 2), :]
            return src_ref(t).at[:, pl.ds(h * (N // 2), N // 2)]

        def dst(t, j, h):
            _, _, kind, split, K, N = geo[t]
            n, k = N // N_CHIPS, K // N_CHIPS
            if kind == "col":
                rows = slice(None) if h is None else pl.ds(h * (K // 2), K // 2)
                if split:
                    return out_refs[t].at[j // 2, rows, pl.ds((j % 2) * n, n)]
                return out_refs[t].at[rows, pl.ds(j * n, n)]
            cols = slice(None) if h is None else pl.ds(h * (N // 2), N // 2)
            return out_refs[t].at[pl.ds(j * k, k), cols]

        sends = [_remote(src_half(t, c), dst(t, me, c), s_ici.at[3 * t + kk], r_ici.at[3 * t + kk], (*chips[kk], c))
                 for t in range(T) for kk in range(3)]
        hops = []
        for t in range(T):
            for kk in range(3):
                mine, theirs = dst(t, idx[kk], c), dst(t, idx[kk], 1 - c)
                hops.append((_remote(mine, mine, s_ici.at[3 * t + kk], r_ici.at[3 * t + kk], sibling),
                             _remote(mine, mine, s_d2d.at[3 * t + kk], r_d2d.at[3 * t + kk], sibling),
                             _remote(theirs, theirs, s_d2d.at[3 * t + kk], r_d2d.at[3 * t + kk], sibling)))
        return sends, hops

    def start(self, in_refs, out_refs, sems):
        for cp in self._copies(in_refs, out_refs, sems)[0]:
            cp.start()

    def finish(self, in_refs, out_refs, sems):
        sends, hops = self._copies(in_refs, out_refs, sems)
        for landed, forward, _ in hops:
            landed.wait_recv()
            forward.start()
        for _, _, from_sibling in hops:
            from_sibling.wait_recv()
        for cp in sends + [h[1] for h in hops]:
            cp.wait_send()


def _place_shard(shard, layer, kind, split, chip_arr, name):
    ks, ns = shard.shape[-2:]
    K, N = (ks, ns * N_CHIPS) if kind == "col" else (ks * N_CHIPS, ns)
    tr = _fit(256, ks)
    nb = ks // tr
    if shard.ndim == 3:
        in_spec = pl.BlockSpec((None, tr, ns), lambda i, me: (layer, i, 0))
    else:
        in_spec = pl.BlockSpec((tr, ns), lambda i, me: (i, 0))
    if kind == "row":
        out_shape, out_spec = (K, N), pl.BlockSpec((tr, ns), lambda i, me: (me[0] * nb + i, 0))
    elif split:
        out_shape, out_spec = (2, K, N // 2), pl.BlockSpec((None, tr, ns), lambda i, me: (me[0] // 2, i, me[0] % 2))
    else:
        out_shape, out_spec = (K, N), pl.BlockSpec((tr, ns), lambda i, me: (i, me[0]))

    def body(me_ref, s_ref, o_ref):
        o_ref[...] = s_ref[...]

    return pl.pallas_call(
        body, name=name, out_shape=jax.ShapeDtypeStruct(out_shape, BF16),
        grid_spec=pltpu.PrefetchScalarGridSpec(num_scalar_prefetch=1, grid=(nb,), in_specs=[in_spec], out_specs=out_spec),
        compiler_params=pltpu.CompilerParams(dimension_semantics=("parallel",), vmem_limit_bytes=_vmem(4 * tr * ns * 2)),
    )(chip_arr, shard)


class _Multi(_Carry):
    def __init__(self, parts):
        self.parts = parts
        self.ins = [a for p in parts for a in p.ins]
        self.out_shape = [a for p in parts for a in p.out_shape]
        self.scratch = [a for p in parts for a in p.scratch]
        self.aliases, n_in, n_out = {}, 0, 0
        for p in parts:
            self.aliases.update({n_in + i: n_out + o for i, o in p.aliases.items()})
            n_in, n_out = n_in + len(p.ins), n_out + len(p.out_shape)

    def _split(self, seq, field):
        out, at = [], 0
        for p in self.parts:
            n = len(getattr(p, field))
            out.append(seq[at:at + n])
            at += n
        return out

    def _each(self, method, in_refs, out_refs, sems):
        for p, i, o, s in zip(self.parts, self._split(in_refs, "ins"), self._split(out_refs, "out_shape"),
                              self._split(sems, "scratch")):
            getattr(p, method)(i, o, s)

    def start(self, in_refs, out_refs, sems):
        self._each("start", in_refs, out_refs, sems)

    def finish(self, in_refs, out_refs, sems):
        self._each("finish", in_refs, out_refs, sems)

    def set_result(self, outs):
        for p, o in zip(self.parts, self._split(list(outs), "out_shape")):
            p.set_result(o)


class _PairSend(_Carry):
    def __init__(self, grads):
        self.names = list(grads)
        self.ins = [grads[n] for n in self.names]
        self.out_shape = [jax.ShapeDtypeStruct(a.shape[1:], BF16) for a in self.ins]
        T = len(self.names)
        self.scratch = [pltpu.SemaphoreType.DMA((T,)), pltpu.SemaphoreType.DMA((T,))]

    def _copies(self, in_refs, out_refs, sems):
        x, y, c, _ = _place()
        return [_remote(in_refs[t].at[1 - c], out_refs[t], sems[0].at[t], sems[1].at[t], (x, y, 1 - c))
                for t in range(len(self.names))]

    def start(self, in_refs, out_refs, sems):
        for cp in self._copies(in_refs, out_refs, sems):
            cp.start()

    def finish(self, in_refs, out_refs, sems):
        for cp in self._copies(in_refs, out_refs, sems):
            cp.wait()


class _ChipScatter(_Carry):
    def __init__(self, sums):
        self.names = list(sums)
        T = len(self.names)
        self.ins = [sums[n][0] for n in self.names] + [sums[n][1] for n in self.names]
        self.out_shape = [jax.ShapeDtypeStruct(a.shape, BF16) for a in self.ins[:T]]
        self.aliases = {T + t: t for t in range(T)}
        self.scratch = [pltpu.SemaphoreType.DMA((3 * T,)), pltpu.SemaphoreType.DMA((3 * T,))]

    def _copies(self, in_refs, out_refs, sems):
        ssem, rsem = sems
        x, y, c, chips = _place()
        me = 2 * x + y
        idx = [2 * cx + cy for cx, cy in chips]
        T = len(self.names)
        sends = [_remote(in_refs[t].at[idx[kk]], out_refs[t].at[me], ssem.at[3 * t + kk], rsem.at[3 * t + kk],
                         (*chips[kk], c)) for t in range(T) for kk in range(3)]
        lands = [_remote(out_refs[t].at[idx[kk]], out_refs[t].at[idx[kk]], ssem.at[3 * t + kk], rsem.at[3 * t + kk],
                         (*chips[kk], c)) for t in range(T) for kk in range(3)]
        return sends, lands

    def start(self, in_refs, out_refs, sems):
        for cp in self._copies(in_refs, out_refs, sems)[0]:
            cp.start()

    def finish(self, in_refs, out_refs, sems):
        sends, lands = self._copies(in_refs, out_refs, sems)
        for cp in lands:
            cp.wait_recv()
        for cp in sends:
            cp.wait_send()


class _PairShare(_Carry):
    def __init__(self, halves):
        self.names = list(halves)
        self.ins = [halves[n] for n in self.names]
        self.out_shape = [jax.ShapeDtypeStruct(a.shape, F32) for a in self.ins]
        T = len(self.names)
        self.aliases = {t: t for t in range(T)}
        self.scratch = [pltpu.SemaphoreType.DMA((T,)), pltpu.SemaphoreType.DMA((T,))]

    def _copies(self, in_refs, out_refs, sems):
        ssem, rsem = sems
        x, y, c, _ = _place()
        sibling = (x, y, 1 - c)
        T = len(self.names)
        sends = [_remote(out_refs[t].at[c], out_refs[t].at[c], ssem.at[t], rsem.at[t], sibling) for t in range(T)]
        lands = [_remote(out_refs[t].at[1 - c], out_refs[t].at[1 - c], ssem.at[t], rsem.at[t], sibling) for t in range(T)]
        return sends, lands

    def start(self, in_refs, out_refs, sems):
        for cp in self._copies(in_refs, out_refs, sems)[0]:
            cp.start()

    def finish(self, in_refs, out_refs, sems):
        sends, lands = self._copies(in_refs, out_refs, sems)
        for cp in lands:
            cp.wait_recv()
        for cp in sends:
            cp.wait_send()


def _pair_sum(own, landed, c_arr, name):
    _, ns, r, cc = own.shape
    rows = ns * r
    tr = _fit(512, rows)

    def body(c_ref, a_ref, b_ref, o_ref, o2_ref):
        total = (a_ref[...].astype(F32) + b_ref[...].astype(F32)).astype(o_ref.dtype)
        o_ref[...] = total
        o2_ref[...] = total

    tile = pl.BlockSpec((tr, cc), lambda i, c_ref: (i, 0))
    out = pl.pallas_call(
        body, name=name, out_shape=[jax.ShapeDtypeStruct((rows, cc), BF16)] * 2,
        grid_spec=pltpu.PrefetchScalarGridSpec(
            num_scalar_prefetch=1, grid=(rows // tr,),
            in_specs=[pl.BlockSpec((None, tr, cc), lambda i, c_ref: (c_ref[0], i, 0)), tile], out_specs=[tile, tile]),
        compiler_params=pltpu.CompilerParams(dimension_semantics=("parallel",), vmem_limit_bytes=_vmem(8 * tr * cc * 4)),
    )(c_arr, own.reshape(2, rows, cc), landed.reshape(rows, cc))
    return out[0].reshape(ns, r, cc), out[1].reshape(ns, r, cc)


def _chip_sum(parts, c_arr, name):
    _, r, cc = parts.shape
    tr = _fit(256, r)

    def body(c_ref, p_ref, o_ref):
        acc = p_ref[0].astype(F32)
        for j in range(1, N_CHIPS):
            acc = acc + p_ref[j].astype(F32)
        o_ref[...] = acc

    return pl.pallas_call(
        body, name=name, out_shape=jax.ShapeDtypeStruct((2, r, cc), F32),
        grid_spec=pltpu.PrefetchScalarGridSpec(
            num_scalar_prefetch=1, grid=(r // tr,),
            in_specs=[pl.BlockSpec((N_CHIPS, tr, cc), lambda i, c_ref: (0, i, 0))],
            out_specs=pl.BlockSpec((None, tr, cc), lambda i, c_ref: (c_ref[0], i, 0))),
        compiler_params=pltpu.CompilerParams(dimension_semantics=("parallel",), vmem_limit_bytes=_vmem(12 * tr * cc * 4)),
    )(c_arr, parts)


def _adamw_math(w, g, m, v):
    m2 = ADAM_B1 * m + (1.0 - ADAM_B1) * g
    v2 = ADAM_B2 * v + (1.0 - ADAM_B2) * jnp.square(g)
    m_hat = m2 / (1.0 - ADAM_B1 ** ADAM_STEP)
    v_hat = v2 / (1.0 - ADAM_B2 ** ADAM_STEP)
    delta = -ADAM_LR * (m_hat / (jnp.sqrt(v_hat) + ADAM_EPS) + ADAM_WD * w)
    return delta, m2, v2


def _adamw_mat(g2, w, m, v, layer, kind, prev, name):
    shape = w.shape
    ks, ns = shape[-2:]
    _, r, cc = g2.shape
    tr, tc = _fit(256, r), _fit(1024, cc)
    assert (r, cc) == ((ks // 2, ns) if kind == "col" else (ks, ns // 2))
    assert r % tr == 0 and cc % tc == 0
    rb, cb = r // tr, cc // tc
    if kind == "col":
        g_spec = pl.BlockSpec((None, tr, tc), lambda i, j: (i // rb, i % rb, j))
    else:
        g_spec = pl.BlockSpec((None, tr, tc), lambda i, j: (j // cb, i, j % cb))
    if w.ndim == 3:
        w_spec = pl.BlockSpec((None, tr, tc), lambda i, j: (layer, i, j))
    else:
        w_spec = pl.BlockSpec((tr, tc), lambda i, j: (i, j))
    n_prev = 0 if prev is None else 4

    def body(*refs):
        g_ref, w_ref, m_ref, v_ref = refs[:4]
        go_ref, d_ref, mo_ref, vo_ref = refs[4 + n_prev:]
        g = g_ref[...]
        delta, m2, v2 = _adamw_math(w_ref[...], g, m_ref[...], v_ref[...])
        go_ref[...] = g
        d_ref[...] = delta
        mo_ref[...] = m2
        vo_ref[...] = v2

    return pl.pallas_call(
        body, name=name, grid=(ks // tr, ns // tc),
        in_specs=[g_spec, w_spec, w_spec, w_spec] + [ANY] * n_prev, out_specs=[w_spec] * 4,
        out_shape=[jax.ShapeDtypeStruct(shape, F32)] * 4,
        input_output_aliases={4 + i: i for i in range(n_prev)},
        compiler_params=pltpu.CompilerParams(dimension_semantics=("parallel", "parallel"),
                                             vmem_limit_bytes=_vmem(16 * tr * tc * 4)),
    )(g2, w, m, v, *(prev or ()))


def _adamw_small(g, w, m, v, name):
    def body(g_ref, w_ref, m_ref, v_ref, d_ref, mo_ref, vo_ref):
        delta, m2, v2 = _adamw_math(w_ref[...], g_ref[...], m_ref[...], v_ref[...])
        d_ref[...] = delta
        mo_ref[...] = m2
        vo_ref[...] = v2

    return pl.pallas_call(body, name=name, out_shape=[jax.ShapeDtypeStruct(w.shape, F32)] * 3)(g, w, m, v)


def _sum8(parts, name):
    def body(p_ref, o_ref):
        acc = p_ref[0]
        for j in range(1, 8):
            acc = acc + p_ref[j]
        o_ref[...] = acc

    return pl.pallas_call(body, name=name, out_shape=jax.ShapeDtypeStruct(parts.shape[1:], F32),
                          compiler_params=pltpu.CompilerParams(vmem_limit_bytes=_vmem(12 * _nbytes(parts.shape[1:], F32))))(parts)


_REDUCE_AT = {
    "d_ple_gate1": (("A", "ple_proj1"),),
    "dh2_1": (("A", "ple_gate1"),),
    "d_mlp_down1": (("B", "ple_proj1"), ("B", "ple_gate1")),
    "dt1": (("A", "mlp_down1"),),
    "d_mlp_up1": (("B", "mlp_down1"), ("C", "ple_proj1"), ("C", "ple_gate1")),
    "dh1_1": (("A", "mlp_up1"),),
    "d_attn_w_o": (("C", "mlp_down1"),),
    "attn_do": (("A", "attn_w_o"),),
    "attn_bwd_g0": (("B", "attn_w_o"),),
    "attn_unrot": (("C", "attn_w_o"),),
    "dx1_q": (("A", "attn_w_q"),),
    "dkvn_k": (("A", "w_kv"),),
    "d_ple_gate0": (("A", "ple_proj0"),),
    "dh2_0": (("A", "ple_gate0"),),
    "d_mlp_down0": (("B", "mlp_up1"), ("B", "ple_proj0")),
    "dt0": (("B", "w_kv"), ("B", "ple_gate0"), ("A", "mlp_down0")),
    "d_mlp_up0": (("B", "mlp_down0"), ("C", "mlp_up1"), ("C", "ple_proj0"), ("C", "w_kv"), ("C", "ple_gate0")),
    "dh1_0": (("A", "mlp_up0"), ("B", "attn_w_q")),
    "d_conv_w_out": (("C", "mlp_down0"), ("C", "attn_w_q")),
    "conv_ds": (("A", "conv_w_out"),),
    "conv_bwd": (("B", "mlp_up0"), ("B", "conv_w_out")),
    "d_conv_w_in_g": (("C", "mlp_up0"), ("C", "conv_w_out")),
    "dx_a": (("A", "conv_w_in"),),
    "dx_g": (("B", "conv_w_in"),),
    "share_last": (("C", "conv_w_in"),),
}


class _Reducer:
    def __init__(self, w, mom, var, c_arr):
        self.w, self.mom, self.var, self.c_arr = w, mom, var, c_arr
        self.mats = {m[0]: m for m in _MATS}
        self.grads, self.pair_sums, self.chip_sums, self.out = {}, {}, {}, {}

    def produced(self, name, grad):
        self.grads[name] = grad

    def carry(self, call):
        parts = []
        for cls, stage, src in ((_PairSend, "A", self.grads), (_ChipScatter, "B", self.pair_sums),
                                (_PairShare, "C", self.chip_sums)):
            names = [n for s, n in _REDUCE_AT.get(call, ()) if s == stage]
            if names:
                parts.append((stage, cls({n: src[n] for n in names})))
        self._parts = parts
        return _Multi([p for _, p in parts]) if parts else None

    def carried(self):
        for stage, part in self._parts:
            for name, val in part.result.items():
                if stage == "A":
                    self.pair_sums[name] = _pair_sum(self.grads[name], val, self.c_arr, f"pair_sum_{name}")
                elif stage == "B":
                    self.chip_sums[name] = _chip_sum(val, self.c_arr, f"chip_sum_{name}")
                else:
                    _, src, layer, kind, _ = self.mats[name]
                    self.out[src] = _adamw_mat(val, self.w[src], self.mom[src], self.var[src], layer or 0, kind,
                                               self.out.get(src), f"adamw_{name}")
        self._parts = []


_WEIGHTS = ("conv_w_in", "conv_b_in", "conv_dw", "conv_dw_b", "conv_ln_g", "conv_ln_b", "conv_w_out", "kv_ln_g",
            "kv_ln_b", "w_kv", "attn_w_q", "attn_w_o", "ln1_g", "ln1_b", "mlp_up", "mlp_down", "ln2_g", "ln2_b",
            "ple_proj", "ple_gate")
_SHARDED_VECS = ("conv_b_in", "conv_dw", "conv_dw_b", "conv_ln_g", "conv_ln_b")
_REPLICATED_VECS = ("kv_ln_g", "kv_ln_b", "ln1_g", "ln1_b", "ln2_g", "ln2_b")


def _pad_rows(a, rows):
    return jnp.concatenate([a, jnp.zeros((rows - a.shape[0], a.shape[1]), a.dtype)], axis=0) if a.shape[0] < rows else a


def _pack_sharded(d):
    n = d["conv_dw_b"].shape[-1]
    rows = [d["conv_b_in"].reshape(2, n), d["conv_dw"].reshape(CONV_WIDTH, n), d["conv_dw_b"].reshape(1, n),
            d["conv_ln_g"].reshape(1, n), d["conv_ln_b"].reshape(1, n)]
    return _pad_rows(jnp.concatenate(rows, axis=0), 40)


def _unpack_sharded(pack, like):
    n = pack.shape[1]
    return {"conv_b_in": pack[0:2].reshape(like["conv_b_in"].shape),
            "conv_dw": pack[2:2 + CONV_WIDTH].reshape(like["conv_dw"].shape),
            "conv_dw_b": pack[33:34].reshape(like["conv_dw_b"].shape),
            "conv_ln_g": pack[34:35].reshape(like["conv_ln_g"].shape),
            "conv_ln_b": pack[35:36].reshape(like["conv_ln_b"].shape)}


def _pack_replicated(d):
    D = d["kv_ln_g"].shape[-1]
    rows = [d[n].reshape(-1, D) for n in _REPLICATED_VECS]
    return _pad_rows(jnp.concatenate(rows, axis=0), 16)


def _unpack_replicated(pack, like):
    out, r = {}, 0
    for n in _REPLICATED_VECS:
        k = like[n].size // pack.shape[1]
        out[n] = pack[r:r + k].reshape(like[n].shape)
        r += k
    return out


def kernel(x, p, positions, conv_w_in, conv_b_in, conv_dw, conv_dw_b, conv_ln_g, conv_ln_b, conv_w_out, kv_ln_g, kv_ln_b, w_kv, attn_w_q, attn_w_o, ln1_g, ln1_b, mlp_up, mlp_down, ln2_g, ln2_b, ple_proj, ple_gate, loss_target, m_conv_w_in, m_conv_b_in, m_conv_dw, m_conv_dw_b, m_conv_ln_g, m_conv_ln_b, m_conv_w_out, m_kv_ln_g, m_kv_ln_b, m_w_kv, m_attn_w_q, m_attn_w_o, m_ln1_g, m_ln1_b, m_mlp_up, m_mlp_down, m_ln2_g, m_ln2_b, m_ple_proj, m_ple_gate, v_conv_w_in, v_conv_b_in, v_conv_dw, v_conv_dw_b, v_conv_ln_g, v_conv_ln_b, v_conv_w_out, v_kv_ln_g, v_kv_ln_b, v_w_kv, v_attn_w_q, v_attn_w_o, v_ln1_g, v_ln1_b, v_mlp_up, v_mlp_down, v_ln2_g, v_ln2_b, v_ple_proj, v_ple_gate):
    args = dict(locals())
    w = {n: args[n] for n in _WEIGHTS}
    mom = {n: args["m_" + n] for n in _WEIGHTS}
    var = {n: args["v_" + n] for n in _WEIGHTS}
    S, D = x.shape[1:]
    n4 = D // N_CHIPS
    chip = 2 * lax.axis_index("x") + lax.axis_index("y")
    c_arr = lax.axis_index("c").astype(jnp.int32).reshape(1)

    shards = {n: w[n].astype(BF16) for n in sorted({m[1] for m in _MATS})}
    vec_all = _allgather8(_pack_sharded(w), "gather_vectors")
    vec_full = jnp.concatenate([vec_all[2 * j] for j in range(N_CHIPS)], axis=1)
    b_in = vec_all[0::2, 0:2, :].reshape(1, 2 * D)
    V = {"conv_b_a": b_in[:, :D], "conv_b_g": b_in[:, D:],
         "conv_dw": _pad_rows(vec_full[2:2 + CONV_WIDTH], CONV_PAD), "conv_dw_b": vec_full[33:34],
         "conv_ln_g": vec_full[34:35], "conv_ln_b": vec_full[35:36],
         "kv_ln_g": kv_ln_g.reshape(1, D), "kv_ln_b": kv_ln_b.reshape(1, D)}
    for l in range(2):
        for n in ("ln1_g", "ln1_b", "ln2_g", "ln2_b"):
            V[f"{n}{l}"] = w[n][l].reshape(1, D)

    half = HEAD_DIM // 2
    inv_freq = ROPE_THETA ** (-jnp.arange(half, dtype=F32) * (2.0 / HEAD_DIM))
    ang = positions[0].astype(F32)[:, None] * inv_freq
    cos, sin = jnp.cos(ang), jnp.sin(ang)
    cosf = jnp.concatenate([cos, cos], axis=-1)
    sinf = jnp.concatenate([-sin, sin], axis=-1)

    reducer = _Reducer(w, mom, var, c_arr)
    loss_cols, grad_x, _, gv = _local_step(x[0], p[:, 0], cosf, sinf, loss_target[0], None, V, shards, reducer,
                                           chip.astype(jnp.int32).reshape(1))
    loss = lax.psum(jnp.sum(loss_cols), ("x", "y", "c"))
    out = dict(reducer.out)

    gpack = jnp.concatenate([gv["conv_b_a"], gv["conv_b_g"], gv["conv_dw"][:CONV_WIDTH], gv["conv_dw_b"],
                             gv["conv_ln_g"], gv["conv_ln_b"], gv["kv_ln_g"], gv["kv_ln_b"],
                             gv["ln1_g0"], gv["ln1_g1"], gv["ln1_b0"], gv["ln1_b1"],
                             gv["ln2_g0"], gv["ln2_g1"], gv["ln2_b0"], gv["ln2_b1"]], axis=0)
    gsum = _sum8(_allgather8(_pad_rows(gpack, 48), "gather_vector_grads"), "sum_vector_grads")
    g_b = lax.dynamic_slice_in_dim(jnp.concatenate([gsum[0:1], gsum[1:2]], axis=1), chip * 2 * n4, 2 * n4, axis=1)
    g_sh = lax.dynamic_slice_in_dim(gsum[2:36], chip * n4, n4, axis=1)
    g_sh = _pad_rows(jnp.concatenate([g_b.reshape(2, n4), g_sh], axis=0), 40)
    d_sh, m_sh, v_sh = _adamw_small(g_sh, _pack_sharded(w), _pack_sharded(mom), _pack_sharded(var), "adamw_sharded_vectors")
    g_rep = _pad_rows(gsum[36:46], 16)
    d_rep, m_rep, v_rep = _adamw_small(g_rep, _pack_replicated(w), _pack_replicated(mom), _pack_replicated(var),
                                       "adamw_replicated_vectors")
    small = {}
    for i, (sh, rep) in enumerate(((g_sh, g_rep), (d_sh, d_rep), (m_sh, m_rep), (v_sh, v_rep))):
        d = {**_unpack_sharded(sh, w), **_unpack_replicated(rep, w)}
        for n, val in d.items():
            small.setdefault(n, [None] * 4)[i] = val
    for n in small:
        out[n] = small[n]

    res = [loss, grad_x[None]]
    for i in range(4):
        res += [out[n][i] for n in _WEIGHTS]
    return tuple(res)
```

```python
import functools

import jax
import jax.numpy as jnp
from jax import lax
from jax.experimental import pallas as pl
from jax.experimental.pallas import tpu as pltpu

F32 = jnp.float32
BF16 = jnp.bfloat16

HEAD_DIM = 128
ATTN_BLOCK = 128
DILATIONS = (1, 4, 16)
N_GROUPS = 3
CONV_WIDTH = 31
CONV_PAD = 32
CONV_ROWS = 32
EPILOGUE_ROWS = 128
ROPE_THETA = 10000.0
LN_EPS = 1e-5
ALPHA = 4.0 ** 0.25
ATTN_SCALE = HEAD_DIM ** -0.5
NEG = -1e30

ADAM_LR = 0.001
ADAM_B1 = 0.9
ADAM_B2 = 0.999
ADAM_EPS = 1e-08
ADAM_WD = 0.01
ADAM_STEP = 10

N_CHIPS = 4
VMEM_CAP = 60 << 20
MESH = pl.DeviceIdType.MESH
ANY = pl.BlockSpec(memory_space=pl.ANY)


def _vmem(nbytes):
    return int(min(max(2 * nbytes + (8 << 20), 24 << 20), VMEM_CAP))


def _fit(tile, n):
    if n <= tile:
        return n
    t = tile - tile % 128
    while n % t:
        t -= 128
    return t


def _nbytes(shape, dtype):
    n = 1
    for s in shape:
        n *= s
    return n * jnp.dtype(dtype).itemsize


_DIMS = {"nn": (((1,), (0,)), ((), ())), "nt": (((1,), (1,)), ((), ())), "tn": (((0,), (0,)), ((), ()))}


def _pcall(body, *, name, grid, in_specs, out_specs, out_shape, operands, scratch_shapes=(), vmem, carry=None):
    if carry is None:
        return pl.pallas_call(
            body, name=name, grid=grid, in_specs=in_specs, out_specs=out_specs, out_shape=out_shape,
            scratch_shapes=list(scratch_shapes),
            compiler_params=pltpu.CompilerParams(dimension_semantics=("arbitrary",) * len(grid), vmem_limit_bytes=vmem),
        )(*operands)
    n_in, n_out, n_scr = len(in_specs), len(out_specs), len(scratch_shapes)
    c_in, c_out = len(carry.ins), len(carry.out_shape)

    def wrapped(*refs):
        ins, refs = refs[:n_in], refs[n_in:]
        c_ins, refs = refs[:c_in], refs[c_in:]
        outs, refs = refs[:n_out], refs[n_out:]
        c_outs, refs = refs[:c_out], refs[c_out:]
        scr, c_sems = refs[:n_scr], refs[n_scr:]
        first = functools.reduce(jnp.logical_and, [pl.program_id(d) == 0 for d in range(len(grid))])
        last = functools.reduce(jnp.logical_and, [pl.program_id(d) == grid[d] - 1 for d in range(len(grid))])
        pl.when(first)(lambda: carry.start(c_ins, c_outs, c_sems))
        body(*ins, *outs, *scr)
        pl.when(last)(lambda: carry.finish(c_ins, c_outs, c_sems))

    res = pl.pallas_call(
        wrapped, name=name, grid=grid, in_specs=list(in_specs) + [ANY] * c_in, out_specs=list(out_specs) + [ANY] * c_out,
        out_shape=list(out_shape) + list(carry.out_shape), scratch_shapes=list(scratch_shapes) + list(carry.scratch),
        input_output_aliases={len(operands) + i: n_out + o for i, o in carry.aliases.items()},
        compiler_params=pltpu.CompilerParams(dimension_semantics=("arbitrary",) * len(grid), vmem_limit_bytes=vmem),
    )(*operands, *carry.ins)
    carry.set_result(res[n_out:])
    return res[:n_out]


def _mm(a, b, *, mode, outs, name, epilogue=None, extras=(), rextras=(), vecs=(), a_sel=None, b_sel=None,
        tm=None, tn=2048, tk=None, layout=None, carry=None, ep_rows=None, a_fn=None, sums=0):
    a2, b2 = a.shape[-2:], b.shape[-2:]
    if mode == "nn":
        (M, K), (K2, N) = a2, b2
    elif mode == "nt":
        (M, K), (N, K2) = a2, b2
    else:
        (K, M), (K2, N) = a2, b2
    assert K == K2, (a.shape, b.shape, mode)
    if tm is None:
        tm = 1024 if mode == "tn" else 512
    if tk is None:
        tk = 1024 if mode == "tn" else 2048
    if layout is not None:
        kind, nslots = layout
        r, c = (M // 2, N // nslots) if kind == "col" else (M // nslots, N // 2)
        tm, tn = _fit(tm, r), _fit(tn, c)
        assert r % tm == 0 and c % tn == 0
    else:
        tm, tn = _fit(tm, M), _fit(tn, N)
    tk = _fit(tk, K)
    assert M % tm == 0 and N % tn == 0 and K % tk == 0, (M, N, K, tm, tn, tk)
    nk = K // tk
    grid = (N // tn, M // tm, nk)

    def spec(arr, sel, blk, imap):
        if arr.ndim == 3:
            return pl.BlockSpec((None,) + blk, lambda j, i, k: (sel,) + imap(j, i, k))
        return pl.BlockSpec(blk, imap)

    if mode == "tn":
        a_spec = spec(a, a_sel, (tk, tm), lambda j, i, k: (k, i))
    else:
        a_spec = spec(a, a_sel, (tm, tk), lambda j, i, k: (i, k))
    if mode == "nt":
        b_spec = spec(b, b_sel, (tn, tk), lambda j, i, k: (j, k))
    else:
        b_spec = spec(b, b_sel, (tk, tn), lambda j, i, k: (k, j))
    in_specs = [a_spec, b_spec]
    in_specs += [pl.BlockSpec((tm, tn), lambda j, i, k: (i, j)) for _ in extras]
    in_specs += [pl.BlockSpec((tm, e.shape[1]), lambda j, i, k: (i, 0)) for e in rextras]
    in_specs += [pl.BlockSpec((1, tn), lambda j, i, k: (0, j)) for _ in vecs]

    if layout is None:
        out_shape = [jax.ShapeDtypeStruct((M, N), d) for d in outs] + [jax.ShapeDtypeStruct((1, N), F32)] * sums
        out_specs = [pl.BlockSpec((tm, tn), lambda j, i, k: (i, j)) for _ in outs]
        out_specs += [pl.BlockSpec((1, tn), lambda j, i, k: (0, j))] * sums
    else:
        assert len(outs) == 1
        out_shape = [jax.ShapeDtypeStruct((2, nslots, r, c), outs[0])]
        rb, cb = r // tm, c // tn
        if kind == "col":
            omap = lambda j, i, k: (i // rb, j // cb, i % rb, j % cb)
        else:
            omap = lambda j, i, k: (j // cb, i // rb, i % rb, j % cb)
        out_specs = [pl.BlockSpec((None, None, tm, tn), omap)]

    ne, nr, nv, no = len(extras), len(rextras), len(vecs), len(outs)
    dims = _DIMS[mode]

    def body(*refs):
        a_ref, b_ref = refs[0], refs[1]
        rest = refs[2:2 + ne + nr + nv]
        o_refs = refs[2 + ne + nr + nv:2 + ne + nr + nv + no]
        s_refs = refs[2 + ne + nr + nv + no:2 + ne + nr + nv + no + sums]

        def finish(total):
            if epilogue is None:
                for o in o_refs:
                    o[...] = total.astype(o.dtype)
                return
            step = min(ep_rows or tm, tm)
            col_sums = [None] * sums
            for r0 in range(0, tm, step):
                rows = slice(r0, r0 + step)
                tiles = [x[rows, :] for x in rest[:ne + nr]] + [x[...] for x in rest[ne + nr:]]
                res = epilogue(total[rows, :], *tiles)
                for o, val in zip(o_refs, res[:no]):
                    o[rows, :] = val.astype(o.dtype)
                for n_, val in enumerate(res[no:]):
                    part_sum = jnp.sum(val, axis=0, keepdims=True)
                    col_sums[n_] = part_sum if col_sums[n_] is None else col_sums[n_] + part_sum
            if sums:
                @pl.when(pl.program_id(1) == 0)
                def _():
                    for s_ref, val in zip(s_refs, col_sums):
                        s_ref[...] = val

                @pl.when(pl.program_id(1) > 0)
                def _():
                    for s_ref, val in zip(s_refs, col_sums):
                        s_ref[...] += val

        def product():
            a_tile = a_ref[...] if a_fn is None else a_fn(a_ref[...])
            return lax.dot_general(a_tile.astype(BF16), b_ref[...].astype(BF16), dims, preferred_element_type=F32)

        if nk == 1:
            finish(product())
            return
        acc = refs[-1]
        k = pl.program_id(2)

        @pl.when(k == 0)
        def _():
            acc[...] = product()

        @pl.when(k > 0)
        def _():
            acc[...] += product()

        @pl.when(k == nk - 1)
        def _():
            finish(acc[...])

    blk = (_nbytes((tm, tk), a.dtype) + _nbytes((tk, tn), b.dtype) + sum(_nbytes((tm, tn), e.dtype) for e in extras)
           + sum(_nbytes((tm, tn), d) for d in outs) + 2 * tm * tn * 4)
    res = _pcall(body, name=name, grid=grid, in_specs=in_specs, out_specs=out_specs, out_shape=out_shape,
                 operands=(a, b, *extras, *rextras, *vecs),
                 scratch_shapes=[pltpu.VMEM((tm, tn), F32)] if nk > 1 else [], vmem=_vmem(blk), carry=carry)
    return res[0] if no + sums == 1 else tuple(res)


def _rows(fn, rows, vecs, outs, sums, *, tm, name, carry=None):
    S = rows[0].shape[0]
    tm = min(tm, S)
    assert S % tm == 0
    nr, nv, no, ns = len(rows), len(vecs), len(outs), len(sums)

    def body(*refs):
        vals = fn(*[r[...] for r in refs[:nr + nv]])
        o_refs = refs[nr + nv:nr + nv + no]
        s_refs = refs[nr + nv + no:]
        for o, val in zip(o_refs, vals[:no]):
            o[...] = val.astype(o.dtype)
        if ns:
            @pl.when(pl.program_id(0) == 0)
            def _():
                for s in s_refs:
                    s[...] = jnp.zeros_like(s)

            for s, val in zip(s_refs, vals[no:]):
                s[...] += jnp.sum(val.astype(F32), axis=0, keepdims=True)

    in_specs = [pl.BlockSpec((tm, r.shape[1]), lambda i: (i, 0)) for r in rows]
    in_specs += [pl.BlockSpec(v.shape, lambda i: (0, 0)) for v in vecs]
    out_specs = [pl.BlockSpec((tm, c), lambda i: (i, 0)) for c, _ in outs]
    out_specs += [pl.BlockSpec((1, c), lambda i: (0, 0)) for c in sums]
    out_shape = [jax.ShapeDtypeStruct((S, c), d) for c, d in outs]
    out_shape += [jax.ShapeDtypeStruct((1, c), F32) for c in sums]
    blk = sum(_nbytes((tm, r.shape[1]), r.dtype) for r in rows) + sum(_nbytes((tm, c), d) for c, d in outs)
    blk += 6 * tm * max(r.shape[1] for r in rows) * 4
    res = _pcall(body, name=name, grid=(S // tm,), in_specs=in_specs, out_specs=out_specs, out_shape=out_shape,
                 operands=(*rows, *vecs), vmem=_vmem(blk), carry=carry)
    return tuple(res)


def _ln_norm(z):
    mu = jnp.mean(z, axis=-1, keepdims=True)
    d = z - mu
    var = jnp.mean(d * d, axis=-1, keepdims=True)
    rstd = lax.rsqrt(var + LN_EPS)
    return d * rstd, rstd


def _ln(z, g, b):
    return _ln_norm(z)[0] * g + b


def _ln_bwd(dy, n, rstd, g):
    dn = dy * g
    return rstd * (dn - jnp.mean(dn, axis=-1, keepdims=True) - n * jnp.mean(dn * n, axis=-1, keepdims=True))


def _sq_relu(t):
    return jnp.square(jnp.maximum(t.astype(F32), 0.0))


def _sigmoid(x):
    return 1.0 / (1.0 + jnp.exp(-x))


def _per_head(x, fn):
    h = x.shape[1] // HEAD_DIM
    return jnp.concatenate([fn(x[:, i * HEAD_DIM:(i + 1) * HEAD_DIM], i) for i in range(h)], axis=1)


def _rot(x, cosf, sinf):
    return _per_head(x, lambda xh, i: xh * cosf + pltpu.roll(xh, HEAD_DIM // 2, 1) * sinf)


def _rot_t(dy, cosf, sinf):
    return _per_head(dy, lambda dh, i: dh * cosf + pltpu.roll(dh * sinf, HEAD_DIM // 2, 1))


def _shift_copies(win):
    rows = win.shape[1] - 8
    for s in range(1, 8):
        win[s, 0:rows, :] = win[0, s:s + rows, :]


def _rows_at(win, start):
    s = start % 8
    return win[s, start - s:start - s + CONV_ROWS, :]


def _conv_fwd(glu, dw, dwb, *, tm=256, tc=512, name="conv_fwd", carry=None):
    S, D = glu.shape
    tm, tc = min(tm, S), min(tc, D)
    ni = S // tm

    def body(cur_ref, prev_ref, dw_ref, dwb_ref, o_ref, win):
        i = pl.program_id(1)
        tail = prev_ref[tm - CONV_PAD:tm, :]
        win[0, 0:CONV_PAD, :] = jnp.where(i > 0, tail, jnp.zeros_like(tail))
        win[0, CONV_PAD:CONV_PAD + tm, :] = cur_ref[...]
        _shift_copies(win)
        first = CONV_PAD - CONV_WIDTH + 1
        for r0 in range(0, tm, CONV_ROWS):
            acc = jnp.zeros((CONV_ROWS, tc), F32) + dwb_ref[...]
            for k in range(CONV_WIDTH):
                acc = acc + _rows_at(win, r0 + first + k) * dw_ref[k:k + 1, :]
            o_ref[r0:r0 + CONV_ROWS, :] = acc

    return _pcall(
        body, name=name, grid=(D // tc, ni),
        in_specs=[pl.BlockSpec((tm, tc), lambda j, i: (i, j)),
                  pl.BlockSpec((tm, tc), lambda j, i: (jnp.maximum(i - 1, 0), j)),
                  pl.BlockSpec((CONV_PAD, tc), lambda j, i: (0, j)),
                  pl.BlockSpec((1, tc), lambda j, i: (0, j))],
        out_specs=[pl.BlockSpec((tm, tc), lambda j, i: (i, j))],
        out_shape=[jax.ShapeDtypeStruct((S, D), F32)],
        scratch_shapes=[pltpu.VMEM((8, tm + CONV_PAD, tc), F32)],
        operands=(glu, glu, dw, dwb), vmem=_vmem(8 * tm * tc * 4), carry=carry)[0]


def _conv_bwd(dc, glu, a_pre, g_pre, dw, ba, bg, *, tm=256, tc=512, name="conv_bwd", carry=None):
    S, D = dc.shape
    tm, tc = min(tm, S), min(tc, D)
    ni = S // tm

    def fold8(v):
        out = v[0:8]
        for r in range(8, CONV_ROWS, 8):
            out = out + v[r:r + 8]
        return out

    def body(dc_ref, dcn_ref, glu_ref, glup_ref, a_ref, g_ref, dw_ref, ba_ref, bg_ref,
             da_ref, dg_ref, ddw_ref, dba_ref, dbg_ref, dwin, gwin, taps):
        i = pl.program_id(1)

        @pl.when(i == 0)
        def _():
            ddw_ref[...] = jnp.zeros_like(ddw_ref)
            dba_ref[...] = jnp.zeros_like(dba_ref)
            dbg_ref[...] = jnp.zeros_like(dbg_ref)

        head = dcn_ref[0:CONV_PAD, :]
        dwin[0, 0:tm, :] = dc_ref[...]
        dwin[0, tm:tm + CONV_PAD, :] = jnp.where(i < ni - 1, head, jnp.zeros_like(head))
        tail = glup_ref[tm - CONV_PAD:tm, :]
        gwin[0, 0:CONV_PAD, :] = jnp.where(i > 0, tail, jnp.zeros_like(tail))
        gwin[0, CONV_PAD:CONV_PAD + tm, :] = glu_ref[...]
        _shift_copies(dwin)
        _shift_copies(gwin)
        taps[...] = jnp.zeros_like(taps)
        first = CONV_PAD - CONV_WIDTH + 1
        sum_a = jnp.zeros((8, tc), F32)
        sum_g = jnp.zeros((8, tc), F32)
        for r0 in range(0, tm, CONV_ROWS):
            dcur = dc_ref[r0:r0 + CONV_ROWS, :]
            dglu = jnp.zeros((CONV_ROWS, tc), F32)
            for k in range(CONV_WIDTH):
                dglu = dglu + _rows_at(dwin, r0 + CONV_WIDTH - 1 - k) * dw_ref[k:k + 1, :]
                taps[k] += fold8(dcur * _rows_at(gwin, r0 + first + k))
            a = a_ref[r0:r0 + CONV_ROWS, :] + ba_ref[...]
            sg = _sigmoid(g_ref[r0:r0 + CONV_ROWS, :] + bg_ref[...])
            da = dglu * sg
            dg = dglu * a * sg * (1.0 - sg)
            da_ref[r0:r0 + CONV_ROWS, :] = da.astype(BF16)
            dg_ref[r0:r0 + CONV_ROWS, :] = dg.astype(BF16)
            sum_a = sum_a + fold8(da)
            sum_g = sum_g + fold8(dg)
        ddw_ref[...] += jnp.sum(taps[...], axis=1)
        dba_ref[...] += jnp.sum(sum_a, axis=0, keepdims=True)
        dbg_ref[...] += jnp.sum(sum_g, axis=0, keepdims=True)

    tile = lambda f: pl.BlockSpec((tm, tc), f)
    vec = pl.BlockSpec((1, tc), lambda j, i: (0, j))
    return _pcall(
        body, name=name, grid=(D // tc, ni),
        in_specs=[tile(lambda j, i: (i, j)), tile(lambda j, i: (jnp.minimum(i + 1, ni - 1), j)),
                  tile(lambda j, i: (i, j)), tile(lambda j, i: (jnp.maximum(i - 1, 0), j)),
                  tile(lambda j, i: (i, j)), tile(lambda j, i: (i, j)),
                  pl.BlockSpec((CONV_PAD, tc), lambda j, i: (0, j)), vec, vec],
        out_specs=[tile(lambda j, i: (i, j)), tile(lambda j, i: (i, j)),
                   pl.BlockSpec((CONV_PAD, tc), lambda j, i: (0, j)), vec, vec],
        out_shape=[jax.ShapeDtypeStruct((S, D), BF16), jax.ShapeDtypeStruct((S, D), BF16),
                   jax.ShapeDtypeStruct((CONV_PAD, D), F32), jax.ShapeDtypeStruct((1, D), F32),
                   jax.ShapeDtypeStruct((1, D), F32)],
        scratch_shapes=[pltpu.VMEM((8, tm + CONV_PAD, tc), F32), pltpu.VMEM((8, tm + CONV_PAD, tc), F32),
                        pltpu.VMEM((CONV_PAD, 8, tc), F32)],
        operands=(dc, dc, glu, glu, a_pre, g_pre, dw, ba, bg), vmem=_vmem(16 * tm * tc * 4), carry=carry)


def _nt(a, b):
    return lax.dot_general(a, b, _DIMS["nt"], preferred_element_type=F32)


def _tn(a, b):
    return lax.dot_general(a, b, _DIMS["tn"], preferred_element_type=F32)


HEADS_TOGETHER = 8
STAT_LANES = 128


def _per_head_pack(cols):
    rows = cols[0].shape[0]
    lane = lax.broadcasted_iota(jnp.int32, (rows, STAT_LANES), 1)
    out = jnp.zeros((rows, STAT_LANES), F32)
    for h, col in enumerate(cols):
        out = jnp.where(lane == h, col, out)
    return out


def _window_mask(qi, kj, first_key):
    B = ATTN_BLOCK
    return ((kj < B) & (kj >= qi) & (kj >= first_key)) | ((kj >= B) & (kj - B <= qi))


def _attn_fwd(q_rot, k, v, g, dil):
    S, D = k.shape
    H = D // HEAD_DIM
    L = S // dil
    nb_count = L // ATTN_BLOCK
    B = ATTN_BLOCK

    def body(q_ref, kc_ref, kp_ref, vc_ref, vp_ref, o_ref, lse_ref):
        nb = pl.program_id(1)
        qi = lax.broadcasted_iota(jnp.int32, (B, 2 * B), 0)
        kj = lax.broadcasted_iota(jnp.int32, (B, 2 * B), 1)
        valid = _window_mask(qi, kj, jnp.where(nb > 0, 0, B))
        stats = []
        for h0 in range(0, H, HEADS_TOGETHER):
            heads = range(h0, min(h0 + HEADS_TOGETHER, H))
            hs = [slice(h * HEAD_DIM, (h + 1) * HEAD_DIM) for h in heads]
            kk = [jnp.concatenate([kp_ref[:, c], kc_ref[:, c]], axis=0) for c in hs]
            vv = [jnp.concatenate([vp_ref[:, c], vc_ref[:, c]], axis=0) for c in hs]
            s = [jnp.where(valid, _nt(q_ref[:, c], kk_) * ATTN_SCALE, NEG) for c, kk_ in zip(hs, kk)]
            m = [jnp.max(s_, axis=1, keepdims=True) for s_ in s]
            p = [jnp.exp(s_ - m_) for s_, m_ in zip(s, m)]
            l = [jnp.sum(p_, axis=1, keepdims=True) for p_ in p]
            o = [jnp.dot(p_.astype(BF16), vv_, preferred_element_type=F32) / l_ for p_, vv_, l_ in zip(p, vv, l)]
            for c, o_ in zip(hs, o):
                o_ref[:, c] = o_.astype(o_ref.dtype)
            stats += [m_ + jnp.log(l_) for m_, l_ in zip(m, l)]
        lse_ref[...] = _per_head_pack(stats)

    blk = lambda f: pl.BlockSpec((B, D), f)
    cur = lambda r, nb: (nb, r)
    prev = lambda r, nb: (jnp.maximum(nb - 1, 0), r)
    o, lse = pl.pallas_call(
        body, name=f"attn_fwd_g{g}", grid=(dil, nb_count),
        in_specs=[blk(lambda r, nb: (nb, r * N_GROUPS + g)), blk(cur), blk(prev), blk(cur), blk(prev)],
        out_specs=[blk(cur), pl.BlockSpec((B, STAT_LANES), cur)],
        out_shape=[jax.ShapeDtypeStruct((L, dil * D), BF16), jax.ShapeDtypeStruct((L, dil * STAT_LANES), F32)],
        compiler_params=pltpu.CompilerParams(dimension_semantics=("parallel", "arbitrary"),
                                             vmem_limit_bytes=_vmem(12 * B * D * 4)),
    )(q_rot.reshape(L, dil * N_GROUPS * D), k.reshape(L, dil * D), k.reshape(L, dil * D),
      v.reshape(L, dil * D), v.reshape(L, dil * D))
    return o.reshape(S, D), lse.reshape(S, STAT_LANES)


def _attn_bwd(q_rot, k, v, do, lse, dlt, g, dil, *, name, carry=None):
    S, D = k.shape
    H = D // HEAD_DIM
    L = S // dil
    nb_count = L // ATTN_BLOCK
    B = ATTN_BLOCK

    def body(q_ref, kc_ref, kp_ref, vc_ref, vp_ref, do_ref, l_ref, d_ref, dq_ref, dk_ref, dv_ref, keep_k, keep_v):
        s_id = pl.program_id(1)

        @pl.when(s_id == 0)
        def _():
            keep_k[...] = jnp.zeros_like(keep_k)
            keep_v[...] = jnp.zeros_like(keep_v)

        @pl.when(s_id < nb_count)
        def _():
            qi = lax.broadcasted_iota(jnp.int32, (B, 2 * B), 0)
            kj = lax.broadcasted_iota(jnp.int32, (B, 2 * B), 1)
            valid = _window_mask(qi, kj, jnp.where(s_id > 0, 0, B))
            for h0 in range(0, H, HEADS_TOGETHER):
                heads = list(range(h0, min(h0 + HEADS_TOGETHER, H)))
                hs = [slice(h * HEAD_DIM, (h + 1) * HEAD_DIM) for h in heads]
                q = [q_ref[:, c] for c in hs]
                dout = [do_ref[:, c] for c in hs]
                kk = [jnp.concatenate([kp_ref[:, c], kc_ref[:, c]], axis=0) for c in hs]
                vv = [jnp.concatenate([vp_ref[:, c], vc_ref[:, c]], axis=0) for c in hs]
                s = [jnp.where(valid, _nt(q_, kk_) * ATTN_SCALE, NEG) for q_, kk_ in zip(q, kk)]
                dp = [_nt(do_, vv_) for do_, vv_ in zip(dout, vv)]
                p = [jnp.exp(s_ - l_ref[:, h:h + 1]) for s_, h in zip(s, heads)]
                ds = [(p_ * (dp_ - d_ref[:, h:h + 1])).astype(BF16) for p_, dp_, h in zip(p, dp, heads)]
                dq = [jnp.dot(ds_, kk_, preferred_element_type=F32) * ATTN_SCALE for ds_, kk_ in zip(ds, kk)]
                dkk = [_tn(ds_, q_) * ATTN_SCALE for ds_, q_ in zip(ds, q)]
                dvv = [_tn(p_.astype(BF16), do_) for p_, do_ in zip(p, dout)]
                for c, dq_, dkk_, dvv_ in zip(hs, dq, dkk, dvv):
                    dq_ref[:, c] = dq_.astype(dq_ref.dtype)
                    dk_ref[:, c] = (keep_k[:, c] + dkk_[0:B]).astype(dk_ref.dtype)
                    dv_ref[:, c] = (keep_v[:, c] + dvv_[0:B]).astype(dv_ref.dtype)
                    keep_k[:, c] = dkk_[B:2 * B]
                    keep_v[:, c] = dvv_[B:2 * B]

        @pl.when(s_id == nb_count)
        def _():
            dk_ref[...] = keep_k[...].astype(dk_ref.dtype)
            dv_ref[...] = keep_v[...].astype(dv_ref.dtype)

    last = nb_count - 1
    blk = lambda f: pl.BlockSpec((B, D), f)
    cur = lambda r, s: (jnp.minimum(s, last), r)
    prev = lambda r, s: (jnp.maximum(jnp.minimum(s, last) - 1, 0), r)
    lag = lambda r, s: (jnp.maximum(s - 1, 0), r)
    qcur = lambda r, s: (jnp.minimum(s, last), r * N_GROUPS + g)
    qv = q_rot.reshape(L, dil * N_GROUPS * D)
    view = lambda t: t.reshape(L, dil * D)
    sview = lambda t: t.reshape(L, dil * STAT_LANES)
    stat = lambda f: pl.BlockSpec((B, STAT_LANES), f)
    dq, dk, dv = _pcall(
        body, name=name, grid=(dil, nb_count + 1),
        in_specs=[blk(qcur), blk(cur), blk(prev), blk(cur), blk(prev), blk(cur), stat(cur), stat(cur)],
        out_specs=[blk(cur), blk(lag), blk(lag)],
        out_shape=[jax.ShapeDtypeStruct((L, dil * D), BF16)] * 3,
        scratch_shapes=[pltpu.VMEM((B, D), F32), pltpu.VMEM((B, D), F32)],
        operands=(qv, view(k), view(k), view(v), view(v), view(do), sview(lse), sview(dlt)),
        vmem=_vmem(24 * B * D * 4), carry=carry)
    return dq.reshape(S, D), dk.reshape(S, D), dv.reshape(S, D)


def _mlp_ple_fwd(z1, h1b, p_l, W, vec, l, run, kv_vec=None, target=None):
    D = z1.shape[1]
    g1, b1, g2, b2 = vec

    t = run(_mm, h1b, W[f"mlp_up{l}"], mode="nn", outs=[BF16], tm=1024, name=f"mlp_up{l}")

    def z2_ep(acc, z1_t, g1_, b1_, g2_, b2_):
        z2 = ALPHA * _ln(z1_t, g1_, b1_) + acc
        return z2, _ln(z2, g2_, b2_)

    z2, h2b = run(_mm, t, W[f"mlp_down{l}"], mode="nn", outs=[F32, BF16], extras=[z1], vecs=[g1, b1, g2, b2], a_fn=_sq_relu,
                  epilogue=z2_ep, ep_rows=EPILOGUE_ROWS, name=f"mlp_down{l}")
    act = None
    pp = run(_mm, p_l, W[f"ple_proj{l}"], mode="nn", outs=[F32], name=f"ple_proj{l}")
    if kv_vec is None:
        def head_ep(acc, z2_t, pp_t, tgt_t, g2_, b2_):
            y, gt = _ple_out(z2_t, pp_t, acc, g2_, b2_)
            err = y - tgt_t
            dy = err * (1.0 / D)
            return dy, dy * gt, dy * pp_t * gt * (1.0 - gt), 0.5 * err * err * (1.0 / D)

        dy, d_pp, d_gpre, loss_cols = run(_mm, h2b, W[f"ple_gate{l}"], mode="nn", outs=[F32, BF16, BF16], sums=1,
                                          extras=[z2, pp, target], vecs=[g2, b2], epilogue=head_ep,
                                          ep_rows=EPILOGUE_ROWS, tm=256, name=f"ple_gate{l}")
        return t, act, z2, h2b, pp, dy, d_pp, d_gpre, loss_cols

    def x1_ep(acc, z2_t, pp_t, g2_, b2_, kg, kb):
        x1, _ = _ple_out(z2_t, pp_t, acc, g2_, b2_)
        return acc, x1, _ln(x1, kg, kb)

    gpre, x1, kvn = run(_mm, h2b, W[f"ple_gate{l}"], mode="nn", outs=[F32, F32, BF16], extras=[z2, pp],
                        vecs=[g2, b2, *kv_vec], epilogue=x1_ep, ep_rows=EPILOGUE_ROWS, tm=256, name=f"ple_gate{l}")
    return t, act, z2, h2b, pp, gpre, x1, kvn


def _mlp_ple_bwd(dy, d_pp, d_gpre, p_l, z1, h1b, t, act, z2, h2b, wts, vec, l, run, produce):
    D = z1.shape[1]
    up, down, pp_w, pg_w = wts
    g1, b1, g2, b2 = vec
    produce(f"ple_proj{l}", run(_mm, p_l, d_pp, mode="tn", outs=[BF16], layout=("col", N_CHIPS), name=f"d_ple_proj{l}"))
    produce(f"ple_gate{l}", run(_mm, h2b, d_gpre, mode="tn", outs=[BF16], layout=("row", N_CHIPS), name=f"d_ple_gate{l}"))
    def ln2_bwd(acc, dy_t, z2_t, g2_):
        dh2 = acc + dy_t
        n, rstd = _ln_norm(z2_t)
        dz2 = _ln_bwd(dh2, n, rstd, g2_)
        return dz2, dz2, dh2 * n, dh2

    dz2, dz2b, dg2, db2 = run(_mm, d_gpre, pg_w, mode="nt", outs=[F32, BF16], sums=2, extras=[dy, z2], vecs=[g2],
                              epilogue=ln2_bwd, ep_rows=EPILOGUE_ROWS, tm=256, name=f"dh2_{l}")
    produce(f"mlp_down{l}", run(_mm, t, dz2b, mode="tn", outs=[BF16], layout=("row", N_CHIPS), a_fn=_sq_relu,
                                name=f"d_mlp_down{l}"))
    dt = run(_mm, dz2b, down, mode="nt", outs=[BF16], extras=[t],
             epilogue=lambda acc, t_: (acc * 2.0 * jnp.maximum(t_.astype(F32), 0.0),), name=f"dt{l}")
    produce(f"mlp_up{l}", run(_mm, h1b, dt, mode="tn", outs=[BF16], layout=("col", N_CHIPS), name=f"d_mlp_up{l}"))
    def ln1_bwd(acc, dz2_t, z1_t, g1_):
        dh1 = acc + ALPHA * dz2_t
        n, rstd = _ln_norm(z1_t)
        dz1 = _ln_bwd(dh1, n, rstd, g1_)
        return dz1, dz1, dh1 * n, dh1

    dz1, dz1b, dg1, db1 = run(_mm, dt, up, mode="nt", outs=[F32, BF16], sums=2, extras=[dz2, z1], vecs=[g1],
                              epilogue=ln1_bwd, ep_rows=EPILOGUE_ROWS, tk=1024, name=f"dh1_{l}")
    return dz1, dz1b, (dg1, db1, dg2, db2)


def _ple_out(z2, pp, gpre, g2, b2):
    gt = _sigmoid(gpre)
    return _ln(z2, g2, b2) + pp * gt, gt


_GATHER_AT = {
    "conv_in_a": ("conv_w_out",),
    "conv_in_g": ("ple_gate0", "ple_proj0"),
    "conv_fwd": ("mlp_up0",),
    "mlp_up0": ("mlp_down0",),
    "mlp_down0": ("attn_w_q", "w_kv"),
    "kv_k": ("attn_w_o",),
    "kv_v": ("ple_proj1", "ple_gate1"),
    "attn_q": ("mlp_up1",),
    "mlp_up1": ("mlp_down1",),
}
_GATHER_FIRST = ("conv_w_in",)


def _local_step(x, p, cosf, sinf, target, W, V, shards=None, reducer=None, chip_arr=None):
    S, D = x.shape
    gw, gv = {}, {}
    if shards is not None:
        W = dict(_Gather(_GATHER_FIRST, shards, chip_arr).run_alone("gather_first"))

    def run(fn, *args, name, **kw):
        gather = _Gather(_GATHER_AT[name], shards, chip_arr) if (shards is not None and name in _GATHER_AT) else None
        carry = gather if reducer is None or gather is not None else reducer.carry(name)
        out = fn(*args, name=name, carry=carry, **kw)
        if gather is not None:
            W.update(gather.result)
        elif reducer is not None:
            reducer.carried()
        return out

    a_pre = run(_mm, x, W["conv_w_in"], b_sel=0, mode="nn", outs=[F32], name="conv_in_a")
    g_pre, glu = run(_mm, x, W["conv_w_in"], b_sel=1, mode="nn", outs=[F32, F32], extras=[a_pre],
                     vecs=[V["conv_b_a"], V["conv_b_g"]], tm=256, ep_rows=EPILOGUE_ROWS,
                     epilogue=lambda acc, a, ba, bg: (acc, (a + ba) * _sigmoid(acc + bg)), name="conv_in_g")
    cv = run(_conv_fwd, glu, V["conv_dw"], V["conv_dw_b"], name="conv_fwd")

    def silu_ln(c, g_, b_):
        y = _ln(c, g_, b_)
        return (y * _sigmoid(y),)

    (sb,) = _rows(silu_ln, [cv], [V["conv_ln_g"], V["conv_ln_b"]], [(D, BF16)], [], tm=256, name="conv_ln_fwd")
    def z1_ep(acc, x_t, g_, b_):
        z1 = ALPHA * x_t + acc
        return z1, _ln(z1, g_, b_)

    vec0 = (V["ln1_g0"], V["ln1_b0"], V["ln2_g0"], V["ln2_b0"])
    vec1 = (V["ln1_g1"], V["ln1_b1"], V["ln2_g1"], V["ln2_b1"])
    z1_0, h1b_0 = _mm(sb, W["conv_w_out"], mode="nn", outs=[F32, BF16], extras=[x], vecs=[vec0[0], vec0[1]],
                      epilogue=z1_ep, ep_rows=EPILOGUE_ROWS, name="conv_out")
    t0, act0, z2_0, h2b_0, pp0, gpre0, x1, kvn = _mlp_ple_fwd(z1_0, h1b_0, p[0], W, vec0, 0, run,
                                                              (V["kv_ln_g"], V["kv_ln_b"]))

    rot_ep = lambda acc, c_, s_: (_rot(acc, c_, s_),)
    k_rot = run(_mm, kvn, W["w_kv"], b_sel=0, mode="nn", outs=[BF16], rextras=[cosf, sinf], epilogue=rot_ep, name="kv_k")
    v_b = run(_mm, kvn, W["w_kv"], b_sel=1, mode="nn", outs=[BF16], name="kv_v")
    q_rot = run(_mm, x1, W["attn_w_q"], mode="nn", outs=[BF16], rextras=[cosf, sinf], epilogue=rot_ep, name="attn_q")
    og, lg = [], []
    for g, dil in enumerate(DILATIONS):
        o_g, l_g = _attn_fwd(q_rot, k_rot, v_b, g, dil)
        og.append(o_g)
        lg.append(l_g)

    def merge(o0, o1, o2, l0, l1, l2):
        m = jnp.maximum(jnp.maximum(l0, l1), l2)
        e = [jnp.exp(l0 - m), jnp.exp(l1 - m), jnp.exp(l2 - m)]
        den = e[0] + e[1] + e[2]
        w = [e_g / den for e_g in e]
        o = _per_head(o0, lambda oh, h: sum(w[g][:, h:h + 1] * (o0, o1, o2)[g][:, h * HEAD_DIM:(h + 1) * HEAD_DIM].astype(F32)
                                            for g in range(N_GROUPS)))
        return o, m + jnp.log(den)

    ob, lse = _rows(merge, og + lg, [], [(D, BF16), (STAT_LANES, F32)], [], tm=256, name="attn_merge")
    z1_1, h1b_1 = _mm(ob, W["attn_w_o"], mode="nn", outs=[F32, BF16], extras=[x1], vecs=[vec1[0], vec1[1]],
                      epilogue=z1_ep, ep_rows=EPILOGUE_ROWS, name="attn_out")
    t1, act1, z2_1, h2b_1, pp1, dy1, d_pp1, d_gpre1, loss_cols = _mlp_ple_fwd(z1_1, h1b_1, p[1], W, vec1, 1, run,
                                                                              target=target)
    wts0 = (W["mlp_up0"], W["mlp_down0"], W["ple_proj0"], W["ple_gate0"])
    wts1 = (W["mlp_up1"], W["mlp_down1"], W["ple_proj1"], W["ple_gate1"])


    def produce(name, grad):
        gw[name] = grad
        if reducer is not None:
            reducer.produced(name, grad)

    dz1_1, dz1b_1, (gv["ln1_g1"], gv["ln1_b1"], gv["ln2_g1"], gv["ln2_b1"]) = _mlp_ple_bwd(
        dy1, d_pp1, d_gpre1, p[1], z1_1, h1b_1, t1, act1, z2_1, h2b_1, wts1, vec1, 1, run, produce)
    produce("attn_w_o", run(_mm, ob, dz1b_1, mode="tn", outs=[BF16], layout=("row", N_CHIPS), name="d_attn_w_o"))

    do_b = run(_mm, dz1b_1, W["attn_w_o"], mode="nt", outs=[BF16], name="attn_do")

    def delta(do_t, o_t):
        prod = do_t.astype(F32) * o_t.astype(F32)
        H = D // HEAD_DIM
        return (_per_head_pack([jnp.sum(prod[:, h * HEAD_DIM:(h + 1) * HEAD_DIM], axis=1, keepdims=True) for h in range(H)]),)

    (dlt,) = _rows(delta, [do_b, ob], [], [(STAT_LANES, F32)], [], tm=256, name="attn_delta")
    dqs, dks, dvs = [], [], []
    for g, dil in enumerate(DILATIONS):
        dq_g, dk_g, dv_g = run(_attn_bwd, q_rot, k_rot, v_b, do_b, lse, dlt, g, dil, name=f"attn_bwd_g{g}")
        dqs.append(dq_g)
        dks.append(dk_g)
        dvs.append(dv_g)

    def unrot(q0, q1, q2, k0, k1, k2, v0, v1, v2, c_, s_):
        dq = jnp.concatenate([_rot_t(t_.astype(F32), c_, s_) for t_ in (q0, q1, q2)], axis=1)
        f = lambda t_: t_.astype(F32)
        return dq, _rot_t(f(k0) + f(k1) + f(k2), c_, s_), f(v0) + f(v1) + f(v2)

    dq, dk, dv = run(_rows, unrot, dqs + dks + dvs + [cosf, sinf], [], [(N_GROUPS * D, BF16), (D, BF16), (D, BF16)], [],
                     tm=128, name="attn_unrot")
    produce("attn_w_q", run(_mm, x1, dq, mode="tn", outs=[BF16], layout=("col", N_CHIPS), name="d_attn_w_q"))
    dx1_q = run(_mm, dq, W["attn_w_q"], mode="nt", outs=[F32], extras=[dz1_1],
                epilogue=lambda acc, e: (acc + ALPHA * e,), name="dx1_q")
    produce("w_kv", jnp.concatenate(
        [run(_mm, kvn, dk, mode="tn", outs=[BF16], layout=("col", 2), name="d_w_kv_k"),
         run(_mm, kvn, dv, mode="tn", outs=[BF16], layout=("col", 2), name="d_w_kv_v")], axis=1))
    dkvn_k = run(_mm, dk, W["w_kv"], b_sel=0, mode="nt", outs=[F32], name="dkvn_k")
    def x1_bwd(acc, dkvn_k_t, dx1q_t, x1_t, pp, gpre, kg):
        dkvn_t = acc + dkvn_k_t
        n, rstd = _ln_norm(x1_t)
        dy = dx1q_t + _ln_bwd(dkvn_t, n, rstd, kg)
        gt = _sigmoid(gpre)
        return dy, dy * gt, dy * pp * gt * (1.0 - gt), dkvn_t * n, dkvn_t

    dy0, d_pp0, d_gpre0, gv["kv_ln_g"], gv["kv_ln_b"] = run(
        _mm, dv, W["w_kv"], b_sel=1, mode="nt", outs=[F32, BF16, BF16], sums=2, extras=[dkvn_k, dx1_q, x1, pp0, gpre0],
        vecs=[V["kv_ln_g"]], epilogue=x1_bwd, ep_rows=EPILOGUE_ROWS, tm=256, name="dkvn_v")

    dz1_0, dz1b_0, (gv["ln1_g0"], gv["ln1_b0"], gv["ln2_g0"], gv["ln2_b0"]) = _mlp_ple_bwd(
        dy0, d_pp0, d_gpre0, p[0], z1_0, h1b_0, t0, act0, z2_0, h2b_0, wts0, vec0, 0, run, produce)
    produce("conv_w_out", run(_mm, sb, dz1b_0, mode="tn", outs=[BF16], layout=("row", N_CHIPS), name="d_conv_w_out"))
    def conv_ln_bwd(ds_t, c_t, g_, b_):
        n, rstd = _ln_norm(c_t)
        y = n * g_ + b_
        sg = _sigmoid(y)
        dln = ds_t * sg * (1.0 + y * (1.0 - sg))
        dc = _ln_bwd(dln, n, rstd, g_)
        return dc, dln * n, dln, dc

    dc, gv["conv_ln_g"], gv["conv_ln_b"], gv["conv_dw_b"] = run(
        _mm, dz1b_0, W["conv_w_out"], mode="nt", outs=[F32], sums=3, extras=[cv], vecs=[V["conv_ln_g"], V["conv_ln_b"]],
        epilogue=conv_ln_bwd, ep_rows=EPILOGUE_ROWS, tm=256, name="conv_ds")
    da, dg, gv["conv_dw"], gv["conv_b_a"], gv["conv_b_g"] = run(
        _conv_bwd, dc, glu, a_pre, g_pre, V["conv_dw"], V["conv_b_a"], V["conv_b_g"], name="conv_bwd")
    produce("conv_w_in", jnp.concatenate(
        [run(_mm, x, da, mode="tn", outs=[BF16], layout=("col", 2), name="d_conv_w_in_a"),
         run(_mm, x, dg, mode="tn", outs=[BF16], layout=("col", 2), name="d_conv_w_in_g")], axis=1))
    dx_a = run(_mm, da, W["conv_w_in"], b_sel=0, mode="nt", outs=[F32], extras=[dz1_0],
               epilogue=lambda acc, e: (acc + ALPHA * e,), name="dx_a")
    grad_x = run(_mm, dg, W["conv_w_in"], b_sel=1, mode="nt", outs=[F32], extras=[dx_a],
                 epilogue=lambda acc, e: (acc + e,), name="dx_g")
    if reducer is not None:
        reducer.carry("share_last").run_alone("share_last")
        reducer.carried()
    return loss_cols, grad_x, gw, gv


def _place():
    x, y, c = lax.axis_index("x"), lax.axis_index("y"), lax.axis_index("c")
    chips = [(1 - x, y), (x, 1 - y), (1 - x, 1 - y)]
    return x, y, c, chips


def _remote(src, dst, ssem, rsem, dev):
    return pltpu.make_async_remote_copy(src_ref=src, dst_ref=dst, send_sem=ssem, recv_sem=rsem, device_id=dev,
                                        device_id_type=MESH)


def _allgather8(block, name):
    R, C = block.shape

    def body(x_ref, out_ref, send_sems, recv_sems, local_sem):
        x, y, c, chips = _place()
        me, sibling = (x, y, c), (x, y, 1 - c)

        def slot(px, py, pc):
            return out_ref.at[4 * px + 2 * py + pc]

        def copy(k, blockpos, to, src=None):
            return _remote(slot(*blockpos) if src is None else src, slot(*blockpos), send_sems.at[k], recv_sems.at[k], to)

        mine = pltpu.make_async_copy(x_ref, slot(*me), local_sem)
        mine.start()
        first = [copy(0, me, sibling, src=x_ref)]
        first += [copy(1 + j, me, (*chip, c), src=x_ref) for j, chip in enumerate(chips)]
        for cp in first:
            cp.start()
        passed = [copy(4 + j, (*chip, c), sibling) for j, chip in enumerate(chips)]
        for j, chip in enumerate(chips):
            copy(1 + j, (*chip, c), me).wait_recv()
            passed[j].start()
        copy(0, sibling, me).wait_recv()
        for j, chip in enumerate(chips):
            copy(4 + j, (*chip, 1 - c), me).wait_recv()
        for cp in first + passed:
            cp.wait_send()
        mine.wait()

    return pl.pallas_call(
        body, name=name, out_shape=jax.ShapeDtypeStruct((8, R, C), block.dtype),
        in_specs=[pl.BlockSpec(memory_space=pltpu.VMEM)], out_specs=pl.BlockSpec(memory_space=pltpu.VMEM),
        scratch_shapes=[pltpu.SemaphoreType.DMA((7,)), pltpu.SemaphoreType.DMA((7,)), pltpu.SemaphoreType.DMA],
        compiler_params=pltpu.CompilerParams(vmem_limit_bytes=_vmem(10 * _nbytes((R, C), block.dtype))),
    )(block)


_MATS = (
    ("conv_w_in", "conv_w_in", 0, "col", True),
    ("conv_w_out", "conv_w_out", 0, "row", False),
    ("mlp_up0", "mlp_up", 0, "col", False),
    ("mlp_down0", "mlp_down", 0, "row", False),
    ("ple_proj0", "ple_proj", 0, "col", False),
    ("ple_gate0", "ple_gate", 0, "row", False),
    ("w_kv", "w_kv", None, "col", True),
    ("attn_w_q", "attn_w_q", 0, "col", False),
    ("attn_w_o", "attn_w_o", 0, "row", False),
    ("mlp_up1", "mlp_up", 1, "col", False),
    ("mlp_down1", "mlp_down", 1, "row", False),
    ("ple_proj1", "ple_proj", 1, "col", False),
    ("ple_gate1", "ple_gate", 1, "row", False),
)


class _Carry:
    result = None
    aliases = {}

    def set_result(self, outs):
        self.result = dict(zip(self.names, outs))

    def run_alone(self, name):
        n_in, n_out = len(self.ins), len(self.out_shape)

        def body(*refs):
            in_refs, out_refs, sems = refs[:n_in], refs[n_in:n_in + n_out], refs[n_in + n_out:]
            self.start(in_refs, out_refs, sems)
            self.finish(in_refs, out_refs, sems)

        outs = pl.pallas_call(body, name=name, out_shape=self.out_shape, in_specs=[ANY] * n_in, out_specs=[ANY] * n_out,
                              scratch_shapes=self.scratch, input_output_aliases=dict(self.aliases))(*self.ins)
        self.set_result(outs)
        return self.result


class _Gather(_Carry):
    def __init__(self, names, shards, chip_arr):
        mats = [m for m in _MATS if m[0] in names]
        srcs = sorted({m[1] for m in mats})
        self.names = [m[0] for m in mats]
        self.out_shape, self.geo, placed = [], [], []
        for name, src, layer, kind, split in mats:
            s = shards[src]
            ks, ns = s.shape[-2:]
            K, N = (ks, ns * N_CHIPS) if kind == "col" else (ks * N_CHIPS, ns)
            self.out_shape.append(jax.ShapeDtypeStruct((2, K, N // 2) if split else (K, N), BF16))
            self.geo.append((srcs.index(src), layer if s.ndim == 3 else None, kind, split, K, N))
            placed.append(_place_shard(s, layer if s.ndim == 3 else None, kind, split, chip_arr, f"place_{name}"))
        T = len(mats)
        self.ins = [shards[n] for n in srcs] + placed
        self.aliases = {len(srcs) + t: t for t in range(T)}
        self.scratch = [pltpu.SemaphoreType.DMA((3 * T,)) for _ in range(4)]
        self.result = None

    def _copies(self, in_refs, out_refs, sems):
        geo, T = self.geo, len(self.geo)
        s_ici, r_ici, s_d2d, r_d2d = sems
        x, y, c, chips = _place()
        me = 2 * x + y
        sibling = (x, y, 1 - c)
        idx = [2 * cx + cy for cx, cy in chips]

        def src_ref(t):
            i, layer, _, _, _, _ = geo[t]
            return in_refs[i] if layer is None else in_refs[i].at[layer]

        def src_half(t, h):
            _, _, kind, _, K, N = geo[t]
            if kind == "col":
                return src_ref(t).at[pl.ds(h * (K // 2), K // 2), :]
            return src_ref(t).at[:, pl.ds(h * (N // 2), N // 2)]

        def dst(t, j, h):
            _, _, kind, split, K, N = geo[t]
            n, k = N // N_CHIPS, K // N_CHIPS
            if kind == "col":
                rows = slice(None) if h is None else pl.ds(h * (K // 2), K // 2)
                if split:
                    return out_refs[t].at[j // 2, rows, pl.ds((j % 2) * n, n)]
                return out_refs[t].at[rows, pl.ds(j * n, n)]
            cols = slice(None) if h is None else pl.ds(h * (N // 2), N // 2)
            return out_refs[t].at[pl.ds(j * k, k), cols]

        sends = [_remote(src_half(t, c), dst(t, me, c), s_ici.at[3 * t + kk], r_ici.at[3 * t + kk], (*chips[kk], c))
                 for t in range(T) for kk in range(3)]
        hops = []
        for t in range(T):
            for kk in range(3):
                mine, theirs = dst(t, idx[kk], c), dst(t, idx[kk], 1 - c)
                hops.append((_remote(mine, mine, s_ici.at[3 * t + kk], r_ici.at[3 * t + kk], sibling),
                             _remote(mine, mine, s_d2d.at[3 * t + kk], r_d2d.at[3 * t + kk], sibling),
                             _remote(theirs, theirs, s_d2d.at[3 * t + kk], r_d2d.at[3 * t + kk], sibling)))
        return sends, hops

    def start(self, in_refs, out_refs, sems):
        for cp in self._copies(in_refs, out_refs, sems)[0]:
            cp.start()

    def finish(self, in_refs, out_refs, sems):
        sends, hops = self._copies(in_refs, out_refs, sems)
        for landed, forward, _ in hops:
            landed.wait_recv()
            forward.start()
        for _, _, from_sibling in hops:
            from_sibling.wait_recv()
        for cp in sends + [h[1] for h in hops]:
            cp.wait_send()


def _place_shard(shard, layer, kind, split, chip_arr, name):
    ks, ns = shard.shape[-2:]
    K, N = (ks, ns * N_CHIPS) if kind == "col" else (ks * N_CHIPS, ns)
    tr = _fit(256, ks)
    nb = ks // tr
    if shard.ndim == 3:
        in_spec = pl.BlockSpec((None, tr, ns), lambda i, me: (layer, i, 0))
    else:
        in_spec = pl.BlockSpec((tr, ns), lambda i, me: (i, 0))
    if kind == "row":
        out_shape, out_spec = (K, N), pl.BlockSpec((tr, ns), lambda i, me: (me[0] * nb + i, 0))
    elif split:
        out_shape, out_spec = (2, K, N // 2), pl.BlockSpec((None, tr, ns), lambda i, me: (me[0] // 2, i, me[0] % 2))
    else:
        out_shape, out_spec = (K, N), pl.BlockSpec((tr, ns), lambda i, me: (i, me[0]))

    def body(me_ref, s_ref, o_ref):
        o_ref[...] = s_ref[...]

    return pl.pallas_call(
        body, name=name, out_shape=jax.ShapeDtypeStruct(out_shape, BF16),
        grid_spec=pltpu.PrefetchScalarGridSpec(num_scalar_prefetch=1, grid=(nb,), in_specs=[in_spec], out_specs=out_spec),
        compiler_params=pltpu.CompilerParams(dimension_semantics=("parallel",), vmem_limit_bytes=_vmem(4 * tr * ns * 2)),
    )(chip_arr, shard)


class _Multi(_Carry):
    def __init__(self, parts):
        self.parts = parts
        self.ins = [a for p in parts for a in p.ins]
        self.out_shape = [a for p in parts for a in p.out_shape]
        self.scratch = [a for p in parts for a in p.scratch]
        self.aliases, n_in, n_out = {}, 0, 0
        for p in parts:
            self.aliases.update({n_in + i: n_out + o for i, o in p.aliases.items()})
            n_in, n_out = n_in + len(p.ins), n_out + len(p.out_shape)

    def _split(self, seq, field):
        out, at = [], 0
        for p in self.parts:
            n = len(getattr(p, field))
            out.append(seq[at:at + n])
            at += n
        return out

    def _each(self, method, in_refs, out_refs, sems):
        for p, i, o, s in zip(self.parts, self._split(in_refs, "ins"), self._split(out_refs, "out_shape"),
                              self._split(sems, "scratch")):
            getattr(p, method)(i, o, s)

    def start(self, in_refs, out_refs, sems):
        self._each("start", in_refs, out_refs, sems)

    def finish(self, in_refs, out_refs, sems):
        self._each("finish", in_refs, out_refs, sems)

    def set_result(self, outs):
        for p, o in zip(self.parts, self._split(list(outs), "out_shape")):
            p.set_result(o)


class _PairSend(_Carry):
    def __init__(self, grads):
        self.names = list(grads)
        self.ins = [grads[n] for n in self.names]
        self.out_shape = [jax.ShapeDtypeStruct(a.shape[1:], BF16) for a in self.ins]
        T = len(self.names)
        self.scratch = [pltpu.SemaphoreType.DMA((T,)), pltpu.SemaphoreType.DMA((T,))]

    def _copies(self, in_refs, out_refs, sems):
        x, y, c, _ = _place()
        return [_remote(in_refs[t].at[1 - c], out_refs[t], sems[0].at[t], sems[1].at[t], (x, y, 1 - c))
                for t in range(len(self.names))]

    def start(self, in_refs, out_refs, sems):
        for cp in self._copies(in_refs, out_refs, sems):
            cp.start()

    def finish(self, in_refs, out_refs, sems):
        for cp in self._copies(in_refs, out_refs, sems):
            cp.wait()


class _ChipScatter(_Carry):
    def __init__(self, sums):
        self.names = list(sums)
        T = len(self.names)
        self.ins = [sums[n][0] for n in self.names] + [sums[n][1] for n in self.names]
        self.out_shape = [jax.ShapeDtypeStruct(a.shape, BF16) for a in self.ins[:T]]
        self.aliases = {T + t: t for t in range(T)}
        self.scratch = [pltpu.SemaphoreType.DMA((3 * T,)), pltpu.SemaphoreType.DMA((3 * T,))]

    def _copies(self, in_refs, out_refs, sems):
        ssem, rsem = sems
        x, y, c, chips = _place()
        me = 2 * x + y
        idx = [2 * cx + cy for cx, cy in chips]
        T = len(self.names)
        sends = [_remote(in_refs[t].at[idx[kk]], out_refs[t].at[me], ssem.at[3 * t + kk], rsem.at[3 * t + kk],
                         (*chips[kk], c)) for t in range(T) for kk in range(3)]
        lands = [_remote(out_refs[t].at[idx[kk]], out_refs[t].at[idx[kk]], ssem.at[3 * t + kk], rsem.at[3 * t + kk],
                         (*chips[kk], c)) for t in range(T) for kk in range(3)]
        return sends, lands

    def start(self, in_refs, out_refs, sems):
        for cp in self._copies(in_refs, out_refs, sems)[0]:
            cp.start()

    def finish(self, in_refs, out_refs, sems):
        sends, lands = self._copies(in_refs, out_refs, sems)
        for cp in lands:
            cp.wait_recv()
        for cp in sends:
            cp.wait_send()


class _PairShare(_Carry):
    def __init__(self, halves):
        self.names = list(halves)
        self.ins = [halves[n] for n in self.names]
        self.out_shape = [jax.ShapeDtypeStruct(a.shape, F32) for a in self.ins]
        T = len(self.names)
        self.aliases = {t: t for t in range(T)}
        self.scratch = [pltpu.SemaphoreType.DMA((T,)), pltpu.SemaphoreType.DMA((T,))]

    def _copies(self, in_refs, out_refs, sems):
        ssem, rsem = sems
        x, y, c, _ = _place()
        sibling = (x, y, 1 - c)
        T = len(self.names)
        sends = [_remote(out_refs[t].at[c], out_refs[t].at[c], ssem.at[t], rsem.at[t], sibling) for t in range(T)]
        lands = [_remote(out_refs[t].at[1 - c], out_refs[t].at[1 - c], ssem.at[t], rsem.at[t], sibling) for t in range(T)]
        return sends, lands

    def start(self, in_refs, out_refs, sems):
        for cp in self._copies(in_refs, out_refs, sems)[0]:
            cp.start()

    def finish(self, in_refs, out_refs, sems):
        sends, lands = self._copies(in_refs, out_refs, sems)
        for cp in lands:
            cp.wait_recv()
        for cp in sends:
            cp.wait_send()


def _pair_sum(own, landed, c_arr, name):
    _, ns, r, cc = own.shape
    rows = ns * r
    tr = _fit(512, rows)

    def body(c_ref, a_ref, b_ref, o_ref, o2_ref):
        total = (a_ref[...].astype(F32) + b_ref[...].astype(F32)).astype(o_ref.dtype)
        o_ref[...] = total
        o2_ref[...] = total

    tile = pl.BlockSpec((tr, cc), lambda i, c_ref: (i, 0))
    out = pl.pallas_call(
        body, name=name, out_shape=[jax.ShapeDtypeStruct((rows, cc), BF16)] * 2,
        grid_spec=pltpu.PrefetchScalarGridSpec(
            num_scalar_prefetch=1, grid=(rows // tr,),
            in_specs=[pl.BlockSpec((None, tr, cc), lambda i, c_ref: (c_ref[0], i, 0)), tile], out_specs=[tile, tile]),
        compiler_params=pltpu.CompilerParams(dimension_semantics=("parallel",), vmem_limit_bytes=_vmem(8 * tr * cc * 4)),
    )(c_arr, own.reshape(2, rows, cc), landed.reshape(rows, cc))
    return out[0].reshape(ns, r, cc), out[1].reshape(ns, r, cc)


def _chip_sum(parts, c_arr, name):
    _, r, cc = parts.shape
    tr = _fit(256, r)

    def body(c_ref, p_ref, o_ref):
        acc = p_ref[0].astype(F32)
        for j in range(1, N_CHIPS):
            acc = acc + p_ref[j].astype(F32)
        o_ref[...] = acc

    return pl.pallas_call(
        body, name=name, out_shape=jax.ShapeDtypeStruct((2, r, cc), F32),
        grid_spec=pltpu.PrefetchScalarGridSpec(
            num_scalar_prefetch=1, grid=(r // tr,),
            in_specs=[pl.BlockSpec((N_CHIPS, tr, cc), lambda i, c_ref: (0, i, 0))],
            out_specs=pl.BlockSpec((None, tr, cc), lambda i, c_ref: (c_ref[0], i, 0))),
        compiler_params=pltpu.CompilerParams(dimension_semantics=("parallel",), vmem_limit_bytes=_vmem(12 * tr * cc * 4)),
    )(c_arr, parts)


def _adamw_math(w, g, m, v):
    m2 = ADAM_B1 * m + (1.0 - ADAM_B1) * g
    v2 = ADAM_B2 * v + (1.0 - ADAM_B2) * jnp.square(g)
    m_hat = m2 / (1.0 - ADAM_B1 ** ADAM_STEP)
    v_hat = v2 / (1.0 - ADAM_B2 ** ADAM_STEP)
    delta = -ADAM_LR * (m_hat / (jnp.sqrt(v_hat) + ADAM_EPS) + ADAM_WD * w)
    return delta, m2, v2


def _adamw_mat(g2, w, m, v, layer, kind, prev, name):
    shape = w.shape
    ks, ns = shape[-2:]
    _, r, cc = g2.shape
    tr, tc = _fit(256, r), _fit(1024, cc)
    assert (r, cc) == ((ks // 2, ns) if kind == "col" else (ks, ns // 2))
    assert r % tr == 0 and cc % tc == 0
    rb, cb = r // tr, cc // tc
    if kind == "col":
        g_spec = pl.BlockSpec((None, tr, tc), lambda i, j: (i // rb, i % rb, j))
    else:
        g_spec = pl.BlockSpec((None, tr, tc), lambda i, j: (j // cb, i, j % cb))
    if w.ndim == 3:
        w_spec = pl.BlockSpec((None, tr, tc), lambda i, j: (layer, i, j))
    else:
        w_spec = pl.BlockSpec((tr, tc), lambda i, j: (i, j))
    n_prev = 0 if prev is None else 4

    def body(*refs):
        g_ref, w_ref, m_ref, v_ref = refs[:4]
        go_ref, d_ref, mo_ref, vo_ref = refs[4 + n_prev:]
        g = g_ref[...]
        delta, m2, v2 = _adamw_math(w_ref[...], g, m_ref[...], v_ref[...])
        go_ref[...] = g
        d_ref[...] = delta
        mo_ref[...] = m2
        vo_ref[...] = v2

    return pl.pallas_call(
        body, name=name, grid=(ks // tr, ns // tc),
        in_specs=[g_spec, w_spec, w_spec, w_spec] + [ANY] * n_prev, out_specs=[w_spec] * 4,
        out_shape=[jax.ShapeDtypeStruct(shape, F32)] * 4,
        input_output_aliases={4 + i: i for i in range(n_prev)},
        compiler_params=pltpu.CompilerParams(dimension_semantics=("parallel", "parallel"),
                                             vmem_limit_bytes=_vmem(16 * tr * tc * 4)),
    )(g2, w, m, v, *(prev or ()))


def _adamw_small(g, w, m, v, name):
    def body(g_ref, w_ref, m_ref, v_ref, d_ref, mo_ref, vo_ref):
        delta, m2, v2 = _adamw_math(w_ref[...], g_ref[...], m_ref[...], v_ref[...])
        d_ref[...] = delta
        mo_ref[...] = m2
        vo_ref[...] = v2

    return pl.pallas_call(body, name=name, out_shape=[jax.ShapeDtypeStruct(w.shape, F32)] * 3)(g, w, m, v)


def _sum8(parts, name):
    def body(p_ref, o_ref):
        acc = p_ref[0]
        for j in range(1, 8):
            acc = acc + p_ref[j]
        o_ref[...] = acc

    return pl.pallas_call(body, name=name, out_shape=jax.ShapeDtypeStruct(parts.shape[1:], F32),
                          compiler_params=pltpu.CompilerParams(vmem_limit_bytes=_vmem(12 * _nbytes(parts.shape[1:], F32))))(parts)


_REDUCE_AT = {
    "d_ple_gate1": (("A", "ple_proj1"),),
    "dh2_1": (("A", "ple_gate1"),),
    "d_mlp_down1": (("B", "ple_proj1"), ("B", "ple_gate1")),
    "dt1": (("A", "mlp_down1"),),
    "d_mlp_up1": (("B", "mlp_down1"), ("C", "ple_proj1"), ("C", "ple_gate1")),
    "dh1_1": (("A", "mlp_up1"),),
    "d_attn_w_o": (("C", "mlp_down1"),),
    "attn_do": (("A", "attn_w_o"),),
    "attn_bwd_g0": (("B", "attn_w_o"),),
    "attn_unrot": (("C", "attn_w_o"),),
    "dx1_q": (("A", "attn_w_q"),),
    "dkvn_k": (("A", "w_kv"),),
    "d_ple_gate0": (("A", "ple_proj0"),),
    "dh2_0": (("A", "ple_gate0"),),
    "d_mlp_down0": (("B", "mlp_up1"), ("B", "ple_proj0")),
    "dt0": (("B", "w_kv"), ("B", "ple_gate0"), ("A", "mlp_down0")),
    "d_mlp_up0": (("B", "mlp_down0"), ("C", "mlp_up1"), ("C", "ple_proj0"), ("C", "w_kv"), ("C", "ple_gate0")),
    "dh1_0": (("A", "mlp_up0"), ("B", "attn_w_q")),
    "d_conv_w_out": (("C", "mlp_down0"), ("C", "attn_w_q")),
    "conv_ds": (("A", "conv_w_out"),),
    "conv_bwd": (("B", "mlp_up0"), ("B", "conv_w_out")),
    "d_conv_w_in_g": (("C", "mlp_up0"), ("C", "conv_w_out")),
    "dx_a": (("A", "conv_w_in"),),
    "dx_g": (("B", "conv_w_in"),),
    "share_last": (("C", "conv_w_in"),),
}


class _Reducer:
    def __init__(self, w, mom, var, c_arr):
        self.w, self.mom, self.var, self.c_arr = w, mom, var, c_arr
        self.mats = {m[0]: m for m in _MATS}
        self.grads, self.pair_sums, self.chip_sums, self.out = {}, {}, {}, {}

    def produced(self, name, grad):
        self.grads[name] = grad

    def carry(self, call):
        parts = []
        for cls, stage, src in ((_PairSend, "A", self.grads), (_ChipScatter, "B", self.pair_sums),
                                (_PairShare, "C", self.chip_sums)):
            names = [n for s, n in _REDUCE_AT.get(call, ()) if s == stage]
            if names:
                parts.append((stage, cls({n: src[n] for n in names})))
        self._parts = parts
        return _Multi([p for _, p in parts]) if parts else None

    def carried(self):
        for stage, part in self._parts:
            for name, val in part.result.items():
                if stage == "A":
                    self.pair_sums[name] = _pair_sum(self.grads[name], val, self.c_arr, f"pair_sum_{name}")
                elif stage == "B":
                    self.chip_sums[name] = _chip_sum(val, self.c_arr, f"chip_sum_{name}")
                else:
                    _, src, layer, kind, _ = self.mats[name]
                    self.out[src] = _adamw_mat(val, self.w[src], self.mom[src], self.var[src], layer or 0, kind,
                                               self.out.get(src), f"adamw_{name}")
        self._parts = []


_WEIGHTS = ("conv_w_in", "conv_b_in", "conv_dw", "conv_dw_b", "conv_ln_g", "conv_ln_b", "conv_w_out", "kv_ln_g",
            "kv_ln_b", "w_kv", "attn_w_q", "attn_w_o", "ln1_g", "ln1_b", "mlp_up", "mlp_down", "ln2_g", "ln2_b",
            "ple_proj", "ple_gate")
_SHARDED_VECS = ("conv_b_in", "conv_dw", "conv_dw_b", "conv_ln_g", "conv_ln_b")
_REPLICATED_VECS = ("kv_ln_g", "kv_ln_b", "ln1_g", "ln1_b", "ln2_g", "ln2_b")


def _pad_rows(a, rows):
    return jnp.concatenate([a, jnp.zeros((rows - a.shape[0], a.shape[1]), a.dtype)], axis=0) if a.shape[0] < rows else a


def _pack_sharded(d):
    n = d["conv_dw_b"].shape[-1]
    rows = [d["conv_b_in"].reshape(2, n), d["conv_dw"].reshape(CONV_WIDTH, n), d["conv_dw_b"].reshape(1, n),
            d["conv_ln_g"].reshape(1, n), d["conv_ln_b"].reshape(1, n)]
    return _pad_rows(jnp.concatenate(rows, axis=0), 40)


def _unpack_sharded(pack, like):
    n = pack.shape[1]
    return {"conv_b_in": pack[0:2].reshape(like["conv_b_in"].shape),
            "conv_dw": pack[2:2 + CONV_WIDTH].reshape(like["conv_dw"].shape),
            "conv_dw_b": pack[33:34].reshape(like["conv_dw_b"].shape),
            "conv_ln_g": pack[34:35].reshape(like["conv_ln_g"].shape),
            "conv_ln_b": pack[35:36].reshape(like["conv_ln_b"].shape)}


def _pack_replicated(d):
    D = d["kv_ln_g"].shape[-1]
    rows = [d[n].reshape(-1, D) for n in _REPLICATED_VECS]
    return _pad_rows(jnp.concatenate(rows, axis=0), 16)


def _unpack_replicated(pack, like):
    out, r = {}, 0
    for n in _REPLICATED_VECS:
        k = like[n].size // pack.shape[1]
        out[n] = pack[r:r + k].reshape(like[n].shape)
        r += k
    return out


def kernel(x, p, positions, conv_w_in, conv_b_in, conv_dw, conv_dw_b, conv_ln_g, conv_ln_b, conv_w_out, kv_ln_g, kv_ln_b, w_kv, attn_w_q, attn_w_o, ln1_g, ln1_b, mlp_up, mlp_down, ln2_g, ln2_b, ple_proj, ple_gate, loss_target, m_conv_w_in, m_conv_b_in, m_conv_dw, m_conv_dw_b, m_conv_ln_g, m_conv_ln_b, m_conv_w_out, m_kv_ln_g, m_kv_ln_b, m_w_kv, m_attn_w_q, m_attn_w_o, m_ln1_g, m_ln1_b, m_mlp_up, m_mlp_down, m_ln2_g, m_ln2_b, m_ple_proj, m_ple_gate, v_conv_w_in, v_conv_b_in, v_conv_dw, v_conv_dw_b, v_conv_ln_g, v_conv_ln_b, v_conv_w_out, v_kv_ln_g, v_kv_ln_b, v_w_kv, v_attn_w_q, v_attn_w_o, v_ln1_g, v_ln1_b, v_mlp_up, v_mlp_down, v_ln2_g, v_ln2_b, v_ple_proj, v_ple_gate):
    args = dict(locals())
    w = {n: args[n] for n in _WEIGHTS}
    mom = {n: args["m_" + n] for n in _WEIGHTS}
    var = {n: args["v_" + n] for n in _WEIGHTS}
    S, D = x.shape[1:]
    n4 = D // N_CHIPS
    chip = 2 * lax.axis_index("x") + lax.axis_index("y")
    c_arr = lax.axis_index("c").astype(jnp.int32).reshape(1)

    shards = {n: w[n].astype(BF16) for n in sorted({m[1] for m in _MATS})}
    vec_all = _allgather8(_pack_sharded(w), "gather_vectors")
    vec_full = jnp.concatenate([vec_all[2 * j] for j in range(N_CHIPS)], axis=1)
    b_in = vec_all[0::2, 0:2, :].reshape(1, 2 * D)
    V = {"conv_b_a": b_in[:, :D], "conv_b_g": b_in[:, D:],
         "conv_dw": _pad_rows(vec_full[2:2 + CONV_WIDTH], CONV_PAD), "conv_dw_b": vec_full[33:34],
         "conv_ln_g": vec_full[34:35], "conv_ln_b": vec_full[35:36],
         "kv_ln_g": kv_ln_g.reshape(1, D), "kv_ln_b": kv_ln_b.reshape(1, D)}
    for l in range(2):
        for n in ("ln1_g", "ln1_b", "ln2_g", "ln2_b"):
            V[f"{n}{l}"] = w[n][l].reshape(1, D)

    half = HEAD_DIM // 2
    inv_freq = ROPE_THETA ** (-jnp.arange(half, dtype=F32) * (2.0 / HEAD_DIM))
    ang = positions[0].astype(F32)[:, None] * inv_freq
    cos, sin = jnp.cos(ang), jnp.sin(ang)
    cosf = jnp.concatenate([cos, cos], axis=-1)
    sinf = jnp.concatenate([-sin, sin], axis=-1)

    reducer = _Reducer(w, mom, var, c_arr)
    loss_cols, grad_x, _, gv = _local_step(x[0], p[:, 0], cosf, sinf, loss_target[0], None, V, shards, reducer,
                                           chip.astype(jnp.int32).reshape(1))
    loss = lax.psum(jnp.sum(loss_cols), ("x", "y", "c"))
    out = dict(reducer.out)

    gpack = jnp.concatenate([gv["conv_b_a"], gv["conv_b_g"], gv["conv_dw"][:CONV_WIDTH], gv["conv_dw_b"],
                             gv["conv_ln_g"], gv["conv_ln_b"], gv["kv_ln_g"], gv["kv_ln_b"],
                             gv["ln1_g0"], gv["ln1_g1"], gv["ln1_b0"], gv["ln1_b1"],
                             gv["ln2_g0"], gv["ln2_g1"], gv["ln2_b0"], gv["ln2_b1"]], axis=0)
    gsum = _sum8(_allgather8(_pad_rows(gpack, 48), "gather_vector_grads"), "sum_vector_grads")
    g_b = lax.dynamic_slice_in_dim(jnp.concatenate([gsum[0:1], gsum[1:2]], axis=1), chip * 2 * n4, 2 * n4, axis=1)
    g_sh = lax.dynamic_slice_in_dim(gsum[2:36], chip * n4, n4, axis=1)
    g_sh = _pad_rows(jnp.concatenate([g_b.reshape(2, n4), g_sh], axis=0), 40)
    d_sh, m_sh, v_sh = _adamw_small(g_sh, _pack_sharded(w), _pack_sharded(mom), _pack_sharded(var), "adamw_sharded_vectors")
    g_rep = _pad_rows(gsum[36:46], 16)
    d_rep, m_rep, v_rep = _adamw_small(g_rep, _pack_replicated(w), _pack_replicated(mom), _pack_replicated(var),
                                       "adamw_replicated_vectors")
    small = {}
    for i, (sh, rep) in enumerate(((g_sh, g_rep), (d_sh, d_rep), (m_sh, m_rep), (v_sh, v_rep))):
        d = {**_unpack_sharded(sh, w), **_unpack_replicated(rep, w)}
        for n, val in d.items():
            small.setdefault(n, [None] * 4)[i] = val
    for n in small:
        out[n] = small[n]

    res = [loss, grad_x[None]]
    for i in range(4):
        res += [out[n][i] for n in _WEIGHTS]
    return tuple(res)
```

```python
import functools

import jax
import jax.numpy as jnp
from jax import lax
from jax.experimental import pallas as pl
from jax.experimental.pallas import tpu as pltpu

F32 = jnp.float32
BF16 = jnp.bfloat16

HEAD_DIM = 128
ATTN_BLOCK = 128
DILATIONS = (1, 4, 16)
N_GROUPS = 3
CONV_WIDTH = 31
CONV_PAD = 32
CONV_ROWS = 32
EPILOGUE_ROWS = 128
ROPE_THETA = 10000.0
LN_EPS = 1e-5
ALPHA = 4.0 ** 0.25
ATTN_SCALE = HEAD_DIM ** -0.5
NEG = -1e30

ADAM_LR = 0.001
ADAM_B1 = 0.9
ADAM_B2 = 0.999
ADAM_EPS = 1e-08
ADAM_WD = 0.01
ADAM_STEP = 10

N_CHIPS = 4
VMEM_CAP = 60 << 20
MESH = pl.DeviceIdType.MESH
ANY = pl.BlockSpec(memory_space=pl.ANY)


def _vmem(nbytes):
    return int(min(max(2 * nbytes + (8 << 20), 24 << 20), VMEM_CAP))


def _fit(tile, n):
    if n <= tile:
        return n
    t = tile - tile % 128
    while n % t:
        t -= 128
    return t


def _nbytes(shape, dtype):
    n = 1
    for s in shape:
        n *= s
    return n * jnp.dtype(dtype).itemsize


_DIMS = {"nn": (((1,), (0,)), ((), ())), "nt": (((1,), (1,)), ((), ())), "tn": (((0,), (0,)), ((), ()))}


def _pcall(body, *, name, grid, in_specs, out_specs, out_shape, operands, scratch_shapes=(), vmem, carry=None):
    if carry is None:
        return pl.pallas_call(
            body, name=name, grid=grid, in_specs=in_specs, out_specs=out_specs, out_shape=out_shape,
            scratch_shapes=list(scratch_shapes),
            compiler_params=pltpu.CompilerParams(dimension_semantics=("arbitrary",) * len(grid), vmem_limit_bytes=vmem),
        )(*operands)
    n_in, n_out, n_scr = len(in_specs), len(out_specs), len(scratch_shapes)
    c_in, c_out = len(carry.ins), len(carry.out_shape)

    def wrapped(*refs):
        ins, refs = refs[:n_in], refs[n_in:]
        c_ins, refs = refs[:c_in], refs[c_in:]
        outs, refs = refs[:n_out], refs[n_out:]
        c_outs, refs = refs[:c_out], refs[c_out:]
        scr, c_sems = refs[:n_scr], refs[n_scr:]
        first = functools.reduce(jnp.logical_and, [pl.program_id(d) == 0 for d in range(len(grid))])
        last = functools.reduce(jnp.logical_and, [pl.program_id(d) == grid[d] - 1 for d in range(len(grid))])
        pl.when(first)(lambda: carry.start(c_ins, c_outs, c_sems))
        body(*ins, *outs, *scr)
        pl.when(last)(lambda: carry.finish(c_ins, c_outs, c_sems))

    res = pl.pallas_call(
        wrapped, name=name, grid=grid, in_specs=list(in_specs) + [ANY] * c_in, out_specs=list(out_specs) + [ANY] * c_out,
        out_shape=list(out_shape) + list(carry.out_shape), scratch_shapes=list(scratch_shapes) + list(carry.scratch),
        input_output_aliases={len(operands) + i: n_out + o for i, o in carry.aliases.items()},
        compiler_params=pltpu.CompilerParams(dimension_semantics=("arbitrary",) * len(grid), vmem_limit_bytes=vmem),
    )(*operands, *carry.ins)
    carry.set_result(res[n_out:])
    return res[:n_out]


def _mm(a, b, *, mode, outs, name, epilogue=None, extras=(), rextras=(), vecs=(), a_sel=None, b_sel=None,
        tm=None, tn=2048, tk=None, layout=None, carry=None, ep_rows=None, a_fn=None, sums=0):
    a2, b2 = a.shape[-2:], b.shape[-2:]
    if mode == "nn":
        (M, K), (K2, N) = a2, b2
    elif mode == "nt":
        (M, K), (N, K2) = a2, b2
    else:
        (K, M), (K2, N) = a2, b2
    assert K == K2, (a.shape, b.shape, mode)
    if tm is None:
        tm = 1024 if mode == "tn" else 512
    if tk is None:
        tk = 2048
    if layout is not None:
        kind, nslots = layout
        r, c = (M // 2, N // nslots) if kind == "col" else (M // nslots, N // 2)
        tm, tn = _fit(tm, r), _fit(tn, c)
        assert r % tm == 0 and c % tn == 0
    else:
        tm, tn = _fit(tm, M), _fit(tn, N)
    tk = _fit(tk, K)
    assert M % tm == 0 and N % tn == 0 and K % tk == 0, (M, N, K, tm, tn, tk)
    nk = K // tk
    grid = (N // tn, M // tm, nk)

    def spec(arr, sel, blk, imap):
        if arr.ndim == 3:
            return pl.BlockSpec((None,) + blk, lambda j, i, k: (sel,) + imap(j, i, k))
        return pl.BlockSpec(blk, imap)

    if mode == "tn":
        a_spec = spec(a, a_sel, (tk, tm), lambda j, i, k: (k, i))
    else:
        a_spec = spec(a, a_sel, (tm, tk), lambda j, i, k: (i, k))
    if mode == "nt":
        b_spec = spec(b, b_sel, (tn, tk), lambda j, i, k: (j, k))
    else:
        b_spec = spec(b, b_sel, (tk, tn), lambda j, i, k: (k, j))
    in_specs = [a_spec, b_spec]
    in_specs += [pl.BlockSpec((tm, tn), lambda j, i, k: (i, j)) for _ in extras]
    in_specs += [pl.BlockSpec((tm, e.shape[1]), lambda j, i, k: (i, 0)) for e in rextras]
    in_specs += [pl.BlockSpec((1, tn), lambda j, i, k: (0, j)) for _ in vecs]

    if layout is None:
        out_shape = [jax.ShapeDtypeStruct((M, N), d) for d in outs] + [jax.ShapeDtypeStruct((1, N), F32)] * sums
        out_specs = [pl.BlockSpec((tm, tn), lambda j, i, k: (i, j)) for _ in outs]
        out_specs += [pl.BlockSpec((1, tn), lambda j, i, k: (0, j))] * sums
    else:
        assert len(outs) == 1
        out_shape = [jax.ShapeDtypeStruct((2, nslots, r, c), outs[0])]
        rb, cb = r // tm, c // tn
        if kind == "col":
            omap = lambda j, i, k: (i // rb, j // cb, i % rb, j % cb)
        else:
            omap = lambda j, i, k: (j // cb, i // rb, i % rb, j % cb)
        out_specs = [pl.BlockSpec((None, None, tm, tn), omap)]

    ne, nr, nv, no = len(extras), len(rextras), len(vecs), len(outs)
    dims = _DIMS[mode]

    def body(*refs):
        a_ref, b_ref = refs[0], refs[1]
        rest = refs[2:2 + ne + nr + nv]
        o_refs = refs[2 + ne + nr + nv:2 + ne + nr + nv + no]
        s_refs = refs[2 + ne + nr + nv + no:2 + ne + nr + nv + no + sums]

        def finish(total):
            if epilogue is None:
                for o in o_refs:
                    o[...] = total.astype(o.dtype)
                return
            step = min(ep_rows or tm, tm)
            col_sums = [None] * sums
            for r0 in range(0, tm, step):
                rows = slice(r0, r0 + step)
                tiles = [x[rows, :] for x in rest[:ne + nr]] + [x[...] for x in rest[ne + nr:]]
                res = epilogue(total[rows, :], *tiles)
                for o, val in zip(o_refs, res[:no]):
                    o[rows, :] = val.astype(o.dtype)
                for n_, val in enumerate(res[no:]):
                    part_sum = jnp.sum(val, axis=0, keepdims=True)
                    col_sums[n_] = part_sum if col_sums[n_] is None else col_sums[n_] + part_sum
            if sums:
                @pl.when(pl.program_id(1) == 0)
                def _():
                    for s_ref, val in zip(s_refs, col_sums):
                        s_ref[...] = val

                @pl.when(pl.program_id(1) > 0)
                def _():
                    for s_ref, val in zip(s_refs, col_sums):
                        s_ref[...] += val

        def product():
            a_tile = a_ref[...] if a_fn is None else a_fn(a_ref[...])
            return lax.dot_general(a_tile.astype(BF16), b_ref[...].astype(BF16), dims, preferred_element_type=F32)

        if nk == 1:
            finish(product())
            return
        acc = refs[-1]
        k = pl.program_id(2)

        @pl.when(k == 0)
        def _():
            acc[...] = product()

        @pl.when(k > 0)
        def _():
            acc[...] += product()

        @pl.when(k == nk - 1)
        def _():
            finish(acc[...])

    blk = (_nbytes((tm, tk), a.dtype) + _nbytes((tk, tn), b.dtype) + sum(_nbytes((tm, tn), e.dtype) for e in extras)
           + sum(_nbytes((tm, tn), d) for d in outs) + 2 * tm * tn * 4)
    res = _pcall(body, name=name, grid=grid, in_specs=in_specs, out_specs=out_specs, out_shape=out_shape,
                 operands=(a, b, *extras, *rextras, *vecs),
                 scratch_shapes=[pltpu.VMEM((tm, tn), F32)] if nk > 1 else [], vmem=_vmem(blk), carry=carry)
    return res[0] if no + sums == 1 else tuple(res)


def _rows(fn, rows, vecs, outs, sums, *, tm, name, carry=None):
    S = rows[0].shape[0]
    tm = min(tm, S)
    assert S % tm == 0
    nr, nv, no, ns = len(rows), len(vecs), len(outs), len(sums)

    def body(*refs):
        vals = fn(*[r[...] for r in refs[:nr + nv]])
        o_refs = refs[nr + nv:nr + nv + no]
        s_refs = refs[nr + nv + no:]
        for o, val in zip(o_refs, vals[:no]):
            o[...] = val.astype(o.dtype)
        if ns:
            @pl.when(pl.program_id(0) == 0)
            def _():
                for s in s_refs:
                    s[...] = jnp.zeros_like(s)

            for s, val in zip(s_refs, vals[no:]):
                s[...] += jnp.sum(val.astype(F32), axis=0, keepdims=True)

    in_specs = [pl.BlockSpec((tm, r.shape[1]), lambda i: (i, 0)) for r in rows]
    in_specs += [pl.BlockSpec(v.shape, lambda i: (0, 0)) for v in vecs]
    out_specs = [pl.BlockSpec((tm, c), lambda i: (i, 0)) for c, _ in outs]
    out_specs += [pl.BlockSpec((1, c), lambda i: (0, 0)) for c in sums]
    out_shape = [jax.ShapeDtypeStruct((S, c), d) for c, d in outs]
    out_shape += [jax.ShapeDtypeStruct((1, c), F32) for c in sums]
    blk = sum(_nbytes((tm, r.shape[1]), r.dtype) for r in rows) + sum(_nbytes((tm, c), d) for c, d in outs)
    blk += 6 * tm * max(r.shape[1] for r in rows) * 4
    res = _pcall(body, name=name, grid=(S // tm,), in_specs=in_specs, out_specs=out_specs, out_shape=out_shape,
                 operands=(*rows, *vecs), vmem=_vmem(blk), carry=carry)
    return tuple(res)


def _ln_norm(z):
    mu = jnp.mean(z, axis=-1, keepdims=True)
    d = z - mu
    var = jnp.mean(d * d, axis=-1, keepdims=True)
    rstd = lax.rsqrt(var + LN_EPS)
    return d * rstd, rstd


def _ln(z, g, b):
    return _ln_norm(z)[0] * g + b


def _ln_bwd(dy, n, rstd, g):
    dn = dy * g
    return rstd * (dn - jnp.mean(dn, axis=-1, keepdims=True) - n * jnp.mean(dn * n, axis=-1, keepdims=True))


def _sq_relu(t):
    return jnp.square(jnp.maximum(t.astype(F32), 0.0))


def _sigmoid(x):
    return 1.0 / (1.0 + jnp.exp(-x))


def _per_head(x, fn):
    h = x.shape[1] // HEAD_DIM
    return jnp.concatenate([fn(x[:, i * HEAD_DIM:(i + 1) * HEAD_DIM], i) for i in range(h)], axis=1)


def _rot(x, cosf, sinf):
    return _per_head(x, lambda xh, i: xh * cosf + pltpu.roll(xh, HEAD_DIM // 2, 1) * sinf)


def _rot_t(dy, cosf, sinf):
    return _per_head(dy, lambda dh, i: dh * cosf + pltpu.roll(dh * sinf, HEAD_DIM // 2, 1))


def _shift_copies(win):
    rows = win.shape[1] - 8
    for s in range(1, 8):
        win[s, 0:rows, :] = win[0, s:s + rows, :]


def _rows_at(win, start):
    s = start % 8
    return win[s, start - s:start - s + CONV_ROWS, :]


def _conv_fwd(glu, dw, dwb, *, tm=256, tc=512, name="conv_fwd", carry=None):
    S, D = glu.shape
    tm, tc = min(tm, S), min(tc, D)
    ni = S // tm

    def body(cur_ref, prev_ref, dw_ref, dwb_ref, o_ref, win):
        i = pl.program_id(1)
        tail = prev_ref[tm - CONV_PAD:tm, :]
        win[0, 0:CONV_PAD, :] = jnp.where(i > 0, tail, jnp.zeros_like(tail))
        win[0, CONV_PAD:CONV_PAD + tm, :] = cur_ref[...]
        _shift_copies(win)
        first = CONV_PAD - CONV_WIDTH + 1
        for r0 in range(0, tm, CONV_ROWS):
            acc = jnp.zeros((CONV_ROWS, tc), F32) + dwb_ref[...]
            for k in range(CONV_WIDTH):
                acc = acc + _rows_at(win, r0 + first + k) * dw_ref[k:k + 1, :]
            o_ref[r0:r0 + CONV_ROWS, :] = acc

    return _pcall(
        body, name=name, grid=(D // tc, ni),
        in_specs=[pl.BlockSpec((tm, tc), lambda j, i: (i, j)),
                  pl.BlockSpec((tm, tc), lambda j, i: (jnp.maximum(i - 1, 0), j)),
                  pl.BlockSpec((CONV_PAD, tc), lambda j, i: (0, j)),
                  pl.BlockSpec((1, tc), lambda j, i: (0, j))],
        out_specs=[pl.BlockSpec((tm, tc), lambda j, i: (i, j))],
        out_shape=[jax.ShapeDtypeStruct((S, D), F32)],
        scratch_shapes=[pltpu.VMEM((8, tm + CONV_PAD, tc), F32)],
        operands=(glu, glu, dw, dwb), vmem=_vmem(8 * tm * tc * 4), carry=carry)[0]


def _conv_bwd(dc, glu, a_pre, g_pre, dw, ba, bg, *, tm=256, tc=512, name="conv_bwd", carry=None):
    S, D = dc.shape
    tm, tc = min(tm, S), min(tc, D)
    ni = S // tm

    def fold8(v):
        out = v[0:8]
        for r in range(8, CONV_ROWS, 8):
            out = out + v[r:r + 8]
        return out

    def body(dc_ref, dcn_ref, glu_ref, glup_ref, a_ref, g_ref, dw_ref, ba_ref, bg_ref,
             da_ref, dg_ref, ddw_ref, dba_ref, dbg_ref, dwin, gwin, taps):
        i = pl.program_id(1)

        @pl.when(i == 0)
        def _():
            ddw_ref[...] = jnp.zeros_like(ddw_ref)
            dba_ref[...] = jnp.zeros_like(dba_ref)
            dbg_ref[...] = jnp.zeros_like(dbg_ref)

        head = dcn_ref[0:CONV_PAD, :]
        dwin[0, 0:tm, :] = dc_ref[...]
        dwin[0, tm:tm + CONV_PAD, :] = jnp.where(i < ni - 1, head, jnp.zeros_like(head))
        tail = glup_ref[tm - CONV_PAD:tm, :]
        gwin[0, 0:CONV_PAD, :] = jnp.where(i > 0, tail, jnp.zeros_like(tail))
        gwin[0, CONV_PAD:CONV_PAD + tm, :] = glu_ref[...]
        _shift_copies(dwin)
        _shift_copies(gwin)
        taps[...] = jnp.zeros_like(taps)
        first = CONV_PAD - CONV_WIDTH + 1
        sum_a = jnp.zeros((8, tc), F32)
        sum_g = jnp.zeros((8, tc), F32)
        for r0 in range(0, tm, CONV_ROWS):
            dcur = dc_ref[r0:r0 + CONV_ROWS, :]
            dglu = jnp.zeros((CONV_ROWS, tc), F32)
            for k in range(CONV_WIDTH):
                dglu = dglu + _rows_at(dwin, r0 + CONV_WIDTH - 1 - k) * dw_ref[k:k + 1, :]
                taps[k] += fold8(dcur * _rows_at(gwin, r0 + first + k))
            a = a_ref[r0:r0 + CONV_ROWS, :] + ba_ref[...]
            sg = _sigmoid(g_ref[r0:r0 + CONV_ROWS, :] + bg_ref[...])
            da = dglu * sg
            dg = dglu * a * sg * (1.0 - sg)
            da_ref[r0:r0 + CONV_ROWS, :] = da.astype(BF16)
            dg_ref[r0:r0 + CONV_ROWS, :] = dg.astype(BF16)
            sum_a = sum_a + fold8(da)
            sum_g = sum_g + fold8(dg)
        ddw_ref[...] += jnp.sum(taps[...], axis=1)
        dba_ref[...] += jnp.sum(sum_a, axis=0, keepdims=True)
        dbg_ref[...] += jnp.sum(sum_g, axis=0, keepdims=True)

    tile = lambda f: pl.BlockSpec((tm, tc), f)
    vec = pl.BlockSpec((1, tc), lambda j, i: (0, j))
    return _pcall(
        body, name=name, grid=(D // tc, ni),
        in_specs=[tile(lambda j, i: (i, j)), tile(lambda j, i: (jnp.minimum(i + 1, ni - 1), j)),
                  tile(lambda j, i: (i, j)), tile(lambda j, i: (jnp.maximum(i - 1, 0), j)),
                  tile(lambda j, i: (i, j)), tile(lambda j, i: (i, j)),
                  pl.BlockSpec((CONV_PAD, tc), lambda j, i: (0, j)), vec, vec],
        out_specs=[tile(lambda j, i: (i, j)), tile(lambda j, i: (i, j)),
                   pl.BlockSpec((CONV_PAD, tc), lambda j, i: (0, j)), vec, vec],
        out_shape=[jax.ShapeDtypeStruct((S, D), BF16), jax.ShapeDtypeStruct((S, D), BF16),
                   jax.ShapeDtypeStruct((CONV_PAD, D), F32), jax.ShapeDtypeStruct((1, D), F32),
                   jax.ShapeDtypeStruct((1, D), F32)],
        scratch_shapes=[pltpu.VMEM((8, tm + CONV_PAD, tc), F32), pltpu.VMEM((8, tm + CONV_PAD, tc), F32),
                        pltpu.VMEM((CONV_PAD, 8, tc), F32)],
        operands=(dc, dc, glu, glu, a_pre, g_pre, dw, ba, bg), vmem=_vmem(16 * tm * tc * 4), carry=carry)


def _nt(a, b):
    return lax.dot_general(a, b, _DIMS["nt"], preferred_element_type=F32)


def _tn(a, b):
    return lax.dot_general(a, b, _DIMS["tn"], preferred_element_type=F32)


HEADS_TOGETHER = 8
STAT_LANES = 128


def _per_head_pack(cols):
    rows = cols[0].shape[0]
    lane = lax.broadcasted_iota(jnp.int32, (rows, STAT_LANES), 1)
    out = jnp.zeros((rows, STAT_LANES), F32)
    for h, col in enumerate(cols):
        out = jnp.where(lane == h, col, out)
    return out


def _window_mask(qi, kj, first_key):
    B = ATTN_BLOCK
    return ((kj < B) & (kj >= qi) & (kj >= first_key)) | ((kj >= B) & (kj - B <= qi))


def _attn_fwd(q_rot, k, v, g, dil):
    S, D = k.shape
    H = D // HEAD_DIM
    L = S // dil
    nb_count = L // ATTN_BLOCK
    B = ATTN_BLOCK

    def body(q_ref, kc_ref, kp_ref, vc_ref, vp_ref, o_ref, lse_ref):
        nb = pl.program_id(1)
        qi = lax.broadcasted_iota(jnp.int32, (B, 2 * B), 0)
        kj = lax.broadcasted_iota(jnp.int32, (B, 2 * B), 1)
        valid = _window_mask(qi, kj, jnp.where(nb > 0, 0, B))
        stats = []
        for h0 in range(0, H, HEADS_TOGETHER):
            heads = range(h0, min(h0 + HEADS_TOGETHER, H))
            hs = [slice(h * HEAD_DIM, (h + 1) * HEAD_DIM) for h in heads]
            kk = [jnp.concatenate([kp_ref[:, c], kc_ref[:, c]], axis=0) for c in hs]
            vv = [jnp.concatenate([vp_ref[:, c], vc_ref[:, c]], axis=0) for c in hs]
            s = [jnp.where(valid, _nt(q_ref[:, c], kk_) * ATTN_SCALE, NEG) for c, kk_ in zip(hs, kk)]
            m = [jnp.max(s_, axis=1, keepdims=True) for s_ in s]
            p = [jnp.exp(s_ - m_) for s_, m_ in zip(s, m)]
            l = [jnp.sum(p_, axis=1, keepdims=True) for p_ in p]
            o = [jnp.dot(p_.astype(BF16), vv_, preferred_element_type=F32) / l_ for p_, vv_, l_ in zip(p, vv, l)]
            for c, o_ in zip(hs, o):
                o_ref[:, c] = o_.astype(o_ref.dtype)
            stats += [m_ + jnp.log(l_) for m_, l_ in zip(m, l)]
        lse_ref[...] = _per_head_pack(stats)

    blk = lambda f: pl.BlockSpec((B, D), f)
    cur = lambda r, nb: (nb, r)
    prev = lambda r, nb: (jnp.maximum(nb - 1, 0), r)
    o, lse = pl.pallas_call(
        body, name=f"attn_fwd_g{g}", grid=(dil, nb_count),
        in_specs=[blk(lambda r, nb: (nb, r * N_GROUPS + g)), blk(cur), blk(prev), blk(cur), blk(prev)],
        out_specs=[blk(cur), pl.BlockSpec((B, STAT_LANES), cur)],
        out_shape=[jax.ShapeDtypeStruct((L, dil * D), BF16), jax.ShapeDtypeStruct((L, dil * STAT_LANES), F32)],
        compiler_params=pltpu.CompilerParams(dimension_semantics=("parallel", "arbitrary"),
                                             vmem_limit_bytes=_vmem(12 * B * D * 4)),
    )(q_rot.reshape(L, dil * N_GROUPS * D), k.reshape(L, dil * D), k.reshape(L, dil * D),
      v.reshape(L, dil * D), v.reshape(L, dil * D))
    return o.reshape(S, D), lse.reshape(S, STAT_LANES)


def _attn_bwd(q_rot, k, v, do, lse, dlt, g, dil, *, name, carry=None):
    S, D = k.shape
    H = D // HEAD_DIM
    L = S // dil
    nb_count = L // ATTN_BLOCK
    B = ATTN_BLOCK

    def body(q_ref, kc_ref, kp_ref, vc_ref, vp_ref, do_ref, l_ref, d_ref, dq_ref, dk_ref, dv_ref, keep_k, keep_v):
        s_id = pl.program_id(1)

        @pl.when(s_id == 0)
        def _():
            keep_k[...] = jnp.zeros_like(keep_k)
            keep_v[...] = jnp.zeros_like(keep_v)

        @pl.when(s_id < nb_count)
        def _():
            qi = lax.broadcasted_iota(jnp.int32, (B, 2 * B), 0)
            kj = lax.broadcasted_iota(jnp.int32, (B, 2 * B), 1)
            valid = _window_mask(qi, kj, jnp.where(s_id > 0, 0, B))
            for h0 in range(0, H, HEADS_TOGETHER):
                heads = list(range(h0, min(h0 + HEADS_TOGETHER, H)))
                hs = [slice(h * HEAD_DIM, (h + 1) * HEAD_DIM) for h in heads]
                q = [q_ref[:, c] for c in hs]
                dout = [do_ref[:, c] for c in hs]
                kk = [jnp.concatenate([kp_ref[:, c], kc_ref[:, c]], axis=0) for c in hs]
                vv = [jnp.concatenate([vp_ref[:, c], vc_ref[:, c]], axis=0) for c in hs]
                s = [jnp.where(valid, _nt(q_, kk_) * ATTN_SCALE, NEG) for q_, kk_ in zip(q, kk)]
                dp = [_nt(do_, vv_) for do_, vv_ in zip(dout, vv)]
                p = [jnp.exp(s_ - l_ref[:, h:h + 1]) for s_, h in zip(s, heads)]
                ds = [(p_ * (dp_ - d_ref[:, h:h + 1])).astype(BF16) for p_, dp_, h in zip(p, dp, heads)]
                dq = [jnp.dot(ds_, kk_, preferred_element_type=F32) * ATTN_SCALE for ds_, kk_ in zip(ds, kk)]
                dkk = [_tn(ds_, q_) * ATTN_SCALE for ds_, q_ in zip(ds, q)]
                dvv = [_tn(p_.astype(BF16), do_) for p_, do_ in zip(p, dout)]
                for c, dq_, dkk_, dvv_ in zip(hs, dq, dkk, dvv):
                    dq_ref[:, c] = dq_.astype(dq_ref.dtype)
                    dk_ref[:, c] = (keep_k[:, c] + dkk_[0:B]).astype(dk_ref.dtype)
                    dv_ref[:, c] = (keep_v[:, c] + dvv_[0:B]).astype(dv_ref.dtype)
                    keep_k[:, c] = dkk_[B:2 * B]
                    keep_v[:, c] = dvv_[B:2 * B]

        @pl.when(s_id == nb_count)
        def _():
            dk_ref[...] = keep_k[...].astype(dk_ref.dtype)
            dv_ref[...] = keep_v[...].astype(dv_ref.dtype)

    last = nb_count - 1
    blk = lambda f: pl.BlockSpec((B, D), f)
    cur = lambda r, s: (jnp.minimum(s, last), r)
    prev = lambda r, s: (jnp.maximum(jnp.minimum(s, last) - 1, 0), r)
    lag = lambda r, s: (jnp.maximum(s - 1, 0), r)
    qcur = lambda r, s: (jnp.minimum(s, last), r * N_GROUPS + g)
    qv = q_rot.reshape(L, dil * N_GROUPS * D)
    view = lambda t: t.reshape(L, dil * D)
    sview = lambda t: t.reshape(L, dil * STAT_LANES)
    stat = lambda f: pl.BlockSpec((B, STAT_LANES), f)
    dq, dk, dv = _pcall(
        body, name=name, grid=(dil, nb_count + 1),
        in_specs=[blk(qcur), blk(cur), blk(prev), blk(cur), blk(prev), blk(cur), stat(cur), stat(cur)],
        out_specs=[blk(cur), blk(lag), blk(lag)],
        out_shape=[jax.ShapeDtypeStruct((L, dil * D), BF16)] * 3,
        scratch_shapes=[pltpu.VMEM((B, D), F32), pltpu.VMEM((B, D), F32)],
        operands=(qv, view(k), view(k), view(v), view(v), view(do), sview(lse), sview(dlt)),
        vmem=_vmem(24 * B * D * 4), carry=carry)
    return dq.reshape(S, D), dk.reshape(S, D), dv.reshape(S, D)


def _mlp_ple_fwd(z1, h1b, p_l, W, vec, l, run, kv_vec=None, target=None):
    D = z1.shape[1]
    g1, b1, g2, b2 = vec

    t = run(_mm, h1b, W[f"mlp_up{l}"], mode="nn", outs=[BF16], tm=1024, name=f"mlp_up{l}")

    def z2_ep(acc, z1_t, g1_, b1_, g2_, b2_):
        z2 = ALPHA * _ln(z1_t, g1_, b1_) + acc
        return z2, _ln(z2, g2_, b2_)

    z2, h2b = run(_mm, t, W[f"mlp_down{l}"], mode="nn", outs=[F32, BF16], extras=[z1], vecs=[g1, b1, g2, b2], a_fn=_sq_relu,
                  epilogue=z2_ep, ep_rows=EPILOGUE_ROWS, name=f"mlp_down{l}")
    act = None
    pp = run(_mm, p_l, W[f"ple_proj{l}"], mode="nn", outs=[F32], name=f"ple_proj{l}")
    if kv_vec is None:
        def head_ep(acc, z2_t, pp_t, tgt_t, g2_, b2_):
            y, gt = _ple_out(z2_t, pp_t, acc, g2_, b2_)
            err = y - tgt_t
            dy = err * (1.0 / D)
            return dy, dy * gt, dy * pp_t * gt * (1.0 - gt), 0.5 * err * err * (1.0 / D)

        dy, d_pp, d_gpre, loss_cols = run(_mm, h2b, W[f"ple_gate{l}"], mode="nn", outs=[F32, BF16, BF16], sums=1,
                                          extras=[z2, pp, target], vecs=[g2, b2], epilogue=head_ep,
                                          ep_rows=EPILOGUE_ROWS, tm=256, name=f"ple_gate{l}")
        return t, act, z2, h2b, pp, dy, d_pp, d_gpre, loss_cols

    def x1_ep(acc, z2_t, pp_t, g2_, b2_, kg, kb):
        x1, _ = _ple_out(z2_t, pp_t, acc, g2_, b2_)
        return acc, x1, _ln(x1, kg, kb)

    gpre, x1, kvn = run(_mm, h2b, W[f"ple_gate{l}"], mode="nn", outs=[F32, F32, BF16], extras=[z2, pp],
                        vecs=[g2, b2, *kv_vec], epilogue=x1_ep, ep_rows=EPILOGUE_ROWS, tm=256, name=f"ple_gate{l}")
    return t, act, z2, h2b, pp, gpre, x1, kvn


def _mlp_ple_bwd(dy, d_pp, d_gpre, p_l, z1, h1b, t, act, z2, h2b, wts, vec, l, run, produce):
    D = z1.shape[1]
    up, down, pp_w, pg_w = wts
    g1, b1, g2, b2 = vec
    produce(f"ple_proj{l}", run(_mm, p_l, d_pp, mode="tn", outs=[BF16], layout=("col", N_CHIPS), name=f"d_ple_proj{l}"))
    produce(f"ple_gate{l}", run(_mm, h2b, d_gpre, mode="tn", outs=[BF16], layout=("row", N_CHIPS), name=f"d_ple_gate{l}"))
    def ln2_bwd(acc, dy_t, z2_t, g2_):
        dh2 = acc + dy_t
        n, rstd = _ln_norm(z2_t)
        dz2 = _ln_bwd(dh2, n, rstd, g2_)
        return dz2, dz2, dh2 * n, dh2

    dz2, dz2b, dg2, db2 = run(_mm, d_gpre, pg_w, mode="nt", outs=[F32, BF16], sums=2, extras=[dy, z2], vecs=[g2],
                              epilogue=ln2_bwd, ep_rows=EPILOGUE_ROWS, tm=256, name=f"dh2_{l}")
    produce(f"mlp_down{l}", run(_mm, t, dz2b, mode="tn", outs=[BF16], layout=("row", N_CHIPS), a_fn=_sq_relu,
                                name=f"d_mlp_down{l}"))
    dt = run(_mm, dz2b, down, mode="nt", outs=[BF16], extras=[t],
             epilogue=lambda acc, t_: (acc * 2.0 * jnp.maximum(t_.astype(F32), 0.0),), name=f"dt{l}")
    produce(f"mlp_up{l}", run(_mm, h1b, dt, mode="tn", outs=[BF16], layout=("col", N_CHIPS), name=f"d_mlp_up{l}"))
    def ln1_bwd(acc, dz2_t, z1_t, g1_):
        dh1 = acc + ALPHA * dz2_t
        n, rstd = _ln_norm(z1_t)
        dz1 = _ln_bwd(dh1, n, rstd, g1_)
        return dz1, dz1, dh1 * n, dh1

    dz1, dz1b, dg1, db1 = run(_mm, dt, up, mode="nt", outs=[F32, BF16], sums=2, extras=[dz2, z1], vecs=[g1],
                              epilogue=ln1_bwd, ep_rows=EPILOGUE_ROWS, tk=1024, name=f"dh1_{l}")
    return dz1, dz1b, (dg1, db1, dg2, db2)


def _ple_out(z2, pp, gpre, g2, b2):
    gt = _sigmoid(gpre)
    return _ln(z2, g2, b2) + pp * gt, gt


_GATHER_AT = {
    "conv_in_a": ("conv_w_out",),
    "conv_in_g": ("ple_gate0", "ple_proj0"),
    "conv_fwd": ("mlp_up0",),
    "mlp_up0": ("mlp_down0",),
    "mlp_down0": ("attn_w_q", "w_kv"),
    "kv_k": ("attn_w_o",),
    "kv_v": ("ple_proj1", "ple_gate1"),
    "attn_q": ("mlp_up1",),
    "mlp_up1": ("mlp_down1",),
}
_GATHER_FIRST = ("conv_w_in",)


def _local_step(x, p, cosf, sinf, target, W, V, shards=None, reducer=None, chip_arr=None):
    S, D = x.shape
    gw, gv = {}, {}
    if shards is not None:
        W = dict(_Gather(_GATHER_FIRST, shards, chip_arr).run_alone("gather_first"))

    def run(fn, *args, name, **kw):
        gather = _Gather(_GATHER_AT[name], shards, chip_arr) if (shards is not None and name in _GATHER_AT) else None
        carry = gather if reducer is None or gather is not None else reducer.carry(name)
        out = fn(*args, name=name, carry=carry, **kw)
        if gather is not None:
            W.update(gather.result)
        elif reducer is not None:
            reducer.carried()
        return out

    a_pre = run(_mm, x, W["conv_w_in"], b_sel=0, mode="nn", outs=[F32], name="conv_in_a")
    g_pre, glu = run(_mm, x, W["conv_w_in"], b_sel=1, mode="nn", outs=[F32, F32], extras=[a_pre],
                     vecs=[V["conv_b_a"], V["conv_b_g"]], tm=256, ep_rows=EPILOGUE_ROWS,
                     epilogue=lambda acc, a, ba, bg: (acc, (a + ba) * _sigmoid(acc + bg)), name="conv_in_g")
    cv = run(_conv_fwd, glu, V["conv_dw"], V["conv_dw_b"], name="conv_fwd")

    def silu_ln(c, g_, b_):
        y = _ln(c, g_, b_)
        return (y * _sigmoid(y),)

    (sb,) = _rows(silu_ln, [cv], [V["conv_ln_g"], V["conv_ln_b"]], [(D, BF16)], [], tm=256, name="conv_ln_fwd")
    def z1_ep(acc, x_t, g_, b_):
        z1 = ALPHA * x_t + acc
        return z1, _ln(z1, g_, b_)

    vec0 = (V["ln1_g0"], V["ln1_b0"], V["ln2_g0"], V["ln2_b0"])
    vec1 = (V["ln1_g1"], V["ln1_b1"], V["ln2_g1"], V["ln2_b1"])
    z1_0, h1b_0 = _mm(sb, W["conv_w_out"], mode="nn", outs=[F32, BF16], extras=[x], vecs=[vec0[0], vec0[1]],
                      epilogue=z1_ep, ep_rows=EPILOGUE_ROWS, name="conv_out")
    t0, act0, z2_0, h2b_0, pp0, gpre0, x1, kvn = _mlp_ple_fwd(z1_0, h1b_0, p[0], W, vec0, 0, run,
                                                              (V["kv_ln_g"], V["kv_ln_b"]))

    rot_ep = lambda acc, c_, s_: (_rot(acc, c_, s_),)
    k_rot = run(_mm, kvn, W["w_kv"], b_sel=0, mode="nn", outs=[BF16], rextras=[cosf, sinf], epilogue=rot_ep, name="kv_k")
    v_b = run(_mm, kvn, W["w_kv"], b_sel=1, mode="nn", outs=[BF16], name="kv_v")
    q_rot = run(_mm, x1, W["attn_w_q"], mode="nn", outs=[BF16], rextras=[cosf, sinf], epilogue=rot_ep, name="attn_q")
    og, lg = [], []
    for g, dil in enumerate(DILATIONS):
        o_g, l_g = _attn_fwd(q_rot, k_rot, v_b, g, dil)
        og.append(o_g)
        lg.append(l_g)

    def merge(o0, o1, o2, l0, l1, l2):
        m = jnp.maximum(jnp.maximum(l0, l1), l2)
        e = [jnp.exp(l0 - m), jnp.exp(l1 - m), jnp.exp(l2 - m)]
        den = e[0] + e[1] + e[2]
        w = [e_g / den for e_g in e]
        o = _per_head(o0, lambda oh, h: sum(w[g][:, h:h + 1] * (o0, o1, o2)[g][:, h * HEAD_DIM:(h + 1) * HEAD_DIM].astype(F32)
                                            for g in range(N_GROUPS)))
        return o, m + jnp.log(den)

    ob, lse = _rows(merge, og + lg, [], [(D, BF16), (STAT_LANES, F32)], [], tm=256, name="attn_merge")
    z1_1, h1b_1 = _mm(ob, W["attn_w_o"], mode="nn", outs=[F32, BF16], extras=[x1], vecs=[vec1[0], vec1[1]],
                      epilogue=z1_ep, ep_rows=EPILOGUE_ROWS, name="attn_out")
    t1, act1, z2_1, h2b_1, pp1, dy1, d_pp1, d_gpre1, loss_cols = _mlp_ple_fwd(z1_1, h1b_1, p[1], W, vec1, 1, run,
                                                                              target=target)
    wts0 = (W["mlp_up0"], W["mlp_down0"], W["ple_proj0"], W["ple_gate0"])
    wts1 = (W["mlp_up1"], W["mlp_down1"], W["ple_proj1"], W["ple_gate1"])


    def produce(name, grad):
        gw[name] = grad
        if reducer is not None:
            reducer.produced(name, grad)

    dz1_1, dz1b_1, (gv["ln1_g1"], gv["ln1_b1"], gv["ln2_g1"], gv["ln2_b1"]) = _mlp_ple_bwd(
        dy1, d_pp1, d_gpre1, p[1], z1_1, h1b_1, t1, act1, z2_1, h2b_1, wts1, vec1, 1, run, produce)
    produce("attn_w_o", run(_mm, ob, dz1b_1, mode="tn", outs=[BF16], layout=("row", N_CHIPS), name="d_attn_w_o"))

    do_b = run(_mm, dz1b_1, W["attn_w_o"], mode="nt", outs=[BF16], name="attn_do")

    def delta(do_t, o_t):
        prod = do_t.astype(F32) * o_t.astype(F32)
        H = D // HEAD_DIM
        return (_per_head_pack([jnp.sum(prod[:, h * HEAD_DIM:(h + 1) * HEAD_DIM], axis=1, keepdims=True) for h in range(H)]),)

    (dlt,) = _rows(delta, [do_b, ob], [], [(STAT_LANES, F32)], [], tm=256, name="attn_delta")
    dqs, dks, dvs = [], [], []
    for g, dil in enumerate(DILATIONS):
        dq_g, dk_g, dv_g = run(_attn_bwd, q_rot, k_rot, v_b, do_b, lse, dlt, g, dil, name=f"attn_bwd_g{g}")
        dqs.append(dq_g)
        dks.append(dk_g)
        dvs.append(dv_g)

    def unrot(q0, q1, q2, k0, k1, k2, v0, v1, v2, c_, s_):
        dq = jnp.concatenate([_rot_t(t_.astype(F32), c_, s_) for t_ in (q0, q1, q2)], axis=1)
        f = lambda t_: t_.astype(F32)
        return dq, _rot_t(f(k0) + f(k1) + f(k2), c_, s_), f(v0) + f(v1) + f(v2)

    dq, dk, dv = run(_rows, unrot, dqs + dks + dvs + [cosf, sinf], [], [(N_GROUPS * D, BF16), (D, BF16), (D, BF16)], [],
                     tm=128, name="attn_unrot")
    produce("attn_w_q", run(_mm, x1, dq, mode="tn", outs=[BF16], layout=("col", N_CHIPS), name="d_attn_w_q"))
    dx1_q = run(_mm, dq, W["attn_w_q"], mode="nt", outs=[F32], extras=[dz1_1],
                epilogue=lambda acc, e: (acc + ALPHA * e,), name="dx1_q")
    produce("w_kv", jnp.concatenate(
        [run(_mm, kvn, dk, mode="tn", outs=[BF16], layout=("col", 2), name="d_w_kv_k"),
         run(_mm, kvn, dv, mode="tn", outs=[BF16], layout=("col", 2), name="d_w_kv_v")], axis=1))
    dkvn_k = run(_mm, dk, W["w_kv"], b_sel=0, mode="nt", outs=[F32], name="dkvn_k")
    def x1_bwd(acc, dkvn_k_t, dx1q_t, x1_t, pp, gpre, kg):
        dkvn_t = acc + dkvn_k_t
        n, rstd = _ln_norm(x1_t)
        dy = dx1q_t + _ln_bwd(dkvn_t, n, rstd, kg)
        gt = _sigmoid(gpre)
        return dy, dy * gt, dy * pp * gt * (1.0 - gt), dkvn_t * n, dkvn_t

    dy0, d_pp0, d_gpre0, gv["kv_ln_g"], gv["kv_ln_b"] = run(
        _mm, dv, W["w_kv"], b_sel=1, mode="nt", outs=[F32, BF16, BF16], sums=2, extras=[dkvn_k, dx1_q, x1, pp0, gpre0],
        vecs=[V["kv_ln_g"]], epilogue=x1_bwd, ep_rows=EPILOGUE_ROWS, tm=256, name="dkvn_v")

    dz1_0, dz1b_0, (gv["ln1_g0"], gv["ln1_b0"], gv["ln2_g0"], gv["ln2_b0"]) = _mlp_ple_bwd(
        dy0, d_pp0, d_gpre0, p[0], z1_0, h1b_0, t0, act0, z2_0, h2b_0, wts0, vec0, 0, run, produce)
    produce("conv_w_out", run(_mm, sb, dz1b_0, mode="tn", outs=[BF16], layout=("row", N_CHIPS), name="d_conv_w_out"))
    ds = run(_mm, dz1b_0, W["conv_w_out"], mode="nt", outs=[F32], name="conv_ds")

    def conv_ln_bwd(ds_t, c_t, g_, b_):
        n, rstd = _ln_norm(c_t)
        y = n * g_ + b_
        sg = _sigmoid(y)
        dln = ds_t * sg * (1.0 + y * (1.0 - sg))
        dc = _ln_bwd(dln, n, rstd, g_)
        return dc, dln * n, dln, dc

    dc, gv["conv_ln_g"], gv["conv_ln_b"], gv["conv_dw_b"] = _rows(
        conv_ln_bwd, [ds, cv], [V["conv_ln_g"], V["conv_ln_b"]], [(D, F32)], [D, D, D], tm=256, name="conv_ln_bwd")
    da, dg, gv["conv_dw"], gv["conv_b_a"], gv["conv_b_g"] = run(
        _conv_bwd, dc, glu, a_pre, g_pre, V["conv_dw"], V["conv_b_a"], V["conv_b_g"], name="conv_bwd")
    produce("conv_w_in", jnp.concatenate(
        [run(_mm, x, da, mode="tn", outs=[BF16], layout=("col", 2), name="d_conv_w_in_a"),
         run(_mm, x, dg, mode="tn", outs=[BF16], layout=("col", 2), name="d_conv_w_in_g")], axis=1))
    dx_a = run(_mm, da, W["conv_w_in"], b_sel=0, mode="nt", outs=[F32], extras=[dz1_0],
               epilogue=lambda acc, e: (acc + ALPHA * e,), name="dx_a")
    grad_x = run(_mm, dg, W["conv_w_in"], b_sel=1, mode="nt", outs=[F32], extras=[dx_a],
                 epilogue=lambda acc, e: (acc + e,), name="dx_g")
    if reducer is not None:
        reducer.carry("share_last").run_alone("share_last")
        reducer.carried()
    return loss_cols, grad_x, gw, gv


def _place():
    x, y, c = lax.axis_index("x"), lax.axis_index("y"), lax.axis_index("c")
    chips = [(1 - x, y), (x, 1 - y), (1 - x, 1 - y)]
    return x, y, c, chips


def _remote(src, dst, ssem, rsem, dev):
    return pltpu.make_async_remote_copy(src_ref=src, dst_ref=dst, send_sem=ssem, recv_sem=rsem, device_id=dev,
                                        device_id_type=MESH)


def _allgather8(block, name):
    R, C = block.shape

    def body(x_ref, out_ref, send_sems, recv_sems, local_sem):
        x, y, c, chips = _place()
        me, sibling = (x, y, c), (x, y, 1 - c)

        def slot(px, py, pc):
            return out_ref.at[4 * px + 2 * py + pc]

        def copy(k, blockpos, to, src=None):
            return _remote(slot(*blockpos) if src is None else src, slot(*blockpos), send_sems.at[k], recv_sems.at[k], to)

        mine = pltpu.make_async_copy(x_ref, slot(*me), local_sem)
        mine.start()
        first = [copy(0, me, sibling, src=x_ref)]
        first += [copy(1 + j, me, (*chip, c), src=x_ref) for j, chip in enumerate(chips)]
        for cp in first:
            cp.start()
        passed = [copy(4 + j, (*chip, c), sibling) for j, chip in enumerate(chips)]
        for j, chip in enumerate(chips):
            copy(1 + j, (*chip, c), me).wait_recv()
            passed[j].start()
        copy(0, sibling, me).wait_recv()
        for j, chip in enumerate(chips):
            copy(4 + j, (*chip, 1 - c), me).wait_recv()
        for cp in first + passed:
            cp.wait_send()
        mine.wait()

    return pl.pallas_call(
        body, name=name, out_shape=jax.ShapeDtypeStruct((8, R, C), block.dtype),
        in_specs=[pl.BlockSpec(memory_space=pltpu.VMEM)], out_specs=pl.BlockSpec(memory_space=pltpu.VMEM),
        scratch_shapes=[pltpu.SemaphoreType.DMA((7,)), pltpu.SemaphoreType.DMA((7,)), pltpu.SemaphoreType.DMA],
        compiler_params=pltpu.CompilerParams(vmem_limit_bytes=_vmem(10 * _nbytes((R, C), block.dtype))),
    )(block)


_MATS = (
    ("conv_w_in", "conv_w_in", 0, "col", True),
    ("conv_w_out", "conv_w_out", 0, "row", False),
    ("mlp_up0", "mlp_up", 0, "col", False),
    ("mlp_down0", "mlp_down", 0, "row", False),
    ("ple_proj0", "ple_proj", 0, "col", False),
    ("ple_gate0", "ple_gate", 0, "row", False),
    ("w_kv", "w_kv", None, "col", True),
    ("attn_w_q", "attn_w_q", 0, "col", False),
    ("attn_w_o", "attn_w_o", 0, "row", False),
    ("mlp_up1", "mlp_up", 1, "col", False),
    ("mlp_down1", "mlp_down", 1, "row", False),
    ("ple_proj1", "ple_proj", 1, "col", False),
    ("ple_gate1", "ple_gate", 1, "row", False),
)


class _Carry:
    result = None
    aliases = {}

    def set_result(self, outs):
        self.result = dict(zip(self.names, outs))

    def run_alone(self, name):
        n_in, n_out = len(self.ins), len(self.out_shape)

        def body(*refs):
            in_refs, out_refs, sems = refs[:n_in], refs[n_in:n_in + n_out], refs[n_in + n_out:]
            self.start(in_refs, out_refs, sems)
            self.finish(in_refs, out_refs, sems)

        outs = pl.pallas_call(body, name=name, out_shape=self.out_shape, in_specs=[ANY] * n_in, out_specs=[ANY] * n_out,
                              scratch_shapes=self.scratch, input_output_aliases=dict(self.aliases))(*self.ins)
        self.set_result(outs)
        return self.result


class _Gather(_Carry):
    def __init__(self, names, shards, chip_arr):
        mats = [m for m in _MATS if m[0] in names]
        srcs = sorted({m[1] for m in mats})
        self.names = [m[0] for m in mats]
        self.out_shape, self.geo, placed = [], [], []
        for name, src, layer, kind, split in mats:
            s = shards[src]
            ks, ns = s.shape[-2:]
            K, N = (ks, ns * N_CHIPS) if kind == "col" else (ks * N_CHIPS, ns)
            self.out_shape.append(jax.ShapeDtypeStruct((2, K, N // 2) if split else (K, N), BF16))
            self.geo.append((srcs.index(src), layer if s.ndim == 3 else None, kind, split, K, N))
            placed.append(_place_shard(s, layer if s.ndim == 3 else None, kind, split, chip_arr, f"place_{name}"))
        T = len(mats)
        self.ins = [shards[n] for n in srcs] + placed
        self.aliases = {len(srcs) + t: t for t in range(T)}
        self.scratch = [pltpu.SemaphoreType.DMA((3 * T,)) for _ in range(4)]
        self.result = None

    def _copies(self, in_refs, out_refs, sems):
        geo, T = self.geo, len(self.geo)
        s_ici, r_ici, s_d2d, r_d2d = sems
        x, y, c, chips = _place()
        me = 2 * x + y
        sibling = (x, y, 1 - c)
        idx = [2 * cx + cy for cx, cy in chips]

        def src_ref(t):
            i, layer, _, _, _, _ = geo[t]
            return in_refs[i] if layer is None else in_refs[i].at[layer]

        def src_half(t, h):
            _, _, kind, _, K, N = geo[t]
            if kind == "col":
                return src_ref(t).at[pl.ds(h * (K // 2), K // 2), :]
            return src_ref(t).at[:, pl.ds(h * (N // 2), N // 2)]

        def dst(t, j, h):
            _, _, kind, split, K, N = geo[t]
            n, k = N // N_CHIPS, K // N_CHIPS
            if kind == "col":
                rows = slice(None) if h is None else pl.ds(h * (K // 2), K // 2)
                if split:
                    return out_refs[t].at[j // 2, rows, pl.ds((j % 2) * n, n)]
                return out_refs[t].at[rows, pl.ds(j * n, n)]
            cols = slice(None) if h is None else pl.ds(h * (N // 2), N // 2)
            return out_refs[t].at[pl.ds(j * k, k), cols]

        sends = [_remote(src_half(t, c), dst(t, me, c), s_ici.at[3 * t + kk], r_ici.at[3 * t + kk], (*chips[kk], c))
                 for t in range(T) for kk in range(3)]
        hops = []
        for t in range(T):
            for kk in range(3):
                mine, theirs = dst(t, idx[kk], c), dst(t, idx[kk], 1 - c)
                hops.append((_remote(mine, mine, s_ici.at[3 * t + kk], r_ici.at[3 * t + kk], sibling),
                             _remote(mine, mine, s_d2d.at[3 * t + kk], r_d2d.at[3 * t + kk], sibling),
                             _remote(theirs, theirs, s_d2d.at[3 * t + kk], r_d2d.at[3 * t + kk], sibling)))
        return sends, hops

    def start(self, in_refs, out_refs, sems):
        for cp in self._copies(in_refs, out_refs, sems)[0]:
            cp.start()

    def finish(self, in_refs, out_refs, sems):
        sends, hops = self._copies(in_refs, out_refs, sems)
        for landed, forward, _ in hops:
            landed.wait_recv()
            forward.start()
        for _, _, from_sibling in hops:
            from_sibling.wait_recv()
        for cp in sends + [h[1] for h in hops]:
            cp.wait_send()


def _place_shard(shard, layer, kind, split, chip_arr, name):
    ks, ns = shard.shape[-2:]
    K, N = (ks, ns * N_CHIPS) if kind == "col" else (ks * N_CHIPS, ns)
    tr = _fit(256, ks)
    nb = ks // tr
    if shard.ndim == 3:
        in_spec = pl.BlockSpec((None, tr, ns), lambda i, me: (layer, i, 0))
    else:
        in_spec = pl.BlockSpec((tr, ns), lambda i, me: (i, 0))
    if kind == "row":
        out_shape, out_spec = (K, N), pl.BlockSpec((tr, ns), lambda i, me: (me[0] * nb + i, 0))
    elif split:
        out_shape, out_spec = (2, K, N // 2), pl.BlockSpec((None, tr, ns), lambda i, me: (me[0] // 2, i, me[0] % 2))
    else:
        out_shape, out_spec = (K, N), pl.BlockSpec((tr, ns), lambda i, me: (i, me[0]))

    def body(me_ref, s_ref, o_ref):
        o_ref[...] = s_ref[...]

    return pl.pallas_call(
        body, name=name, out_shape=jax.ShapeDtypeStruct(out_shape, BF16),
        grid_spec=pltpu.PrefetchScalarGridSpec(num_scalar_prefetch=1, grid=(nb,), in_specs=[in_spec], out_specs=out_spec),
        compiler_params=pltpu.CompilerParams(dimension_semantics=("parallel",), vmem_limit_bytes=_vmem(4 * tr * ns * 2)),
    )(chip_arr, shard)


class _Multi(_Carry):
    def __init__(self, parts):
        self.parts = parts
        self.ins = [a for p in parts for a in p.ins]
        self.out_shape = [a for p in parts for a in p.out_shape]
        self.scratch = [a for p in parts for a in p.scratch]
        self.aliases, n_in, n_out = {}, 0, 0
        for p in parts:
            self.aliases.update({n_in + i: n_out + o for i, o in p.aliases.items()})
            n_in, n_out = n_in + len(p.ins), n_out + len(p.out_shape)

    def _split(self, seq, field):
        out, at = [], 0
        for p in self.parts:
            n = len(getattr(p, field))
            out.append(seq[at:at + n])
            at += n
        return out

    def _each(self, method, in_refs, out_refs, sems):
        for p, i, o, s in zip(self.parts, self._split(in_refs, "ins"), self._split(out_refs, "out_shape"),
                              self._split(sems, "scratch")):
            getattr(p, method)(i, o, s)

    def start(self, in_refs, out_refs, sems):
        self._each("start", in_refs, out_refs, sems)

    def finish(self, in_refs, out_refs, sems):
        self._each("finish", in_refs, out_refs, sems)

    def set_result(self, outs):
        for p, o in zip(self.parts, self._split(list(outs), "out_shape")):
            p.set_result(o)


class _PairSend(_Carry):
    def __init__(self, grads):
        self.names = list(grads)
        self.ins = [grads[n] for n in self.names]
        self.out_shape = [jax.ShapeDtypeStruct(a.shape[1:], BF16) for a in self.ins]
        T = len(self.names)
        self.scratch = [pltpu.SemaphoreType.DMA((T,)), pltpu.SemaphoreType.DMA((T,))]

    def _copies(self, in_refs, out_refs, sems):
        x, y, c, _ = _place()
        return [_remote(in_refs[t].at[1 - c], out_refs[t], sems[0].at[t], sems[1].at[t], (x, y, 1 - c))
                for t in range(len(self.names))]

    def start(self, in_refs, out_refs, sems):
        for cp in self._copies(in_refs, out_refs, sems):
            cp.start()

    def finish(self, in_refs, out_refs, sems):
        for cp in self._copies(in_refs, out_refs, sems):
            cp.wait()


class _ChipScatter(_Carry):
    def __init__(self, sums):
        self.names = list(sums)
        T = len(self.names)
        self.ins = [sums[n][0] for n in self.names] + [sums[n][1] for n in self.names]
        self.out_shape = [jax.ShapeDtypeStruct(a.shape, BF16) for a in self.ins[:T]]
        self.aliases = {T + t: t for t in range(T)}
        self.scratch = [pltpu.SemaphoreType.DMA((3 * T,)), pltpu.SemaphoreType.DMA((3 * T,))]

    def _copies(self, in_refs, out_refs, sems):
        ssem, rsem = sems
        x, y, c, chips = _place()
        me = 2 * x + y
        idx = [2 * cx + cy for cx, cy in chips]
        T = len(self.names)
        sends = [_remote(in_refs[t].at[idx[kk]], out_refs[t].at[me], ssem.at[3 * t + kk], rsem.at[3 * t + kk],
                         (*chips[kk], c)) for t in range(T) for kk in range(3)]
        lands = [_remote(out_refs[t].at[idx[kk]], out_refs[t].at[idx[kk]], ssem.at[3 * t + kk], rsem.at[3 * t + kk],
                         (*chips[kk], c)) for t in range(T) for kk in range(3)]
        return sends, lands

    def start(self, in_refs, out_refs, sems):
        for cp in self._copies(in_refs, out_refs, sems)[0]:
            cp.start()

    def finish(self, in_refs, out_refs, sems):
        sends, lands = self._copies(in_refs, out_refs, sems)
        for cp in lands:
            cp.wait_recv()
        for cp in sends:
            cp.wait_send()


class _PairShare(_Carry):
    def __init__(self, halves):
        self.names = list(halves)
        self.ins = [halves[n] for n in self.names]
        self.out_shape = [jax.ShapeDtypeStruct(a.shape, F32) for a in self.ins]
        T = len(self.names)
        self.aliases = {t: t for t in range(T)}
        self.scratch = [pltpu.SemaphoreType.DMA((T,)), pltpu.SemaphoreType.DMA((T,))]

    def _copies(self, in_refs, out_refs, sems):
        ssem, rsem = sems
        x, y, c, _ = _place()
        sibling = (x, y, 1 - c)
        T = len(self.names)
        sends = [_remote(out_refs[t].at[c], out_refs[t].at[c], ssem.at[t], rsem.at[t], sibling) for t in range(T)]
        lands = [_remote(out_refs[t].at[1 - c], out_refs[t].at[1 - c], ssem.at[t], rsem.at[t], sibling) for t in range(T)]
        return sends, lands

    def start(self, in_refs, out_refs, sems):
        for cp in self._copies(in_refs, out_refs, sems)[0]:
            cp.start()

    def finish(self, in_refs, out_refs, sems):
        sends, lands = self._copies(in_refs, out_refs, sems)
        for cp in lands:
            cp.wait_recv()
        for cp in sends:
            cp.wait_send()


def _pair_sum(own, landed, c_arr, name):
    _, ns, r, cc = own.shape
    rows = ns * r
    tr = _fit(512, rows)

    def body(c_ref, a_ref, b_ref, o_ref, o2_ref):
        total = (a_ref[...].astype(F32) + b_ref[...].astype(F32)).astype(o_ref.dtype)
        o_ref[...] = total
        o2_ref[...] = total

    tile = pl.BlockSpec((tr, cc), lambda i, c_ref: (i, 0))
    out = pl.pallas_call(
        body, name=name, out_shape=[jax.ShapeDtypeStruct((rows, cc), BF16)] * 2,
        grid_spec=pltpu.PrefetchScalarGridSpec(
            num_scalar_prefetch=1, grid=(rows // tr,),
            in_specs=[pl.BlockSpec((None, tr, cc), lambda i, c_ref: (c_ref[0], i, 0)), tile], out_specs=[tile, tile]),
        compiler_params=pltpu.CompilerParams(dimension_semantics=("parallel",), vmem_limit_bytes=_vmem(8 * tr * cc * 4)),
    )(c_arr, own.reshape(2, rows, cc), landed.reshape(rows, cc))
    return out[0].reshape(ns, r, cc), out[1].reshape(ns, r, cc)


def _chip_sum(parts, c_arr, name):
    _, r, cc = parts.shape
    tr = _fit(256, r)

    def body(c_ref, p_ref, o_ref):
        acc = p_ref[0].astype(F32)
        for j in range(1, N_CHIPS):
            acc = acc + p_ref[j].astype(F32)
        o_ref[...] = acc

    return pl.pallas_call(
        body, name=name, out_shape=jax.ShapeDtypeStruct((2, r, cc), F32),
        grid_spec=pltpu.PrefetchScalarGridSpec(
            num_scalar_prefetch=1, grid=(r // tr,),
            in_specs=[pl.BlockSpec((N_CHIPS, tr, cc), lambda i, c_ref: (0, i, 0))],
            out_specs=pl.BlockSpec((None, tr, cc), lambda i, c_ref: (c_ref[0], i, 0))),
        compiler_params=pltpu.CompilerParams(dimension_semantics=("parallel",), vmem_limit_bytes=_vmem(12 * tr * cc * 4)),
    )(c_arr, parts)


def _adamw_math(w, g, m, v):
    m2 = ADAM_B1 * m + (1.0 - ADAM_B1) * g
    v2 = ADAM_B2 * v + (1.0 - ADAM_B2) * jnp.square(g)
    m_hat = m2 / (1.0 - ADAM_B1 ** ADAM_STEP)
    v_hat = v2 / (1.0 - ADAM_B2 ** ADAM_STEP)
    delta = -ADAM_LR * (m_hat / (jnp.sqrt(v_hat) + ADAM_EPS) + ADAM_WD * w)
    return delta, m2, v2


def _adamw_mat(g2, w, m, v, layer, kind, prev, name):
    shape = w.shape
    ks, ns = shape[-2:]
    _, r, cc = g2.shape
    tr, tc = _fit(256, r), _fit(1024, cc)
    assert (r, cc) == ((ks // 2, ns) if kind == "col" else (ks, ns // 2))
    assert r % tr == 0 and cc % tc == 0
    rb, cb = r // tr, cc // tc
    if kind == "col":
        g_spec = pl.BlockSpec((None, tr, tc), lambda i, j: (i // rb, i % rb, j))
    else:
        g_spec = pl.BlockSpec((None, tr, tc), lambda i, j: (j // cb, i, j % cb))
    if w.ndim == 3:
        w_spec = pl.BlockSpec((None, tr, tc), lambda i, j: (layer, i, j))
    else:
        w_spec = pl.BlockSpec((tr, tc), lambda i, j: (i, j))
    n_prev = 0 if prev is None else 4

    def body(*refs):
        g_ref, w_ref, m_ref, v_ref = refs[:4]
        go_ref, d_ref, mo_ref, vo_ref = refs[4 + n_prev:]
        g = g_ref[...]
        delta, m2, v2 = _adamw_math(w_ref[...], g, m_ref[...], v_ref[...])
        go_ref[...] = g
        d_ref[...] = delta
        mo_ref[...] = m2
        vo_ref[...] = v2

    return pl.pallas_call(
        body, name=name, grid=(ks // tr, ns // tc),
        in_specs=[g_spec, w_spec, w_spec, w_spec] + [ANY] * n_prev, out_specs=[w_spec] * 4,
        out_shape=[jax.ShapeDtypeStruct(shape, F32)] * 4,
        input_output_aliases={4 + i: i for i in range(n_prev)},
        compiler_params=pltpu.CompilerParams(dimension_semantics=("parallel", "parallel"),
                                             vmem_limit_bytes=_vmem(16 * tr * tc * 4)),
    )(g2, w, m, v, *(prev or ()))


def _adamw_small(g, w, m, v, name):
    def body(g_ref, w_ref, m_ref, v_ref, d_ref, mo_ref, vo_ref):
        delta, m2, v2 = _adamw_math(w_ref[...], g_ref[...], m_ref[...], v_ref[...])
        d_ref[...] = delta
        mo_ref[...] = m2
        vo_ref[...] = v2

    return pl.pallas_call(body, name=name, out_shape=[jax.ShapeDtypeStruct(w.shape, F32)] * 3)(g, w, m, v)


def _sum8(parts, name):
    def body(p_ref, o_ref):
        acc = p_ref[0]
        for j in range(1, 8):
            acc = acc + p_ref[j]
        o_ref[...] = acc

    return pl.pallas_call(body, name=name, out_shape=jax.ShapeDtypeStruct(parts.shape[1:], F32),
                          compiler_params=pltpu.CompilerParams(vmem_limit_bytes=_vmem(12 * _nbytes(parts.shape[1:], F32))))(parts)


_REDUCE_AT = {
    "d_ple_gate1": (("A", "ple_proj1"),),
    "dh2_1": (("A", "ple_gate1"),),
    "d_mlp_down1": (("B", "ple_proj1"), ("B", "ple_gate1")),
    "dt1": (("A", "mlp_down1"),),
    "d_mlp_up1": (("B", "mlp_down1"), ("C", "ple_proj1"), ("C", "ple_gate1")),
    "dh1_1": (("A", "mlp_up1"),),
    "d_attn_w_o": (("C", "mlp_down1"),),
    "attn_do": (("A", "attn_w_o"),),
    "attn_bwd_g0": (("B", "attn_w_o"),),
    "attn_unrot": (("C", "attn_w_o"),),
    "dx1_q": (("A", "attn_w_q"),),
    "dkvn_k": (("A", "w_kv"),),
    "d_ple_gate0": (("A", "ple_proj0"),),
    "dh2_0": (("A", "ple_gate0"),),
    "d_mlp_down0": (("B", "mlp_up1"), ("B", "ple_proj0")),
    "dt0": (("B", "w_kv"), ("B", "ple_gate0"), ("A", "mlp_down0")),
    "d_mlp_up0": (("B", "mlp_down0"), ("C", "mlp_up1"), ("C", "ple_proj0"), ("C", "w_kv"), ("C", "ple_gate0")),
    "dh1_0": (("A", "mlp_up0"), ("B", "attn_w_q")),
    "d_conv_w_out": (("C", "mlp_down0"), ("C", "attn_w_q")),
    "conv_ds": (("A", "conv_w_out"),),
    "conv_bwd": (("B", "mlp_up0"), ("B", "conv_w_out")),
    "d_conv_w_in_g": (("C", "mlp_up0"), ("C", "conv_w_out")),
    "dx_a": (("A", "conv_w_in"),),
    "dx_g": (("B", "conv_w_in"),),
    "share_last": (("C", "conv_w_in"),),
}


class _Reducer:
    def __init__(self, w, mom, var, c_arr):
        self.w, self.mom, self.var, self.c_arr = w, mom, var, c_arr
        self.mats = {m[0]: m for m in _MATS}
        self.grads, self.pair_sums, self.chip_sums, self.out = {}, {}, {}, {}

    def produced(self, name, grad):
        self.grads[name] = grad

    def carry(self, call):
        parts = []
        for cls, stage, src in ((_PairSend, "A", self.grads), (_ChipScatter, "B", self.pair_sums),
                                (_PairShare, "C", self.chip_sums)):
            names = [n for s, n in _REDUCE_AT.get(call, ()) if s == stage]
            if names:
                parts.append((stage, cls({n: src[n] for n in names})))
        self._parts = parts
        return _Multi([p for _, p in parts]) if parts else None

    def carried(self):
        for stage, part in self._parts:
            for name, val in part.result.items():
                if stage == "A":
                    self.pair_sums[name] = _pair_sum(self.grads[name], val, self.c_arr, f"pair_sum_{name}")
                elif stage == "B":
                    self.chip_sums[name] = _chip_sum(val, self.c_arr, f"chip_sum_{name}")
                else:
                    _, src, layer, kind, _ = self.mats[name]
                    self.out[src] = _adamw_mat(val, self.w[src], self.mom[src], self.var[src], layer or 0, kind,
                                               self.out.get(src), f"adamw_{name}")
        self._parts = []


_WEIGHTS = ("conv_w_in", "conv_b_in", "conv_dw", "conv_dw_b", "conv_ln_g", "conv_ln_b", "conv_w_out", "kv_ln_g",
            "kv_ln_b", "w_kv", "attn_w_q", "attn_w_o", "ln1_g", "ln1_b", "mlp_up", "mlp_down", "ln2_g", "ln2_b",
            "ple_proj", "ple_gate")
_SHARDED_VECS = ("conv_b_in", "conv_dw", "conv_dw_b", "conv_ln_g", "conv_ln_b")
_REPLICATED_VECS = ("kv_ln_g", "kv_ln_b", "ln1_g", "ln1_b", "ln2_g", "ln2_b")


def _pad_rows(a, rows):
    return jnp.concatenate([a, jnp.zeros((rows - a.shape[0], a.shape[1]), a.dtype)], axis=0) if a.shape[0] < rows else a


def _pack_sharded(d):
    n = d["conv_dw_b"].shape[-1]
    rows = [d["conv_b_in"].reshape(2, n), d["conv_dw"].reshape(CONV_WIDTH, n), d["conv_dw_b"].reshape(1, n),
            d["conv_ln_g"].reshape(1, n), d["conv_ln_b"].reshape(1, n)]
    return _pad_rows(jnp.concatenate(rows, axis=0), 40)


def _unpack_sharded(pack, like):
    n = pack.shape[1]
    return {"conv_b_in": pack[0:2].reshape(like["conv_b_in"].shape),
            "conv_dw": pack[2:2 + CONV_WIDTH].reshape(like["conv_dw"].shape),
            "conv_dw_b": pack[33:34].reshape(like["conv_dw_b"].shape),
            "conv_ln_g": pack[34:35].reshape(like["conv_ln_g"].shape),
            "conv_ln_b": pack[35:36].reshape(like["conv_ln_b"].shape)}


def _pack_replicated(d):
    D = d["kv_ln_g"].shape[-1]
    rows = [d[n].reshape(-1, D) for n in _REPLICATED_VECS]
    return _pad_rows(jnp.concatenate(rows, axis=0), 16)


def _unpack_replicated(pack, like):
    out, r = {}, 0
    for n in _REPLICATED_VECS:
        k = like[n].size // pack.shape[1]
        out[n] = pack[r:r + k].reshape(like[n].shape)
        r += k
    return out


def kernel(x, p, positions, conv_w_in, conv_b_in, conv_dw, conv_dw_b, conv_ln_g, conv_ln_b, conv_w_out, kv_ln_g, kv_ln_b, w_kv, attn_w_q, attn_w_o, ln1_g, ln1_b, mlp_up, mlp_down, ln2_g, ln2_b, ple_proj, ple_gate, loss_target, m_conv_w_in, m_conv_b_in, m_conv_dw, m_conv_dw_b, m_conv_ln_g, m_conv_ln_b, m_conv_w_out, m_kv_ln_g, m_kv_ln_b, m_w_kv, m_attn_w_q, m_attn_w_o, m_ln1_g, m_ln1_b, m_mlp_up, m_mlp_down, m_ln2_g, m_ln2_b, m_ple_proj, m_ple_gate, v_conv_w_in, v_conv_b_in, v_conv_dw, v_conv_dw_b, v_conv_ln_g, v_conv_ln_b, v_conv_w_out, v_kv_ln_g, v_kv_ln_b, v_w_kv, v_attn_w_q, v_attn_w_o, v_ln1_g, v_ln1_b, v_mlp_up, v_mlp_down, v_ln2_g, v_ln2_b, v_ple_proj, v_ple_gate):
    args = dict(locals())
    w = {n: args[n] for n in _WEIGHTS}
    mom = {n: args["m_" + n] for n in _WEIGHTS}
    var = {n: args["v_" + n] for n in _WEIGHTS}
    S, D = x.shape[1:]
    n4 = D // N_CHIPS
    chip = 2 * lax.axis_index("x") + lax.axis_index("y")
    c_arr = lax.axis_index("c").astype(jnp.int32).reshape(1)

    shards = {n: w[n].astype(BF16) for n in sorted({m[1] for m in _MATS})}
    vec_all = _allgather8(_pack_sharded(w), "gather_vectors")
    vec_full = jnp.concatenate([vec_all[2 * j] for j in range(N_CHIPS)], axis=1)
    b_in = vec_all[0::2, 0:2, :].reshape(1, 2 * D)
    V = {"conv_b_a": b_in[:, :D], "conv_b_g": b_in[:, D:],
         "conv_dw": _pad_rows(vec_full[2:2 + CONV_WIDTH], CONV_PAD), "conv_dw_b": vec_full[33:34],
         "conv_ln_g": vec_full[34:35], "conv_ln_b": vec_full[35:36],
         "kv_ln_g": kv_ln_g.reshape(1, D), "kv_ln_b": kv_ln_b.reshape(1, D)}
    for l in range(2):
        for n in ("ln1_g", "ln1_b", "ln2_g", "ln2_b"):
            V[f"{n}{l}"] = w[n][l].reshape(1, D)

    half = HEAD_DIM // 2
    inv_freq = ROPE_THETA ** (-jnp.arange(half, dtype=F32) * (2.0 / HEAD_DIM))
    ang = positions[0].astype(F32)[:, None] * inv_freq
    cos, sin = jnp.cos(ang), jnp.sin(ang)
    cosf = jnp.concatenate([cos, cos], axis=-1)
    sinf = jnp.concatenate([-sin, sin], axis=-1)

    reducer = _Reducer(w, mom, var, c_arr)
    loss_cols, grad_x, _, gv = _local_step(x[0], p[:, 0], cosf, sinf, loss_target[0], None, V, shards, reducer,
                                           chip.astype(jnp.int32).reshape(1))
    loss = lax.psum(jnp.sum(loss_cols), ("x", "y", "c"))
    out = dict(reducer.out)

    gpack = jnp.concatenate([gv["conv_b_a"], gv["conv_b_g"], gv["conv_dw"][:CONV_WIDTH], gv["conv_dw_b"],
                             gv["conv_ln_g"], gv["conv_ln_b"], gv["kv_ln_g"], gv["kv_ln_b"],
                             gv["ln1_g0"], gv["ln1_g1"], gv["ln1_b0"], gv["ln1_b1"],
                             gv["ln2_g0"], gv["ln2_g1"], gv["ln2_b0"], gv["ln2_b1"]], axis=0)
    gsum = _sum8(_allgather8(_pad_rows(gpack, 48), "gather_vector_grads"), "sum_vector_grads")
    g_b = lax.dynamic_slice_in_dim(jnp.concatenate([gsum[0:1], gsum[1:2]], axis=1), chip * 2 * n4, 2 * n4, axis=1)
    g_sh = lax.dynamic_slice_in_dim(gsum[2:36], chip * n4, n4, axis=1)
    g_sh = _pad_rows(jnp.concatenate([g_b.reshape(2, n4), g_sh], axis=0), 40)
    d_sh, m_sh, v_sh = _adamw_small(g_sh, _pack_sharded(w), _pack_sharded(mom), _pack_sharded(var), "adamw_sharded_vectors")
    g_rep = _pad_rows(gsum[36:46], 16)
    d_rep, m_rep, v_rep = _adamw_small(g_rep, _pack_replicated(w), _pack_replicated(mom), _pack_replicated(var),
                                       "adamw_replicated_vectors")
    small = {}
    for i, (sh, rep) in enumerate(((g_sh, g_rep), (d_sh, d_rep), (m_sh, m_rep), (v_sh, v_rep))):
        d = {**_unpack_sharded(sh, w), **_unpack_replicated(rep, w)}
        for n, val in d.items():
            small.setdefault(n, [None] * 4)[i] = val
    for n in small:
        out[n] = small[n]

    res = [loss, grad_x[None]]
    for i in range(4):
        res += [out[n][i] for n in _WEIGHTS]
    return tuple(res)
```

```python
import functools

import jax
import jax.numpy as jnp
from jax import lax
from jax.experimental import pallas as pl
from jax.experimental.pallas import tpu as pltpu

F32 = jnp.float32
BF16 = jnp.bfloat16

HEAD_DIM = 128
ATTN_BLOCK = 128
DILATIONS = (1, 4, 16)
N_GROUPS = 3
CONV_WIDTH = 31
CONV_PAD = 32
CONV_ROWS = 32
EPILOGUE_ROWS = 128
ROPE_THETA = 10000.0
LN_EPS = 1e-5
ALPHA = 4.0 ** 0.25
ATTN_SCALE = HEAD_DIM ** -0.5
NEG = -1e30

ADAM_LR = 0.001
ADAM_B1 = 0.9
ADAM_B2 = 0.999
ADAM_EPS = 1e-08
ADAM_WD = 0.01
ADAM_STEP = 10

N_CHIPS = 4
VMEM_CAP = 60 << 20
MESH = pl.DeviceIdType.MESH
ANY = pl.BlockSpec(memory_space=pl.ANY)


def _vmem(nbytes):
    return int(min(max(2 * nbytes + (8 << 20), 24 << 20), VMEM_CAP))


def _fit(tile, n):
    if n <= tile:
        return n
    t = tile - tile % 128
    while n % t:
        t -= 128
    return t


def _nbytes(shape, dtype):
    n = 1
    for s in shape:
        n *= s
    return n * jnp.dtype(dtype).itemsize


_DIMS = {"nn": (((1,), (0,)), ((), ())), "nt": (((1,), (1,)), ((), ())), "tn": (((0,), (0,)), ((), ()))}


def _pcall(body, *, name, grid, in_specs, out_specs, out_shape, operands, scratch_shapes=(), vmem, carry=None):
    if carry is None:
        return pl.pallas_call(
            body, name=name, grid=grid, in_specs=in_specs, out_specs=out_specs, out_shape=out_shape,
            scratch_shapes=list(scratch_shapes),
            compiler_params=pltpu.CompilerParams(dimension_semantics=("arbitrary",) * len(grid), vmem_limit_bytes=vmem),
        )(*operands)
    n_in, n_out, n_scr = len(in_specs), len(out_specs), len(scratch_shapes)
    c_in, c_out = len(carry.ins), len(carry.out_shape)

    def wrapped(*refs):
        ins, refs = refs[:n_in], refs[n_in:]
        c_ins, refs = refs[:c_in], refs[c_in:]
        outs, refs = refs[:n_out], refs[n_out:]
        c_outs, refs = refs[:c_out], refs[c_out:]
        scr, c_sems = refs[:n_scr], refs[n_scr:]
        first = functools.reduce(jnp.logical_and, [pl.program_id(d) == 0 for d in range(len(grid))])
        last = functools.reduce(jnp.logical_and, [pl.program_id(d) == grid[d] - 1 for d in range(len(grid))])
        pl.when(first)(lambda: carry.start(c_ins, c_outs, c_sems))
        body(*ins, *outs, *scr)
        pl.when(last)(lambda: carry.finish(c_ins, c_outs, c_sems))

    res = pl.pallas_call(
        wrapped, name=name, grid=grid, in_specs=list(in_specs) + [ANY] * c_in, out_specs=list(out_specs) + [ANY] * c_out,
        out_shape=list(out_shape) + list(carry.out_shape), scratch_shapes=list(scratch_shapes) + list(carry.scratch),
        input_output_aliases={len(operands) + i: n_out + o for i, o in carry.aliases.items()},
        compiler_params=pltpu.CompilerParams(dimension_semantics=("arbitrary",) * len(grid), vmem_limit_bytes=vmem),
    )(*operands, *carry.ins)
    carry.set_result(res[n_out:])
    return res[:n_out]


def _mm(a, b, *, mode, outs, name, epilogue=None, extras=(), rextras=(), vecs=(), a_sel=None, b_sel=None,
        tm=None, tn=2048, tk=None, layout=None, carry=None, ep_rows=None, a_fn=None, sums=0):
    a2, b2 = a.shape[-2:], b.shape[-2:]
    if mode == "nn":
        (M, K), (K2, N) = a2, b2
    elif mode == "nt":
        (M, K), (N, K2) = a2, b2
    else:
        (K, M), (K2, N) = a2, b2
    assert K == K2, (a.shape, b.shape, mode)
    if tm is None:
        tm = 1024 if mode == "tn" else 512
    if tk is None:
        tk = 2048
    if layout is not None:
        kind, nslots = layout
        r, c = (M // 2, N // nslots) if kind == "col" else (M // nslots, N // 2)
        tm, tn = _fit(tm, r), _fit(tn, c)
        assert r % tm == 0 and c % tn == 0
    else:
        tm, tn = _fit(tm, M), _fit(tn, N)
    tk = _fit(tk, K)
    assert M % tm == 0 and N % tn == 0 and K % tk == 0, (M, N, K, tm, tn, tk)
    nk = K // tk
    grid = (N // tn, M // tm, nk)

    def spec(arr, sel, blk, imap):
        if arr.ndim == 3:
            return pl.BlockSpec((None,) + blk, lambda j, i, k: (sel,) + imap(j, i, k))
        return pl.BlockSpec(blk, imap)

    if mode == "tn":
        a_spec = spec(a, a_sel, (tk, tm), lambda j, i, k: (k, i))
    else:
        a_spec = spec(a, a_sel, (tm, tk), lambda j, i, k: (i, k))
    if mode == "nt":
        b_spec = spec(b, b_sel, (tn, tk), lambda j, i, k: (j, k))
    else:
        b_spec = spec(b, b_sel, (tk, tn), lambda j, i, k: (k, j))
    in_specs = [a_spec, b_spec]
    in_specs += [pl.BlockSpec((tm, tn), lambda j, i, k: (i, j)) for _ in extras]
    in_specs += [pl.BlockSpec((tm, e.shape[1]), lambda j, i, k: (i, 0)) for e in rextras]
    in_specs += [pl.BlockSpec((1, tn), lambda j, i, k: (0, j)) for _ in vecs]

    if layout is None:
        out_shape = [jax.ShapeDtypeStruct((M, N), d) for d in outs] + [jax.ShapeDtypeStruct((1, N), F32)] * sums
        out_specs = [pl.BlockSpec((tm, tn), lambda j, i, k: (i, j)) for _ in outs]
        out_specs += [pl.BlockSpec((1, tn), lambda j, i, k: (0, j))] * sums
    else:
        assert len(outs) == 1
        out_shape = [jax.ShapeDtypeStruct((2, nslots, r, c), outs[0])]
        rb, cb = r // tm, c // tn
        if kind == "col":
            omap = lambda j, i, k: (i // rb, j // cb, i % rb, j % cb)
        else:
            omap = lambda j, i, k: (j // cb, i // rb, i % rb, j % cb)
        out_specs = [pl.BlockSpec((None, None, tm, tn), omap)]

    ne, nr, nv, no = len(extras), len(rextras), len(vecs), len(outs)
    dims = _DIMS[mode]

    def body(*refs):
        a_ref, b_ref = refs[0], refs[1]
        rest = refs[2:2 + ne + nr + nv]
        o_refs = refs[2 + ne + nr + nv:2 + ne + nr + nv + no]
        s_refs = refs[2 + ne + nr + nv + no:2 + ne + nr + nv + no + sums]

        def finish(total):
            if epilogue is None:
                for o in o_refs:
                    o[...] = total.astype(o.dtype)
                return
            step = min(ep_rows or tm, tm)
            col_sums = [None] * sums
            for r0 in range(0, tm, step):
                rows = slice(r0, r0 + step)
                tiles = [x[rows, :] for x in rest[:ne + nr]] + [x[...] for x in rest[ne + nr:]]
                res = epilogue(total[rows, :], *tiles)
                for o, val in zip(o_refs, res[:no]):
                    o[rows, :] = val.astype(o.dtype)
                for n_, val in enumerate(res[no:]):
                    part_sum = jnp.sum(val, axis=0, keepdims=True)
                    col_sums[n_] = part_sum if col_sums[n_] is None else col_sums[n_] + part_sum
            if sums:
                @pl.when(pl.program_id(1) == 0)
                def _():
                    for s_ref, val in zip(s_refs, col_sums):
                        s_ref[...] = val

                @pl.when(pl.program_id(1) > 0)
                def _():
                    for s_ref, val in zip(s_refs, col_sums):
                        s_ref[...] += val

        def product():
            a_tile = a_ref[...] if a_fn is None else a_fn(a_ref[...])
            return lax.dot_general(a_tile.astype(BF16), b_ref[...].astype(BF16), dims, preferred_element_type=F32)

        if nk == 1:
            finish(product())
            return
        acc = refs[-1]
        k = pl.program_id(2)

        @pl.when(k == 0)
        def _():
            acc[...] = product()

        @pl.when(k > 0)
        def _():
            acc[...] += product()

        @pl.when(k == nk - 1)
        def _():
            finish(acc[...])

    blk = (_nbytes((tm, tk), a.dtype) + _nbytes((tk, tn), b.dtype) + sum(_nbytes((tm, tn), e.dtype) for e in extras)
           + sum(_nbytes((tm, tn), d) for d in outs) + 2 * tm * tn * 4)
    res = _pcall(body, name=name, grid=grid, in_specs=in_specs, out_specs=out_specs, out_shape=out_shape,
                 operands=(a, b, *extras, *rextras, *vecs),
                 scratch_shapes=[pltpu.VMEM((tm, tn), F32)] if nk > 1 else [], vmem=_vmem(blk), carry=carry)
    return res[0] if no + sums == 1 else tuple(res)


def _rows(fn, rows, vecs, outs, sums, *, tm, name, carry=None):
    S = rows[0].shape[0]
    tm = min(tm, S)
    assert S % tm == 0
    nr, nv, no, ns = len(rows), len(vecs), len(outs), len(sums)

    def body(*refs):
        vals = fn(*[r[...] for r in refs[:nr + nv]])
        o_refs = refs[nr + nv:nr + nv + no]
        s_refs = refs[nr + nv + no:]
        for o, val in zip(o_refs, vals[:no]):
            o[...] = val.astype(o.dtype)
        if ns:
            @pl.when(pl.program_id(0) == 0)
            def _():
                for s in s_refs:
                    s[...] = jnp.zeros_like(s)

            for s, val in zip(s_refs, vals[no:]):
                s[...] += jnp.sum(val.astype(F32), axis=0, keepdims=True)

    in_specs = [pl.BlockSpec((tm, r.shape[1]), lambda i: (i, 0)) for r in rows]
    in_specs += [pl.BlockSpec(v.shape, lambda i: (0, 0)) for v in vecs]
    out_specs = [pl.BlockSpec((tm, c), lambda i: (i, 0)) for c, _ in outs]
    out_specs += [pl.BlockSpec((1, c), lambda i: (0, 0)) for c in sums]
    out_shape = [jax.ShapeDtypeStruct((S, c), d) for c, d in outs]
    out_shape += [jax.ShapeDtypeStruct((1, c), F32) for c in sums]
    blk = sum(_nbytes((tm, r.shape[1]), r.dtype) for r in rows) + sum(_nbytes((tm, c), d) for c, d in outs)
    blk += 6 * tm * max(r.shape[1] for r in rows) * 4
    res = _pcall(body, name=name, grid=(S // tm,), in_specs=in_specs, out_specs=out_specs, out_shape=out_shape,
                 operands=(*rows, *vecs), vmem=_vmem(blk), carry=carry)
    return tuple(res)


def _ln_norm(z):
    mu = jnp.mean(z, axis=-1, keepdims=True)
    d = z - mu
    var = jnp.mean(d * d, axis=-1, keepdims=True)
    rstd = lax.rsqrt(var + LN_EPS)
    return d * rstd, rstd


def _ln(z, g, b):
    return _ln_norm(z)[0] * g + b


def _ln_bwd(dy, n, rstd, g):
    dn = dy * g
    return rstd * (dn - jnp.mean(dn, axis=-1, keepdims=True) - n * jnp.mean(dn * n, axis=-1, keepdims=True))


def _sq_relu(t):
    return jnp.square(jnp.maximum(t.astype(F32), 0.0))


def _sigmoid(x):
    return 1.0 / (1.0 + jnp.exp(-x))


def _per_head(x, fn):
    h = x.shape[1] // HEAD_DIM
    return jnp.concatenate([fn(x[:, i * HEAD_DIM:(i + 1) * HEAD_DIM], i) for i in range(h)], axis=1)


def _rot(x, cosf, sinf):
    return _per_head(x, lambda xh, i: xh * cosf + pltpu.roll(xh, HEAD_DIM // 2, 1) * sinf)


def _rot_t(dy, cosf, sinf):
    return _per_head(dy, lambda dh, i: dh * cosf + pltpu.roll(dh * sinf, HEAD_DIM // 2, 1))


def _shift_copies(win):
    rows = win.shape[1] - 8
    for s in range(1, 8):
        win[s, 0:rows, :] = win[0, s:s + rows, :]


def _rows_at(win, start):
    s = start % 8
    return win[s, start - s:start - s + CONV_ROWS, :]


def _conv_fwd(glu, dw, dwb, *, tm=256, tc=512, name="conv_fwd", carry=None):
    S, D = glu.shape
    tm, tc = min(tm, S), min(tc, D)
    ni = S // tm

    def body(cur_ref, prev_ref, dw_ref, dwb_ref, o_ref, win):
        i = pl.program_id(1)
        tail = prev_ref[tm - CONV_PAD:tm, :]
        win[0, 0:CONV_PAD, :] = jnp.where(i > 0, tail, jnp.zeros_like(tail))
        win[0, CONV_PAD:CONV_PAD + tm, :] = cur_ref[...]
        _shift_copies(win)
        first = CONV_PAD - CONV_WIDTH + 1
        for r0 in range(0, tm, CONV_ROWS):
            acc = jnp.zeros((CONV_ROWS, tc), F32) + dwb_ref[...]
            for k in range(CONV_WIDTH):
                acc = acc + _rows_at(win, r0 + first + k) * dw_ref[k:k + 1, :]
            o_ref[r0:r0 + CONV_ROWS, :] = acc

    return _pcall(
        body, name=name, grid=(D // tc, ni),
        in_specs=[pl.BlockSpec((tm, tc), lambda j, i: (i, j)),
                  pl.BlockSpec((tm, tc), lambda j, i: (jnp.maximum(i - 1, 0), j)),
                  pl.BlockSpec((CONV_PAD, tc), lambda j, i: (0, j)),
                  pl.BlockSpec((1, tc), lambda j, i: (0, j))],
        out_specs=[pl.BlockSpec((tm, tc), lambda j, i: (i, j))],
        out_shape=[jax.ShapeDtypeStruct((S, D), F32)],
        scratch_shapes=[pltpu.VMEM((8, tm + CONV_PAD, tc), F32)],
        operands=(glu, glu, dw, dwb), vmem=_vmem(8 * tm * tc * 4), carry=carry)[0]


def _conv_bwd(dc, glu, a_pre, g_pre, dw, ba, bg, *, tm=256, tc=512, name="conv_bwd", carry=None):
    S, D = dc.shape
    tm, tc = min(tm, S), min(tc, D)
    ni = S // tm

    def fold8(v):
        out = v[0:8]
        for r in range(8, CONV_ROWS, 8):
            out = out + v[r:r + 8]
        return out

    def body(dc_ref, dcn_ref, glu_ref, glup_ref, a_ref, g_ref, dw_ref, ba_ref, bg_ref,
             da_ref, dg_ref, ddw_ref, dba_ref, dbg_ref, dwin, gwin, taps):
        i = pl.program_id(1)

        @pl.when(i == 0)
        def _():
            ddw_ref[...] = jnp.zeros_like(ddw_ref)
            dba_ref[...] = jnp.zeros_like(dba_ref)
            dbg_ref[...] = jnp.zeros_like(dbg_ref)

        head = dcn_ref[0:CONV_PAD, :]
        dwin[0, 0:tm, :] = dc_ref[...]
        dwin[0, tm:tm + CONV_PAD, :] = jnp.where(i < ni - 1, head, jnp.zeros_like(head))
        tail = glup_ref[tm - CONV_PAD:tm, :]
        gwin[0, 0:CONV_PAD, :] = jnp.where(i > 0, tail, jnp.zeros_like(tail))
        gwin[0, CONV_PAD:CONV_PAD + tm, :] = glu_ref[...]
        _shift_copies(dwin)
        _shift_copies(gwin)
        taps[...] = jnp.zeros_like(taps)
        first = CONV_PAD - CONV_WIDTH + 1
        sum_a = jnp.zeros((8, tc), F32)
        sum_g = jnp.zeros((8, tc), F32)
        for r0 in range(0, tm, CONV_ROWS):
            dcur = dc_ref[r0:r0 + CONV_ROWS, :]
            dglu = jnp.zeros((CONV_ROWS, tc), F32)
            for k in range(CONV_WIDTH):
                dglu = dglu + _rows_at(dwin, r0 + CONV_WIDTH - 1 - k) * dw_ref[k:k + 1, :]
                taps[k] += fold8(dcur * _rows_at(gwin, r0 + first + k))
            a = a_ref[r0:r0 + CONV_ROWS, :] + ba_ref[...]
            sg = _sigmoid(g_ref[r0:r0 + CONV_ROWS, :] + bg_ref[...])
            da = dglu * sg
            dg = dglu * a * sg * (1.0 - sg)
            da_ref[r0:r0 + CONV_ROWS, :] = da.astype(BF16)
            dg_ref[r0:r0 + CONV_ROWS, :] = dg.astype(BF16)
            sum_a = sum_a + fold8(da)
            sum_g = sum_g + fold8(dg)
        ddw_ref[...] += jnp.sum(taps[...], axis=1)
        dba_ref[...] += jnp.sum(sum_a, axis=0, keepdims=True)
        dbg_ref[...] += jnp.sum(sum_g, axis=0, keepdims=True)

    tile = lambda f: pl.BlockSpec((tm, tc), f)
    vec = pl.BlockSpec((1, tc), lambda j, i: (0, j))
    return _pcall(
        body, name=name, grid=(D // tc, ni),
        in_specs=[tile(lambda j, i: (i, j)), tile(lambda j, i: (jnp.minimum(i + 1, ni - 1), j)),
                  tile(lambda j, i: (i, j)), tile(lambda j, i: (jnp.maximum(i - 1, 0), j)),
                  tile(lambda j, i: (i, j)), tile(lambda j, i: (i, j)),
                  pl.BlockSpec((CONV_PAD, tc), lambda j, i: (0, j)), vec, vec],
        out_specs=[tile(lambda j, i: (i, j)), tile(lambda j, i: (i, j)),
                   pl.BlockSpec((CONV_PAD, tc), lambda j, i: (0, j)), vec, vec],
        out_shape=[jax.ShapeDtypeStruct((S, D), BF16), jax.ShapeDtypeStruct((S, D), BF16),
                   jax.ShapeDtypeStruct((CONV_PAD, D), F32), jax.ShapeDtypeStruct((1, D), F32),
                   jax.ShapeDtypeStruct((1, D), F32)],
        scratch_shapes=[pltpu.VMEM((8, tm + CONV_PAD, tc), F32), pltpu.VMEM((8, tm + CONV_PAD, tc), F32),
                        pltpu.VMEM((CONV_PAD, 8, tc), F32)],
        operands=(dc, dc, glu, glu, a_pre, g_pre, dw, ba, bg), vmem=_vmem(16 * tm * tc * 4), carry=carry)


def _nt(a, b):
    return lax.dot_general(a, b, _DIMS["nt"], preferred_element_type=F32)


def _tn(a, b):
    return lax.dot_general(a, b, _DIMS["tn"], preferred_element_type=F32)


HEADS_TOGETHER = 8
STAT_LANES = 128


def _per_head_pack(cols):
    rows = cols[0].shape[0]
    lane = lax.broadcasted_iota(jnp.int32, (rows, STAT_LANES), 1)
    out = jnp.zeros((rows, STAT_LANES), F32)
    for h, col in enumerate(cols):
        out = jnp.where(lane == h, col, out)
    return out


def _window_mask(qi, kj, first_key):
    B = ATTN_BLOCK
    return ((kj < B) & (kj >= qi) & (kj >= first_key)) | ((kj >= B) & (kj - B <= qi))


def _attn_fwd(q_rot, k, v, g, dil):
    S, D = k.shape
    H = D // HEAD_DIM
    L = S // dil
    nb_count = L // ATTN_BLOCK
    B = ATTN_BLOCK

    def body(q_ref, kc_ref, kp_ref, vc_ref, vp_ref, o_ref, lse_ref):
        nb = pl.program_id(1)
        qi = lax.broadcasted_iota(jnp.int32, (B, 2 * B), 0)
        kj = lax.broadcasted_iota(jnp.int32, (B, 2 * B), 1)
        valid = _window_mask(qi, kj, jnp.where(nb > 0, 0, B))
        stats = []
        for h0 in range(0, H, HEADS_TOGETHER):
            heads = range(h0, min(h0 + HEADS_TOGETHER, H))
            hs = [slice(h * HEAD_DIM, (h + 1) * HEAD_DIM) for h in heads]
            kk = [jnp.concatenate([kp_ref[:, c], kc_ref[:, c]], axis=0) for c in hs]
            vv = [jnp.concatenate([vp_ref[:, c], vc_ref[:, c]], axis=0) for c in hs]
            s = [jnp.where(valid, _nt(q_ref[:, c], kk_) * ATTN_SCALE, NEG) for c, kk_ in zip(hs, kk)]
            m = [jnp.max(s_, axis=1, keepdims=True) for s_ in s]
            p = [jnp.exp(s_ - m_) for s_, m_ in zip(s, m)]
            l = [jnp.sum(p_, axis=1, keepdims=True) for p_ in p]
            o = [jnp.dot(p_.astype(BF16), vv_, preferred_element_type=F32) / l_ for p_, vv_, l_ in zip(p, vv, l)]
            for c, o_ in zip(hs, o):
                o_ref[:, c] = o_.astype(o_ref.dtype)
            stats += [m_ + jnp.log(l_) for m_, l_ in zip(m, l)]
        lse_ref[...] = _per_head_pack(stats)

    blk = lambda f: pl.BlockSpec((B, D), f)
    cur = lambda r, nb: (nb, r)
    prev = lambda r, nb: (jnp.maximum(nb - 1, 0), r)
    o, lse = pl.pallas_call(
        body, name=f"attn_fwd_g{g}", grid=(dil, nb_count),
        in_specs=[blk(lambda r, nb: (nb, r * N_GROUPS + g)), blk(cur), blk(prev), blk(cur), blk(prev)],
        out_specs=[blk(cur), pl.BlockSpec((B, STAT_LANES), cur)],
        out_shape=[jax.ShapeDtypeStruct((L, dil * D), BF16), jax.ShapeDtypeStruct((L, dil * STAT_LANES), F32)],
        compiler_params=pltpu.CompilerParams(dimension_semantics=("parallel", "arbitrary"),
                                             vmem_limit_bytes=_vmem(12 * B * D * 4)),
    )(q_rot.reshape(L, dil * N_GROUPS * D), k.reshape(L, dil * D), k.reshape(L, dil * D),
      v.reshape(L, dil * D), v.reshape(L, dil * D))
    return o.reshape(S, D), lse.reshape(S, STAT_LANES)


def _attn_bwd(q_rot, k, v, do, lse, dlt, g, dil, *, name, carry=None):
    S, D = k.shape
    H = D // HEAD_DIM
    L = S // dil
    nb_count = L // ATTN_BLOCK
    B = ATTN_BLOCK

    def body(q_ref, kc_ref, kp_ref, vc_ref, vp_ref, do_ref, l_ref, d_ref, dq_ref, dk_ref, dv_ref, keep_k, keep_v):
        s_id = pl.program_id(1)

        @pl.when(s_id == 0)
        def _():
            keep_k[...] = jnp.zeros_like(keep_k)
            keep_v[...] = jnp.zeros_like(keep_v)

        @pl.when(s_id < nb_count)
        def _():
            qi = lax.broadcasted_iota(jnp.int32, (B, 2 * B), 0)
            kj = lax.broadcasted_iota(jnp.int32, (B, 2 * B), 1)
            valid = _window_mask(qi, kj, jnp.where(s_id > 0, 0, B))
            for h0 in range(0, H, HEADS_TOGETHER):
                heads = list(range(h0, min(h0 + HEADS_TOGETHER, H)))
                hs = [slice(h * HEAD_DIM, (h + 1) * HEAD_DIM) for h in heads]
                q = [q_ref[:, c] for c in hs]
                dout = [do_ref[:, c] for c in hs]
                kk = [jnp.concatenate([kp_ref[:, c], kc_ref[:, c]], axis=0) for c in hs]
                vv = [jnp.concatenate([vp_ref[:, c], vc_ref[:, c]], axis=0) for c in hs]
                s = [jnp.where(valid, _nt(q_, kk_) * ATTN_SCALE, NEG) for q_, kk_ in zip(q, kk)]
                dp = [_nt(do_, vv_) for do_, vv_ in zip(dout, vv)]
                p = [jnp.exp(s_ - l_ref[:, h:h + 1]) for s_, h in zip(s, heads)]
                ds = [(p_ * (dp_ - d_ref[:, h:h + 1])).astype(BF16) for p_, dp_, h in zip(p, dp, heads)]
                dq = [jnp.dot(ds_, kk_, preferred_element_type=F32) * ATTN_SCALE for ds_, kk_ in zip(ds, kk)]
                dkk = [_tn(ds_, q_) * ATTN_SCALE for ds_, q_ in zip(ds, q)]
                dvv = [_tn(p_.astype(BF16), do_) for p_, do_ in zip(p, dout)]
                for c, dq_, dkk_, dvv_ in zip(hs, dq, dkk, dvv):
                    dq_ref[:, c] = dq_.astype(dq_ref.dtype)
                    dk_ref[:, c] = (keep_k[:, c] + dkk_[0:B]).astype(dk_ref.dtype)
                    dv_ref[:, c] = (keep_v[:, c] + dvv_[0:B]).astype(dv_ref.dtype)
                    keep_k[:, c] = dkk_[B:2 * B]
                    keep_v[:, c] = dvv_[B:2 * B]

        @pl.when(s_id == nb_count)
        def _():
            dk_ref[...] = keep_k[...].astype(dk_ref.dtype)
            dv_ref[...] = keep_v[...].astype(dv_ref.dtype)

    last = nb_count - 1
    blk = lambda f: pl.BlockSpec((B, D), f)
    cur = lambda r, s: (jnp.minimum(s, last), r)
    prev = lambda r, s: (jnp.maximum(jnp.minimum(s, last) - 1, 0), r)
    lag = lambda r, s: (jnp.maximum(s - 1, 0), r)
    qcur = lambda r, s: (jnp.minimum(s, last), r * N_GROUPS + g)
    qv = q_rot.reshape(L, dil * N_GROUPS * D)
    view = lambda t: t.reshape(L, dil * D)
    sview = lambda t: t.reshape(L, dil * STAT_LANES)
    stat = lambda f: pl.BlockSpec((B, STAT_LANES), f)
    dq, dk, dv = _pcall(
        body, name=name, grid=(dil, nb_count + 1),
        in_specs=[blk(qcur), blk(cur), blk(prev), blk(cur), blk(prev), blk(cur), stat(cur), stat(cur)],
        out_specs=[blk(cur), blk(lag), blk(lag)],
        out_shape=[jax.ShapeDtypeStruct((L, dil * D), BF16)] * 3,
        scratch_shapes=[pltpu.VMEM((B, D), F32), pltpu.VMEM((B, D), F32)],
        operands=(qv, view(k), view(k), view(v), view(v), view(do), sview(lse), sview(dlt)),
        vmem=_vmem(24 * B * D * 4), carry=carry)
    return dq.reshape(S, D), dk.reshape(S, D), dv.reshape(S, D)


def _mlp_ple_fwd(z1, h1b, p_l, W, vec, l, run, kv_vec=None, target=None):
    D = z1.shape[1]
    g1, b1, g2, b2 = vec

    t = run(_mm, h1b, W[f"mlp_up{l}"], mode="nn", outs=[BF16], tm=1024, name=f"mlp_up{l}")

    def z2_ep(acc, z1_t, g1_, b1_, g2_, b2_):
        z2 = ALPHA * _ln(z1_t, g1_, b1_) + acc
        return z2, _ln(z2, g2_, b2_)

    z2, h2b = run(_mm, t, W[f"mlp_down{l}"], mode="nn", outs=[F32, BF16], extras=[z1], vecs=[g1, b1, g2, b2], a_fn=_sq_relu,
                  epilogue=z2_ep, ep_rows=EPILOGUE_ROWS, name=f"mlp_down{l}")
    act = None
    pp = run(_mm, p_l, W[f"ple_proj{l}"], mode="nn", outs=[F32], name=f"ple_proj{l}")
    if kv_vec is None:
        def head_ep(acc, z2_t, pp_t, tgt_t, g2_, b2_):
            y, gt = _ple_out(z2_t, pp_t, acc, g2_, b2_)
            err = y - tgt_t
            dy = err * (1.0 / D)
            return dy, dy * gt, dy * pp_t * gt * (1.0 - gt), 0.5 * err * err * (1.0 / D)

        dy, d_pp, d_gpre, loss_cols = run(_mm, h2b, W[f"ple_gate{l}"], mode="nn", outs=[F32, BF16, BF16], sums=1,
                                          extras=[z2, pp, target], vecs=[g2, b2], epilogue=head_ep,
                                          ep_rows=EPILOGUE_ROWS, tm=256, name=f"ple_gate{l}")
        return t, act, z2, h2b, pp, dy, d_pp, d_gpre, loss_cols

    def x1_ep(acc, z2_t, pp_t, g2_, b2_, kg, kb):
        x1, _ = _ple_out(z2_t, pp_t, acc, g2_, b2_)
        return acc, x1, _ln(x1, kg, kb), x1

    gpre, x1, kvn, x1b = run(_mm, h2b, W[f"ple_gate{l}"], mode="nn", outs=[F32, F32, BF16, BF16], extras=[z2, pp],
                             vecs=[g2, b2, *kv_vec], epilogue=x1_ep, ep_rows=EPILOGUE_ROWS, tm=256, name=f"ple_gate{l}")
    return t, act, z2, h2b, pp, gpre, x1, kvn, x1b


def _mlp_ple_bwd(dy, d_pp, d_gpre, p_l, z1, h1b, t, act, z2, h2b, wts, vec, l, run, produce):
    D = z1.shape[1]
    up, down, pp_w, pg_w = wts
    g1, b1, g2, b2 = vec
    produce(f"ple_proj{l}", run(_mm, p_l, d_pp, mode="tn", outs=[BF16], layout=("col", N_CHIPS), name=f"d_ple_proj{l}"))
    produce(f"ple_gate{l}", run(_mm, h2b, d_gpre, mode="tn", outs=[BF16], layout=("row", N_CHIPS), name=f"d_ple_gate{l}"))
    def ln2_bwd(acc, dy_t, z2_t, g2_):
        dh2 = acc + dy_t
        n, rstd = _ln_norm(z2_t)
        dz2 = _ln_bwd(dh2, n, rstd, g2_)
        return dz2, dz2, dh2 * n, dh2

    dz2, dz2b, dg2, db2 = run(_mm, d_gpre, pg_w, mode="nt", outs=[F32, BF16], sums=2, extras=[dy, z2], vecs=[g2],
                              epilogue=ln2_bwd, ep_rows=EPILOGUE_ROWS, tm=256, name=f"dh2_{l}")
    produce(f"mlp_down{l}", run(_mm, t, dz2b, mode="tn", outs=[BF16], layout=("row", N_CHIPS), a_fn=_sq_relu,
                                name=f"d_mlp_down{l}"))
    dt = run(_mm, dz2b, down, mode="nt", outs=[BF16], extras=[t],
             epilogue=lambda acc, t_: (acc * 2.0 * jnp.maximum(t_.astype(F32), 0.0),), name=f"dt{l}")
    produce(f"mlp_up{l}", run(_mm, h1b, dt, mode="tn", outs=[BF16], layout=("col", N_CHIPS), name=f"d_mlp_up{l}"))
    def ln1_bwd(acc, dz2_t, z1_t, g1_):
        dh1 = acc + ALPHA * dz2_t
        n, rstd = _ln_norm(z1_t)
        dz1 = _ln_bwd(dh1, n, rstd, g1_)
        return dz1, dz1, dh1 * n, dh1

    dz1, dz1b, dg1, db1 = run(_mm, dt, up, mode="nt", outs=[F32, BF16], sums=2, extras=[dz2, z1], vecs=[g1],
                              epilogue=ln1_bwd, ep_rows=EPILOGUE_ROWS, tk=1024, name=f"dh1_{l}")
    return dz1, dz1b, (dg1, db1, dg2, db2)


def _ple_out(z2, pp, gpre, g2, b2):
    gt = _sigmoid(gpre)
    return _ln(z2, g2, b2) + pp * gt, gt


_GATHER_AT = {
    "conv_in_a": ("conv_w_out",),
    "conv_in_g": ("ple_gate0", "ple_proj0"),
    "conv_fwd": ("mlp_up0",),
    "mlp_up0": ("mlp_down0",),
    "mlp_down0": ("attn_w_q", "w_kv"),
    "kv_k": ("attn_w_o",),
    "kv_v": ("ple_proj1", "ple_gate1"),
    "attn_q": ("mlp_up1",),
    "mlp_up1": ("mlp_down1",),
}
_GATHER_FIRST = ("conv_w_in",)


def _local_step(x, p, cosf, sinf, target, W, V, shards=None, reducer=None, chip_arr=None):
    S, D = x.shape
    gw, gv = {}, {}
    if shards is not None:
        W = dict(_Gather(_GATHER_FIRST, shards, chip_arr).run_alone("gather_first"))

    def run(fn, *args, name, **kw):
        gather = _Gather(_GATHER_AT[name], shards, chip_arr) if (shards is not None and name in _GATHER_AT) else None
        carry = gather if reducer is None or gather is not None else reducer.carry(name)
        out = fn(*args, name=name, carry=carry, **kw)
        if gather is not None:
            W.update(gather.result)
        elif reducer is not None:
            reducer.carried()
        return out

    (xb,) = _rows(lambda t_: (t_,), [x], [], [(D, BF16)], [], tm=256, name="x_bf16")
    a_pre = run(_mm, xb, W["conv_w_in"], b_sel=0, mode="nn", outs=[F32], name="conv_in_a")
    g_pre, glu = run(_mm, xb, W["conv_w_in"], b_sel=1, mode="nn", outs=[F32, F32], extras=[a_pre],
                     vecs=[V["conv_b_a"], V["conv_b_g"]], tm=256, ep_rows=EPILOGUE_ROWS,
                     epilogue=lambda acc, a, ba, bg: (acc, (a + ba) * _sigmoid(acc + bg)), name="conv_in_g")
    cv = run(_conv_fwd, glu, V["conv_dw"], V["conv_dw_b"], name="conv_fwd")

    def silu_ln(c, g_, b_):
        y = _ln(c, g_, b_)
        return (y * _sigmoid(y),)

    (sb,) = _rows(silu_ln, [cv], [V["conv_ln_g"], V["conv_ln_b"]], [(D, BF16)], [], tm=256, name="conv_ln_fwd")
    def z1_ep(acc, x_t, g_, b_):
        z1 = ALPHA * x_t + acc
        return z1, _ln(z1, g_, b_)

    vec0 = (V["ln1_g0"], V["ln1_b0"], V["ln2_g0"], V["ln2_b0"])
    vec1 = (V["ln1_g1"], V["ln1_b1"], V["ln2_g1"], V["ln2_b1"])
    z1_0, h1b_0 = _mm(sb, W["conv_w_out"], mode="nn", outs=[F32, BF16], extras=[x], vecs=[vec0[0], vec0[1]],
                      epilogue=z1_ep, ep_rows=EPILOGUE_ROWS, name="conv_out")
    t0, act0, z2_0, h2b_0, pp0, gpre0, x1, kvn, x1b = _mlp_ple_fwd(z1_0, h1b_0, p[0], W, vec0, 0, run,
                                                                   (V["kv_ln_g"], V["kv_ln_b"]))

    rot_ep = lambda acc, c_, s_: (_rot(acc, c_, s_),)
    k_rot = run(_mm, kvn, W["w_kv"], b_sel=0, mode="nn", outs=[BF16], rextras=[cosf, sinf], epilogue=rot_ep, name="kv_k")
    v_b = run(_mm, kvn, W["w_kv"], b_sel=1, mode="nn", outs=[BF16], name="kv_v")
    q_rot = run(_mm, x1b, W["attn_w_q"], mode="nn", outs=[BF16], rextras=[cosf, sinf], epilogue=rot_ep, name="attn_q")
    og, lg = [], []
    for g, dil in enumerate(DILATIONS):
        o_g, l_g = _attn_fwd(q_rot, k_rot, v_b, g, dil)
        og.append(o_g)
        lg.append(l_g)

    def merge(o0, o1, o2, l0, l1, l2):
        m = jnp.maximum(jnp.maximum(l0, l1), l2)
        e = [jnp.exp(l0 - m), jnp.exp(l1 - m), jnp.exp(l2 - m)]
        den = e[0] + e[1] + e[2]
        w = [e_g / den for e_g in e]
        o = _per_head(o0, lambda oh, h: sum(w[g][:, h:h + 1] * (o0, o1, o2)[g][:, h * HEAD_DIM:(h + 1) * HEAD_DIM].astype(F32)
                                            for g in range(N_GROUPS)))
        return o, m + jnp.log(den)

    ob, lse = _rows(merge, og + lg, [], [(D, BF16), (STAT_LANES, F32)], [], tm=256, name="attn_merge")
    z1_1, h1b_1 = _mm(ob, W["attn_w_o"], mode="nn", outs=[F32, BF16], extras=[x1], vecs=[vec1[0], vec1[1]],
                      epilogue=z1_ep, ep_rows=EPILOGUE_ROWS, name="attn_out")
    t1, act1, z2_1, h2b_1, pp1, dy1, d_pp1, d_gpre1, loss_cols = _mlp_ple_fwd(z1_1, h1b_1, p[1], W, vec1, 1, run,
                                                                              target=target)
    wts0 = (W["mlp_up0"], W["mlp_down0"], W["ple_proj0"], W["ple_gate0"])
    wts1 = (W["mlp_up1"], W["mlp_down1"], W["ple_proj1"], W["ple_gate1"])


    def produce(name, grad):
        gw[name] = grad
        if reducer is not None:
            reducer.produced(name, grad)

    dz1_1, dz1b_1, (gv["ln1_g1"], gv["ln1_b1"], gv["ln2_g1"], gv["ln2_b1"]) = _mlp_ple_bwd(
        dy1, d_pp1, d_gpre1, p[1], z1_1, h1b_1, t1, act1, z2_1, h2b_1, wts1, vec1, 1, run, produce)
    produce("attn_w_o", run(_mm, ob, dz1b_1, mode="tn", outs=[BF16], layout=("row", N_CHIPS), name="d_attn_w_o"))

    do_b = run(_mm, dz1b_1, W["attn_w_o"], mode="nt", outs=[BF16], name="attn_do")

    def delta(do_t, o_t):
        prod = do_t.astype(F32) * o_t.astype(F32)
        H = D // HEAD_DIM
        return (_per_head_pack([jnp.sum(prod[:, h * HEAD_DIM:(h + 1) * HEAD_DIM], axis=1, keepdims=True) for h in range(H)]),)

    (dlt,) = _rows(delta, [do_b, ob], [], [(STAT_LANES, F32)], [], tm=256, name="attn_delta")
    dqs, dks, dvs = [], [], []
    for g, dil in enumerate(DILATIONS):
        dq_g, dk_g, dv_g = run(_attn_bwd, q_rot, k_rot, v_b, do_b, lse, dlt, g, dil, name=f"attn_bwd_g{g}")
        dqs.append(dq_g)
        dks.append(dk_g)
        dvs.append(dv_g)

    def unrot(q0, q1, q2, k0, k1, k2, v0, v1, v2, c_, s_):
        dq = jnp.concatenate([_rot_t(t_.astype(F32), c_, s_) for t_ in (q0, q1, q2)], axis=1)
        f = lambda t_: t_.astype(F32)
        return dq, _rot_t(f(k0) + f(k1) + f(k2), c_, s_), f(v0) + f(v1) + f(v2)

    dq, dk, dv = run(_rows, unrot, dqs + dks + dvs + [cosf, sinf], [], [(N_GROUPS * D, BF16), (D, BF16), (D, BF16)], [],
                     tm=128, name="attn_unrot")
    produce("attn_w_q", run(_mm, x1b, dq, mode="tn", outs=[BF16], layout=("col", N_CHIPS), name="d_attn_w_q"))
    dx1_q = run(_mm, dq, W["attn_w_q"], mode="nt", outs=[F32], extras=[dz1_1],
                epilogue=lambda acc, e: (acc + ALPHA * e,), name="dx1_q")
    produce("w_kv", jnp.concatenate(
        [run(_mm, kvn, dk, mode="tn", outs=[BF16], layout=("col", 2), name="d_w_kv_k"),
         run(_mm, kvn, dv, mode="tn", outs=[BF16], layout=("col", 2), name="d_w_kv_v")], axis=1))
    dkvn_k = run(_mm, dk, W["w_kv"], b_sel=0, mode="nt", outs=[F32], name="dkvn_k")
    def x1_bwd(acc, dkvn_k_t, dx1q_t, x1_t, pp, gpre, kg):
        dkvn_t = acc + dkvn_k_t
        n, rstd = _ln_norm(x1_t)
        dy = dx1q_t + _ln_bwd(dkvn_t, n, rstd, kg)
        gt = _sigmoid(gpre)
        return dy, dy * gt, dy * pp * gt * (1.0 - gt), dkvn_t * n, dkvn_t

    dy0, d_pp0, d_gpre0, gv["kv_ln_g"], gv["kv_ln_b"] = run(
        _mm, dv, W["w_kv"], b_sel=1, mode="nt", outs=[F32, BF16, BF16], sums=2, extras=[dkvn_k, dx1_q, x1, pp0, gpre0],
        vecs=[V["kv_ln_g"]], epilogue=x1_bwd, ep_rows=EPILOGUE_ROWS, tm=256, name="dkvn_v")

    dz1_0, dz1b_0, (gv["ln1_g0"], gv["ln1_b0"], gv["ln2_g0"], gv["ln2_b0"]) = _mlp_ple_bwd(
        dy0, d_pp0, d_gpre0, p[0], z1_0, h1b_0, t0, act0, z2_0, h2b_0, wts0, vec0, 0, run, produce)
    produce("conv_w_out", run(_mm, sb, dz1b_0, mode="tn", outs=[BF16], layout=("row", N_CHIPS), name="d_conv_w_out"))
    ds = run(_mm, dz1b_0, W["conv_w_out"], mode="nt", outs=[F32], name="conv_ds")

    def conv_ln_bwd(ds_t, c_t, g_, b_):
        n, rstd = _ln_norm(c_t)
        y = n * g_ + b_
        sg = _sigmoid(y)
        dln = ds_t * sg * (1.0 + y * (1.0 - sg))
        dc = _ln_bwd(dln, n, rstd, g_)
        return dc, dln * n, dln, dc

    dc, gv["conv_ln_g"], gv["conv_ln_b"], gv["conv_dw_b"] = _rows(
        conv_ln_bwd, [ds, cv], [V["conv_ln_g"], V["conv_ln_b"]], [(D, F32)], [D, D, D], tm=256, name="conv_ln_bwd")
    da, dg, gv["conv_dw"], gv["conv_b_a"], gv["conv_b_g"] = run(
        _conv_bwd, dc, glu, a_pre, g_pre, V["conv_dw"], V["conv_b_a"], V["conv_b_g"], name="conv_bwd")
    produce("conv_w_in", jnp.concatenate(
        [run(_mm, xb, da, mode="tn", outs=[BF16], layout=("col", 2), name="d_conv_w_in_a"),
         run(_mm, xb, dg, mode="tn", outs=[BF16], layout=("col", 2), name="d_conv_w_in_g")], axis=1))
    dx_a = run(_mm, da, W["conv_w_in"], b_sel=0, mode="nt", outs=[F32], extras=[dz1_0],
               epilogue=lambda acc, e: (acc + ALPHA * e,), name="dx_a")
    grad_x = run(_mm, dg, W["conv_w_in"], b_sel=1, mode="nt", outs=[F32], extras=[dx_a],
                 epilogue=lambda acc, e: (acc + e,), name="dx_g")
    if reducer is not None:
        reducer.carry("share_last").run_alone("share_last")
        reducer.carried()
    return loss_cols, grad_x, gw, gv


def _place():
    x, y, c = lax.axis_index("x"), lax.axis_index("y"), lax.axis_index("c")
    chips = [(1 - x, y), (x, 1 - y), (1 - x, 1 - y)]
    return x, y, c, chips


def _remote(src, dst, ssem, rsem, dev):
    return pltpu.make_async_remote_copy(src_ref=src, dst_ref=dst, send_sem=ssem, recv_sem=rsem, device_id=dev,
                                        device_id_type=MESH)


def _allgather8(block, name):
    R, C = block.shape

    def body(x_ref, out_ref, send_sems, recv_sems, local_sem):
        x, y, c, chips = _place()
        me, sibling = (x, y, c), (x, y, 1 - c)

        def slot(px, py, pc):
            return out_ref.at[4 * px + 2 * py + pc]

        def copy(k, blockpos, to, src=None):
            return _remote(slot(*blockpos) if src is None else src, slot(*blockpos), send_sems.at[k], recv_sems.at[k], to)

        mine = pltpu.make_async_copy(x_ref, slot(*me), local_sem)
        mine.start()
        first = [copy(0, me, sibling, src=x_ref)]
        first += [copy(1 + j, me, (*chip, c), src=x_ref) for j, chip in enumerate(chips)]
        for cp in first:
            cp.start()
        passed = [copy(4 + j, (*chip, c), sibling) for j, chip in enumerate(chips)]
        for j, chip in enumerate(chips):
            copy(1 + j, (*chip, c), me).wait_recv()
            passed[j].start()
        copy(0, sibling, me).wait_recv()
        for j, chip in enumerate(chips):
            copy(4 + j, (*chip, 1 - c), me).wait_recv()
        for cp in first + passed:
            cp.wait_send()
        mine.wait()

    return pl.pallas_call(
        body, name=name, out_shape=jax.ShapeDtypeStruct((8, R, C), block.dtype),
        in_specs=[pl.BlockSpec(memory_space=pltpu.VMEM)], out_specs=pl.BlockSpec(memory_space=pltpu.VMEM),
        scratch_shapes=[pltpu.SemaphoreType.DMA((7,)), pltpu.SemaphoreType.DMA((7,)), pltpu.SemaphoreType.DMA],
        compiler_params=pltpu.CompilerParams(vmem_limit_bytes=_vmem(10 * _nbytes((R, C), block.dtype))),
    )(block)


_MATS = (
    ("conv_w_in", "conv_w_in", 0, "col", True),
    ("conv_w_out", "conv_w_out", 0, "row", False),
    ("mlp_up0", "mlp_up", 0, "col", False),
    ("mlp_down0", "mlp_down", 0, "row", False),
    ("ple_proj0", "ple_proj", 0, "col", False),
    ("ple_gate0", "ple_gate", 0, "row", False),
    ("w_kv", "w_kv", None, "col", True),
    ("attn_w_q", "attn_w_q", 0, "col", False),
    ("attn_w_o", "attn_w_o", 0, "row", False),
    ("mlp_up1", "mlp_up", 1, "col", False),
    ("mlp_down1", "mlp_down", 1, "row", False),
    ("ple_proj1", "ple_proj", 1, "col", False),
    ("ple_gate1", "ple_gate", 1, "row", False),
)


class _Carry:
    result = None
    aliases = {}

    def set_result(self, outs):
        self.result = dict(zip(self.names, outs))

    def run_alone(self, name):
        n_in, n_out = len(self.ins), len(self.out_shape)

        def body(*refs):
            in_refs, out_refs, sems = refs[:n_in], refs[n_in:n_in + n_out], refs[n_in + n_out:]
            self.start(in_refs, out_refs, sems)
            self.finish(in_refs, out_refs, sems)

        outs = pl.pallas_call(body, name=name, out_shape=self.out_shape, in_specs=[ANY] * n_in, out_specs=[ANY] * n_out,
                              scratch_shapes=self.scratch, input_output_aliases=dict(self.aliases))(*self.ins)
        self.set_result(outs)
        return self.result


class _Gather(_Carry):
    def __init__(self, names, shards, chip_arr):
        mats = [m for m in _MATS if m[0] in names]
        srcs = sorted({m[1] for m in mats})
        self.names = [m[0] for m in mats]
        self.out_shape, self.geo, placed = [], [], []
        for name, src, layer, kind, split in mats:
            s = shards[src]
            ks, ns = s.shape[-2:]
            K, N = (ks, ns * N_CHIPS) if kind == "col" else (ks * N_CHIPS, ns)
            self.out_shape.append(jax.ShapeDtypeStruct((2, K, N // 2) if split else (K, N), BF16))
            self.geo.append((srcs.index(src), layer if s.ndim == 3 else None, kind, split, K, N))
            placed.append(_place_shard(s, layer if s.ndim == 3 else None, kind, split, chip_arr, f"place_{name}"))
        T = len(mats)
        self.ins = [shards[n] for n in srcs] + placed
        self.aliases = {len(srcs) + t: t for t in range(T)}
        self.scratch = [pltpu.SemaphoreType.DMA((3 * T,)) for _ in range(4)]
        self.result = None

    def _copies(self, in_refs, out_refs, sems):
        geo, T = self.geo, len(self.geo)
        s_ici, r_ici, s_d2d, r_d2d = sems
        x, y, c, chips = _place()
        me = 2 * x + y
        sibling = (x, y, 1 - c)
        idx = [2 * cx + cy for cx, cy in chips]

        def src_ref(t):
            i, layer, _, _, _, _ = geo[t]
            return in_refs[i] if layer is None else in_refs[i].at[layer]

        def src_half(t, h):
            _, _, kind, _, K, N = geo[t]
            if kind == "col":
                return src_ref(t).at[pl.ds(h * (K // 2), K // 2), :]
            return src_ref(t).at[:, pl.ds(h * (N // 2), N // 2)]

        def dst(t, j, h):
            _, _, kind, split, K, N = geo[t]
            n, k = N // N_CHIPS, K // N_CHIPS
            if kind == "col":
                rows = slice(None) if h is None else pl.ds(h * (K // 2), K // 2)
                if split:
                    return out_refs[t].at[j // 2, rows, pl.ds((j % 2) * n, n)]
                return out_refs[t].at[rows, pl.ds(j * n, n)]
            cols = slice(None) if h is None else pl.ds(h * (N // 2), N // 2)
            return out_refs[t].at[pl.ds(j * k, k), cols]

        sends = [_remote(src_half(t, c), dst(t, me, c), s_ici.at[3 * t + kk], r_ici.at[3 * t + kk], (*chips[kk], c))
                 for t in range(T) for kk in range(3)]
        hops = []
        for t in range(T):
            for kk in range(3):
                mine, theirs = dst(t, idx[kk], c), dst(t, idx[kk], 1 - c)
                hops.append((_remote(mine, mine, s_ici.at[3 * t + kk], r_ici.at[3 * t + kk], sibling),
                             _remote(mine, mine, s_d2d.at[3 * t + kk], r_d2d.at[3 * t + kk], sibling),
                             _remote(theirs, theirs, s_d2d.at[3 * t + kk], r_d2d.at[3 * t + kk], sibling)))
        return sends, hops

    def start(self, in_refs, out_refs, sems):
        for cp in self._copies(in_refs, out_refs, sems)[0]:
            cp.start()

    def finish(self, in_refs, out_refs, sems):
        sends, hops = self._copies(in_refs, out_refs, sems)
        for landed, forward, _ in hops:
            landed.wait_recv()
            forward.start()
        for _, _, from_sibling in hops:
            from_sibling.wait_recv()
        for cp in sends + [h[1] for h in hops]:
            cp.wait_send()


def _place_shard(shard, layer, kind, split, chip_arr, name):
    ks, ns = shard.shape[-2:]
    K, N = (ks, ns * N_CHIPS) if kind == "col" else (ks * N_CHIPS, ns)
    tr = _fit(256, ks)
    nb = ks // tr
    if shard.ndim == 3:
        in_spec = pl.BlockSpec((None, tr, ns), lambda i, me: (layer, i, 0))
    else:
        in_spec = pl.BlockSpec((tr, ns), lambda i, me: (i, 0))
    if kind == "row":
        out_shape, out_spec = (K, N), pl.BlockSpec((tr, ns), lambda i, me: (me[0] * nb + i, 0))
    elif split:
        out_shape, out_spec = (2, K, N // 2), pl.BlockSpec((None, tr, ns), lambda i, me: (me[0] // 2, i, me[0] % 2))
    else:
        out_shape, out_spec = (K, N), pl.BlockSpec((tr, ns), lambda i, me: (i, me[0]))

    def body(me_ref, s_ref, o_ref):
        o_ref[...] = s_ref[...]

    return pl.pallas_call(
        body, name=name, out_shape=jax.ShapeDtypeStruct(out_shape, BF16),
        grid_spec=pltpu.PrefetchScalarGridSpec(num_scalar_prefetch=1, grid=(nb,), in_specs=[in_spec], out_specs=out_spec),
        compiler_params=pltpu.CompilerParams(dimension_semantics=("parallel",), vmem_limit_bytes=_vmem(4 * tr * ns * 2)),
    )(chip_arr, shard)


class _Multi(_Carry):
    def __init__(self, parts):
        self.parts = parts
        self.ins = [a for p in parts for a in p.ins]
        self.out_shape = [a for p in parts for a in p.out_shape]
        self.scratch = [a for p in parts for a in p.scratch]
        self.aliases, n_in, n_out = {}, 0, 0
        for p in parts:
            self.aliases.update({n_in + i: n_out + o for i, o in p.aliases.items()})
            n_in, n_out = n_in + len(p.ins), n_out + len(p.out_shape)

    def _split(self, seq, field):
        out, at = [], 0
        for p in self.parts:
            n = len(getattr(p, field))
            out.append(seq[at:at + n])
            at += n
        return out

    def _each(self, method, in_refs, out_refs, sems):
        for p, i, o, s in zip(self.parts, self._split(in_refs, "ins"), self._split(out_refs, "out_shape"),
                              self._split(sems, "scratch")):
            getattr(p, method)(i, o, s)

    def start(self, in_refs, out_refs, sems):
        self._each("start", in_refs, out_refs, sems)

    def finish(self, in_refs, out_refs, sems):
        self._each("finish", in_refs, out_refs, sems)

    def set_result(self, outs):
        for p, o in zip(self.parts, self._split(list(outs), "out_shape")):
            p.set_result(o)


class _PairSend(_Carry):
    def __init__(self, grads):
        self.names = list(grads)
        self.ins = [grads[n] for n in self.names]
        self.out_shape = [jax.ShapeDtypeStruct(a.shape[1:], BF16) for a in self.ins]
        T = len(self.names)
        self.scratch = [pltpu.SemaphoreType.DMA((T,)), pltpu.SemaphoreType.DMA((T,))]

    def _copies(self, in_refs, out_refs, sems):
        x, y, c, _ = _place()
        return [_remote(in_refs[t].at[1 - c], out_refs[t], sems[0].at[t], sems[1].at[t], (x, y, 1 - c))
                for t in range(len(self.names))]

    def start(self, in_refs, out_refs, sems):
        for cp in self._copies(in_refs, out_refs, sems):
            cp.start()

    def finish(self, in_refs, out_refs, sems):
        for cp in self._copies(in_refs, out_refs, sems):
            cp.wait()


class _ChipScatter(_Carry):
    def __init__(self, sums):
        self.names = list(sums)
        T = len(self.names)
        self.ins = [sums[n][0] for n in self.names] + [sums[n][1] for n in self.names]
        self.out_shape = [jax.ShapeDtypeStruct(a.shape, BF16) for a in self.ins[:T]]
        self.aliases = {T + t: t for t in range(T)}
        self.scratch = [pltpu.SemaphoreType.DMA((3 * T,)), pltpu.SemaphoreType.DMA((3 * T,))]

    def _copies(self, in_refs, out_refs, sems):
        ssem, rsem = sems
        x, y, c, chips = _place()
        me = 2 * x + y
        idx = [2 * cx + cy for cx, cy in chips]
        T = len(self.names)
        sends = [_remote(in_refs[t].at[idx[kk]], out_refs[t].at[me], ssem.at[3 * t + kk], rsem.at[3 * t + kk],
                         (*chips[kk], c)) for t in range(T) for kk in range(3)]
        lands = [_remote(out_refs[t].at[idx[kk]], out_refs[t].at[idx[kk]], ssem.at[3 * t + kk], rsem.at[3 * t + kk],
                         (*chips[kk], c)) for t in range(T) for kk in range(3)]
        return sends, lands

    def start(self, in_refs, out_refs, sems):
        for cp in self._copies(in_refs, out_refs, sems)[0]:
            cp.start()

    def finish(self, in_refs, out_refs, sems):
        sends, lands = self._copies(in_refs, out_refs, sems)
        for cp in lands:
            cp.wait_recv()
        for cp in sends:
            cp.wait_send()


class _PairShare(_Carry):
    def __init__(self, halves):
        self.names = list(halves)
        self.ins = [halves[n] for n in self.names]
        self.out_shape = [jax.ShapeDtypeStruct(a.shape, F32) for a in self.ins]
        T = len(self.names)
        self.aliases = {t: t for t in range(T)}
        self.scratch = [pltpu.SemaphoreType.DMA((T,)), pltpu.SemaphoreType.DMA((T,))]

    def _copies(self, in_refs, out_refs, sems):
        ssem, rsem = sems
        x, y, c, _ = _place()
        sibling = (x, y, 1 - c)
        T = len(self.names)
        sends = [_remote(out_refs[t].at[c], out_refs[t].at[c], ssem.at[t], rsem.at[t], sibling) for t in range(T)]
        lands = [_remote(out_refs[t].at[1 - c], out_refs[t].at[1 - c], ssem.at[t], rsem.at[t], sibling) for t in range(T)]
        return sends, lands

    def start(self, in_refs, out_refs, sems):
        for cp in self._copies(in_refs, out_refs, sems)[0]:
            cp.start()

    def finish(self, in_refs, out_refs, sems):
        sends, lands = self._copies(in_refs, out_refs, sems)
        for cp in lands:
            cp.wait_recv()
        for cp in sends:
            cp.wait_send()


def _pair_sum(own, landed, c_arr, name):
    _, ns, r, cc = own.shape
    rows = ns * r
    tr = _fit(512, rows)

    def body(c_ref, a_ref, b_ref, o_ref, o2_ref):
        total = (a_ref[...].astype(F32) + b_ref[...].astype(F32)).astype(o_ref.dtype)
        o_ref[...] = total
        o2_ref[...] = total

    tile = pl.BlockSpec((tr, cc), lambda i, c_ref: (i, 0))
    out = pl.pallas_call(
        body, name=name, out_shape=[jax.ShapeDtypeStruct((rows, cc), BF16)] * 2,
        grid_spec=pltpu.PrefetchScalarGridSpec(
            num_scalar_prefetch=1, grid=(rows // tr,),
            in_specs=[pl.BlockSpec((None, tr, cc), lambda i, c_ref: (c_ref[0], i, 0)), tile], out_specs=[tile, tile]),
        compiler_params=pltpu.CompilerParams(dimension_semantics=("parallel",), vmem_limit_bytes=_vmem(8 * tr * cc * 4)),
    )(c_arr, own.reshape(2, rows, cc), landed.reshape(rows, cc))
    return out[0].reshape(ns, r, cc), out[1].reshape(ns, r, cc)


def _chip_sum(parts, c_arr, name):
    _, r, cc = parts.shape
    tr = _fit(256, r)

    def body(c_ref, p_ref, o_ref):
        acc = p_ref[0].astype(F32)
        for j in range(1, N_CHIPS):
            acc = acc + p_ref[j].astype(F32)
        o_ref[...] = acc

    return pl.pallas_call(
        body, name=name, out_shape=jax.ShapeDtypeStruct((2, r, cc), F32),
        grid_spec=pltpu.PrefetchScalarGridSpec(
            num_scalar_prefetch=1, grid=(r // tr,),
            in_specs=[pl.BlockSpec((N_CHIPS, tr, cc), lambda i, c_ref: (0, i, 0))],
            out_specs=pl.BlockSpec((None, tr, cc), lambda i, c_ref: (c_ref[0], i, 0))),
        compiler_params=pltpu.CompilerParams(dimension_semantics=("parallel",), vmem_limit_bytes=_vmem(12 * tr * cc * 4)),
    )(c_arr, parts)


def _adamw_math(w, g, m, v):
    m2 = ADAM_B1 * m + (1.0 - ADAM_B1) * g
    v2 = ADAM_B2 * v + (1.0 - ADAM_B2) * jnp.square(g)
    m_hat = m2 / (1.0 - ADAM_B1 ** ADAM_STEP)
    v_hat = v2 / (1.0 - ADAM_B2 ** ADAM_STEP)
    delta = -ADAM_LR * (m_hat / (jnp.sqrt(v_hat) + ADAM_EPS) + ADAM_WD * w)
    return delta, m2, v2


def _adamw_mat(g2, w, m, v, layer, kind, prev, name):
    shape = w.shape
    ks, ns = shape[-2:]
    _, r, cc = g2.shape
    tr, tc = _fit(256, r), _fit(1024, cc)
    assert (r, cc) == ((ks // 2, ns) if kind == "col" else (ks, ns // 2))
    assert r % tr == 0 and cc % tc == 0
    rb, cb = r // tr, cc // tc
    if kind == "col":
        g_spec = pl.BlockSpec((None, tr, tc), lambda i, j: (i // rb, i % rb, j))
    else:
        g_spec = pl.BlockSpec((None, tr, tc), lambda i, j: (j // cb, i, j % cb))
    if w.ndim == 3:
        w_spec = pl.BlockSpec((None, tr, tc), lambda i, j: (layer, i, j))
    else:
        w_spec = pl.BlockSpec((tr, tc), lambda i, j: (i, j))
    n_prev = 0 if prev is None else 4

    def body(*refs):
        g_ref, w_ref, m_ref, v_ref = refs[:4]
        go_ref, d_ref, mo_ref, vo_ref = refs[4 + n_prev:]
        g = g_ref[...]
        delta, m2, v2 = _adamw_math(w_ref[...], g, m_ref[...], v_ref[...])
        go_ref[...] = g
        d_ref[...] = delta
        mo_ref[...] = m2
        vo_ref[...] = v2

    return pl.pallas_call(
        body, name=name, grid=(ks // tr, ns // tc),
        in_specs=[g_spec, w_spec, w_spec, w_spec] + [ANY] * n_prev, out_specs=[w_spec] * 4,
        out_shape=[jax.ShapeDtypeStruct(shape, F32)] * 4,
        input_output_aliases={4 + i: i for i in range(n_prev)},
        compiler_params=pltpu.CompilerParams(dimension_semantics=("parallel", "parallel"),
                                             vmem_limit_bytes=_vmem(16 * tr * tc * 4)),
    )(g2, w, m, v, *(prev or ()))


def _adamw_small(g, w, m, v, name):
    def body(g_ref, w_ref, m_ref, v_ref, d_ref, mo_ref, vo_ref):
        delta, m2, v2 = _adamw_math(w_ref[...], g_ref[...], m_ref[...], v_ref[...])
        d_ref[...] = delta
        mo_ref[...] = m2
        vo_ref[...] = v2

    return pl.pallas_call(body, name=name, out_shape=[jax.ShapeDtypeStruct(w.shape, F32)] * 3)(g, w, m, v)


def _sum8(parts, name):
    def body(p_ref, o_ref):
        acc = p_ref[0]
        for j in range(1, 8):
            acc = acc + p_ref[j]
        o_ref[...] = acc

    return pl.pallas_call(body, name=name, out_shape=jax.ShapeDtypeStruct(parts.shape[1:], F32),
                          compiler_params=pltpu.CompilerParams(vmem_limit_bytes=_vmem(12 * _nbytes(parts.shape[1:], F32))))(parts)


_REDUCE_AT = {
    "d_ple_gate1": (("A", "ple_proj1"),),
    "dh2_1": (("A", "ple_gate1"),),
    "d_mlp_down1": (("B", "ple_proj1"), ("B", "ple_gate1")),
    "dt1": (("A", "mlp_down1"),),
    "d_mlp_up1": (("B", "mlp_down1"), ("C", "ple_proj1"), ("C", "ple_gate1")),
    "dh1_1": (("A", "mlp_up1"),),
    "d_attn_w_o": (("C", "mlp_down1"),),
    "attn_do": (("A", "attn_w_o"),),
    "attn_bwd_g0": (("B", "attn_w_o"),),
    "attn_unrot": (("C", "attn_w_o"),),
    "dx1_q": (("A", "attn_w_q"),),
    "dkvn_k": (("A", "w_kv"),),
    "d_ple_gate0": (("A", "ple_proj0"),),
    "dh2_0": (("A", "ple_gate0"),),
    "d_mlp_down0": (("B", "mlp_up1"), ("B", "ple_proj0")),
    "dt0": (("B", "w_kv"), ("B", "ple_gate0"), ("A", "mlp_down0")),
    "d_mlp_up0": (("B", "mlp_down0"), ("C", "mlp_up1"), ("C", "ple_proj0"), ("C", "w_kv"), ("C", "ple_gate0")),
    "dh1_0": (("A", "mlp_up0"), ("B", "attn_w_q")),
    "d_conv_w_out": (("C", "mlp_down0"), ("C", "attn_w_q")),
    "conv_ds": (("A", "conv_w_out"),),
    "conv_bwd": (("B", "mlp_up0"), ("B", "conv_w_out")),
    "d_conv_w_in_g": (("C", "mlp_up0"), ("C", "conv_w_out")),
    "dx_a": (("A", "conv_w_in"),),
    "dx_g": (("B", "conv_w_in"),),
    "share_last": (("C", "conv_w_in"),),
}


class _Reducer:
    def __init__(self, w, mom, var, c_arr):
        self.w, self.mom, self.var, self.c_arr = w, mom, var, c_arr
        self.mats = {m[0]: m for m in _MATS}
        self.grads, self.pair_sums, self.chip_sums, self.out = {}, {}, {}, {}

    def produced(self, name, grad):
        self.grads[name] = grad

    def carry(self, call):
        parts = []
        for cls, stage, src in ((_PairSend, "A", self.grads), (_ChipScatter, "B", self.pair_sums),
                                (_PairShare, "C", self.chip_sums)):
            names = [n for s, n in _REDUCE_AT.get(call, ()) if s == stage]
            if names:
                parts.append((stage, cls({n: src[n] for n in names})))
        self._parts = parts
        return _Multi([p for _, p in parts]) if parts else None

    def carried(self):
        for stage, part in self._parts:
            for name, val in part.result.items():
                if stage == "A":
                    self.pair_sums[name] = _pair_sum(self.grads[name], val, self.c_arr, f"pair_sum_{name}")
                elif stage == "B":
                    self.chip_sums[name] = _chip_sum(val, self.c_arr, f"chip_sum_{name}")
                else:
                    _, src, layer, kind, _ = self.mats[name]
                    self.out[src] = _adamw_mat(val, self.w[src], self.mom[src], self.var[src], layer or 0, kind,
                                               self.out.get(src), f"adamw_{name}")
        self._parts = []


_WEIGHTS = ("conv_w_in", "conv_b_in", "conv_dw", "conv_dw_b", "conv_ln_g", "conv_ln_b", "conv_w_out", "kv_ln_g",
            "kv_ln_b", "w_kv", "attn_w_q", "attn_w_o", "ln1_g", "ln1_b", "mlp_up", "mlp_down", "ln2_g", "ln2_b",
            "ple_proj", "ple_gate")
_SHARDED_VECS = ("conv_b_in", "conv_dw", "conv_dw_b", "conv_ln_g", "conv_ln_b")
_REPLICATED_VECS = ("kv_ln_g", "kv_ln_b", "ln1_g", "ln1_b", "ln2_g", "ln2_b")


def _pad_rows(a, rows):
    return jnp.concatenate([a, jnp.zeros((rows - a.shape[0], a.shape[1]), a.dtype)], axis=0) if a.shape[0] < rows else a


def _pack_sharded(d):
    n = d["conv_dw_b"].shape[-1]
    rows = [d["conv_b_in"].reshape(2, n), d["conv_dw"].reshape(CONV_WIDTH, n), d["conv_dw_b"].reshape(1, n),
            d["conv_ln_g"].reshape(1, n), d["conv_ln_b"].reshape(1, n)]
    return _pad_rows(jnp.concatenate(rows, axis=0), 40)


def _unpack_sharded(pack, like):
    n = pack.shape[1]
    return {"conv_b_in": pack[0:2].reshape(like["conv_b_in"].shape),
            "conv_dw": pack[2:2 + CONV_WIDTH].reshape(like["conv_dw"].shape),
            "conv_dw_b": pack[33:34].reshape(like["conv_dw_b"].shape),
            "conv_ln_g": pack[34:35].reshape(like["conv_ln_g"].shape),
            "conv_ln_b": pack[35:36].reshape(like["conv_ln_b"].shape)}


def _pack_replicated(d):
    D = d["kv_ln_g"].shape[-1]
    rows = [d[n].reshape(-1, D) for n in _REPLICATED_VECS]
    return _pad_rows(jnp.concatenate(rows, axis=0), 16)


def _unpack_replicated(pack, like):
    out, r = {}, 0
    for n in _REPLICATED_VECS:
        k = like[n].size // pack.shape[1]
        out[n] = pack[r:r + k].reshape(like[n].shape)
        r += k
    return out


def kernel(x, p, positions, conv_w_in, conv_b_in, conv_dw, conv_dw_b, conv_ln_g, conv_ln_b, conv_w_out, kv_ln_g, kv_ln_b, w_kv, attn_w_q, attn_w_o, ln1_g, ln1_b, mlp_up, mlp_down, ln2_g, ln2_b, ple_proj, ple_gate, loss_target, m_conv_w_in, m_conv_b_in, m_conv_dw, m_conv_dw_b, m_conv_ln_g, m_conv_ln_b, m_conv_w_out, m_kv_ln_g, m_kv_ln_b, m_w_kv, m_attn_w_q, m_attn_w_o, m_ln1_g, m_ln1_b, m_mlp_up, m_mlp_down, m_ln2_g, m_ln2_b, m_ple_proj, m_ple_gate, v_conv_w_in, v_conv_b_in, v_conv_dw, v_conv_dw_b, v_conv_ln_g, v_conv_ln_b, v_conv_w_out, v_kv_ln_g, v_kv_ln_b, v_w_kv, v_attn_w_q, v_attn_w_o, v_ln1_g, v_ln1_b, v_mlp_up, v_mlp_down, v_ln2_g, v_ln2_b, v_ple_proj, v_ple_gate):
    args = dict(locals())
    w = {n: args[n] for n in _WEIGHTS}
    mom = {n: args["m_" + n] for n in _WEIGHTS}
    var = {n: args["v_" + n] for n in _WEIGHTS}
    S, D = x.shape[1:]
    n4 = D // N_CHIPS
    chip = 2 * lax.axis_index("x") + lax.axis_index("y")
    c_arr = lax.axis_index("c").astype(jnp.int32).reshape(1)

    shards = {n: w[n].astype(BF16) for n in sorted({m[1] for m in _MATS})}
    vec_all = _allgather8(_pack_sharded(w), "gather_vectors")
    vec_full = jnp.concatenate([vec_all[2 * j] for j in range(N_CHIPS)], axis=1)
    b_in = vec_all[0::2, 0:2, :].reshape(1, 2 * D)
    V = {"conv_b_a": b_in[:, :D], "conv_b_g": b_in[:, D:],
         "conv_dw": _pad_rows(vec_full[2:2 + CONV_WIDTH], CONV_PAD), "conv_dw_b": vec_full[33:34],
         "conv_ln_g": vec_full[34:35], "conv_ln_b": vec_full[35:36],
         "kv_ln_g": kv_ln_g.reshape(1, D), "kv_ln_b": kv_ln_b.reshape(1, D)}
    for l in range(2):
        for n in ("ln1_g", "ln1_b", "ln2_g", "ln2_b"):
            V[f"{n}{l}"] = w[n][l].reshape(1, D)

    half = HEAD_DIM // 2
    inv_freq = ROPE_THETA ** (-jnp.arange(half, dtype=F32) * (2.0 / HEAD_DIM))
    ang = positions[0].astype(F32)[:, None] * inv_freq
    cos, sin = jnp.cos(ang), jnp.sin(ang)
    cosf = jnp.concatenate([cos, cos], axis=-1)
    sinf = jnp.concatenate([-sin, sin], axis=-1)

    reducer = _Reducer(w, mom, var, c_arr)
    loss_cols, grad_x, _, gv = _local_step(x[0], p[:, 0], cosf, sinf, loss_target[0], None, V, shards, reducer,
                                           chip.astype(jnp.int32).reshape(1))
    loss = lax.psum(jnp.sum(loss_cols), ("x", "y", "c"))
    out = dict(reducer.out)

    gpack = jnp.concatenate([gv["conv_b_a"], gv["conv_b_g"], gv["conv_dw"][:CONV_WIDTH], gv["conv_dw_b"],
                             gv["conv_ln_g"], gv["conv_ln_b"], gv["kv_ln_g"], gv["kv_ln_b"],
                             gv["ln1_g0"], gv["ln1_g1"], gv["ln1_b0"], gv["ln1_b1"],
                             gv["ln2_g0"], gv["ln2_g1"], gv["ln2_b0"], gv["ln2_b1"]], axis=0)
    gsum = _sum8(_allgather8(_pad_rows(gpack, 48), "gather_vector_grads"), "sum_vector_grads")
    g_b = lax.dynamic_slice_in_dim(jnp.concatenate([gsum[0:1], gsum[1:2]], axis=1), chip * 2 * n4, 2 * n4, axis=1)
    g_sh = lax.dynamic_slice_in_dim(gsum[2:36], chip * n4, n4, axis=1)
    g_sh = _pad_rows(jnp.concatenate([g_b.reshape(2, n4), g_sh], axis=0), 40)
    d_sh, m_sh, v_sh = _adamw_small(g_sh, _pack_sharded(w), _pack_sharded(mom), _pack_sharded(var), "adamw_sharded_vectors")
    g_rep = _pad_rows(gsum[36:46], 16)
    d_rep, m_rep, v_rep = _adamw_small(g_rep, _pack_replicated(w), _pack_replicated(mom), _pack_replicated(var),
                                       "adamw_replicated_vectors")
    small = {}
    for i, (sh, rep) in enumerate(((g_sh, g_rep), (d_sh, d_rep), (m_sh, m_rep), (v_sh, v_rep))):
        d = {**_unpack_sharded(sh, w), **_unpack_replicated(rep, w)}
        for n, val in d.items():
            small.setdefault(n, [None] * 4)[i] = val
    for n in small:
        out[n] = small[n]

    res = [loss, grad_x[None]]
    for i in range(4):
        res += [out[n][i] for n in _WEIGHTS]
    return tuple(res)
```
